```python
import jax, jax.numpy as jnp
from jax import lax
import numpy as np

D_MODEL = 1024
BATCH = 8
SEQ = 8192
DEPTH = 2

MIX_WIDTH = D_MODEL
WIDTH_A = MIX_WIDTH // 2
WIDTH_B = MIX_WIDTH - WIDTH_A
HEADS_A = 4
HEAD_DIM_A = WIDTH_A // HEADS_A
GROUPS_B = 4
CHUNK = 128
CONV_K = 3
PLE_DIM = 256
EPS = 1e-6
SPLITS = [WIDTH_A, WIDTH_A, WIDTH_A, WIDTH_B, WIDTH_B, WIDTH_B, WIDTH_B]
PROJ_WIDTH = sum(SPLITS)

kernel_name = "hybrid_sgu_shortconv_ple_trunk"


def rmsnorm(x, g):
    xf = x.astype(jnp.float32)
    y = xf * lax.rsqrt(jnp.mean(xf * xf, axis=-1, keepdims=True) + EPS)
    return (y * g.astype(jnp.float32)).astype(x.dtype)


def layernorm(x, g, b):
    xf = x.astype(jnp.float32)
    mu = jnp.mean(xf, axis=-1, keepdims=True)
    xc = xf - mu
    var = jnp.mean(xc * xc, axis=-1, keepdims=True)
    y = xc * lax.rsqrt(var + EPS)
    return (y * g.astype(jnp.float32) + b.astype(jnp.float32)).astype(x.dtype)


def spatial_gating(u, v, ln_g, ln_b, w_s, b_s):
    bsz, s_len, _ = v.shape
    n_chunks = s_len // CHUNK
    v = layernorm(v, ln_g, ln_b)
    vh = v.reshape(bsz, n_chunks, CHUNK, HEADS_A, HEAD_DIM_A)
    mask = jnp.tril(jnp.ones((CHUNK, CHUNK), dtype=w_s.dtype))
    ws = w_s * mask[None]
    mixed = jnp.einsum('hts,bnshd->bnthd', ws, vh)
    mixed = mixed + jnp.transpose(b_s)[None, None, :, :, None]
    return u * mixed.reshape(bsz, s_len, WIDTH_A)


def gated_short_conv(h, gate_b, gate_c, conv_w):
    s_len = h.shape[1]
    xc = gate_c * h
    xp = jnp.pad(xc, ((0, 0), (CONV_K - 1, 0), (0, 0)))
    y = xp[:, 0:s_len] * conv_w[:, 0]
    for k in range(1, CONV_K):
        y = y + xp[:, k:k + s_len] * conv_w[:, k]
    return gate_b * y


def _fwd_setup_inputs(seed: int = 0) -> dict:
    key = jax.random.key(seed)
    ks = jax.random.split(key, 16)
    f32 = jnp.float32
    x = jax.random.normal(ks[0], (BATCH, SEQ, D_MODEL), f32)
    p = jax.random.normal(ks[1], (DEPTH, BATCH, SEQ, PLE_DIM), f32)
    norm_g = 1.0 + 0.02 * jax.random.normal(ks[2], (DEPTH, D_MODEL), f32)
    w_in = jax.random.normal(ks[3], (DEPTH, D_MODEL, PROJ_WIDTH), f32) * D_MODEL ** -0.5
    ln_v_g = 1.0 + 0.02 * jax.random.normal(ks[4], (DEPTH, WIDTH_A), f32)
    ln_v_b = 0.02 * jax.random.normal(ks[5], (DEPTH, WIDTH_A), f32)
    w_s = 0.5 * jax.random.normal(ks[6], (DEPTH, HEADS_A, CHUNK, CHUNK), f32) * CHUNK ** -0.5
    b_s = 1.0 + 0.1 * jax.random.normal(ks[7], (DEPTH, HEADS_A, CHUNK), f32)
    conv_w = jax.random.normal(ks[8], (DEPTH, WIDTH_B, CONV_K), f32) * CONV_K ** -0.5
    w_out = jax.random.normal(ks[9], (DEPTH, MIX_WIDTH, D_MODEL), f32) * MIX_WIDTH ** -0.5
    ple_norm_g = 1.0 + 0.02 * jax.random.normal(ks[10], (DEPTH, D_MODEL), f32)
    w_ple_gate = jax.random.normal(ks[11], (DEPTH, D_MODEL, D_MODEL), f32) * D_MODEL ** -0.5
    w_ple_proj = 0.5 * jax.random.normal(ks[12], (DEPTH, PLE_DIM, D_MODEL), f32) * PLE_DIM ** -0.5
    final_g = 1.0 + 0.02 * jax.random.normal(ks[13], (D_MODEL,), f32)
    return {"x": x, "p": p, "norm_g": norm_g, "w_in": w_in, "ln_v_g": ln_v_g,
            "ln_v_b": ln_v_b, "w_s": w_s, "b_s": b_s, "conv_w": conv_w,
            "w_out": w_out, "ple_norm_g": ple_norm_g, "w_ple_gate": w_ple_gate,
            "w_ple_proj": w_ple_proj, "final_g": final_g}


def _fwd_reference(x, p, norm_g, w_in, ln_v_g, ln_v_b, w_s, b_s, conv_w, w_out,
              ple_norm_g, w_ple_gate, w_ple_proj, final_g):
    split_idx = list(np.cumsum(SPLITS)[:-1])
    for i in range(DEPTH):
        hn = rmsnorm(x, norm_g[i])
        proj = hn @ w_in[i]
        u_a, v_a, z_a, h_b, gb, gc, z_b = jnp.split(proj, split_idx, axis=-1)
        out_a = spatial_gating(u_a, v_a, ln_v_g[i], ln_v_b[i], w_s[i], b_s[i]) * jax.nn.silu(z_a)
        out_b = gated_short_conv(h_b, gb, gc, conv_w[i]) * jax.nn.silu(z_b)
        x = x + jnp.concatenate([out_a, out_b], axis=-1) @ w_out[i]
        gate = jax.nn.sigmoid(rmsnorm(x, ple_norm_g[i]) @ w_ple_gate[i])
        x = x + gate * (p[i] @ w_ple_proj[i])
    return rmsnorm(x, final_g)


import jax as _jax
import jax.numpy as _jnp

TWIN_FORMAT = 'train_step'
FWD_PARAMS = ['x', 'p', 'norm_g', 'w_in', 'ln_v_g', 'ln_v_b', 'w_s', 'b_s', 'conv_w', 'w_out', 'ple_norm_g', 'w_ple_gate', 'w_ple_proj', 'final_g']
TWIN_WEIGHTS = ['norm_g', 'w_in', 'ln_v_g', 'ln_v_b', 'w_s', 'b_s', 'conv_w', 'w_out', 'ple_norm_g', 'w_ple_gate', 'w_ple_proj', 'final_g']
TWIN_DIFF_INPUT = 'x'
TWIN_INPUTS = ['x', 'p', 'norm_g', 'w_in', 'ln_v_g', 'ln_v_b', 'w_s', 'b_s', 'conv_w', 'w_out', 'ple_norm_g', 'w_ple_gate', 'w_ple_proj', 'final_g', 'loss_target', 'm_norm_g', 'm_w_in', 'm_ln_v_g', 'm_ln_v_b', 'm_w_s', 'm_b_s', 'm_conv_w', 'm_w_out', 'm_ple_norm_g', 'm_w_ple_gate', 'm_w_ple_proj', 'm_final_g', 'v_norm_g', 'v_w_in', 'v_ln_v_g', 'v_ln_v_b', 'v_w_s', 'v_b_s', 'v_conv_w', 'v_w_out', 'v_ple_norm_g', 'v_w_ple_gate', 'v_w_ple_proj', 'v_final_g']
TWIN_OUTPUTS = ['loss', 'grad_x', 'grad_norm_g', 'grad_w_in', 'grad_ln_v_g', 'grad_ln_v_b', 'grad_w_s', 'grad_b_s', 'grad_conv_w', 'grad_w_out', 'grad_ple_norm_g', 'grad_w_ple_gate', 'grad_w_ple_proj', 'grad_final_g', 'delta_norm_g', 'delta_w_in', 'delta_ln_v_g', 'delta_ln_v_b', 'delta_w_s', 'delta_b_s', 'delta_conv_w', 'delta_w_out', 'delta_ple_norm_g', 'delta_w_ple_gate', 'delta_w_ple_proj', 'delta_final_g', 'new_m_norm_g', 'new_m_w_in', 'new_m_ln_v_g', 'new_m_ln_v_b', 'new_m_w_s', 'new_m_b_s', 'new_m_conv_w', 'new_m_w_out', 'new_m_ple_norm_g', 'new_m_w_ple_gate', 'new_m_w_ple_proj', 'new_m_final_g', 'new_v_norm_g', 'new_v_w_in', 'new_v_ln_v_g', 'new_v_ln_v_b', 'new_v_w_s', 'new_v_b_s', 'new_v_conv_w', 'new_v_w_out', 'new_v_ple_norm_g', 'new_v_w_ple_gate', 'new_v_w_ple_proj', 'new_v_final_g']
TWIN_LEAF_KINDS = {'loss': 'loss', 'grad_x': 'grad_x', 'grad_norm_g': 'grad_w', 'grad_w_in': 'grad_w', 'grad_ln_v_g': 'grad_w', 'grad_ln_v_b': 'grad_w', 'grad_w_s': 'grad_w', 'grad_b_s': 'grad_w', 'grad_conv_w': 'grad_w', 'grad_w_out': 'grad_w', 'grad_ple_norm_g': 'grad_w', 'grad_w_ple_gate': 'grad_w', 'grad_w_ple_proj': 'grad_w', 'grad_final_g': 'grad_w', 'delta_norm_g': 'delta_w', 'delta_w_in': 'delta_w', 'delta_ln_v_g': 'delta_w', 'delta_ln_v_b': 'delta_w', 'delta_w_s': 'delta_w', 'delta_b_s': 'delta_w', 'delta_conv_w': 'delta_w', 'delta_w_out': 'delta_w', 'delta_ple_norm_g': 'delta_w', 'delta_w_ple_gate': 'delta_w', 'delta_w_ple_proj': 'delta_w', 'delta_final_g': 'delta_w', 'new_m_norm_g': 'new_m', 'new_m_w_in': 'new_m', 'new_m_ln_v_g': 'new_m', 'new_m_ln_v_b': 'new_m', 'new_m_w_s': 'new_m', 'new_m_b_s': 'new_m', 'new_m_conv_w': 'new_m', 'new_m_w_out': 'new_m', 'new_m_ple_norm_g': 'new_m', 'new_m_w_ple_gate': 'new_m', 'new_m_w_ple_proj': 'new_m', 'new_m_final_g': 'new_m', 'new_v_norm_g': 'new_v', 'new_v_w_in': 'new_v', 'new_v_ln_v_g': 'new_v', 'new_v_ln_v_b': 'new_v', 'new_v_w_s': 'new_v', 'new_v_b_s': 'new_v', 'new_v_conv_w': 'new_v', 'new_v_w_out': 'new_v', 'new_v_ple_norm_g': 'new_v', 'new_v_w_ple_gate': 'new_v', 'new_v_w_ple_proj': 'new_v', 'new_v_final_g': 'new_v'}


def _forward(args):
    return _fwd_reference(*[args[k] for k in FWD_PARAMS])


def _output_shape():
    out = _jax.eval_shape(lambda: _forward(_fwd_setup_inputs(0)))
    return out.shape, out.dtype

N_MICROBATCH = 1
ADAM_LR = 0.001
ADAM_B1 = 0.9
ADAM_B2 = 0.999
ADAM_EPS = 1e-08
ADAM_WD = 0.01
ADAM_STEP = 10
PER_EXAMPLE_BATCH_AXIS = {'x': 0, 'p': 1, 'loss_target': 0}
SHARED_INPUTS = []
_WEIGHT_DTYPES = {'norm_g': _jnp.float32, 'w_in': _jnp.float32, 'ln_v_g': _jnp.float32, 'ln_v_b': _jnp.float32, 'w_s': _jnp.float32, 'b_s': _jnp.float32, 'conv_w': _jnp.float32, 'w_out': _jnp.float32, 'ple_norm_g': _jnp.float32, 'w_ple_gate': _jnp.float32, 'w_ple_proj': _jnp.float32, 'final_g': _jnp.float32}
MOMENT_SCALE = {'norm_g': 2.392153e-01, 'w_in': 1.232597e-01, 'ln_v_g': 4.604306e-02, 'ln_v_b': 4.851639e-02, 'w_s': 9.177303e-02, 'b_s': 1.300649e-01, 'conv_w': 1.398117e-01, 'w_out': 1.315421e-01, 'ple_norm_g': 2.476696e-02, 'w_ple_gate': 2.287830e-02, 'w_ple_proj': 1.174788e-01, 'final_g': 6.403494e+01}


def _to_microbatches(a, axis):
    t = _jnp.moveaxis(a, axis, 0)
    t = t.reshape((N_MICROBATCH, t.shape[0] // N_MICROBATCH) + t.shape[1:])
    return _jnp.moveaxis(t, 1, axis + 1)


def setup_inputs(seed: int = 0) -> dict:
    inp = _fwd_setup_inputs(seed)
    key = _jax.random.fold_in(_jax.random.key(seed), 7919)
    shape, _ = _output_shape()
    out = dict(inp)
    out["loss_target"] = _jax.random.normal(_jax.random.fold_in(key, 0), shape, _jnp.float32)
    for i, name in enumerate(TWIN_WEIGHTS):
        w = inp[name].astype(_jnp.float32)
        if MOMENT_SCALE is None:
            s = _jnp.sqrt(_jnp.mean(_jnp.square(w)) + 1e-30)
        else:
            s = MOMENT_SCALE[name]
        km, kv = _jax.random.split(_jax.random.fold_in(key, i + 1))
        out[name] = w
        out["m_" + name] = s * _jax.random.normal(km, w.shape, _jnp.float32)
        out["v_" + name] = (s * s) * _jax.random.uniform(kv, w.shape, _jnp.float32, 0.5, 1.5)
    if N_MICROBATCH > 1:
        for name, axis in PER_EXAMPLE_BATCH_AXIS.items():
            out[name] = _to_microbatches(out[name], axis)
    return {'x': out['x'], 'p': out['p'], 'norm_g': out['norm_g'], 'w_in': out['w_in'], 'ln_v_g': out['ln_v_g'], 'ln_v_b': out['ln_v_b'], 'w_s': out['w_s'], 'b_s': out['b_s'], 'conv_w': out['conv_w'], 'w_out': out['w_out'], 'ple_norm_g': out['ple_norm_g'], 'w_ple_gate': out['w_ple_gate'], 'w_ple_proj': out['w_ple_proj'], 'final_g': out['final_g'], 'loss_target': out['loss_target'], 'm_norm_g': out['m_norm_g'], 'm_w_in': out['m_w_in'], 'm_ln_v_g': out['m_ln_v_g'], 'm_ln_v_b': out['m_ln_v_b'], 'm_w_s': out['m_w_s'], 'm_b_s': out['m_b_s'], 'm_conv_w': out['m_conv_w'], 'm_w_out': out['m_w_out'], 'm_ple_norm_g': out['m_ple_norm_g'], 'm_w_ple_gate': out['m_w_ple_gate'], 'm_w_ple_proj': out['m_w_ple_proj'], 'm_final_g': out['m_final_g'], 'v_norm_g': out['v_norm_g'], 'v_w_in': out['v_w_in'], 'v_ln_v_g': out['v_ln_v_g'], 'v_ln_v_b': out['v_ln_v_b'], 'v_w_s': out['v_w_s'], 'v_b_s': out['v_b_s'], 'v_conv_w': out['v_conv_w'], 'v_w_out': out['v_w_out'], 'v_ple_norm_g': out['v_ple_norm_g'], 'v_w_ple_gate': out['v_w_ple_gate'], 'v_w_ple_proj': out['v_w_ple_proj'], 'v_final_g': out['v_final_g']}


def _loss(weights, diff, rest, loss_target):
    with _jax.named_scope("forward"):
        args = {**rest, TWIN_DIFF_INPUT: diff, **{k: w.astype(_WEIGHT_DTYPES[k]) for k, w in weights.items()}}
        y = _forward(args)
    with _jax.named_scope("loss_head"):
        err = _jnp.square(y.astype(_jnp.float32) - loss_target)
        return 0.5 * _jnp.sum(_jnp.mean(err, axis=-1)) if err.ndim else 0.5 * err


def _adamw(w, g, m, v):
    m = ADAM_B1 * m + (1.0 - ADAM_B1) * g
    v = ADAM_B2 * v + (1.0 - ADAM_B2) * _jnp.square(g)
    m_hat = m / (1.0 - ADAM_B1 ** ADAM_STEP)
    v_hat = v / (1.0 - ADAM_B2 ** ADAM_STEP)
    delta = -ADAM_LR * (m_hat / (_jnp.sqrt(v_hat) + ADAM_EPS) + ADAM_WD * w)
    return delta, m, v


def reference(x, p, norm_g, w_in, ln_v_g, ln_v_b, w_s, b_s, conv_w, w_out, ple_norm_g, w_ple_gate, w_ple_proj, final_g, loss_target, m_norm_g, m_w_in, m_ln_v_g, m_ln_v_b, m_w_s, m_b_s, m_conv_w, m_w_out, m_ple_norm_g, m_w_ple_gate, m_w_ple_proj, m_final_g, v_norm_g, v_w_in, v_ln_v_g, v_ln_v_b, v_w_s, v_b_s, v_conv_w, v_w_out, v_ple_norm_g, v_w_ple_gate, v_w_ple_proj, v_final_g):
    given = dict(x=x, p=p, norm_g=norm_g, w_in=w_in, ln_v_g=ln_v_g, ln_v_b=ln_v_b, w_s=w_s, b_s=b_s, conv_w=conv_w, w_out=w_out, ple_norm_g=ple_norm_g, w_ple_gate=w_ple_gate, w_ple_proj=w_ple_proj, final_g=final_g, loss_target=loss_target, m_norm_g=m_norm_g, m_w_in=m_w_in, m_ln_v_g=m_ln_v_g, m_ln_v_b=m_ln_v_b, m_w_s=m_w_s, m_b_s=m_b_s, m_conv_w=m_conv_w, m_w_out=m_w_out, m_ple_norm_g=m_ple_norm_g, m_w_ple_gate=m_w_ple_gate, m_w_ple_proj=m_w_ple_proj, m_final_g=m_final_g, v_norm_g=v_norm_g, v_w_in=v_w_in, v_ln_v_g=v_ln_v_g, v_ln_v_b=v_ln_v_b, v_w_s=v_w_s, v_b_s=v_b_s, v_conv_w=v_conv_w, v_w_out=v_w_out, v_ple_norm_g=v_ple_norm_g, v_w_ple_gate=v_w_ple_gate, v_w_ple_proj=v_w_ple_proj, v_final_g=v_final_g)
    weights = {n: given[n] for n in TWIN_WEIGHTS}
    shared = {n: given[n] for n in SHARED_INPUTS}
    per_example = {n: given[n] for n in ['x', 'p']}
    grad_fn = _jax.value_and_grad(_loss, argnums=(0, 1))

    def one_microbatch(ex, loss_target):
        ex = dict(ex)
        diff = ex.pop(TWIN_DIFF_INPUT)
        return grad_fn(weights, diff, {**shared, **ex}, loss_target)

    if N_MICROBATCH == 1:
        loss, (grad_w, grad_x) = one_microbatch(per_example, given["loss_target"])
    else:
        def body(carry, xs):
            loss_sum, grad_sum = carry
            l_k, (gw_k, gx_k) = one_microbatch(xs[0], xs[1])
            with _jax.named_scope("update"):
                return (loss_sum + l_k, _jax.tree.map(_jnp.add, grad_sum, gw_k)), gx_k

        init = (_jnp.zeros((), _jnp.float32), _jax.tree.map(_jnp.zeros_like, weights))
        (loss, grad_w), grad_x = _jax.lax.scan(body, init, (per_example, given["loss_target"]))
    with _jax.named_scope("update"):
        delta_w, new_m, new_v = {}, {}, {}
        for n in TWIN_WEIGHTS:
            delta_w[n], new_m[n], new_v[n] = _adamw(weights[n], grad_w[n], given["m_" + n], given["v_" + n])
    return (loss, grad_x, *[grad_w[n] for n in TWIN_WEIGHTS], *[delta_w[n] for n in TWIN_WEIGHTS],
            *[new_m[n] for n in TWIN_WEIGHTS], *[new_v[n] for n in TWIN_WEIGHTS])
```

```python
import jax
import jax.numpy as jnp
from jax import lax
from jax.experimental import pallas as pl
from jax.experimental.pallas import tpu as pltpu

F32 = jnp.float32
BF16 = jnp.bfloat16

D_MODEL = 1024
WIDTH_A = 512
WIDTH_B = 512
HEADS_A = 4
HEAD_DIM = 128
CHUNK = 128
PLE_DIM = 256
PROJ_WIDTH = 3584
DEPTH = 2
EPS = 1e-6
N_DEV = 8

ADAM_LR = 0.001
ADAM_B1 = 0.9
ADAM_B2 = 0.999
ADAM_EPS = 1e-08
ADAM_WD = 0.01
ADAM_STEP = 10

ROWS_IN = PROJ_WIDTH // N_DEV
ROWS_OUT = D_MODEL // N_DEV
ROWS_GATE = D_MODEL // N_DEV
ROWS_PROJ = (D_MODEL // N_DEV) * PLE_DIM // D_MODEL
ROWS_CONV = 16
OFF_IN = 0
OFF_OUT = OFF_IN + ROWS_IN
OFF_GATE = OFF_OUT + ROWS_OUT
OFF_PROJ = OFF_GATE + ROWS_GATE
OFF_CONV = OFF_PROJ + ROWS_PROJ
ROWS_GRAD = OFF_CONV
ROWS_LAYER = OFF_CONV + ROWS_CONV
ROWS_SHARD = DEPTH * ROWS_LAYER

MIB = 1024 * 1024
MESH = pl.DeviceIdType.MESH

NT_DIMS = (((1,), (1,)), ((), ()))
TN_DIMS = (((0,), (0,)), ((), ()))


def _dot(a, b):
    return jnp.dot(a, b, preferred_element_type=F32)


def _dot_nt(a, b):
    return lax.dot_general(a, b, NT_DIMS, preferred_element_type=F32)


def _dot_tn(a, b):
    return lax.dot_general(a, b, TN_DIMS, preferred_element_type=F32)


def _colsum8(a):
    rows, n = a.shape
    return jnp.sum(a.reshape(rows // 8, 8, n), axis=0)


def _sigmoid(z):
    return 1.0 / (1.0 + jnp.exp(-z))


def _tile(t, want):
    return want if t % want == 0 else t


def _all_gather_rows(shard):
    m_per, n = shard.shape

    def body(x_ref, out_ref, send_sems, recv_sems, local_sem):
        x, y, c = lax.axis_index("x"), lax.axis_index("y"), lax.axis_index("c")
        me, sibling = (x, y, c), (x, y, 1 - c)
        chips = [(1 - x, y), (x, 1 - y), (1 - x, 1 - y)]

        def rows(px, py, pc):
            return out_ref.at[pl.ds((4 * px + 2 * py + pc) * m_per, m_per), :]

        def copy(k, block, to, src=None):
            return pltpu.make_async_remote_copy(
                src_ref=rows(*block) if src is None else src,
                dst_ref=rows(*block),
                send_sem=send_sems.at[k],
                recv_sem=recv_sems.at[k],
                device_id=to,
                device_id_type=MESH,
            )

        mine = pltpu.make_async_copy(x_ref, rows(*me), local_sem)
        mine.start()
        first = [copy(0, me, sibling, src=x_ref)]
        first += [copy(1 + j, me, (*chip, c), src=x_ref) for j, chip in enumerate(chips)]
        for cp in first:
            cp.start()
        passed = [copy(4 + j, (*chip, c), sibling) for j, chip in enumerate(chips)]
        for j, chip in enumerate(chips):
            copy(1 + j, (*chip, c), me).wait_recv()
            passed[j].start()
        copy(0, sibling, me).wait_recv()
        for j, chip in enumerate(chips):
            copy(4 + j, (*chip, 1 - c), me).wait_recv()
        for cp in first + passed:
            cp.wait_send()
        mine.wait()

    return pl.pallas_call(
        body,
        name="weights_all_gather",
        out_shape=jax.ShapeDtypeStruct((N_DEV * m_per, n), shard.dtype),
        in_specs=[pl.BlockSpec(memory_space=pltpu.VMEM)],
        out_specs=pl.BlockSpec(memory_space=pltpu.VMEM),
        scratch_shapes=[
            pltpu.SemaphoreType.DMA((7,)),
            pltpu.SemaphoreType.DMA((7,)),
            pltpu.SemaphoreType.DMA,
        ],
        compiler_params=pltpu.CompilerParams(vmem_limit_bytes=48 * MIB),
    )(shard)


def _weight_copies(wg_ref, layer, w_in_t, w_out, w_gate, sems):
    copies = []
    k = 0
    for s in range(N_DEV):
        base = s * ROWS_SHARD + layer * ROWS_LAYER
        for dst, off, rows in ((w_in_t, OFF_IN, ROWS_IN), (w_out, OFF_OUT, ROWS_OUT), (w_gate, OFF_GATE, ROWS_GATE)):
            copies.append(
                pltpu.make_async_copy(
                    wg_ref.at[pl.ds(base + off, rows), :], dst.at[pl.ds(s * rows, rows), :], sems.at[k]
                )
            )
            k += 1
    return copies


def _forward_layer(layer, x, p_l, wg, w_proj_t, conv_k, norm_g, ln_g, ln_b, w_mix, b_mix, ple_g):
    t = x.shape[0]
    tm = _tile(t, 256)
    nt = t // tm
    n_chunks = tm // CHUNK

    def body(x_ref, p_ref, wg_ref, wpt_ref, cw_ref, ng_ref, lng_ref, lnb_ref, wm_ref, bm_ref, pg_ref,
             proj_ref, hn_ref, cat_ref, r_ref, gpre_ref, x1_ref, x2_ref,
             w_in_t, w_out, w_gate, vln_s, mixed_s, halo_s, sems):
        i = pl.program_id(0)

        @pl.when(i == 0)
        def _():
            copies = _weight_copies(wg_ref, layer, w_in_t, w_out, w_gate, sems)
            for cp in copies:
                cp.start()
            halo_s[...] = jnp.zeros_like(halo_s)
            for cp in copies:
                cp.wait()

        xv = x_ref[...]
        rstd0 = lax.rsqrt(jnp.mean(xv * xv, axis=-1, keepdims=True) + EPS)
        hn = (xv * rstd0 * ng_ref[...]).astype(BF16)
        hn_ref[...] = hn
        proj = _dot_nt(hn, w_in_t[...])
        proj_ref[...] = proj.astype(BF16)

        u = proj[:, 0:512]
        v = proj[:, 512:1024]
        za = proj[:, 1024:1536]
        mu = jnp.mean(v, axis=-1, keepdims=True)
        vc = v - mu
        var = jnp.mean(vc * vc, axis=-1, keepdims=True)
        vln = vc * lax.rsqrt(var + EPS) * lng_ref[...] + lnb_ref[...]
        vln_s[...] = vln.astype(BF16)
        for ci in range(n_chunks):
            rows = pl.ds(ci * CHUNK, CHUNK)
            for h in range(HEADS_A):
                cols = pl.ds(h * HEAD_DIM, HEAD_DIM)
                mixed_s[rows, cols] = _dot(wm_ref[h], vln_s[rows, cols]) + bm_ref[h]
        out_a = u * mixed_s[...] * (za * _sigmoid(za))
        cat_ref[:, 0:512] = out_a.astype(BF16)

        hb = proj[:, 1536:2048]
        gb = proj[:, 2048:2560]
        gc = proj[:, 2560:3072]
        zb = proj[:, 3072:3584]
        xc = gc * hb
        prev = halo_s[...]
        row = lax.broadcasted_iota(jnp.int32, (tm, WIDTH_B), 0)
        xc_m1 = jnp.where(row == 0, prev[7:8, :], pltpu.roll(xc, 1, 0))
        xc_m2 = jnp.where(row == 0, prev[6:7, :], jnp.where(row == 1, prev[7:8, :], pltpu.roll(xc, 2, 0)))
        halo_s[...] = xc[tm - 8:tm, :]
        cw = cw_ref[...]
        yc = cw[0:1, :] * xc_m2 + cw[1:2, :] * xc_m1 + cw[2:3, :] * xc
        out_b = gb * yc * (zb * _sigmoid(zb))
        cat_ref[:, 512:1024] = out_b.astype(BF16)

        x1 = xv + _dot(cat_ref[...], w_out[...])
        x1_ref[...] = x1
        rstd1 = lax.rsqrt(jnp.mean(x1 * x1, axis=-1, keepdims=True) + EPS)
        r = (x1 * rstd1 * pg_ref[...]).astype(BF16)
        r_ref[...] = r
        gpre = _dot(r, w_gate[...])
        gpre_ref[...] = gpre.astype(BF16)
        pp = _dot_nt(p_ref[...].astype(BF16), wpt_ref[...])
        x2_ref[...] = x1 + _sigmoid(gpre) * pp

    def tok(width):
        return pl.BlockSpec((tm, width), lambda i: (i, 0))

    def whole(shape):
        return pl.BlockSpec(shape, lambda i: (0,) * len(shape))

    return pl.pallas_call(
        body,
        name=f"layer{layer}_forward",
        grid=(nt,),
        in_specs=[
            tok(D_MODEL), tok(PLE_DIM), pl.BlockSpec(memory_space=pl.ANY),
            whole((D_MODEL, PLE_DIM)), whole((8, WIDTH_B)), whole((1, D_MODEL)), whole((1, WIDTH_A)), whole((1, WIDTH_A)),
            whole((HEADS_A, CHUNK, CHUNK)), whole((HEADS_A, CHUNK, HEAD_DIM)), whole((1, D_MODEL)),
        ],
        out_specs=[tok(PROJ_WIDTH), tok(D_MODEL), tok(D_MODEL), tok(D_MODEL), tok(D_MODEL), tok(D_MODEL), tok(D_MODEL)],
        out_shape=[
            jax.ShapeDtypeStruct((t, PROJ_WIDTH), BF16),
            jax.ShapeDtypeStruct((t, D_MODEL), BF16),
            jax.ShapeDtypeStruct((t, D_MODEL), BF16),
            jax.ShapeDtypeStruct((t, D_MODEL), BF16),
            jax.ShapeDtypeStruct((t, D_MODEL), BF16),
            jax.ShapeDtypeStruct((t, D_MODEL), F32),
            jax.ShapeDtypeStruct((t, D_MODEL), F32),
        ],
        scratch_shapes=[
            pltpu.VMEM((PROJ_WIDTH, D_MODEL), BF16),
            pltpu.VMEM((D_MODEL, D_MODEL), BF16),
            pltpu.VMEM((D_MODEL, D_MODEL), BF16),
            pltpu.VMEM((tm, WIDTH_A), BF16),
            pltpu.VMEM((tm, WIDTH_A), F32),
            pltpu.VMEM((8, WIDTH_B), F32),
            pltpu.SemaphoreType.DMA((3 * N_DEV,)),
        ],
        compiler_params=pltpu.CompilerParams(dimension_semantics=("arbitrary",), vmem_limit_bytes=56 * MIB),
    )(x, p_l, wg, w_proj_t, conv_k, norm_g, ln_g, ln_b, w_mix, b_mix, ple_g)


def _loss_head(x2, target, final_g):
    t = x2.shape[0]
    tm = _tile(t, 512)
    nt = t // tm

    def body(x_ref, tgt_ref, g_ref, dx_ref, loss_ref, dg_ref, loss_acc, dg_acc):
        i = pl.program_id(0)

        @pl.when(i == 0)
        def _():
            loss_acc[...] = jnp.zeros_like(loss_acc)
            dg_acc[...] = jnp.zeros_like(dg_acc)

        xv = x_ref[...]
        g = g_ref[...]
        rstd = lax.rsqrt(jnp.mean(xv * xv, axis=-1, keepdims=True) + EPS)
        xhat = xv * rstd
        err = xhat * g - tgt_ref[...]
        loss_acc[...] += _colsum8(err * err)
        dy = err * (1.0 / D_MODEL)
        dg_acc[...] += _colsum8(dy * xhat)
        dxhat = dy * g
        dx_ref[...] = rstd * (dxhat - xhat * jnp.mean(dxhat * xhat, axis=-1, keepdims=True))

        @pl.when(i == nt - 1)
        def _():
            total = jnp.sum(loss_acc[...]) * (0.5 / D_MODEL)
            r8 = lax.broadcasted_iota(jnp.int32, (8, 128), 0)
            c8 = lax.broadcasted_iota(jnp.int32, (8, 128), 1)
            loss_ref[...] = jnp.where((r8 == 0) & (c8 == 0), total, 0.0)
            dg_ref[...] = jnp.sum(dg_acc[...], axis=0, keepdims=True)

    return pl.pallas_call(
        body,
        name="loss_head",
        grid=(nt,),
        in_specs=[
            pl.BlockSpec((tm, D_MODEL), lambda i: (i, 0)),
            pl.BlockSpec((tm, D_MODEL), lambda i: (i, 0)),
            pl.BlockSpec((1, D_MODEL), lambda i: (0, 0)),
        ],
        out_specs=[
            pl.BlockSpec((tm, D_MODEL), lambda i: (i, 0)),
            pl.BlockSpec((8, 128), lambda i: (0, 0)),
            pl.BlockSpec((1, D_MODEL), lambda i: (0, 0)),
        ],
        out_shape=[
            jax.ShapeDtypeStruct((t, D_MODEL), F32),
            jax.ShapeDtypeStruct((8, 128), F32),
            jax.ShapeDtypeStruct((1, D_MODEL), F32),
        ],
        scratch_shapes=[pltpu.VMEM((8, D_MODEL), F32), pltpu.VMEM((8, D_MODEL), F32)],
        compiler_params=pltpu.CompilerParams(dimension_semantics=("arbitrary",), vmem_limit_bytes=32 * MIB),
    )(x2, target, final_g)


def _backward_layer(layer, dx2, x_in, x1, proj, gpre, p_l, wg, w_proj_t, conv_k, norm_g, ln_g, ln_b,
                    w_mix, w_mix_t, b_mix, ple_g):
    t = x_in.shape[0]
    tm = _tile(t, 256)
    nt = t // tm
    n_chunks = tm // CHUNK
    halo_rows = 16

    def body(dx2_ref, xin_ref, x1_ref, proj_ref, halo_ref, gpre_ref, p_ref, wg_ref, wpt_ref, cw_ref,
             ng_ref, lng_ref, lnb_ref, wm_ref, wmt_ref, bm_ref, pg_ref,
             dxin_ref, dproj_ref, dx1_ref, dgpre_ref, dpp_ref,
             dng_ref, dpg_ref, dlng_ref, dlnb_ref, dws_ref, dbm_ref, dcw_ref,
             w_in_t, w_out, w_gate, vln_s, mixed_s, dmix_s, dvln_s, carry_s,
             ng_acc, pg_acc, lng_acc, lnb_acc, cw_acc, sems):
        i = pl.program_id(0)
        tile = nt - 1 - i

        @pl.when(i == 0)
        def _():
            copies = _weight_copies(wg_ref, layer, w_in_t, w_out, w_gate, sems)
            for cp in copies:
                cp.start()
            carry_s[...] = jnp.zeros_like(carry_s)
            ng_acc[...] = jnp.zeros_like(ng_acc)
            pg_acc[...] = jnp.zeros_like(pg_acc)
            lng_acc[...] = jnp.zeros_like(lng_acc)
            lnb_acc[...] = jnp.zeros_like(lnb_acc)
            cw_acc[...] = jnp.zeros_like(cw_acc)
            dws_ref[...] = jnp.zeros_like(dws_ref)
            dbm_ref[...] = jnp.zeros_like(dbm_ref)
            for cp in copies:
                cp.wait()

        dx2v = dx2_ref[...]
        gate = _sigmoid(gpre_ref[...].astype(F32))
        pp = _dot_nt(p_ref[...].astype(BF16), wpt_ref[...])
        dpp_ref[...] = (dx2v * gate).astype(BF16)
        dgpre = (dx2v * pp * gate * (1.0 - gate)).astype(BF16)
        dgpre_ref[...] = dgpre
        dr = _dot_nt(dgpre, w_gate[...])
        x1v = x1_ref[...]
        rstd1 = lax.rsqrt(jnp.mean(x1v * x1v, axis=-1, keepdims=True) + EPS)
        xhat1 = x1v * rstd1
        pg_acc[...] += _colsum8(dr * xhat1)
        dxh = dr * pg_ref[...]
        dx1 = dx2v + rstd1 * (dxh - xhat1 * jnp.mean(dxh * xhat1, axis=-1, keepdims=True))
        dx1b = dx1.astype(BF16)
        dx1_ref[...] = dx1b

        dcat = _dot_nt(dx1b, w_out[...])
        dca = dcat[:, 0:512]
        dcb = dcat[:, 512:1024]

        u = proj_ref[:, 0:512].astype(F32)
        v = proj_ref[:, 512:1024].astype(F32)
        za = proj_ref[:, 1024:1536].astype(F32)
        mu = jnp.mean(v, axis=-1, keepdims=True)
        vc = v - mu
        var = jnp.mean(vc * vc, axis=-1, keepdims=True)
        rs = lax.rsqrt(var + EPS)
        vhat = vc * rs
        lng = lng_ref[...]
        vln_s[...] = (vhat * lng + lnb_ref[...]).astype(BF16)
        for ci in range(n_chunks):
            rows = pl.ds(ci * CHUNK, CHUNK)
            for h in range(HEADS_A):
                cols = pl.ds(h * HEAD_DIM, HEAD_DIM)
                mixed_s[rows, cols] = _dot(wm_ref[h], vln_s[rows, cols]) + bm_ref[h]
        mixed = mixed_s[...]
        sga = _sigmoid(za)
        sa = za * sga
        dsa = sga * (1.0 + za * (1.0 - sga))
        dproj_ref[:, 0:512] = (dca * mixed * sa).astype(BF16)
        dmix = dca * u * sa
        dproj_ref[:, 1024:1536] = (dca * u * mixed * dsa).astype(BF16)
        dmix_s[...] = dmix.astype(BF16)
        dbm_acc = jnp.zeros((CHUNK, WIDTH_A), F32)
        for ci in range(n_chunks):
            rows = pl.ds(ci * CHUNK, CHUNK)
            dbm_acc = dbm_acc + dmix[ci * CHUNK:(ci + 1) * CHUNK, :]
            for h in range(HEADS_A):
                cols = pl.ds(h * HEAD_DIM, HEAD_DIM)
                dvln_s[rows, cols] = _dot(wmt_ref[h], dmix_s[rows, cols])
                dws_ref[h] += _dot_nt(dmix_s[rows, cols], vln_s[rows, cols])
        dbm_ref[...] += dbm_acc
        dvln = dvln_s[...]
        lng_acc[...] += _colsum8(dvln * vhat)
        lnb_acc[...] += _colsum8(dvln)
        dvh = dvln * lng
        dv = rs * (dvh - jnp.mean(dvh, axis=-1, keepdims=True) - vhat * jnp.mean(dvh * vhat, axis=-1, keepdims=True))
        dproj_ref[:, 512:1024] = dv.astype(BF16)

        hb = proj_ref[:, 1536:2048].astype(F32)
        gb = proj_ref[:, 2048:2560].astype(F32)
        gc = proj_ref[:, 2560:3072].astype(F32)
        zb = proj_ref[:, 3072:3584].astype(F32)
        xc = gc * hb
        prev = halo_ref[:, 2560:3072].astype(F32) * halo_ref[:, 1536:2048].astype(F32)
        prev = jnp.where(tile > 0, prev, 0.0)
        row = lax.broadcasted_iota(jnp.int32, (tm, WIDTH_B), 0)
        p1 = prev[halo_rows - 1:halo_rows, :]
        p2 = prev[halo_rows - 2:halo_rows - 1, :]
        xc_m1 = jnp.where(row == 0, p1, pltpu.roll(xc, 1, 0))
        xc_m2 = jnp.where(row == 0, p2, jnp.where(row == 1, p1, pltpu.roll(xc, 2, 0)))
        cw = cw_ref[...]
        yc = cw[0:1, :] * xc_m2 + cw[1:2, :] * xc_m1 + cw[2:3, :] * xc
        sgb = _sigmoid(zb)
        sb = zb * sgb
        dsb = sgb * (1.0 + zb * (1.0 - sgb))
        dproj_ref[:, 2048:2560] = (dcb * yc * sb).astype(BF16)
        dyc = dcb * gb * sb
        dproj_ref[:, 3072:3584] = (dcb * gb * yc * dsb).astype(BF16)
        nxt = carry_s[...]
        dyc_p1 = jnp.where(row == tm - 1, nxt[0:1, :], pltpu.roll(dyc, tm - 1, 0))
        dyc_p2 = jnp.where(row == tm - 1, nxt[1:2, :], jnp.where(row == tm - 2, nxt[0:1, :], pltpu.roll(dyc, tm - 2, 0)))
        carry_s[...] = dyc[0:8, :]
        dxc = cw[2:3, :] * dyc + cw[1:2, :] * dyc_p1 + cw[0:1, :] * dyc_p2
        cw_acc[0] += _colsum8(dyc * xc_m2)
        cw_acc[1] += _colsum8(dyc * xc_m1)
        cw_acc[2] += _colsum8(dyc * xc)
        dproj_ref[:, 1536:2048] = (dxc * gc).astype(BF16)
        dproj_ref[:, 2560:3072] = (dxc * hb).astype(BF16)

        dhn = _dot(dproj_ref[...], w_in_t[...])
        xv = xin_ref[...]
        rstd0 = lax.rsqrt(jnp.mean(xv * xv, axis=-1, keepdims=True) + EPS)
        xhat0 = xv * rstd0
        ng_acc[...] += _colsum8(dhn * xhat0)
        dxh0 = dhn * ng_ref[...]
        dxin_ref[...] = dx1 + rstd0 * (dxh0 - xhat0 * jnp.mean(dxh0 * xhat0, axis=-1, keepdims=True))

        @pl.when(i == nt - 1)
        def _():
            dng_ref[...] = jnp.sum(ng_acc[...], axis=0, keepdims=True)
            dpg_ref[...] = jnp.sum(pg_acc[...], axis=0, keepdims=True)
            dlng_ref[...] = jnp.sum(lng_acc[...], axis=0, keepdims=True)
            dlnb_ref[...] = jnp.sum(lnb_acc[...], axis=0, keepdims=True)
            for k in range(3):
                dcw_ref[k:k + 1, :] = jnp.sum(cw_acc[k], axis=0, keepdims=True)
            dcw_ref[3:8, :] = jnp.zeros((5, WIDTH_B), F32)

    def tok(width):
        return pl.BlockSpec((tm, width), lambda i: (nt - 1 - i, 0))

    def whole(shape):
        return pl.BlockSpec(shape, lambda i: (0,) * len(shape))

    halo_spec = pl.BlockSpec(
        (halo_rows, PROJ_WIDTH), lambda i: (jnp.maximum((nt - 1 - i) * (tm // halo_rows) - 1, 0), 0)
    )

    return pl.pallas_call(
        body,
        name=f"layer{layer}_backward",
        grid=(nt,),
        in_specs=[
            tok(D_MODEL), tok(D_MODEL), tok(D_MODEL), tok(PROJ_WIDTH), halo_spec, tok(D_MODEL), tok(PLE_DIM),
            pl.BlockSpec(memory_space=pl.ANY),
            whole((D_MODEL, PLE_DIM)), whole((8, WIDTH_B)), whole((1, D_MODEL)), whole((1, WIDTH_A)), whole((1, WIDTH_A)),
            whole((HEADS_A, CHUNK, CHUNK)), whole((HEADS_A, CHUNK, CHUNK)), whole((HEADS_A, CHUNK, HEAD_DIM)),
            whole((1, D_MODEL)),
        ],
        out_specs=[
            tok(D_MODEL), tok(PROJ_WIDTH), tok(D_MODEL), tok(D_MODEL), tok(D_MODEL),
            whole((1, D_MODEL)), whole((1, D_MODEL)), whole((1, WIDTH_A)), whole((1, WIDTH_A)),
            whole((HEADS_A, CHUNK, CHUNK)), whole((CHUNK, WIDTH_A)), whole((8, WIDTH_B)),
        ],
        out_shape=[
            jax.ShapeDtypeStruct((t, D_MODEL), F32),
            jax.ShapeDtypeStruct((t, PROJ_WIDTH), BF16),
            jax.ShapeDtypeStruct((t, D_MODEL), BF16),
            jax.ShapeDtypeStruct((t, D_MODEL), BF16),
            jax.ShapeDtypeStruct((t, D_MODEL), BF16),
            jax.ShapeDtypeStruct((1, D_MODEL), F32),
            jax.ShapeDtypeStruct((1, D_MODEL), F32),
            jax.ShapeDtypeStruct((1, WIDTH_A), F32),
            jax.ShapeDtypeStruct((1, WIDTH_A), F32),
            jax.ShapeDtypeStruct((HEADS_A, CHUNK, CHUNK), F32),
            jax.ShapeDtypeStruct((CHUNK, WIDTH_A), F32),
            jax.ShapeDtypeStruct((8, WIDTH_B), F32),
        ],
        scratch_shapes=[
            pltpu.VMEM((PROJ_WIDTH, D_MODEL), BF16),
            pltpu.VMEM((D_MODEL, D_MODEL), BF16),
            pltpu.VMEM((D_MODEL, D_MODEL), BF16),
            pltpu.VMEM((tm, WIDTH_A), BF16),
            pltpu.VMEM((tm, WIDTH_A), F32),
            pltpu.VMEM((tm, WIDTH_A), BF16),
            pltpu.VMEM((tm, WIDTH_A), F32),
            pltpu.VMEM((8, WIDTH_B), F32),
            pltpu.VMEM((8, D_MODEL), F32),
            pltpu.VMEM((8, D_MODEL), F32),
            pltpu.VMEM((8, WIDTH_A), F32),
            pltpu.VMEM((8, WIDTH_A), F32),
            pltpu.VMEM((3, 8, WIDTH_B), F32),
            pltpu.SemaphoreType.DMA((3 * N_DEV,)),
        ],
        compiler_params=pltpu.CompilerParams(dimension_semantics=("arbitrary",), vmem_limit_bytes=56 * MIB),
    )(dx2, x_in, x1, proj, proj, gpre, p_l, wg, w_proj_t, conv_k, norm_g, ln_g, ln_b, w_mix, w_mix_t, b_mix, ple_g)


def _weight_grads(layer, dproj, hn, cat, dx1, r, dgpre, dpp, p_l):
    t = hn.shape[0]
    tk = _tile(t, 512)
    nt = t // tk
    in_blocks = PROJ_WIDTH // 512

    def body(dproj_ref, hn_ref, cat_ref, dx1_ref, r_ref, dgpre_ref, dpp_ref, p_ref, pack_ref,
             acc_in, acc_out, acc_gate, acc_proj, stage, sems):
        i = pl.program_id(0)

        @pl.when(i == 0)
        def _():
            acc_in[...] = jnp.zeros_like(acc_in)
            acc_out[...] = jnp.zeros_like(acc_out)
            acc_gate[...] = jnp.zeros_like(acc_gate)
            acc_proj[...] = jnp.zeros_like(acc_proj)

        hnv = hn_ref[...]
        for b in range(in_blocks):
            acc_in[pl.ds(b * 512, 512), :] += _dot_tn(dproj_ref[:, b * 512:(b + 1) * 512], hnv)
        dx1v = dx1_ref[...]
        dgv = dgpre_ref[...]
        for b in range(D_MODEL // 512):
            acc_out[pl.ds(b * 512, 512), :] += _dot_tn(cat_ref[:, b * 512:(b + 1) * 512], dx1v)
            acc_gate[pl.ds(b * 512, 512), :] += _dot_tn(r_ref[:, b * 512:(b + 1) * 512], dgv)
        pv = p_ref[...].astype(BF16)
        for b in range(D_MODEL // 512):
            acc_proj[pl.ds(b * 512, 512), :] += _dot_tn(dpp_ref[:, b * 512:(b + 1) * 512], pv)

        @pl.when(i == nt - 1)
        def _():
            def out_copy(s):
                return pltpu.make_async_copy(stage.at[s % 2], pack_ref.at[s], sems.at[s % 2])

            for s in range(N_DEV):
                if s >= 2:
                    out_copy(s - 2).wait()
                buf = stage.at[s % 2]
                buf[pl.ds(OFF_IN, ROWS_IN), :] = acc_in[pl.ds(s * ROWS_IN, ROWS_IN), :].astype(BF16)
                buf[pl.ds(OFF_OUT, ROWS_OUT), :] = acc_out[pl.ds(s * ROWS_OUT, ROWS_OUT), :].astype(BF16)
                buf[pl.ds(OFF_GATE, ROWS_GATE), :] = acc_gate[pl.ds(s * ROWS_GATE, ROWS_GATE), :].astype(BF16)
                for j in range(D_MODEL // PLE_DIM):
                    buf[pl.ds(OFF_PROJ, ROWS_PROJ), pl.ds(j * PLE_DIM, PLE_DIM)] = acc_proj[
                        pl.ds(s * ROWS_OUT + j * ROWS_PROJ, ROWS_PROJ), :
                    ].astype(BF16)
                out_copy(s).start()
            out_copy(N_DEV - 2).wait()
            out_copy(N_DEV - 1).wait()

    def tok(width):
        return pl.BlockSpec((tk, width), lambda i: (i, 0))

    return pl.pallas_call(
        body,
        name=f"layer{layer}_weight_grads",
        grid=(nt,),
        in_specs=[tok(PROJ_WIDTH), tok(D_MODEL), tok(D_MODEL), tok(D_MODEL), tok(D_MODEL), tok(D_MODEL), tok(D_MODEL), tok(PLE_DIM)],
        out_specs=pl.BlockSpec(memory_space=pl.ANY),
        out_shape=jax.ShapeDtypeStruct((N_DEV, ROWS_GRAD, D_MODEL), BF16),
        scratch_shapes=[
            pltpu.VMEM((PROJ_WIDTH, D_MODEL), F32),
            pltpu.VMEM((D_MODEL, D_MODEL), F32),
            pltpu.VMEM((D_MODEL, D_MODEL), F32),
            pltpu.VMEM((D_MODEL, PLE_DIM), F32),
            pltpu.VMEM((2, ROWS_GRAD, D_MODEL), BF16),
            pltpu.SemaphoreType.DMA((2,)),
        ],
        compiler_params=pltpu.CompilerParams(dimension_semantics=("arbitrary",), vmem_limit_bytes=58 * MIB),
    )(dproj, hn, cat, dx1, r, dgpre, dpp, p_l)


def _reduce_scatter(layer, pack):
    rows, n = pack.shape[1], pack.shape[2]

    def body(g_ref, out_ref, r1, a_s, r2, send1, recv1, send2, recv2):
        x, y, c = lax.axis_index("x"), lax.axis_index("y"), lax.axis_index("c")
        sibling = (x, y, 1 - c)
        chip = 2 * x + y

        def to_sibling(j):
            return pltpu.make_async_remote_copy(
                src_ref=g_ref.at[2 * j + 1 - c], dst_ref=r1.at[j], send_sem=send1.at[j], recv_sem=recv1.at[j],
                device_id=sibling, device_id_type=MESH,
            )

        first = [to_sibling(j) for j in range(4)]
        for cp in first:
            cp.start()
        for j in range(4):
            first[j].wait_recv()
            a_s[j] = (g_ref[2 * j + c].astype(F32) + r1[j].astype(F32)).astype(BF16)

        flips = [(1, 0), (0, 1), (1, 1)]

        def to_chip(k):
            fx, fy = flips[k]
            tx, ty = x ^ fx, y ^ fy
            return pltpu.make_async_remote_copy(
                src_ref=a_s.at[2 * tx + ty], dst_ref=r2.at[k], send_sem=send2.at[k], recv_sem=recv2.at[k],
                device_id=(tx, ty, c), device_id_type=MESH,
            )

        second = [to_chip(k) for k in range(3)]
        for cp in second:
            cp.start()
        total = g_ref[2 * chip + c].astype(F32) + r1[chip].astype(F32)
        for k in range(3):
            second[k].wait_recv()
            total = total + r2[k].astype(F32)
        out_ref[...] = total
        for cp in first + second:
            cp.wait_send()

    return pl.pallas_call(
        body,
        name=f"layer{layer}_grad_reduce_scatter",
        out_shape=jax.ShapeDtypeStruct((rows, n), F32),
        in_specs=[pl.BlockSpec(memory_space=pltpu.VMEM)],
        out_specs=pl.BlockSpec(memory_space=pltpu.VMEM),
        scratch_shapes=[
            pltpu.VMEM((4, rows, n), BF16),
            pltpu.VMEM((4, rows, n), BF16),
            pltpu.VMEM((3, rows, n), BF16),
            pltpu.SemaphoreType.DMA((4,)),
            pltpu.SemaphoreType.DMA((4,)),
            pltpu.SemaphoreType.DMA((3,)),
            pltpu.SemaphoreType.DMA((3,)),
        ],
        compiler_params=pltpu.CompilerParams(vmem_limit_bytes=48 * MIB),
    )(pack)


def _all_reduce_small(part):
    rows, n = part.shape

    def body(p_ref, out_ref, r1, q, send1, recv1, send2, recv2):
        x, y, c = lax.axis_index("x"), lax.axis_index("y"), lax.axis_index("c")
        chip = 2 * x + y
        pair = pltpu.make_async_remote_copy(
            src_ref=p_ref, dst_ref=r1, send_sem=send1, recv_sem=recv1, device_id=(x, y, 1 - c), device_id_type=MESH
        )
        pair.start()
        pair.wait_recv()
        q[chip] = p_ref[...] + r1[...]

        flips = [(1, 0), (0, 1), (1, 1)]
        second = []
        for k, (fx, fy) in enumerate(flips):
            second.append(
                pltpu.make_async_remote_copy(
                    src_ref=q.at[chip], dst_ref=q.at[chip], send_sem=send2.at[k], recv_sem=recv2.at[k],
                    device_id=(x ^ fx, y ^ fy, c), device_id_type=MESH,
                )
            )
        for cp in second:
            cp.start()
        for cp in second:
            cp.wait_recv()
        out_ref[...] = ((q[0] + q[1]) + q[2]) + q[3]
        pair.wait_send()
        for cp in second:
            cp.wait_send()

    return pl.pallas_call(
        body,
        name="small_grads_all_reduce",
        out_shape=jax.ShapeDtypeStruct((rows, n), F32),
        in_specs=[pl.BlockSpec(memory_space=pltpu.VMEM)],
        out_specs=pl.BlockSpec(memory_space=pltpu.VMEM),
        scratch_shapes=[
            pltpu.VMEM((rows, n), F32),
            pltpu.VMEM((4, rows, n), F32),
            pltpu.SemaphoreType.DMA,
            pltpu.SemaphoreType.DMA,
            pltpu.SemaphoreType.DMA((3,)),
            pltpu.SemaphoreType.DMA((3,)),
        ],
    )(part)


def _adamw(name, groups):
    n = len(groups)
    bc1 = 1.0 - ADAM_B1 ** ADAM_STEP
    bc2 = 1.0 - ADAM_B2 ** ADAM_STEP

    def body(*refs):
        ins, outs = refs[: 4 * n], refs[4 * n:]
        for k in range(n):
            w_ref, g_ref, m_ref, v_ref = ins[4 * k: 4 * k + 4]
            d_ref, nm_ref, nv_ref = outs[3 * k: 3 * k + 3]
            g = g_ref[...]
            m = ADAM_B1 * m_ref[...] + (1.0 - ADAM_B1) * g
            v = ADAM_B2 * v_ref[...] + (1.0 - ADAM_B2) * (g * g)
            m_hat = m / bc1
            v_hat = v / bc2
            d_ref[...] = -ADAM_LR * (m_hat / (jnp.sqrt(v_hat) + ADAM_EPS) + ADAM_WD * w_ref[...])
            nm_ref[...] = m
            nv_ref[...] = v

    flat = [a for grp in groups for a in grp]
    out_shape = []
    for w, _, _, _ in groups:
        out_shape += [jax.ShapeDtypeStruct(w.shape, F32)] * 3
    outs = pl.pallas_call(
        body,
        name=name,
        out_shape=out_shape,
        in_specs=[pl.BlockSpec(memory_space=pltpu.VMEM)] * len(flat),
        out_specs=[pl.BlockSpec(memory_space=pltpu.VMEM)] * len(out_shape),
        compiler_params=pltpu.CompilerParams(vmem_limit_bytes=48 * MIB),
    )(*flat)
    return [tuple(outs[3 * k: 3 * k + 3]) for k in range(n)]


def _rows128(a):
    flat = a.reshape(-1)
    pad = (-flat.shape[0]) % 1024
    if pad:
        flat = jnp.concatenate([flat, jnp.zeros((pad,), flat.dtype)])
    return flat.reshape(-1, 128)


SMALL_FIELDS = ("w_s", "norm_g", "ple_norm_g", "ln_v_g", "ln_v_b", "b_s", "final_g", "conv_w")


def _pack_small(parts):
    return jnp.concatenate([_rows128(parts[k]) for k in SMALL_FIELDS], axis=0)


def _unpack_small(packed, shapes):
    out, row = {}, 0
    for k in SMALL_FIELDS:
        size = 1
        for d in shapes[k]:
            size *= d
        nrows = -(-size // 1024) * 8
        out[k] = packed[row:row + nrows].reshape(-1)[:size].reshape(shapes[k])
        row += nrows
    return out, row


def _split3_bf16(a):
    b1 = a.astype(BF16)
    r1 = a - b1.astype(F32)
    b2 = r1.astype(BF16)
    b3 = (r1 - b2.astype(F32)).astype(BF16)
    return b1, b2, b3


def _pack_weight_shard(w_in, w_out, w_gate, w_proj, conv_w):
    layers = []
    for l in range(DEPTH):
        w_in_t = jnp.transpose(w_in[l]).astype(BF16)
        proj_t = jnp.transpose(w_proj[l]).astype(BF16)
        proj_rows = proj_t.reshape(D_MODEL // PLE_DIM, ROWS_PROJ, PLE_DIM).transpose(1, 0, 2).reshape(ROWS_PROJ, D_MODEL)
        conv_parts = jnp.concatenate([b.reshape(-1) for b in _split3_bf16(conv_w[l])])
        conv_rows = jnp.concatenate([conv_parts, jnp.zeros((ROWS_CONV * D_MODEL - conv_parts.shape[0],), BF16)])
        layers += [w_in_t, w_out[l].astype(BF16), w_gate[l].astype(BF16), proj_rows, conv_rows.reshape(ROWS_CONV, D_MODEL)]
    return jnp.concatenate(layers, axis=0)


def _unpack_proj_conv(wg, layer):
    per_dev = wg.reshape(N_DEV, ROWS_SHARD, D_MODEL)[:, layer * ROWS_LAYER:(layer + 1) * ROWS_LAYER]
    proj_rows = per_dev[:, OFF_PROJ:OFF_PROJ + ROWS_PROJ]
    proj_t = proj_rows.reshape(N_DEV, ROWS_PROJ, D_MODEL // PLE_DIM, PLE_DIM).transpose(0, 2, 1, 3).reshape(D_MODEL, PLE_DIM)
    n_conv = (WIDTH_B // N_DEV) * 3
    conv_parts = per_dev[:, OFF_CONV].astype(F32)[:, :3 * n_conv].reshape(N_DEV, 3, n_conv)
    conv = (conv_parts[:, 0] + conv_parts[:, 1]) + conv_parts[:, 2]
    conv_k = jnp.transpose(conv.reshape(WIDTH_B, 3))
    conv_k = jnp.concatenate([conv_k, jnp.zeros((5, WIDTH_B), F32)], axis=0)
    return proj_t, conv_k


def _unpack_grad_shard(red):
    g_in = jnp.transpose(red[OFF_IN:OFF_IN + ROWS_IN])
    g_out = red[OFF_OUT:OFF_OUT + ROWS_OUT]
    g_gate = red[OFF_GATE:OFF_GATE + ROWS_GATE]
    proj_rows = red[OFF_PROJ:OFF_PROJ + ROWS_PROJ]
    proj_t = proj_rows.reshape(ROWS_PROJ, D_MODEL // PLE_DIM, PLE_DIM).transpose(1, 0, 2).reshape(ROWS_OUT, PLE_DIM)
    return g_in, g_out, g_gate, jnp.transpose(proj_t)


def kernel(x, p, norm_g, w_in, ln_v_g, ln_v_b, w_s, b_s, conv_w, w_out, ple_norm_g, w_ple_gate, w_ple_proj, final_g, loss_target, m_norm_g, m_w_in, m_ln_v_g, m_ln_v_b, m_w_s, m_b_s, m_conv_w, m_w_out, m_ple_norm_g, m_w_ple_gate, m_w_ple_proj, m_final_g, v_norm_g, v_w_in, v_ln_v_g, v_ln_v_b, v_w_s, v_b_s, v_conv_w, v_w_out, v_ple_norm_g, v_w_ple_gate, v_w_ple_proj, v_final_g):
    me = 4 * lax.axis_index("x") + 2 * lax.axis_index("y") + lax.axis_index("c")
    xs = x[0]
    target = loss_target[0]

    wg = _all_gather_rows(_pack_weight_shard(w_in, w_out, w_ple_gate, w_ple_proj, conv_w))

    tril = jnp.tril(jnp.ones((CHUNK, CHUNK), F32))
    layer_consts = []
    for l in range(DEPTH):
        w_proj_t, conv_k = _unpack_proj_conv(wg, l)
        w_mix = w_s[l] * tril[None]
        layer_consts.append(dict(
            w_proj_t=w_proj_t, conv_k=conv_k,
            norm_g=norm_g[l].reshape(1, D_MODEL), ln_g=ln_v_g[l].reshape(1, WIDTH_A), ln_b=ln_v_b[l].reshape(1, WIDTH_A),
            w_mix=w_mix.astype(BF16), w_mix_t=jnp.swapaxes(w_mix, 1, 2).astype(BF16),
            b_mix=jnp.broadcast_to(b_s[l][:, :, None], (HEADS_A, CHUNK, HEAD_DIM)),
            ple_g=ple_norm_g[l].reshape(1, D_MODEL),
        ))

    saved = []
    h = xs
    for l in range(DEPTH):
        k = layer_consts[l]
        proj, hn, cat, r, gpre, x1, x2 = _forward_layer(
            l, h, p[l, 0], wg, k["w_proj_t"], k["conv_k"], k["norm_g"], k["ln_g"], k["ln_b"], k["w_mix"], k["b_mix"], k["ple_g"])
        saved.append(dict(x_in=h, proj=proj, hn=hn, cat=cat, r=r, gpre=gpre, x1=x1))
        h = x2

    dx, loss_tile, d_final = _loss_head(h, target, final_g.reshape(1, D_MODEL))

    small = {}
    reduced = [None] * DEPTH
    for l in reversed(range(DEPTH)):
        k, s = layer_consts[l], saved[l]
        (dx, dproj, dx1, dgpre, dpp, d_ng, d_pg, d_lng, d_lnb, d_ws, d_bm, d_cw) = _backward_layer(
            l, dx, s["x_in"], s["x1"], s["proj"], s["gpre"], p[l, 0], wg, k["w_proj_t"], k["conv_k"],
            k["norm_g"], k["ln_g"], k["ln_b"], k["w_mix"], k["w_mix_t"], k["b_mix"], k["ple_g"])
        pack = _weight_grads(l, dproj, s["hn"], s["cat"], dx1, s["r"], dgpre, dpp, p[l, 0])
        reduced[l] = _reduce_scatter(l, pack)
        small[l] = dict(
            w_s=d_ws * tril[None], norm_g=d_ng[0], ple_norm_g=d_pg[0], ln_v_g=d_lng[0], ln_v_b=d_lnb[0],
            b_s=jnp.transpose(jnp.sum(d_bm.reshape(CHUNK, HEADS_A, HEAD_DIM), axis=-1)),
            conv_w=jnp.transpose(d_cw[0:3]),
        )
    grad_x = dx[None]

    small_parts = {f: jnp.stack([small[l][f] for l in range(DEPTH)]) for f in SMALL_FIELDS if f != "final_g"}
    small_parts["final_g"] = d_final[0]
    part = jnp.concatenate([_pack_small(small_parts), loss_tile], axis=0)
    total = _all_reduce_small(part)
    small_shapes = {f: small_parts[f].shape for f in SMALL_FIELDS}
    g_small, row = _unpack_small(total, small_shapes)
    loss = total[row, 0]
    g_small["conv_w"] = lax.dynamic_slice_in_dim(g_small["conv_w"], me * (WIDTH_B // N_DEV), WIDTH_B // N_DEV, axis=1)

    per_layer = [_unpack_grad_shard(reduced[l]) for l in range(DEPTH)]
    g_w_in = jnp.stack([per_layer[l][0] for l in range(DEPTH)])
    g_w_out = jnp.stack([per_layer[l][1] for l in range(DEPTH)])
    g_w_gate = jnp.stack([per_layer[l][2] for l in range(DEPTH)])
    g_w_proj = jnp.stack([per_layer[l][3] for l in range(DEPTH)])

    w_small = dict(w_s=w_s, norm_g=norm_g, ple_norm_g=ple_norm_g, ln_v_g=ln_v_g, ln_v_b=ln_v_b, b_s=b_s, final_g=final_g, conv_w=conv_w)
    m_small = dict(w_s=m_w_s, norm_g=m_norm_g, ple_norm_g=m_ple_norm_g, ln_v_g=m_ln_v_g, ln_v_b=m_ln_v_b, b_s=m_b_s, final_g=m_final_g, conv_w=m_conv_w)
    v_small = dict(w_s=v_w_s, norm_g=v_norm_g, ple_norm_g=v_ple_norm_g, ln_v_g=v_ln_v_g, ln_v_b=v_ln_v_b, b_s=v_b_s, final_g=v_final_g, conv_w=v_conv_w)
    (big_in,) = _adamw("adamw_w_in", [(w_in, g_w_in, m_w_in, v_w_in)])
    rest = _adamw("adamw_rest", [
        (w_out, g_w_out, m_w_out, v_w_out),
        (w_ple_gate, g_w_gate, m_w_ple_gate, v_w_ple_gate),
        (w_ple_proj, g_w_proj, m_w_ple_proj, v_w_ple_proj),
        (_pack_small(w_small), _pack_small(g_small), _pack_small(m_small), _pack_small(v_small)),
    ])
    w_shapes = {f: w_small[f].shape for f in SMALL_FIELDS}
    upd_small = [_unpack_small(a, w_shapes)[0] for a in rest[3]]

    grads = dict(g_small, w_in=g_w_in, w_out=g_w_out, w_ple_gate=g_w_gate, w_ple_proj=g_w_proj)
    updates = []
    for j in range(3):
        updates.append(dict(upd_small[j], w_in=big_in[j], w_out=rest[0][j], w_ple_gate=rest[1][j], w_ple_proj=rest[2][j]))
    order = ["norm_g", "w_in", "ln_v_g", "ln_v_b", "w_s", "b_s", "conv_w", "w_out", "ple_norm_g", "w_ple_gate", "w_ple_proj", "final_g"]
    return (loss, grad_x, *[grads[n] for n in order], *[updates[0][n] for n in order],
            *[updates[1][n] for n in order], *[updates[2][n] for n in order])
```

```python
import jax
import jax.numpy as jnp
from jax import lax
from jax.experimental import pallas as pl
from jax.experimental.pallas import tpu as pltpu

F32 = jnp.float32
BF16 = jnp.bfloat16

D_MODEL = 1024
WIDTH_A = 512
WIDTH_B = 512
HEADS_A = 4
HEAD_DIM = 128
CHUNK = 128
PLE_DIM = 256
PROJ_WIDTH = 3584
DEPTH = 2
EPS = 1e-6
N_DEV = 8

ADAM_LR = 0.001
ADAM_B1 = 0.9
ADAM_B2 = 0.999
ADAM_EPS = 1e-08
ADAM_WD = 0.01
ADAM_STEP = 10

ROWS_IN = PROJ_WIDTH // N_DEV
ROWS_OUT = D_MODEL // N_DEV
ROWS_GATE = D_MODEL // N_DEV
ROWS_PROJ = (D_MODEL // N_DEV) * PLE_DIM // D_MODEL
ROWS_CONV = 16
OFF_IN = 0
OFF_OUT = OFF_IN + ROWS_IN
OFF_GATE = OFF_OUT + ROWS_OUT
OFF_PROJ = OFF_GATE + ROWS_GATE
OFF_CONV = OFF_PROJ + ROWS_PROJ
ROWS_GRAD = OFF_CONV
ROWS_LAYER = OFF_CONV + ROWS_CONV

MIB = 1024 * 1024
MESH = pl.DeviceIdType.MESH

NT_DIMS = (((1,), (1,)), ((), ()))
TN_DIMS = (((0,), (0,)), ((), ()))


def _dot(a, b):
    return jnp.dot(a, b, preferred_element_type=F32)


def _dot_nt(a, b):
    return lax.dot_general(a, b, NT_DIMS, preferred_element_type=F32)


def _dot_tn(a, b):
    return lax.dot_general(a, b, TN_DIMS, preferred_element_type=F32)


def _colsum8(a):
    rows, n = a.shape
    return jnp.sum(a.reshape(rows // 8, 8, n), axis=0)


def _sigmoid(z):
    return 1.0 / (1.0 + jnp.exp(-z))


def _tile(t, want):
    return want if t % want == 0 else t


class _TwoLevelGather:
    def __init__(self, x_ref, out_ref, m_per, send_sems, recv_sems, local_sem):
        x, y, c = lax.axis_index("x"), lax.axis_index("y"), lax.axis_index("c")
        self.c = c
        self.me, self.sibling = (x, y, c), (x, y, 1 - c)
        self.chips = [(1 - x, y), (x, 1 - y), (1 - x, 1 - y)]
        self.x_ref, self.out_ref, self.m_per = x_ref, out_ref, m_per
        self.send_sems, self.recv_sems = send_sems, recv_sems
        self.mine = pltpu.make_async_copy(x_ref, self.rows(*self.me), local_sem)

    def rows(self, px, py, pc):
        return self.out_ref.at[pl.ds((4 * px + 2 * py + pc) * self.m_per, self.m_per), :]

    def copy(self, k, block, to, src=None):
        return pltpu.make_async_remote_copy(
            src_ref=self.rows(*block) if src is None else src,
            dst_ref=self.rows(*block),
            send_sem=self.send_sems.at[k],
            recv_sem=self.recv_sems.at[k],
            device_id=to,
            device_id_type=MESH,
        )

    def first(self):
        out = [self.copy(0, self.me, self.sibling, src=self.x_ref)]
        return out + [self.copy(1 + j, self.me, (*chip, self.c), src=self.x_ref) for j, chip in enumerate(self.chips)]

    def passed(self):
        return [self.copy(4 + j, (*chip, self.c), self.sibling) for j, chip in enumerate(self.chips)]

    def start(self):
        self.mine.start()
        for cp in self.first():
            cp.start()

    def pass_on(self):
        passed = self.passed()
        for j, chip in enumerate(self.chips):
            self.copy(1 + j, (*chip, self.c), self.me).wait_recv()
            passed[j].start()

    def finish(self):
        self.copy(0, self.sibling, self.me).wait_recv()
        for j, chip in enumerate(self.chips):
            self.copy(4 + j, (*chip, 1 - self.c), self.me).wait_recv()
        for cp in self.first() + self.passed():
            cp.wait_send()
        self.mine.wait()


GATHER_SEMS = [pltpu.SemaphoreType.DMA((7,)), pltpu.SemaphoreType.DMA((7,)), pltpu.SemaphoreType.DMA]


def _all_gather_rows(shard):
    m_per, n = shard.shape

    def body(x_ref, out_ref, send_sems, recv_sems, local_sem):
        ag = _TwoLevelGather(x_ref, out_ref, m_per, send_sems, recv_sems, local_sem)
        ag.start()
        ag.pass_on()
        ag.finish()

    return pl.pallas_call(
        body,
        name="weights_all_gather",
        out_shape=jax.ShapeDtypeStruct((N_DEV * m_per, n), shard.dtype),
        in_specs=[pl.BlockSpec(memory_space=pltpu.VMEM)],
        out_specs=pl.BlockSpec(memory_space=pltpu.VMEM),
        scratch_shapes=list(GATHER_SEMS),
        compiler_params=pltpu.CompilerParams(vmem_limit_bytes=48 * MIB),
    )(shard)


def _weight_copies(wg_ref, w_in_t, w_out, w_gate, sems):
    copies = []
    k = 0
    for s in range(N_DEV):
        base = s * ROWS_LAYER
        for dst, off, rows in ((w_in_t, OFF_IN, ROWS_IN), (w_out, OFF_OUT, ROWS_OUT), (w_gate, OFF_GATE, ROWS_GATE)):
            copies.append(
                pltpu.make_async_copy(
                    wg_ref.at[pl.ds(base + off, rows), :], dst.at[pl.ds(s * rows, rows), :], sems.at[k]
                )
            )
            k += 1
    return copies


def _forward_layer(layer, x, p_l, wg, w_proj_t, conv_k, norm_g, ln_g, ln_b, w_mix, b_mix, ple_g, next_shard=None):
    t = x.shape[0]
    tm = _tile(t, 256)
    nt = t // tm
    n_chunks = tm // CHUNK
    gathers = next_shard is not None

    def body(*refs):
        (x_ref, p_ref, wg_ref, wpt_ref, cw_ref, ng_ref, lng_ref, lnb_ref, wm_ref, bm_ref, pg_ref) = refs[:11]
        refs = refs[11:]
        if gathers:
            shard_ref, refs = refs[0], refs[1:]
        (proj_ref, hn_ref, cat_ref, r_ref, gpre_ref, x1_ref, x2_ref) = refs[:7]
        refs = refs[7:]
        if gathers:
            gathered_ref, refs = refs[0], refs[1:]
        (w_in_t, w_out, w_gate, vln_s, mixed_s, halo_s, sems) = refs[:7]
        i = pl.program_id(0)
        if gathers:
            ag = _TwoLevelGather(shard_ref, gathered_ref, ROWS_LAYER, *refs[7:10])

            @pl.when(i == 0)
            def _():
                ag.start()

            @pl.when(i == nt // 2)
            def _():
                ag.pass_on()

        @pl.when(i == 0)
        def _():
            copies = _weight_copies(wg_ref, w_in_t, w_out, w_gate, sems)
            for cp in copies:
                cp.start()
            halo_s[...] = jnp.zeros_like(halo_s)
            for cp in copies:
                cp.wait()

        xv = x_ref[...]
        rstd0 = lax.rsqrt(jnp.mean(xv * xv, axis=-1, keepdims=True) + EPS)
        hn = (xv * rstd0 * ng_ref[...]).astype(BF16)
        hn_ref[...] = hn
        proj = _dot_nt(hn, w_in_t[...])
        proj_ref[...] = proj.astype(BF16)

        u = proj[:, 0:512]
        v = proj[:, 512:1024]
        za = proj[:, 1024:1536]
        mu = jnp.mean(v, axis=-1, keepdims=True)
        vc = v - mu
        var = jnp.mean(vc * vc, axis=-1, keepdims=True)
        vln = vc * lax.rsqrt(var + EPS) * lng_ref[...] + lnb_ref[...]
        vln_s[...] = vln.astype(BF16)
        for ci in range(n_chunks):
            rows = pl.ds(ci * CHUNK, CHUNK)
            for h in range(HEADS_A):
                cols = pl.ds(h * HEAD_DIM, HEAD_DIM)
                mixed_s[rows, cols] = _dot(wm_ref[h], vln_s[rows, cols]) + bm_ref[h]
        out_a = u * mixed_s[...] * (za * _sigmoid(za))
        cat_ref[:, 0:512] = out_a.astype(BF16)

        hb = proj[:, 1536:2048]
        gb = proj[:, 2048:2560]
        gc = proj[:, 2560:3072]
        zb = proj[:, 3072:3584]
        xc = gc * hb
        prev = halo_s[...]
        row = lax.broadcasted_iota(jnp.int32, (tm, WIDTH_B), 0)
        xc_m1 = jnp.where(row == 0, prev[7:8, :], pltpu.roll(xc, 1, 0))
        xc_m2 = jnp.where(row == 0, prev[6:7, :], jnp.where(row == 1, prev[7:8, :], pltpu.roll(xc, 2, 0)))
        halo_s[...] = xc[tm - 8:tm, :]
        cw = cw_ref[...]
        yc = cw[0:1, :] * xc_m2 + cw[1:2, :] * xc_m1 + cw[2:3, :] * xc
        out_b = gb * yc * (zb * _sigmoid(zb))
        cat_ref[:, 512:1024] = out_b.astype(BF16)

        x1 = xv + _dot(cat_ref[...], w_out[...])
        x1_ref[...] = x1
        rstd1 = lax.rsqrt(jnp.mean(x1 * x1, axis=-1, keepdims=True) + EPS)
        r = (x1 * rstd1 * pg_ref[...]).astype(BF16)
        r_ref[...] = r
        gpre = _dot(r, w_gate[...])
        gpre_ref[...] = gpre.astype(BF16)
        pp = _dot_nt(p_ref[...].astype(BF16), wpt_ref[...])
        x2_ref[...] = x1 + _sigmoid(gpre) * pp

        if gathers:
            @pl.when(i == nt - 1)
            def _():
                ag.finish()

    def tok(width):
        return pl.BlockSpec((tm, width), lambda i: (i, 0))

    def whole(shape):
        return pl.BlockSpec(shape, lambda i: (0,) * len(shape))

    hbm = pl.BlockSpec(memory_space=pl.ANY)
    operands = [x, p_l, wg, w_proj_t, conv_k, norm_g, ln_g, ln_b, w_mix, b_mix, ple_g]
    in_specs = [
        tok(D_MODEL), tok(PLE_DIM), hbm,
        whole((D_MODEL, PLE_DIM)), whole((8, WIDTH_B)), whole((1, D_MODEL)), whole((1, WIDTH_A)), whole((1, WIDTH_A)),
        whole((HEADS_A, CHUNK, CHUNK)), whole((HEADS_A, CHUNK, HEAD_DIM)), whole((1, D_MODEL)),
    ]
    out_specs = [tok(PROJ_WIDTH), tok(D_MODEL), tok(D_MODEL), tok(D_MODEL), tok(D_MODEL), tok(D_MODEL), tok(D_MODEL)]
    out_shape = [
        jax.ShapeDtypeStruct((t, PROJ_WIDTH), BF16),
        jax.ShapeDtypeStruct((t, D_MODEL), BF16),
        jax.ShapeDtypeStruct((t, D_MODEL), BF16),
        jax.ShapeDtypeStruct((t, D_MODEL), BF16),
        jax.ShapeDtypeStruct((t, D_MODEL), BF16),
        jax.ShapeDtypeStruct((t, D_MODEL), F32),
        jax.ShapeDtypeStruct((t, D_MODEL), F32),
    ]
    scratch_shapes = [
        pltpu.VMEM((PROJ_WIDTH, D_MODEL), BF16),
        pltpu.VMEM((D_MODEL, D_MODEL), BF16),
        pltpu.VMEM((D_MODEL, D_MODEL), BF16),
        pltpu.VMEM((tm, WIDTH_A), BF16),
        pltpu.VMEM((tm, WIDTH_A), F32),
        pltpu.VMEM((8, WIDTH_B), F32),
        pltpu.SemaphoreType.DMA((3 * N_DEV,)),
    ]
    if gathers:
        operands.append(next_shard)
        in_specs.append(hbm)
        out_specs.append(hbm)
        out_shape.append(jax.ShapeDtypeStruct((N_DEV * ROWS_LAYER, D_MODEL), BF16))
        scratch_shapes += list(GATHER_SEMS)

    return pl.pallas_call(
        body,
        name=f"layer{layer}_forward",
        grid=(nt,),
        in_specs=in_specs,
        out_specs=out_specs,
        out_shape=out_shape,
        scratch_shapes=scratch_shapes,
        compiler_params=pltpu.CompilerParams(dimension_semantics=("arbitrary",), vmem_limit_bytes=56 * MIB),
    )(*operands)


def _loss_head(x2, target, final_g):
    t = x2.shape[0]
    tm = _tile(t, 512)
    nt = t // tm

    def body(x_ref, tgt_ref, g_ref, dx_ref, loss_ref, dg_ref, loss_acc, dg_acc):
        i = pl.program_id(0)

        @pl.when(i == 0)
        def _():
            loss_acc[...] = jnp.zeros_like(loss_acc)
            dg_acc[...] = jnp.zeros_like(dg_acc)

        xv = x_ref[...]
        g = g_ref[...]
        rstd = lax.rsqrt(jnp.mean(xv * xv, axis=-1, keepdims=True) + EPS)
        xhat = xv * rstd
        err = xhat * g - tgt_ref[...]
        loss_acc[...] += _colsum8(err * err)
        dy = err * (1.0 / D_MODEL)
        dg_acc[...] += _colsum8(dy * xhat)
        dxhat = dy * g
        dx_ref[...] = rstd * (dxhat - xhat * jnp.mean(dxhat * xhat, axis=-1, keepdims=True))

        @pl.when(i == nt - 1)
        def _():
            total = jnp.sum(loss_acc[...]) * (0.5 / D_MODEL)
            r8 = lax.broadcasted_iota(jnp.int32, (8, 128), 0)
            c8 = lax.broadcasted_iota(jnp.int32, (8, 128), 1)
            loss_ref[...] = jnp.where((r8 == 0) & (c8 == 0), total, 0.0)
            dg_ref[...] = jnp.sum(dg_acc[...], axis=0, keepdims=True)

    return pl.pallas_call(
        body,
        name="loss_head",
        grid=(nt,),
        in_specs=[
            pl.BlockSpec((tm, D_MODEL), lambda i: (i, 0)),
            pl.BlockSpec((tm, D_MODEL), lambda i: (i, 0)),
            pl.BlockSpec((1, D_MODEL), lambda i: (0, 0)),
        ],
        out_specs=[
            pl.BlockSpec((tm, D_MODEL), lambda i: (i, 0)),
            pl.BlockSpec((8, 128), lambda i: (0, 0)),
            pl.BlockSpec((1, D_MODEL), lambda i: (0, 0)),
        ],
        out_shape=[
            jax.ShapeDtypeStruct((t, D_MODEL), F32),
            jax.ShapeDtypeStruct((8, 128), F32),
            jax.ShapeDtypeStruct((1, D_MODEL), F32),
        ],
        scratch_shapes=[pltpu.VMEM((8, D_MODEL), F32), pltpu.VMEM((8, D_MODEL), F32)],
        compiler_params=pltpu.CompilerParams(dimension_semantics=("arbitrary",), vmem_limit_bytes=32 * MIB),
    )(x2, target, final_g)


class _DirectScatter:
    def __init__(self, pack_ref, pieces_ref, send_sems, recv_sems, local_sem):
        x, y, c = lax.axis_index("x"), lax.axis_index("y"), lax.axis_index("c")
        me = 4 * x + 2 * y + c
        self.copies = []
        for k in range(N_DEV - 1):
            fx, fy, fc = ((k + 1) >> 2) & 1, ((k + 1) >> 1) & 1, (k + 1) & 1
            tx, ty, tc = x ^ fx, y ^ fy, c ^ fc
            self.copies.append(
                pltpu.make_async_remote_copy(
                    src_ref=pack_ref.at[4 * tx + 2 * ty + tc], dst_ref=pieces_ref.at[me],
                    send_sem=send_sems.at[k], recv_sem=recv_sems.at[k],
                    device_id=(tx, ty, tc), device_id_type=MESH,
                )
            )
        self.mine = pltpu.make_async_copy(pack_ref.at[me], pieces_ref.at[me], local_sem)

    def start(self):
        self.mine.start()
        for cp in self.copies:
            cp.start()

    def finish(self):
        for cp in self.copies:
            cp.wait_recv()
        for cp in self.copies:
            cp.wait_send()
        self.mine.wait()


SCATTER_SEMS = [pltpu.SemaphoreType.DMA((N_DEV - 1,)), pltpu.SemaphoreType.DMA((N_DEV - 1,)), pltpu.SemaphoreType.DMA]


def _backward_layer(layer, dx2, x_in, x1, proj, gpre, p_l, wg, w_proj_t, conv_k, norm_g, ln_g, ln_b,
                    w_mix, w_mix_t, b_mix, ple_g, scatter_pack=None):
    t = x_in.shape[0]
    tm = _tile(t, 256)
    nt = t // tm
    n_chunks = tm // CHUNK
    halo_rows = 16
    scatters = scatter_pack is not None

    def body(*refs):
        (dx2_ref, xin_ref, x1_ref, proj_ref, halo_ref, gpre_ref, p_ref, wg_ref, wpt_ref, cw_ref,
         ng_ref, lng_ref, lnb_ref, wm_ref, wmt_ref, bm_ref, pg_ref) = refs[:17]
        refs = refs[17:]
        if scatters:
            pack_ref, refs = refs[0], refs[1:]
        (dxin_ref, dproj_ref, dx1_ref, dgpre_ref, dpp_ref,
         dng_ref, dpg_ref, dlng_ref, dlnb_ref, dws_ref, dbm_ref, dcw_ref) = refs[:12]
        refs = refs[12:]
        if scatters:
            pieces_ref, refs = refs[0], refs[1:]
        (w_in_t, w_out, w_gate, vln_s, mixed_s, dmix_s, dvln_s, carry_s,
         ng_acc, pg_acc, lng_acc, lnb_acc, cw_acc, sems) = refs[:14]
        i = pl.program_id(0)
        tile = nt - 1 - i
        if scatters:
            scatter = _DirectScatter(pack_ref, pieces_ref, *refs[14:17])

            @pl.when(i == 0)
            def _():
                scatter.start()

        @pl.when(i == 0)
        def _():
            copies = _weight_copies(wg_ref, w_in_t, w_out, w_gate, sems)
            for cp in copies:
                cp.start()
            carry_s[...] = jnp.zeros_like(carry_s)
            ng_acc[...] = jnp.zeros_like(ng_acc)
            pg_acc[...] = jnp.zeros_like(pg_acc)
            lng_acc[...] = jnp.zeros_like(lng_acc)
            lnb_acc[...] = jnp.zeros_like(lnb_acc)
            cw_acc[...] = jnp.zeros_like(cw_acc)
            dws_ref[...] = jnp.zeros_like(dws_ref)
            dbm_ref[...] = jnp.zeros_like(dbm_ref)
            for cp in copies:
                cp.wait()

        dx2v = dx2_ref[...]
        gate = _sigmoid(gpre_ref[...].astype(F32))
        pp = _dot_nt(p_ref[...].astype(BF16), wpt_ref[...])
        dpp_ref[...] = (dx2v * gate).astype(BF16)
        dgpre = (dx2v * pp * gate * (1.0 - gate)).astype(BF16)
        dgpre_ref[...] = dgpre
        dr = _dot_nt(dgpre, w_gate[...])
        x1v = x1_ref[...]
        rstd1 = lax.rsqrt(jnp.mean(x1v * x1v, axis=-1, keepdims=True) + EPS)
        xhat1 = x1v * rstd1
        pg_acc[...] += _colsum8(dr * xhat1)
        dxh = dr * pg_ref[...]
        dx1 = dx2v + rstd1 * (dxh - xhat1 * jnp.mean(dxh * xhat1, axis=-1, keepdims=True))
        dx1b = dx1.astype(BF16)
        dx1_ref[...] = dx1b

        dcat = _dot_nt(dx1b, w_out[...])
        dca = dcat[:, 0:512]
        dcb = dcat[:, 512:1024]

        u = proj_ref[:, 0:512].astype(F32)
        v = proj_ref[:, 512:1024].astype(F32)
        za = proj_ref[:, 1024:1536].astype(F32)
        mu = jnp.mean(v, axis=-1, keepdims=True)
        vc = v - mu
        var = jnp.mean(vc * vc, axis=-1, keepdims=True)
        rs = lax.rsqrt(var + EPS)
        vhat = vc * rs
        lng = lng_ref[...]
        vln_s[...] = (vhat * lng + lnb_ref[...]).astype(BF16)
        for ci in range(n_chunks):
            rows = pl.ds(ci * CHUNK, CHUNK)
            for h in range(HEADS_A):
                cols = pl.ds(h * HEAD_DIM, HEAD_DIM)
                mixed_s[rows, cols] = _dot(wm_ref[h], vln_s[rows, cols]) + bm_ref[h]
        mixed = mixed_s[...]
        sga = _sigmoid(za)
        sa = za * sga
        dsa = sga * (1.0 + za * (1.0 - sga))
        dproj_ref[:, 0:512] = (dca * mixed * sa).astype(BF16)
        dmix = dca * u * sa
        dproj_ref[:, 1024:1536] = (dca * u * mixed * dsa).astype(BF16)
        dmix_s[...] = dmix.astype(BF16)
        dbm_acc = jnp.zeros((CHUNK, WIDTH_A), F32)
        for ci in range(n_chunks):
            rows = pl.ds(ci * CHUNK, CHUNK)
            dbm_acc = dbm_acc + dmix[ci * CHUNK:(ci + 1) * CHUNK, :]
            for h in range(HEADS_A):
                cols = pl.ds(h * HEAD_DIM, HEAD_DIM)
                dvln_s[rows, cols] = _dot(wmt_ref[h], dmix_s[rows, cols])
                dws_ref[h] += _dot_nt(dmix_s[rows, cols], vln_s[rows, cols])
        dbm_ref[...] += dbm_acc
        dvln = dvln_s[...]
        lng_acc[...] += _colsum8(dvln * vhat)
        lnb_acc[...] += _colsum8(dvln)
        dvh = dvln * lng
        dv = rs * (dvh - jnp.mean(dvh, axis=-1, keepdims=True) - vhat * jnp.mean(dvh * vhat, axis=-1, keepdims=True))
        dproj_ref[:, 512:1024] = dv.astype(BF16)

        hb = proj_ref[:, 1536:2048].astype(F32)
        gb = proj_ref[:, 2048:2560].astype(F32)
        gc = proj_ref[:, 2560:3072].astype(F32)
        zb = proj_ref[:, 3072:3584].astype(F32)
        xc = gc * hb
        prev = halo_ref[:, 2560:3072].astype(F32) * halo_ref[:, 1536:2048].astype(F32)
        prev = jnp.where(tile > 0, prev, 0.0)
        row = lax.broadcasted_iota(jnp.int32, (tm, WIDTH_B), 0)
        p1 = prev[halo_rows - 1:halo_rows, :]
        p2 = prev[halo_rows - 2:halo_rows - 1, :]
        xc_m1 = jnp.where(row == 0, p1, pltpu.roll(xc, 1, 0))
        xc_m2 = jnp.where(row == 0, p2, jnp.where(row == 1, p1, pltpu.roll(xc, 2, 0)))
        cw = cw_ref[...]
        yc = cw[0:1, :] * xc_m2 + cw[1:2, :] * xc_m1 + cw[2:3, :] * xc
        sgb = _sigmoid(zb)
        sb = zb * sgb
        dsb = sgb * (1.0 + zb * (1.0 - sgb))
        dproj_ref[:, 2048:2560] = (dcb * yc * sb).astype(BF16)
        dyc = dcb * gb * sb
        dproj_ref[:, 3072:3584] = (dcb * gb * yc * dsb).astype(BF16)
        nxt = carry_s[...]
        dyc_p1 = jnp.where(row == tm - 1, nxt[0:1, :], pltpu.roll(dyc, tm - 1, 0))
        dyc_p2 = jnp.where(row == tm - 1, nxt[1:2, :], jnp.where(row == tm - 2, nxt[0:1, :], pltpu.roll(dyc, tm - 2, 0)))
        carry_s[...] = dyc[0:8, :]
        dxc = cw[2:3, :] * dyc + cw[1:2, :] * dyc_p1 + cw[0:1, :] * dyc_p2
        cw_acc[0] += _colsum8(dyc * xc_m2)
        cw_acc[1] += _colsum8(dyc * xc_m1)
        cw_acc[2] += _colsum8(dyc * xc)
        dproj_ref[:, 1536:2048] = (dxc * gc).astype(BF16)
        dproj_ref[:, 2560:3072] = (dxc * hb).astype(BF16)

        dhn = _dot(dproj_ref[...], w_in_t[...])
        xv = xin_ref[...]
        rstd0 = lax.rsqrt(jnp.mean(xv * xv, axis=-1, keepdims=True) + EPS)
        xhat0 = xv * rstd0
        ng_acc[...] += _colsum8(dhn * xhat0)
        dxh0 = dhn * ng_ref[...]
        dxin_ref[...] = dx1 + rstd0 * (dxh0 - xhat0 * jnp.mean(dxh0 * xhat0, axis=-1, keepdims=True))

        @pl.when(i == nt - 1)
        def _():
            dng_ref[...] = jnp.sum(ng_acc[...], axis=0, keepdims=True)
            dpg_ref[...] = jnp.sum(pg_acc[...], axis=0, keepdims=True)
            dlng_ref[...] = jnp.sum(lng_acc[...], axis=0, keepdims=True)
            dlnb_ref[...] = jnp.sum(lnb_acc[...], axis=0, keepdims=True)
            for k in range(3):
                dcw_ref[k:k + 1, :] = jnp.sum(cw_acc[k], axis=0, keepdims=True)
            dcw_ref[3:8, :] = jnp.zeros((5, WIDTH_B), F32)
            if scatters:
                scatter.finish()

    def tok(width):
        return pl.BlockSpec((tm, width), lambda i: (nt - 1 - i, 0))

    def whole(shape):
        return pl.BlockSpec(shape, lambda i: (0,) * len(shape))

    halo_spec = pl.BlockSpec(
        (halo_rows, PROJ_WIDTH), lambda i: (jnp.maximum((nt - 1 - i) * (tm // halo_rows) - 1, 0), 0)
    )
    hbm = pl.BlockSpec(memory_space=pl.ANY)
    operands = [dx2, x_in, x1, proj, proj, gpre, p_l, wg, w_proj_t, conv_k, norm_g, ln_g, ln_b, w_mix, w_mix_t, b_mix, ple_g]
    in_specs = [
        tok(D_MODEL), tok(D_MODEL), tok(D_MODEL), tok(PROJ_WIDTH), halo_spec, tok(D_MODEL), tok(PLE_DIM), hbm,
        whole((D_MODEL, PLE_DIM)), whole((8, WIDTH_B)), whole((1, D_MODEL)), whole((1, WIDTH_A)), whole((1, WIDTH_A)),
        whole((HEADS_A, CHUNK, CHUNK)), whole((HEADS_A, CHUNK, CHUNK)), whole((HEADS_A, CHUNK, HEAD_DIM)),
        whole((1, D_MODEL)),
    ]
    out_specs = [
        tok(D_MODEL), tok(PROJ_WIDTH), tok(D_MODEL), tok(D_MODEL), tok(D_MODEL),
        whole((1, D_MODEL)), whole((1, D_MODEL)), whole((1, WIDTH_A)), whole((1, WIDTH_A)),
        whole((HEADS_A, CHUNK, CHUNK)), whole((CHUNK, WIDTH_A)), whole((8, WIDTH_B)),
    ]
    out_shape = [
        jax.ShapeDtypeStruct((t, D_MODEL), F32),
        jax.ShapeDtypeStruct((t, PROJ_WIDTH), BF16),
        jax.ShapeDtypeStruct((t, D_MODEL), BF16),
        jax.ShapeDtypeStruct((t, D_MODEL), BF16),
        jax.ShapeDtypeStruct((t, D_MODEL), BF16),
        jax.ShapeDtypeStruct((1, D_MODEL), F32),
        jax.ShapeDtypeStruct((1, D_MODEL), F32),
        jax.ShapeDtypeStruct((1, WIDTH_A), F32),
        jax.ShapeDtypeStruct((1, WIDTH_A), F32),
        jax.ShapeDtypeStruct((HEADS_A, CHUNK, CHUNK), F32),
        jax.ShapeDtypeStruct((CHUNK, WIDTH_A), F32),
        jax.ShapeDtypeStruct((8, WIDTH_B), F32),
    ]
    scratch_shapes = [
        pltpu.VMEM((PROJ_WIDTH, D_MODEL), BF16),
        pltpu.VMEM((D_MODEL, D_MODEL), BF16),
        pltpu.VMEM((D_MODEL, D_MODEL), BF16),
        pltpu.VMEM((tm, WIDTH_A), BF16),
        pltpu.VMEM((tm, WIDTH_A), F32),
        pltpu.VMEM((tm, WIDTH_A), BF16),
        pltpu.VMEM((tm, WIDTH_A), F32),
        pltpu.VMEM((8, WIDTH_B), F32),
        pltpu.VMEM((8, D_MODEL), F32),
        pltpu.VMEM((8, D_MODEL), F32),
        pltpu.VMEM((8, WIDTH_A), F32),
        pltpu.VMEM((8, WIDTH_A), F32),
        pltpu.VMEM((3, 8, WIDTH_B), F32),
        pltpu.SemaphoreType.DMA((3 * N_DEV,)),
    ]
    if scatters:
        operands.append(scatter_pack)
        in_specs.append(hbm)
        out_specs.append(hbm)
        out_shape.append(jax.ShapeDtypeStruct(scatter_pack.shape, scatter_pack.dtype))
        scratch_shapes += list(SCATTER_SEMS)

    return pl.pallas_call(
        body,
        name=f"layer{layer}_backward",
        grid=(nt,),
        in_specs=in_specs,
        out_specs=out_specs,
        out_shape=out_shape,
        scratch_shapes=scratch_shapes,
        compiler_params=pltpu.CompilerParams(dimension_semantics=("arbitrary",), vmem_limit_bytes=56 * MIB),
    )(*operands)


def _sum_pieces(layer, pieces):
    rows, n = pieces.shape[1], pieces.shape[2]

    def body(p_ref, out_ref):
        total = p_ref[0].astype(F32)
        for j in range(1, N_DEV):
            total = total + p_ref[j].astype(F32)
        out_ref[...] = total

    return pl.pallas_call(
        body,
        name=f"layer{layer}_grad_sum",
        out_shape=jax.ShapeDtypeStruct((rows, n), F32),
        in_specs=[pl.BlockSpec(memory_space=pltpu.VMEM)],
        out_specs=pl.BlockSpec(memory_space=pltpu.VMEM),
        compiler_params=pltpu.CompilerParams(vmem_limit_bytes=32 * MIB),
    )(pieces)


def _weight_grads(layer, dproj, hn, cat, dx1, r, dgpre, dpp, p_l):
    t = hn.shape[0]
    tk = _tile(t, 512)
    nt = t // tk
    in_blocks = PROJ_WIDTH // 512

    def body(dproj_ref, hn_ref, cat_ref, dx1_ref, r_ref, dgpre_ref, dpp_ref, p_ref, pack_ref,
             acc_in, acc_out, acc_gate, acc_proj, stage, sems):
        i = pl.program_id(0)

        @pl.when(i == 0)
        def _():
            acc_in[...] = jnp.zeros_like(acc_in)
            acc_out[...] = jnp.zeros_like(acc_out)
            acc_gate[...] = jnp.zeros_like(acc_gate)
            acc_proj[...] = jnp.zeros_like(acc_proj)

        hnv = hn_ref[...]
        for b in range(in_blocks):
            acc_in[pl.ds(b * 512, 512), :] += _dot_tn(dproj_ref[:, b * 512:(b + 1) * 512], hnv)
        dx1v = dx1_ref[...]
        dgv = dgpre_ref[...]
        for b in range(D_MODEL // 512):
            acc_out[pl.ds(b * 512, 512), :] += _dot_tn(cat_ref[:, b * 512:(b + 1) * 512], dx1v)
            acc_gate[pl.ds(b * 512, 512), :] += _dot_tn(r_ref[:, b * 512:(b + 1) * 512], dgv)
        pv = p_ref[...].astype(BF16)
        for b in range(D_MODEL // 512):
            acc_proj[pl.ds(b * 512, 512), :] += _dot_tn(dpp_ref[:, b * 512:(b + 1) * 512], pv)

        @pl.when(i == nt - 1)
        def _():
            def out_copy(s):
                return pltpu.make_async_copy(stage.at[s % 2], pack_ref.at[s], sems.at[s % 2])

            for s in range(N_DEV):
                if s >= 2:
                    out_copy(s - 2).wait()
                buf = stage.at[s % 2]
                buf[pl.ds(OFF_IN, ROWS_IN), :] = acc_in[pl.ds(s * ROWS_IN, ROWS_IN), :].astype(BF16)
                buf[pl.ds(OFF_OUT, ROWS_OUT), :] = acc_out[pl.ds(s * ROWS_OUT, ROWS_OUT), :].astype(BF16)
                buf[pl.ds(OFF_GATE, ROWS_GATE), :] = acc_gate[pl.ds(s * ROWS_GATE, ROWS_GATE), :].astype(BF16)
                for j in range(D_MODEL // PLE_DIM):
                    buf[pl.ds(OFF_PROJ, ROWS_PROJ), pl.ds(j * PLE_DIM, PLE_DIM)] = acc_proj[
                        pl.ds(s * ROWS_OUT + j * ROWS_PROJ, ROWS_PROJ), :
                    ].astype(BF16)
                out_copy(s).start()
            out_copy(N_DEV - 2).wait()
            out_copy(N_DEV - 1).wait()

    def tok(width):
        return pl.BlockSpec((tk, width), lambda i: (i, 0))

    return pl.pallas_call(
        body,
        name=f"layer{layer}_weight_grads",
        grid=(nt,),
        in_specs=[tok(PROJ_WIDTH), tok(D_MODEL), tok(D_MODEL), tok(D_MODEL), tok(D_MODEL), tok(D_MODEL), tok(D_MODEL), tok(PLE_DIM)],
        out_specs=pl.BlockSpec(memory_space=pl.ANY),
        out_shape=jax.ShapeDtypeStruct((N_DEV, ROWS_GRAD, D_MODEL), BF16),
        scratch_shapes=[
            pltpu.VMEM((PROJ_WIDTH, D_MODEL), F32),
            pltpu.VMEM((D_MODEL, D_MODEL), F32),
            pltpu.VMEM((D_MODEL, D_MODEL), F32),
            pltpu.VMEM((D_MODEL, PLE_DIM), F32),
            pltpu.VMEM((2, ROWS_GRAD, D_MODEL), BF16),
            pltpu.SemaphoreType.DMA((2,)),
        ],
        compiler_params=pltpu.CompilerParams(dimension_semantics=("arbitrary",), vmem_limit_bytes=58 * MIB),
    )(dproj, hn, cat, dx1, r, dgpre, dpp, p_l)


def _reduce_scatter(layer, pack):
    rows, n = pack.shape[1], pack.shape[2]

    def body(g_ref, out_ref, r1, a_s, r2, send1, recv1, send2, recv2):
        x, y, c = lax.axis_index("x"), lax.axis_index("y"), lax.axis_index("c")
        sibling = (x, y, 1 - c)
        chip = 2 * x + y

        def to_sibling(j):
            return pltpu.make_async_remote_copy(
                src_ref=g_ref.at[2 * j + 1 - c], dst_ref=r1.at[j], send_sem=send1.at[j], recv_sem=recv1.at[j],
                device_id=sibling, device_id_type=MESH,
            )

        first = [to_sibling(j) for j in range(4)]
        for cp in first:
            cp.start()
        for j in range(4):
            first[j].wait_recv()
            a_s[j] = (g_ref[2 * j + c].astype(F32) + r1[j].astype(F32)).astype(BF16)

        flips = [(1, 0), (0, 1), (1, 1)]

        def to_chip(k):
            fx, fy = flips[k]
            tx, ty = x ^ fx, y ^ fy
            return pltpu.make_async_remote_copy(
                src_ref=a_s.at[2 * tx + ty], dst_ref=r2.at[k], send_sem=send2.at[k], recv_sem=recv2.at[k],
                device_id=(tx, ty, c), device_id_type=MESH,
            )

        second = [to_chip(k) for k in range(3)]
        for cp in second:
            cp.start()
        total = g_ref[2 * chip + c].astype(F32) + r1[chip].astype(F32)
        for k in range(3):
            second[k].wait_recv()
            total = total + r2[k].astype(F32)
        out_ref[...] = total
        for cp in first + second:
            cp.wait_send()

    return pl.pallas_call(
        body,
        name=f"layer{layer}_grad_reduce_scatter",
        out_shape=jax.ShapeDtypeStruct((rows, n), F32),
        in_specs=[pl.BlockSpec(memory_space=pltpu.VMEM)],
        out_specs=pl.BlockSpec(memory_space=pltpu.VMEM),
        scratch_shapes=[
            pltpu.VMEM((4, rows, n), BF16),
            pltpu.VMEM((4, rows, n), BF16),
            pltpu.VMEM((3, rows, n), BF16),
            pltpu.SemaphoreType.DMA((4,)),
            pltpu.SemaphoreType.DMA((4,)),
            pltpu.SemaphoreType.DMA((3,)),
            pltpu.SemaphoreType.DMA((3,)),
        ],
        compiler_params=pltpu.CompilerParams(vmem_limit_bytes=48 * MIB),
    )(pack)


def _all_reduce_small(part):
    rows, n = part.shape

    def body(p_ref, out_ref, r1, q, send1, recv1, send2, recv2):
        x, y, c = lax.axis_index("x"), lax.axis_index("y"), lax.axis_index("c")
        chip = 2 * x + y
        pair = pltpu.make_async_remote_copy(
            src_ref=p_ref, dst_ref=r1, send_sem=send1, recv_sem=recv1, device_id=(x, y, 1 - c), device_id_type=MESH
        )
        pair.start()
        pair.wait_recv()
        q[chip] = p_ref[...] + r1[...]

        flips = [(1, 0), (0, 1), (1, 1)]
        second = []
        for k, (fx, fy) in enumerate(flips):
            second.append(
                pltpu.make_async_remote_copy(
                    src_ref=q.at[chip], dst_ref=q.at[chip], send_sem=send2.at[k], recv_sem=recv2.at[k],
                    device_id=(x ^ fx, y ^ fy, c), device_id_type=MESH,
                )
            )
        for cp in second:
            cp.start()
        for cp in second:
            cp.wait_recv()
        out_ref[...] = ((q[0] + q[1]) + q[2]) + q[3]
        pair.wait_send()
        for cp in second:
            cp.wait_send()

    return pl.pallas_call(
        body,
        name="small_grads_all_reduce",
        out_shape=jax.ShapeDtypeStruct((rows, n), F32),
        in_specs=[pl.BlockSpec(memory_space=pltpu.VMEM)],
        out_specs=pl.BlockSpec(memory_space=pltpu.VMEM),
        scratch_shapes=[
            pltpu.VMEM((rows, n), F32),
            pltpu.VMEM((4, rows, n), F32),
            pltpu.SemaphoreType.DMA,
            pltpu.SemaphoreType.DMA,
            pltpu.SemaphoreType.DMA((3,)),
            pltpu.SemaphoreType.DMA((3,)),
        ],
    )(part)


def _adamw(name, groups):
    n = len(groups)
    bc1 = 1.0 - ADAM_B1 ** ADAM_STEP
    bc2 = 1.0 - ADAM_B2 ** ADAM_STEP

    def body(*refs):
        ins, outs = refs[: 4 * n], refs[4 * n:]
        for k in range(n):
            w_ref, g_ref, m_ref, v_ref = ins[4 * k: 4 * k + 4]
            d_ref, nm_ref, nv_ref = outs[3 * k: 3 * k + 3]
            g = g_ref[...]
            m = ADAM_B1 * m_ref[...] + (1.0 - ADAM_B1) * g
            v = ADAM_B2 * v_ref[...] + (1.0 - ADAM_B2) * (g * g)
            m_hat = m / bc1
            v_hat = v / bc2
            d_ref[...] = -ADAM_LR * (m_hat / (jnp.sqrt(v_hat) + ADAM_EPS) + ADAM_WD * w_ref[...])
            nm_ref[...] = m
            nv_ref[...] = v

    flat = [a for grp in groups for a in grp]
    out_shape = []
    for w, _, _, _ in groups:
        out_shape += [jax.ShapeDtypeStruct(w.shape, F32)] * 3
    outs = pl.pallas_call(
        body,
        name=name,
        out_shape=out_shape,
        in_specs=[pl.BlockSpec(memory_space=pltpu.VMEM)] * len(flat),
        out_specs=[pl.BlockSpec(memory_space=pltpu.VMEM)] * len(out_shape),
        compiler_params=pltpu.CompilerParams(vmem_limit_bytes=48 * MIB),
    )(*flat)
    return [tuple(outs[3 * k: 3 * k + 3]) for k in range(n)]


def _rows128(a):
    flat = a.reshape(-1)
    pad = (-flat.shape[0]) % 1024
    if pad:
        flat = jnp.concatenate([flat, jnp.zeros((pad,), flat.dtype)])
    return flat.reshape(-1, 128)


SMALL_FIELDS = ("w_s", "norm_g", "ple_norm_g", "ln_v_g", "ln_v_b", "b_s", "final_g", "conv_w")


def _pack_small(parts):
    return jnp.concatenate([_rows128(parts[k]) for k in SMALL_FIELDS], axis=0)


def _unpack_small(packed, shapes):
    out, row = {}, 0
    for k in SMALL_FIELDS:
        size = 1
        for d in shapes[k]:
            size *= d
        nrows = -(-size // 1024) * 8
        out[k] = packed[row:row + nrows].reshape(-1)[:size].reshape(shapes[k])
        row += nrows
    return out, row


def _split3_bf16(a):
    b1 = a.astype(BF16)
    r1 = a - b1.astype(F32)
    b2 = r1.astype(BF16)
    b3 = (r1 - b2.astype(F32)).astype(BF16)
    return b1, b2, b3


def _pack_weight_shard(w_in_l, w_out_l, w_gate_l, w_proj_l, conv_w_l):
    w_in_t = jnp.transpose(w_in_l).astype(BF16)
    proj_t = jnp.transpose(w_proj_l).astype(BF16)
    proj_rows = proj_t.reshape(D_MODEL // PLE_DIM, ROWS_PROJ, PLE_DIM).transpose(1, 0, 2).reshape(ROWS_PROJ, D_MODEL)
    conv_parts = jnp.concatenate([b.reshape(-1) for b in _split3_bf16(conv_w_l)])
    conv_rows = jnp.concatenate([conv_parts, jnp.zeros((ROWS_CONV * D_MODEL - conv_parts.shape[0],), BF16)])
    return jnp.concatenate(
        [w_in_t, w_out_l.astype(BF16), w_gate_l.astype(BF16), proj_rows, conv_rows.reshape(ROWS_CONV, D_MODEL)], axis=0
    )


def _unpack_proj_conv(wg):
    per_dev = wg.reshape(N_DEV, ROWS_LAYER, D_MODEL)
    proj_rows = per_dev[:, OFF_PROJ:OFF_PROJ + ROWS_PROJ]
    proj_t = proj_rows.reshape(N_DEV, ROWS_PROJ, D_MODEL // PLE_DIM, PLE_DIM).transpose(0, 2, 1, 3).reshape(D_MODEL, PLE_DIM)
    n_conv = (WIDTH_B // N_DEV) * 3
    conv_parts = per_dev[:, OFF_CONV].astype(F32)[:, :3 * n_conv].reshape(N_DEV, 3, n_conv)
    conv = (conv_parts[:, 0] + conv_parts[:, 1]) + conv_parts[:, 2]
    conv_k = jnp.transpose(conv.reshape(WIDTH_B, 3))
    conv_k = jnp.concatenate([conv_k, jnp.zeros((5, WIDTH_B), F32)], axis=0)
    return proj_t, conv_k


def _unpack_grad_shard(red):
    g_in = jnp.transpose(red[OFF_IN:OFF_IN + ROWS_IN])
    g_out = red[OFF_OUT:OFF_OUT + ROWS_OUT]
    g_gate = red[OFF_GATE:OFF_GATE + ROWS_GATE]
    proj_rows = red[OFF_PROJ:OFF_PROJ + ROWS_PROJ]
    proj_t = proj_rows.reshape(ROWS_PROJ, D_MODEL // PLE_DIM, PLE_DIM).transpose(1, 0, 2).reshape(ROWS_OUT, PLE_DIM)
    return g_in, g_out, g_gate, jnp.transpose(proj_t)


def kernel(x, p, norm_g, w_in, ln_v_g, ln_v_b, w_s, b_s, conv_w, w_out, ple_norm_g, w_ple_gate, w_ple_proj, final_g, loss_target, m_norm_g, m_w_in, m_ln_v_g, m_ln_v_b, m_w_s, m_b_s, m_conv_w, m_w_out, m_ple_norm_g, m_w_ple_gate, m_w_ple_proj, m_final_g, v_norm_g, v_w_in, v_ln_v_g, v_ln_v_b, v_w_s, v_b_s, v_conv_w, v_w_out, v_ple_norm_g, v_w_ple_gate, v_w_ple_proj, v_final_g):
    me = 4 * lax.axis_index("x") + 2 * lax.axis_index("y") + lax.axis_index("c")
    xs = x[0]
    target = loss_target[0]

    shards = [_pack_weight_shard(w_in[l], w_out[l], w_ple_gate[l], w_ple_proj[l], conv_w[l]) for l in range(DEPTH)]
    tril = jnp.tril(jnp.ones((CHUNK, CHUNK), F32))

    def consts(l, wg_l):
        w_proj_t, conv_k = _unpack_proj_conv(wg_l)
        w_mix = w_s[l] * tril[None]
        return dict(
            wg=wg_l, w_proj_t=w_proj_t, conv_k=conv_k,
            norm_g=norm_g[l].reshape(1, D_MODEL), ln_g=ln_v_g[l].reshape(1, WIDTH_A), ln_b=ln_v_b[l].reshape(1, WIDTH_A),
            w_mix=w_mix.astype(BF16), w_mix_t=jnp.swapaxes(w_mix, 1, 2).astype(BF16),
            b_mix=jnp.broadcast_to(b_s[l][:, :, None], (HEADS_A, CHUNK, HEAD_DIM)),
            ple_g=ple_norm_g[l].reshape(1, D_MODEL),
        )

    layer_consts = [consts(0, _all_gather_rows(shards[0]))]
    saved = []
    h = xs
    for l in range(DEPTH):
        k = layer_consts[l]
        outs = _forward_layer(
            l, h, p[l, 0], k["wg"], k["w_proj_t"], k["conv_k"], k["norm_g"], k["ln_g"], k["ln_b"], k["w_mix"], k["b_mix"],
            k["ple_g"], next_shard=shards[l + 1] if l + 1 < DEPTH else None)
        proj, hn, cat, r, gpre, x1, x2 = outs[:7]
        if l + 1 < DEPTH:
            layer_consts.append(consts(l + 1, outs[7]))
        saved.append(dict(x_in=h, proj=proj, hn=hn, cat=cat, r=r, gpre=gpre, x1=x1))
        h = x2

    dx, loss_tile, d_final = _loss_head(h, target, final_g.reshape(1, D_MODEL))

    small = {}
    reduced = [None] * DEPTH
    pending = None
    for l in reversed(range(DEPTH)):
        k, s = layer_consts[l], saved[l]
        outs = _backward_layer(
            l, dx, s["x_in"], s["x1"], s["proj"], s["gpre"], p[l, 0], k["wg"], k["w_proj_t"], k["conv_k"],
            k["norm_g"], k["ln_g"], k["ln_b"], k["w_mix"], k["w_mix_t"], k["b_mix"], k["ple_g"], scatter_pack=pending)
        (dx, dproj, dx1, dgpre, dpp, d_ng, d_pg, d_lng, d_lnb, d_ws, d_bm, d_cw) = outs[:12]
        if pending is not None:
            reduced[l + 1] = _sum_pieces(l + 1, outs[12])
        pack = _weight_grads(l, dproj, s["hn"], s["cat"], dx1, s["r"], dgpre, dpp, p[l, 0])
        if l > 0:
            pending = pack
        else:
            reduced[l] = _reduce_scatter(l, pack)
        small[l] = dict(
            w_s=d_ws * tril[None], norm_g=d_ng[0], ple_norm_g=d_pg[0], ln_v_g=d_lng[0], ln_v_b=d_lnb[0],
            b_s=jnp.transpose(jnp.sum(d_bm.reshape(CHUNK, HEADS_A, HEAD_DIM), axis=-1)),
            conv_w=jnp.transpose(d_cw[0:3]),
        )
    grad_x = dx[None]

    small_parts = {f: jnp.stack([small[l][f] for l in range(DEPTH)]) for f in SMALL_FIELDS if f != "final_g"}
    small_parts["final_g"] = d_final[0]
    part = jnp.concatenate([_pack_small(small_parts), loss_tile], axis=0)
    total = _all_reduce_small(part)
    small_shapes = {f: small_parts[f].shape for f in SMALL_FIELDS}
    g_small, row = _unpack_small(total, small_shapes)
    loss = total[row, 0]
    g_small["conv_w"] = lax.dynamic_slice_in_dim(g_small["conv_w"], me * (WIDTH_B // N_DEV), WIDTH_B // N_DEV, axis=1)

    per_layer = [_unpack_grad_shard(reduced[l]) for l in range(DEPTH)]
    g_w_in = jnp.stack([per_layer[l][0] for l in range(DEPTH)])
    g_w_out = jnp.stack([per_layer[l][1] for l in range(DEPTH)])
    g_w_gate = jnp.stack([per_layer[l][2] for l in range(DEPTH)])
    g_w_proj = jnp.stack([per_layer[l][3] for l in range(DEPTH)])

    w_small = dict(w_s=w_s, norm_g=norm_g, ple_norm_g=ple_norm_g, ln_v_g=ln_v_g, ln_v_b=ln_v_b, b_s=b_s, final_g=final_g, conv_w=conv_w)
    m_small = dict(w_s=m_w_s, norm_g=m_norm_g, ple_norm_g=m_ple_norm_g, ln_v_g=m_ln_v_g, ln_v_b=m_ln_v_b, b_s=m_b_s, final_g=m_final_g, conv_w=m_conv_w)
    v_small = dict(w_s=v_w_s, norm_g=v_norm_g, ple_norm_g=v_ple_norm_g, ln_v_g=v_ln_v_g, ln_v_b=v_ln_v_b, b_s=v_b_s, final_g=v_final_g, conv_w=v_conv_w)
    (big_in,) = _adamw("adamw_w_in", [(w_in, g_w_in, m_w_in, v_w_in)])
    rest = _adamw("adamw_rest", [
        (w_out, g_w_out, m_w_out, v_w_out),
        (w_ple_gate, g_w_gate, m_w_ple_gate, v_w_ple_gate),
        (w_ple_proj, g_w_proj, m_w_ple_proj, v_w_ple_proj),
        (_pack_small(w_small), _pack_small(g_small), _pack_small(m_small), _pack_small(v_small)),
    ])
    w_shapes = {f: w_small[f].shape for f in SMALL_FIELDS}
    upd_small = [_unpack_small(a, w_shapes)[0] for a in rest[3]]

    grads = dict(g_small, w_in=g_w_in, w_out=g_w_out, w_ple_gate=g_w_gate, w_ple_proj=g_w_proj)
    updates = []
    for j in range(3):
        updates.append(dict(upd_small[j], w_in=big_in[j], w_out=rest[0][j], w_ple_gate=rest[1][j], w_ple_proj=rest[2][j]))
    order = ["norm_g", "w_in", "ln_v_g", "ln_v_b", "w_s", "b_s", "conv_w", "w_out", "ple_norm_g", "w_ple_gate", "w_ple_proj", "final_g"]
    return (loss, grad_x, *[grads[n] for n in order], *[updates[0][n] for n in order],
            *[updates[1][n] for n in order], *[updates[2][n] for n in order])
```

```python
import jax
import jax.numpy as jnp
from jax import lax
from jax.experimental import pallas as pl
from jax.experimental.pallas import tpu as pltpu

F32 = jnp.float32
BF16 = jnp.bfloat16

D_MODEL = 1024
WIDTH_A = 512
WIDTH_B = 512
HEADS_A = 4
HEAD_DIM = 128
CHUNK = 128
PLE_DIM = 256
PROJ_WIDTH = 3584
DEPTH = 2
EPS = 1e-6
N_DEV = 8

ADAM_LR = 0.001
ADAM_B1 = 0.9
ADAM_B2 = 0.999
ADAM_EPS = 1e-08
ADAM_WD = 0.01
ADAM_STEP = 10

ROWS_IN = PROJ_WIDTH // N_DEV
ROWS_OUT = D_MODEL // N_DEV
ROWS_GATE = D_MODEL // N_DEV
ROWS_PROJ = (D_MODEL // N_DEV) * PLE_DIM // D_MODEL
ROWS_CONV = 16
OFF_IN = 0
OFF_OUT = OFF_IN + ROWS_IN
OFF_GATE = OFF_OUT + ROWS_OUT
OFF_PROJ = OFF_GATE + ROWS_GATE
OFF_CONV = OFF_PROJ + ROWS_PROJ
ROWS_GRAD = OFF_CONV
ROWS_LAYER = OFF_CONV + ROWS_CONV

SMALL_ROWS = 8
SMALL_NORM = 0
SMALL_PLE = 1
SMALL_LN = 2
SMALL_BS = 3
SMALL_CONV = 4
HEAD_FINAL = 0
HEAD_LOSS = 1
TOTAL_HEAD = DEPTH * SMALL_ROWS
TOTAL_WS = TOTAL_HEAD + SMALL_ROWS
TOTAL_ROWS = TOTAL_WS + CHUNK

MIB = 1024 * 1024
MESH = pl.DeviceIdType.MESH

NT_DIMS = (((1,), (1,)), ((), ()))
TN_DIMS = (((0,), (0,)), ((), ()))


def _dot(a, b):
    return jnp.dot(a, b, preferred_element_type=F32)


def _dot_nt(a, b):
    return lax.dot_general(a, b, NT_DIMS, preferred_element_type=F32)


def _dot_tn(a, b):
    return lax.dot_general(a, b, TN_DIMS, preferred_element_type=F32)


def _colsum8(a):
    rows, n = a.shape
    return jnp.sum(a.reshape(rows // 8, 8, n), axis=0)


def _sigmoid(z):
    return 1.0 / (1.0 + jnp.exp(-z))


def _tile(t, want):
    return want if t % want == 0 else t


class _TwoLevelGather:
    def __init__(self, x_ref, out_ref, m_per, send_sems, recv_sems, local_sem):
        x, y, c = lax.axis_index("x"), lax.axis_index("y"), lax.axis_index("c")
        self.c = c
        self.me, self.sibling = (x, y, c), (x, y, 1 - c)
        self.chips = [(1 - x, y), (x, 1 - y), (1 - x, 1 - y)]
        self.x_ref, self.out_ref, self.m_per = x_ref, out_ref, m_per
        self.send_sems, self.recv_sems = send_sems, recv_sems
        self.mine = pltpu.make_async_copy(x_ref, self.rows(*self.me), local_sem)

    def rows(self, px, py, pc):
        return self.out_ref.at[pl.ds((4 * px + 2 * py + pc) * self.m_per, self.m_per), :]

    def copy(self, k, block, to, src=None):
        return pltpu.make_async_remote_copy(
            src_ref=self.rows(*block) if src is None else src,
            dst_ref=self.rows(*block),
            send_sem=self.send_sems.at[k],
            recv_sem=self.recv_sems.at[k],
            device_id=to,
            device_id_type=MESH,
        )

    def first(self):
        out = [self.copy(0, self.me, self.sibling, src=self.x_ref)]
        return out + [self.copy(1 + j, self.me, (*chip, self.c), src=self.x_ref) for j, chip in enumerate(self.chips)]

    def passed(self):
        return [self.copy(4 + j, (*chip, self.c), self.sibling) for j, chip in enumerate(self.chips)]

    def start(self):
        self.mine.start()
        for cp in self.first():
            cp.start()

    def pass_on(self):
        passed = self.passed()
        for j, chip in enumerate(self.chips):
            self.copy(1 + j, (*chip, self.c), self.me).wait_recv()
            passed[j].start()

    def finish(self):
        self.copy(0, self.sibling, self.me).wait_recv()
        for j, chip in enumerate(self.chips):
            self.copy(4 + j, (*chip, 1 - self.c), self.me).wait_recv()
        for cp in self.first() + self.passed():
            cp.wait_send()
        self.mine.wait()


GATHER_SEMS = [pltpu.SemaphoreType.DMA((7,)), pltpu.SemaphoreType.DMA((7,)), pltpu.SemaphoreType.DMA]


def _all_gather_rows(shard):
    m_per, n = shard.shape

    def body(x_ref, out_ref, send_sems, recv_sems, local_sem):
        ag = _TwoLevelGather(x_ref, out_ref, m_per, send_sems, recv_sems, local_sem)
        ag.start()
        ag.pass_on()
        ag.finish()

    return pl.pallas_call(
        body,
        name="weights_all_gather",
        out_shape=jax.ShapeDtypeStruct((N_DEV * m_per, n), shard.dtype),
        in_specs=[pl.BlockSpec(memory_space=pltpu.VMEM)],
        out_specs=pl.BlockSpec(memory_space=pltpu.VMEM),
        scratch_shapes=list(GATHER_SEMS),
        compiler_params=pltpu.CompilerParams(vmem_limit_bytes=48 * MIB),
    )(shard)


def _weight_copies(wg_ref, w_in_t, w_out, w_gate, sems):
    copies = []
    k = 0
    for s in range(N_DEV):
        base = s * ROWS_LAYER
        for dst, off, rows in ((w_in_t, OFF_IN, ROWS_IN), (w_out, OFF_OUT, ROWS_OUT), (w_gate, OFF_GATE, ROWS_GATE)):
            copies.append(
                pltpu.make_async_copy(
                    wg_ref.at[pl.ds(base + off, rows), :], dst.at[pl.ds(s * rows, rows), :], sems.at[k]
                )
            )
            k += 1
    return copies


def _forward_layer(layer, x, p_l, wg, w_proj_t, conv_k, norm_g, ln_g, ln_b, w_mix, b_mix, ple_g, next_shard=None):
    t = x.shape[0]
    tm = _tile(t, 256)
    nt = t // tm
    n_chunks = tm // CHUNK
    gathers = next_shard is not None

    def body(*refs):
        (x_ref, p_ref, wg_ref, wpt_ref, cw_ref, ng_ref, lng_ref, lnb_ref, wm_ref, bm_ref, pg_ref) = refs[:11]
        refs = refs[11:]
        if gathers:
            shard_ref, refs = refs[0], refs[1:]
        (proj_ref, hn_ref, cat_ref, r_ref, gpre_ref, x1_ref, x2_ref) = refs[:7]
        refs = refs[7:]
        if gathers:
            gathered_ref, refs = refs[0], refs[1:]
        (w_in_t, w_out, w_gate, vln_s, mixed_s, halo_s, sems) = refs[:7]
        i = pl.program_id(0)
        if gathers:
            ag = _TwoLevelGather(shard_ref, gathered_ref, ROWS_LAYER, *refs[7:10])

            @pl.when(i == 0)
            def _():
                ag.start()

            @pl.when(i == nt // 2)
            def _():
                ag.pass_on()

        @pl.when(i == 0)
        def _():
            copies = _weight_copies(wg_ref, w_in_t, w_out, w_gate, sems)
            for cp in copies:
                cp.start()
            halo_s[...] = jnp.zeros_like(halo_s)
            for cp in copies:
                cp.wait()

        xv = x_ref[...]
        rstd0 = lax.rsqrt(jnp.mean(xv * xv, axis=-1, keepdims=True) + EPS)
        hn = (xv * rstd0 * ng_ref[...]).astype(BF16)
        hn_ref[...] = hn
        proj = _dot_nt(hn, w_in_t[...])
        proj_ref[...] = proj.astype(BF16)

        u = proj[:, 0:512]
        v = proj[:, 512:1024]
        za = proj[:, 1024:1536]
        mu = jnp.mean(v, axis=-1, keepdims=True)
        vc = v - mu
        var = jnp.mean(vc * vc, axis=-1, keepdims=True)
        vln = vc * lax.rsqrt(var + EPS) * lng_ref[...] + lnb_ref[...]
        vln_s[...] = vln.astype(BF16)
        for ci in range(n_chunks):
            rows = pl.ds(ci * CHUNK, CHUNK)
            for h in range(HEADS_A):
                cols = pl.ds(h * HEAD_DIM, HEAD_DIM)
                mixed_s[rows, cols] = _dot(wm_ref[h], vln_s[rows, cols]) + bm_ref[h]
        out_a = u * mixed_s[...] * (za * _sigmoid(za))
        cat_ref[:, 0:512] = out_a.astype(BF16)

        hb = proj[:, 1536:2048]
        gb = proj[:, 2048:2560]
        gc = proj[:, 2560:3072]
        zb = proj[:, 3072:3584]
        xc = gc * hb
        prev = halo_s[...]
        row = lax.broadcasted_iota(jnp.int32, (tm, WIDTH_B), 0)
        xc_m1 = jnp.where(row == 0, prev[7:8, :], pltpu.roll(xc, 1, 0))
        xc_m2 = jnp.where(row == 0, prev[6:7, :], jnp.where(row == 1, prev[7:8, :], pltpu.roll(xc, 2, 0)))
        halo_s[...] = xc[tm - 8:tm, :]
        cw = cw_ref[...]
        yc = cw[0:1, :] * xc_m2 + cw[1:2, :] * xc_m1 + cw[2:3, :] * xc
        out_b = gb * yc * (zb * _sigmoid(zb))
        cat_ref[:, 512:1024] = out_b.astype(BF16)

        x1 = xv + _dot(cat_ref[...], w_out[...])
        x1_ref[...] = x1
        rstd1 = lax.rsqrt(jnp.mean(x1 * x1, axis=-1, keepdims=True) + EPS)
        r = (x1 * rstd1 * pg_ref[...]).astype(BF16)
        r_ref[...] = r
        gpre = _dot(r, w_gate[...])
        gpre_ref[...] = gpre.astype(BF16)
        pp = _dot_nt(p_ref[...].astype(BF16), wpt_ref[...])
        x2_ref[...] = x1 + _sigmoid(gpre) * pp

        if gathers:
            @pl.when(i == nt - 1)
            def _():
                ag.finish()

    def tok(width):
        return pl.BlockSpec((tm, width), lambda i: (i, 0))

    def whole(shape):
        return pl.BlockSpec(shape, lambda i: (0,) * len(shape))

    hbm = pl.BlockSpec(memory_space=pl.ANY)
    operands = [x, p_l, wg, w_proj_t, conv_k, norm_g, ln_g, ln_b, w_mix, b_mix, ple_g]
    in_specs = [
        tok(D_MODEL), tok(PLE_DIM), hbm,
        whole((D_MODEL, PLE_DIM)), whole((8, WIDTH_B)), whole((1, D_MODEL)), whole((1, WIDTH_A)), whole((1, WIDTH_A)),
        whole((HEADS_A, CHUNK, CHUNK)), whole((HEADS_A, CHUNK, HEAD_DIM)), whole((1, D_MODEL)),
    ]
    out_specs = [tok(PROJ_WIDTH), tok(D_MODEL), tok(D_MODEL), tok(D_MODEL), tok(D_MODEL), tok(D_MODEL), tok(D_MODEL)]
    out_shape = [
        jax.ShapeDtypeStruct((t, PROJ_WIDTH), BF16),
        jax.ShapeDtypeStruct((t, D_MODEL), BF16),
        jax.ShapeDtypeStruct((t, D_MODEL), BF16),
        jax.ShapeDtypeStruct((t, D_MODEL), BF16),
        jax.ShapeDtypeStruct((t, D_MODEL), BF16),
        jax.ShapeDtypeStruct((t, D_MODEL), F32),
        jax.ShapeDtypeStruct((t, D_MODEL), F32),
    ]
    scratch_shapes = [
        pltpu.VMEM((PROJ_WIDTH, D_MODEL), BF16),
        pltpu.VMEM((D_MODEL, D_MODEL), BF16),
        pltpu.VMEM((D_MODEL, D_MODEL), BF16),
        pltpu.VMEM((tm, WIDTH_A), BF16),
        pltpu.VMEM((tm, WIDTH_A), F32),
        pltpu.VMEM((8, WIDTH_B), F32),
        pltpu.SemaphoreType.DMA((3 * N_DEV,)),
    ]
    if gathers:
        operands.append(next_shard)
        in_specs.append(hbm)
        out_specs.append(hbm)
        out_shape.append(jax.ShapeDtypeStruct((N_DEV * ROWS_LAYER, D_MODEL), BF16))
        scratch_shapes += list(GATHER_SEMS)

    return pl.pallas_call(
        body,
        name=f"layer{layer}_forward",
        grid=(nt,),
        in_specs=in_specs,
        out_specs=out_specs,
        out_shape=out_shape,
        scratch_shapes=scratch_shapes,
        compiler_params=pltpu.CompilerParams(dimension_semantics=("arbitrary",), vmem_limit_bytes=56 * MIB),
    )(*operands)


def _loss_head(x2, target, final_g):
    t = x2.shape[0]
    tm = _tile(t, 512)
    nt = t // tm

    def body(x_ref, tgt_ref, g_ref, dx_ref, head_ref, loss_acc, dg_acc):
        i = pl.program_id(0)

        @pl.when(i == 0)
        def _():
            loss_acc[...] = jnp.zeros_like(loss_acc)
            dg_acc[...] = jnp.zeros_like(dg_acc)

        xv = x_ref[...]
        g = g_ref[...]
        rstd = lax.rsqrt(jnp.mean(xv * xv, axis=-1, keepdims=True) + EPS)
        xhat = xv * rstd
        err = xhat * g - tgt_ref[...]
        loss_acc[...] += _colsum8(err * err)
        dy = err * (1.0 / D_MODEL)
        dg_acc[...] += _colsum8(dy * xhat)
        dxhat = dy * g
        dx_ref[...] = rstd * (dxhat - xhat * jnp.mean(dxhat * xhat, axis=-1, keepdims=True))

        @pl.when(i == nt - 1)
        def _():
            total = jnp.sum(loss_acc[...]) * (0.5 / D_MODEL)
            rows = lax.broadcasted_iota(jnp.int32, (SMALL_ROWS, D_MODEL), 0)
            lanes = lax.broadcasted_iota(jnp.int32, (SMALL_ROWS, D_MODEL), 1)
            head_ref[...] = jnp.where((rows == HEAD_LOSS) & (lanes == 0), total, 0.0)
            head_ref[HEAD_FINAL:HEAD_FINAL + 1, :] = jnp.sum(dg_acc[...], axis=0, keepdims=True)

    return pl.pallas_call(
        body,
        name="loss_head",
        grid=(nt,),
        in_specs=[
            pl.BlockSpec((tm, D_MODEL), lambda i: (i, 0)),
            pl.BlockSpec((tm, D_MODEL), lambda i: (i, 0)),
            pl.BlockSpec((1, D_MODEL), lambda i: (0, 0)),
        ],
        out_specs=[
            pl.BlockSpec((tm, D_MODEL), lambda i: (i, 0)),
            pl.BlockSpec((SMALL_ROWS, D_MODEL), lambda i: (0, 0)),
        ],
        out_shape=[
            jax.ShapeDtypeStruct((t, D_MODEL), F32),
            jax.ShapeDtypeStruct((SMALL_ROWS, D_MODEL), F32),
        ],
        scratch_shapes=[pltpu.VMEM((8, D_MODEL), F32), pltpu.VMEM((8, D_MODEL), F32)],
        compiler_params=pltpu.CompilerParams(dimension_semantics=("arbitrary",), vmem_limit_bytes=32 * MIB),
    )(x2, target, final_g)


class _DirectScatter:
    def __init__(self, pack_ref, pieces_ref, send_sems, recv_sems, local_sem):
        x, y, c = lax.axis_index("x"), lax.axis_index("y"), lax.axis_index("c")
        me = 4 * x + 2 * y + c
        self.copies = []
        for k in range(N_DEV - 1):
            fx, fy, fc = ((k + 1) >> 2) & 1, ((k + 1) >> 1) & 1, (k + 1) & 1
            tx, ty, tc = x ^ fx, y ^ fy, c ^ fc
            self.copies.append(
                pltpu.make_async_remote_copy(
                    src_ref=pack_ref.at[4 * tx + 2 * ty + tc], dst_ref=pieces_ref.at[me],
                    send_sem=send_sems.at[k], recv_sem=recv_sems.at[k],
                    device_id=(tx, ty, tc), device_id_type=MESH,
                )
            )
        self.mine = pltpu.make_async_copy(pack_ref.at[me], pieces_ref.at[me], local_sem)

    def start(self):
        self.mine.start()
        for cp in self.copies:
            cp.start()

    def finish(self):
        for cp in self.copies:
            cp.wait_recv()
        for cp in self.copies:
            cp.wait_send()
        self.mine.wait()


SCATTER_SEMS = [pltpu.SemaphoreType.DMA((N_DEV - 1,)), pltpu.SemaphoreType.DMA((N_DEV - 1,)), pltpu.SemaphoreType.DMA]


def _backward_layer(layer, dx2, x_in, x1, proj, gpre, p_l, wg, w_proj_t, conv_k, norm_g, ln_g, ln_b,
                    w_mix, w_mix_t, b_mix, ple_g, scatter_pack=None):
    t = x_in.shape[0]
    tm = _tile(t, 256)
    nt = t // tm
    n_chunks = tm // CHUNK
    halo_rows = 16
    scatters = scatter_pack is not None

    def body(*refs):
        (dx2_ref, xin_ref, x1_ref, proj_ref, halo_ref, gpre_ref, p_ref, wg_ref, wpt_ref, cw_ref,
         ng_ref, lng_ref, lnb_ref, wm_ref, wmt_ref, bm_ref, pg_ref) = refs[:17]
        refs = refs[17:]
        if scatters:
            pack_ref, refs = refs[0], refs[1:]
        (dxin_ref, dproj_ref, dx1_ref, dgpre_ref, dpp_ref, small_ref, dws_ref) = refs[:7]
        refs = refs[7:]
        if scatters:
            pieces_ref, refs = refs[0], refs[1:]
        (w_in_t, w_out, w_gate, vln_s, mixed_s, dmix_s, dvln_s, carry_s,
         ng_acc, pg_acc, lng_acc, lnb_acc, cw_acc, dbm_ref, sems) = refs[:15]
        i = pl.program_id(0)
        tile = nt - 1 - i
        if scatters:
            scatter = _DirectScatter(pack_ref, pieces_ref, *refs[15:18])

            @pl.when(i == 0)
            def _():
                scatter.start()

        @pl.when(i == 0)
        def _():
            copies = _weight_copies(wg_ref, w_in_t, w_out, w_gate, sems)
            for cp in copies:
                cp.start()
            carry_s[...] = jnp.zeros_like(carry_s)
            ng_acc[...] = jnp.zeros_like(ng_acc)
            pg_acc[...] = jnp.zeros_like(pg_acc)
            lng_acc[...] = jnp.zeros_like(lng_acc)
            lnb_acc[...] = jnp.zeros_like(lnb_acc)
            cw_acc[...] = jnp.zeros_like(cw_acc)
            dws_ref[...] = jnp.zeros_like(dws_ref)
            dbm_ref[...] = jnp.zeros_like(dbm_ref)
            for cp in copies:
                cp.wait()

        dx2v = dx2_ref[...]
        gate = _sigmoid(gpre_ref[...].astype(F32))
        pp = _dot_nt(p_ref[...].astype(BF16), wpt_ref[...])
        dpp_ref[...] = (dx2v * gate).astype(BF16)
        dgpre = (dx2v * pp * gate * (1.0 - gate)).astype(BF16)
        dgpre_ref[...] = dgpre
        dr = _dot_nt(dgpre, w_gate[...])
        x1v = x1_ref[...]
        rstd1 = lax.rsqrt(jnp.mean(x1v * x1v, axis=-1, keepdims=True) + EPS)
        xhat1 = x1v * rstd1
        pg_acc[...] += _colsum8(dr * xhat1)
        dxh = dr * pg_ref[...]
        dx1 = dx2v + rstd1 * (dxh - xhat1 * jnp.mean(dxh * xhat1, axis=-1, keepdims=True))
        dx1b = dx1.astype(BF16)
        dx1_ref[...] = dx1b

        dcat = _dot_nt(dx1b, w_out[...])
        dca = dcat[:, 0:512]
        dcb = dcat[:, 512:1024]

        u = proj_ref[:, 0:512].astype(F32)
        v = proj_ref[:, 512:1024].astype(F32)
        za = proj_ref[:, 1024:1536].astype(F32)
        mu = jnp.mean(v, axis=-1, keepdims=True)
        vc = v - mu
        var = jnp.mean(vc * vc, axis=-1, keepdims=True)
        rs = lax.rsqrt(var + EPS)
        vhat = vc * rs
        lng = lng_ref[...]
        vln_s[...] = (vhat * lng + lnb_ref[...]).astype(BF16)
        for ci in range(n_chunks):
            rows = pl.ds(ci * CHUNK, CHUNK)
            for h in range(HEADS_A):
                cols = pl.ds(h * HEAD_DIM, HEAD_DIM)
                mixed_s[rows, cols] = _dot(wm_ref[h], vln_s[rows, cols]) + bm_ref[h]
        mixed = mixed_s[...]
        sga = _sigmoid(za)
        sa = za * sga
        dsa = sga * (1.0 + za * (1.0 - sga))
        dproj_ref[:, 0:512] = (dca * mixed * sa).astype(BF16)
        dmix = dca * u * sa
        dproj_ref[:, 1024:1536] = (dca * u * mixed * dsa).astype(BF16)
        dmix_s[...] = dmix.astype(BF16)
        dbm_acc = jnp.zeros((CHUNK, WIDTH_A), F32)
        for ci in range(n_chunks):
            rows = pl.ds(ci * CHUNK, CHUNK)
            dbm_acc = dbm_acc + dmix[ci * CHUNK:(ci + 1) * CHUNK, :]
            for h in range(HEADS_A):
                cols = pl.ds(h * HEAD_DIM, HEAD_DIM)
                dvln_s[rows, cols] = _dot(wmt_ref[h], dmix_s[rows, cols])
                dws_ref[:, cols] += _dot_nt(dmix_s[rows, cols], vln_s[rows, cols])
        dbm_ref[...] += dbm_acc
        dvln = dvln_s[...]
        lng_acc[...] += _colsum8(dvln * vhat)
        lnb_acc[...] += _colsum8(dvln)
        dvh = dvln * lng
        dv = rs * (dvh - jnp.mean(dvh, axis=-1, keepdims=True) - vhat * jnp.mean(dvh * vhat, axis=-1, keepdims=True))
        dproj_ref[:, 512:1024] = dv.astype(BF16)

        hb = proj_ref[:, 1536:2048].astype(F32)
        gb = proj_ref[:, 2048:2560].astype(F32)
        gc = proj_ref[:, 2560:3072].astype(F32)
        zb = proj_ref[:, 3072:3584].astype(F32)
        xc = gc * hb
        prev = halo_ref[:, 2560:3072].astype(F32) * halo_ref[:, 1536:2048].astype(F32)
        prev = jnp.where(tile > 0, prev, 0.0)
        row = lax.broadcasted_iota(jnp.int32, (tm, WIDTH_B), 0)
        p1 = prev[halo_rows - 1:halo_rows, :]
        p2 = prev[halo_rows - 2:halo_rows - 1, :]
        xc_m1 = jnp.where(row == 0, p1, pltpu.roll(xc, 1, 0))
        xc_m2 = jnp.where(row == 0, p2, jnp.where(row == 1, p1, pltpu.roll(xc, 2, 0)))
        cw = cw_ref[...]
        yc = cw[0:1, :] * xc_m2 + cw[1:2, :] * xc_m1 + cw[2:3, :] * xc
        sgb = _sigmoid(zb)
        sb = zb * sgb
        dsb = sgb * (1.0 + zb * (1.0 - sgb))
        dproj_ref[:, 2048:2560] = (dcb * yc * sb).astype(BF16)
        dyc = dcb * gb * sb
        dproj_ref[:, 3072:3584] = (dcb * gb * yc * dsb).astype(BF16)
        nxt = carry_s[...]
        dyc_p1 = jnp.where(row == tm - 1, nxt[0:1, :], pltpu.roll(dyc, tm - 1, 0))
        dyc_p2 = jnp.where(row == tm - 1, nxt[1:2, :], jnp.where(row == tm - 2, nxt[0:1, :], pltpu.roll(dyc, tm - 2, 0)))
        carry_s[...] = dyc[0:8, :]
        dxc = cw[2:3, :] * dyc + cw[1:2, :] * dyc_p1 + cw[0:1, :] * dyc_p2
        cw_acc[0] += _colsum8(dyc * xc_m2)
        cw_acc[1] += _colsum8(dyc * xc_m1)
        cw_acc[2] += _colsum8(dyc * xc)
        dproj_ref[:, 1536:2048] = (dxc * gc).astype(BF16)
        dproj_ref[:, 2560:3072] = (dxc * hb).astype(BF16)

        dhn = _dot(dproj_ref[...], w_in_t[...])
        xv = xin_ref[...]
        rstd0 = lax.rsqrt(jnp.mean(xv * xv, axis=-1, keepdims=True) + EPS)
        xhat0 = xv * rstd0
        ng_acc[...] += _colsum8(dhn * xhat0)
        dxh0 = dhn * ng_ref[...]
        dxin_ref[...] = dx1 + rstd0 * (dxh0 - xhat0 * jnp.mean(dxh0 * xhat0, axis=-1, keepdims=True))

        @pl.when(i == nt - 1)
        def _():
            small_ref[...] = jnp.zeros_like(small_ref)
            small_ref[SMALL_NORM:SMALL_NORM + 1, :] = jnp.sum(ng_acc[...], axis=0, keepdims=True)
            small_ref[SMALL_PLE:SMALL_PLE + 1, :] = jnp.sum(pg_acc[...], axis=0, keepdims=True)
            small_ref[SMALL_LN:SMALL_LN + 1, 0:WIDTH_A] = jnp.sum(lng_acc[...], axis=0, keepdims=True)
            small_ref[SMALL_LN:SMALL_LN + 1, WIDTH_A:2 * WIDTH_A] = jnp.sum(lnb_acc[...], axis=0, keepdims=True)
            for h in range(HEADS_A):
                cols = pl.ds(h * HEAD_DIM, HEAD_DIM)
                small_ref[SMALL_BS:SMALL_BS + 1, cols] = jnp.sum(jnp.transpose(dbm_ref[:, cols]), axis=0, keepdims=True)
            for k in range(3):
                small_ref[SMALL_CONV + k:SMALL_CONV + k + 1, 0:WIDTH_B] = jnp.sum(cw_acc[k], axis=0, keepdims=True)
            if scatters:
                scatter.finish()

    def tok(width):
        return pl.BlockSpec((tm, width), lambda i: (nt - 1 - i, 0))

    def whole(shape):
        return pl.BlockSpec(shape, lambda i: (0,) * len(shape))

    halo_spec = pl.BlockSpec(
        (halo_rows, PROJ_WIDTH), lambda i: (jnp.maximum((nt - 1 - i) * (tm // halo_rows) - 1, 0), 0)
    )
    hbm = pl.BlockSpec(memory_space=pl.ANY)
    operands = [dx2, x_in, x1, proj, proj, gpre, p_l, wg, w_proj_t, conv_k, norm_g, ln_g, ln_b, w_mix, w_mix_t, b_mix, ple_g]
    in_specs = [
        tok(D_MODEL), tok(D_MODEL), tok(D_MODEL), tok(PROJ_WIDTH), halo_spec, tok(D_MODEL), tok(PLE_DIM), hbm,
        whole((D_MODEL, PLE_DIM)), whole((8, WIDTH_B)), whole((1, D_MODEL)), whole((1, WIDTH_A)), whole((1, WIDTH_A)),
        whole((HEADS_A, CHUNK, CHUNK)), whole((HEADS_A, CHUNK, CHUNK)), whole((HEADS_A, CHUNK, HEAD_DIM)),
        whole((1, D_MODEL)),
    ]
    out_specs = [
        tok(D_MODEL), tok(PROJ_WIDTH), tok(D_MODEL), tok(D_MODEL), tok(D_MODEL),
        whole((SMALL_ROWS, D_MODEL)), whole((CHUNK, WIDTH_A)),
    ]
    out_shape = [
        jax.ShapeDtypeStruct((t, D_MODEL), F32),
        jax.ShapeDtypeStruct((t, PROJ_WIDTH), BF16),
        jax.ShapeDtypeStruct((t, D_MODEL), BF16),
        jax.ShapeDtypeStruct((t, D_MODEL), BF16),
        jax.ShapeDtypeStruct((t, D_MODEL), BF16),
        jax.ShapeDtypeStruct((SMALL_ROWS, D_MODEL), F32),
        jax.ShapeDtypeStruct((CHUNK, WIDTH_A), F32),
    ]
    scratch_shapes = [
        pltpu.VMEM((PROJ_WIDTH, D_MODEL), BF16),
        pltpu.VMEM((D_MODEL, D_MODEL), BF16),
        pltpu.VMEM((D_MODEL, D_MODEL), BF16),
        pltpu.VMEM((tm, WIDTH_A), BF16),
        pltpu.VMEM((tm, WIDTH_A), F32),
        pltpu.VMEM((tm, WIDTH_A), BF16),
        pltpu.VMEM((tm, WIDTH_A), F32),
        pltpu.VMEM((8, WIDTH_B), F32),
        pltpu.VMEM((8, D_MODEL), F32),
        pltpu.VMEM((8, D_MODEL), F32),
        pltpu.VMEM((8, WIDTH_A), F32),
        pltpu.VMEM((8, WIDTH_A), F32),
        pltpu.VMEM((3, 8, WIDTH_B), F32),
        pltpu.VMEM((CHUNK, WIDTH_A), F32),
        pltpu.SemaphoreType.DMA((3 * N_DEV,)),
    ]
    if scatters:
        operands.append(scatter_pack)
        in_specs.append(hbm)
        out_specs.append(hbm)
        out_shape.append(jax.ShapeDtypeStruct(scatter_pack.shape, scatter_pack.dtype))
        scratch_shapes += list(SCATTER_SEMS)

    return pl.pallas_call(
        body,
        name=f"layer{layer}_backward",
        grid=(nt,),
        in_specs=in_specs,
        out_specs=out_specs,
        out_shape=out_shape,
        scratch_shapes=scratch_shapes,
        compiler_params=pltpu.CompilerParams(dimension_semantics=("arbitrary",), vmem_limit_bytes=56 * MIB),
    )(*operands)


def _sum_pieces(layer, pieces):
    rows, n = pieces.shape[1], pieces.shape[2]

    def body(p_ref, out_ref):
        total = p_ref[0].astype(F32)
        for j in range(1, N_DEV):
            total = total + p_ref[j].astype(F32)
        out_ref[...] = total

    return pl.pallas_call(
        body,
        name=f"layer{layer}_grad_sum",
        out_shape=jax.ShapeDtypeStruct((rows, n), F32),
        in_specs=[pl.BlockSpec(memory_space=pltpu.VMEM)],
        out_specs=pl.BlockSpec(memory_space=pltpu.VMEM),
        compiler_params=pltpu.CompilerParams(vmem_limit_bytes=32 * MIB),
    )(pieces)


def _weight_grads(layer, dproj, hn, cat, dx1, r, dgpre, dpp, p_l):
    t = hn.shape[0]
    tk = _tile(t, 512)
    nt = t // tk
    in_blocks = PROJ_WIDTH // 512

    def body(dproj_ref, hn_ref, cat_ref, dx1_ref, r_ref, dgpre_ref, dpp_ref, p_ref, pack_ref,
             acc_in, acc_out, acc_gate, acc_proj, stage, sems):
        i = pl.program_id(0)

        @pl.when(i == 0)
        def _():
            acc_in[...] = jnp.zeros_like(acc_in)
            acc_out[...] = jnp.zeros_like(acc_out)
            acc_gate[...] = jnp.zeros_like(acc_gate)
            acc_proj[...] = jnp.zeros_like(acc_proj)

        hnv = hn_ref[...]
        for b in range(in_blocks):
            acc_in[pl.ds(b * 512, 512), :] += _dot_tn(dproj_ref[:, b * 512:(b + 1) * 512], hnv)
        dx1v = dx1_ref[...]
        dgv = dgpre_ref[...]
        for b in range(D_MODEL // 512):
            acc_out[pl.ds(b * 512, 512), :] += _dot_tn(cat_ref[:, b * 512:(b + 1) * 512], dx1v)
            acc_gate[pl.ds(b * 512, 512), :] += _dot_tn(r_ref[:, b * 512:(b + 1) * 512], dgv)
        pv = p_ref[...].astype(BF16)
        for b in range(D_MODEL // 512):
            acc_proj[pl.ds(b * 512, 512), :] += _dot_tn(dpp_ref[:, b * 512:(b + 1) * 512], pv)

        @pl.when(i == nt - 1)
        def _():
            def out_copy(s):
                return pltpu.make_async_copy(stage.at[s % 2], pack_ref.at[s], sems.at[s % 2])

            for s in range(N_DEV):
                if s >= 2:
                    out_copy(s - 2).wait()
                buf = stage.at[s % 2]
                buf[pl.ds(OFF_IN, ROWS_IN), :] = acc_in[pl.ds(s * ROWS_IN, ROWS_IN), :].astype(BF16)
                buf[pl.ds(OFF_OUT, ROWS_OUT), :] = acc_out[pl.ds(s * ROWS_OUT, ROWS_OUT), :].astype(BF16)
                buf[pl.ds(OFF_GATE, ROWS_GATE), :] = acc_gate[pl.ds(s * ROWS_GATE, ROWS_GATE), :].astype(BF16)
                for j in range(D_MODEL // PLE_DIM):
                    buf[pl.ds(OFF_PROJ, ROWS_PROJ), pl.ds(j * PLE_DIM, PLE_DIM)] = acc_proj[
                        pl.ds(s * ROWS_OUT + j * ROWS_PROJ, ROWS_PROJ), :
                    ].astype(BF16)
                out_copy(s).start()
            out_copy(N_DEV - 2).wait()
            out_copy(N_DEV - 1).wait()

    def tok(width):
        return pl.BlockSpec((tk, width), lambda i: (i, 0))

    return pl.pallas_call(
        body,
        name=f"layer{layer}_weight_grads",
        grid=(nt,),
        in_specs=[tok(PROJ_WIDTH), tok(D_MODEL), tok(D_MODEL), tok(D_MODEL), tok(D_MODEL), tok(D_MODEL), tok(D_MODEL), tok(PLE_DIM)],
        out_specs=pl.BlockSpec(memory_space=pl.ANY),
        out_shape=jax.ShapeDtypeStruct((N_DEV, ROWS_GRAD, D_MODEL), BF16),
        scratch_shapes=[
            pltpu.VMEM((PROJ_WIDTH, D_MODEL), F32),
            pltpu.VMEM((D_MODEL, D_MODEL), F32),
            pltpu.VMEM((D_MODEL, D_MODEL), F32),
            pltpu.VMEM((D_MODEL, PLE_DIM), F32),
            pltpu.VMEM((2, ROWS_GRAD, D_MODEL), BF16),
            pltpu.SemaphoreType.DMA((2,)),
        ],
        compiler_params=pltpu.CompilerParams(dimension_semantics=("arbitrary",), vmem_limit_bytes=58 * MIB),
    )(dproj, hn, cat, dx1, r, dgpre, dpp, p_l)


def _reduce_scatter_all_reduce(layer, pack, smalls, head, dws):
    rows, n = pack.shape[1], pack.shape[2]
    assert DEPTH * WIDTH_A == D_MODEL and n == D_MODEL

    def body(g_ref, *refs):
        small_refs, refs = refs[:DEPTH], refs[DEPTH:]
        head_ref, refs = refs[0], refs[1:]
        dws_refs, refs = refs[:DEPTH], refs[DEPTH:]
        (out_ref, total_ref, r1, a_s, r2, sp, sr1, sq, send1, recv1, send2, recv2, ssend, srecv) = refs
        x, y, c = lax.axis_index("x"), lax.axis_index("y"), lax.axis_index("c")
        sibling = (x, y, 1 - c)
        chip = 2 * x + y
        flips = [(1, 0), (0, 1), (1, 1)]

        for l in range(DEPTH):
            sp[l * SMALL_ROWS:(l + 1) * SMALL_ROWS, :] = small_refs[l][...]
            sp[TOTAL_WS:TOTAL_ROWS, l * WIDTH_A:(l + 1) * WIDTH_A] = dws_refs[l][...]
        sp[TOTAL_HEAD:TOTAL_WS, :] = head_ref[...]

        small_pair = pltpu.make_async_remote_copy(
            src_ref=sp, dst_ref=sr1, send_sem=ssend.at[0], recv_sem=srecv.at[0], device_id=sibling, device_id_type=MESH
        )

        def to_sibling(j):
            return pltpu.make_async_remote_copy(
                src_ref=g_ref.at[2 * j + 1 - c], dst_ref=r1.at[j], send_sem=send1.at[j], recv_sem=recv1.at[j],
                device_id=sibling, device_id_type=MESH,
            )

        first = [to_sibling(j) for j in range(4)]
        small_pair.start()
        for cp in first:
            cp.start()

        small_pair.wait_recv()
        sq[chip] = sp[...] + sr1[...]
        small_chips = [
            pltpu.make_async_remote_copy(
                src_ref=sq.at[chip], dst_ref=sq.at[chip], send_sem=ssend.at[1 + k], recv_sem=srecv.at[1 + k],
                device_id=(x ^ fx, y ^ fy, c), device_id_type=MESH,
            )
            for k, (fx, fy) in enumerate(flips)
        ]
        for cp in small_chips:
            cp.start()

        for j in range(4):
            first[j].wait_recv()
            a_s[j] = (g_ref[2 * j + c].astype(F32) + r1[j].astype(F32)).astype(BF16)

        def to_chip(k):
            fx, fy = flips[k]
            tx, ty = x ^ fx, y ^ fy
            return pltpu.make_async_remote_copy(
                src_ref=a_s.at[2 * tx + ty], dst_ref=r2.at[k], send_sem=send2.at[k], recv_sem=recv2.at[k],
                device_id=(tx, ty, c), device_id_type=MESH,
            )

        second = [to_chip(k) for k in range(3)]
        for cp in second:
            cp.start()
        total = g_ref[2 * chip + c].astype(F32) + r1[chip].astype(F32)
        for cp in small_chips:
            cp.wait_recv()
        total_ref[...] = ((sq[0] + sq[1]) + sq[2]) + sq[3]
        for k in range(3):
            second[k].wait_recv()
            total = total + r2[k].astype(F32)
        out_ref[...] = total
        small_pair.wait_send()
        for cp in first + second + small_chips:
            cp.wait_send()

    vmem = pl.BlockSpec(memory_space=pltpu.VMEM)
    return pl.pallas_call(
        body,
        name=f"layer{layer}_grad_reduce_scatter",
        out_shape=[jax.ShapeDtypeStruct((rows, n), F32), jax.ShapeDtypeStruct((TOTAL_ROWS, D_MODEL), F32)],
        in_specs=[vmem] * (2 + 2 * DEPTH),
        out_specs=[vmem, vmem],
        scratch_shapes=[
            pltpu.VMEM((4, rows, n), BF16),
            pltpu.VMEM((4, rows, n), BF16),
            pltpu.VMEM((3, rows, n), BF16),
            pltpu.VMEM((TOTAL_ROWS, D_MODEL), F32),
            pltpu.VMEM((TOTAL_ROWS, D_MODEL), F32),
            pltpu.VMEM((4, TOTAL_ROWS, D_MODEL), F32),
            pltpu.SemaphoreType.DMA((4,)),
            pltpu.SemaphoreType.DMA((4,)),
            pltpu.SemaphoreType.DMA((3,)),
            pltpu.SemaphoreType.DMA((3,)),
            pltpu.SemaphoreType.DMA((4,)),
            pltpu.SemaphoreType.DMA((4,)),
        ],
        compiler_params=pltpu.CompilerParams(vmem_limit_bytes=48 * MIB),
    )(pack, *smalls, head, *dws)


def _adam_step(w, g, m, v):
    m = ADAM_B1 * m + (1.0 - ADAM_B1) * g
    v = ADAM_B2 * v + (1.0 - ADAM_B2) * (g * g)
    m_hat = m / (1.0 - ADAM_B1 ** ADAM_STEP)
    v_hat = v / (1.0 - ADAM_B2 ** ADAM_STEP)
    return -ADAM_LR * (m_hat / (jnp.sqrt(v_hat) + ADAM_EPS) + ADAM_WD * w), m, v


def _adamw_rows(name, reduced, row_off, states):
    n = len(states)

    def body(*refs):
        red = refs[:DEPTH]
        ins = refs[DEPTH:DEPTH + 3 * n]
        outs = refs[DEPTH + 3 * n:]
        for k in range(n):
            w_ref, m_ref, v_ref = ins[3 * k:3 * k + 3]
            g_ref, d_ref, nm_ref, nv_ref = outs[4 * k:4 * k + 4]
            r = w_ref.shape[1]
            for l in range(DEPTH):
                g = red[l][row_off[k]:row_off[k] + r, :]
                d, m, v = _adam_step(w_ref[l], g, m_ref[l], v_ref[l])
                g_ref[l] = g
                d_ref[l] = d
                nm_ref[l] = m
                nv_ref[l] = v

    flat = [a for st in states for a in st]
    out_shape = []
    for w, _, _ in states:
        out_shape += [jax.ShapeDtypeStruct(w.shape, F32)] * 4
    vmem = pl.BlockSpec(memory_space=pltpu.VMEM)
    outs = pl.pallas_call(
        body,
        name=name,
        out_shape=out_shape,
        in_specs=[vmem] * (DEPTH + len(flat)),
        out_specs=[vmem] * len(out_shape),
        compiler_params=pltpu.CompilerParams(vmem_limit_bytes=48 * MIB),
    )(*reduced, *flat)
    return [tuple(outs[4 * k:4 * k + 4]) for k in range(n)]


def _adamw_small(total, g_conv, g_proj, st):
    names = ["norm_g", "ple_norm_g", "ln_v_g", "ln_v_b", "b_s", "w_s", "final_g", "conv_w", "w_ple_proj"]
    cut = names[:7]

    def body(total_ref, gconv_ref, gproj_ref, *refs):
        ins = {nm: refs[3 * k:3 * k + 3] for k, nm in enumerate(names)}
        outs, pos = {}, 3 * len(names)
        for nm in names:
            cnt = 4 if nm in cut else 3
            outs[nm] = refs[pos:pos + cnt]
            pos += cnt

        def update(nm, idx, g):
            w_ref, m_ref, v_ref = ins[nm]
            d, m, v = _adam_step(w_ref[idx], g, m_ref[idx], v_ref[idx])
            o = outs[nm]
            if nm in cut:
                o[0][idx] = g
                o = o[1:]
            o[0][idx] = d
            o[1][idx] = m
            o[2][idx] = v

        tril = (lax.broadcasted_iota(jnp.int32, (CHUNK, CHUNK), 0) >= lax.broadcasted_iota(jnp.int32, (CHUNK, CHUNK), 1))
        for l in range(DEPTH):
            base = l * SMALL_ROWS
            row = (slice(l, l + 1), slice(None))
            update("norm_g", row, total_ref[base + SMALL_NORM:base + SMALL_NORM + 1, :])
            update("ple_norm_g", row, total_ref[base + SMALL_PLE:base + SMALL_PLE + 1, :])
            update("ln_v_g", row, total_ref[base + SMALL_LN:base + SMALL_LN + 1, 0:WIDTH_A])
            update("ln_v_b", row, total_ref[base + SMALL_LN:base + SMALL_LN + 1, WIDTH_A:2 * WIDTH_A])
            for h in range(HEADS_A):
                update("b_s", (l, slice(h, h + 1), slice(None)),
                       total_ref[base + SMALL_BS:base + SMALL_BS + 1, h * HEAD_DIM:(h + 1) * HEAD_DIM])
                lanes = slice(l * WIDTH_A + h * CHUNK, l * WIDTH_A + (h + 1) * CHUNK)
                update("w_s", (l, h), jnp.where(tril, total_ref[TOTAL_WS:TOTAL_ROWS, lanes], 0.0))
        update("final_g", (slice(None), slice(None)), total_ref[TOTAL_HEAD + HEAD_FINAL:TOTAL_HEAD + HEAD_FINAL + 1, :])
        update("conv_w", (slice(None),) * 3, gconv_ref[...])
        update("w_ple_proj", (slice(None),) * 3, gproj_ref[...])

    flat = [a for nm in names for a in st[nm]]
    out_shape = []
    for nm in names:
        out_shape += [jax.ShapeDtypeStruct(st[nm][0].shape, F32)] * (4 if nm in cut else 3)
    vmem = pl.BlockSpec(memory_space=pltpu.VMEM)
    outs = pl.pallas_call(
        body,
        name="adamw_small",
        out_shape=out_shape,
        in_specs=[vmem] * (3 + len(flat)),
        out_specs=[vmem] * len(out_shape),
        compiler_params=pltpu.CompilerParams(vmem_limit_bytes=32 * MIB),
    )(total, g_conv, g_proj, *flat)
    res, pos = {}, 0
    for nm in names:
        cnt = 4 if nm in cut else 3
        got = tuple(outs[pos:pos + cnt])
        res[nm] = got if nm in cut else ((g_conv if nm == "conv_w" else g_proj),) + got
        pos += cnt
    return res


def _split3_bf16(a):
    b1 = a.astype(BF16)
    r1 = a - b1.astype(F32)
    b2 = r1.astype(BF16)
    b3 = (r1 - b2.astype(F32)).astype(BF16)
    return b1, b2, b3


def _pack_weight_shard(w_in_l, w_out_l, w_gate_l, w_proj_l, conv_w_l):
    w_in_t = jnp.transpose(w_in_l).astype(BF16)
    proj_t = jnp.transpose(w_proj_l).astype(BF16)
    proj_rows = proj_t.reshape(D_MODEL // PLE_DIM, ROWS_PROJ, PLE_DIM).transpose(1, 0, 2).reshape(ROWS_PROJ, D_MODEL)
    conv_parts = jnp.concatenate([b.reshape(-1) for b in _split3_bf16(conv_w_l)])
    conv_rows = jnp.concatenate([conv_parts, jnp.zeros((ROWS_CONV * D_MODEL - conv_parts.shape[0],), BF16)])
    return jnp.concatenate(
        [w_in_t, w_out_l.astype(BF16), w_gate_l.astype(BF16), proj_rows, conv_rows.reshape(ROWS_CONV, D_MODEL)], axis=0
    )


def _unpack_proj_conv(wg):
    per_dev = wg.reshape(N_DEV, ROWS_LAYER, D_MODEL)
    proj_rows = per_dev[:, OFF_PROJ:OFF_PROJ + ROWS_PROJ]
    proj_t = proj_rows.reshape(N_DEV, ROWS_PROJ, D_MODEL // PLE_DIM, PLE_DIM).transpose(0, 2, 1, 3).reshape(D_MODEL, PLE_DIM)
    n_conv = (WIDTH_B // N_DEV) * 3
    conv_parts = per_dev[:, OFF_CONV].astype(F32)[:, :3 * n_conv].reshape(N_DEV, 3, n_conv)
    conv = (conv_parts[:, 0] + conv_parts[:, 1]) + conv_parts[:, 2]
    conv_k = jnp.transpose(conv.reshape(WIDTH_B, 3))
    conv_k = jnp.concatenate([conv_k, jnp.zeros((5, WIDTH_B), F32)], axis=0)
    return proj_t, conv_k


def _unpack_grad_proj(red):
    proj_rows = red[OFF_PROJ:OFF_PROJ + ROWS_PROJ]
    proj_t = proj_rows.reshape(ROWS_PROJ, D_MODEL // PLE_DIM, PLE_DIM).transpose(1, 0, 2).reshape(ROWS_OUT, PLE_DIM)
    return jnp.transpose(proj_t)


def kernel(x, p, norm_g, w_in, ln_v_g, ln_v_b, w_s, b_s, conv_w, w_out, ple_norm_g, w_ple_gate, w_ple_proj, final_g, loss_target, m_norm_g, m_w_in, m_ln_v_g, m_ln_v_b, m_w_s, m_b_s, m_conv_w, m_w_out, m_ple_norm_g, m_w_ple_gate, m_w_ple_proj, m_final_g, v_norm_g, v_w_in, v_ln_v_g, v_ln_v_b, v_w_s, v_b_s, v_conv_w, v_w_out, v_ple_norm_g, v_w_ple_gate, v_w_ple_proj, v_final_g):
    me = 4 * lax.axis_index("x") + 2 * lax.axis_index("y") + lax.axis_index("c")
    xs = x[0]
    target = loss_target[0]

    shards = [_pack_weight_shard(w_in[l], w_out[l], w_ple_gate[l], w_ple_proj[l], conv_w[l]) for l in range(DEPTH)]
    tril = jnp.tril(jnp.ones((CHUNK, CHUNK), F32))

    def consts(l, wg_l):
        w_proj_t, conv_k = _unpack_proj_conv(wg_l)
        w_mix = w_s[l] * tril[None]
        return dict(
            wg=wg_l, w_proj_t=w_proj_t, conv_k=conv_k,
            norm_g=norm_g[l].reshape(1, D_MODEL), ln_g=ln_v_g[l].reshape(1, WIDTH_A), ln_b=ln_v_b[l].reshape(1, WIDTH_A),
            w_mix=w_mix.astype(BF16), w_mix_t=jnp.swapaxes(w_mix, 1, 2).astype(BF16),
            b_mix=jnp.broadcast_to(b_s[l][:, :, None], (HEADS_A, CHUNK, HEAD_DIM)),
            ple_g=ple_norm_g[l].reshape(1, D_MODEL),
        )

    layer_consts = [consts(0, _all_gather_rows(shards[0]))]
    saved = []
    h = xs
    for l in range(DEPTH):
        k = layer_consts[l]
        outs = _forward_layer(
            l, h, p[l, 0], k["wg"], k["w_proj_t"], k["conv_k"], k["norm_g"], k["ln_g"], k["ln_b"], k["w_mix"], k["b_mix"],
            k["ple_g"], next_shard=shards[l + 1] if l + 1 < DEPTH else None)
        proj, hn, cat, r, gpre, x1, x2 = outs[:7]
        if l + 1 < DEPTH:
            layer_consts.append(consts(l + 1, outs[7]))
        saved.append(dict(x_in=h, proj=proj, hn=hn, cat=cat, r=r, gpre=gpre, x1=x1))
        h = x2

    dx, head = _loss_head(h, target, final_g.reshape(1, D_MODEL))

    smalls, dws = [None] * DEPTH, [None] * DEPTH
    reduced = [None] * DEPTH
    pending = None
    for l in reversed(range(DEPTH)):
        k, s = layer_consts[l], saved[l]
        outs = _backward_layer(
            l, dx, s["x_in"], s["x1"], s["proj"], s["gpre"], p[l, 0], k["wg"], k["w_proj_t"], k["conv_k"],
            k["norm_g"], k["ln_g"], k["ln_b"], k["w_mix"], k["w_mix_t"], k["b_mix"], k["ple_g"], scatter_pack=pending)
        dx, dproj, dx1, dgpre, dpp, smalls[l], dws[l] = outs[:7]
        if pending is not None:
            reduced[l + 1] = _sum_pieces(l + 1, outs[7])
        pending = _weight_grads(l, dproj, s["hn"], s["cat"], dx1, s["r"], dgpre, dpp, p[l, 0])
    reduced[0], total = _reduce_scatter_all_reduce(0, pending, smalls, head, dws)
    grad_x = dx[None]
    loss = total[TOTAL_HEAD + HEAD_LOSS, 0]

    n_ch = WIDTH_B // N_DEV
    g_conv = jnp.stack([total[l * SMALL_ROWS + SMALL_CONV:l * SMALL_ROWS + SMALL_CONV + 3, 0:WIDTH_B] for l in range(DEPTH)], axis=1)
    g_conv = lax.dynamic_slice_in_dim(g_conv, me * n_ch, n_ch, axis=2)
    g_proj = jnp.stack([_unpack_grad_proj(reduced[l]) for l in range(DEPTH)])

    def t_in(a):
        return jnp.swapaxes(a, 1, 2)

    def t_conv(a):
        return jnp.transpose(a, (2, 0, 1))

    (r_in,) = _adamw_rows("adamw_w_in", reduced, [OFF_IN], [(t_in(w_in), t_in(m_w_in), t_in(v_w_in))])
    r_out, r_gate = _adamw_rows(
        "adamw_w_out_gate", reduced, [OFF_OUT, OFF_GATE],
        [(w_out, m_w_out, v_w_out), (w_ple_gate, m_w_ple_gate, v_w_ple_gate)])
    small = _adamw_small(total, g_conv, g_proj, dict(
        norm_g=(norm_g, m_norm_g, v_norm_g), ple_norm_g=(ple_norm_g, m_ple_norm_g, v_ple_norm_g),
        ln_v_g=(ln_v_g, m_ln_v_g, v_ln_v_g), ln_v_b=(ln_v_b, m_ln_v_b, v_ln_v_b),
        b_s=(b_s, m_b_s, v_b_s), w_s=(w_s, m_w_s, v_w_s),
        final_g=tuple(a.reshape(1, D_MODEL) for a in (final_g, m_final_g, v_final_g)),
        conv_w=(t_conv(conv_w), t_conv(m_conv_w), t_conv(v_conv_w)),
        w_ple_proj=(w_ple_proj, m_w_ple_proj, v_w_ple_proj),
    ))
    res = dict(small, w_in=tuple(t_in(a) for a in r_in), w_out=r_out, w_ple_gate=r_gate)
    res["final_g"] = tuple(a.reshape(D_MODEL) for a in res["final_g"])
    res["conv_w"] = tuple(jnp.transpose(a, (1, 2, 0)) for a in res["conv_w"])
    order = ["norm_g", "w_in", "ln_v_g", "ln_v_b", "w_s", "b_s", "conv_w", "w_out", "ple_norm_g", "w_ple_gate", "w_ple_proj", "final_g"]
    return (loss, grad_x, *[res[n][0] for n in order], *[res[n][1] for n in order],
            *[res[n][2] for n in order], *[res[n][3] for n in order])
```

```python
import jax
import jax.numpy as jnp
from jax import lax
from jax.experimental import pallas as pl
from jax.experimental.pallas import tpu as pltpu

F32 = jnp.float32
BF16 = jnp.bfloat16

D_MODEL = 1024
WIDTH_A = 512
WIDTH_B = 512
HEADS_A = 4
HEAD_DIM = 128
CHUNK = 128
PLE_DIM = 256
PROJ_WIDTH = 3584
DEPTH = 2
EPS = 1e-6
N_DEV = 8

ADAM_LR = 0.001
ADAM_B1 = 0.9
ADAM_B2 = 0.999
ADAM_EPS = 1e-08
ADAM_WD = 0.01
ADAM_STEP = 10

ROWS_IN = PROJ_WIDTH // N_DEV
ROWS_OUT = D_MODEL // N_DEV
ROWS_GATE = D_MODEL // N_DEV
ROWS_PROJ = (D_MODEL // N_DEV) * PLE_DIM // D_MODEL
ROWS_CONV = 16
OFF_IN = 0
OFF_OUT = OFF_IN + ROWS_IN
OFF_GATE = OFF_OUT + ROWS_OUT
OFF_PROJ = OFF_GATE + ROWS_GATE
OFF_CONV = OFF_PROJ + ROWS_PROJ
ROWS_GRAD = OFF_CONV
ROWS_LAYER = OFF_CONV + ROWS_CONV

SMALL_ROWS = 8
SMALL_NORM = 0
SMALL_PLE = 1
SMALL_LN = 2
SMALL_BS = 3
SMALL_CONV = 4
HEAD_FINAL = 0
HEAD_LOSS = 1
TOTAL_HEAD = DEPTH * SMALL_ROWS
TOTAL_WS = TOTAL_HEAD + SMALL_ROWS
TOTAL_ROWS = TOTAL_WS + CHUNK

MIB = 1024 * 1024
MESH = pl.DeviceIdType.MESH

NT_DIMS = (((1,), (1,)), ((), ()))
TN_DIMS = (((0,), (0,)), ((), ()))


def _dot(a, b):
    return jnp.dot(a, b, preferred_element_type=F32)


def _dot_nt(a, b):
    return lax.dot_general(a, b, NT_DIMS, preferred_element_type=F32)


def _dot_tn(a, b):
    return lax.dot_general(a, b, TN_DIMS, preferred_element_type=F32)


def _colsum8(a):
    rows, n = a.shape
    return jnp.sum(a.reshape(rows // 8, 8, n), axis=0)


def _sigmoid(z):
    return 1.0 / (1.0 + jnp.exp(-z))


def _tile(t, want):
    return want if t % want == 0 else t


class _TwoLevelGather:
    def __init__(self, x_ref, out_ref, m_per, send_sems, recv_sems, local_sem):
        x, y, c = lax.axis_index("x"), lax.axis_index("y"), lax.axis_index("c")
        self.c = c
        self.me, self.sibling = (x, y, c), (x, y, 1 - c)
        self.chips = [(1 - x, y), (x, 1 - y), (1 - x, 1 - y)]
        self.x_ref, self.out_ref, self.m_per = x_ref, out_ref, m_per
        self.send_sems, self.recv_sems = send_sems, recv_sems
        self.mine = pltpu.make_async_copy(x_ref, self.rows(*self.me), local_sem)

    def rows(self, px, py, pc):
        return self.out_ref.at[pl.ds((4 * px + 2 * py + pc) * self.m_per, self.m_per), :]

    def copy(self, k, block, to, src=None):
        return pltpu.make_async_remote_copy(
            src_ref=self.rows(*block) if src is None else src,
            dst_ref=self.rows(*block),
            send_sem=self.send_sems.at[k],
            recv_sem=self.recv_sems.at[k],
            device_id=to,
            device_id_type=MESH,
        )

    def first(self):
        out = [self.copy(0, self.me, self.sibling, src=self.x_ref)]
        return out + [self.copy(1 + j, self.me, (*chip, self.c), src=self.x_ref) for j, chip in enumerate(self.chips)]

    def passed(self):
        return [self.copy(4 + j, (*chip, self.c), self.sibling) for j, chip in enumerate(self.chips)]

    def start(self):
        self.mine.start()
        for cp in self.first():
            cp.start()

    def pass_on(self):
        passed = self.passed()
        for j, chip in enumerate(self.chips):
            self.copy(1 + j, (*chip, self.c), self.me).wait_recv()
            passed[j].start()

    def finish(self):
        self.copy(0, self.sibling, self.me).wait_recv()
        for j, chip in enumerate(self.chips):
            self.copy(4 + j, (*chip, 1 - self.c), self.me).wait_recv()
        for cp in self.first() + self.passed():
            cp.wait_send()
        self.mine.wait()


GATHER_SEMS = [pltpu.SemaphoreType.DMA((7,)), pltpu.SemaphoreType.DMA((7,)), pltpu.SemaphoreType.DMA]


def _all_gather_rows(shard):
    m_per, n = shard.shape

    def body(x_ref, out_ref, send_sems, recv_sems, local_sem):
        ag = _TwoLevelGather(x_ref, out_ref, m_per, send_sems, recv_sems, local_sem)
        ag.start()
        ag.pass_on()
        ag.finish()

    return pl.pallas_call(
        body,
        name="weights_all_gather",
        out_shape=jax.ShapeDtypeStruct((N_DEV * m_per, n), shard.dtype),
        in_specs=[pl.BlockSpec(memory_space=pltpu.VMEM)],
        out_specs=pl.BlockSpec(memory_space=pltpu.VMEM),
        scratch_shapes=list(GATHER_SEMS),
        compiler_params=pltpu.CompilerParams(vmem_limit_bytes=48 * MIB),
    )(shard)


PROJ_PARTS = D_MODEL // PLE_DIM
N_WEIGHT_COPIES = N_DEV * (3 + PROJ_PARTS)


def _weight_copies(wg_ref, w_in_t, w_out, w_gate, w_proj_t, sems):
    copies = []
    for s in range(N_DEV):
        base = s * ROWS_LAYER
        for dst, off, rows in ((w_in_t, OFF_IN, ROWS_IN), (w_out, OFF_OUT, ROWS_OUT), (w_gate, OFF_GATE, ROWS_GATE)):
            copies.append((wg_ref.at[pl.ds(base + off, rows), :], dst.at[pl.ds(s * rows, rows), :]))
        for j in range(PROJ_PARTS):
            copies.append((
                wg_ref.at[pl.ds(base + OFF_PROJ, ROWS_PROJ), pl.ds(j * PLE_DIM, PLE_DIM)],
                w_proj_t.at[pl.ds(s * ROWS_OUT + j * ROWS_PROJ, ROWS_PROJ), :],
            ))
    return [pltpu.make_async_copy(src, dst, sems.at[k]) for k, (src, dst) in enumerate(copies)]


def _forward_layer(layer, x, p_l, wg, conv_k, norm_g, ln_g, ln_b, w_mix, b_mix, ple_g, next_shard=None):
    t = x.shape[0]
    tm = _tile(t, 512)
    nt = t // tm
    gathers = next_shard is not None

    def body(*refs):
        (x_ref, p_ref, wg_ref, cw_ref, ng_ref, lng_ref, lnb_ref, wm_ref, bm_ref, pg_ref) = refs[:10]
        refs = refs[10:]
        if gathers:
            shard_ref, refs = refs[0], refs[1:]
        (proj_ref, hn_ref, cat_ref, r_ref, gpre_ref, x1_ref, x2_ref) = refs[:7]
        refs = refs[7:]
        if gathers:
            gathered_ref, refs = refs[0], refs[1:]
        (w_in_t, w_out, w_gate, wpt_ref, vln_s, mixed_s, halo_s, sems) = refs[:8]
        i = pl.program_id(0)
        if gathers:
            ag = _TwoLevelGather(shard_ref, gathered_ref, ROWS_LAYER, *refs[8:11])

            @pl.when(i == 0)
            def _():
                ag.start()

            @pl.when(i == nt // 2)
            def _():
                ag.pass_on()

        @pl.when(i == 0)
        def _():
            copies = _weight_copies(wg_ref, w_in_t, w_out, w_gate, wpt_ref, sems)
            for cp in copies:
                cp.start()
            halo_s[...] = jnp.zeros_like(halo_s)
            for cp in copies:
                cp.wait()

        xv = x_ref[...]
        rstd0 = lax.rsqrt(jnp.mean(xv * xv, axis=-1, keepdims=True) + EPS)
        hn_ref[...] = (xv * rstd0 * ng_ref[...]).astype(BF16)

        def proj_section(k):
            sec = _dot_nt(hn_ref[...], w_in_t[pl.ds(k * 512, 512), :])
            proj_ref[:, k * 512:(k + 1) * 512] = sec.astype(BF16)
            return sec

        v = proj_section(1)
        mu = jnp.mean(v, axis=-1, keepdims=True)
        vc = v - mu
        var = jnp.mean(vc * vc, axis=-1, keepdims=True)
        vln = vc * lax.rsqrt(var + EPS) * lng_ref[...] + lnb_ref[...]
        vln_s[...] = vln.astype(BF16)
        for ci in range(tm // CHUNK):
            rows = pl.ds(ci * CHUNK, CHUNK)
            for h in range(HEADS_A):
                cols = pl.ds(h * HEAD_DIM, HEAD_DIM)
                mixed_s[rows, cols] = _dot(wm_ref[h], vln_s[rows, cols]) + bm_ref[h]
        u = proj_section(0)
        za = proj_section(2)
        out_a = u * mixed_s[...] * (za * _sigmoid(za))
        cat_ref[:, 0:512] = out_a.astype(BF16)

        xc = proj_section(5) * proj_section(3)
        prev = halo_s[...]
        row = lax.broadcasted_iota(jnp.int32, (tm, WIDTH_B), 0)
        xc_m1 = jnp.where(row == 0, prev[7:8, :], pltpu.roll(xc, 1, 0))
        xc_m2 = jnp.where(row == 0, prev[6:7, :], jnp.where(row == 1, prev[7:8, :], pltpu.roll(xc, 2, 0)))
        halo_s[...] = xc[tm - 8:tm, :]
        cw = cw_ref[...]
        yc = cw[0:1, :] * xc_m2 + cw[1:2, :] * xc_m1 + cw[2:3, :] * xc
        zb = proj_section(6)
        out_b = proj_section(4) * yc * (zb * _sigmoid(zb))
        cat_ref[:, 512:1024] = out_b.astype(BF16)

        x1 = xv + _dot(cat_ref[...], w_out[...])
        x1_ref[...] = x1
        rstd1 = lax.rsqrt(jnp.mean(x1 * x1, axis=-1, keepdims=True) + EPS)
        r_ref[...] = (x1 * rstd1 * pg_ref[...]).astype(BF16)
        gpre = _dot(r_ref[...], w_gate[...])
        gpre_ref[...] = gpre.astype(BF16)
        pp = _dot_nt(p_ref[...].astype(BF16), wpt_ref[...])
        x2_ref[...] = x1 + _sigmoid(gpre) * pp

        if gathers:
            @pl.when(i == nt - 1)
            def _():
                ag.finish()

    def tok(width):
        return pl.BlockSpec((tm, width), lambda i: (i, 0))

    def whole(shape):
        return pl.BlockSpec(shape, lambda i: (0,) * len(shape))

    hbm = pl.BlockSpec(memory_space=pl.ANY)
    operands = [x, p_l, wg, conv_k, norm_g, ln_g, ln_b, w_mix, b_mix, ple_g]
    in_specs = [
        tok(D_MODEL), tok(PLE_DIM), hbm,
        whole((8, WIDTH_B)), whole((1, D_MODEL)), whole((1, WIDTH_A)), whole((1, WIDTH_A)),
        whole((HEADS_A, CHUNK, CHUNK)), whole((HEADS_A, CHUNK, HEAD_DIM)), whole((1, D_MODEL)),
    ]
    out_specs = [tok(PROJ_WIDTH), tok(D_MODEL), tok(D_MODEL), tok(D_MODEL), tok(D_MODEL), tok(D_MODEL), tok(D_MODEL)]
    out_shape = [
        jax.ShapeDtypeStruct((t, PROJ_WIDTH), BF16),
        jax.ShapeDtypeStruct((t, D_MODEL), BF16),
        jax.ShapeDtypeStruct((t, D_MODEL), BF16),
        jax.ShapeDtypeStruct((t, D_MODEL), BF16),
        jax.ShapeDtypeStruct((t, D_MODEL), BF16),
        jax.ShapeDtypeStruct((t, D_MODEL), F32),
        jax.ShapeDtypeStruct((t, D_MODEL), F32),
    ]
    scratch_shapes = [
        pltpu.VMEM((PROJ_WIDTH, D_MODEL), BF16),
        pltpu.VMEM((D_MODEL, D_MODEL), BF16),
        pltpu.VMEM((D_MODEL, D_MODEL), BF16),
        pltpu.VMEM((D_MODEL, PLE_DIM), BF16),
        pltpu.VMEM((tm, WIDTH_A), BF16),
        pltpu.VMEM((tm, WIDTH_A), F32),
        pltpu.VMEM((8, WIDTH_B), F32),
        pltpu.SemaphoreType.DMA((N_WEIGHT_COPIES,)),
    ]
    if gathers:
        operands.append(next_shard)
        in_specs.append(hbm)
        out_specs.append(hbm)
        out_shape.append(jax.ShapeDtypeStruct((N_DEV * ROWS_LAYER, D_MODEL), BF16))
        scratch_shapes += list(GATHER_SEMS)

    return pl.pallas_call(
        body,
        name=f"layer{layer}_forward",
        grid=(nt,),
        in_specs=in_specs,
        out_specs=out_specs,
        out_shape=out_shape,
        scratch_shapes=scratch_shapes,
        compiler_params=pltpu.CompilerParams(dimension_semantics=("arbitrary",), vmem_limit_bytes=56 * MIB),
    )(*operands)


def _loss_head(x2, target, final_g):
    t = x2.shape[0]
    tm = _tile(t, 512)
    nt = t // tm

    def body(x_ref, tgt_ref, g_ref, dx_ref, head_ref, loss_acc, dg_acc):
        i = pl.program_id(0)

        @pl.when(i == 0)
        def _():
            loss_acc[...] = jnp.zeros_like(loss_acc)
            dg_acc[...] = jnp.zeros_like(dg_acc)

        xv = x_ref[...]
        g = g_ref[...]
        rstd = lax.rsqrt(jnp.mean(xv * xv, axis=-1, keepdims=True) + EPS)
        xhat = xv * rstd
        err = xhat * g - tgt_ref[...]
        loss_acc[...] += _colsum8(err * err)
        dy = err * (1.0 / D_MODEL)
        dg_acc[...] += _colsum8(dy * xhat)
        dxhat = dy * g
        dx_ref[...] = rstd * (dxhat - xhat * jnp.mean(dxhat * xhat, axis=-1, keepdims=True))

        @pl.when(i == nt - 1)
        def _():
            total = jnp.sum(loss_acc[...]) * (0.5 / D_MODEL)
            rows = lax.broadcasted_iota(jnp.int32, (SMALL_ROWS, D_MODEL), 0)
            lanes = lax.broadcasted_iota(jnp.int32, (SMALL_ROWS, D_MODEL), 1)
            head_ref[...] = jnp.where((rows == HEAD_LOSS) & (lanes == 0), total, 0.0)
            head_ref[HEAD_FINAL:HEAD_FINAL + 1, :] = jnp.sum(dg_acc[...], axis=0, keepdims=True)

    return pl.pallas_call(
        body,
        name="loss_head",
        grid=(nt,),
        in_specs=[
            pl.BlockSpec((tm, D_MODEL), lambda i: (i, 0)),
            pl.BlockSpec((tm, D_MODEL), lambda i: (i, 0)),
            pl.BlockSpec((1, D_MODEL), lambda i: (0, 0)),
        ],
        out_specs=[
            pl.BlockSpec((tm, D_MODEL), lambda i: (i, 0)),
            pl.BlockSpec((SMALL_ROWS, D_MODEL), lambda i: (0, 0)),
        ],
        out_shape=[
            jax.ShapeDtypeStruct((t, D_MODEL), F32),
            jax.ShapeDtypeStruct((SMALL_ROWS, D_MODEL), F32),
        ],
        scratch_shapes=[pltpu.VMEM((8, D_MODEL), F32), pltpu.VMEM((8, D_MODEL), F32)],
        compiler_params=pltpu.CompilerParams(dimension_semantics=("arbitrary",), vmem_limit_bytes=32 * MIB),
    )(x2, target, final_g)


class _DirectScatter:
    def __init__(self, pack_ref, pieces_ref, send_sems, recv_sems, local_sem):
        x, y, c = lax.axis_index("x"), lax.axis_index("y"), lax.axis_index("c")
        me = 4 * x + 2 * y + c
        self.copies = []
        for k in range(N_DEV - 1):
            fx, fy, fc = ((k + 1) >> 2) & 1, ((k + 1) >> 1) & 1, (k + 1) & 1
            tx, ty, tc = x ^ fx, y ^ fy, c ^ fc
            self.copies.append(
                pltpu.make_async_remote_copy(
                    src_ref=pack_ref.at[4 * tx + 2 * ty + tc], dst_ref=pieces_ref.at[me],
                    send_sem=send_sems.at[k], recv_sem=recv_sems.at[k],
                    device_id=(tx, ty, tc), device_id_type=MESH,
                )
            )
        self.mine = pltpu.make_async_copy(pack_ref.at[me], pieces_ref.at[me], local_sem)

    def start(self):
        self.mine.start()
        for cp in self.copies:
            cp.start()

    def finish(self):
        for cp in self.copies:
            cp.wait_recv()
        for cp in self.copies:
            cp.wait_send()
        self.mine.wait()


SCATTER_SEMS = [pltpu.SemaphoreType.DMA((N_DEV - 1,)), pltpu.SemaphoreType.DMA((N_DEV - 1,)), pltpu.SemaphoreType.DMA]


def _backward_layer(layer, dx2, x_in, x1, proj, gpre, p_l, wg, conv_k, norm_g, ln_g, ln_b,
                    w_mix, w_mix_t, b_mix, ple_g):
    t = x_in.shape[0]
    tm = _tile(t, 256)
    nt = t // tm
    n_chunks = tm // CHUNK
    halo_rows = 16

    def body(dx2_ref, xin_ref, x1_ref, proj_ref, halo_ref, gpre_ref, p_ref, wg_ref, cw_ref,
             ng_ref, lng_ref, lnb_ref, wm_ref, wmt_ref, bm_ref, pg_ref,
             dxin_ref, dproj_ref, dx1_ref, dgpre_ref, dpp_ref, small_ref, dws_ref,
             w_in_t, w_out, w_gate, wpt_ref, vln_s, mixed_s, dmix_s, dvln_s, carry_s,
             ng_acc, pg_acc, lng_acc, lnb_acc, cw_acc, dbm_ref, sems):
        i = pl.program_id(0)
        tile = nt - 1 - i

        @pl.when(i == 0)
        def _():
            copies = _weight_copies(wg_ref, w_in_t, w_out, w_gate, wpt_ref, sems)
            for cp in copies:
                cp.start()
            carry_s[...] = jnp.zeros_like(carry_s)
            ng_acc[...] = jnp.zeros_like(ng_acc)
            pg_acc[...] = jnp.zeros_like(pg_acc)
            lng_acc[...] = jnp.zeros_like(lng_acc)
            lnb_acc[...] = jnp.zeros_like(lnb_acc)
            cw_acc[...] = jnp.zeros_like(cw_acc)
            dws_ref[...] = jnp.zeros_like(dws_ref)
            dbm_ref[...] = jnp.zeros_like(dbm_ref)
            for cp in copies:
                cp.wait()

        dx2v = dx2_ref[...]
        gate = _sigmoid(gpre_ref[...].astype(F32))
        pp = _dot_nt(p_ref[...].astype(BF16), wpt_ref[...])
        dpp_ref[...] = (dx2v * gate).astype(BF16)
        dgpre = (dx2v * pp * gate * (1.0 - gate)).astype(BF16)
        dgpre_ref[...] = dgpre
        dr = _dot_nt(dgpre, w_gate[...])
        x1v = x1_ref[...]
        rstd1 = lax.rsqrt(jnp.mean(x1v * x1v, axis=-1, keepdims=True) + EPS)
        xhat1 = x1v * rstd1
        pg_acc[...] += _colsum8(dr * xhat1)
        dxh = dr * pg_ref[...]
        dx1 = dx2v + rstd1 * (dxh - xhat1 * jnp.mean(dxh * xhat1, axis=-1, keepdims=True))
        dx1b = dx1.astype(BF16)
        dx1_ref[...] = dx1b

        dcat = _dot_nt(dx1b, w_out[...])
        dca = dcat[:, 0:512]
        dcb = dcat[:, 512:1024]

        u = proj_ref[:, 0:512].astype(F32)
        v = proj_ref[:, 512:1024].astype(F32)
        za = proj_ref[:, 1024:1536].astype(F32)
        mu = jnp.mean(v, axis=-1, keepdims=True)
        vc = v - mu
        var = jnp.mean(vc * vc, axis=-1, keepdims=True)
        rs = lax.rsqrt(var + EPS)
        vhat = vc * rs
        lng = lng_ref[...]
        vln_s[...] = (vhat * lng + lnb_ref[...]).astype(BF16)
        for ci in range(n_chunks):
            rows = pl.ds(ci * CHUNK, CHUNK)
            for h in range(HEADS_A):
                cols = pl.ds(h * HEAD_DIM, HEAD_DIM)
                mixed_s[rows, cols] = _dot(wm_ref[h], vln_s[rows, cols]) + bm_ref[h]
        mixed = mixed_s[...]
        sga = _sigmoid(za)
        sa = za * sga
        dsa = sga * (1.0 + za * (1.0 - sga))
        dproj_ref[:, 0:512] = (dca * mixed * sa).astype(BF16)
        dmix = dca * u * sa
        dproj_ref[:, 1024:1536] = (dca * u * mixed * dsa).astype(BF16)
        dmix_s[...] = dmix.astype(BF16)
        dbm_acc = jnp.zeros((CHUNK, WIDTH_A), F32)
        for ci in range(n_chunks):
            rows = pl.ds(ci * CHUNK, CHUNK)
            dbm_acc = dbm_acc + dmix[ci * CHUNK:(ci + 1) * CHUNK, :]
            for h in range(HEADS_A):
                cols = pl.ds(h * HEAD_DIM, HEAD_DIM)
                dvln_s[rows, cols] = _dot(wmt_ref[h], dmix_s[rows, cols])
                dws_ref[:, cols] += _dot_nt(dmix_s[rows, cols], vln_s[rows, cols])
        dbm_ref[...] += dbm_acc
        dvln = dvln_s[...]
        lng_acc[...] += _colsum8(dvln * vhat)
        lnb_acc[...] += _colsum8(dvln)
        dvh = dvln * lng
        dv = rs * (dvh - jnp.mean(dvh, axis=-1, keepdims=True) - vhat * jnp.mean(dvh * vhat, axis=-1, keepdims=True))
        dproj_ref[:, 512:1024] = dv.astype(BF16)

        hb = proj_ref[:, 1536:2048].astype(F32)
        gb = proj_ref[:, 2048:2560].astype(F32)
        gc = proj_ref[:, 2560:3072].astype(F32)
        zb = proj_ref[:, 3072:3584].astype(F32)
        xc = gc * hb
        prev = halo_ref[:, 2560:3072].astype(F32) * halo_ref[:, 1536:2048].astype(F32)
        prev = jnp.where(tile > 0, prev, 0.0)
        row = lax.broadcasted_iota(jnp.int32, (tm, WIDTH_B), 0)
        p1 = prev[halo_rows - 1:halo_rows, :]
        p2 = prev[halo_rows - 2:halo_rows - 1, :]
        xc_m1 = jnp.where(row == 0, p1, pltpu.roll(xc, 1, 0))
        xc_m2 = jnp.where(row == 0, p2, jnp.where(row == 1, p1, pltpu.roll(xc, 2, 0)))
        cw = cw_ref[...]
        yc = cw[0:1, :] * xc_m2 + cw[1:2, :] * xc_m1 + cw[2:3, :] * xc
        sgb = _sigmoid(zb)
        sb = zb * sgb
        dsb = sgb * (1.0 + zb * (1.0 - sgb))
        dproj_ref[:, 2048:2560] = (dcb * yc * sb).astype(BF16)
        dyc = dcb * gb * sb
        dproj_ref[:, 3072:3584] = (dcb * gb * yc * dsb).astype(BF16)
        nxt = carry_s[...]
        dyc_p1 = jnp.where(row == tm - 1, nxt[0:1, :], pltpu.roll(dyc, tm - 1, 0))
        dyc_p2 = jnp.where(row == tm - 1, nxt[1:2, :], jnp.where(row == tm - 2, nxt[0:1, :], pltpu.roll(dyc, tm - 2, 0)))
        carry_s[...] = dyc[0:8, :]
        dxc = cw[2:3, :] * dyc + cw[1:2, :] * dyc_p1 + cw[0:1, :] * dyc_p2
        cw_acc[0] += _colsum8(dyc * xc_m2)
        cw_acc[1] += _colsum8(dyc * xc_m1)
        cw_acc[2] += _colsum8(dyc * xc)
        dproj_ref[:, 1536:2048] = (dxc * gc).astype(BF16)
        dproj_ref[:, 2560:3072] = (dxc * hb).astype(BF16)

        dhn = _dot(dproj_ref[...], w_in_t[...])
        xv = xin_ref[...]
        rstd0 = lax.rsqrt(jnp.mean(xv * xv, axis=-1, keepdims=True) + EPS)
        xhat0 = xv * rstd0
        ng_acc[...] += _colsum8(dhn * xhat0)
        dxh0 = dhn * ng_ref[...]
        dxin_ref[...] = dx1 + rstd0 * (dxh0 - xhat0 * jnp.mean(dxh0 * xhat0, axis=-1, keepdims=True))

        @pl.when(i == nt - 1)
        def _():
            small_ref[...] = jnp.zeros_like(small_ref)
            small_ref[SMALL_NORM:SMALL_NORM + 1, :] = jnp.sum(ng_acc[...], axis=0, keepdims=True)
            small_ref[SMALL_PLE:SMALL_PLE + 1, :] = jnp.sum(pg_acc[...], axis=0, keepdims=True)
            small_ref[SMALL_LN:SMALL_LN + 1, 0:WIDTH_A] = jnp.sum(lng_acc[...], axis=0, keepdims=True)
            small_ref[SMALL_LN:SMALL_LN + 1, WIDTH_A:2 * WIDTH_A] = jnp.sum(lnb_acc[...], axis=0, keepdims=True)
            for h in range(HEADS_A):
                cols = pl.ds(h * HEAD_DIM, HEAD_DIM)
                small_ref[SMALL_BS:SMALL_BS + 1, cols] = jnp.sum(jnp.transpose(dbm_ref[:, cols]), axis=0, keepdims=True)
            for k in range(3):
                small_ref[SMALL_CONV + k:SMALL_CONV + k + 1, 0:WIDTH_B] = jnp.sum(cw_acc[k], axis=0, keepdims=True)

    def tok(width):
        return pl.BlockSpec((tm, width), lambda i: (nt - 1 - i, 0))

    def whole(shape):
        return pl.BlockSpec(shape, lambda i: (0,) * len(shape))

    halo_spec = pl.BlockSpec(
        (halo_rows, PROJ_WIDTH), lambda i: (jnp.maximum((nt - 1 - i) * (tm // halo_rows) - 1, 0), 0)
    )
    hbm = pl.BlockSpec(memory_space=pl.ANY)
    operands = [dx2, x_in, x1, proj, proj, gpre, p_l, wg, conv_k, norm_g, ln_g, ln_b, w_mix, w_mix_t, b_mix, ple_g]
    in_specs = [
        tok(D_MODEL), tok(D_MODEL), tok(D_MODEL), tok(PROJ_WIDTH), halo_spec, tok(D_MODEL), tok(PLE_DIM), hbm,
        whole((8, WIDTH_B)), whole((1, D_MODEL)), whole((1, WIDTH_A)), whole((1, WIDTH_A)),
        whole((HEADS_A, CHUNK, CHUNK)), whole((HEADS_A, CHUNK, CHUNK)), whole((HEADS_A, CHUNK, HEAD_DIM)),
        whole((1, D_MODEL)),
    ]
    out_specs = [
        tok(D_MODEL), tok(PROJ_WIDTH), tok(D_MODEL), tok(D_MODEL), tok(D_MODEL),
        whole((SMALL_ROWS, D_MODEL)), whole((CHUNK, WIDTH_A)),
    ]
    out_shape = [
        jax.ShapeDtypeStruct((t, D_MODEL), F32),
        jax.ShapeDtypeStruct((t, PROJ_WIDTH), BF16),
        jax.ShapeDtypeStruct((t, D_MODEL), BF16),
        jax.ShapeDtypeStruct((t, D_MODEL), BF16),
        jax.ShapeDtypeStruct((t, D_MODEL), BF16),
        jax.ShapeDtypeStruct((SMALL_ROWS, D_MODEL), F32),
        jax.ShapeDtypeStruct((CHUNK, WIDTH_A), F32),
    ]
    scratch_shapes = [
        pltpu.VMEM((PROJ_WIDTH, D_MODEL), BF16),
        pltpu.VMEM((D_MODEL, D_MODEL), BF16),
        pltpu.VMEM((D_MODEL, D_MODEL), BF16),
        pltpu.VMEM((D_MODEL, PLE_DIM), BF16),
        pltpu.VMEM((tm, WIDTH_A), BF16),
        pltpu.VMEM((tm, WIDTH_A), F32),
        pltpu.VMEM((tm, WIDTH_A), BF16),
        pltpu.VMEM((tm, WIDTH_A), F32),
        pltpu.VMEM((8, WIDTH_B), F32),
        pltpu.VMEM((8, D_MODEL), F32),
        pltpu.VMEM((8, D_MODEL), F32),
        pltpu.VMEM((8, WIDTH_A), F32),
        pltpu.VMEM((8, WIDTH_A), F32),
        pltpu.VMEM((3, 8, WIDTH_B), F32),
        pltpu.VMEM((CHUNK, WIDTH_A), F32),
        pltpu.SemaphoreType.DMA((N_WEIGHT_COPIES,)),
    ]

    return pl.pallas_call(
        body,
        name=f"layer{layer}_backward",
        grid=(nt,),
        in_specs=in_specs,
        out_specs=out_specs,
        out_shape=out_shape,
        scratch_shapes=scratch_shapes,
        compiler_params=pltpu.CompilerParams(dimension_semantics=("arbitrary",), vmem_limit_bytes=56 * MIB),
    )(*operands)


def _sum_pieces(layer, pieces):
    rows, n = pieces.shape[1], pieces.shape[2]

    def body(p_ref, out_ref):
        total = p_ref[0].astype(F32)
        for j in range(1, N_DEV):
            total = total + p_ref[j].astype(F32)
        out_ref[...] = total

    return pl.pallas_call(
        body,
        name=f"layer{layer}_grad_sum",
        out_shape=jax.ShapeDtypeStruct((rows, n), F32),
        in_specs=[pl.BlockSpec(memory_space=pltpu.VMEM)],
        out_specs=pl.BlockSpec(memory_space=pltpu.VMEM),
        compiler_params=pltpu.CompilerParams(vmem_limit_bytes=32 * MIB),
    )(pieces)


def _weight_grads(layer, dproj, hn, cat, dx1, r, dgpre, dpp, p_l, scatter_pack=None):
    t = hn.shape[0]
    tk = _tile(t, 512)
    nt = t // tk
    in_blocks = PROJ_WIDTH // 512
    scatters = scatter_pack is not None

    def body(*refs):
        (dproj_ref, hn_ref, cat_ref, dx1_ref, r_ref, dgpre_ref, dpp_ref, p_ref) = refs[:8]
        refs = refs[8:]
        if scatters:
            prior_ref, refs = refs[0], refs[1:]
        pack_ref, refs = refs[0], refs[1:]
        if scatters:
            pieces_ref, refs = refs[0], refs[1:]
        (acc_in, acc_out, acc_gate, acc_proj, stage, sems) = refs[:6]
        i = pl.program_id(0)
        if scatters:
            scatter = _DirectScatter(prior_ref, pieces_ref, *refs[6:9])

            @pl.when(i == 0)
            def _():
                scatter.start()

        @pl.when(i == 0)
        def _():
            acc_in[...] = jnp.zeros_like(acc_in)
            acc_out[...] = jnp.zeros_like(acc_out)
            acc_gate[...] = jnp.zeros_like(acc_gate)
            acc_proj[...] = jnp.zeros_like(acc_proj)

        hnv = hn_ref[...]
        for b in range(in_blocks):
            acc_in[pl.ds(b * 512, 512), :] += _dot_tn(dproj_ref[:, b * 512:(b + 1) * 512], hnv)
        dx1v = dx1_ref[...]
        dgv = dgpre_ref[...]
        for b in range(D_MODEL // 512):
            acc_out[pl.ds(b * 512, 512), :] += _dot_tn(cat_ref[:, b * 512:(b + 1) * 512], dx1v)
            acc_gate[pl.ds(b * 512, 512), :] += _dot_tn(r_ref[:, b * 512:(b + 1) * 512], dgv)
        pv = p_ref[...].astype(BF16)
        for b in range(D_MODEL // 512):
            acc_proj[pl.ds(b * 512, 512), :] += _dot_tn(dpp_ref[:, b * 512:(b + 1) * 512], pv)

        @pl.when(i == nt - 1)
        def _():
            def out_copy(s):
                return pltpu.make_async_copy(stage.at[s % 2], pack_ref.at[s], sems.at[s % 2])

            for s in range(N_DEV):
                if s >= 2:
                    out_copy(s - 2).wait()
                buf = stage.at[s % 2]
                buf[pl.ds(OFF_IN, ROWS_IN), :] = acc_in[pl.ds(s * ROWS_IN, ROWS_IN), :].astype(BF16)
                buf[pl.ds(OFF_OUT, ROWS_OUT), :] = acc_out[pl.ds(s * ROWS_OUT, ROWS_OUT), :].astype(BF16)
                buf[pl.ds(OFF_GATE, ROWS_GATE), :] = acc_gate[pl.ds(s * ROWS_GATE, ROWS_GATE), :].astype(BF16)
                for j in range(D_MODEL // PLE_DIM):
                    buf[pl.ds(OFF_PROJ, ROWS_PROJ), pl.ds(j * PLE_DIM, PLE_DIM)] = acc_proj[
                        pl.ds(s * ROWS_OUT + j * ROWS_PROJ, ROWS_PROJ), :
                    ].astype(BF16)
                out_copy(s).start()
            out_copy(N_DEV - 2).wait()
            out_copy(N_DEV - 1).wait()
            if scatters:
                scatter.finish()

    def tok(width):
        return pl.BlockSpec((tk, width), lambda i: (i, 0))

    hbm = pl.BlockSpec(memory_space=pl.ANY)
    pack_shape = jax.ShapeDtypeStruct((N_DEV, ROWS_GRAD, D_MODEL), BF16)
    operands = [dproj, hn, cat, dx1, r, dgpre, dpp, p_l]
    in_specs = [tok(PROJ_WIDTH), tok(D_MODEL), tok(D_MODEL), tok(D_MODEL), tok(D_MODEL), tok(D_MODEL), tok(D_MODEL), tok(PLE_DIM)]
    out_specs, out_shape = [hbm], [pack_shape]
    scratch_shapes = [
        pltpu.VMEM((PROJ_WIDTH, D_MODEL), F32),
        pltpu.VMEM((D_MODEL, D_MODEL), F32),
        pltpu.VMEM((D_MODEL, D_MODEL), F32),
        pltpu.VMEM((D_MODEL, PLE_DIM), F32),
        pltpu.VMEM((2, ROWS_GRAD, D_MODEL), BF16),
        pltpu.SemaphoreType.DMA((2,)),
    ]
    if scatters:
        operands.append(scatter_pack)
        in_specs.append(hbm)
        out_specs.append(hbm)
        out_shape.append(pack_shape)
        scratch_shapes += list(SCATTER_SEMS)

    return pl.pallas_call(
        body,
        name=f"layer{layer}_weight_grads",
        grid=(nt,),
        in_specs=in_specs,
        out_specs=out_specs,
        out_shape=out_shape,
        scratch_shapes=scratch_shapes,
        compiler_params=pltpu.CompilerParams(dimension_semantics=("arbitrary",), vmem_limit_bytes=58 * MIB),
    )(*operands)


def _reduce_scatter_all_reduce(layer, pack, smalls, head, dws):
    rows, n = pack.shape[1], pack.shape[2]
    assert DEPTH * WIDTH_A == D_MODEL and n == D_MODEL

    def body(g_ref, *refs):
        small_refs, refs = refs[:DEPTH], refs[DEPTH:]
        head_ref, refs = refs[0], refs[1:]
        dws_refs, refs = refs[:DEPTH], refs[DEPTH:]
        (out_ref, total_ref, r1, a_s, r2, sp, sr1, sq, send1, recv1, send2, recv2, ssend, srecv) = refs
        x, y, c = lax.axis_index("x"), lax.axis_index("y"), lax.axis_index("c")
        sibling = (x, y, 1 - c)
        chip = 2 * x + y
        flips = [(1, 0), (0, 1), (1, 1)]

        for l in range(DEPTH):
            sp[l * SMALL_ROWS:(l + 1) * SMALL_ROWS, :] = small_refs[l][...]
            sp[TOTAL_WS:TOTAL_ROWS, l * WIDTH_A:(l + 1) * WIDTH_A] = dws_refs[l][...]
        sp[TOTAL_HEAD:TOTAL_WS, :] = head_ref[...]

        small_pair = pltpu.make_async_remote_copy(
            src_ref=sp, dst_ref=sr1, send_sem=ssend.at[0], recv_sem=srecv.at[0], device_id=sibling, device_id_type=MESH
        )

        def to_sibling(j):
            return pltpu.make_async_remote_copy(
                src_ref=g_ref.at[2 * j + 1 - c], dst_ref=r1.at[j], send_sem=send1.at[j], recv_sem=recv1.at[j],
                device_id=sibling, device_id_type=MESH,
            )

        first = [to_sibling(j) for j in range(4)]
        small_pair.start()
        for cp in first:
            cp.start()

        small_pair.wait_recv()
        sq[chip] = sp[...] + sr1[...]
        small_chips = [
            pltpu.make_async_remote_copy(
                src_ref=sq.at[chip], dst_ref=sq.at[chip], send_sem=ssend.at[1 + k], recv_sem=srecv.at[1 + k],
                device_id=(x ^ fx, y ^ fy, c), device_id_type=MESH,
            )
            for k, (fx, fy) in enumerate(flips)
        ]
        for cp in small_chips:
            cp.start()

        for j in range(4):
            first[j].wait_recv()
            a_s[j] = (g_ref[2 * j + c].astype(F32) + r1[j].astype(F32)).astype(BF16)

        def to_chip(k):
            fx, fy = flips[k]
            tx, ty = x ^ fx, y ^ fy
            return pltpu.make_async_remote_copy(
                src_ref=a_s.at[2 * tx + ty], dst_ref=r2.at[k], send_sem=send2.at[k], recv_sem=recv2.at[k],
                device_id=(tx, ty, c), device_id_type=MESH,
            )

        second = [to_chip(k) for k in range(3)]
        for cp in second:
            cp.start()
        total = g_ref[2 * chip + c].astype(F32) + r1[chip].astype(F32)
        for cp in small_chips:
            cp.wait_recv()
        total_ref[...] = ((sq[0] + sq[1]) + sq[2]) + sq[3]
        for k in range(3):
            second[k].wait_recv()
            total = total + r2[k].astype(F32)
        out_ref[...] = total
        small_pair.wait_send()
        for cp in first + second + small_chips:
            cp.wait_send()

    vmem = pl.BlockSpec(memory_space=pltpu.VMEM)
    return pl.pallas_call(
        body,
        name=f"layer{layer}_grad_reduce_scatter",
        out_shape=[jax.ShapeDtypeStruct((rows, n), F32), jax.ShapeDtypeStruct((TOTAL_ROWS, D_MODEL), F32)],
        in_specs=[vmem] * (2 + 2 * DEPTH),
        out_specs=[vmem, vmem],
        scratch_shapes=[
            pltpu.VMEM((4, rows, n), BF16),
            pltpu.VMEM((4, rows, n), BF16),
            pltpu.VMEM((3, rows, n), BF16),
            pltpu.VMEM((TOTAL_ROWS, D_MODEL), F32),
            pltpu.VMEM((TOTAL_ROWS, D_MODEL), F32),
            pltpu.VMEM((4, TOTAL_ROWS, D_MODEL), F32),
            pltpu.SemaphoreType.DMA((4,)),
            pltpu.SemaphoreType.DMA((4,)),
            pltpu.SemaphoreType.DMA((3,)),
            pltpu.SemaphoreType.DMA((3,)),
            pltpu.SemaphoreType.DMA((4,)),
            pltpu.SemaphoreType.DMA((4,)),
        ],
        compiler_params=pltpu.CompilerParams(vmem_limit_bytes=48 * MIB),
    )(pack, *smalls, head, *dws)


def _adam_step(w, g, m, v):
    m = ADAM_B1 * m + (1.0 - ADAM_B1) * g
    v = ADAM_B2 * v + (1.0 - ADAM_B2) * (g * g)
    m_hat = m / (1.0 - ADAM_B1 ** ADAM_STEP)
    v_hat = v / (1.0 - ADAM_B2 ** ADAM_STEP)
    return -ADAM_LR * (m_hat / (jnp.sqrt(v_hat) + ADAM_EPS) + ADAM_WD * w), m, v


def _adamw_rows(name, reduced, row_off, states):
    n = len(states)

    def body(*refs):
        red = refs[:DEPTH]
        ins = refs[DEPTH:DEPTH + 3 * n]
        outs = refs[DEPTH + 3 * n:]
        for k in range(n):
            w_ref, m_ref, v_ref = ins[3 * k:3 * k + 3]
            g_ref, d_ref, nm_ref, nv_ref = outs[4 * k:4 * k + 4]
            r = w_ref.shape[1]
            for l in range(DEPTH):
                g = red[l][row_off[k]:row_off[k] + r, :]
                d, m, v = _adam_step(w_ref[l], g, m_ref[l], v_ref[l])
                g_ref[l] = g
                d_ref[l] = d
                nm_ref[l] = m
                nv_ref[l] = v

    flat = [a for st in states for a in st]
    out_shape = []
    for w, _, _ in states:
        out_shape += [jax.ShapeDtypeStruct(w.shape, F32)] * 4
    vmem = pl.BlockSpec(memory_space=pltpu.VMEM)
    outs = pl.pallas_call(
        body,
        name=name,
        out_shape=out_shape,
        in_specs=[vmem] * (DEPTH + len(flat)),
        out_specs=[vmem] * len(out_shape),
        compiler_params=pltpu.CompilerParams(vmem_limit_bytes=48 * MIB),
    )(*reduced, *flat)
    return [tuple(outs[4 * k:4 * k + 4]) for k in range(n)]


def _adamw_small(total, g_conv, g_proj, st):
    names = ["norm_g", "ple_norm_g", "ln_v_g", "ln_v_b", "b_s", "w_s", "final_g", "conv_w", "w_ple_proj"]
    cut = names[:7]

    def body(total_ref, gconv_ref, gproj_ref, *refs):
        ins = {nm: refs[3 * k:3 * k + 3] for k, nm in enumerate(names)}
        outs, pos = {}, 3 * len(names)
        for nm in names:
            cnt = 4 if nm in cut else 3
            outs[nm] = refs[pos:pos + cnt]
            pos += cnt

        def update(nm, idx, g):
            w_ref, m_ref, v_ref = ins[nm]
            d, m, v = _adam_step(w_ref[idx], g, m_ref[idx], v_ref[idx])
            o = outs[nm]
            if nm in cut:
                o[0][idx] = g
                o = o[1:]
            o[0][idx] = d
            o[1][idx] = m
            o[2][idx] = v

        tril = (lax.broadcasted_iota(jnp.int32, (CHUNK, CHUNK), 0) >= lax.broadcasted_iota(jnp.int32, (CHUNK, CHUNK), 1))
        for l in range(DEPTH):
            base = l * SMALL_ROWS
            row = (slice(l, l + 1), slice(None))
            update("norm_g", row, total_ref[base + SMALL_NORM:base + SMALL_NORM + 1, :])
            update("ple_norm_g", row, total_ref[base + SMALL_PLE:base + SMALL_PLE + 1, :])
            update("ln_v_g", row, total_ref[base + SMALL_LN:base + SMALL_LN + 1, 0:WIDTH_A])
            update("ln_v_b", row, total_ref[base + SMALL_LN:base + SMALL_LN + 1, WIDTH_A:2 * WIDTH_A])
            for h in range(HEADS_A):
                update("b_s", (l, slice(h, h + 1), slice(None)),
                       total_ref[base + SMALL_BS:base + SMALL_BS + 1, h * HEAD_DIM:(h + 1) * HEAD_DIM])
                lanes = slice(l * WIDTH_A + h * CHUNK, l * WIDTH_A + (h + 1) * CHUNK)
                update("w_s", (l, h), jnp.where(tril, total_ref[TOTAL_WS:TOTAL_ROWS, lanes], 0.0))
        update("final_g", (slice(None), slice(None)), total_ref[TOTAL_HEAD + HEAD_FINAL:TOTAL_HEAD + HEAD_FINAL + 1, :])
        update("conv_w", (slice(None),) * 3, gconv_ref[...])
        update("w_ple_proj", (slice(None),) * 3, gproj_ref[...])

    flat = [a for nm in names for a in st[nm]]
    out_shape = []
    for nm in names:
        out_shape += [jax.ShapeDtypeStruct(st[nm][0].shape, F32)] * (4 if nm in cut else 3)
    vmem = pl.BlockSpec(memory_space=pltpu.VMEM)
    outs = pl.pallas_call(
        body,
        name="adamw_small",
        out_shape=out_shape,
        in_specs=[vmem] * (3 + len(flat)),
        out_specs=[vmem] * len(out_shape),
        compiler_params=pltpu.CompilerParams(vmem_limit_bytes=32 * MIB),
    )(total, g_conv, g_proj, *flat)
    res, pos = {}, 0
    for nm in names:
        cnt = 4 if nm in cut else 3
        got = tuple(outs[pos:pos + cnt])
        res[nm] = got if nm in cut else ((g_conv if nm == "conv_w" else g_proj),) + got
        pos += cnt
    return res


def _split3_bf16(a):
    b1 = a.astype(BF16)
    r1 = a - b1.astype(F32)
    b2 = r1.astype(BF16)
    b3 = (r1 - b2.astype(F32)).astype(BF16)
    return b1, b2, b3


def _pack_weight_shard(w_in_l, w_out_l, w_gate_l, w_proj_l, conv_w_l):
    w_in_t = jnp.transpose(w_in_l).astype(BF16)
    proj_t = jnp.transpose(w_proj_l).astype(BF16)
    proj_rows = proj_t.reshape(D_MODEL // PLE_DIM, ROWS_PROJ, PLE_DIM).transpose(1, 0, 2).reshape(ROWS_PROJ, D_MODEL)
    conv_parts = jnp.concatenate([b.reshape(-1) for b in _split3_bf16(conv_w_l)])
    conv_rows = jnp.concatenate([conv_parts, jnp.zeros((ROWS_CONV * D_MODEL - conv_parts.shape[0],), BF16)])
    return jnp.concatenate(
        [w_in_t, w_out_l.astype(BF16), w_gate_l.astype(BF16), proj_rows, conv_rows.reshape(ROWS_CONV, D_MODEL)], axis=0
    )


def _unpack_conv(wg):
    per_dev = wg.reshape(N_DEV, ROWS_LAYER, D_MODEL)
    n_conv = (WIDTH_B // N_DEV) * 3
    conv_parts = per_dev[:, OFF_CONV].astype(F32)[:, :3 * n_conv].reshape(N_DEV, 3, n_conv)
    conv = (conv_parts[:, 0] + conv_parts[:, 1]) + conv_parts[:, 2]
    conv_k = jnp.transpose(conv.reshape(WIDTH_B, 3))
    conv_k = jnp.concatenate([conv_k, jnp.zeros((5, WIDTH_B), F32)], axis=0)
    return conv_k


def _unpack_grad_proj(red):
    proj_rows = red[OFF_PROJ:OFF_PROJ + ROWS_PROJ]
    proj_t = proj_rows.reshape(ROWS_PROJ, D_MODEL // PLE_DIM, PLE_DIM).transpose(1, 0, 2).reshape(ROWS_OUT, PLE_DIM)
    return jnp.transpose(proj_t)


def kernel(x, p, norm_g, w_in, ln_v_g, ln_v_b, w_s, b_s, conv_w, w_out, ple_norm_g, w_ple_gate, w_ple_proj, final_g, loss_target, m_norm_g, m_w_in, m_ln_v_g, m_ln_v_b, m_w_s, m_b_s, m_conv_w, m_w_out, m_ple_norm_g, m_w_ple_gate, m_w_ple_proj, m_final_g, v_norm_g, v_w_in, v_ln_v_g, v_ln_v_b, v_w_s, v_b_s, v_conv_w, v_w_out, v_ple_norm_g, v_w_ple_gate, v_w_ple_proj, v_final_g):
    me = 4 * lax.axis_index("x") + 2 * lax.axis_index("y") + lax.axis_index("c")
    xs = x[0]
    target = loss_target[0]

    shards = [_pack_weight_shard(w_in[l], w_out[l], w_ple_gate[l], w_ple_proj[l], conv_w[l]) for l in range(DEPTH)]
    tril = jnp.tril(jnp.ones((CHUNK, CHUNK), F32))

    def consts(l, wg_l):
        conv_k = _unpack_conv(wg_l)
        w_mix = w_s[l] * tril[None]
        return dict(
            wg=wg_l, conv_k=conv_k,
            norm_g=norm_g[l].reshape(1, D_MODEL), ln_g=ln_v_g[l].reshape(1, WIDTH_A), ln_b=ln_v_b[l].reshape(1, WIDTH_A),
            w_mix=w_mix.astype(BF16), w_mix_t=jnp.swapaxes(w_mix, 1, 2).astype(BF16),
            b_mix=jnp.broadcast_to(b_s[l][:, :, None], (HEADS_A, CHUNK, HEAD_DIM)),
            ple_g=ple_norm_g[l].reshape(1, D_MODEL),
        )

    layer_consts = [consts(0, _all_gather_rows(shards[0]))]
    saved = []
    h = xs
    for l in range(DEPTH):
        k = layer_consts[l]
        outs = _forward_layer(
            l, h, p[l, 0], k["wg"], k["conv_k"], k["norm_g"], k["ln_g"], k["ln_b"], k["w_mix"], k["b_mix"],
            k["ple_g"], next_shard=shards[l + 1] if l + 1 < DEPTH else None)
        proj, hn, cat, r, gpre, x1, x2 = outs[:7]
        if l + 1 < DEPTH:
            layer_consts.append(consts(l + 1, outs[7]))
        saved.append(dict(x_in=h, proj=proj, hn=hn, cat=cat, r=r, gpre=gpre, x1=x1))
        h = x2

    dx, head = _loss_head(h, target, final_g.reshape(1, D_MODEL))

    smalls, dws = [None] * DEPTH, [None] * DEPTH
    reduced = [None] * DEPTH
    pending = None
    for l in reversed(range(DEPTH)):
        k, s = layer_consts[l], saved[l]
        dx, dproj, dx1, dgpre, dpp, smalls[l], dws[l] = _backward_layer(
            l, dx, s["x_in"], s["x1"], s["proj"], s["gpre"], p[l, 0], k["wg"], k["conv_k"],
            k["norm_g"], k["ln_g"], k["ln_b"], k["w_mix"], k["w_mix_t"], k["b_mix"], k["ple_g"])
        outs = _weight_grads(l, dproj, s["hn"], s["cat"], dx1, s["r"], dgpre, dpp, p[l, 0], scatter_pack=pending)
        if pending is not None:
            reduced[l + 1] = _sum_pieces(l + 1, outs[1])
        pending = outs[0]
    reduced[0], total = _reduce_scatter_all_reduce(0, pending, smalls, head, dws)
    grad_x = dx[None]
    loss = total[TOTAL_HEAD + HEAD_LOSS, 0]

    n_ch = WIDTH_B // N_DEV
    g_conv = jnp.stack([total[l * SMALL_ROWS + SMALL_CONV:l * SMALL_ROWS + SMALL_CONV + 3, 0:WIDTH_B] for l in range(DEPTH)], axis=1)
    g_conv = lax.dynamic_slice_in_dim(g_conv, me * n_ch, n_ch, axis=2)
    g_proj = jnp.stack([_unpack_grad_proj(reduced[l]) for l in range(DEPTH)])

    def t_in(a):
        return jnp.swapaxes(a, 1, 2)

    def t_conv(a):
        return jnp.transpose(a, (2, 0, 1))

    (r_in,) = _adamw_rows("adamw_w_in", reduced, [OFF_IN], [(t_in(w_in), t_in(m_w_in), t_in(v_w_in))])
    r_out, r_gate = _adamw_rows(
        "adamw_w_out_gate", reduced, [OFF_OUT, OFF_GATE],
        [(w_out, m_w_out, v_w_out), (w_ple_gate, m_w_ple_gate, v_w_ple_gate)])
    small = _adamw_small(total, g_conv, g_proj, dict(
        norm_g=(norm_g, m_norm_g, v_norm_g), ple_norm_g=(ple_norm_g, m_ple_norm_g, v_ple_norm_g),
        ln_v_g=(ln_v_g, m_ln_v_g, v_ln_v_g), ln_v_b=(ln_v_b, m_ln_v_b, v_ln_v_b),
        b_s=(b_s, m_b_s, v_b_s), w_s=(w_s, m_w_s, v_w_s),
        final_g=tuple(a.reshape(1, D_MODEL) for a in (final_g, m_final_g, v_final_g)),
        conv_w=(t_conv(conv_w), t_conv(m_conv_w), t_conv(v_conv_w)),
        w_ple_proj=(w_ple_proj, m_w_ple_proj, v_w_ple_proj),
    ))
    res = dict(small, w_in=tuple(t_in(a) for a in r_in), w_out=r_out, w_ple_gate=r_gate)
    res["final_g"] = tuple(a.reshape(D_MODEL) for a in res["final_g"])
    res["conv_w"] = tuple(jnp.transpose(a, (1, 2, 0)) for a in res["conv_w"])
    order = ["norm_g", "w_in", "ln_v_g", "ln_v_b", "w_s", "b_s", "conv_w", "w_out", "ple_norm_g", "w_ple_gate", "w_ple_proj", "final_g"]
    return (loss, grad_x, *[res[n][0] for n in order], *[res[n][1] for n in order],
            *[res[n][2] for n in order], *[res[n][3] for n in order])
```

```python
import jax
import jax.numpy as jnp
from jax import lax
from jax.experimental import pallas as pl
from jax.experimental.pallas import tpu as pltpu

F32 = jnp.float32
BF16 = jnp.bfloat16

D_MODEL = 1024
WIDTH_A = 512
WIDTH_B = 512
HEADS_A = 4
HEAD_DIM = 128
CHUNK = 128
PLE_DIM = 256
PROJ_WIDTH = 3584
DEPTH = 2
EPS = 1e-6
N_DEV = 8

ADAM_LR = 0.001
ADAM_B1 = 0.9
ADAM_B2 = 0.999
ADAM_EPS = 1e-08
ADAM_WD = 0.01
ADAM_STEP = 10

ROWS_IN = PROJ_WIDTH // N_DEV
ROWS_OUT = D_MODEL // N_DEV
ROWS_GATE = D_MODEL // N_DEV
ROWS_PROJ = (D_MODEL // N_DEV) * PLE_DIM // D_MODEL
ROWS_CONV = 16
OFF_IN = 0
OFF_OUT = OFF_IN + ROWS_IN
OFF_GATE = OFF_OUT + ROWS_OUT
OFF_PROJ = OFF_GATE + ROWS_GATE
OFF_CONV = OFF_PROJ + ROWS_PROJ
ROWS_GRAD = OFF_CONV
ROWS_LAYER = OFF_CONV + ROWS_CONV

SMALL_ROWS = 8
SMALL_NORM = 0
SMALL_PLE = 1
SMALL_LN = 2
SMALL_BS = 3
SMALL_CONV = 4
HEAD_FINAL = 0
HEAD_LOSS = 1
TOTAL_HEAD = DEPTH * SMALL_ROWS
TOTAL_WS = TOTAL_HEAD + SMALL_ROWS
TOTAL_ROWS = TOTAL_WS + CHUNK

MIB = 1024 * 1024
MESH = pl.DeviceIdType.MESH

NT_DIMS = (((1,), (1,)), ((), ()))
TN_DIMS = (((0,), (0,)), ((), ()))


def _dot(a, b):
    return jnp.dot(a, b, preferred_element_type=F32)


def _dot_nt(a, b):
    return lax.dot_general(a, b, NT_DIMS, preferred_element_type=F32)


def _dot_tn(a, b):
    return lax.dot_general(a, b, TN_DIMS, preferred_element_type=F32)


def _colsum8(a):
    rows, n = a.shape
    return jnp.sum(a.reshape(rows // 8, 8, n), axis=0)


def _sigmoid(z):
    return 1.0 / (1.0 + jnp.exp(-z))


def _tile(t, want):
    return want if t % want == 0 else t


class _TwoLevelGather:
    def __init__(self, x_ref, out_ref, m_per, send_sems, recv_sems, local_sem):
        x, y, c = lax.axis_index("x"), lax.axis_index("y"), lax.axis_index("c")
        self.c = c
        self.me, self.sibling = (x, y, c), (x, y, 1 - c)
        self.chips = [(1 - x, y), (x, 1 - y), (1 - x, 1 - y)]
        self.x_ref, self.out_ref, self.m_per = x_ref, out_ref, m_per
        self.send_sems, self.recv_sems = send_sems, recv_sems
        self.mine = pltpu.make_async_copy(x_ref, self.rows(*self.me), local_sem)

    def rows(self, px, py, pc):
        return self.out_ref.at[pl.ds((4 * px + 2 * py + pc) * self.m_per, self.m_per), :]

    def copy(self, k, block, to, src=None):
        return pltpu.make_async_remote_copy(
            src_ref=self.rows(*block) if src is None else src,
            dst_ref=self.rows(*block),
            send_sem=self.send_sems.at[k],
            recv_sem=self.recv_sems.at[k],
            device_id=to,
            device_id_type=MESH,
        )

    def first(self):
        out = [self.copy(0, self.me, self.sibling, src=self.x_ref)]
        return out + [self.copy(1 + j, self.me, (*chip, self.c), src=self.x_ref) for j, chip in enumerate(self.chips)]

    def passed(self):
        return [self.copy(4 + j, (*chip, self.c), self.sibling) for j, chip in enumerate(self.chips)]

    def start(self):
        self.mine.start()
        for cp in self.first():
            cp.start()

    def pass_on(self):
        passed = self.passed()
        for j, chip in enumerate(self.chips):
            self.copy(1 + j, (*chip, self.c), self.me).wait_recv()
            passed[j].start()

    def finish(self):
        self.copy(0, self.sibling, self.me).wait_recv()
        for j, chip in enumerate(self.chips):
            self.copy(4 + j, (*chip, 1 - self.c), self.me).wait_recv()
        for cp in self.first() + self.passed():
            cp.wait_send()
        self.mine.wait()


GATHER_SEMS = [pltpu.SemaphoreType.DMA((7,)), pltpu.SemaphoreType.DMA((7,)), pltpu.SemaphoreType.DMA]


def _all_gather_rows(shard):
    m_per, n = shard.shape

    def body(x_ref, out_ref, send_sems, recv_sems, local_sem):
        ag = _TwoLevelGather(x_ref, out_ref, m_per, send_sems, recv_sems, local_sem)
        ag.start()
        ag.pass_on()
        ag.finish()

    return pl.pallas_call(
        body,
        name="weights_all_gather",
        out_shape=jax.ShapeDtypeStruct((N_DEV * m_per, n), shard.dtype),
        in_specs=[pl.BlockSpec(memory_space=pltpu.VMEM)],
        out_specs=pl.BlockSpec(memory_space=pltpu.VMEM),
        scratch_shapes=list(GATHER_SEMS),
        compiler_params=pltpu.CompilerParams(vmem_limit_bytes=48 * MIB),
    )(shard)


PROJ_PARTS = D_MODEL // PLE_DIM
N_WEIGHT_COPIES = N_DEV * (3 + PROJ_PARTS)


def _weight_copies(wg_ref, w_in_t, w_out, w_gate, w_proj_t, sems):
    copies = []
    for s in range(N_DEV):
        base = s * ROWS_LAYER
        for dst, off, rows in ((w_in_t, OFF_IN, ROWS_IN), (w_out, OFF_OUT, ROWS_OUT), (w_gate, OFF_GATE, ROWS_GATE)):
            copies.append((wg_ref.at[pl.ds(base + off, rows), :], dst.at[pl.ds(s * rows, rows), :]))
        for j in range(PROJ_PARTS):
            copies.append((
                wg_ref.at[pl.ds(base + OFF_PROJ, ROWS_PROJ), pl.ds(j * PLE_DIM, PLE_DIM)],
                w_proj_t.at[pl.ds(s * ROWS_OUT + j * ROWS_PROJ, ROWS_PROJ), :],
            ))
    return [pltpu.make_async_copy(src, dst, sems.at[k]) for k, (src, dst) in enumerate(copies)]


def _forward_layer(layer, x, p_all, wg, conv_k, norm_g, ln_g, ln_b, w_mix, b_mix, ple_g, next_shard=None):
    t = x.shape[0]
    tm = _tile(t, 512)
    nt = t // tm
    gathers = next_shard is not None

    def body(*refs):
        (x_ref, p_ref, wg_ref, cw_ref, ng_ref, lng_ref, lnb_ref, wm_ref, bm_ref, pg_ref) = refs[:10]
        refs = refs[10:]
        if gathers:
            shard_ref, refs = refs[0], refs[1:]
        (proj_ref, hn_ref, cat_ref, r_ref, gpre_ref, x1_ref, x2_ref) = refs[:7]
        refs = refs[7:]
        if gathers:
            gathered_ref, refs = refs[0], refs[1:]
        (w_in_t, w_out, w_gate, wpt_ref, vln_s, mixed_s, halo_s, sems) = refs[:8]
        i = pl.program_id(0)
        if gathers:
            ag = _TwoLevelGather(shard_ref, gathered_ref, ROWS_LAYER, *refs[8:11])

            @pl.when(i == 0)
            def _():
                ag.start()

            @pl.when(i == nt // 2)
            def _():
                ag.pass_on()

        @pl.when(i == 0)
        def _():
            copies = _weight_copies(wg_ref, w_in_t, w_out, w_gate, wpt_ref, sems)
            for cp in copies:
                cp.start()
            halo_s[...] = jnp.zeros_like(halo_s)
            for cp in copies:
                cp.wait()

        xv = x_ref[...]
        rstd0 = lax.rsqrt(jnp.mean(xv * xv, axis=-1, keepdims=True) + EPS)
        hn_ref[...] = (xv * rstd0 * ng_ref[...]).astype(BF16)

        def proj_section(k):
            sec = _dot_nt(hn_ref[...], w_in_t[pl.ds(k * 512, 512), :])
            proj_ref[:, k * 512:(k + 1) * 512] = sec.astype(BF16)
            return sec

        v = proj_section(1)
        mu = jnp.mean(v, axis=-1, keepdims=True)
        vc = v - mu
        var = jnp.mean(vc * vc, axis=-1, keepdims=True)
        vln = vc * lax.rsqrt(var + EPS) * lng_ref[...] + lnb_ref[...]
        vln_s[...] = vln.astype(BF16)
        for ci in range(tm // CHUNK):
            rows = pl.ds(ci * CHUNK, CHUNK)
            for h in range(HEADS_A):
                cols = pl.ds(h * HEAD_DIM, HEAD_DIM)
                mixed_s[rows, cols] = _dot(wm_ref[h], vln_s[rows, cols]) + bm_ref[h]
        u = proj_section(0)
        za = proj_section(2)
        out_a = u * mixed_s[...] * (za * _sigmoid(za))
        cat_ref[:, 0:512] = out_a.astype(BF16)

        xc = proj_section(5) * proj_section(3)
        prev = halo_s[...]
        row = lax.broadcasted_iota(jnp.int32, (tm, WIDTH_B), 0)
        xc_m1 = jnp.where(row == 0, prev[7:8, :], pltpu.roll(xc, 1, 0))
        xc_m2 = jnp.where(row == 0, prev[6:7, :], jnp.where(row == 1, prev[7:8, :], pltpu.roll(xc, 2, 0)))
        halo_s[...] = xc[tm - 8:tm, :]
        cw = cw_ref[...]
        yc = cw[0:1, :] * xc_m2 + cw[1:2, :] * xc_m1 + cw[2:3, :] * xc
        zb = proj_section(6)
        out_b = proj_section(4) * yc * (zb * _sigmoid(zb))
        cat_ref[:, 512:1024] = out_b.astype(BF16)

        x1 = xv + _dot(cat_ref[...], w_out[...])
        x1_ref[...] = x1
        rstd1 = lax.rsqrt(jnp.mean(x1 * x1, axis=-1, keepdims=True) + EPS)
        r_ref[...] = (x1 * rstd1 * pg_ref[...]).astype(BF16)
        gpre = _dot(r_ref[...], w_gate[...])
        gpre_ref[...] = gpre.astype(BF16)
        pp = _dot_nt(p_ref[...].astype(BF16), wpt_ref[...])
        x2_ref[...] = x1 + _sigmoid(gpre) * pp

        if gathers:
            @pl.when(i == nt - 1)
            def _():
                ag.finish()

    def tok(width):
        return pl.BlockSpec((tm, width), lambda i: (i, 0))

    def whole(shape):
        return pl.BlockSpec(shape, lambda i: (0,) * len(shape))

    hbm = pl.BlockSpec(memory_space=pl.ANY)
    operands = [x, p_all, wg, conv_k, norm_g, ln_g, ln_b, w_mix, b_mix, ple_g]
    in_specs = [
        tok(D_MODEL), pl.BlockSpec((None, None, tm, PLE_DIM), lambda i: (layer, 0, i, 0)), hbm,
        whole((8, WIDTH_B)), whole((1, D_MODEL)), whole((1, WIDTH_A)), whole((1, WIDTH_A)),
        whole((HEADS_A, CHUNK, CHUNK)), whole((HEADS_A, CHUNK, HEAD_DIM)), whole((1, D_MODEL)),
    ]
    out_specs = [tok(PROJ_WIDTH), tok(D_MODEL), tok(D_MODEL), tok(D_MODEL), tok(D_MODEL), tok(D_MODEL), tok(D_MODEL)]
    out_shape = [
        jax.ShapeDtypeStruct((t, PROJ_WIDTH), BF16),
        jax.ShapeDtypeStruct((t, D_MODEL), BF16),
        jax.ShapeDtypeStruct((t, D_MODEL), BF16),
        jax.ShapeDtypeStruct((t, D_MODEL), BF16),
        jax.ShapeDtypeStruct((t, D_MODEL), BF16),
        jax.ShapeDtypeStruct((t, D_MODEL), F32),
        jax.ShapeDtypeStruct((t, D_MODEL), F32),
    ]
    scratch_shapes = [
        pltpu.VMEM((PROJ_WIDTH, D_MODEL), BF16),
        pltpu.VMEM((D_MODEL, D_MODEL), BF16),
        pltpu.VMEM((D_MODEL, D_MODEL), BF16),
        pltpu.VMEM((D_MODEL, PLE_DIM), BF16),
        pltpu.VMEM((tm, WIDTH_A), BF16),
        pltpu.VMEM((tm, WIDTH_A), F32),
        pltpu.VMEM((8, WIDTH_B), F32),
        pltpu.SemaphoreType.DMA((N_WEIGHT_COPIES,)),
    ]
    if gathers:
        operands.append(next_shard)
        in_specs.append(hbm)
        out_specs.append(hbm)
        out_shape.append(jax.ShapeDtypeStruct((N_DEV * ROWS_LAYER, D_MODEL), BF16))
        scratch_shapes += list(GATHER_SEMS)

    return pl.pallas_call(
        body,
        name=f"layer{layer}_forward",
        grid=(nt,),
        in_specs=in_specs,
        out_specs=out_specs,
        out_shape=out_shape,
        scratch_shapes=scratch_shapes,
        compiler_params=pltpu.CompilerParams(dimension_semantics=("arbitrary",), vmem_limit_bytes=56 * MIB),
    )(*operands)


class _DirectScatter:
    def __init__(self, pack_ref, pieces_ref, send_sems, recv_sems, local_sem):
        x, y, c = lax.axis_index("x"), lax.axis_index("y"), lax.axis_index("c")
        me = 4 * x + 2 * y + c
        self.copies = []
        for k in range(N_DEV - 1):
            fx, fy, fc = ((k + 1) >> 2) & 1, ((k + 1) >> 1) & 1, (k + 1) & 1
            tx, ty, tc = x ^ fx, y ^ fy, c ^ fc
            self.copies.append(
                pltpu.make_async_remote_copy(
                    src_ref=pack_ref.at[4 * tx + 2 * ty + tc], dst_ref=pieces_ref.at[me],
                    send_sem=send_sems.at[k], recv_sem=recv_sems.at[k],
                    device_id=(tx, ty, tc), device_id_type=MESH,
                )
            )
        self.mine = pltpu.make_async_copy(pack_ref.at[me], pieces_ref.at[me], local_sem)

    def start(self):
        self.mine.start()
        for cp in self.copies:
            cp.start()

    def finish(self):
        for cp in self.copies:
            cp.wait_recv()
        for cp in self.copies:
            cp.wait_send()
        self.mine.wait()


SCATTER_SEMS = [pltpu.SemaphoreType.DMA((N_DEV - 1,)), pltpu.SemaphoreType.DMA((N_DEV - 1,)), pltpu.SemaphoreType.DMA]


def _backward_layer(layer, dx2, x_in, x1, proj, gpre, p_all, wg, conv_k, norm_g, ln_g, ln_b,
                    w_mix, w_mix_t, b_mix, ple_g, loss_head=None):
    t = x_in.shape[0]
    tm = _tile(t, 256)
    nt = t // tm
    n_chunks = tm // CHUNK
    halo_rows = 16
    heads = loss_head is not None

    def body(*refs):
        (dx2_ref, xin_ref, x1_ref, proj_ref, halo_ref, gpre_ref, p_ref, wg_ref, cw_ref,
         ng_ref, lng_ref, lnb_ref, wm_ref, wmt_ref, bm_ref, pg_ref) = refs[:16]
        refs = refs[16:]
        if heads:
            tgt_ref, fg_ref = refs[:2]
            refs = refs[2:]
        (dxin_ref, dproj_ref, dx1_ref, dgpre_ref, dpp_ref, small_ref, dws_ref) = refs[:7]
        refs = refs[7:]
        if heads:
            head_ref, refs = refs[0], refs[1:]
        (w_in_t, w_out, w_gate, wpt_ref, vln_s, mixed_s, dmix_s, dvln_s, carry_s,
         ng_acc, pg_acc, lng_acc, lnb_acc, cw_acc, dbm_ref, sems) = refs[:16]
        if heads:
            loss_acc, fg_acc = refs[16:18]
        i = pl.program_id(0)
        tile = nt - 1 - i

        @pl.when(i == 0)
        def _():
            copies = _weight_copies(wg_ref, w_in_t, w_out, w_gate, wpt_ref, sems)
            for cp in copies:
                cp.start()
            if heads:
                loss_acc[...] = jnp.zeros_like(loss_acc)
                fg_acc[...] = jnp.zeros_like(fg_acc)
            carry_s[...] = jnp.zeros_like(carry_s)
            ng_acc[...] = jnp.zeros_like(ng_acc)
            pg_acc[...] = jnp.zeros_like(pg_acc)
            lng_acc[...] = jnp.zeros_like(lng_acc)
            lnb_acc[...] = jnp.zeros_like(lnb_acc)
            cw_acc[...] = jnp.zeros_like(cw_acc)
            dws_ref[...] = jnp.zeros_like(dws_ref)
            dbm_ref[...] = jnp.zeros_like(dbm_ref)
            for cp in copies:
                cp.wait()

        if heads:
            x2v = dx2_ref[...]
            fg = fg_ref[...]
            rstdf = lax.rsqrt(jnp.mean(x2v * x2v, axis=-1, keepdims=True) + EPS)
            xhatf = x2v * rstdf
            err = xhatf * fg - tgt_ref[...]
            loss_acc[...] += _colsum8(err * err)
            dy = err * (1.0 / D_MODEL)
            fg_acc[...] += _colsum8(dy * xhatf)
            dxhf = dy * fg
            dx2v = rstdf * (dxhf - xhatf * jnp.mean(dxhf * xhatf, axis=-1, keepdims=True))
        else:
            dx2v = dx2_ref[...]

        gate = _sigmoid(gpre_ref[...].astype(F32))
        pp = _dot_nt(p_ref[...].astype(BF16), wpt_ref[...])
        dpp_ref[...] = (dx2v * gate).astype(BF16)
        dgpre = (dx2v * pp * gate * (1.0 - gate)).astype(BF16)
        dgpre_ref[...] = dgpre
        dr = _dot_nt(dgpre, w_gate[...])
        x1v = x1_ref[...]
        rstd1 = lax.rsqrt(jnp.mean(x1v * x1v, axis=-1, keepdims=True) + EPS)
        xhat1 = x1v * rstd1
        pg_acc[...] += _colsum8(dr * xhat1)
        dxh = dr * pg_ref[...]
        dx1 = dx2v + rstd1 * (dxh - xhat1 * jnp.mean(dxh * xhat1, axis=-1, keepdims=True))
        dx1b = dx1.astype(BF16)
        dx1_ref[...] = dx1b

        dcat = _dot_nt(dx1b, w_out[...])
        dca = dcat[:, 0:512]
        dcb = dcat[:, 512:1024]

        u = proj_ref[:, 0:512].astype(F32)
        v = proj_ref[:, 512:1024].astype(F32)
        za = proj_ref[:, 1024:1536].astype(F32)
        mu = jnp.mean(v, axis=-1, keepdims=True)
        vc = v - mu
        var = jnp.mean(vc * vc, axis=-1, keepdims=True)
        rs = lax.rsqrt(var + EPS)
        vhat = vc * rs
        lng = lng_ref[...]
        vln_s[...] = (vhat * lng + lnb_ref[...]).astype(BF16)
        for ci in range(n_chunks):
            rows = pl.ds(ci * CHUNK, CHUNK)
            for h in range(HEADS_A):
                cols = pl.ds(h * HEAD_DIM, HEAD_DIM)
                mixed_s[rows, cols] = _dot(wm_ref[h], vln_s[rows, cols]) + bm_ref[h]
        mixed = mixed_s[...]
        sga = _sigmoid(za)
        sa = za * sga
        dsa = sga * (1.0 + za * (1.0 - sga))
        dproj_ref[:, 0:512] = (dca * mixed * sa).astype(BF16)
        dmix = dca * u * sa
        dproj_ref[:, 1024:1536] = (dca * u * mixed * dsa).astype(BF16)
        dmix_s[...] = dmix.astype(BF16)
        dbm_acc = jnp.zeros((CHUNK, WIDTH_A), F32)
        for ci in range(n_chunks):
            rows = pl.ds(ci * CHUNK, CHUNK)
            dbm_acc = dbm_acc + dmix[ci * CHUNK:(ci + 1) * CHUNK, :]
            for h in range(HEADS_A):
                cols = pl.ds(h * HEAD_DIM, HEAD_DIM)
                dvln_s[rows, cols] = _dot(wmt_ref[h], dmix_s[rows, cols])
                dws_ref[:, cols] += _dot_nt(dmix_s[rows, cols], vln_s[rows, cols])
        dbm_ref[...] += dbm_acc
        dvln = dvln_s[...]
        lng_acc[...] += _colsum8(dvln * vhat)
        lnb_acc[...] += _colsum8(dvln)
        dvh = dvln * lng
        dv = rs * (dvh - jnp.mean(dvh, axis=-1, keepdims=True) - vhat * jnp.mean(dvh * vhat, axis=-1, keepdims=True))
        dproj_ref[:, 512:1024] = dv.astype(BF16)

        hb = proj_ref[:, 1536:2048].astype(F32)
        gb = proj_ref[:, 2048:2560].astype(F32)
        gc = proj_ref[:, 2560:3072].astype(F32)
        zb = proj_ref[:, 3072:3584].astype(F32)
        xc = gc * hb
        prev = halo_ref[:, 2560:3072].astype(F32) * halo_ref[:, 1536:2048].astype(F32)
        prev = jnp.where(tile > 0, prev, 0.0)
        row = lax.broadcasted_iota(jnp.int32, (tm, WIDTH_B), 0)
        p1 = prev[halo_rows - 1:halo_rows, :]
        p2 = prev[halo_rows - 2:halo_rows - 1, :]
        xc_m1 = jnp.where(row == 0, p1, pltpu.roll(xc, 1, 0))
        xc_m2 = jnp.where(row == 0, p2, jnp.where(row == 1, p1, pltpu.roll(xc, 2, 0)))
        cw = cw_ref[...]
        yc = cw[0:1, :] * xc_m2 + cw[1:2, :] * xc_m1 + cw[2:3, :] * xc
        sgb = _sigmoid(zb)
        sb = zb * sgb
        dsb = sgb * (1.0 + zb * (1.0 - sgb))
        dproj_ref[:, 2048:2560] = (dcb * yc * sb).astype(BF16)
        dyc = dcb * gb * sb
        dproj_ref[:, 3072:3584] = (dcb * gb * yc * dsb).astype(BF16)
        nxt = carry_s[...]
        dyc_p1 = jnp.where(row == tm - 1, nxt[0:1, :], pltpu.roll(dyc, tm - 1, 0))
        dyc_p2 = jnp.where(row == tm - 1, nxt[1:2, :], jnp.where(row == tm - 2, nxt[0:1, :], pltpu.roll(dyc, tm - 2, 0)))
        carry_s[...] = dyc[0:8, :]
        dxc = cw[2:3, :] * dyc + cw[1:2, :] * dyc_p1 + cw[0:1, :] * dyc_p2
        cw_acc[0] += _colsum8(dyc * xc_m2)
        cw_acc[1] += _colsum8(dyc * xc_m1)
        cw_acc[2] += _colsum8(dyc * xc)
        dproj_ref[:, 1536:2048] = (dxc * gc).astype(BF16)
        dproj_ref[:, 2560:3072] = (dxc * hb).astype(BF16)

        dhn = _dot(dproj_ref[...], w_in_t[...])
        xv = xin_ref[...]
        rstd0 = lax.rsqrt(jnp.mean(xv * xv, axis=-1, keepdims=True) + EPS)
        xhat0 = xv * rstd0
        ng_acc[...] += _colsum8(dhn * xhat0)
        dxh0 = dhn * ng_ref[...]
        dxin_ref[...] = dx1 + rstd0 * (dxh0 - xhat0 * jnp.mean(dxh0 * xhat0, axis=-1, keepdims=True))

        @pl.when(i == nt - 1)
        def _():
            small_ref[...] = jnp.zeros_like(small_ref)
            small_ref[SMALL_NORM:SMALL_NORM + 1, :] = jnp.sum(ng_acc[...], axis=0, keepdims=True)
            small_ref[SMALL_PLE:SMALL_PLE + 1, :] = jnp.sum(pg_acc[...], axis=0, keepdims=True)
            small_ref[SMALL_LN:SMALL_LN + 1, 0:WIDTH_A] = jnp.sum(lng_acc[...], axis=0, keepdims=True)
            small_ref[SMALL_LN:SMALL_LN + 1, WIDTH_A:2 * WIDTH_A] = jnp.sum(lnb_acc[...], axis=0, keepdims=True)
            for h in range(HEADS_A):
                cols = pl.ds(h * HEAD_DIM, HEAD_DIM)
                small_ref[SMALL_BS:SMALL_BS + 1, cols] = jnp.sum(jnp.transpose(dbm_ref[:, cols]), axis=0, keepdims=True)
            for k in range(3):
                small_ref[SMALL_CONV + k:SMALL_CONV + k + 1, 0:WIDTH_B] = jnp.sum(cw_acc[k], axis=0, keepdims=True)
            if heads:
                total = jnp.sum(loss_acc[...]) * (0.5 / D_MODEL)
                rows8 = lax.broadcasted_iota(jnp.int32, (SMALL_ROWS, D_MODEL), 0)
                lanes8 = lax.broadcasted_iota(jnp.int32, (SMALL_ROWS, D_MODEL), 1)
                head_ref[...] = jnp.where((rows8 == HEAD_LOSS) & (lanes8 == 0), total, 0.0)
                head_ref[HEAD_FINAL:HEAD_FINAL + 1, :] = jnp.sum(fg_acc[...], axis=0, keepdims=True)

    def tok(width):
        return pl.BlockSpec((tm, width), lambda i: (nt - 1 - i, 0))

    def whole(shape):
        return pl.BlockSpec(shape, lambda i: (0,) * len(shape))

    halo_spec = pl.BlockSpec(
        (halo_rows, PROJ_WIDTH), lambda i: (jnp.maximum((nt - 1 - i) * (tm // halo_rows) - 1, 0), 0)
    )
    hbm = pl.BlockSpec(memory_space=pl.ANY)
    operands = [dx2, x_in, x1, proj, proj, gpre, p_all, wg, conv_k, norm_g, ln_g, ln_b, w_mix, w_mix_t, b_mix, ple_g]
    in_specs = [
        tok(D_MODEL), tok(D_MODEL), tok(D_MODEL), tok(PROJ_WIDTH), halo_spec, tok(D_MODEL),
        pl.BlockSpec((None, None, tm, PLE_DIM), lambda i: (layer, 0, nt - 1 - i, 0)), hbm,
        whole((8, WIDTH_B)), whole((1, D_MODEL)), whole((1, WIDTH_A)), whole((1, WIDTH_A)),
        whole((HEADS_A, CHUNK, CHUNK)), whole((HEADS_A, CHUNK, CHUNK)), whole((HEADS_A, CHUNK, HEAD_DIM)),
        whole((1, D_MODEL)),
    ]
    out_specs = [
        tok(D_MODEL), tok(PROJ_WIDTH), tok(D_MODEL), tok(D_MODEL), tok(D_MODEL),
        whole((SMALL_ROWS, D_MODEL)), whole((CHUNK, WIDTH_A)),
    ]
    out_shape = [
        jax.ShapeDtypeStruct((t, D_MODEL), F32),
        jax.ShapeDtypeStruct((t, PROJ_WIDTH), BF16),
        jax.ShapeDtypeStruct((t, D_MODEL), BF16),
        jax.ShapeDtypeStruct((t, D_MODEL), BF16),
        jax.ShapeDtypeStruct((t, D_MODEL), BF16),
        jax.ShapeDtypeStruct((SMALL_ROWS, D_MODEL), F32),
        jax.ShapeDtypeStruct((CHUNK, WIDTH_A), F32),
    ]
    scratch_shapes = [
        pltpu.VMEM((PROJ_WIDTH, D_MODEL), BF16),
        pltpu.VMEM((D_MODEL, D_MODEL), BF16),
        pltpu.VMEM((D_MODEL, D_MODEL), BF16),
        pltpu.VMEM((D_MODEL, PLE_DIM), BF16),
        pltpu.VMEM((tm, WIDTH_A), BF16),
        pltpu.VMEM((tm, WIDTH_A), F32),
        pltpu.VMEM((tm, WIDTH_A), BF16),
        pltpu.VMEM((tm, WIDTH_A), F32),
        pltpu.VMEM((8, WIDTH_B), F32),
        pltpu.VMEM((8, D_MODEL), F32),
        pltpu.VMEM((8, D_MODEL), F32),
        pltpu.VMEM((8, WIDTH_A), F32),
        pltpu.VMEM((8, WIDTH_A), F32),
        pltpu.VMEM((3, 8, WIDTH_B), F32),
        pltpu.VMEM((CHUNK, WIDTH_A), F32),
        pltpu.SemaphoreType.DMA((N_WEIGHT_COPIES,)),
    ]
    if heads:
        operands += list(loss_head)
        in_specs += [tok(D_MODEL), whole((1, D_MODEL))]
        out_specs.append(whole((SMALL_ROWS, D_MODEL)))
        out_shape.append(jax.ShapeDtypeStruct((SMALL_ROWS, D_MODEL), F32))
        scratch_shapes += [pltpu.VMEM((8, D_MODEL), F32), pltpu.VMEM((8, D_MODEL), F32)]

    return pl.pallas_call(
        body,
        name=f"layer{layer}_backward",
        grid=(nt,),
        in_specs=in_specs,
        out_specs=out_specs,
        out_shape=out_shape,
        scratch_shapes=scratch_shapes,
        compiler_params=pltpu.CompilerParams(dimension_semantics=("arbitrary",), vmem_limit_bytes=56 * MIB),
    )(*operands)


def _sum_pieces(layer, pieces):
    rows, n = pieces.shape[1], pieces.shape[2]

    def body(p_ref, out_ref):
        total = p_ref[0].astype(F32)
        for j in range(1, N_DEV):
            total = total + p_ref[j].astype(F32)
        out_ref[...] = total

    return pl.pallas_call(
        body,
        name=f"layer{layer}_grad_sum",
        out_shape=jax.ShapeDtypeStruct((rows, n), F32),
        in_specs=[pl.BlockSpec(memory_space=pltpu.VMEM)],
        out_specs=pl.BlockSpec(memory_space=pltpu.VMEM),
        compiler_params=pltpu.CompilerParams(vmem_limit_bytes=32 * MIB),
    )(pieces)


def _weight_grads(layer, dproj, hn, cat, dx1, r, dgpre, dpp, p_all, scatter_pack=None):
    t = hn.shape[0]
    tk = _tile(t, 512)
    nt = t // tk
    in_blocks = PROJ_WIDTH // 512
    scatters = scatter_pack is not None

    def body(*refs):
        (dproj_ref, hn_ref, cat_ref, dx1_ref, r_ref, dgpre_ref, dpp_ref, p_ref) = refs[:8]
        refs = refs[8:]
        if scatters:
            prior_ref, refs = refs[0], refs[1:]
        pack_ref, refs = refs[0], refs[1:]
        if scatters:
            pieces_ref, refs = refs[0], refs[1:]
        (acc_in, acc_out, acc_gate, acc_proj, stage, sems) = refs[:6]
        i = pl.program_id(0)
        if scatters:
            scatter = _DirectScatter(prior_ref, pieces_ref, *refs[6:9])

            @pl.when(i == 0)
            def _():
                scatter.start()

        @pl.when(i == 0)
        def _():
            acc_in[...] = jnp.zeros_like(acc_in)
            acc_out[...] = jnp.zeros_like(acc_out)
            acc_gate[...] = jnp.zeros_like(acc_gate)
            acc_proj[...] = jnp.zeros_like(acc_proj)

        hnv = hn_ref[...]
        for b in range(in_blocks):
            acc_in[pl.ds(b * 512, 512), :] += _dot_tn(dproj_ref[:, b * 512:(b + 1) * 512], hnv)
        dx1v = dx1_ref[...]
        dgv = dgpre_ref[...]
        for b in range(D_MODEL // 512):
            acc_out[pl.ds(b * 512, 512), :] += _dot_tn(cat_ref[:, b * 512:(b + 1) * 512], dx1v)
            acc_gate[pl.ds(b * 512, 512), :] += _dot_tn(r_ref[:, b * 512:(b + 1) * 512], dgv)
        pv = p_ref[...].astype(BF16)
        for b in range(D_MODEL // 512):
            acc_proj[pl.ds(b * 512, 512), :] += _dot_tn(dpp_ref[:, b * 512:(b + 1) * 512], pv)

        @pl.when(i == nt - 1)
        def _():
            def out_copy(s):
                return pltpu.make_async_copy(stage.at[s % 2], pack_ref.at[s], sems.at[s % 2])

            for s in range(N_DEV):
                if s >= 2:
                    out_copy(s - 2).wait()
                buf = stage.at[s % 2]
                buf[pl.ds(OFF_IN, ROWS_IN), :] = acc_in[pl.ds(s * ROWS_IN, ROWS_IN), :].astype(BF16)
                buf[pl.ds(OFF_OUT, ROWS_OUT), :] = acc_out[pl.ds(s * ROWS_OUT, ROWS_OUT), :].astype(BF16)
                buf[pl.ds(OFF_GATE, ROWS_GATE), :] = acc_gate[pl.ds(s * ROWS_GATE, ROWS_GATE), :].astype(BF16)
                for j in range(D_MODEL // PLE_DIM):
                    buf[pl.ds(OFF_PROJ, ROWS_PROJ), pl.ds(j * PLE_DIM, PLE_DIM)] = acc_proj[
                        pl.ds(s * ROWS_OUT + j * ROWS_PROJ, ROWS_PROJ), :
                    ].astype(BF16)
                out_copy(s).start()
            out_copy(N_DEV - 2).wait()
            out_copy(N_DEV - 1).wait()
            if scatters:
                scatter.finish()

    def tok(width):
        return pl.BlockSpec((tk, width), lambda i: (i, 0))

    hbm = pl.BlockSpec(memory_space=pl.ANY)
    pack_shape = jax.ShapeDtypeStruct((N_DEV, ROWS_GRAD, D_MODEL), BF16)
    operands = [dproj, hn, cat, dx1, r, dgpre, dpp, p_all]
    in_specs = [tok(PROJ_WIDTH), tok(D_MODEL), tok(D_MODEL), tok(D_MODEL), tok(D_MODEL), tok(D_MODEL), tok(D_MODEL),
                pl.BlockSpec((None, None, tk, PLE_DIM), lambda i: (layer, 0, i, 0))]
    out_specs, out_shape = [hbm], [pack_shape]
    scratch_shapes = [
        pltpu.VMEM((PROJ_WIDTH, D_MODEL), F32),
        pltpu.VMEM((D_MODEL, D_MODEL), F32),
        pltpu.VMEM((D_MODEL, D_MODEL), F32),
        pltpu.VMEM((D_MODEL, PLE_DIM), F32),
        pltpu.VMEM((2, ROWS_GRAD, D_MODEL), BF16),
        pltpu.SemaphoreType.DMA((2,)),
    ]
    if scatters:
        operands.append(scatter_pack)
        in_specs.append(hbm)
        out_specs.append(hbm)
        out_shape.append(pack_shape)
        scratch_shapes += list(SCATTER_SEMS)

    return pl.pallas_call(
        body,
        name=f"layer{layer}_weight_grads",
        grid=(nt,),
        in_specs=in_specs,
        out_specs=out_specs,
        out_shape=out_shape,
        scratch_shapes=scratch_shapes,
        compiler_params=pltpu.CompilerParams(dimension_semantics=("arbitrary",), vmem_limit_bytes=58 * MIB),
    )(*operands)


def _reduce_scatter_all_reduce(layer, pack, smalls, head, dws):
    rows, n = pack.shape[1], pack.shape[2]
    assert DEPTH * WIDTH_A == D_MODEL and n == D_MODEL

    def body(g_ref, *refs):
        small_refs, refs = refs[:DEPTH], refs[DEPTH:]
        head_ref, refs = refs[0], refs[1:]
        dws_refs, refs = refs[:DEPTH], refs[DEPTH:]
        (out_ref, total_ref, r1, a_s, r2, sp, sr1, sq, send1, recv1, send2, recv2, ssend, srecv) = refs
        x, y, c = lax.axis_index("x"), lax.axis_index("y"), lax.axis_index("c")
        sibling = (x, y, 1 - c)
        chip = 2 * x + y
        flips = [(1, 0), (0, 1), (1, 1)]

        for l in range(DEPTH):
            sp[l * SMALL_ROWS:(l + 1) * SMALL_ROWS, :] = small_refs[l][...]
            sp[TOTAL_WS:TOTAL_ROWS, l * WIDTH_A:(l + 1) * WIDTH_A] = dws_refs[l][...]
        sp[TOTAL_HEAD:TOTAL_WS, :] = head_ref[...]

        small_pair = pltpu.make_async_remote_copy(
            src_ref=sp, dst_ref=sr1, send_sem=ssend.at[0], recv_sem=srecv.at[0], device_id=sibling, device_id_type=MESH
        )

        def to_sibling(j):
            return pltpu.make_async_remote_copy(
                src_ref=g_ref.at[2 * j + 1 - c], dst_ref=r1.at[j], send_sem=send1.at[j], recv_sem=recv1.at[j],
                device_id=sibling, device_id_type=MESH,
            )

        first = [to_sibling(j) for j in range(4)]
        small_pair.start()
        for cp in first:
            cp.start()

        small_pair.wait_recv()
        sq[chip] = sp[...] + sr1[...]
        small_chips = [
            pltpu.make_async_remote_copy(
                src_ref=sq.at[chip], dst_ref=sq.at[chip], send_sem=ssend.at[1 + k], recv_sem=srecv.at[1 + k],
                device_id=(x ^ fx, y ^ fy, c), device_id_type=MESH,
            )
            for k, (fx, fy) in enumerate(flips)
        ]
        for cp in small_chips:
            cp.start()

        for j in range(4):
            first[j].wait_recv()
            a_s[j] = (g_ref[2 * j + c].astype(F32) + r1[j].astype(F32)).astype(BF16)

        def to_chip(k):
            fx, fy = flips[k]
            tx, ty = x ^ fx, y ^ fy
            return pltpu.make_async_remote_copy(
                src_ref=a_s.at[2 * tx + ty], dst_ref=r2.at[k], send_sem=send2.at[k], recv_sem=recv2.at[k],
                device_id=(tx, ty, c), device_id_type=MESH,
            )

        second = [to_chip(k) for k in range(3)]
        for cp in second:
            cp.start()
        total = g_ref[2 * chip + c].astype(F32) + r1[chip].astype(F32)
        for cp in small_chips:
            cp.wait_recv()
        total_ref[...] = ((sq[0] + sq[1]) + sq[2]) + sq[3]
        for k in range(3):
            second[k].wait_recv()
            total = total + r2[k].astype(F32)
        out_ref[...] = total
        small_pair.wait_send()
        for cp in first + second + small_chips:
            cp.wait_send()

    vmem = pl.BlockSpec(memory_space=pltpu.VMEM)
    return pl.pallas_call(
        body,
        name=f"layer{layer}_grad_reduce_scatter",
        out_shape=[jax.ShapeDtypeStruct((rows, n), F32), jax.ShapeDtypeStruct((TOTAL_ROWS, D_MODEL), F32)],
        in_specs=[vmem] * (2 + 2 * DEPTH),
        out_specs=[vmem, vmem],
        scratch_shapes=[
            pltpu.VMEM((4, rows, n), BF16),
            pltpu.VMEM((4, rows, n), BF16),
            pltpu.VMEM((3, rows, n), BF16),
            pltpu.VMEM((TOTAL_ROWS, D_MODEL), F32),
            pltpu.VMEM((TOTAL_ROWS, D_MODEL), F32),
            pltpu.VMEM((4, TOTAL_ROWS, D_MODEL), F32),
            pltpu.SemaphoreType.DMA((4,)),
            pltpu.SemaphoreType.DMA((4,)),
            pltpu.SemaphoreType.DMA((3,)),
            pltpu.SemaphoreType.DMA((3,)),
            pltpu.SemaphoreType.DMA((4,)),
            pltpu.SemaphoreType.DMA((4,)),
        ],
        compiler_params=pltpu.CompilerParams(vmem_limit_bytes=48 * MIB),
    )(pack, *smalls, head, *dws)


def _adam_step(w, g, m, v):
    m = ADAM_B1 * m + (1.0 - ADAM_B1) * g
    v = ADAM_B2 * v + (1.0 - ADAM_B2) * (g * g)
    m_hat = m / (1.0 - ADAM_B1 ** ADAM_STEP)
    v_hat = v / (1.0 - ADAM_B2 ** ADAM_STEP)
    return -ADAM_LR * (m_hat / (jnp.sqrt(v_hat) + ADAM_EPS) + ADAM_WD * w), m, v


def _adamw_rows(name, reduced, row_off, states):
    n = len(states)

    def body(*refs):
        red = refs[:DEPTH]
        ins = refs[DEPTH:DEPTH + 3 * n]
        outs = refs[DEPTH + 3 * n:]
        for k in range(n):
            w_ref, m_ref, v_ref = ins[3 * k:3 * k + 3]
            g_ref, d_ref, nm_ref, nv_ref = outs[4 * k:4 * k + 4]
            r = w_ref.shape[1]
            for l in range(DEPTH):
                g = red[l][row_off[k]:row_off[k] + r, :]
                d, m, v = _adam_step(w_ref[l], g, m_ref[l], v_ref[l])
                g_ref[l] = g
                d_ref[l] = d
                nm_ref[l] = m
                nv_ref[l] = v

    flat = [a for st in states for a in st]
    out_shape = []
    for w, _, _ in states:
        out_shape += [jax.ShapeDtypeStruct(w.shape, F32)] * 4
    vmem = pl.BlockSpec(memory_space=pltpu.VMEM)
    outs = pl.pallas_call(
        body,
        name=name,
        out_shape=out_shape,
        in_specs=[vmem] * (DEPTH + len(flat)),
        out_specs=[vmem] * len(out_shape),
        compiler_params=pltpu.CompilerParams(vmem_limit_bytes=48 * MIB),
    )(*reduced, *flat)
    return [tuple(outs[4 * k:4 * k + 4]) for k in range(n)]


def _adamw_small(total, g_conv, g_proj, st):
    names = ["norm_g", "ple_norm_g", "ln_v_g", "ln_v_b", "b_s", "w_s", "final_g", "conv_w", "w_ple_proj"]
    cut = names[:7]

    def body(total_ref, gconv_ref, gproj_ref, *refs):
        ins = {nm: refs[3 * k:3 * k + 3] for k, nm in enumerate(names)}
        outs, pos = {}, 3 * len(names)
        for nm in names:
            cnt = 4 if nm in cut else 3
            outs[nm] = refs[pos:pos + cnt]
            pos += cnt

        def update(nm, idx, g):
            w_ref, m_ref, v_ref = ins[nm]
            d, m, v = _adam_step(w_ref[idx], g, m_ref[idx], v_ref[idx])
            o = outs[nm]
            if nm in cut:
                o[0][idx] = g
                o = o[1:]
            o[0][idx] = d
            o[1][idx] = m
            o[2][idx] = v

        tril = (lax.broadcasted_iota(jnp.int32, (CHUNK, CHUNK), 0) >= lax.broadcasted_iota(jnp.int32, (CHUNK, CHUNK), 1))
        for l in range(DEPTH):
            base = l * SMALL_ROWS
            row = (slice(l, l + 1), slice(None))
            update("norm_g", row, total_ref[base + SMALL_NORM:base + SMALL_NORM + 1, :])
            update("ple_norm_g", row, total_ref[base + SMALL_PLE:base + SMALL_PLE + 1, :])
            update("ln_v_g", row, total_ref[base + SMALL_LN:base + SMALL_LN + 1, 0:WIDTH_A])
            update("ln_v_b", row, total_ref[base + SMALL_LN:base + SMALL_LN + 1, WIDTH_A:2 * WIDTH_A])
            for h in range(HEADS_A):
                update("b_s", (l, slice(h, h + 1), slice(None)),
                       total_ref[base + SMALL_BS:base + SMALL_BS + 1, h * HEAD_DIM:(h + 1) * HEAD_DIM])
                lanes = slice(l * WIDTH_A + h * CHUNK, l * WIDTH_A + (h + 1) * CHUNK)
                update("w_s", (l, h), jnp.where(tril, total_ref[TOTAL_WS:TOTAL_ROWS, lanes], 0.0))
        update("final_g", (slice(None), slice(None)), total_ref[TOTAL_HEAD + HEAD_FINAL:TOTAL_HEAD + HEAD_FINAL + 1, :])
        update("conv_w", (slice(None),) * 3, gconv_ref[...])
        update("w_ple_proj", (slice(None),) * 3, gproj_ref[...])

    flat = [a for nm in names for a in st[nm]]
    out_shape = []
    for nm in names:
        out_shape += [jax.ShapeDtypeStruct(st[nm][0].shape, F32)] * (4 if nm in cut else 3)
    vmem = pl.BlockSpec(memory_space=pltpu.VMEM)
    outs = pl.pallas_call(
        body,
        name="adamw_small",
        out_shape=out_shape,
        in_specs=[vmem] * (3 + len(flat)),
        out_specs=[vmem] * len(out_shape),
        compiler_params=pltpu.CompilerParams(vmem_limit_bytes=32 * MIB),
    )(total, g_conv, g_proj, *flat)
    res, pos = {}, 0
    for nm in names:
        cnt = 4 if nm in cut else 3
        got = tuple(outs[pos:pos + cnt])
        res[nm] = got if nm in cut else ((g_conv if nm == "conv_w" else g_proj),) + got
        pos += cnt
    return res


def _split3_bf16(a):
    b1 = a.astype(BF16)
    r1 = a - b1.astype(F32)
    b2 = r1.astype(BF16)
    b3 = (r1 - b2.astype(F32)).astype(BF16)
    return b1, b2, b3


def _pack_weight_shard(w_in_l, w_out_l, w_gate_l, w_proj_l, conv_w_l):
    w_in_t = jnp.transpose(w_in_l).astype(BF16)
    proj_t = jnp.transpose(w_proj_l).astype(BF16)
    proj_rows = proj_t.reshape(D_MODEL // PLE_DIM, ROWS_PROJ, PLE_DIM).transpose(1, 0, 2).reshape(ROWS_PROJ, D_MODEL)
    conv_parts = jnp.concatenate([b.reshape(-1) for b in _split3_bf16(conv_w_l)])
    conv_rows = jnp.concatenate([conv_parts, jnp.zeros((ROWS_CONV * D_MODEL - conv_parts.shape[0],), BF16)])
    return jnp.concatenate(
        [w_in_t, w_out_l.astype(BF16), w_gate_l.astype(BF16), proj_rows, conv_rows.reshape(ROWS_CONV, D_MODEL)], axis=0
    )


def _unpack_conv(wg):
    per_dev = wg.reshape(N_DEV, ROWS_LAYER, D_MODEL)
    n_conv = (WIDTH_B // N_DEV) * 3
    conv_parts = per_dev[:, OFF_CONV].astype(F32)[:, :3 * n_conv].reshape(N_DEV, 3, n_conv)
    conv = (conv_parts[:, 0] + conv_parts[:, 1]) + conv_parts[:, 2]
    conv_k = jnp.transpose(conv.reshape(WIDTH_B, 3))
    conv_k = jnp.concatenate([conv_k, jnp.zeros((5, WIDTH_B), F32)], axis=0)
    return conv_k


def _unpack_grad_proj(red):
    proj_rows = red[OFF_PROJ:OFF_PROJ + ROWS_PROJ]
    proj_t = proj_rows.reshape(ROWS_PROJ, D_MODEL // PLE_DIM, PLE_DIM).transpose(1, 0, 2).reshape(ROWS_OUT, PLE_DIM)
    return jnp.transpose(proj_t)


def kernel(x, p, norm_g, w_in, ln_v_g, ln_v_b, w_s, b_s, conv_w, w_out, ple_norm_g, w_ple_gate, w_ple_proj, final_g, loss_target, m_norm_g, m_w_in, m_ln_v_g, m_ln_v_b, m_w_s, m_b_s, m_conv_w, m_w_out, m_ple_norm_g, m_w_ple_gate, m_w_ple_proj, m_final_g, v_norm_g, v_w_in, v_ln_v_g, v_ln_v_b, v_w_s, v_b_s, v_conv_w, v_w_out, v_ple_norm_g, v_w_ple_gate, v_w_ple_proj, v_final_g):
    me = 4 * lax.axis_index("x") + 2 * lax.axis_index("y") + lax.axis_index("c")
    xs = x[0]
    target = loss_target[0]

    shards = [_pack_weight_shard(w_in[l], w_out[l], w_ple_gate[l], w_ple_proj[l], conv_w[l]) for l in range(DEPTH)]
    tril = jnp.tril(jnp.ones((CHUNK, CHUNK), F32))

    def consts(l, wg_l):
        conv_k = _unpack_conv(wg_l)
        w_mix = w_s[l] * tril[None]
        return dict(
            wg=wg_l, conv_k=conv_k,
            norm_g=norm_g[l].reshape(1, D_MODEL), ln_g=ln_v_g[l].reshape(1, WIDTH_A), ln_b=ln_v_b[l].reshape(1, WIDTH_A),
            w_mix=w_mix.astype(BF16), w_mix_t=jnp.swapaxes(w_mix, 1, 2).astype(BF16),
            b_mix=jnp.broadcast_to(b_s[l][:, :, None], (HEADS_A, CHUNK, HEAD_DIM)),
            ple_g=ple_norm_g[l].reshape(1, D_MODEL),
        )

    layer_consts = [consts(0, _all_gather_rows(shards[0]))]
    saved = []
    h = xs
    for l in range(DEPTH):
        k = layer_consts[l]
        outs = _forward_layer(
            l, h, p, k["wg"], k["conv_k"], k["norm_g"], k["ln_g"], k["ln_b"], k["w_mix"], k["b_mix"],
            k["ple_g"], next_shard=shards[l + 1] if l + 1 < DEPTH else None)
        proj, hn, cat, r, gpre, x1, x2 = outs[:7]
        if l + 1 < DEPTH:
            layer_consts.append(consts(l + 1, outs[7]))
        saved.append(dict(x_in=h, proj=proj, hn=hn, cat=cat, r=r, gpre=gpre, x1=x1))
        h = x2

    smalls, dws = [None] * DEPTH, [None] * DEPTH
    reduced = [None] * DEPTH
    pending = None
    dx = h
    for l in reversed(range(DEPTH)):
        k, s = layer_consts[l], saved[l]
        outs = _backward_layer(
            l, dx, s["x_in"], s["x1"], s["proj"], s["gpre"], p, k["wg"], k["conv_k"],
            k["norm_g"], k["ln_g"], k["ln_b"], k["w_mix"], k["w_mix_t"], k["b_mix"], k["ple_g"],
            loss_head=(target, final_g.reshape(1, D_MODEL)) if l == DEPTH - 1 else None)
        dx, dproj, dx1, dgpre, dpp, smalls[l], dws[l] = outs[:7]
        if l == DEPTH - 1:
            head = outs[7]
        outs = _weight_grads(l, dproj, s["hn"], s["cat"], dx1, s["r"], dgpre, dpp, p, scatter_pack=pending)
        if pending is not None:
            reduced[l + 1] = _sum_pieces(l + 1, outs[1])
        pending = outs[0]
    reduced[0], total = _reduce_scatter_all_reduce(0, pending, smalls, head, dws)
    grad_x = dx[None]
    loss = total[TOTAL_HEAD + HEAD_LOSS, 0]

    n_ch = WIDTH_B // N_DEV
    g_conv = jnp.stack([total[l * SMALL_ROWS + SMALL_CONV:l * SMALL_ROWS + SMALL_CONV + 3, 0:WIDTH_B] for l in range(DEPTH)], axis=1)
    g_conv = lax.dynamic_slice_in_dim(g_conv, me * n_ch, n_ch, axis=2)
    g_proj = jnp.stack([_unpack_grad_proj(reduced[l]) for l in range(DEPTH)])

    def t_in(a):
        return jnp.swapaxes(a, 1, 2)

    def t_conv(a):
        return jnp.transpose(a, (2, 0, 1))

    (r_in,) = _adamw_rows("adamw_w_in", reduced, [OFF_IN], [(t_in(w_in), t_in(m_w_in), t_in(v_w_in))])
    r_out, r_gate = _adamw_rows(
        "adamw_w_out_gate", reduced, [OFF_OUT, OFF_GATE],
        [(w_out, m_w_out, v_w_out), (w_ple_gate, m_w_ple_gate, v_w_ple_gate)])
    small = _adamw_small(total, g_conv, g_proj, dict(
        norm_g=(norm_g, m_norm_g, v_norm_g), ple_norm_g=(ple_norm_g, m_ple_norm_g, v_ple_norm_g),
        ln_v_g=(ln_v_g, m_ln_v_g, v_ln_v_g), ln_v_b=(ln_v_b, m_ln_v_b, v_ln_v_b),
        b_s=(b_s, m_b_s, v_b_s), w_s=(w_s, m_w_s, v_w_s),
        final_g=tuple(a.reshape(1, D_MODEL) for a in (final_g, m_final_g, v_final_g)),
        conv_w=(t_conv(conv_w), t_conv(m_conv_w), t_conv(v_conv_w)),
        w_ple_proj=(w_ple_proj, m_w_ple_proj, v_w_ple_proj),
    ))
    res = dict(small, w_in=tuple(t_in(a) for a in r_in), w_out=r_out, w_ple_gate=r_gate)
    res["final_g"] = tuple(a.reshape(D_MODEL) for a in res["final_g"])
    res["conv_w"] = tuple(jnp.transpose(a, (1, 2, 0)) for a in res["conv_w"])
    order = ["norm_g", "w_in", "ln_v_g", "ln_v_b", "w_s", "b_s", "conv_w", "w_out", "ple_norm_g", "w_ple_gate", "w_ple_proj", "final_g"]
    return (loss, grad_x, *[res[n][0] for n in order], *[res[n][1] for n in order],
            *[res[n][2] for n in order], *[res[n][3] for n in order])
```

```python
import jax
import jax.numpy as jnp
from jax import lax
from jax.experimental import pallas as pl
from jax.experimental.pallas import tpu as pltpu

F32 = jnp.float32
BF16 = jnp.bfloat16

D_MODEL = 1024
WIDTH_A = 512
WIDTH_B = 512
HEADS_A = 4
HEAD_DIM = 128
CHUNK = 128
PLE_DIM = 256
PROJ_WIDTH = 3584
DEPTH = 2
EPS = 1e-6
N_DEV = 8

ADAM_LR = 0.001
ADAM_B1 = 0.9
ADAM_B2 = 0.999
ADAM_EPS = 1e-08
ADAM_WD = 0.01
ADAM_STEP = 10

ROWS_IN = PROJ_WIDTH // N_DEV
ROWS_OUT = D_MODEL // N_DEV
ROWS_GATE = D_MODEL // N_DEV
ROWS_PROJ = (D_MODEL // N_DEV) * PLE_DIM // D_MODEL
ROWS_CONV = 16
OFF_IN = 0
OFF_OUT = OFF_IN + ROWS_IN
OFF_GATE = OFF_OUT + ROWS_OUT
OFF_PROJ = OFF_GATE + ROWS_GATE
OFF_CONV = OFF_PROJ + ROWS_PROJ
ROWS_GRAD = OFF_CONV
ROWS_LAYER = OFF_CONV + ROWS_CONV

SMALL_ROWS = 8
SMALL_NORM = 0
SMALL_PLE = 1
SMALL_LN = 2
SMALL_BS = 3
SMALL_CONV = 4
HEAD_FINAL = 0
HEAD_LOSS = 1
TOTAL_HEAD = DEPTH * SMALL_ROWS
TOTAL_WS = TOTAL_HEAD + SMALL_ROWS
TOTAL_ROWS = TOTAL_WS + CHUNK

MIB = 1024 * 1024
MESH = pl.DeviceIdType.MESH

NT_DIMS = (((1,), (1,)), ((), ()))
TN_DIMS = (((0,), (0,)), ((), ()))


def _dot(a, b):
    return jnp.dot(a, b, preferred_element_type=F32)


def _dot_nt(a, b):
    return lax.dot_general(a, b, NT_DIMS, preferred_element_type=F32)


def _dot_tn(a, b):
    return lax.dot_general(a, b, TN_DIMS, preferred_element_type=F32)


def _colsum8(a):
    rows, n = a.shape
    return jnp.sum(a.reshape(rows // 8, 8, n), axis=0)


def _sigmoid(z):
    return 1.0 / (1.0 + jnp.exp(-z))


def _tile(t, want):
    return want if t % want == 0 else t


class _TwoLevelGather:
    def __init__(self, x_ref, out_ref, m_per, send_sems, recv_sems, local_sem):
        x, y, c = lax.axis_index("x"), lax.axis_index("y"), lax.axis_index("c")
        self.c = c
        self.me, self.sibling = (x, y, c), (x, y, 1 - c)
        self.chips = [(1 - x, y), (x, 1 - y), (1 - x, 1 - y)]
        self.x_ref, self.out_ref, self.m_per = x_ref, out_ref, m_per
        self.send_sems, self.recv_sems = send_sems, recv_sems
        self.mine = pltpu.make_async_copy(x_ref, self.rows(*self.me), local_sem)

    def rows(self, px, py, pc):
        return self.out_ref.at[pl.ds((4 * px + 2 * py + pc) * self.m_per, self.m_per), :]

    def copy(self, k, block, to, src=None):
        return pltpu.make_async_remote_copy(
            src_ref=self.rows(*block) if src is None else src,
            dst_ref=self.rows(*block),
            send_sem=self.send_sems.at[k],
            recv_sem=self.recv_sems.at[k],
            device_id=to,
            device_id_type=MESH,
        )

    def first(self):
        out = [self.copy(0, self.me, self.sibling, src=self.x_ref)]
        return out + [self.copy(1 + j, self.me, (*chip, self.c), src=self.x_ref) for j, chip in enumerate(self.chips)]

    def passed(self):
        return [self.copy(4 + j, (*chip, self.c), self.sibling) for j, chip in enumerate(self.chips)]

    def start(self):
        self.mine.start()
        for cp in self.first():
            cp.start()

    def pass_on(self):
        passed = self.passed()
        for j, chip in enumerate(self.chips):
            self.copy(1 + j, (*chip, self.c), self.me).wait_recv()
            passed[j].start()

    def finish(self):
        self.copy(0, self.sibling, self.me).wait_recv()
        for j, chip in enumerate(self.chips):
            self.copy(4 + j, (*chip, 1 - self.c), self.me).wait_recv()
        for cp in self.first() + self.passed():
            cp.wait_send()
        self.mine.wait()


GATHER_SEMS = [pltpu.SemaphoreType.DMA((7,)), pltpu.SemaphoreType.DMA((7,)), pltpu.SemaphoreType.DMA]


def _all_gather_rows(shard):
    m_per, n = shard.shape

    def body(x_ref, out_ref, send_sems, recv_sems, local_sem):
        ag = _TwoLevelGather(x_ref, out_ref, m_per, send_sems, recv_sems, local_sem)
        ag.start()
        ag.pass_on()
        ag.finish()

    return pl.pallas_call(
        body,
        name="weights_all_gather",
        out_shape=jax.ShapeDtypeStruct((N_DEV * m_per, n), shard.dtype),
        in_specs=[pl.BlockSpec(memory_space=pl.ANY)],
        out_specs=pl.BlockSpec(memory_space=pl.ANY),
        scratch_shapes=list(GATHER_SEMS),
    )(shard)


PROJ_PARTS = D_MODEL // PLE_DIM
N_WEIGHT_COPIES = N_DEV * (3 + PROJ_PARTS)


def _weight_copies(wg_ref, w_in_t, w_out, w_gate, w_proj_t, sems):
    copies = []
    for s in range(N_DEV):
        base = s * ROWS_LAYER
        for dst, off, rows in ((w_in_t, OFF_IN, ROWS_IN), (w_out, OFF_OUT, ROWS_OUT), (w_gate, OFF_GATE, ROWS_GATE)):
            copies.append((wg_ref.at[pl.ds(base + off, rows), :], dst.at[pl.ds(s * rows, rows), :]))
        for j in range(PROJ_PARTS):
            copies.append((
                wg_ref.at[pl.ds(base + OFF_PROJ, ROWS_PROJ), pl.ds(j * PLE_DIM, PLE_DIM)],
                w_proj_t.at[pl.ds(s * ROWS_OUT + j * ROWS_PROJ, ROWS_PROJ), :],
            ))
    return [pltpu.make_async_copy(src, dst, sems.at[k]) for k, (src, dst) in enumerate(copies)]


def _forward_layer(layer, x, p_all, wg, conv_k, norm_g, ln_g, ln_b, w_mix, b_mix, ple_g, next_shard=None):
    t = x.shape[0]
    tm = _tile(t, 512)
    nt = t // tm
    gathers = next_shard is not None

    def body(*refs):
        (x_ref, p_ref, wg_ref, cw_ref, ng_ref, lng_ref, lnb_ref, wm_ref, bm_ref, pg_ref) = refs[:10]
        refs = refs[10:]
        if gathers:
            shard_ref, refs = refs[0], refs[1:]
        (proj_ref, hn_ref, cat_ref, r_ref, gpre_ref, x1_ref, x2_ref) = refs[:7]
        refs = refs[7:]
        if gathers:
            gathered_ref, refs = refs[0], refs[1:]
        (w_in_t, w_out, w_gate, wpt_ref, vln_s, mixed_s, halo_s, sems) = refs[:8]
        i = pl.program_id(0)
        if gathers:
            ag = _TwoLevelGather(shard_ref, gathered_ref, ROWS_LAYER, *refs[8:11])

            @pl.when(i == 0)
            def _():
                ag.start()

            @pl.when(i == nt // 2)
            def _():
                ag.pass_on()

        @pl.when(i == 0)
        def _():
            copies = _weight_copies(wg_ref, w_in_t, w_out, w_gate, wpt_ref, sems)
            for cp in copies:
                cp.start()
            halo_s[...] = jnp.zeros_like(halo_s)
            for cp in copies:
                cp.wait()

        xv = x_ref[...]
        rstd0 = lax.rsqrt(jnp.mean(xv * xv, axis=-1, keepdims=True) + EPS)
        hn_ref[...] = (xv * rstd0 * ng_ref[...]).astype(BF16)

        def proj_section(k):
            sec = _dot_nt(hn_ref[...], w_in_t[pl.ds(k * 512, 512), :])
            proj_ref[:, k * 512:(k + 1) * 512] = sec.astype(BF16)
            return sec

        v = proj_section(1)
        mu = jnp.mean(v, axis=-1, keepdims=True)
        vc = v - mu
        var = jnp.mean(vc * vc, axis=-1, keepdims=True)
        vln = vc * lax.rsqrt(var + EPS) * lng_ref[...] + lnb_ref[...]
        vln_s[...] = vln.astype(BF16)
        for ci in range(tm // CHUNK):
            rows = pl.ds(ci * CHUNK, CHUNK)
            for h in range(HEADS_A):
                cols = pl.ds(h * HEAD_DIM, HEAD_DIM)
                mixed_s[rows, cols] = _dot(wm_ref[h], vln_s[rows, cols]) + bm_ref[h]
        u = proj_section(0)
        za = proj_section(2)
        out_a = u * mixed_s[...] * (za * _sigmoid(za))
        cat_ref[:, 0:512] = out_a.astype(BF16)

        xc = proj_section(5) * proj_section(3)
        prev = halo_s[...]
        row = lax.broadcasted_iota(jnp.int32, (tm, WIDTH_B), 0)
        xc_m1 = jnp.where(row == 0, prev[7:8, :], pltpu.roll(xc, 1, 0))
        xc_m2 = jnp.where(row == 0, prev[6:7, :], jnp.where(row == 1, prev[7:8, :], pltpu.roll(xc, 2, 0)))
        halo_s[...] = xc[tm - 8:tm, :]
        cw = cw_ref[...]
        yc = cw[0:1, :] * xc_m2 + cw[1:2, :] * xc_m1 + cw[2:3, :] * xc
        zb = proj_section(6)
        out_b = proj_section(4) * yc * (zb * _sigmoid(zb))
        cat_ref[:, 512:1024] = out_b.astype(BF16)

        x1 = xv + _dot(cat_ref[...], w_out[...])
        x1_ref[...] = x1
        rstd1 = lax.rsqrt(jnp.mean(x1 * x1, axis=-1, keepdims=True) + EPS)
        r_ref[...] = (x1 * rstd1 * pg_ref[...]).astype(BF16)
        gpre = _dot(r_ref[...], w_gate[...])
        gpre_ref[...] = gpre.astype(BF16)
        pp = _dot_nt(p_ref[...].astype(BF16), wpt_ref[...])
        x2_ref[...] = x1 + _sigmoid(gpre) * pp

        if gathers:
            @pl.when(i == nt - 1)
            def _():
                ag.finish()

    def tok(width):
        return pl.BlockSpec((tm, width), lambda i: (i, 0))

    def whole(shape):
        return pl.BlockSpec(shape, lambda i: (0,) * len(shape))

    hbm = pl.BlockSpec(memory_space=pl.ANY)
    operands = [x, p_all, wg, conv_k, norm_g, ln_g, ln_b, w_mix, b_mix, ple_g]
    in_specs = [
        tok(D_MODEL), pl.BlockSpec((None, None, tm, PLE_DIM), lambda i: (layer, 0, i, 0)), hbm,
        whole((8, WIDTH_B)), whole((1, D_MODEL)), whole((1, WIDTH_A)), whole((1, WIDTH_A)),
        whole((HEADS_A, CHUNK, CHUNK)), whole((HEADS_A, CHUNK, HEAD_DIM)), whole((1, D_MODEL)),
    ]
    out_specs = [tok(PROJ_WIDTH), tok(D_MODEL), tok(D_MODEL), tok(D_MODEL), tok(D_MODEL), tok(D_MODEL), tok(D_MODEL)]
    out_shape = [
        jax.ShapeDtypeStruct((t, PROJ_WIDTH), BF16),
        jax.ShapeDtypeStruct((t, D_MODEL), BF16),
        jax.ShapeDtypeStruct((t, D_MODEL), BF16),
        jax.ShapeDtypeStruct((t, D_MODEL), BF16),
        jax.ShapeDtypeStruct((t, D_MODEL), BF16),
        jax.ShapeDtypeStruct((t, D_MODEL), F32),
        jax.ShapeDtypeStruct((t, D_MODEL), F32),
    ]
    scratch_shapes = [
        pltpu.VMEM((PROJ_WIDTH, D_MODEL), BF16),
        pltpu.VMEM((D_MODEL, D_MODEL), BF16),
        pltpu.VMEM((D_MODEL, D_MODEL), BF16),
        pltpu.VMEM((D_MODEL, PLE_DIM), BF16),
        pltpu.VMEM((tm, WIDTH_A), BF16),
        pltpu.VMEM((tm, WIDTH_A), F32),
        pltpu.VMEM((8, WIDTH_B), F32),
        pltpu.SemaphoreType.DMA((N_WEIGHT_COPIES,)),
    ]
    if gathers:
        operands.append(next_shard)
        in_specs.append(hbm)
        out_specs.append(hbm)
        out_shape.append(jax.ShapeDtypeStruct((N_DEV * ROWS_LAYER, D_MODEL), BF16))
        scratch_shapes += list(GATHER_SEMS)

    return pl.pallas_call(
        body,
        name=f"layer{layer}_forward",
        grid=(nt,),
        in_specs=in_specs,
        out_specs=out_specs,
        out_shape=out_shape,
        scratch_shapes=scratch_shapes,
        compiler_params=pltpu.CompilerParams(dimension_semantics=("arbitrary",), vmem_limit_bytes=56 * MIB),
    )(*operands)


class _DirectScatter:
    def __init__(self, pack_ref, pieces_ref, send_sems, recv_sems, local_sem):
        x, y, c = lax.axis_index("x"), lax.axis_index("y"), lax.axis_index("c")
        me = 4 * x + 2 * y + c
        self.copies = []
        for k in range(N_DEV - 1):
            fx, fy, fc = ((k + 1) >> 2) & 1, ((k + 1) >> 1) & 1, (k + 1) & 1
            tx, ty, tc = x ^ fx, y ^ fy, c ^ fc
            self.copies.append(
                pltpu.make_async_remote_copy(
                    src_ref=pack_ref.at[4 * tx + 2 * ty + tc], dst_ref=pieces_ref.at[me],
                    send_sem=send_sems.at[k], recv_sem=recv_sems.at[k],
                    device_id=(tx, ty, tc), device_id_type=MESH,
                )
            )
        self.mine = pltpu.make_async_copy(pack_ref.at[me], pieces_ref.at[me], local_sem)

    def start(self):
        self.mine.start()
        for cp in self.copies:
            cp.start()

    def finish(self):
        for cp in self.copies:
            cp.wait_recv()
        for cp in self.copies:
            cp.wait_send()
        self.mine.wait()


SCATTER_SEMS = [pltpu.SemaphoreType.DMA((N_DEV - 1,)), pltpu.SemaphoreType.DMA((N_DEV - 1,)), pltpu.SemaphoreType.DMA]


def _backward_layer(layer, dx2, x_in, x1, proj, gpre, p_all, wg, conv_k, norm_g, ln_g, ln_b,
                    w_mix, w_mix_t, b_mix, ple_g, loss_head=None):
    t = x_in.shape[0]
    tm = _tile(t, 256)
    nt = t // tm
    n_chunks = tm // CHUNK
    halo_rows = 16
    heads = loss_head is not None

    def body(*refs):
        (dx2_ref, xin_ref, x1_ref, proj_ref, halo_ref, gpre_ref, p_ref, wg_ref, cw_ref,
         ng_ref, lng_ref, lnb_ref, wm_ref, wmt_ref, bm_ref, pg_ref) = refs[:16]
        refs = refs[16:]
        if heads:
            tgt_ref, fg_ref = refs[:2]
            refs = refs[2:]
        (dxin_ref, dproj_ref, dx1_ref, dgpre_ref, dpp_ref, small_ref, dws_ref) = refs[:7]
        refs = refs[7:]
        if heads:
            head_ref, refs = refs[0], refs[1:]
        (w_in_t, w_out, w_gate, wpt_ref, vln_s, mixed_s, dmix_s, dvln_s, carry_s,
         ng_acc, pg_acc, lng_acc, lnb_acc, cw_acc, dbm_ref, sems) = refs[:16]
        if heads:
            loss_acc, fg_acc = refs[16:18]
        i = pl.program_id(0)
        tile = nt - 1 - i

        @pl.when(i == 0)
        def _():
            copies = _weight_copies(wg_ref, w_in_t, w_out, w_gate, wpt_ref, sems)
            for cp in copies:
                cp.start()
            if heads:
                loss_acc[...] = jnp.zeros_like(loss_acc)
                fg_acc[...] = jnp.zeros_like(fg_acc)
            carry_s[...] = jnp.zeros_like(carry_s)
            ng_acc[...] = jnp.zeros_like(ng_acc)
            pg_acc[...] = jnp.zeros_like(pg_acc)
            lng_acc[...] = jnp.zeros_like(lng_acc)
            lnb_acc[...] = jnp.zeros_like(lnb_acc)
            cw_acc[...] = jnp.zeros_like(cw_acc)
            dws_ref[...] = jnp.zeros_like(dws_ref)
            dbm_ref[...] = jnp.zeros_like(dbm_ref)
            for cp in copies:
                cp.wait()

        if heads:
            x2v = dx2_ref[...]
            fg = fg_ref[...]
            rstdf = lax.rsqrt(jnp.mean(x2v * x2v, axis=-1, keepdims=True) + EPS)
            xhatf = x2v * rstdf
            err = xhatf * fg - tgt_ref[...]
            loss_acc[...] += _colsum8(err * err)
            dy = err * (1.0 / D_MODEL)
            fg_acc[...] += _colsum8(dy * xhatf)
            dxhf = dy * fg
            dx2v = rstdf * (dxhf - xhatf * jnp.mean(dxhf * xhatf, axis=-1, keepdims=True))
        else:
            dx2v = dx2_ref[...]

        gate = _sigmoid(gpre_ref[...].astype(F32))
        pp = _dot_nt(p_ref[...].astype(BF16), wpt_ref[...])
        dpp_ref[...] = (dx2v * gate).astype(BF16)
        dgpre = (dx2v * pp * gate * (1.0 - gate)).astype(BF16)
        dgpre_ref[...] = dgpre
        dr = _dot_nt(dgpre, w_gate[...])
        x1v = x1_ref[...]
        rstd1 = lax.rsqrt(jnp.mean(x1v * x1v, axis=-1, keepdims=True) + EPS)
        xhat1 = x1v * rstd1
        pg_acc[...] += _colsum8(dr * xhat1)
        dxh = dr * pg_ref[...]
        dx1 = dx2v + rstd1 * (dxh - xhat1 * jnp.mean(dxh * xhat1, axis=-1, keepdims=True))
        dx1b = dx1.astype(BF16)
        dx1_ref[...] = dx1b

        dcat = _dot_nt(dx1b, w_out[...])
        dca = dcat[:, 0:512]
        dcb = dcat[:, 512:1024]

        u = proj_ref[:, 0:512].astype(F32)
        v = proj_ref[:, 512:1024].astype(F32)
        za = proj_ref[:, 1024:1536].astype(F32)
        mu = jnp.mean(v, axis=-1, keepdims=True)
        vc = v - mu
        var = jnp.mean(vc * vc, axis=-1, keepdims=True)
        rs = lax.rsqrt(var + EPS)
        vhat = vc * rs
        lng = lng_ref[...]
        vln_s[...] = (vhat * lng + lnb_ref[...]).astype(BF16)
        for ci in range(n_chunks):
            rows = pl.ds(ci * CHUNK, CHUNK)
            for h in range(HEADS_A):
                cols = pl.ds(h * HEAD_DIM, HEAD_DIM)
                mixed_s[rows, cols] = _dot(wm_ref[h], vln_s[rows, cols]) + bm_ref[h]
        mixed = mixed_s[...]
        sga = _sigmoid(za)
        sa = za * sga
        dsa = sga * (1.0 + za * (1.0 - sga))
        def put_section(k, val):
            dproj_ref[:, k * 512:(k + 1) * 512] = val.astype(BF16)

        put_section(0, dca * mixed * sa)
        dmix = dca * u * sa
        put_section(2, dca * u * mixed * dsa)
        dmix_s[...] = dmix.astype(BF16)
        dbm_acc = jnp.zeros((CHUNK, WIDTH_A), F32)
        for ci in range(n_chunks):
            rows = pl.ds(ci * CHUNK, CHUNK)
            dbm_acc = dbm_acc + dmix[ci * CHUNK:(ci + 1) * CHUNK, :]
            for h in range(HEADS_A):
                cols = pl.ds(h * HEAD_DIM, HEAD_DIM)
                dvln_s[rows, cols] = _dot(wmt_ref[h], dmix_s[rows, cols])
                dws_ref[:, cols] += _dot_nt(dmix_s[rows, cols], vln_s[rows, cols])
        dbm_ref[...] += dbm_acc
        dvln = dvln_s[...]
        lng_acc[...] += _colsum8(dvln * vhat)
        lnb_acc[...] += _colsum8(dvln)
        dvh = dvln * lng
        dv = rs * (dvh - jnp.mean(dvh, axis=-1, keepdims=True) - vhat * jnp.mean(dvh * vhat, axis=-1, keepdims=True))
        put_section(1, dv)

        hb = proj_ref[:, 1536:2048].astype(F32)
        gb = proj_ref[:, 2048:2560].astype(F32)
        gc = proj_ref[:, 2560:3072].astype(F32)
        zb = proj_ref[:, 3072:3584].astype(F32)
        xc = gc * hb
        prev = halo_ref[:, 2560:3072].astype(F32) * halo_ref[:, 1536:2048].astype(F32)
        prev = jnp.where(tile > 0, prev, 0.0)
        row = lax.broadcasted_iota(jnp.int32, (tm, WIDTH_B), 0)
        p1 = prev[halo_rows - 1:halo_rows, :]
        p2 = prev[halo_rows - 2:halo_rows - 1, :]
        xc_m1 = jnp.where(row == 0, p1, pltpu.roll(xc, 1, 0))
        xc_m2 = jnp.where(row == 0, p2, jnp.where(row == 1, p1, pltpu.roll(xc, 2, 0)))
        cw = cw_ref[...]
        yc = cw[0:1, :] * xc_m2 + cw[1:2, :] * xc_m1 + cw[2:3, :] * xc
        sgb = _sigmoid(zb)
        sb = zb * sgb
        dsb = sgb * (1.0 + zb * (1.0 - sgb))
        put_section(4, dcb * yc * sb)
        dyc = dcb * gb * sb
        put_section(6, dcb * gb * yc * dsb)
        nxt = carry_s[...]
        dyc_p1 = jnp.where(row == tm - 1, nxt[0:1, :], pltpu.roll(dyc, tm - 1, 0))
        dyc_p2 = jnp.where(row == tm - 1, nxt[1:2, :], jnp.where(row == tm - 2, nxt[0:1, :], pltpu.roll(dyc, tm - 2, 0)))
        carry_s[...] = dyc[0:8, :]
        dxc = cw[2:3, :] * dyc + cw[1:2, :] * dyc_p1 + cw[0:1, :] * dyc_p2
        cw_acc[0] += _colsum8(dyc * xc_m2)
        cw_acc[1] += _colsum8(dyc * xc_m1)
        cw_acc[2] += _colsum8(dyc * xc)
        put_section(3, dxc * gc)
        put_section(5, dxc * hb)

        dhn = _dot(dproj_ref[...], w_in_t[...])
        xv = xin_ref[...]
        rstd0 = lax.rsqrt(jnp.mean(xv * xv, axis=-1, keepdims=True) + EPS)
        xhat0 = xv * rstd0
        ng_acc[...] += _colsum8(dhn * xhat0)
        dxh0 = dhn * ng_ref[...]
        dxin_ref[...] = dx1 + rstd0 * (dxh0 - xhat0 * jnp.mean(dxh0 * xhat0, axis=-1, keepdims=True))

        @pl.when(i == nt - 1)
        def _():
            small_ref[...] = jnp.zeros_like(small_ref)
            small_ref[SMALL_NORM:SMALL_NORM + 1, :] = jnp.sum(ng_acc[...], axis=0, keepdims=True)
            small_ref[SMALL_PLE:SMALL_PLE + 1, :] = jnp.sum(pg_acc[...], axis=0, keepdims=True)
            small_ref[SMALL_LN:SMALL_LN + 1, 0:WIDTH_A] = jnp.sum(lng_acc[...], axis=0, keepdims=True)
            small_ref[SMALL_LN:SMALL_LN + 1, WIDTH_A:2 * WIDTH_A] = jnp.sum(lnb_acc[...], axis=0, keepdims=True)
            for h in range(HEADS_A):
                cols = pl.ds(h * HEAD_DIM, HEAD_DIM)
                small_ref[SMALL_BS:SMALL_BS + 1, cols] = jnp.sum(jnp.transpose(dbm_ref[:, cols]), axis=0, keepdims=True)
            for k in range(3):
                small_ref[SMALL_CONV + k:SMALL_CONV + k + 1, 0:WIDTH_B] = jnp.sum(cw_acc[k], axis=0, keepdims=True)
            if heads:
                total = jnp.sum(loss_acc[...]) * (0.5 / D_MODEL)
                rows8 = lax.broadcasted_iota(jnp.int32, (SMALL_ROWS, D_MODEL), 0)
                lanes8 = lax.broadcasted_iota(jnp.int32, (SMALL_ROWS, D_MODEL), 1)
                head_ref[...] = jnp.where((rows8 == HEAD_LOSS) & (lanes8 == 0), total, 0.0)
                head_ref[HEAD_FINAL:HEAD_FINAL + 1, :] = jnp.sum(fg_acc[...], axis=0, keepdims=True)

    def tok(width):
        return pl.BlockSpec((tm, width), lambda i: (nt - 1 - i, 0))

    def whole(shape):
        return pl.BlockSpec(shape, lambda i: (0,) * len(shape))

    halo_spec = pl.BlockSpec(
        (halo_rows, PROJ_WIDTH), lambda i: (jnp.maximum((nt - 1 - i) * (tm // halo_rows) - 1, 0), 0)
    )
    hbm = pl.BlockSpec(memory_space=pl.ANY)
    operands = [dx2, x_in, x1, proj, proj, gpre, p_all, wg, conv_k, norm_g, ln_g, ln_b, w_mix, w_mix_t, b_mix, ple_g]
    in_specs = [
        tok(D_MODEL), tok(D_MODEL), tok(D_MODEL), tok(PROJ_WIDTH), halo_spec, tok(D_MODEL),
        pl.BlockSpec((None, None, tm, PLE_DIM), lambda i: (layer, 0, nt - 1 - i, 0)), hbm,
        whole((8, WIDTH_B)), whole((1, D_MODEL)), whole((1, WIDTH_A)), whole((1, WIDTH_A)),
        whole((HEADS_A, CHUNK, CHUNK)), whole((HEADS_A, CHUNK, CHUNK)), whole((HEADS_A, CHUNK, HEAD_DIM)),
        whole((1, D_MODEL)),
    ]
    out_specs = [
        tok(D_MODEL), tok(PROJ_WIDTH), tok(D_MODEL), tok(D_MODEL), tok(D_MODEL),
        whole((SMALL_ROWS, D_MODEL)), whole((CHUNK, WIDTH_A)),
    ]
    out_shape = [
        jax.ShapeDtypeStruct((t, D_MODEL), F32),
        jax.ShapeDtypeStruct((t, PROJ_WIDTH), BF16),
        jax.ShapeDtypeStruct((t, D_MODEL), BF16),
        jax.ShapeDtypeStruct((t, D_MODEL), BF16),
        jax.ShapeDtypeStruct((t, D_MODEL), BF16),
        jax.ShapeDtypeStruct((SMALL_ROWS, D_MODEL), F32),
        jax.ShapeDtypeStruct((CHUNK, WIDTH_A), F32),
    ]
    scratch_shapes = [
        pltpu.VMEM((PROJ_WIDTH, D_MODEL), BF16),
        pltpu.VMEM((D_MODEL, D_MODEL), BF16),
        pltpu.VMEM((D_MODEL, D_MODEL), BF16),
        pltpu.VMEM((D_MODEL, PLE_DIM), BF16),
        pltpu.VMEM((tm, WIDTH_A), BF16),
        pltpu.VMEM((tm, WIDTH_A), F32),
        pltpu.VMEM((tm, WIDTH_A), BF16),
        pltpu.VMEM((tm, WIDTH_A), F32),
        pltpu.VMEM((8, WIDTH_B), F32),
        pltpu.VMEM((8, D_MODEL), F32),
        pltpu.VMEM((8, D_MODEL), F32),
        pltpu.VMEM((8, WIDTH_A), F32),
        pltpu.VMEM((8, WIDTH_A), F32),
        pltpu.VMEM((3, 8, WIDTH_B), F32),
        pltpu.VMEM((CHUNK, WIDTH_A), F32),
        pltpu.SemaphoreType.DMA((N_WEIGHT_COPIES,)),
    ]
    if heads:
        operands += list(loss_head)
        in_specs += [tok(D_MODEL), whole((1, D_MODEL))]
        out_specs.append(whole((SMALL_ROWS, D_MODEL)))
        out_shape.append(jax.ShapeDtypeStruct((SMALL_ROWS, D_MODEL), F32))
        scratch_shapes += [pltpu.VMEM((8, D_MODEL), F32), pltpu.VMEM((8, D_MODEL), F32)]

    return pl.pallas_call(
        body,
        name=f"layer{layer}_backward",
        grid=(nt,),
        in_specs=in_specs,
        out_specs=out_specs,
        out_shape=out_shape,
        scratch_shapes=scratch_shapes,
        compiler_params=pltpu.CompilerParams(dimension_semantics=("arbitrary",), vmem_limit_bytes=56 * MIB),
    )(*operands)


def _sum_pieces(layer, pieces):
    rows, n = pieces.shape[1], pieces.shape[2]
    blocks = 2
    rb = rows // blocks

    def body(p_ref, out_ref):
        total = p_ref[0].astype(F32)
        for j in range(1, N_DEV):
            total = total + p_ref[j].astype(F32)
        out_ref[...] = total

    return pl.pallas_call(
        body,
        name=f"layer{layer}_grad_sum",
        grid=(blocks,),
        out_shape=jax.ShapeDtypeStruct((rows, n), F32),
        in_specs=[pl.BlockSpec((N_DEV, rb, n), lambda i: (0, i, 0))],
        out_specs=pl.BlockSpec((rb, n), lambda i: (i, 0)),
        compiler_params=pltpu.CompilerParams(dimension_semantics=("arbitrary",), vmem_limit_bytes=32 * MIB),
    )(pieces)


def _weight_grads(layer, dproj, hn, cat, dx1, r, dgpre, dpp, p_all, scatter_pack=None):
    t = hn.shape[0]
    tk = _tile(t, 512)
    nt = t // tk
    in_blocks = PROJ_WIDTH // 512
    scatters = scatter_pack is not None

    def body(*refs):
        (dproj_ref, hn_ref, cat_ref, dx1_ref, r_ref, dgpre_ref, dpp_ref, p_ref) = refs[:8]
        refs = refs[8:]
        if scatters:
            prior_ref, refs = refs[0], refs[1:]
        pack_ref, refs = refs[0], refs[1:]
        if scatters:
            pieces_ref, refs = refs[0], refs[1:]
        (acc_in, acc_out, acc_gate, acc_proj, stage, sems) = refs[:6]
        i = pl.program_id(0)
        if scatters:
            scatter = _DirectScatter(prior_ref, pieces_ref, *refs[6:9])

            @pl.when(i == 0)
            def _():
                scatter.start()

        @pl.when(i == 0)
        def _():
            acc_in[...] = jnp.zeros_like(acc_in)
            acc_out[...] = jnp.zeros_like(acc_out)
            acc_gate[...] = jnp.zeros_like(acc_gate)
            acc_proj[...] = jnp.zeros_like(acc_proj)

        hnv = hn_ref[...]
        for b in range(in_blocks):
            acc_in[pl.ds(b * 512, 512), :] += _dot_tn(dproj_ref[:, b * 512:(b + 1) * 512], hnv)
        dx1v = dx1_ref[...]
        dgv = dgpre_ref[...]
        for b in range(D_MODEL // 512):
            acc_out[pl.ds(b * 512, 512), :] += _dot_tn(cat_ref[:, b * 512:(b + 1) * 512], dx1v)
            acc_gate[pl.ds(b * 512, 512), :] += _dot_tn(r_ref[:, b * 512:(b + 1) * 512], dgv)
        pv = p_ref[...].astype(BF16)
        for b in range(D_MODEL // 512):
            acc_proj[pl.ds(b * 512, 512), :] += _dot_tn(dpp_ref[:, b * 512:(b + 1) * 512], pv)

        @pl.when(i == nt - 1)
        def _():
            def out_copy(s):
                return pltpu.make_async_copy(stage.at[s % 2], pack_ref.at[s], sems.at[s % 2])

            for s in range(N_DEV):
                if s >= 2:
                    out_copy(s - 2).wait()
                buf = stage.at[s % 2]
                buf[pl.ds(OFF_IN, ROWS_IN), :] = acc_in[pl.ds(s * ROWS_IN, ROWS_IN), :].astype(BF16)
                buf[pl.ds(OFF_OUT, ROWS_OUT), :] = acc_out[pl.ds(s * ROWS_OUT, ROWS_OUT), :].astype(BF16)
                buf[pl.ds(OFF_GATE, ROWS_GATE), :] = acc_gate[pl.ds(s * ROWS_GATE, ROWS_GATE), :].astype(BF16)
                for j in range(D_MODEL // PLE_DIM):
                    buf[pl.ds(OFF_PROJ, ROWS_PROJ), pl.ds(j * PLE_DIM, PLE_DIM)] = acc_proj[
                        pl.ds(s * ROWS_OUT + j * ROWS_PROJ, ROWS_PROJ), :
                    ].astype(BF16)
                out_copy(s).start()
            out_copy(N_DEV - 2).wait()
            out_copy(N_DEV - 1).wait()
            if scatters:
                scatter.finish()

    def tok(width):
        return pl.BlockSpec((tk, width), lambda i: (i, 0))

    hbm = pl.BlockSpec(memory_space=pl.ANY)
    pack_shape = jax.ShapeDtypeStruct((N_DEV, ROWS_GRAD, D_MODEL), BF16)
    operands = [dproj, hn, cat, dx1, r, dgpre, dpp, p_all]
    in_specs = [tok(PROJ_WIDTH), tok(D_MODEL), tok(D_MODEL), tok(D_MODEL), tok(D_MODEL), tok(D_MODEL), tok(D_MODEL),
                pl.BlockSpec((None, None, tk, PLE_DIM), lambda i: (layer, 0, i, 0))]
    out_specs, out_shape = [hbm], [pack_shape]
    scratch_shapes = [
        pltpu.VMEM((PROJ_WIDTH, D_MODEL), F32),
        pltpu.VMEM((D_MODEL, D_MODEL), F32),
        pltpu.VMEM((D_MODEL, D_MODEL), F32),
        pltpu.VMEM((D_MODEL, PLE_DIM), F32),
        pltpu.VMEM((2, ROWS_GRAD, D_MODEL), BF16),
        pltpu.SemaphoreType.DMA((2,)),
    ]
    if scatters:
        operands.append(scatter_pack)
        in_specs.append(hbm)
        out_specs.append(hbm)
        out_shape.append(pack_shape)
        scratch_shapes += list(SCATTER_SEMS)

    return pl.pallas_call(
        body,
        name=f"layer{layer}_weight_grads",
        grid=(nt,),
        in_specs=in_specs,
        out_specs=out_specs,
        out_shape=out_shape,
        scratch_shapes=scratch_shapes,
        compiler_params=pltpu.CompilerParams(dimension_semantics=("arbitrary",), vmem_limit_bytes=58 * MIB),
    )(*operands)


def _reduce_scatter_all_reduce(layer, pack, smalls, head, dws):
    rows, n = pack.shape[1], pack.shape[2]
    assert DEPTH * WIDTH_A == D_MODEL and n == D_MODEL

    def body(g_ref, *refs):
        small_refs, refs = refs[:DEPTH], refs[DEPTH:]
        head_ref, refs = refs[0], refs[1:]
        dws_refs, refs = refs[:DEPTH], refs[DEPTH:]
        (out_ref, total_ref, r1, a_s, r2, sp, sr1, sq, send1, recv1, send2, recv2, ssend, srecv) = refs
        x, y, c = lax.axis_index("x"), lax.axis_index("y"), lax.axis_index("c")
        sibling = (x, y, 1 - c)
        chip = 2 * x + y
        flips = [(1, 0), (0, 1), (1, 1)]

        for l in range(DEPTH):
            sp[l * SMALL_ROWS:(l + 1) * SMALL_ROWS, :] = small_refs[l][...]
            sp[TOTAL_WS:TOTAL_ROWS, l * WIDTH_A:(l + 1) * WIDTH_A] = dws_refs[l][...]
        sp[TOTAL_HEAD:TOTAL_WS, :] = head_ref[...]

        small_pair = pltpu.make_async_remote_copy(
            src_ref=sp, dst_ref=sr1, send_sem=ssend.at[0], recv_sem=srecv.at[0], device_id=sibling, device_id_type=MESH
        )

        def to_sibling(j):
            return pltpu.make_async_remote_copy(
                src_ref=g_ref.at[2 * j + 1 - c], dst_ref=r1.at[j], send_sem=send1.at[j], recv_sem=recv1.at[j],
                device_id=sibling, device_id_type=MESH,
            )

        first = [to_sibling(j) for j in range(4)]
        small_pair.start()
        for cp in first:
            cp.start()

        small_pair.wait_recv()
        sq[chip] = sp[...] + sr1[...]
        small_chips = [
            pltpu.make_async_remote_copy(
                src_ref=sq.at[chip], dst_ref=sq.at[chip], send_sem=ssend.at[1 + k], recv_sem=srecv.at[1 + k],
                device_id=(x ^ fx, y ^ fy, c), device_id_type=MESH,
            )
            for k, (fx, fy) in enumerate(flips)
        ]
        for cp in small_chips:
            cp.start()

        def to_chip(j):
            return pltpu.make_async_remote_copy(
                src_ref=a_s.at[j], dst_ref=r2.at[chip], send_sem=send2.at[j], recv_sem=recv2.at[chip],
                device_id=(j // 2, j % 2, c), device_id_type=MESH,
            )

        def from_chip(k):
            return pltpu.make_async_remote_copy(
                src_ref=a_s.at[k], dst_ref=r2.at[k], send_sem=send2.at[k], recv_sem=recv2.at[k],
                device_id=(k // 2, k % 2, c), device_id_type=MESH,
            )

        for j in range(4):
            first[j].wait_recv()

            @pl.when(chip != j)
            def _():
                a_s[j] = (g_ref[2 * j + c].astype(F32) + r1[j].astype(F32)).astype(BF16)
                to_chip(j).start()

        out_ref[...] = g_ref[2 * chip + c].astype(F32) + r1[chip].astype(F32)
        for cp in small_chips:
            cp.wait_recv()
        total_ref[...] = ((sq[0] + sq[1]) + sq[2]) + sq[3]
        for k in range(4):
            @pl.when(chip != k)
            def _():
                from_chip(k).wait_recv()
                out_ref[...] += r2[k].astype(F32)
        small_pair.wait_send()
        for cp in first + small_chips:
            cp.wait_send()
        for j in range(4):
            @pl.when(chip != j)
            def _():
                to_chip(j).wait_send()

    vmem = pl.BlockSpec(memory_space=pltpu.VMEM)
    return pl.pallas_call(
        body,
        name=f"layer{layer}_grad_reduce_scatter",
        out_shape=[jax.ShapeDtypeStruct((rows, n), F32), jax.ShapeDtypeStruct((TOTAL_ROWS, D_MODEL), F32)],
        in_specs=[vmem] * (2 + 2 * DEPTH),
        out_specs=[vmem, vmem],
        scratch_shapes=[
            pltpu.VMEM((4, rows, n), BF16),
            pltpu.VMEM((4, rows, n), BF16),
            pltpu.VMEM((4, rows, n), BF16),
            pltpu.VMEM((TOTAL_ROWS, D_MODEL), F32),
            pltpu.VMEM((TOTAL_ROWS, D_MODEL), F32),
            pltpu.VMEM((4, TOTAL_ROWS, D_MODEL), F32),
            pltpu.SemaphoreType.DMA((4,)),
            pltpu.SemaphoreType.DMA((4,)),
            pltpu.SemaphoreType.DMA((4,)),
            pltpu.SemaphoreType.DMA((4,)),
            pltpu.SemaphoreType.DMA((4,)),
            pltpu.SemaphoreType.DMA((4,)),
        ],
        compiler_params=pltpu.CompilerParams(vmem_limit_bytes=48 * MIB),
    )(pack, *smalls, head, *dws)


def _adam_step(w, g, m, v):
    m = ADAM_B1 * m + (1.0 - ADAM_B1) * g
    v = ADAM_B2 * v + (1.0 - ADAM_B2) * (g * g)
    m_hat = m / (1.0 - ADAM_B1 ** ADAM_STEP)
    v_hat = v / (1.0 - ADAM_B2 ** ADAM_STEP)
    return -ADAM_LR * (m_hat / (jnp.sqrt(v_hat) + ADAM_EPS) + ADAM_WD * w), m, v


def _adamw_rows(name, reduced, row_off, states):
    n = len(states)

    def body(*refs):
        red = refs[:DEPTH]
        ins = refs[DEPTH:DEPTH + 3 * n]
        outs = refs[DEPTH + 3 * n:]
        layer = pl.program_id(0)
        for l in range(DEPTH):
            @pl.when(layer == l)
            def _():
                for k in range(n):
                    w_ref, m_ref, v_ref = ins[3 * k:3 * k + 3]
                    g_ref, d_ref, nm_ref, nv_ref = outs[4 * k:4 * k + 4]
                    g = red[l][row_off[k]:row_off[k] + w_ref.shape[0], :]
                    d, m, v = _adam_step(w_ref[...], g, m_ref[...], v_ref[...])
                    g_ref[...] = g
                    d_ref[...] = d
                    nm_ref[...] = m
                    nv_ref[...] = v

    flat = [a for st in states for a in st]
    state_specs, out_specs, out_shape = [], [], []
    for w, _, _ in states:
        spec = pl.BlockSpec((None,) + w.shape[1:], lambda l: (l, 0, 0))
        state_specs += [spec] * 3
        out_specs += [spec] * 4
        out_shape += [jax.ShapeDtypeStruct(w.shape, F32)] * 4
    red_specs = [pl.BlockSpec(a.shape, lambda l: (0, 0)) for a in reduced]
    outs = pl.pallas_call(
        body,
        name=name,
        grid=(DEPTH,),
        out_shape=out_shape,
        in_specs=red_specs + state_specs,
        out_specs=out_specs,
        compiler_params=pltpu.CompilerParams(dimension_semantics=("arbitrary",), vmem_limit_bytes=48 * MIB),
    )(*reduced, *flat)
    return [tuple(outs[4 * k:4 * k + 4]) for k in range(n)]


def _adamw_small(total, g_conv, g_proj, st):
    names = ["norm_g", "ple_norm_g", "ln_v_g", "ln_v_b", "b_s", "w_s", "final_g", "conv_w", "w_ple_proj"]
    cut = names[:7]

    def body(total_ref, gconv_ref, gproj_ref, *refs):
        ins = {nm: refs[3 * k:3 * k + 3] for k, nm in enumerate(names)}
        outs, pos = {}, 3 * len(names)
        for nm in names:
            cnt = 4 if nm in cut else 3
            outs[nm] = refs[pos:pos + cnt]
            pos += cnt

        def update(nm, idx, g):
            w_ref, m_ref, v_ref = ins[nm]
            d, m, v = _adam_step(w_ref[idx], g, m_ref[idx], v_ref[idx])
            o = outs[nm]
            if nm in cut:
                o[0][idx] = g
                o = o[1:]
            o[0][idx] = d
            o[1][idx] = m
            o[2][idx] = v

        tril = (lax.broadcasted_iota(jnp.int32, (CHUNK, CHUNK), 0) >= lax.broadcasted_iota(jnp.int32, (CHUNK, CHUNK), 1))
        for l in range(DEPTH):
            base = l * SMALL_ROWS
            row = (slice(l, l + 1), slice(None))
            update("norm_g", row, total_ref[base + SMALL_NORM:base + SMALL_NORM + 1, :])
            update("ple_norm_g", row, total_ref[base + SMALL_PLE:base + SMALL_PLE + 1, :])
            update("ln_v_g", row, total_ref[base + SMALL_LN:base + SMALL_LN + 1, 0:WIDTH_A])
            update("ln_v_b", row, total_ref[base + SMALL_LN:base + SMALL_LN + 1, WIDTH_A:2 * WIDTH_A])
            for h in range(HEADS_A):
                update("b_s", (l, slice(h, h + 1), slice(None)),
                       total_ref[base + SMALL_BS:base + SMALL_BS + 1, h * HEAD_DIM:(h + 1) * HEAD_DIM])
                lanes = slice(l * WIDTH_A + h * CHUNK, l * WIDTH_A + (h + 1) * CHUNK)
                update("w_s", (l, h), jnp.where(tril, total_ref[TOTAL_WS:TOTAL_ROWS, lanes], 0.0))
        update("final_g", (slice(None), slice(None)), total_ref[TOTAL_HEAD + HEAD_FINAL:TOTAL_HEAD + HEAD_FINAL + 1, :])
        update("conv_w", (slice(None),) * 3, gconv_ref[...])
        update("w_ple_proj", (slice(None),) * 3, gproj_ref[...])

    flat = [a for nm in names for a in st[nm]]
    out_shape = []
    for nm in names:
        out_shape += [jax.ShapeDtypeStruct(st[nm][0].shape, F32)] * (4 if nm in cut else 3)
    def whole(a):
        return pl.BlockSpec(a.shape, lambda i: (0,) * len(a.shape))

    operands = [total, g_conv, g_proj, *flat]
    outs = pl.pallas_call(
        body,
        name="adamw_small",
        grid=(1,),
        out_shape=out_shape,
        in_specs=[whole(a) for a in operands],
        out_specs=[whole(a) for a in out_shape],
        compiler_params=pltpu.CompilerParams(dimension_semantics=("arbitrary",), vmem_limit_bytes=32 * MIB),
    )(*operands)
    res, pos = {}, 0
    for nm in names:
        cnt = 4 if nm in cut else 3
        got = tuple(outs[pos:pos + cnt])
        res[nm] = got if nm in cut else ((g_conv if nm == "conv_w" else g_proj),) + got
        pos += cnt
    return res


def _split3_bf16(a):
    b1 = a.astype(BF16)
    r1 = a - b1.astype(F32)
    b2 = r1.astype(BF16)
    b3 = (r1 - b2.astype(F32)).astype(BF16)
    return b1, b2, b3


def _pack_weight_shard(w_in_l, w_out_l, w_gate_l, w_proj_l, conv_w_l):
    w_in_t = jnp.transpose(w_in_l).astype(BF16)
    proj_t = jnp.transpose(w_proj_l).astype(BF16)
    proj_rows = proj_t.reshape(D_MODEL // PLE_DIM, ROWS_PROJ, PLE_DIM).transpose(1, 0, 2).reshape(ROWS_PROJ, D_MODEL)
    conv_parts = jnp.concatenate([b.reshape(-1) for b in _split3_bf16(conv_w_l)])
    conv_rows = jnp.concatenate([conv_parts, jnp.zeros((ROWS_CONV * D_MODEL - conv_parts.shape[0],), BF16)])
    return jnp.concatenate(
        [w_in_t, w_out_l.astype(BF16), w_gate_l.astype(BF16), proj_rows, conv_rows.reshape(ROWS_CONV, D_MODEL)], axis=0
    )


def _unpack_conv(wg):
    per_dev = wg.reshape(N_DEV, ROWS_LAYER, D_MODEL)
    n_conv = (WIDTH_B // N_DEV) * 3
    conv_parts = per_dev[:, OFF_CONV].astype(F32)[:, :3 * n_conv].reshape(N_DEV, 3, n_conv)
    conv = (conv_parts[:, 0] + conv_parts[:, 1]) + conv_parts[:, 2]
    conv_k = jnp.transpose(conv.reshape(WIDTH_B, 3))
    conv_k = jnp.concatenate([conv_k, jnp.zeros((5, WIDTH_B), F32)], axis=0)
    return conv_k


def _unpack_grad_proj(red):
    proj_rows = red[OFF_PROJ:OFF_PROJ + ROWS_PROJ]
    proj_t = proj_rows.reshape(ROWS_PROJ, D_MODEL // PLE_DIM, PLE_DIM).transpose(1, 0, 2).reshape(ROWS_OUT, PLE_DIM)
    return jnp.transpose(proj_t)


def kernel(x, p, norm_g, w_in, ln_v_g, ln_v_b, w_s, b_s, conv_w, w_out, ple_norm_g, w_ple_gate, w_ple_proj, final_g, loss_target, m_norm_g, m_w_in, m_ln_v_g, m_ln_v_b, m_w_s, m_b_s, m_conv_w, m_w_out, m_ple_norm_g, m_w_ple_gate, m_w_ple_proj, m_final_g, v_norm_g, v_w_in, v_ln_v_g, v_ln_v_b, v_w_s, v_b_s, v_conv_w, v_w_out, v_ple_norm_g, v_w_ple_gate, v_w_ple_proj, v_final_g):
    me = 4 * lax.axis_index("x") + 2 * lax.axis_index("y") + lax.axis_index("c")
    xs = x[0]
    target = loss_target[0]

    shards = [_pack_weight_shard(w_in[l], w_out[l], w_ple_gate[l], w_ple_proj[l], conv_w[l]) for l in range(DEPTH)]
    tril = jnp.tril(jnp.ones((CHUNK, CHUNK), F32))

    def consts(l, wg_l):
        conv_k = _unpack_conv(wg_l)
        w_mix = w_s[l] * tril[None]
        return dict(
            wg=wg_l, conv_k=conv_k,
            norm_g=norm_g[l].reshape(1, D_MODEL), ln_g=ln_v_g[l].reshape(1, WIDTH_A), ln_b=ln_v_b[l].reshape(1, WIDTH_A),
            w_mix=w_mix.astype(BF16), w_mix_t=jnp.swapaxes(w_mix, 1, 2).astype(BF16),
            b_mix=jnp.broadcast_to(b_s[l][:, :, None], (HEADS_A, CHUNK, HEAD_DIM)),
            ple_g=ple_norm_g[l].reshape(1, D_MODEL),
        )

    layer_consts = [consts(0, _all_gather_rows(shards[0]))]
    saved = []
    h = xs
    for l in range(DEPTH):
        k = layer_consts[l]
        outs = _forward_layer(
            l, h, p, k["wg"], k["conv_k"], k["norm_g"], k["ln_g"], k["ln_b"], k["w_mix"], k["b_mix"],
            k["ple_g"], next_shard=shards[l + 1] if l + 1 < DEPTH else None)
        proj, hn, cat, r, gpre, x1, x2 = outs[:7]
        if l + 1 < DEPTH:
            layer_consts.append(consts(l + 1, outs[7]))
        saved.append(dict(x_in=h, proj=proj, hn=hn, cat=cat, r=r, gpre=gpre, x1=x1))
        h = x2

    smalls, dws = [None] * DEPTH, [None] * DEPTH
    reduced = [None] * DEPTH
    pending = None
    dx = h
    for l in reversed(range(DEPTH)):
        k, s = layer_consts[l], saved[l]
        outs = _backward_layer(
            l, dx, s["x_in"], s["x1"], s["proj"], s["gpre"], p, k["wg"], k["conv_k"],
            k["norm_g"], k["ln_g"], k["ln_b"], k["w_mix"], k["w_mix_t"], k["b_mix"], k["ple_g"],
            loss_head=(target, final_g.reshape(1, D_MODEL)) if l == DEPTH - 1 else None)
        dx, dproj, dx1, dgpre, dpp, smalls[l], dws[l] = outs[:7]
        if l == DEPTH - 1:
            head = outs[7]
        outs = _weight_grads(l, dproj, s["hn"], s["cat"], dx1, s["r"], dgpre, dpp, p, scatter_pack=pending)
        if pending is not None:
            reduced[l + 1] = _sum_pieces(l + 1, outs[1])
        pending = outs[0]
    reduced[0], total = _reduce_scatter_all_reduce(0, pending, smalls, head, dws)
    grad_x = dx[None]
    loss = total[TOTAL_HEAD + HEAD_LOSS, 0]

    n_ch = WIDTH_B // N_DEV
    g_conv = jnp.stack([total[l * SMALL_ROWS + SMALL_CONV:l * SMALL_ROWS + SMALL_CONV + 3, 0:WIDTH_B] for l in range(DEPTH)], axis=1)
    g_conv = lax.dynamic_slice_in_dim(g_conv, me * n_ch, n_ch, axis=2)
    g_proj = jnp.stack([_unpack_grad_proj(reduced[l]) for l in range(DEPTH)])

    def t_in(a):
        return jnp.swapaxes(a, 1, 2)

    def t_conv(a):
        return jnp.transpose(a, (2, 0, 1))

    (r_in,) = _adamw_rows("adamw_w_in", reduced, [OFF_IN], [(t_in(w_in), t_in(m_w_in), t_in(v_w_in))])
    r_out, r_gate = _adamw_rows(
        "adamw_w_out_gate", reduced, [OFF_OUT, OFF_GATE],
        [(w_out, m_w_out, v_w_out), (w_ple_gate, m_w_ple_gate, v_w_ple_gate)])
    small = _adamw_small(total, g_conv, g_proj, dict(
        norm_g=(norm_g, m_norm_g, v_norm_g), ple_norm_g=(ple_norm_g, m_ple_norm_g, v_ple_norm_g),
        ln_v_g=(ln_v_g, m_ln_v_g, v_ln_v_g), ln_v_b=(ln_v_b, m_ln_v_b, v_ln_v_b),
        b_s=(b_s, m_b_s, v_b_s), w_s=(w_s, m_w_s, v_w_s),
        final_g=tuple(a.reshape(1, D_MODEL) for a in (final_g, m_final_g, v_final_g)),
        conv_w=(t_conv(conv_w), t_conv(m_conv_w), t_conv(v_conv_w)),
        w_ple_proj=(w_ple_proj, m_w_ple_proj, v_w_ple_proj),
    ))
    res = dict(small, w_in=tuple(t_in(a) for a in r_in), w_out=r_out, w_ple_gate=r_gate)
    res["final_g"] = tuple(a.reshape(D_MODEL) for a in res["final_g"])
    res["conv_w"] = tuple(jnp.transpose(a, (1, 2, 0)) for a in res["conv_w"])
    order = ["norm_g", "w_in", "ln_v_g", "ln_v_b", "w_s", "b_s", "conv_w", "w_out", "ple_norm_g", "w_ple_gate", "w_ple_proj", "final_g"]
    return (loss, grad_x, *[res[n][0] for n in order], *[res[n][1] for n in order],
            *[res[n][2] for n in order], *[res[n][3] for n in order])
```

```python
import jax
import jax.numpy as jnp
from jax import lax
from jax.experimental import pallas as pl
from jax.experimental.pallas import tpu as pltpu

F32 = jnp.float32
BF16 = jnp.bfloat16

D_MODEL = 1024
WIDTH_A = 512
WIDTH_B = 512
HEADS_A = 4
HEAD_DIM = 128
CHUNK = 128
PLE_DIM = 256
PROJ_WIDTH = 3584
DEPTH = 2
EPS = 1e-6
N_DEV = 8

ADAM_LR = 0.001
ADAM_B1 = 0.9
ADAM_B2 = 0.999
ADAM_EPS = 1e-08
ADAM_WD = 0.01
ADAM_STEP = 10

ROWS_IN = PROJ_WIDTH // N_DEV
ROWS_OUT = D_MODEL // N_DEV
ROWS_GATE = D_MODEL // N_DEV
ROWS_PROJ = (D_MODEL // N_DEV) * PLE_DIM // D_MODEL
ROWS_CONV = 16
OFF_IN = 0
OFF_OUT = OFF_IN + ROWS_IN
OFF_GATE = OFF_OUT + ROWS_OUT
OFF_PROJ = OFF_GATE + ROWS_GATE
OFF_CONV = OFF_PROJ + ROWS_PROJ
ROWS_GRAD = OFF_CONV
ROWS_LAYER = OFF_CONV + ROWS_CONV

SMALL_ROWS = 8
SMALL_NORM = 0
SMALL_PLE = 1
SMALL_LN = 2
SMALL_BS = 3
SMALL_CONV = 4
HEAD_FINAL = 0
HEAD_LOSS = 1
TOTAL_HEAD = DEPTH * SMALL_ROWS
TOTAL_WS = TOTAL_HEAD + SMALL_ROWS
TOTAL_ROWS = TOTAL_WS + CHUNK

MIB = 1024 * 1024
MESH = pl.DeviceIdType.MESH

NT_DIMS = (((1,), (1,)), ((), ()))
TN_DIMS = (((0,), (0,)), ((), ()))


def _dot(a, b):
    return jnp.dot(a, b, preferred_element_type=F32)


def _dot_nt(a, b):
    return lax.dot_general(a, b, NT_DIMS, preferred_element_type=F32)


def _dot_tn(a, b):
    return lax.dot_general(a, b, TN_DIMS, preferred_element_type=F32)


def _colsum8(a):
    rows, n = a.shape
    return jnp.sum(a.reshape(rows // 8, 8, n), axis=0)


def _sigmoid(z):
    return 1.0 / (1.0 + jnp.exp(-z))


def _tile(t, want):
    return want if t % want == 0 else t


class _TwoLevelGather:
    def __init__(self, x_ref, out_ref, m_per, send_sems, recv_sems, local_sem):
        x, y, c = lax.axis_index("x"), lax.axis_index("y"), lax.axis_index("c")
        self.c = c
        self.me, self.sibling = (x, y, c), (x, y, 1 - c)
        self.chips = [(1 - x, y), (x, 1 - y), (1 - x, 1 - y)]
        self.x_ref, self.out_ref, self.m_per = x_ref, out_ref, m_per
        self.send_sems, self.recv_sems = send_sems, recv_sems
        self.mine = pltpu.make_async_copy(x_ref, self.rows(*self.me), local_sem)

    def rows(self, px, py, pc):
        return self.out_ref.at[pl.ds((4 * px + 2 * py + pc) * self.m_per, self.m_per), :]

    def copy(self, k, block, to, src=None):
        return pltpu.make_async_remote_copy(
            src_ref=self.rows(*block) if src is None else src,
            dst_ref=self.rows(*block),
            send_sem=self.send_sems.at[k],
            recv_sem=self.recv_sems.at[k],
            device_id=to,
            device_id_type=MESH,
        )

    def first(self):
        out = [self.copy(0, self.me, self.sibling, src=self.x_ref)]
        return out + [self.copy(1 + j, self.me, (*chip, self.c), src=self.x_ref) for j, chip in enumerate(self.chips)]

    def passed(self):
        return [self.copy(4 + j, (*chip, self.c), self.sibling) for j, chip in enumerate(self.chips)]

    def start(self):
        self.mine.start()
        for cp in self.first():
            cp.start()

    def pass_on(self):
        passed = self.passed()
        for j, chip in enumerate(self.chips):
            self.copy(1 + j, (*chip, self.c), self.me).wait_recv()
            passed[j].start()

    def finish(self):
        self.copy(0, self.sibling, self.me).wait_recv()
        for j, chip in enumerate(self.chips):
            self.copy(4 + j, (*chip, 1 - self.c), self.me).wait_recv()
        for cp in self.first() + self.passed():
            cp.wait_send()
        self.mine.wait()


GATHER_SEMS = [pltpu.SemaphoreType.DMA((7,)), pltpu.SemaphoreType.DMA((7,)), pltpu.SemaphoreType.DMA]


def _all_gather_rows(shard):
    m_per, n = shard.shape

    def body(x_ref, out_ref, send_sems, recv_sems, local_sem):
        ag = _TwoLevelGather(x_ref, out_ref, m_per, send_sems, recv_sems, local_sem)
        ag.start()
        ag.pass_on()
        ag.finish()

    return pl.pallas_call(
        body,
        name="weights_all_gather",
        out_shape=pltpu.HBM((N_DEV * m_per, n), shard.dtype),
        in_specs=[pl.BlockSpec(memory_space=pltpu.HBM)],
        out_specs=pl.BlockSpec(memory_space=pltpu.HBM),
        scratch_shapes=list(GATHER_SEMS),
    )(pltpu.with_memory_space_constraint(shard, pltpu.HBM))


PROJ_PARTS = D_MODEL // PLE_DIM
N_WEIGHT_COPIES = N_DEV * (3 + PROJ_PARTS)


def _weight_copies(wg_ref, w_in_t, w_out, w_gate, w_proj_t, sems):
    copies = []
    for s in range(N_DEV):
        base = s * ROWS_LAYER
        for dst, off, rows in ((w_in_t, OFF_IN, ROWS_IN), (w_out, OFF_OUT, ROWS_OUT), (w_gate, OFF_GATE, ROWS_GATE)):
            copies.append((wg_ref.at[pl.ds(base + off, rows), :], dst.at[pl.ds(s * rows, rows), :]))
        for j in range(PROJ_PARTS):
            copies.append((
                wg_ref.at[pl.ds(base + OFF_PROJ, ROWS_PROJ), pl.ds(j * PLE_DIM, PLE_DIM)],
                w_proj_t.at[pl.ds(s * ROWS_OUT + j * ROWS_PROJ, ROWS_PROJ), :],
            ))
    return [pltpu.make_async_copy(src, dst, sems.at[k]) for k, (src, dst) in enumerate(copies)]


def _forward_layer(layer, x, p_all, wg, conv_k, norm_g, ln_g, ln_b, w_mix, b_mix, ple_g, next_shard=None):
    t = x.shape[0]
    tm = _tile(t, 512)
    nt = t // tm
    gathers = next_shard is not None

    def body(*refs):
        (x_ref, p_ref, wg_ref, cw_ref, ng_ref, lng_ref, lnb_ref, wm_ref, bm_ref, pg_ref) = refs[:10]
        refs = refs[10:]
        if gathers:
            shard_ref, refs = refs[0], refs[1:]
        (proj_ref, hn_ref, cat_ref, r_ref, gpre_ref, x1_ref, x2_ref) = refs[:7]
        refs = refs[7:]
        if gathers:
            gathered_ref, refs = refs[0], refs[1:]
        (w_in_t, w_out, w_gate, wpt_ref, vln_s, mixed_s, halo_s, sems) = refs[:8]
        i = pl.program_id(0)
        if gathers:
            ag = _TwoLevelGather(shard_ref, gathered_ref, ROWS_LAYER, *refs[8:11])

            @pl.when(i == 0)
            def _():
                ag.start()

            @pl.when(i == nt // 2)
            def _():
                ag.pass_on()

        @pl.when(i == 0)
        def _():
            copies = _weight_copies(wg_ref, w_in_t, w_out, w_gate, wpt_ref, sems)
            for cp in copies:
                cp.start()
            halo_s[...] = jnp.zeros_like(halo_s)
            for cp in copies:
                cp.wait()

        xv = x_ref[...]
        rstd0 = lax.rsqrt(jnp.mean(xv * xv, axis=-1, keepdims=True) + EPS)
        hn_ref[...] = (xv * rstd0 * ng_ref[...]).astype(BF16)

        def proj_section(k):
            sec = _dot_nt(hn_ref[...], w_in_t[pl.ds(k * 512, 512), :])
            proj_ref[:, k * 512:(k + 1) * 512] = sec.astype(BF16)
            return sec

        v = proj_section(1)
        mu = jnp.mean(v, axis=-1, keepdims=True)
        vc = v - mu
        var = jnp.mean(vc * vc, axis=-1, keepdims=True)
        vln = vc * lax.rsqrt(var + EPS) * lng_ref[...] + lnb_ref[...]
        vln_s[...] = vln.astype(BF16)
        for ci in range(tm // CHUNK):
            rows = pl.ds(ci * CHUNK, CHUNK)
            for h in range(HEADS_A):
                cols = pl.ds(h * HEAD_DIM, HEAD_DIM)
                mixed_s[rows, cols] = _dot(wm_ref[h], vln_s[rows, cols]) + bm_ref[h]
        u = proj_section(0)
        za = proj_section(2)
        out_a = u * mixed_s[...] * (za * _sigmoid(za))
        cat_ref[:, 0:512] = out_a.astype(BF16)

        xc = proj_section(5) * proj_section(3)
        prev = halo_s[...]
        row = lax.broadcasted_iota(jnp.int32, (tm, WIDTH_B), 0)
        xc_m1 = jnp.where(row == 0, prev[7:8, :], pltpu.roll(xc, 1, 0))
        xc_m2 = jnp.where(row == 0, prev[6:7, :], jnp.where(row == 1, prev[7:8, :], pltpu.roll(xc, 2, 0)))
        halo_s[...] = xc[tm - 8:tm, :]
        cw = cw_ref[...]
        yc = cw[0:1, :] * xc_m2 + cw[1:2, :] * xc_m1 + cw[2:3, :] * xc
        zb = proj_section(6)
        out_b = proj_section(4) * yc * (zb * _sigmoid(zb))
        cat_ref[:, 512:1024] = out_b.astype(BF16)

        x1 = xv + _dot(cat_ref[...], w_out[...])
        x1_ref[...] = x1
        rstd1 = lax.rsqrt(jnp.mean(x1 * x1, axis=-1, keepdims=True) + EPS)
        r_ref[...] = (x1 * rstd1 * pg_ref[...]).astype(BF16)
        gpre = _dot(r_ref[...], w_gate[...])
        gpre_ref[...] = gpre.astype(BF16)
        pp = _dot_nt(p_ref[...].astype(BF16), wpt_ref[...])
        x2_ref[...] = x1 + _sigmoid(gpre) * pp

        if gathers:
            @pl.when(i == nt - 1)
            def _():
                ag.finish()

    def tok(width):
        return pl.BlockSpec((tm, width), lambda i: (i, 0))

    def whole(shape):
        return pl.BlockSpec(shape, lambda i: (0,) * len(shape))

    hbm = pl.BlockSpec(memory_space=pl.ANY)
    operands = [x, p_all, wg, conv_k, norm_g, ln_g, ln_b, w_mix, b_mix, ple_g]
    in_specs = [
        tok(D_MODEL), pl.BlockSpec((None, None, tm, PLE_DIM), lambda i: (layer, 0, i, 0)), hbm,
        whole((8, WIDTH_B)), whole((1, D_MODEL)), whole((1, WIDTH_A)), whole((1, WIDTH_A)),
        whole((HEADS_A, CHUNK, CHUNK)), whole((HEADS_A, CHUNK, HEAD_DIM)), whole((1, D_MODEL)),
    ]
    out_specs = [tok(PROJ_WIDTH), tok(D_MODEL), tok(D_MODEL), tok(D_MODEL), tok(D_MODEL), tok(D_MODEL), tok(D_MODEL)]
    out_shape = [
        jax.ShapeDtypeStruct((t, PROJ_WIDTH), BF16),
        jax.ShapeDtypeStruct((t, D_MODEL), BF16),
        jax.ShapeDtypeStruct((t, D_MODEL), BF16),
        jax.ShapeDtypeStruct((t, D_MODEL), BF16),
        jax.ShapeDtypeStruct((t, D_MODEL), BF16),
        jax.ShapeDtypeStruct((t, D_MODEL), F32),
        jax.ShapeDtypeStruct((t, D_MODEL), F32),
    ]
    scratch_shapes = [
        pltpu.VMEM((PROJ_WIDTH, D_MODEL), BF16),
        pltpu.VMEM((D_MODEL, D_MODEL), BF16),
        pltpu.VMEM((D_MODEL, D_MODEL), BF16),
        pltpu.VMEM((D_MODEL, PLE_DIM), BF16),
        pltpu.VMEM((tm, WIDTH_A), BF16),
        pltpu.VMEM((tm, WIDTH_A), F32),
        pltpu.VMEM((8, WIDTH_B), F32),
        pltpu.SemaphoreType.DMA((N_WEIGHT_COPIES,)),
    ]
    if gathers:
        operands.append(pltpu.with_memory_space_constraint(next_shard, pltpu.HBM))
        in_specs.append(pl.BlockSpec(memory_space=pltpu.HBM))
        out_specs.append(pl.BlockSpec(memory_space=pltpu.HBM))
        out_shape.append(pltpu.HBM((N_DEV * ROWS_LAYER, D_MODEL), BF16))
        scratch_shapes += list(GATHER_SEMS)

    return pl.pallas_call(
        body,
        name=f"layer{layer}_forward",
        grid=(nt,),
        in_specs=in_specs,
        out_specs=out_specs,
        out_shape=out_shape,
        scratch_shapes=scratch_shapes,
        compiler_params=pltpu.CompilerParams(dimension_semantics=("arbitrary",), vmem_limit_bytes=56 * MIB),
    )(*operands)


class _DirectScatter:
    def __init__(self, pack_ref, pieces_ref, send_sems, recv_sems, local_sem):
        x, y, c = lax.axis_index("x"), lax.axis_index("y"), lax.axis_index("c")
        me = 4 * x + 2 * y + c
        self.copies = []
        for k in range(N_DEV - 1):
            fx, fy, fc = ((k + 1) >> 2) & 1, ((k + 1) >> 1) & 1, (k + 1) & 1
            tx, ty, tc = x ^ fx, y ^ fy, c ^ fc
            self.copies.append(
                pltpu.make_async_remote_copy(
                    src_ref=pack_ref.at[4 * tx + 2 * ty + tc], dst_ref=pieces_ref.at[me],
                    send_sem=send_sems.at[k], recv_sem=recv_sems.at[k],
                    device_id=(tx, ty, tc), device_id_type=MESH,
                )
            )
        self.mine = pltpu.make_async_copy(pack_ref.at[me], pieces_ref.at[me], local_sem)

    def start(self):
        self.mine.start()
        for cp in self.copies:
            cp.start()

    def finish(self):
        for cp in self.copies:
            cp.wait_recv()
        for cp in self.copies:
            cp.wait_send()
        self.mine.wait()


SCATTER_SEMS = [pltpu.SemaphoreType.DMA((N_DEV - 1,)), pltpu.SemaphoreType.DMA((N_DEV - 1,)), pltpu.SemaphoreType.DMA]


def _backward_layer(layer, dx2, x_in, x1, proj, gpre, p_all, wg, conv_k, norm_g, ln_g, ln_b,
                    w_mix, w_mix_t, b_mix, ple_g, loss_head=None):
    t = x_in.shape[0]
    tm = _tile(t, 256)
    nt = t // tm
    n_chunks = tm // CHUNK
    halo_rows = 16
    heads = loss_head is not None

    def body(*refs):
        (dx2_ref, xin_ref, x1_ref, proj_ref, halo_ref, gpre_ref, p_ref, wg_ref, cw_ref,
         ng_ref, lng_ref, lnb_ref, wm_ref, wmt_ref, bm_ref, pg_ref) = refs[:16]
        refs = refs[16:]
        if heads:
            tgt_ref, fg_ref = refs[:2]
            refs = refs[2:]
        (dxin_ref, dproj_ref, dx1_ref, dgpre_ref, dpp_ref, small_ref, dws_ref) = refs[:7]
        refs = refs[7:]
        if heads:
            head_ref, refs = refs[0], refs[1:]
        (w_in_t, w_out, w_gate, wpt_ref, vln_s, mixed_s, dmix_s, dvln_s, carry_s,
         ng_acc, pg_acc, lng_acc, lnb_acc, cw_acc, dbm_ref, sems) = refs[:16]
        if heads:
            loss_acc, fg_acc = refs[16:18]
        i = pl.program_id(0)
        tile = nt - 1 - i

        @pl.when(i == 0)
        def _():
            copies = _weight_copies(wg_ref, w_in_t, w_out, w_gate, wpt_ref, sems)
            for cp in copies:
                cp.start()
            if heads:
                loss_acc[...] = jnp.zeros_like(loss_acc)
                fg_acc[...] = jnp.zeros_like(fg_acc)
            carry_s[...] = jnp.zeros_like(carry_s)
            ng_acc[...] = jnp.zeros_like(ng_acc)
            pg_acc[...] = jnp.zeros_like(pg_acc)
            lng_acc[...] = jnp.zeros_like(lng_acc)
            lnb_acc[...] = jnp.zeros_like(lnb_acc)
            cw_acc[...] = jnp.zeros_like(cw_acc)
            dws_ref[...] = jnp.zeros_like(dws_ref)
            dbm_ref[...] = jnp.zeros_like(dbm_ref)
            for cp in copies:
                cp.wait()

        if heads:
            x2v = dx2_ref[...]
            fg = fg_ref[...]
            rstdf = lax.rsqrt(jnp.mean(x2v * x2v, axis=-1, keepdims=True) + EPS)
            xhatf = x2v * rstdf
            err = xhatf * fg - tgt_ref[...]
            loss_acc[...] += _colsum8(err * err)
            dy = err * (1.0 / D_MODEL)
            fg_acc[...] += _colsum8(dy * xhatf)
            dxhf = dy * fg
            dx2v = rstdf * (dxhf - xhatf * jnp.mean(dxhf * xhatf, axis=-1, keepdims=True))
        else:
            dx2v = dx2_ref[...]

        gate = _sigmoid(gpre_ref[...].astype(F32))
        pp = _dot_nt(p_ref[...].astype(BF16), wpt_ref[...])
        dpp_ref[...] = (dx2v * gate).astype(BF16)
        dgpre = (dx2v * pp * gate * (1.0 - gate)).astype(BF16)
        dgpre_ref[...] = dgpre
        dr = _dot_nt(dgpre, w_gate[...])
        x1v = x1_ref[...]
        rstd1 = lax.rsqrt(jnp.mean(x1v * x1v, axis=-1, keepdims=True) + EPS)
        xhat1 = x1v * rstd1
        pg_acc[...] += _colsum8(dr * xhat1)
        dxh = dr * pg_ref[...]
        dx1 = dx2v + rstd1 * (dxh - xhat1 * jnp.mean(dxh * xhat1, axis=-1, keepdims=True))
        dx1b = dx1.astype(BF16)
        dx1_ref[...] = dx1b

        dcat = _dot_nt(dx1b, w_out[...])
        dca = dcat[:, 0:512]
        dcb = dcat[:, 512:1024]

        u = proj_ref[:, 0:512].astype(F32)
        v = proj_ref[:, 512:1024].astype(F32)
        za = proj_ref[:, 1024:1536].astype(F32)
        mu = jnp.mean(v, axis=-1, keepdims=True)
        vc = v - mu
        var = jnp.mean(vc * vc, axis=-1, keepdims=True)
        rs = lax.rsqrt(var + EPS)
        vhat = vc * rs
        lng = lng_ref[...]
        vln_s[...] = (vhat * lng + lnb_ref[...]).astype(BF16)
        for ci in range(n_chunks):
            rows = pl.ds(ci * CHUNK, CHUNK)
            for h in range(HEADS_A):
                cols = pl.ds(h * HEAD_DIM, HEAD_DIM)
                mixed_s[rows, cols] = _dot(wm_ref[h], vln_s[rows, cols]) + bm_ref[h]
        mixed = mixed_s[...]
        sga = _sigmoid(za)
        sa = za * sga
        dsa = sga * (1.0 + za * (1.0 - sga))
        def put_section(k, val):
            dproj_ref[:, k * 512:(k + 1) * 512] = val.astype(BF16)

        put_section(0, dca * mixed * sa)
        dmix = dca * u * sa
        put_section(2, dca * u * mixed * dsa)
        dmix_s[...] = dmix.astype(BF16)
        dbm_acc = jnp.zeros((CHUNK, WIDTH_A), F32)
        for ci in range(n_chunks):
            rows = pl.ds(ci * CHUNK, CHUNK)
            dbm_acc = dbm_acc + dmix[ci * CHUNK:(ci + 1) * CHUNK, :]
            for h in range(HEADS_A):
                cols = pl.ds(h * HEAD_DIM, HEAD_DIM)
                dvln_s[rows, cols] = _dot(wmt_ref[h], dmix_s[rows, cols])
                dws_ref[:, cols] += _dot_nt(dmix_s[rows, cols], vln_s[rows, cols])
        dbm_ref[...] += dbm_acc
        dvln = dvln_s[...]
        lng_acc[...] += _colsum8(dvln * vhat)
        lnb_acc[...] += _colsum8(dvln)
        dvh = dvln * lng
        dv = rs * (dvh - jnp.mean(dvh, axis=-1, keepdims=True) - vhat * jnp.mean(dvh * vhat, axis=-1, keepdims=True))
        put_section(1, dv)

        hb = proj_ref[:, 1536:2048].astype(F32)
        gb = proj_ref[:, 2048:2560].astype(F32)
        gc = proj_ref[:, 2560:3072].astype(F32)
        zb = proj_ref[:, 3072:3584].astype(F32)
        xc = gc * hb
        prev = halo_ref[:, 2560:3072].astype(F32) * halo_ref[:, 1536:2048].astype(F32)
        prev = jnp.where(tile > 0, prev, 0.0)
        row = lax.broadcasted_iota(jnp.int32, (tm, WIDTH_B), 0)
        p1 = prev[halo_rows - 1:halo_rows, :]
        p2 = prev[halo_rows - 2:halo_rows - 1, :]
        xc_m1 = jnp.where(row == 0, p1, pltpu.roll(xc, 1, 0))
        xc_m2 = jnp.where(row == 0, p2, jnp.where(row == 1, p1, pltpu.roll(xc, 2, 0)))
        cw = cw_ref[...]
        yc = cw[0:1, :] * xc_m2 + cw[1:2, :] * xc_m1 + cw[2:3, :] * xc
        sgb = _sigmoid(zb)
        sb = zb * sgb
        dsb = sgb * (1.0 + zb * (1.0 - sgb))
        put_section(4, dcb * yc * sb)
        dyc = dcb * gb * sb
        put_section(6, dcb * gb * yc * dsb)
        nxt = carry_s[...]
        dyc_p1 = jnp.where(row == tm - 1, nxt[0:1, :], pltpu.roll(dyc, tm - 1, 0))
        dyc_p2 = jnp.where(row == tm - 1, nxt[1:2, :], jnp.where(row == tm - 2, nxt[0:1, :], pltpu.roll(dyc, tm - 2, 0)))
        carry_s[...] = dyc[0:8, :]
        dxc = cw[2:3, :] * dyc + cw[1:2, :] * dyc_p1 + cw[0:1, :] * dyc_p2
        cw_acc[0] += _colsum8(dyc * xc_m2)
        cw_acc[1] += _colsum8(dyc * xc_m1)
        cw_acc[2] += _colsum8(dyc * xc)
        put_section(3, dxc * gc)
        put_section(5, dxc * hb)

        dhn = _dot(dproj_ref[...], w_in_t[...])
        xv = xin_ref[...]
        rstd0 = lax.rsqrt(jnp.mean(xv * xv, axis=-1, keepdims=True) + EPS)
        xhat0 = xv * rstd0
        ng_acc[...] += _colsum8(dhn * xhat0)
        dxh0 = dhn * ng_ref[...]
        dxin_ref[...] = dx1 + rstd0 * (dxh0 - xhat0 * jnp.mean(dxh0 * xhat0, axis=-1, keepdims=True))

        @pl.when(i == nt - 1)
        def _():
            small_ref[...] = jnp.zeros_like(small_ref)
            small_ref[SMALL_NORM:SMALL_NORM + 1, :] = jnp.sum(ng_acc[...], axis=0, keepdims=True)
            small_ref[SMALL_PLE:SMALL_PLE + 1, :] = jnp.sum(pg_acc[...], axis=0, keepdims=True)
            small_ref[SMALL_LN:SMALL_LN + 1, 0:WIDTH_A] = jnp.sum(lng_acc[...], axis=0, keepdims=True)
            small_ref[SMALL_LN:SMALL_LN + 1, WIDTH_A:2 * WIDTH_A] = jnp.sum(lnb_acc[...], axis=0, keepdims=True)
            for h in range(HEADS_A):
                cols = pl.ds(h * HEAD_DIM, HEAD_DIM)
                small_ref[SMALL_BS:SMALL_BS + 1, cols] = jnp.sum(jnp.transpose(dbm_ref[:, cols]), axis=0, keepdims=True)
            for k in range(3):
                small_ref[SMALL_CONV + k:SMALL_CONV + k + 1, 0:WIDTH_B] = jnp.sum(cw_acc[k], axis=0, keepdims=True)
            if heads:
                total = jnp.sum(loss_acc[...]) * (0.5 / D_MODEL)
                rows8 = lax.broadcasted_iota(jnp.int32, (SMALL_ROWS, D_MODEL), 0)
                lanes8 = lax.broadcasted_iota(jnp.int32, (SMALL_ROWS, D_MODEL), 1)
                head_ref[...] = jnp.where((rows8 == HEAD_LOSS) & (lanes8 == 0), total, 0.0)
                head_ref[HEAD_FINAL:HEAD_FINAL + 1, :] = jnp.sum(fg_acc[...], axis=0, keepdims=True)

    def tok(width):
        return pl.BlockSpec((tm, width), lambda i: (nt - 1 - i, 0))

    def whole(shape):
        return pl.BlockSpec(shape, lambda i: (0,) * len(shape))

    halo_spec = pl.BlockSpec(
        (halo_rows, PROJ_WIDTH), lambda i: (jnp.maximum((nt - 1 - i) * (tm // halo_rows) - 1, 0), 0)
    )
    hbm = pl.BlockSpec(memory_space=pl.ANY)
    operands = [dx2, x_in, x1, proj, proj, gpre, p_all, wg, conv_k, norm_g, ln_g, ln_b, w_mix, w_mix_t, b_mix, ple_g]
    in_specs = [
        tok(D_MODEL), tok(D_MODEL), tok(D_MODEL), tok(PROJ_WIDTH), halo_spec, tok(D_MODEL),
        pl.BlockSpec((None, None, tm, PLE_DIM), lambda i: (layer, 0, nt - 1 - i, 0)), hbm,
        whole((8, WIDTH_B)), whole((1, D_MODEL)), whole((1, WIDTH_A)), whole((1, WIDTH_A)),
        whole((HEADS_A, CHUNK, CHUNK)), whole((HEADS_A, CHUNK, CHUNK)), whole((HEADS_A, CHUNK, HEAD_DIM)),
        whole((1, D_MODEL)),
    ]
    out_specs = [
        tok(D_MODEL), tok(PROJ_WIDTH), tok(D_MODEL), tok(D_MODEL), tok(D_MODEL),
        whole((SMALL_ROWS, D_MODEL)), whole((CHUNK, WIDTH_A)),
    ]
    out_shape = [
        jax.ShapeDtypeStruct((t, D_MODEL), F32),
        jax.ShapeDtypeStruct((t, PROJ_WIDTH), BF16),
        jax.ShapeDtypeStruct((t, D_MODEL), BF16),
        jax.ShapeDtypeStruct((t, D_MODEL), BF16),
        jax.ShapeDtypeStruct((t, D_MODEL), BF16),
        jax.ShapeDtypeStruct((SMALL_ROWS, D_MODEL), F32),
        jax.ShapeDtypeStruct((CHUNK, WIDTH_A), F32),
    ]
    scratch_shapes = [
        pltpu.VMEM((PROJ_WIDTH, D_MODEL), BF16),
        pltpu.VMEM((D_MODEL, D_MODEL), BF16),
        pltpu.VMEM((D_MODEL, D_MODEL), BF16),
        pltpu.VMEM((D_MODEL, PLE_DIM), BF16),
        pltpu.VMEM((tm, WIDTH_A), BF16),
        pltpu.VMEM((tm, WIDTH_A), F32),
        pltpu.VMEM((tm, WIDTH_A), BF16),
        pltpu.VMEM((tm, WIDTH_A), F32),
        pltpu.VMEM((8, WIDTH_B), F32),
        pltpu.VMEM((8, D_MODEL), F32),
        pltpu.VMEM((8, D_MODEL), F32),
        pltpu.VMEM((8, WIDTH_A), F32),
        pltpu.VMEM((8, WIDTH_A), F32),
        pltpu.VMEM((3, 8, WIDTH_B), F32),
        pltpu.VMEM((CHUNK, WIDTH_A), F32),
        pltpu.SemaphoreType.DMA((N_WEIGHT_COPIES,)),
    ]
    if heads:
        operands += list(loss_head)
        in_specs += [tok(D_MODEL), whole((1, D_MODEL))]
        out_specs.append(whole((SMALL_ROWS, D_MODEL)))
        out_shape.append(jax.ShapeDtypeStruct((SMALL_ROWS, D_MODEL), F32))
        scratch_shapes += [pltpu.VMEM((8, D_MODEL), F32), pltpu.VMEM((8, D_MODEL), F32)]

    return pl.pallas_call(
        body,
        name=f"layer{layer}_backward",
        grid=(nt,),
        in_specs=in_specs,
        out_specs=out_specs,
        out_shape=out_shape,
        scratch_shapes=scratch_shapes,
        compiler_params=pltpu.CompilerParams(dimension_semantics=("arbitrary",), vmem_limit_bytes=56 * MIB),
    )(*operands)


def _sum_pieces(layer, pieces):
    rows, n = pieces.shape[1], pieces.shape[2]
    blocks = 2
    rb = rows // blocks

    def body(p_ref, out_ref):
        total = p_ref[0].astype(F32)
        for j in range(1, N_DEV):
            total = total + p_ref[j].astype(F32)
        out_ref[...] = total

    return pl.pallas_call(
        body,
        name=f"layer{layer}_grad_sum",
        grid=(blocks,),
        out_shape=pltpu.HBM((rows, n), F32),
        in_specs=[pl.BlockSpec((N_DEV, rb, n), lambda i: (0, i, 0))],
        out_specs=pl.BlockSpec((rb, n), lambda i: (i, 0)),
        compiler_params=pltpu.CompilerParams(dimension_semantics=("arbitrary",), vmem_limit_bytes=32 * MIB),
    )(pieces)


def _weight_grads(layer, dproj, hn, cat, dx1, r, dgpre, dpp, p_all, scatter_pack=None):
    t = hn.shape[0]
    tk = _tile(t, 512)
    nt = t // tk
    in_blocks = PROJ_WIDTH // 512
    scatters = scatter_pack is not None

    def body(*refs):
        (dproj_ref, hn_ref, cat_ref, dx1_ref, r_ref, dgpre_ref, dpp_ref, p_ref) = refs[:8]
        refs = refs[8:]
        if scatters:
            prior_ref, refs = refs[0], refs[1:]
        pack_ref, refs = refs[0], refs[1:]
        if scatters:
            pieces_ref, refs = refs[0], refs[1:]
        (acc_in, acc_out, acc_gate, acc_proj, stage, sems) = refs[:6]
        i = pl.program_id(0)
        if scatters:
            scatter = _DirectScatter(prior_ref, pieces_ref, *refs[6:9])

            @pl.when(i == 0)
            def _():
                scatter.start()

        @pl.when(i == 0)
        def _():
            acc_in[...] = jnp.zeros_like(acc_in)
            acc_out[...] = jnp.zeros_like(acc_out)
            acc_gate[...] = jnp.zeros_like(acc_gate)
            acc_proj[...] = jnp.zeros_like(acc_proj)

        hnv = hn_ref[...]
        for b in range(in_blocks):
            acc_in[pl.ds(b * 512, 512), :] += _dot_tn(dproj_ref[:, b * 512:(b + 1) * 512], hnv)
        dx1v = dx1_ref[...]
        dgv = dgpre_ref[...]
        for b in range(D_MODEL // 512):
            acc_out[pl.ds(b * 512, 512), :] += _dot_tn(cat_ref[:, b * 512:(b + 1) * 512], dx1v)
            acc_gate[pl.ds(b * 512, 512), :] += _dot_tn(r_ref[:, b * 512:(b + 1) * 512], dgv)
        pv = p_ref[...].astype(BF16)
        for b in range(D_MODEL // 512):
            acc_proj[pl.ds(b * 512, 512), :] += _dot_tn(dpp_ref[:, b * 512:(b + 1) * 512], pv)

        @pl.when(i == nt - 1)
        def _():
            def out_copy(s):
                return pltpu.make_async_copy(stage.at[s % 2], pack_ref.at[s], sems.at[s % 2])

            for s in range(N_DEV):
                if s >= 2:
                    out_copy(s - 2).wait()
                buf = stage.at[s % 2]
                buf[pl.ds(OFF_IN, ROWS_IN), :] = acc_in[pl.ds(s * ROWS_IN, ROWS_IN), :].astype(BF16)
                buf[pl.ds(OFF_OUT, ROWS_OUT), :] = acc_out[pl.ds(s * ROWS_OUT, ROWS_OUT), :].astype(BF16)
                buf[pl.ds(OFF_GATE, ROWS_GATE), :] = acc_gate[pl.ds(s * ROWS_GATE, ROWS_GATE), :].astype(BF16)
                for j in range(D_MODEL // PLE_DIM):
                    buf[pl.ds(OFF_PROJ, ROWS_PROJ), pl.ds(j * PLE_DIM, PLE_DIM)] = acc_proj[
                        pl.ds(s * ROWS_OUT + j * ROWS_PROJ, ROWS_PROJ), :
                    ].astype(BF16)
                out_copy(s).start()
            out_copy(N_DEV - 2).wait()
            out_copy(N_DEV - 1).wait()
            if scatters:
                scatter.finish()

    def tok(width):
        return pl.BlockSpec((tk, width), lambda i: (i, 0))

    hbm = pl.BlockSpec(memory_space=pl.ANY)
    pack_shape = jax.ShapeDtypeStruct((N_DEV, ROWS_GRAD, D_MODEL), BF16)
    operands = [dproj, hn, cat, dx1, r, dgpre, dpp, p_all]
    in_specs = [tok(PROJ_WIDTH), tok(D_MODEL), tok(D_MODEL), tok(D_MODEL), tok(D_MODEL), tok(D_MODEL), tok(D_MODEL),
                pl.BlockSpec((None, None, tk, PLE_DIM), lambda i: (layer, 0, i, 0))]
    out_specs, out_shape = [hbm], [pack_shape]
    scratch_shapes = [
        pltpu.VMEM((PROJ_WIDTH, D_MODEL), F32),
        pltpu.VMEM((D_MODEL, D_MODEL), F32),
        pltpu.VMEM((D_MODEL, D_MODEL), F32),
        pltpu.VMEM((D_MODEL, PLE_DIM), F32),
        pltpu.VMEM((2, ROWS_GRAD, D_MODEL), BF16),
        pltpu.SemaphoreType.DMA((2,)),
    ]
    if scatters:
        operands.append(scatter_pack)
        in_specs.append(hbm)
        out_specs.append(hbm)
        out_shape.append(pack_shape)
        scratch_shapes += list(SCATTER_SEMS)

    return pl.pallas_call(
        body,
        name=f"layer{layer}_weight_grads",
        grid=(nt,),
        in_specs=in_specs,
        out_specs=out_specs,
        out_shape=out_shape,
        scratch_shapes=scratch_shapes,
        compiler_params=pltpu.CompilerParams(dimension_semantics=("arbitrary",), vmem_limit_bytes=58 * MIB),
    )(*operands)


def _reduce_scatter_all_reduce(layer, pack, smalls, head, dws):
    rows, n = pack.shape[1], pack.shape[2]
    assert DEPTH * WIDTH_A == D_MODEL and n == D_MODEL

    def body(g_ref, *refs):
        small_refs, refs = refs[:DEPTH], refs[DEPTH:]
        head_ref, refs = refs[0], refs[1:]
        dws_refs, refs = refs[:DEPTH], refs[DEPTH:]
        (out_ref, total_ref, r1, a_s, r2, sp, sr1, sq, send1, recv1, send2, recv2, ssend, srecv) = refs
        x, y, c = lax.axis_index("x"), lax.axis_index("y"), lax.axis_index("c")
        sibling = (x, y, 1 - c)
        chip = 2 * x + y
        flips = [(1, 0), (0, 1), (1, 1)]

        for l in range(DEPTH):
            sp[l * SMALL_ROWS:(l + 1) * SMALL_ROWS, :] = small_refs[l][...]
            sp[TOTAL_WS:TOTAL_ROWS, l * WIDTH_A:(l + 1) * WIDTH_A] = dws_refs[l][...]
        sp[TOTAL_HEAD:TOTAL_WS, :] = head_ref[...]

        small_pair = pltpu.make_async_remote_copy(
            src_ref=sp, dst_ref=sr1, send_sem=ssend.at[0], recv_sem=srecv.at[0], device_id=sibling, device_id_type=MESH
        )

        def to_sibling(j):
            return pltpu.make_async_remote_copy(
                src_ref=g_ref.at[2 * j + 1 - c], dst_ref=r1.at[j], send_sem=send1.at[j], recv_sem=recv1.at[j],
                device_id=sibling, device_id_type=MESH,
            )

        first = [to_sibling(j) for j in range(4)]
        small_pair.start()
        for cp in first:
            cp.start()

        small_pair.wait_recv()
        sq[chip] = sp[...] + sr1[...]
        small_chips = [
            pltpu.make_async_remote_copy(
                src_ref=sq.at[chip], dst_ref=sq.at[chip], send_sem=ssend.at[1 + k], recv_sem=srecv.at[1 + k],
                device_id=(x ^ fx, y ^ fy, c), device_id_type=MESH,
            )
            for k, (fx, fy) in enumerate(flips)
        ]
        for cp in small_chips:
            cp.start()

        def to_chip(j):
            return pltpu.make_async_remote_copy(
                src_ref=a_s.at[j], dst_ref=r2.at[chip], send_sem=send2.at[j], recv_sem=recv2.at[chip],
                device_id=(j // 2, j % 2, c), device_id_type=MESH,
            )

        def from_chip(k):
            return pltpu.make_async_remote_copy(
                src_ref=a_s.at[k], dst_ref=r2.at[k], send_sem=send2.at[k], recv_sem=recv2.at[k],
                device_id=(k // 2, k % 2, c), device_id_type=MESH,
            )

        for j in range(4):
            first[j].wait_recv()

            @pl.when(chip != j)
            def _():
                a_s[j] = (g_ref[2 * j + c].astype(F32) + r1[j].astype(F32)).astype(BF16)
                to_chip(j).start()

        out_ref[...] = g_ref[2 * chip + c].astype(F32) + r1[chip].astype(F32)
        for cp in small_chips:
            cp.wait_recv()
        total_ref[...] = ((sq[0] + sq[1]) + sq[2]) + sq[3]
        for k in range(4):
            @pl.when(chip != k)
            def _():
                from_chip(k).wait_recv()
                out_ref[...] += r2[k].astype(F32)
        small_pair.wait_send()
        for cp in first + small_chips:
            cp.wait_send()
        for j in range(4):
            @pl.when(chip != j)
            def _():
                to_chip(j).wait_send()

    vmem = pl.BlockSpec(memory_space=pltpu.VMEM)
    return pl.pallas_call(
        body,
        name=f"layer{layer}_grad_reduce_scatter",
        out_shape=[jax.ShapeDtypeStruct((rows, n), F32), jax.ShapeDtypeStruct((TOTAL_ROWS, D_MODEL), F32)],
        in_specs=[vmem] * (2 + 2 * DEPTH),
        out_specs=[vmem, vmem],
        scratch_shapes=[
            pltpu.VMEM((4, rows, n), BF16),
            pltpu.VMEM((4, rows, n), BF16),
            pltpu.VMEM((4, rows, n), BF16),
            pltpu.VMEM((TOTAL_ROWS, D_MODEL), F32),
            pltpu.VMEM((TOTAL_ROWS, D_MODEL), F32),
            pltpu.VMEM((4, TOTAL_ROWS, D_MODEL), F32),
            pltpu.SemaphoreType.DMA((4,)),
            pltpu.SemaphoreType.DMA((4,)),
            pltpu.SemaphoreType.DMA((4,)),
            pltpu.SemaphoreType.DMA((4,)),
            pltpu.SemaphoreType.DMA((4,)),
            pltpu.SemaphoreType.DMA((4,)),
        ],
        compiler_params=pltpu.CompilerParams(vmem_limit_bytes=48 * MIB),
    )(pack, *smalls, head, *dws)


def _adam_step(w, g, m, v):
    m = ADAM_B1 * m + (1.0 - ADAM_B1) * g
    v = ADAM_B2 * v + (1.0 - ADAM_B2) * (g * g)
    m_hat = m / (1.0 - ADAM_B1 ** ADAM_STEP)
    v_hat = v / (1.0 - ADAM_B2 ** ADAM_STEP)
    return -ADAM_LR * (m_hat / (jnp.sqrt(v_hat) + ADAM_EPS) + ADAM_WD * w), m, v


def _adamw_rows(name, reduced, row_off, states):
    n = len(states)

    def body(*refs):
        red = refs[:DEPTH]
        ins = refs[DEPTH:DEPTH + 3 * n]
        outs = refs[DEPTH + 3 * n:]
        layer = pl.program_id(0)
        for l in range(DEPTH):
            @pl.when(layer == l)
            def _():
                for k in range(n):
                    w_ref, m_ref, v_ref = ins[3 * k:3 * k + 3]
                    g_ref, d_ref, nm_ref, nv_ref = outs[4 * k:4 * k + 4]
                    g = red[l][row_off[k]:row_off[k] + w_ref.shape[0], :]
                    d, m, v = _adam_step(w_ref[...], g, m_ref[...], v_ref[...])
                    g_ref[...] = g
                    d_ref[...] = d
                    nm_ref[...] = m
                    nv_ref[...] = v

    flat = [a for st in states for a in st]
    state_specs, out_specs, out_shape = [], [], []
    for w, _, _ in states:
        spec = pl.BlockSpec((None,) + w.shape[1:], lambda l: (l, 0, 0))
        state_specs += [spec] * 3
        out_specs += [spec] * 4
        out_shape += [jax.ShapeDtypeStruct(w.shape, F32)] * 4
    red_specs = [pl.BlockSpec(a.shape, lambda l: (0, 0)) for a in reduced]
    operands = [pltpu.with_memory_space_constraint(a, pltpu.HBM) for a in (*reduced, *flat)]
    outs = pl.pallas_call(
        body,
        name=name,
        grid=(DEPTH,),
        out_shape=[pltpu.HBM(a.shape, a.dtype) for a in out_shape],
        in_specs=red_specs + state_specs,
        out_specs=out_specs,
        compiler_params=pltpu.CompilerParams(dimension_semantics=("arbitrary",), vmem_limit_bytes=48 * MIB),
    )(*operands)
    return [tuple(outs[4 * k:4 * k + 4]) for k in range(n)]


def _adamw_small(total, g_conv, g_proj, st):
    names = ["norm_g", "ple_norm_g", "ln_v_g", "ln_v_b", "b_s", "w_s", "final_g", "conv_w", "w_ple_proj"]
    cut = names[:7]

    def body(total_ref, gconv_ref, gproj_ref, *refs):
        ins = {nm: refs[3 * k:3 * k + 3] for k, nm in enumerate(names)}
        outs, pos = {}, 3 * len(names)
        for nm in names:
            cnt = 4 if nm in cut else 3
            outs[nm] = refs[pos:pos + cnt]
            pos += cnt

        def update(nm, idx, g):
            w_ref, m_ref, v_ref = ins[nm]
            d, m, v = _adam_step(w_ref[idx], g, m_ref[idx], v_ref[idx])
            o = outs[nm]
            if nm in cut:
                o[0][idx] = g
                o = o[1:]
            o[0][idx] = d
            o[1][idx] = m
            o[2][idx] = v

        tril = (lax.broadcasted_iota(jnp.int32, (CHUNK, CHUNK), 0) >= lax.broadcasted_iota(jnp.int32, (CHUNK, CHUNK), 1))
        for l in range(DEPTH):
            base = l * SMALL_ROWS
            row = (slice(l, l + 1), slice(None))
            update("norm_g", row, total_ref[base + SMALL_NORM:base + SMALL_NORM + 1, :])
            update("ple_norm_g", row, total_ref[base + SMALL_PLE:base + SMALL_PLE + 1, :])
            update("ln_v_g", row, total_ref[base + SMALL_LN:base + SMALL_LN + 1, 0:WIDTH_A])
            update("ln_v_b", row, total_ref[base + SMALL_LN:base + SMALL_LN + 1, WIDTH_A:2 * WIDTH_A])
            for h in range(HEADS_A):
                update("b_s", (l, slice(h, h + 1), slice(None)),
                       total_ref[base + SMALL_BS:base + SMALL_BS + 1, h * HEAD_DIM:(h + 1) * HEAD_DIM])
                lanes = slice(l * WIDTH_A + h * CHUNK, l * WIDTH_A + (h + 1) * CHUNK)
                update("w_s", (l, h), jnp.where(tril, total_ref[TOTAL_WS:TOTAL_ROWS, lanes], 0.0))
        update("final_g", (slice(None), slice(None)), total_ref[TOTAL_HEAD + HEAD_FINAL:TOTAL_HEAD + HEAD_FINAL + 1, :])
        update("conv_w", (slice(None),) * 3, gconv_ref[...])
        update("w_ple_proj", (slice(None),) * 3, gproj_ref[...])

    flat = [a for nm in names for a in st[nm]]
    out_shape = []
    for nm in names:
        out_shape += [jax.ShapeDtypeStruct(st[nm][0].shape, F32)] * (4 if nm in cut else 3)
    def whole(a):
        return pl.BlockSpec(a.shape, lambda i: (0,) * len(a.shape))

    operands = [pltpu.with_memory_space_constraint(a, pltpu.HBM) for a in (total, g_conv, g_proj, *flat)]
    outs = pl.pallas_call(
        body,
        name="adamw_small",
        grid=(1,),
        out_shape=[pltpu.HBM(a.shape, a.dtype) for a in out_shape],
        in_specs=[whole(a) for a in operands],
        out_specs=[whole(a) for a in out_shape],
        compiler_params=pltpu.CompilerParams(dimension_semantics=("arbitrary",), vmem_limit_bytes=32 * MIB),
    )(*operands)
    res, pos = {}, 0
    for nm in names:
        cnt = 4 if nm in cut else 3
        got = tuple(outs[pos:pos + cnt])
        res[nm] = got if nm in cut else ((g_conv if nm == "conv_w" else g_proj),) + got
        pos += cnt
    return res


def _split3_bf16(a):
    b1 = a.astype(BF16)
    r1 = a - b1.astype(F32)
    b2 = r1.astype(BF16)
    b3 = (r1 - b2.astype(F32)).astype(BF16)
    return b1, b2, b3


def _pack_weight_shard(w_in_l, w_out_l, w_gate_l, w_proj_l, conv_w_l):
    w_in_t = jnp.transpose(w_in_l).astype(BF16)
    proj_t = jnp.transpose(w_proj_l).astype(BF16)
    proj_rows = proj_t.reshape(D_MODEL // PLE_DIM, ROWS_PROJ, PLE_DIM).transpose(1, 0, 2).reshape(ROWS_PROJ, D_MODEL)
    conv_parts = jnp.concatenate([b.reshape(-1) for b in _split3_bf16(conv_w_l)])
    conv_rows = jnp.concatenate([conv_parts, jnp.zeros((ROWS_CONV * D_MODEL - conv_parts.shape[0],), BF16)])
    return jnp.concatenate(
        [w_in_t, w_out_l.astype(BF16), w_gate_l.astype(BF16), proj_rows, conv_rows.reshape(ROWS_CONV, D_MODEL)], axis=0
    )


def _unpack_conv(wg):
    per_dev = wg.reshape(N_DEV, ROWS_LAYER, D_MODEL)
    n_conv = (WIDTH_B // N_DEV) * 3
    conv_parts = per_dev[:, OFF_CONV].astype(F32)[:, :3 * n_conv].reshape(N_DEV, 3, n_conv)
    conv = (conv_parts[:, 0] + conv_parts[:, 1]) + conv_parts[:, 2]
    conv_k = jnp.transpose(conv.reshape(WIDTH_B, 3))
    conv_k = jnp.concatenate([conv_k, jnp.zeros((5, WIDTH_B), F32)], axis=0)
    return conv_k


def _unpack_grad_proj(red):
    proj_rows = red[OFF_PROJ:OFF_PROJ + ROWS_PROJ]
    proj_t = proj_rows.reshape(ROWS_PROJ, D_MODEL // PLE_DIM, PLE_DIM).transpose(1, 0, 2).reshape(ROWS_OUT, PLE_DIM)
    return jnp.transpose(proj_t)


def kernel(x, p, norm_g, w_in, ln_v_g, ln_v_b, w_s, b_s, conv_w, w_out, ple_norm_g, w_ple_gate, w_ple_proj, final_g, loss_target, m_norm_g, m_w_in, m_ln_v_g, m_ln_v_b, m_w_s, m_b_s, m_conv_w, m_w_out, m_ple_norm_g, m_w_ple_gate, m_w_ple_proj, m_final_g, v_norm_g, v_w_in, v_ln_v_g, v_ln_v_b, v_w_s, v_b_s, v_conv_w, v_w_out, v_ple_norm_g, v_w_ple_gate, v_w_ple_proj, v_final_g):
    me = 4 * lax.axis_index("x") + 2 * lax.axis_index("y") + lax.axis_index("c")
    xs = x[0]
    target = loss_target[0]

    shards = [_pack_weight_shard(w_in[l], w_out[l], w_ple_gate[l], w_ple_proj[l], conv_w[l]) for l in range(DEPTH)]
    tril = jnp.tril(jnp.ones((CHUNK, CHUNK), F32))

    def consts(l, wg_l):
        conv_k = _unpack_conv(wg_l)
        w_mix = w_s[l] * tril[None]
        small = dict(
            conv_k=conv_k,
            norm_g=norm_g[l].reshape(1, D_MODEL), ln_g=ln_v_g[l].reshape(1, WIDTH_A), ln_b=ln_v_b[l].reshape(1, WIDTH_A),
            w_mix=w_mix.astype(BF16), w_mix_t=jnp.swapaxes(w_mix, 1, 2).astype(BF16),
            b_mix=jnp.broadcast_to(b_s[l][:, :, None], (HEADS_A, CHUNK, HEAD_DIM)),
            ple_g=ple_norm_g[l].reshape(1, D_MODEL),
        )
        return dict({k: pltpu.with_memory_space_constraint(a, pltpu.HBM) for k, a in small.items()}, wg=wg_l)

    layer_consts = [consts(0, _all_gather_rows(shards[0]))]
    saved = []
    h = xs
    for l in range(DEPTH):
        k = layer_consts[l]
        outs = _forward_layer(
            l, h, p, k["wg"], k["conv_k"], k["norm_g"], k["ln_g"], k["ln_b"], k["w_mix"], k["b_mix"],
            k["ple_g"], next_shard=shards[l + 1] if l + 1 < DEPTH else None)
        proj, hn, cat, r, gpre, x1, x2 = outs[:7]
        if l + 1 < DEPTH:
            layer_consts.append(consts(l + 1, outs[7]))
        saved.append(dict(x_in=h, proj=proj, hn=hn, cat=cat, r=r, gpre=gpre, x1=x1))
        h = x2

    smalls, dws = [None] * DEPTH, [None] * DEPTH
    reduced = [None] * DEPTH
    pending = None
    dx = h
    for l in reversed(range(DEPTH)):
        k, s = layer_consts[l], saved[l]
        outs = _backward_layer(
            l, dx, s["x_in"], s["x1"], s["proj"], s["gpre"], p, k["wg"], k["conv_k"],
            k["norm_g"], k["ln_g"], k["ln_b"], k["w_mix"], k["w_mix_t"], k["b_mix"], k["ple_g"],
            loss_head=(target, final_g.reshape(1, D_MODEL)) if l == DEPTH - 1 else None)
        dx, dproj, dx1, dgpre, dpp, smalls[l], dws[l] = outs[:7]
        if l == DEPTH - 1:
            head = outs[7]
        outs = _weight_grads(l, dproj, s["hn"], s["cat"], dx1, s["r"], dgpre, dpp, p, scatter_pack=pending)
        if pending is not None:
            reduced[l + 1] = _sum_pieces(l + 1, outs[1])
        pending = outs[0]
    reduced[0], total = _reduce_scatter_all_reduce(0, pending, smalls, head, dws)
    grad_x = dx[None]
    loss = total[TOTAL_HEAD + HEAD_LOSS, 0]

    n_ch = WIDTH_B // N_DEV
    g_conv = jnp.stack([total[l * SMALL_ROWS + SMALL_CONV:l * SMALL_ROWS + SMALL_CONV + 3, 0:WIDTH_B] for l in range(DEPTH)], axis=1)
    g_conv = lax.dynamic_slice_in_dim(g_conv, me * n_ch, n_ch, axis=2)
    g_proj = jnp.stack([_unpack_grad_proj(reduced[l]) for l in range(DEPTH)])

    def t_in(a):
        return jnp.swapaxes(a, 1, 2)

    def t_conv(a):
        return jnp.transpose(a, (2, 0, 1))

    (r_in,) = _adamw_rows("adamw_w_in", reduced, [OFF_IN], [(t_in(w_in), t_in(m_w_in), t_in(v_w_in))])
    r_out, r_gate = _adamw_rows(
        "adamw_w_out_gate", reduced, [OFF_OUT, OFF_GATE],
        [(w_out, m_w_out, v_w_out), (w_ple_gate, m_w_ple_gate, v_w_ple_gate)])
    small = _adamw_small(total, g_conv, g_proj, dict(
        norm_g=(norm_g, m_norm_g, v_norm_g), ple_norm_g=(ple_norm_g, m_ple_norm_g, v_ple_norm_g),
        ln_v_g=(ln_v_g, m_ln_v_g, v_ln_v_g), ln_v_b=(ln_v_b, m_ln_v_b, v_ln_v_b),
        b_s=(b_s, m_b_s, v_b_s), w_s=(w_s, m_w_s, v_w_s),
        final_g=tuple(a.reshape(1, D_MODEL) for a in (final_g, m_final_g, v_final_g)),
        conv_w=(t_conv(conv_w), t_conv(m_conv_w), t_conv(v_conv_w)),
        w_ple_proj=(w_ple_proj, m_w_ple_proj, v_w_ple_proj),
    ))
    res = dict(small, w_in=tuple(t_in(a) for a in r_in), w_out=r_out, w_ple_gate=r_gate)
    res["final_g"] = tuple(a.reshape(D_MODEL) for a in res["final_g"])
    res["conv_w"] = tuple(jnp.transpose(a, (1, 2, 0)) for a in res["conv_w"])
    order = ["norm_g", "w_in", "ln_v_g", "ln_v_b", "w_s", "b_s", "conv_w", "w_out", "ple_norm_g", "w_ple_gate", "w_ple_proj", "final_g"]
    return (loss, grad_x, *[res[n][0] for n in order], *[res[n][1] for n in order],
            *[res[n][2] for n in order], *[res[n][3] for n in order])
```

```python
import jax
import jax.numpy as jnp
from jax import lax
from jax.experimental import pallas as pl
from jax.experimental.pallas import tpu as pltpu

F32 = jnp.float32
BF16 = jnp.bfloat16

D_MODEL = 1024
WIDTH_A = 512
WIDTH_B = 512
HEADS_A = 4
HEAD_DIM = 128
CHUNK = 128
PLE_DIM = 256
PROJ_WIDTH = 3584
DEPTH = 2
EPS = 1e-6
N_DEV = 8

ADAM_LR = 0.001
ADAM_B1 = 0.9
ADAM_B2 = 0.999
ADAM_EPS = 1e-08
ADAM_WD = 0.01
ADAM_STEP = 10

ROWS_IN = PROJ_WIDTH // N_DEV
ROWS_OUT = D_MODEL // N_DEV
ROWS_GATE = D_MODEL // N_DEV
ROWS_PROJ = (D_MODEL // N_DEV) * PLE_DIM // D_MODEL
ROWS_CONV = 16
OFF_IN = 0
OFF_OUT = OFF_IN + ROWS_IN
OFF_GATE = OFF_OUT + ROWS_OUT
OFF_PROJ = OFF_GATE + ROWS_GATE
OFF_CONV = OFF_PROJ + ROWS_PROJ
ROWS_GRAD = OFF_CONV
ROWS_SIDE = ROWS_GRAD - OFF_OUT
ROWS_LAYER = OFF_CONV + ROWS_CONV

SMALL_ROWS = 8
SMALL_NORM = 0
SMALL_PLE = 1
SMALL_LN = 2
SMALL_BS = 3
SMALL_CONV = 4
HEAD_FINAL = 0
HEAD_LOSS = 1
TOTAL_HEAD = DEPTH * SMALL_ROWS
TOTAL_WS = TOTAL_HEAD + SMALL_ROWS
TOTAL_ROWS = TOTAL_WS + CHUNK

MIB = 1024 * 1024
MESH = pl.DeviceIdType.MESH

NT_DIMS = (((1,), (1,)), ((), ()))
TN_DIMS = (((0,), (0,)), ((), ()))


def _dot(a, b):
    return jnp.dot(a, b, preferred_element_type=F32)


def _dot_nt(a, b):
    return lax.dot_general(a, b, NT_DIMS, preferred_element_type=F32)


def _dot_tn(a, b):
    return lax.dot_general(a, b, TN_DIMS, preferred_element_type=F32)


def _colsum8(a):
    rows, n = a.shape
    return jnp.sum(a.reshape(rows // 8, 8, n), axis=0)


def _sigmoid(z):
    return 1.0 / (1.0 + jnp.exp(-z))


def _tile(t, want):
    return want if t % want == 0 else t


class _TwoLevelGather:
    def __init__(self, x_ref, out_ref, m_per, send_sems, recv_sems, local_sem):
        x, y, c = lax.axis_index("x"), lax.axis_index("y"), lax.axis_index("c")
        self.c = c
        self.me, self.sibling = (x, y, c), (x, y, 1 - c)
        self.chips = [(1 - x, y), (x, 1 - y), (1 - x, 1 - y)]
        self.x_ref, self.out_ref, self.m_per = x_ref, out_ref, m_per
        self.send_sems, self.recv_sems = send_sems, recv_sems
        self.mine = pltpu.make_async_copy(x_ref, self.rows(*self.me), local_sem)

    def rows(self, px, py, pc):
        return self.out_ref.at[pl.ds((4 * px + 2 * py + pc) * self.m_per, self.m_per), :]

    def copy(self, k, block, to, src=None):
        return pltpu.make_async_remote_copy(
            src_ref=self.rows(*block) if src is None else src,
            dst_ref=self.rows(*block),
            send_sem=self.send_sems.at[k],
            recv_sem=self.recv_sems.at[k],
            device_id=to,
            device_id_type=MESH,
        )

    def first(self):
        out = [self.copy(0, self.me, self.sibling, src=self.x_ref)]
        return out + [self.copy(1 + j, self.me, (*chip, self.c), src=self.x_ref) for j, chip in enumerate(self.chips)]

    def passed(self):
        return [self.copy(4 + j, (*chip, self.c), self.sibling) for j, chip in enumerate(self.chips)]

    def start(self):
        self.mine.start()
        for cp in self.first():
            cp.start()

    def pass_on(self):
        passed = self.passed()
        for j, chip in enumerate(self.chips):
            self.copy(1 + j, (*chip, self.c), self.me).wait_recv()
            passed[j].start()

    def finish(self):
        self.copy(0, self.sibling, self.me).wait_recv()
        for j, chip in enumerate(self.chips):
            self.copy(4 + j, (*chip, 1 - self.c), self.me).wait_recv()
        for cp in self.first() + self.passed():
            cp.wait_send()
        self.mine.wait()


GATHER_SEMS = [pltpu.SemaphoreType.DMA((7,)), pltpu.SemaphoreType.DMA((7,)), pltpu.SemaphoreType.DMA]


def _all_gather_rows(shard):
    m_per, n = shard.shape

    def body(x_ref, out_ref, send_sems, recv_sems, local_sem):
        ag = _TwoLevelGather(x_ref, out_ref, m_per, send_sems, recv_sems, local_sem)
        ag.start()
        ag.pass_on()
        ag.finish()

    return pl.pallas_call(
        body,
        name="weights_all_gather",
        out_shape=pltpu.HBM((N_DEV * m_per, n), shard.dtype),
        in_specs=[pl.BlockSpec(memory_space=pltpu.HBM)],
        out_specs=pl.BlockSpec(memory_space=pltpu.HBM),
        scratch_shapes=list(GATHER_SEMS),
    )(pltpu.with_memory_space_constraint(shard, pltpu.HBM))


PROJ_PARTS = D_MODEL // PLE_DIM
N_WEIGHT_COPIES = N_DEV * (3 + PROJ_PARTS)


def _weight_copies(wg_ref, w_in_t, w_out, w_gate, w_proj_t, sems):
    copies = []
    for s in range(N_DEV):
        base = s * ROWS_LAYER
        for dst, off, rows in ((w_in_t, OFF_IN, ROWS_IN), (w_out, OFF_OUT, ROWS_OUT), (w_gate, OFF_GATE, ROWS_GATE)):
            copies.append((wg_ref.at[pl.ds(base + off, rows), :], dst.at[pl.ds(s * rows, rows), :]))
        for j in range(PROJ_PARTS):
            copies.append((
                wg_ref.at[pl.ds(base + OFF_PROJ, ROWS_PROJ), pl.ds(j * PLE_DIM, PLE_DIM)],
                w_proj_t.at[pl.ds(s * ROWS_OUT + j * ROWS_PROJ, ROWS_PROJ), :],
            ))
    return [pltpu.make_async_copy(src, dst, sems.at[k]) for k, (src, dst) in enumerate(copies)]


def _forward_layer(layer, x, p_all, wg, conv_k, norm_g, ln_g, ln_b, w_mix, b_mix, ple_g, next_shard=None):
    t = x.shape[0]
    tm = _tile(t, 512)
    nt = t // tm
    gathers = next_shard is not None

    def body(*refs):
        (x_ref, p_ref, wg_ref, cw_ref, ng_ref, lng_ref, lnb_ref, wm_ref, bm_ref, pg_ref) = refs[:10]
        refs = refs[10:]
        if gathers:
            shard_ref, refs = refs[0], refs[1:]
        (proj_ref, hn_ref, cat_ref, r_ref, gpre_ref, x1_ref, x2_ref) = refs[:7]
        refs = refs[7:]
        if gathers:
            gathered_ref, refs = refs[0], refs[1:]
        (w_in_t, w_out, w_gate, wpt_ref, vln_s, mixed_s, halo_s, sems) = refs[:8]
        i = pl.program_id(0)
        if gathers:
            ag = _TwoLevelGather(shard_ref, gathered_ref, ROWS_LAYER, *refs[8:11])

            @pl.when(i == 0)
            def _():
                ag.start()

            @pl.when(i == nt // 2)
            def _():
                ag.pass_on()

        @pl.when(i == 0)
        def _():
            copies = _weight_copies(wg_ref, w_in_t, w_out, w_gate, wpt_ref, sems)
            for cp in copies:
                cp.start()
            halo_s[...] = jnp.zeros_like(halo_s)
            for cp in copies:
                cp.wait()

        xv = x_ref[...]
        rstd0 = lax.rsqrt(jnp.mean(xv * xv, axis=-1, keepdims=True) + EPS)
        hn_ref[...] = (xv * rstd0 * ng_ref[...]).astype(BF16)

        def proj_section(k):
            sec = _dot_nt(hn_ref[...], w_in_t[pl.ds(k * 512, 512), :])
            proj_ref[:, k * 512:(k + 1) * 512] = sec.astype(BF16)
            return sec

        v = proj_section(1)
        mu = jnp.mean(v, axis=-1, keepdims=True)
        vc = v - mu
        var = jnp.mean(vc * vc, axis=-1, keepdims=True)
        vln = vc * lax.rsqrt(var + EPS) * lng_ref[...] + lnb_ref[...]
        vln_s[...] = vln.astype(BF16)
        for ci in range(tm // CHUNK):
            rows = pl.ds(ci * CHUNK, CHUNK)
            for h in range(HEADS_A):
                cols = pl.ds(h * HEAD_DIM, HEAD_DIM)
                mixed_s[rows, cols] = _dot(wm_ref[h], vln_s[rows, cols]) + bm_ref[h]
        u = proj_section(0)
        za = proj_section(2)
        out_a = u * mixed_s[...] * (za * _sigmoid(za))
        cat_ref[:, 0:512] = out_a.astype(BF16)

        xc = proj_section(5) * proj_section(3)
        prev = halo_s[...]
        row = lax.broadcasted_iota(jnp.int32, (tm, WIDTH_B), 0)
        xc_m1 = jnp.where(row == 0, prev[7:8, :], pltpu.roll(xc, 1, 0))
        xc_m2 = jnp.where(row == 0, prev[6:7, :], jnp.where(row == 1, prev[7:8, :], pltpu.roll(xc, 2, 0)))
        halo_s[...] = xc[tm - 8:tm, :]
        cw = cw_ref[...]
        yc = cw[0:1, :] * xc_m2 + cw[1:2, :] * xc_m1 + cw[2:3, :] * xc
        zb = proj_section(6)
        out_b = proj_section(4) * yc * (zb * _sigmoid(zb))
        cat_ref[:, 512:1024] = out_b.astype(BF16)

        x1 = xv + _dot(cat_ref[...], w_out[...])
        x1_ref[...] = x1
        rstd1 = lax.rsqrt(jnp.mean(x1 * x1, axis=-1, keepdims=True) + EPS)
        r_ref[...] = (x1 * rstd1 * pg_ref[...]).astype(BF16)
        gpre = _dot(r_ref[...], w_gate[...])
        gpre_ref[...] = gpre.astype(BF16)
        pp = _dot_nt(p_ref[...].astype(BF16), wpt_ref[...])
        x2_ref[...] = x1 + _sigmoid(gpre) * pp

        if gathers:
            @pl.when(i == nt - 1)
            def _():
                ag.finish()

    def tok(width):
        return pl.BlockSpec((tm, width), lambda i: (i, 0))

    def whole(shape):
        return pl.BlockSpec(shape, lambda i: (0,) * len(shape))

    hbm = pl.BlockSpec(memory_space=pl.ANY)
    operands = [x, p_all, wg, conv_k, norm_g, ln_g, ln_b, w_mix, b_mix, ple_g]
    in_specs = [
        tok(D_MODEL), pl.BlockSpec((None, None, tm, PLE_DIM), lambda i: (layer, 0, i, 0)), hbm,
        whole((8, WIDTH_B)), whole((1, D_MODEL)), whole((1, WIDTH_A)), whole((1, WIDTH_A)),
        whole((HEADS_A, CHUNK, CHUNK)), whole((HEADS_A, CHUNK, HEAD_DIM)), whole((1, D_MODEL)),
    ]
    out_specs = [tok(PROJ_WIDTH), tok(D_MODEL), tok(D_MODEL), tok(D_MODEL), tok(D_MODEL), tok(D_MODEL), tok(D_MODEL)]
    out_shape = [
        jax.ShapeDtypeStruct((t, PROJ_WIDTH), BF16),
        jax.ShapeDtypeStruct((t, D_MODEL), BF16),
        jax.ShapeDtypeStruct((t, D_MODEL), BF16),
        jax.ShapeDtypeStruct((t, D_MODEL), BF16),
        jax.ShapeDtypeStruct((t, D_MODEL), BF16),
        jax.ShapeDtypeStruct((t, D_MODEL), F32),
        jax.ShapeDtypeStruct((t, D_MODEL), F32),
    ]
    scratch_shapes = [
        pltpu.VMEM((PROJ_WIDTH, D_MODEL), BF16),
        pltpu.VMEM((D_MODEL, D_MODEL), BF16),
        pltpu.VMEM((D_MODEL, D_MODEL), BF16),
        pltpu.VMEM((D_MODEL, PLE_DIM), BF16),
        pltpu.VMEM((tm, WIDTH_A), BF16),
        pltpu.VMEM((tm, WIDTH_A), F32),
        pltpu.VMEM((8, WIDTH_B), F32),
        pltpu.SemaphoreType.DMA((N_WEIGHT_COPIES,)),
    ]
    if gathers:
        operands.append(pltpu.with_memory_space_constraint(next_shard, pltpu.HBM))
        in_specs.append(pl.BlockSpec(memory_space=pltpu.HBM))
        out_specs.append(pl.BlockSpec(memory_space=pltpu.HBM))
        out_shape.append(pltpu.HBM((N_DEV * ROWS_LAYER, D_MODEL), BF16))
        scratch_shapes += list(GATHER_SEMS)

    return pl.pallas_call(
        body,
        name=f"layer{layer}_forward",
        grid=(nt,),
        in_specs=in_specs,
        out_specs=out_specs,
        out_shape=out_shape,
        scratch_shapes=scratch_shapes,
        compiler_params=pltpu.CompilerParams(dimension_semantics=("arbitrary",), vmem_limit_bytes=56 * MIB),
    )(*operands)


class _DirectScatter:
    def __init__(self, pack_ref, pieces_ref, send_sems, recv_sems, local_sem):
        x, y, c = lax.axis_index("x"), lax.axis_index("y"), lax.axis_index("c")
        me = 4 * x + 2 * y + c
        self.copies = []
        for k in range(N_DEV - 1):
            fx, fy, fc = ((k + 1) >> 2) & 1, ((k + 1) >> 1) & 1, (k + 1) & 1
            tx, ty, tc = x ^ fx, y ^ fy, c ^ fc
            self.copies.append(
                pltpu.make_async_remote_copy(
                    src_ref=pack_ref.at[4 * tx + 2 * ty + tc], dst_ref=pieces_ref.at[me],
                    send_sem=send_sems.at[k], recv_sem=recv_sems.at[k],
                    device_id=(tx, ty, tc), device_id_type=MESH,
                )
            )
        self.mine = pltpu.make_async_copy(pack_ref.at[me], pieces_ref.at[me], local_sem)

    def start(self):
        self.mine.start()
        for cp in self.copies:
            cp.start()

    def finish(self):
        for cp in self.copies:
            cp.wait_recv()
        for cp in self.copies:
            cp.wait_send()
        self.mine.wait()


SCATTER_SEMS = [pltpu.SemaphoreType.DMA((N_DEV - 1,)), pltpu.SemaphoreType.DMA((N_DEV - 1,)), pltpu.SemaphoreType.DMA]


def _backward_layer(layer, dx2, x_in, x1, proj, gpre, cat, r, p_all, wg, conv_k, norm_g, ln_g, ln_b,
                    w_mix, w_mix_t, b_mix, ple_g, loss_head=None):
    t = x_in.shape[0]
    tm = _tile(t, 256)
    nt = t // tm
    n_chunks = tm // CHUNK
    halo_rows = 16
    heads = loss_head is not None

    def body(*refs):
        (dx2_ref, xin_ref, x1_ref, proj_ref, halo_ref, gpre_ref, cat_ref, r_ref, p_ref, wg_ref, cw_ref,
         ng_ref, lng_ref, lnb_ref, wm_ref, wmt_ref, bm_ref, pg_ref) = refs[:18]
        refs = refs[18:]
        if heads:
            tgt_ref, fg_ref = refs[:2]
            refs = refs[2:]
        (dxin_ref, dproj_ref, small_ref, dws_ref, pack_ref) = refs[:5]
        refs = refs[5:]
        if heads:
            head_ref, refs = refs[0], refs[1:]
        (w_in_t, w_out, w_gate, wpt_ref, vln_s, mixed_s, dmix_s, dvln_s, carry_s,
         ng_acc, pg_acc, lng_acc, lnb_acc, cw_acc, dbm_ref, acc_out, acc_gate, acc_proj, stage, stage_sems,
         sems) = refs[:21]
        if heads:
            loss_acc, fg_acc = refs[21:23]
        i = pl.program_id(0)
        tile = nt - 1 - i

        @pl.when(i == 0)
        def _():
            copies = _weight_copies(wg_ref, w_in_t, w_out, w_gate, wpt_ref, sems)
            for cp in copies:
                cp.start()
            if heads:
                loss_acc[...] = jnp.zeros_like(loss_acc)
                fg_acc[...] = jnp.zeros_like(fg_acc)
            acc_out[...] = jnp.zeros_like(acc_out)
            acc_gate[...] = jnp.zeros_like(acc_gate)
            acc_proj[...] = jnp.zeros_like(acc_proj)
            carry_s[...] = jnp.zeros_like(carry_s)
            ng_acc[...] = jnp.zeros_like(ng_acc)
            pg_acc[...] = jnp.zeros_like(pg_acc)
            lng_acc[...] = jnp.zeros_like(lng_acc)
            lnb_acc[...] = jnp.zeros_like(lnb_acc)
            cw_acc[...] = jnp.zeros_like(cw_acc)
            dws_ref[...] = jnp.zeros_like(dws_ref)
            dbm_ref[...] = jnp.zeros_like(dbm_ref)
            for cp in copies:
                cp.wait()

        if heads:
            x2v = dx2_ref[...]
            fg = fg_ref[...]
            rstdf = lax.rsqrt(jnp.mean(x2v * x2v, axis=-1, keepdims=True) + EPS)
            xhatf = x2v * rstdf
            err = xhatf * fg - tgt_ref[...]
            loss_acc[...] += _colsum8(err * err)
            dy = err * (1.0 / D_MODEL)
            fg_acc[...] += _colsum8(dy * xhatf)
            dxhf = dy * fg
            dx2v = rstdf * (dxhf - xhatf * jnp.mean(dxhf * xhatf, axis=-1, keepdims=True))
        else:
            dx2v = dx2_ref[...]

        gate = _sigmoid(gpre_ref[...].astype(F32))
        pb = p_ref[...].astype(BF16)
        pp = _dot_nt(pb, wpt_ref[...])
        dpp = (dx2v * gate).astype(BF16)
        dgpre = (dx2v * pp * gate * (1.0 - gate)).astype(BF16)
        dr = _dot_nt(dgpre, w_gate[...])
        x1v = x1_ref[...]
        rstd1 = lax.rsqrt(jnp.mean(x1v * x1v, axis=-1, keepdims=True) + EPS)
        xhat1 = x1v * rstd1
        pg_acc[...] += _colsum8(dr * xhat1)
        dxh = dr * pg_ref[...]
        dx1 = dx2v + rstd1 * (dxh - xhat1 * jnp.mean(dxh * xhat1, axis=-1, keepdims=True))
        dx1b = dx1.astype(BF16)

        for b in range(D_MODEL // 512):
            blk = pl.ds(b * 512, 512)
            acc_gate[blk, :] += _dot_tn(r_ref[:, b * 512:(b + 1) * 512], dgpre)
            acc_proj[blk, :] += _dot_tn(dpp[:, b * 512:(b + 1) * 512], pb)
            acc_out[blk, :] += _dot_tn(cat_ref[:, b * 512:(b + 1) * 512], dx1b)

        dcat = _dot_nt(dx1b, w_out[...])
        dca = dcat[:, 0:512]
        dcb = dcat[:, 512:1024]

        u = proj_ref[:, 0:512].astype(F32)
        v = proj_ref[:, 512:1024].astype(F32)
        za = proj_ref[:, 1024:1536].astype(F32)
        mu = jnp.mean(v, axis=-1, keepdims=True)
        vc = v - mu
        var = jnp.mean(vc * vc, axis=-1, keepdims=True)
        rs = lax.rsqrt(var + EPS)
        vhat = vc * rs
        lng = lng_ref[...]
        vln_s[...] = (vhat * lng + lnb_ref[...]).astype(BF16)
        for ci in range(n_chunks):
            rows = pl.ds(ci * CHUNK, CHUNK)
            for h in range(HEADS_A):
                cols = pl.ds(h * HEAD_DIM, HEAD_DIM)
                mixed_s[rows, cols] = _dot(wm_ref[h], vln_s[rows, cols]) + bm_ref[h]
        mixed = mixed_s[...]
        sga = _sigmoid(za)
        sa = za * sga
        dsa = sga * (1.0 + za * (1.0 - sga))
        def put_section(k, val):
            dproj_ref[:, k * 512:(k + 1) * 512] = val.astype(BF16)

        put_section(0, dca * mixed * sa)
        dmix = dca * u * sa
        put_section(2, dca * u * mixed * dsa)
        dmix_s[...] = dmix.astype(BF16)
        dbm_acc = jnp.zeros((CHUNK, WIDTH_A), F32)
        for ci in range(n_chunks):
            rows = pl.ds(ci * CHUNK, CHUNK)
            dbm_acc = dbm_acc + dmix[ci * CHUNK:(ci + 1) * CHUNK, :]
            for h in range(HEADS_A):
                cols = pl.ds(h * HEAD_DIM, HEAD_DIM)
                dvln_s[rows, cols] = _dot(wmt_ref[h], dmix_s[rows, cols])
                dws_ref[:, cols] += _dot_nt(dmix_s[rows, cols], vln_s[rows, cols])
        dbm_ref[...] += dbm_acc
        dvln = dvln_s[...]
        lng_acc[...] += _colsum8(dvln * vhat)
        lnb_acc[...] += _colsum8(dvln)
        dvh = dvln * lng
        dv = rs * (dvh - jnp.mean(dvh, axis=-1, keepdims=True) - vhat * jnp.mean(dvh * vhat, axis=-1, keepdims=True))
        put_section(1, dv)

        hb = proj_ref[:, 1536:2048].astype(F32)
        gb = proj_ref[:, 2048:2560].astype(F32)
        gc = proj_ref[:, 2560:3072].astype(F32)
        zb = proj_ref[:, 3072:3584].astype(F32)
        xc = gc * hb
        prev = halo_ref[:, 2560:3072].astype(F32) * halo_ref[:, 1536:2048].astype(F32)
        prev = jnp.where(tile > 0, prev, 0.0)
        row = lax.broadcasted_iota(jnp.int32, (tm, WIDTH_B), 0)
        p1 = prev[halo_rows - 1:halo_rows, :]
        p2 = prev[halo_rows - 2:halo_rows - 1, :]
        xc_m1 = jnp.where(row == 0, p1, pltpu.roll(xc, 1, 0))
        xc_m2 = jnp.where(row == 0, p2, jnp.where(row == 1, p1, pltpu.roll(xc, 2, 0)))
        cw = cw_ref[...]
        yc = cw[0:1, :] * xc_m2 + cw[1:2, :] * xc_m1 + cw[2:3, :] * xc
        sgb = _sigmoid(zb)
        sb = zb * sgb
        dsb = sgb * (1.0 + zb * (1.0 - sgb))
        put_section(4, dcb * yc * sb)
        dyc = dcb * gb * sb
        put_section(6, dcb * gb * yc * dsb)
        nxt = carry_s[...]
        dyc_p1 = jnp.where(row == tm - 1, nxt[0:1, :], pltpu.roll(dyc, tm - 1, 0))
        dyc_p2 = jnp.where(row == tm - 1, nxt[1:2, :], jnp.where(row == tm - 2, nxt[0:1, :], pltpu.roll(dyc, tm - 2, 0)))
        carry_s[...] = dyc[0:8, :]
        dxc = cw[2:3, :] * dyc + cw[1:2, :] * dyc_p1 + cw[0:1, :] * dyc_p2
        cw_acc[0] += _colsum8(dyc * xc_m2)
        cw_acc[1] += _colsum8(dyc * xc_m1)
        cw_acc[2] += _colsum8(dyc * xc)
        put_section(3, dxc * gc)
        put_section(5, dxc * hb)

        dhn = _dot(dproj_ref[...], w_in_t[...])
        xv = xin_ref[...]
        rstd0 = lax.rsqrt(jnp.mean(xv * xv, axis=-1, keepdims=True) + EPS)
        xhat0 = xv * rstd0
        ng_acc[...] += _colsum8(dhn * xhat0)
        dxh0 = dhn * ng_ref[...]
        dxin_ref[...] = dx1 + rstd0 * (dxh0 - xhat0 * jnp.mean(dxh0 * xhat0, axis=-1, keepdims=True))

        @pl.when(i == nt - 1)
        def _():
            small_ref[...] = jnp.zeros_like(small_ref)
            small_ref[SMALL_NORM:SMALL_NORM + 1, :] = jnp.sum(ng_acc[...], axis=0, keepdims=True)
            small_ref[SMALL_PLE:SMALL_PLE + 1, :] = jnp.sum(pg_acc[...], axis=0, keepdims=True)
            small_ref[SMALL_LN:SMALL_LN + 1, 0:WIDTH_A] = jnp.sum(lng_acc[...], axis=0, keepdims=True)
            small_ref[SMALL_LN:SMALL_LN + 1, WIDTH_A:2 * WIDTH_A] = jnp.sum(lnb_acc[...], axis=0, keepdims=True)
            for h in range(HEADS_A):
                cols = pl.ds(h * HEAD_DIM, HEAD_DIM)
                small_ref[SMALL_BS:SMALL_BS + 1, cols] = jnp.sum(jnp.transpose(dbm_ref[:, cols]), axis=0, keepdims=True)
            for k in range(3):
                small_ref[SMALL_CONV + k:SMALL_CONV + k + 1, 0:WIDTH_B] = jnp.sum(cw_acc[k], axis=0, keepdims=True)
            if heads:
                total = jnp.sum(loss_acc[...]) * (0.5 / D_MODEL)
                rows8 = lax.broadcasted_iota(jnp.int32, (SMALL_ROWS, D_MODEL), 0)
                lanes8 = lax.broadcasted_iota(jnp.int32, (SMALL_ROWS, D_MODEL), 1)
                head_ref[...] = jnp.where((rows8 == HEAD_LOSS) & (lanes8 == 0), total, 0.0)
                head_ref[HEAD_FINAL:HEAD_FINAL + 1, :] = jnp.sum(fg_acc[...], axis=0, keepdims=True)

            def out_copy(s):
                return pltpu.make_async_copy(
                    stage.at[s % 2], pack_ref.at[s, pl.ds(OFF_OUT, ROWS_SIDE), :], stage_sems.at[s % 2])

            for s in range(N_DEV):
                if s >= 2:
                    out_copy(s - 2).wait()
                buf = stage.at[s % 2]
                buf[pl.ds(0, ROWS_OUT), :] = acc_out[pl.ds(s * ROWS_OUT, ROWS_OUT), :].astype(BF16)
                buf[pl.ds(ROWS_OUT, ROWS_GATE), :] = acc_gate[pl.ds(s * ROWS_GATE, ROWS_GATE), :].astype(BF16)
                for j in range(PROJ_PARTS):
                    buf[pl.ds(ROWS_OUT + ROWS_GATE, ROWS_PROJ), pl.ds(j * PLE_DIM, PLE_DIM)] = acc_proj[
                        pl.ds(s * ROWS_OUT + j * ROWS_PROJ, ROWS_PROJ), :
                    ].astype(BF16)
                out_copy(s).start()
            out_copy(N_DEV - 2).wait()
            out_copy(N_DEV - 1).wait()

    def tok(width):
        return pl.BlockSpec((tm, width), lambda i: (nt - 1 - i, 0))

    def whole(shape):
        return pl.BlockSpec(shape, lambda i: (0,) * len(shape))

    halo_spec = pl.BlockSpec(
        (halo_rows, PROJ_WIDTH), lambda i: (jnp.maximum((nt - 1 - i) * (tm // halo_rows) - 1, 0), 0)
    )
    hbm = pl.BlockSpec(memory_space=pl.ANY)
    operands = [dx2, x_in, x1, proj, proj, gpre, cat, r, p_all, wg, conv_k, norm_g, ln_g, ln_b, w_mix, w_mix_t, b_mix, ple_g]
    in_specs = [
        tok(D_MODEL), tok(D_MODEL), tok(D_MODEL), tok(PROJ_WIDTH), halo_spec, tok(D_MODEL), tok(D_MODEL), tok(D_MODEL),
        pl.BlockSpec((None, None, tm, PLE_DIM), lambda i: (layer, 0, nt - 1 - i, 0)), hbm,
        whole((8, WIDTH_B)), whole((1, D_MODEL)), whole((1, WIDTH_A)), whole((1, WIDTH_A)),
        whole((HEADS_A, CHUNK, CHUNK)), whole((HEADS_A, CHUNK, CHUNK)), whole((HEADS_A, CHUNK, HEAD_DIM)),
        whole((1, D_MODEL)),
    ]
    out_specs = [tok(D_MODEL), tok(PROJ_WIDTH), whole((SMALL_ROWS, D_MODEL)), whole((CHUNK, WIDTH_A)), hbm]
    out_shape = [
        jax.ShapeDtypeStruct((t, D_MODEL), F32),
        jax.ShapeDtypeStruct((t, PROJ_WIDTH), BF16),
        jax.ShapeDtypeStruct((SMALL_ROWS, D_MODEL), F32),
        jax.ShapeDtypeStruct((CHUNK, WIDTH_A), F32),
        jax.ShapeDtypeStruct((N_DEV, ROWS_GRAD, D_MODEL), BF16),
    ]
    scratch_shapes = [
        pltpu.VMEM((PROJ_WIDTH, D_MODEL), BF16),
        pltpu.VMEM((D_MODEL, D_MODEL), BF16),
        pltpu.VMEM((D_MODEL, D_MODEL), BF16),
        pltpu.VMEM((D_MODEL, PLE_DIM), BF16),
        pltpu.VMEM((tm, WIDTH_A), BF16),
        pltpu.VMEM((tm, WIDTH_A), F32),
        pltpu.VMEM((tm, WIDTH_A), BF16),
        pltpu.VMEM((tm, WIDTH_A), F32),
        pltpu.VMEM((8, WIDTH_B), F32),
        pltpu.VMEM((8, D_MODEL), F32),
        pltpu.VMEM((8, D_MODEL), F32),
        pltpu.VMEM((8, WIDTH_A), F32),
        pltpu.VMEM((8, WIDTH_A), F32),
        pltpu.VMEM((3, 8, WIDTH_B), F32),
        pltpu.VMEM((CHUNK, WIDTH_A), F32),
        pltpu.VMEM((D_MODEL, D_MODEL), F32),
        pltpu.VMEM((D_MODEL, D_MODEL), F32),
        pltpu.VMEM((D_MODEL, PLE_DIM), F32),
        pltpu.VMEM((2, ROWS_SIDE, D_MODEL), BF16),
        pltpu.SemaphoreType.DMA((2,)),
        pltpu.SemaphoreType.DMA((N_WEIGHT_COPIES,)),
    ]
    if heads:
        operands += list(loss_head)
        in_specs += [tok(D_MODEL), whole((1, D_MODEL))]
        out_specs.append(whole((SMALL_ROWS, D_MODEL)))
        out_shape.append(jax.ShapeDtypeStruct((SMALL_ROWS, D_MODEL), F32))
        scratch_shapes += [pltpu.VMEM((8, D_MODEL), F32), pltpu.VMEM((8, D_MODEL), F32)]

    return pl.pallas_call(
        body,
        name=f"layer{layer}_backward",
        grid=(nt,),
        in_specs=in_specs,
        out_specs=out_specs,
        out_shape=out_shape,
        scratch_shapes=scratch_shapes,
        compiler_params=pltpu.CompilerParams(dimension_semantics=("arbitrary",), vmem_limit_bytes=56 * MIB),
    )(*operands)


def _sum_pieces(layer, pieces):
    rows, n = pieces.shape[1], pieces.shape[2]
    blocks = 2
    rb = rows // blocks

    def body(p_ref, out_ref):
        total = p_ref[0].astype(F32)
        for j in range(1, N_DEV):
            total = total + p_ref[j].astype(F32)
        out_ref[...] = total

    return pl.pallas_call(
        body,
        name=f"layer{layer}_grad_sum",
        grid=(blocks,),
        out_shape=pltpu.HBM((rows, n), F32),
        in_specs=[pl.BlockSpec((N_DEV, rb, n), lambda i: (0, i, 0))],
        out_specs=pl.BlockSpec((rb, n), lambda i: (i, 0)),
        compiler_params=pltpu.CompilerParams(dimension_semantics=("arbitrary",), vmem_limit_bytes=32 * MIB),
    )(pieces)


def _weight_grads(layer, dproj, hn, pack, scatter_pack=None):
    t = hn.shape[0]
    tk = _tile(t, 512)
    nt = t // tk
    in_blocks = PROJ_WIDTH // 512
    scatters = scatter_pack is not None

    def body(*refs):
        (dproj_ref, hn_ref, _) = refs[:3]
        refs = refs[3:]
        if scatters:
            prior_ref, refs = refs[0], refs[1:]
        pack_ref, refs = refs[0], refs[1:]
        if scatters:
            pieces_ref, refs = refs[0], refs[1:]
        (acc_in, stage, sems) = refs[:3]
        i = pl.program_id(0)
        if scatters:
            scatter = _DirectScatter(prior_ref, pieces_ref, *refs[3:6])

            @pl.when(i == 0)
            def _():
                scatter.start()

        @pl.when(i == 0)
        def _():
            acc_in[...] = jnp.zeros_like(acc_in)

        hnv = hn_ref[...]
        for b in range(in_blocks):
            acc_in[pl.ds(b * 512, 512), :] += _dot_tn(dproj_ref[:, b * 512:(b + 1) * 512], hnv)

        @pl.when(i == nt - 1)
        def _():
            def out_copy(s):
                return pltpu.make_async_copy(stage.at[s % 2], pack_ref.at[s, pl.ds(OFF_IN, ROWS_IN), :], sems.at[s % 2])

            for s in range(N_DEV):
                if s >= 2:
                    out_copy(s - 2).wait()
                stage[s % 2] = acc_in[pl.ds(s * ROWS_IN, ROWS_IN), :].astype(BF16)
                out_copy(s).start()
            out_copy(N_DEV - 2).wait()
            out_copy(N_DEV - 1).wait()
            if scatters:
                scatter.finish()

    def tok(width):
        return pl.BlockSpec((tk, width), lambda i: (i, 0))

    hbm = pl.BlockSpec(memory_space=pl.ANY)
    pack_shape = jax.ShapeDtypeStruct((N_DEV, ROWS_GRAD, D_MODEL), BF16)
    operands = [dproj, hn, pack]
    in_specs = [tok(PROJ_WIDTH), tok(D_MODEL), hbm]
    out_specs, out_shape = [hbm], [pack_shape]
    scratch_shapes = [
        pltpu.VMEM((PROJ_WIDTH, D_MODEL), F32),
        pltpu.VMEM((2, ROWS_IN, D_MODEL), BF16),
        pltpu.SemaphoreType.DMA((2,)),
    ]
    if scatters:
        operands.append(scatter_pack)
        in_specs.append(hbm)
        out_specs.append(hbm)
        out_shape.append(pack_shape)
        scratch_shapes += list(SCATTER_SEMS)

    return pl.pallas_call(
        body,
        name=f"layer{layer}_weight_grads",
        grid=(nt,),
        in_specs=in_specs,
        out_specs=out_specs,
        out_shape=out_shape,
        scratch_shapes=scratch_shapes,
        input_output_aliases={2: 0},
        compiler_params=pltpu.CompilerParams(dimension_semantics=("arbitrary",), vmem_limit_bytes=48 * MIB),
    )(*operands)


def _reduce_scatter_all_reduce(layer, pack, smalls, head, dws):
    rows, n = pack.shape[1], pack.shape[2]
    assert DEPTH * WIDTH_A == D_MODEL and n == D_MODEL

    def body(g_ref, *refs):
        small_refs, refs = refs[:DEPTH], refs[DEPTH:]
        head_ref, refs = refs[0], refs[1:]
        dws_refs, refs = refs[:DEPTH], refs[DEPTH:]
        (out_ref, total_ref, r1, a_s, r2, sp, sr1, sq, send1, recv1, send2, recv2, ssend, srecv) = refs
        x, y, c = lax.axis_index("x"), lax.axis_index("y"), lax.axis_index("c")
        sibling = (x, y, 1 - c)
        chip = 2 * x + y
        flips = [(1, 0), (0, 1), (1, 1)]

        for l in range(DEPTH):
            sp[l * SMALL_ROWS:(l + 1) * SMALL_ROWS, :] = small_refs[l][...]
            sp[TOTAL_WS:TOTAL_ROWS, l * WIDTH_A:(l + 1) * WIDTH_A] = dws_refs[l][...]
        sp[TOTAL_HEAD:TOTAL_WS, :] = head_ref[...]

        small_pair = pltpu.make_async_remote_copy(
            src_ref=sp, dst_ref=sr1, send_sem=ssend.at[0], recv_sem=srecv.at[0], device_id=sibling, device_id_type=MESH
        )

        def to_sibling(j):
            return pltpu.make_async_remote_copy(
                src_ref=g_ref.at[2 * j + 1 - c], dst_ref=r1.at[j], send_sem=send1.at[j], recv_sem=recv1.at[j],
                device_id=sibling, device_id_type=MESH,
            )

        first = [to_sibling(j) for j in range(4)]
        small_pair.start()
        for cp in first:
            cp.start()

        small_pair.wait_recv()
        sq[chip] = sp[...] + sr1[...]
        small_chips = [
            pltpu.make_async_remote_copy(
                src_ref=sq.at[chip], dst_ref=sq.at[chip], send_sem=ssend.at[1 + k], recv_sem=srecv.at[1 + k],
                device_id=(x ^ fx, y ^ fy, c), device_id_type=MESH,
            )
            for k, (fx, fy) in enumerate(flips)
        ]
        for cp in small_chips:
            cp.start()

        def to_chip(j):
            return pltpu.make_async_remote_copy(
                src_ref=a_s.at[j], dst_ref=r2.at[chip], send_sem=send2.at[j], recv_sem=recv2.at[chip],
                device_id=(j // 2, j % 2, c), device_id_type=MESH,
            )

        def from_chip(k):
            return pltpu.make_async_remote_copy(
                src_ref=a_s.at[k], dst_ref=r2.at[k], send_sem=send2.at[k], recv_sem=recv2.at[k],
                device_id=(k // 2, k % 2, c), device_id_type=MESH,
            )

        for j in range(4):
            first[j].wait_recv()

            @pl.when(chip != j)
            def _():
                a_s[j] = (g_ref[2 * j + c].astype(F32) + r1[j].astype(F32)).astype(BF16)
                to_chip(j).start()

        out_ref[...] = g_ref[2 * chip + c].astype(F32) + r1[chip].astype(F32)
        for cp in small_chips:
            cp.wait_recv()
        total_ref[...] = ((sq[0] + sq[1]) + sq[2]) + sq[3]
        for k in range(4):
            @pl.when(chip != k)
            def _():
                from_chip(k).wait_recv()
                out_ref[...] += r2[k].astype(F32)
        small_pair.wait_send()
        for cp in first + small_chips:
            cp.wait_send()
        for j in range(4):
            @pl.when(chip != j)
            def _():
                to_chip(j).wait_send()

    vmem = pl.BlockSpec(memory_space=pltpu.VMEM)
    return pl.pallas_call(
        body,
        name=f"layer{layer}_grad_reduce_scatter",
        out_shape=[jax.ShapeDtypeStruct((rows, n), F32), jax.ShapeDtypeStruct((TOTAL_ROWS, D_MODEL), F32)],
        in_specs=[vmem] * (2 + 2 * DEPTH),
        out_specs=[vmem, vmem],
        scratch_shapes=[
            pltpu.VMEM((4, rows, n), BF16),
            pltpu.VMEM((4, rows, n), BF16),
            pltpu.VMEM((4, rows, n), BF16),
            pltpu.VMEM((TOTAL_ROWS, D_MODEL), F32),
            pltpu.VMEM((TOTAL_ROWS, D_MODEL), F32),
            pltpu.VMEM((4, TOTAL_ROWS, D_MODEL), F32),
            pltpu.SemaphoreType.DMA((4,)),
            pltpu.SemaphoreType.DMA((4,)),
            pltpu.SemaphoreType.DMA((4,)),
            pltpu.SemaphoreType.DMA((4,)),
            pltpu.SemaphoreType.DMA((4,)),
            pltpu.SemaphoreType.DMA((4,)),
        ],
        compiler_params=pltpu.CompilerParams(vmem_limit_bytes=48 * MIB),
    )(pack, *smalls, head, *dws)


def _adam_step(w, g, m, v):
    m = ADAM_B1 * m + (1.0 - ADAM_B1) * g
    v = ADAM_B2 * v + (1.0 - ADAM_B2) * (g * g)
    m_hat = m / (1.0 - ADAM_B1 ** ADAM_STEP)
    v_hat = v / (1.0 - ADAM_B2 ** ADAM_STEP)
    return -ADAM_LR * (m_hat / (jnp.sqrt(v_hat) + ADAM_EPS) + ADAM_WD * w), m, v


def _adamw_rows(name, reduced, row_off, states):
    n = len(states)

    def body(*refs):
        red = refs[:DEPTH]
        ins = refs[DEPTH:DEPTH + 3 * n]
        outs = refs[DEPTH + 3 * n:]
        layer = pl.program_id(0)
        for l in range(DEPTH):
            @pl.when(layer == l)
            def _():
                for k in range(n):
                    w_ref, m_ref, v_ref = ins[3 * k:3 * k + 3]
                    g_ref, d_ref, nm_ref, nv_ref = outs[4 * k:4 * k + 4]
                    g = red[l][row_off[k]:row_off[k] + w_ref.shape[0], :]
                    d, m, v = _adam_step(w_ref[...], g, m_ref[...], v_ref[...])
                    g_ref[...] = g
                    d_ref[...] = d
                    nm_ref[...] = m
                    nv_ref[...] = v

    flat = [a for st in states for a in st]
    state_specs, out_specs, out_shape = [], [], []
    for w, _, _ in states:
        spec = pl.BlockSpec((None,) + w.shape[1:], lambda l: (l, 0, 0))
        state_specs += [spec] * 3
        out_specs += [spec] * 4
        out_shape += [jax.ShapeDtypeStruct(w.shape, F32)] * 4
    red_specs = [pl.BlockSpec(a.shape, lambda l: (0, 0)) for a in reduced]
    operands = [pltpu.with_memory_space_constraint(a, pltpu.HBM) for a in (*reduced, *flat)]
    outs = pl.pallas_call(
        body,
        name=name,
        grid=(DEPTH,),
        out_shape=[pltpu.HBM(a.shape, a.dtype) for a in out_shape],
        in_specs=red_specs + state_specs,
        out_specs=out_specs,
        compiler_params=pltpu.CompilerParams(dimension_semantics=("arbitrary",), vmem_limit_bytes=48 * MIB),
    )(*operands)
    return [tuple(outs[4 * k:4 * k + 4]) for k in range(n)]


def _adamw_small(total, g_conv, g_proj, st):
    names = ["norm_g", "ple_norm_g", "ln_v_g", "ln_v_b", "b_s", "w_s", "final_g", "conv_w", "w_ple_proj"]
    cut = names[:7]

    def body(total_ref, gconv_ref, gproj_ref, *refs):
        ins = {nm: refs[3 * k:3 * k + 3] for k, nm in enumerate(names)}
        outs, pos = {}, 3 * len(names)
        for nm in names:
            cnt = 4 if nm in cut else 3
            outs[nm] = refs[pos:pos + cnt]
            pos += cnt

        def update(nm, idx, g):
            w_ref, m_ref, v_ref = ins[nm]
            d, m, v = _adam_step(w_ref[idx], g, m_ref[idx], v_ref[idx])
            o = outs[nm]
            if nm in cut:
                o[0][idx] = g
                o = o[1:]
            o[0][idx] = d
            o[1][idx] = m
            o[2][idx] = v

        tril = (lax.broadcasted_iota(jnp.int32, (CHUNK, CHUNK), 0) >= lax.broadcasted_iota(jnp.int32, (CHUNK, CHUNK), 1))
        for l in range(DEPTH):
            base = l * SMALL_ROWS
            row = (slice(l, l + 1), slice(None))
            update("norm_g", row, total_ref[base + SMALL_NORM:base + SMALL_NORM + 1, :])
            update("ple_norm_g", row, total_ref[base + SMALL_PLE:base + SMALL_PLE + 1, :])
            update("ln_v_g", row, total_ref[base + SMALL_LN:base + SMALL_LN + 1, 0:WIDTH_A])
            update("ln_v_b", row, total_ref[base + SMALL_LN:base + SMALL_LN + 1, WIDTH_A:2 * WIDTH_A])
            for h in range(HEADS_A):
                update("b_s", (l, slice(h, h + 1), slice(None)),
                       total_ref[base + SMALL_BS:base + SMALL_BS + 1, h * HEAD_DIM:(h + 1) * HEAD_DIM])
                lanes = slice(l * WIDTH_A + h * CHUNK, l * WIDTH_A + (h + 1) * CHUNK)
                update("w_s", (l, h), jnp.where(tril, total_ref[TOTAL_WS:TOTAL_ROWS, lanes], 0.0))
        update("final_g", (slice(None), slice(None)), total_ref[TOTAL_HEAD + HEAD_FINAL:TOTAL_HEAD + HEAD_FINAL + 1, :])
        update("conv_w", (slice(None),) * 3, gconv_ref[...])
        update("w_ple_proj", (slice(None),) * 3, gproj_ref[...])

    flat = [a for nm in names for a in st[nm]]
    out_shape = []
    for nm in names:
        out_shape += [jax.ShapeDtypeStruct(st[nm][0].shape, F32)] * (4 if nm in cut else 3)
    def whole(a):
        return pl.BlockSpec(a.shape, lambda i: (0,) * len(a.shape))

    operands = [pltpu.with_memory_space_constraint(a, pltpu.HBM) for a in (total, g_conv, g_proj, *flat)]
    outs = pl.pallas_call(
        body,
        name="adamw_small",
        grid=(1,),
        out_shape=[pltpu.HBM(a.shape, a.dtype) for a in out_shape],
        in_specs=[whole(a) for a in operands],
        out_specs=[whole(a) for a in out_shape],
        compiler_params=pltpu.CompilerParams(dimension_semantics=("arbitrary",), vmem_limit_bytes=32 * MIB),
    )(*operands)
    res, pos = {}, 0
    for nm in names:
        cnt = 4 if nm in cut else 3
        got = tuple(outs[pos:pos + cnt])
        res[nm] = got if nm in cut else ((g_conv if nm == "conv_w" else g_proj),) + got
        pos += cnt
    return res


def _split3_bf16(a):
    b1 = a.astype(BF16)
    r1 = a - b1.astype(F32)
    b2 = r1.astype(BF16)
    b3 = (r1 - b2.astype(F32)).astype(BF16)
    return b1, b2, b3


def _pack_weight_shard(w_in_l, w_out_l, w_gate_l, w_proj_l, conv_w_l):
    w_in_t = jnp.transpose(w_in_l).astype(BF16)
    proj_t = jnp.transpose(w_proj_l).astype(BF16)
    proj_rows = proj_t.reshape(D_MODEL // PLE_DIM, ROWS_PROJ, PLE_DIM).transpose(1, 0, 2).reshape(ROWS_PROJ, D_MODEL)
    conv_parts = jnp.concatenate([b.reshape(-1) for b in _split3_bf16(conv_w_l)])
    conv_rows = jnp.concatenate([conv_parts, jnp.zeros((ROWS_CONV * D_MODEL - conv_parts.shape[0],), BF16)])
    return jnp.concatenate(
        [w_in_t, w_out_l.astype(BF16), w_gate_l.astype(BF16), proj_rows, conv_rows.reshape(ROWS_CONV, D_MODEL)], axis=0
    )


def _unpack_conv(wg):
    per_dev = wg.reshape(N_DEV, ROWS_LAYER, D_MODEL)
    n_conv = (WIDTH_B // N_DEV) * 3
    conv_parts = per_dev[:, OFF_CONV].astype(F32)[:, :3 * n_conv].reshape(N_DEV, 3, n_conv)
    conv = (conv_parts[:, 0] + conv_parts[:, 1]) + conv_parts[:, 2]
    conv_k = jnp.transpose(conv.reshape(WIDTH_B, 3))
    conv_k = jnp.concatenate([conv_k, jnp.zeros((5, WIDTH_B), F32)], axis=0)
    return conv_k


def _unpack_grad_proj(red):
    proj_rows = red[OFF_PROJ:OFF_PROJ + ROWS_PROJ]
    proj_t = proj_rows.reshape(ROWS_PROJ, D_MODEL // PLE_DIM, PLE_DIM).transpose(1, 0, 2).reshape(ROWS_OUT, PLE_DIM)
    return jnp.transpose(proj_t)


def kernel(x, p, norm_g, w_in, ln_v_g, ln_v_b, w_s, b_s, conv_w, w_out, ple_norm_g, w_ple_gate, w_ple_proj, final_g, loss_target, m_norm_g, m_w_in, m_ln_v_g, m_ln_v_b, m_w_s, m_b_s, m_conv_w, m_w_out, m_ple_norm_g, m_w_ple_gate, m_w_ple_proj, m_final_g, v_norm_g, v_w_in, v_ln_v_g, v_ln_v_b, v_w_s, v_b_s, v_conv_w, v_w_out, v_ple_norm_g, v_w_ple_gate, v_w_ple_proj, v_final_g):
    me = 4 * lax.axis_index("x") + 2 * lax.axis_index("y") + lax.axis_index("c")
    xs = x[0]
    target = loss_target[0]

    shards = [_pack_weight_shard(w_in[l], w_out[l], w_ple_gate[l], w_ple_proj[l], conv_w[l]) for l in range(DEPTH)]
    tril = jnp.tril(jnp.ones((CHUNK, CHUNK), F32))

    def consts(l, wg_l):
        conv_k = _unpack_conv(wg_l)
        w_mix = w_s[l] * tril[None]
        small = dict(
            conv_k=conv_k,
            norm_g=norm_g[l].reshape(1, D_MODEL), ln_g=ln_v_g[l].reshape(1, WIDTH_A), ln_b=ln_v_b[l].reshape(1, WIDTH_A),
            w_mix=w_mix.astype(BF16), w_mix_t=jnp.swapaxes(w_mix, 1, 2).astype(BF16),
            b_mix=jnp.broadcast_to(b_s[l][:, :, None], (HEADS_A, CHUNK, HEAD_DIM)),
            ple_g=ple_norm_g[l].reshape(1, D_MODEL),
        )
        return dict({k: pltpu.with_memory_space_constraint(a, pltpu.HBM) for k, a in small.items()}, wg=wg_l)

    layer_consts = [consts(0, _all_gather_rows(shards[0]))]
    saved = []
    h = xs
    for l in range(DEPTH):
        k = layer_consts[l]
        outs = _forward_layer(
            l, h, p, k["wg"], k["conv_k"], k["norm_g"], k["ln_g"], k["ln_b"], k["w_mix"], k["b_mix"],
            k["ple_g"], next_shard=shards[l + 1] if l + 1 < DEPTH else None)
        proj, hn, cat, r, gpre, x1, x2 = outs[:7]
        if l + 1 < DEPTH:
            layer_consts.append(consts(l + 1, outs[7]))
        saved.append(dict(x_in=h, proj=proj, hn=hn, cat=cat, r=r, gpre=gpre, x1=x1))
        h = x2

    smalls, dws = [None] * DEPTH, [None] * DEPTH
    reduced = [None] * DEPTH
    pending = None
    dx = h
    for l in reversed(range(DEPTH)):
        k, s = layer_consts[l], saved[l]
        outs = _backward_layer(
            l, dx, s["x_in"], s["x1"], s["proj"], s["gpre"], s["cat"], s["r"], p, k["wg"], k["conv_k"],
            k["norm_g"], k["ln_g"], k["ln_b"], k["w_mix"], k["w_mix_t"], k["b_mix"], k["ple_g"],
            loss_head=(target, final_g.reshape(1, D_MODEL)) if l == DEPTH - 1 else None)
        dx, dproj, smalls[l], dws[l], side_pack = outs[:5]
        if l == DEPTH - 1:
            head = outs[5]
        outs = _weight_grads(l, dproj, s["hn"], side_pack, scatter_pack=pending)
        if pending is not None:
            reduced[l + 1] = _sum_pieces(l + 1, outs[1])
        pending = outs[0]
    reduced[0], total = _reduce_scatter_all_reduce(0, pending, smalls, head, dws)
    grad_x = dx[None]
    loss = total[TOTAL_HEAD + HEAD_LOSS, 0]

    n_ch = WIDTH_B // N_DEV
    g_conv = jnp.stack([total[l * SMALL_ROWS + SMALL_CONV:l * SMALL_ROWS + SMALL_CONV + 3, 0:WIDTH_B] for l in range(DEPTH)], axis=1)
    g_conv = lax.dynamic_slice_in_dim(g_conv, me * n_ch, n_ch, axis=2)
    g_proj = jnp.stack([_unpack_grad_proj(reduced[l]) for l in range(DEPTH)])

    def t_in(a):
        return jnp.swapaxes(a, 1, 2)

    def t_conv(a):
        return jnp.transpose(a, (2, 0, 1))

    (r_in,) = _adamw_rows("adamw_w_in", reduced, [OFF_IN], [(t_in(w_in), t_in(m_w_in), t_in(v_w_in))])
    r_out, r_gate = _adamw_rows(
        "adamw_w_out_gate", reduced, [OFF_OUT, OFF_GATE],
        [(w_out, m_w_out, v_w_out), (w_ple_gate, m_w_ple_gate, v_w_ple_gate)])
    small = _adamw_small(total, g_conv, g_proj, dict(
        norm_g=(norm_g, m_norm_g, v_norm_g), ple_norm_g=(ple_norm_g, m_ple_norm_g, v_ple_norm_g),
        ln_v_g=(ln_v_g, m_ln_v_g, v_ln_v_g), ln_v_b=(ln_v_b, m_ln_v_b, v_ln_v_b),
        b_s=(b_s, m_b_s, v_b_s), w_s=(w_s, m_w_s, v_w_s),
        final_g=tuple(a.reshape(1, D_MODEL) for a in (final_g, m_final_g, v_final_g)),
        conv_w=(t_conv(conv_w), t_conv(m_conv_w), t_conv(v_conv_w)),
        w_ple_proj=(w_ple_proj, m_w_ple_proj, v_w_ple_proj),
    ))
    res = dict(small, w_in=tuple(t_in(a) for a in r_in), w_out=r_out, w_ple_gate=r_gate)
    res["final_g"] = tuple(a.reshape(D_MODEL) for a in res["final_g"])
    res["conv_w"] = tuple(jnp.transpose(a, (1, 2, 0)) for a in res["conv_w"])
    order = ["norm_g", "w_in", "ln_v_g", "ln_v_b", "w_s", "b_s", "conv_w", "w_out", "ple_norm_g", "w_ple_gate", "w_ple_proj", "final_g"]
    return (loss, grad_x, *[res[n][0] for n in order], *[res[n][1] for n in order],
            *[res[n][2] for n in order], *[res[n][3] for n in order])
```

```python
import jax
import jax.numpy as jnp
from jax import lax
from jax.experimental import pallas as pl
from jax.experimental.pallas import tpu as pltpu

F32 = jnp.float32
BF16 = jnp.bfloat16

D_MODEL = 1024
WIDTH_A = 512
WIDTH_B = 512
HEADS_A = 4
HEAD_DIM = 128
CHUNK = 128
PLE_DIM = 256
PROJ_WIDTH = 3584
DEPTH = 2
EPS = 1e-6
N_DEV = 8

ADAM_LR = 0.001
ADAM_B1 = 0.9
ADAM_B2 = 0.999
ADAM_EPS = 1e-08
ADAM_WD = 0.01
ADAM_STEP = 10

ROWS_IN = PROJ_WIDTH // N_DEV
ROWS_OUT = D_MODEL // N_DEV
ROWS_GATE = D_MODEL // N_DEV
ROWS_PROJ = (D_MODEL // N_DEV) * PLE_DIM // D_MODEL
ROWS_CONV = 16
OFF_IN = 0
OFF_OUT = OFF_IN + ROWS_IN
OFF_GATE = OFF_OUT + ROWS_OUT
OFF_PROJ = OFF_GATE + ROWS_GATE
OFF_CONV = OFF_PROJ + ROWS_PROJ
ROWS_GRAD = OFF_CONV
ROWS_LAYER = OFF_CONV + ROWS_CONV

SMALL_ROWS = 8
SMALL_NORM = 0
SMALL_PLE = 1
SMALL_LN = 2
SMALL_BS = 3
SMALL_CONV = 4
HEAD_FINAL = 0
HEAD_LOSS = 1
TOTAL_HEAD = DEPTH * SMALL_ROWS
TOTAL_WS = TOTAL_HEAD + SMALL_ROWS
TOTAL_ROWS = TOTAL_WS + CHUNK

MIB = 1024 * 1024
MESH = pl.DeviceIdType.MESH

NT_DIMS = (((1,), (1,)), ((), ()))
TN_DIMS = (((0,), (0,)), ((), ()))


def _dot(a, b):
    return jnp.dot(a, b, preferred_element_type=F32)


def _dot_nt(a, b):
    return lax.dot_general(a, b, NT_DIMS, preferred_element_type=F32)


def _dot_tn(a, b):
    return lax.dot_general(a, b, TN_DIMS, preferred_element_type=F32)


def _colsum8(a):
    rows, n = a.shape
    return jnp.sum(a.reshape(rows // 8, 8, n), axis=0)


def _sigmoid(z):
    return 1.0 / (1.0 + jnp.exp(-z))


def _tile(t, want):
    return want if t % want == 0 else t


class _TwoLevelGather:
    def __init__(self, x_ref, out_ref, m_per, send_sems, recv_sems, local_sem):
        x, y, c = lax.axis_index("x"), lax.axis_index("y"), lax.axis_index("c")
        self.c = c
        self.me, self.sibling = (x, y, c), (x, y, 1 - c)
        self.chips = [(1 - x, y), (x, 1 - y), (1 - x, 1 - y)]
        self.x_ref, self.out_ref, self.m_per = x_ref, out_ref, m_per
        self.send_sems, self.recv_sems = send_sems, recv_sems
        self.mine = pltpu.make_async_copy(x_ref, self.rows(*self.me), local_sem)

    def rows(self, px, py, pc):
        return self.out_ref.at[pl.ds((4 * px + 2 * py + pc) * self.m_per, self.m_per), :]

    def copy(self, k, block, to, src=None):
        return pltpu.make_async_remote_copy(
            src_ref=self.rows(*block) if src is None else src,
            dst_ref=self.rows(*block),
            send_sem=self.send_sems.at[k],
            recv_sem=self.recv_sems.at[k],
            device_id=to,
            device_id_type=MESH,
        )

    def first(self):
        out = [self.copy(0, self.me, self.sibling, src=self.x_ref)]
        return out + [self.copy(1 + j, self.me, (*chip, self.c), src=self.x_ref) for j, chip in enumerate(self.chips)]

    def passed(self):
        return [self.copy(4 + j, (*chip, self.c), self.sibling) for j, chip in enumerate(self.chips)]

    def start(self):
        self.mine.start()
        for cp in self.first():
            cp.start()

    def pass_on(self):
        passed = self.passed()
        for j, chip in enumerate(self.chips):
            self.copy(1 + j, (*chip, self.c), self.me).wait_recv()
            passed[j].start()

    def finish(self):
        self.copy(0, self.sibling, self.me).wait_recv()
        for j, chip in enumerate(self.chips):
            self.copy(4 + j, (*chip, 1 - self.c), self.me).wait_recv()
        for cp in self.first() + self.passed():
            cp.wait_send()
        self.mine.wait()


GATHER_SEMS = [pltpu.SemaphoreType.DMA((7,)), pltpu.SemaphoreType.DMA((7,)), pltpu.SemaphoreType.DMA]


def _all_gather_rows(shard):
    m_per, n = shard.shape

    def body(x_ref, out_ref, send_sems, recv_sems, local_sem):
        ag = _TwoLevelGather(x_ref, out_ref, m_per, send_sems, recv_sems, local_sem)
        ag.start()
        ag.pass_on()
        ag.finish()

    return pl.pallas_call(
        body,
        name="weights_all_gather",
        out_shape=pltpu.HBM((N_DEV * m_per, n), shard.dtype),
        in_specs=[pl.BlockSpec(memory_space=pltpu.HBM)],
        out_specs=pl.BlockSpec(memory_space=pltpu.HBM),
        scratch_shapes=list(GATHER_SEMS),
    )(pltpu.with_memory_space_constraint(shard, pltpu.HBM))


PROJ_PARTS = D_MODEL // PLE_DIM
N_WEIGHT_COPIES = N_DEV * (3 + PROJ_PARTS)


def _weight_copies(wg_ref, w_in_t, w_out, w_gate, w_proj_t, sems):
    copies = []
    for s in range(N_DEV):
        base = s * ROWS_LAYER
        for dst, off, rows in ((w_in_t, OFF_IN, ROWS_IN), (w_out, OFF_OUT, ROWS_OUT), (w_gate, OFF_GATE, ROWS_GATE)):
            copies.append((wg_ref.at[pl.ds(base + off, rows), :], dst.at[pl.ds(s * rows, rows), :]))
        for j in range(PROJ_PARTS):
            copies.append((
                wg_ref.at[pl.ds(base + OFF_PROJ, ROWS_PROJ), pl.ds(j * PLE_DIM, PLE_DIM)],
                w_proj_t.at[pl.ds(s * ROWS_OUT + j * ROWS_PROJ, ROWS_PROJ), :],
            ))
    return [pltpu.make_async_copy(src, dst, sems.at[k]) for k, (src, dst) in enumerate(copies)]


def _forward_layer(layer, x, p_all, wg, conv_k, norm_g, ln_g, ln_b, w_mix, b_mix, ple_g, next_shard=None):
    t = x.shape[0]
    tm = _tile(t, 512)
    nt = t // tm
    gathers = next_shard is not None

    def body(*refs):
        (x_ref, p_ref, wg_ref, cw_ref, ng_ref, lng_ref, lnb_ref, wm_ref, bm_ref, pg_ref) = refs[:10]
        refs = refs[10:]
        if gathers:
            shard_ref, refs = refs[0], refs[1:]
        (proj_ref, hn_ref, cat_ref, r_ref, gpre_ref, x1_ref, x2_ref) = refs[:7]
        refs = refs[7:]
        if gathers:
            gathered_ref, refs = refs[0], refs[1:]
        (w_in_t, w_out, w_gate, wpt_ref, vln_s, mixed_s, halo_s, sems) = refs[:8]
        i = pl.program_id(0)
        if gathers:
            ag = _TwoLevelGather(shard_ref, gathered_ref, ROWS_LAYER, *refs[8:11])

            @pl.when(i == 0)
            def _():
                ag.start()

            @pl.when(i == nt // 2)
            def _():
                ag.pass_on()

        @pl.when(i == 0)
        def _():
            copies = _weight_copies(wg_ref, w_in_t, w_out, w_gate, wpt_ref, sems)
            for cp in copies:
                cp.start()
            halo_s[...] = jnp.zeros_like(halo_s)
            for cp in copies:
                cp.wait()

        xv = x_ref[...]
        rstd0 = lax.rsqrt(jnp.mean(xv * xv, axis=-1, keepdims=True) + EPS)
        hn_ref[...] = (xv * rstd0 * ng_ref[...]).astype(BF16)

        def proj_section(k):
            sec = _dot_nt(hn_ref[...], w_in_t[pl.ds(k * 512, 512), :])
            proj_ref[:, k * 512:(k + 1) * 512] = sec.astype(BF16)
            return sec

        v = proj_section(1)
        mu = jnp.mean(v, axis=-1, keepdims=True)
        vc = v - mu
        var = jnp.mean(vc * vc, axis=-1, keepdims=True)
        vln = vc * lax.rsqrt(var + EPS) * lng_ref[...] + lnb_ref[...]
        vln_s[...] = vln.astype(BF16)
        for ci in range(tm // CHUNK):
            rows = pl.ds(ci * CHUNK, CHUNK)
            for h in range(HEADS_A):
                cols = pl.ds(h * HEAD_DIM, HEAD_DIM)
                mixed_s[rows, cols] = _dot(wm_ref[h], vln_s[rows, cols]) + bm_ref[h]
        u = proj_section(0)
        za = proj_section(2)
        out_a = u * mixed_s[...] * (za * _sigmoid(za))
        cat_ref[:, 0:512] = out_a.astype(BF16)

        xc = proj_section(5) * proj_section(3)
        prev = halo_s[...]
        row = lax.broadcasted_iota(jnp.int32, (tm, WIDTH_B), 0)
        xc_m1 = jnp.where(row == 0, prev[7:8, :], pltpu.roll(xc, 1, 0))
        xc_m2 = jnp.where(row == 0, prev[6:7, :], jnp.where(row == 1, prev[7:8, :], pltpu.roll(xc, 2, 0)))
        halo_s[...] = xc[tm - 8:tm, :]
        cw = cw_ref[...]
        yc = cw[0:1, :] * xc_m2 + cw[1:2, :] * xc_m1 + cw[2:3, :] * xc
        zb = proj_section(6)
        out_b = proj_section(4) * yc * (zb * _sigmoid(zb))
        cat_ref[:, 512:1024] = out_b.astype(BF16)

        x1 = xv + _dot(cat_ref[...], w_out[...])
        x1_ref[...] = x1
        rstd1 = lax.rsqrt(jnp.mean(x1 * x1, axis=-1, keepdims=True) + EPS)
        r_ref[...] = (x1 * rstd1 * pg_ref[...]).astype(BF16)
        gpre = _dot(r_ref[...], w_gate[...])
        gpre_ref[...] = gpre.astype(BF16)
        pp = _dot_nt(p_ref[...].astype(BF16), wpt_ref[...])
        x2_ref[...] = x1 + _sigmoid(gpre) * pp

        if gathers:
            @pl.when(i == nt - 1)
            def _():
                ag.finish()

    def tok(width):
        return pl.BlockSpec((tm, width), lambda i: (i, 0))

    def whole(shape):
        return pl.BlockSpec(shape, lambda i: (0,) * len(shape))

    hbm = pl.BlockSpec(memory_space=pl.ANY)
    operands = [x, p_all, wg, conv_k, norm_g, ln_g, ln_b, w_mix, b_mix, ple_g]
    in_specs = [
        tok(D_MODEL), pl.BlockSpec((None, None, tm, PLE_DIM), lambda i: (layer, 0, i, 0)), hbm,
        whole((8, WIDTH_B)), whole((1, D_MODEL)), whole((1, WIDTH_A)), whole((1, WIDTH_A)),
        whole((HEADS_A, CHUNK, CHUNK)), whole((HEADS_A, CHUNK, HEAD_DIM)), whole((1, D_MODEL)),
    ]
    out_specs = [tok(PROJ_WIDTH), tok(D_MODEL), tok(D_MODEL), tok(D_MODEL), tok(D_MODEL), tok(D_MODEL), tok(D_MODEL)]
    out_shape = [
        jax.ShapeDtypeStruct((t, PROJ_WIDTH), BF16),
        jax.ShapeDtypeStruct((t, D_MODEL), BF16),
        jax.ShapeDtypeStruct((t, D_MODEL), BF16),
        jax.ShapeDtypeStruct((t, D_MODEL), BF16),
        jax.ShapeDtypeStruct((t, D_MODEL), BF16),
        jax.ShapeDtypeStruct((t, D_MODEL), F32),
        jax.ShapeDtypeStruct((t, D_MODEL), F32),
    ]
    scratch_shapes = [
        pltpu.VMEM((PROJ_WIDTH, D_MODEL), BF16),
        pltpu.VMEM((D_MODEL, D_MODEL), BF16),
        pltpu.VMEM((D_MODEL, D_MODEL), BF16),
        pltpu.VMEM((D_MODEL, PLE_DIM), BF16),
        pltpu.VMEM((tm, WIDTH_A), BF16),
        pltpu.VMEM((tm, WIDTH_A), F32),
        pltpu.VMEM((8, WIDTH_B), F32),
        pltpu.SemaphoreType.DMA((N_WEIGHT_COPIES,)),
    ]
    if gathers:
        operands.append(pltpu.with_memory_space_constraint(next_shard, pltpu.HBM))
        in_specs.append(pl.BlockSpec(memory_space=pltpu.HBM))
        out_specs.append(pl.BlockSpec(memory_space=pltpu.HBM))
        out_shape.append(pltpu.HBM((N_DEV * ROWS_LAYER, D_MODEL), BF16))
        scratch_shapes += list(GATHER_SEMS)

    return pl.pallas_call(
        body,
        name=f"layer{layer}_forward",
        grid=(nt,),
        in_specs=in_specs,
        out_specs=out_specs,
        out_shape=out_shape,
        scratch_shapes=scratch_shapes,
        compiler_params=pltpu.CompilerParams(dimension_semantics=("arbitrary",), vmem_limit_bytes=56 * MIB),
    )(*operands)


class _DirectScatter:
    def __init__(self, pack_ref, pieces_ref, send_sems, recv_sems, local_sem):
        x, y, c = lax.axis_index("x"), lax.axis_index("y"), lax.axis_index("c")
        me = 4 * x + 2 * y + c
        self.copies = []
        for k in range(N_DEV - 1):
            fx, fy, fc = ((k + 1) >> 2) & 1, ((k + 1) >> 1) & 1, (k + 1) & 1
            tx, ty, tc = x ^ fx, y ^ fy, c ^ fc
            self.copies.append(
                pltpu.make_async_remote_copy(
                    src_ref=pack_ref.at[4 * tx + 2 * ty + tc], dst_ref=pieces_ref.at[me],
                    send_sem=send_sems.at[k], recv_sem=recv_sems.at[k],
                    device_id=(tx, ty, tc), device_id_type=MESH,
                )
            )
        self.mine = pltpu.make_async_copy(pack_ref.at[me], pieces_ref.at[me], local_sem)

    def start(self):
        self.mine.start()
        for cp in self.copies:
            cp.start()

    def finish(self):
        for cp in self.copies:
            cp.wait_recv()
        for cp in self.copies:
            cp.wait_send()
        self.mine.wait()


SCATTER_SEMS = [pltpu.SemaphoreType.DMA((N_DEV - 1,)), pltpu.SemaphoreType.DMA((N_DEV - 1,)), pltpu.SemaphoreType.DMA]


def _backward_layer(layer, dx2, x_in, x1, proj, gpre, p_all, wg, conv_k, norm_g, ln_g, ln_b,
                    w_mix, w_mix_t, b_mix, ple_g, loss_head=None):
    t = x_in.shape[0]
    tm = _tile(t, 256)
    nt = t // tm
    n_chunks = tm // CHUNK
    halo_rows = 16
    heads = loss_head is not None

    def body(*refs):
        (dx2_ref, xin_ref, x1_ref, proj_ref, halo_ref, gpre_ref, p_ref, wg_ref, cw_ref,
         ng_ref, lng_ref, lnb_ref, wm_ref, wmt_ref, bm_ref, pg_ref) = refs[:16]
        refs = refs[16:]
        if heads:
            tgt_ref, fg_ref = refs[:2]
            refs = refs[2:]
        (dxin_ref, dproj_ref, dx1_ref, dgpre_ref, dpp_ref, small_ref, dws_ref) = refs[:7]
        refs = refs[7:]
        if heads:
            head_ref, refs = refs[0], refs[1:]
        (w_in_t, w_out, w_gate, wpt_ref, vln_s, mixed_s, dmix_s, dvln_s, carry_s,
         ng_acc, pg_acc, lng_acc, lnb_acc, cw_acc, dbm_ref, sems) = refs[:16]
        if heads:
            loss_acc, fg_acc = refs[16:18]
        i = pl.program_id(0)
        tile = nt - 1 - i

        @pl.when(i == 0)
        def _():
            copies = _weight_copies(wg_ref, w_in_t, w_out, w_gate, wpt_ref, sems)
            for cp in copies:
                cp.start()
            if heads:
                loss_acc[...] = jnp.zeros_like(loss_acc)
                fg_acc[...] = jnp.zeros_like(fg_acc)
            carry_s[...] = jnp.zeros_like(carry_s)
            ng_acc[...] = jnp.zeros_like(ng_acc)
            pg_acc[...] = jnp.zeros_like(pg_acc)
            lng_acc[...] = jnp.zeros_like(lng_acc)
            lnb_acc[...] = jnp.zeros_like(lnb_acc)
            cw_acc[...] = jnp.zeros_like(cw_acc)
            dws_ref[...] = jnp.zeros_like(dws_ref)
            dbm_ref[...] = jnp.zeros_like(dbm_ref)
            for cp in copies:
                cp.wait()

        if heads:
            x2v = dx2_ref[...]
            fg = fg_ref[...]
            rstdf = lax.rsqrt(jnp.mean(x2v * x2v, axis=-1, keepdims=True) + EPS)
            xhatf = x2v * rstdf
            err = xhatf * fg - tgt_ref[...]
            loss_acc[...] += _colsum8(err * err)
            dy = err * (1.0 / D_MODEL)
            fg_acc[...] += _colsum8(dy * xhatf)
            dxhf = dy * fg
            dx2v = rstdf * (dxhf - xhatf * jnp.mean(dxhf * xhatf, axis=-1, keepdims=True))
        else:
            dx2v = dx2_ref[...]

        gate = _sigmoid(gpre_ref[...].astype(F32))
        pp = _dot_nt(p_ref[...].astype(BF16), wpt_ref[...])
        dpp_ref[...] = (dx2v * gate).astype(BF16)
        dgpre = (dx2v * pp * gate * (1.0 - gate)).astype(BF16)
        dgpre_ref[...] = dgpre
        dr = _dot_nt(dgpre, w_gate[...])
        x1v = x1_ref[...]
        rstd1 = lax.rsqrt(jnp.mean(x1v * x1v, axis=-1, keepdims=True) + EPS)
        xhat1 = x1v * rstd1
        pg_acc[...] += _colsum8(dr * xhat1)
        dxh = dr * pg_ref[...]
        dx1 = dx2v + rstd1 * (dxh - xhat1 * jnp.mean(dxh * xhat1, axis=-1, keepdims=True))
        dx1b = dx1.astype(BF16)
        dx1_ref[...] = dx1b

        dcat = _dot_nt(dx1b, w_out[...])
        dca = dcat[:, 0:512]
        dcb = dcat[:, 512:1024]

        u = proj_ref[:, 0:512]
        v = proj_ref[:, 512:1024].astype(F32)
        za = proj_ref[:, 1024:1536]
        mu = jnp.mean(v, axis=-1, keepdims=True)
        vc = v - mu
        var = jnp.mean(vc * vc, axis=-1, keepdims=True)
        rs = lax.rsqrt(var + EPS)
        vhat = vc * rs
        lng = lng_ref[...]
        vln_s[...] = (vhat * lng + lnb_ref[...]).astype(BF16)
        for ci in range(n_chunks):
            rows = pl.ds(ci * CHUNK, CHUNK)
            for h in range(HEADS_A):
                cols = pl.ds(h * HEAD_DIM, HEAD_DIM)
                mixed_s[rows, cols] = (_dot(wm_ref[h], vln_s[rows, cols]) + bm_ref[h]).astype(BF16)
        mixed = mixed_s[...]
        sga = _sigmoid(za)
        sa = za * sga
        dsa = sga + sa * (1.0 - sga)

        def put_section(k, val):
            dproj_ref[:, k * 512:(k + 1) * 512] = val.astype(BF16)

        dcab = dca.astype(BF16)
        dca_sa = dcab * sa
        put_section(0, dca_sa * mixed)
        dmix_s[...] = dca_sa * u
        put_section(2, (dcab * dsa) * (u * mixed))
        dbm_acc = jnp.zeros((CHUNK, WIDTH_A), F32)
        for ci in range(n_chunks):
            rows = pl.ds(ci * CHUNK, CHUNK)
            dbm_acc = dbm_acc + dmix_s[rows, :].astype(F32)
            for h in range(HEADS_A):
                cols = pl.ds(h * HEAD_DIM, HEAD_DIM)
                dvln_s[rows, cols] = _dot(wmt_ref[h], dmix_s[rows, cols])
                dws_ref[:, cols] += _dot_nt(dmix_s[rows, cols], vln_s[rows, cols])
        dbm_ref[...] += dbm_acc
        dvln = dvln_s[...]
        lng_acc[...] += _colsum8(dvln * vhat)
        lnb_acc[...] += _colsum8(dvln)
        dvh = dvln * lng
        dv = rs * (dvh - jnp.mean(dvh, axis=-1, keepdims=True) - vhat * jnp.mean(dvh * vhat, axis=-1, keepdims=True))
        put_section(1, dv)

        hb = proj_ref[:, 1536:2048].astype(F32)
        gb = proj_ref[:, 2048:2560]
        gc = proj_ref[:, 2560:3072].astype(F32)
        zb = proj_ref[:, 3072:3584]
        xc = gc * hb
        prev = halo_ref[:, 2560:3072].astype(F32) * halo_ref[:, 1536:2048].astype(F32)
        prev = jnp.where(tile > 0, prev, 0.0)
        row = lax.broadcasted_iota(jnp.int32, (tm, WIDTH_B), 0)
        p1 = prev[halo_rows - 1:halo_rows, :]
        p2 = prev[halo_rows - 2:halo_rows - 1, :]
        xc_m1 = jnp.where(row == 0, p1, pltpu.roll(xc, 1, 0))
        xc_m2 = jnp.where(row == 0, p2, jnp.where(row == 1, p1, pltpu.roll(xc, 2, 0)))
        cw = cw_ref[...]
        yc = cw[0:1, :] * xc_m2 + cw[1:2, :] * xc_m1 + cw[2:3, :] * xc
        sgb = _sigmoid(zb)
        sb = zb * sgb
        dsb = sgb + sb * (1.0 - sgb)
        dcbb = dcb.astype(BF16)
        ycb = yc.astype(BF16)
        dcb_sb = dcbb * sb
        put_section(4, dcb_sb * ycb)
        dyc = (dcb_sb * gb).astype(F32)
        put_section(6, (dcbb * dsb) * (gb * ycb))
        nxt = carry_s[...]
        dyc_p1 = jnp.where(row == tm - 1, nxt[0:1, :], pltpu.roll(dyc, tm - 1, 0))
        dyc_p2 = jnp.where(row == tm - 1, nxt[1:2, :], jnp.where(row == tm - 2, nxt[0:1, :], pltpu.roll(dyc, tm - 2, 0)))
        carry_s[...] = dyc[0:8, :]
        dxc = cw[2:3, :] * dyc + cw[1:2, :] * dyc_p1 + cw[0:1, :] * dyc_p2
        cw_acc[0] += _colsum8(dyc * xc_m2)
        cw_acc[1] += _colsum8(dyc * xc_m1)
        cw_acc[2] += _colsum8(dyc * xc)
        put_section(3, dxc * gc)
        put_section(5, dxc * hb)

        dhn = _dot(dproj_ref[...], w_in_t[...])
        xv = xin_ref[...]
        rstd0 = lax.rsqrt(jnp.mean(xv * xv, axis=-1, keepdims=True) + EPS)
        xhat0 = xv * rstd0
        ng_acc[...] += _colsum8(dhn * xhat0)
        dxh0 = dhn * ng_ref[...]
        dxin_ref[...] = dx1 + rstd0 * (dxh0 - xhat0 * jnp.mean(dxh0 * xhat0, axis=-1, keepdims=True))

        @pl.when(i == nt - 1)
        def _():
            small_ref[...] = jnp.zeros_like(small_ref)
            small_ref[SMALL_NORM:SMALL_NORM + 1, :] = jnp.sum(ng_acc[...], axis=0, keepdims=True)
            small_ref[SMALL_PLE:SMALL_PLE + 1, :] = jnp.sum(pg_acc[...], axis=0, keepdims=True)
            small_ref[SMALL_LN:SMALL_LN + 1, 0:WIDTH_A] = jnp.sum(lng_acc[...], axis=0, keepdims=True)
            small_ref[SMALL_LN:SMALL_LN + 1, WIDTH_A:2 * WIDTH_A] = jnp.sum(lnb_acc[...], axis=0, keepdims=True)
            for h in range(HEADS_A):
                cols = pl.ds(h * HEAD_DIM, HEAD_DIM)
                small_ref[SMALL_BS:SMALL_BS + 1, cols] = jnp.sum(jnp.transpose(dbm_ref[:, cols]), axis=0, keepdims=True)
            for k in range(3):
                small_ref[SMALL_CONV + k:SMALL_CONV + k + 1, 0:WIDTH_B] = jnp.sum(cw_acc[k], axis=0, keepdims=True)
            if heads:
                total = jnp.sum(loss_acc[...]) * (0.5 / D_MODEL)
                rows8 = lax.broadcasted_iota(jnp.int32, (SMALL_ROWS, D_MODEL), 0)
                lanes8 = lax.broadcasted_iota(jnp.int32, (SMALL_ROWS, D_MODEL), 1)
                head_ref[...] = jnp.where((rows8 == HEAD_LOSS) & (lanes8 == 0), total, 0.0)
                head_ref[HEAD_FINAL:HEAD_FINAL + 1, :] = jnp.sum(fg_acc[...], axis=0, keepdims=True)

    def tok(width):
        return pl.BlockSpec((tm, width), lambda i: (nt - 1 - i, 0))

    def whole(shape):
        return pl.BlockSpec(shape, lambda i: (0,) * len(shape))

    halo_spec = pl.BlockSpec(
        (halo_rows, PROJ_WIDTH), lambda i: (jnp.maximum((nt - 1 - i) * (tm // halo_rows) - 1, 0), 0)
    )
    hbm = pl.BlockSpec(memory_space=pl.ANY)
    operands = [dx2, x_in, x1, proj, proj, gpre, p_all, wg, conv_k, norm_g, ln_g, ln_b, w_mix, w_mix_t, b_mix, ple_g]
    in_specs = [
        tok(D_MODEL), tok(D_MODEL), tok(D_MODEL), tok(PROJ_WIDTH), halo_spec, tok(D_MODEL),
        pl.BlockSpec((None, None, tm, PLE_DIM), lambda i: (layer, 0, nt - 1 - i, 0)), hbm,
        whole((8, WIDTH_B)), whole((1, D_MODEL)), whole((1, WIDTH_A)), whole((1, WIDTH_A)),
        whole((HEADS_A, CHUNK, CHUNK)), whole((HEADS_A, CHUNK, CHUNK)), whole((HEADS_A, CHUNK, HEAD_DIM)),
        whole((1, D_MODEL)),
    ]
    out_specs = [
        tok(D_MODEL), tok(PROJ_WIDTH), tok(D_MODEL), tok(D_MODEL), tok(D_MODEL),
        whole((SMALL_ROWS, D_MODEL)), whole((CHUNK, WIDTH_A)),
    ]
    out_shape = [
        jax.ShapeDtypeStruct((t, D_MODEL), F32),
        jax.ShapeDtypeStruct((t, PROJ_WIDTH), BF16),
        jax.ShapeDtypeStruct((t, D_MODEL), BF16),
        jax.ShapeDtypeStruct((t, D_MODEL), BF16),
        jax.ShapeDtypeStruct((t, D_MODEL), BF16),
        jax.ShapeDtypeStruct((SMALL_ROWS, D_MODEL), F32),
        jax.ShapeDtypeStruct((CHUNK, WIDTH_A), F32),
    ]
    scratch_shapes = [
        pltpu.VMEM((PROJ_WIDTH, D_MODEL), BF16),
        pltpu.VMEM((D_MODEL, D_MODEL), BF16),
        pltpu.VMEM((D_MODEL, D_MODEL), BF16),
        pltpu.VMEM((D_MODEL, PLE_DIM), BF16),
        pltpu.VMEM((tm, WIDTH_A), BF16),
        pltpu.VMEM((tm, WIDTH_A), BF16),
        pltpu.VMEM((tm, WIDTH_A), BF16),
        pltpu.VMEM((tm, WIDTH_A), F32),
        pltpu.VMEM((8, WIDTH_B), F32),
        pltpu.VMEM((8, D_MODEL), F32),
        pltpu.VMEM((8, D_MODEL), F32),
        pltpu.VMEM((8, WIDTH_A), F32),
        pltpu.VMEM((8, WIDTH_A), F32),
        pltpu.VMEM((3, 8, WIDTH_B), F32),
        pltpu.VMEM((CHUNK, WIDTH_A), F32),
        pltpu.SemaphoreType.DMA((N_WEIGHT_COPIES,)),
    ]
    if heads:
        operands += list(loss_head)
        in_specs += [tok(D_MODEL), whole((1, D_MODEL))]
        out_specs.append(whole((SMALL_ROWS, D_MODEL)))
        out_shape.append(jax.ShapeDtypeStruct((SMALL_ROWS, D_MODEL), F32))
        scratch_shapes += [pltpu.VMEM((8, D_MODEL), F32), pltpu.VMEM((8, D_MODEL), F32)]

    return pl.pallas_call(
        body,
        name=f"layer{layer}_backward",
        grid=(nt,),
        in_specs=in_specs,
        out_specs=out_specs,
        out_shape=out_shape,
        scratch_shapes=scratch_shapes,
        compiler_params=pltpu.CompilerParams(dimension_semantics=("arbitrary",), vmem_limit_bytes=56 * MIB),
    )(*operands)


def _sum_pieces(layer, pieces):
    rows, n = pieces.shape[1], pieces.shape[2]
    blocks = 2
    rb = rows // blocks

    def body(p_ref, out_ref):
        total = p_ref[0].astype(F32)
        for j in range(1, N_DEV):
            total = total + p_ref[j].astype(F32)
        out_ref[...] = total

    return pl.pallas_call(
        body,
        name=f"layer{layer}_grad_sum",
        grid=(blocks,),
        out_shape=pltpu.HBM((rows, n), F32),
        in_specs=[pl.BlockSpec((N_DEV, rb, n), lambda i: (0, i, 0))],
        out_specs=pl.BlockSpec((rb, n), lambda i: (i, 0)),
        compiler_params=pltpu.CompilerParams(dimension_semantics=("arbitrary",), vmem_limit_bytes=32 * MIB),
    )(pieces)


def _weight_grads(layer, dproj, hn, cat, dx1, r, dgpre, dpp, p_all, scatter_pack=None):
    t = hn.shape[0]
    tk = _tile(t, 512)
    nt = t // tk
    in_blocks = PROJ_WIDTH // 512
    scatters = scatter_pack is not None

    def body(*refs):
        (dproj_ref, hn_ref, cat_ref, dx1_ref, r_ref, dgpre_ref, dpp_ref, p_ref) = refs[:8]
        refs = refs[8:]
        if scatters:
            prior_ref, refs = refs[0], refs[1:]
        pack_ref, refs = refs[0], refs[1:]
        if scatters:
            pieces_ref, refs = refs[0], refs[1:]
        (acc_in, acc_out, acc_gate, acc_proj, stage, sems) = refs[:6]
        i = pl.program_id(0)
        if scatters:
            scatter = _DirectScatter(prior_ref, pieces_ref, *refs[6:9])

            @pl.when(i == 0)
            def _():
                scatter.start()

        @pl.when(i == 0)
        def _():
            acc_in[...] = jnp.zeros_like(acc_in)
            acc_out[...] = jnp.zeros_like(acc_out)
            acc_gate[...] = jnp.zeros_like(acc_gate)
            acc_proj[...] = jnp.zeros_like(acc_proj)

        hnv = hn_ref[...]
        for b in range(in_blocks):
            acc_in[pl.ds(b * 512, 512), :] += _dot_tn(dproj_ref[:, b * 512:(b + 1) * 512], hnv)
        dx1v = dx1_ref[...]
        dgv = dgpre_ref[...]
        for b in range(D_MODEL // 512):
            acc_out[pl.ds(b * 512, 512), :] += _dot_tn(cat_ref[:, b * 512:(b + 1) * 512], dx1v)
            acc_gate[pl.ds(b * 512, 512), :] += _dot_tn(r_ref[:, b * 512:(b + 1) * 512], dgv)
        pv = p_ref[...].astype(BF16)
        for b in range(D_MODEL // 512):
            acc_proj[pl.ds(b * 512, 512), :] += _dot_tn(dpp_ref[:, b * 512:(b + 1) * 512], pv)

        @pl.when(i == nt - 1)
        def _():
            def out_copy(s):
                return pltpu.make_async_copy(stage.at[s % 2], pack_ref.at[s], sems.at[s % 2])

            for s in range(N_DEV):
                if s >= 2:
                    out_copy(s - 2).wait()
                buf = stage.at[s % 2]
                buf[pl.ds(OFF_IN, ROWS_IN), :] = acc_in[pl.ds(s * ROWS_IN, ROWS_IN), :].astype(BF16)
                buf[pl.ds(OFF_OUT, ROWS_OUT), :] = acc_out[pl.ds(s * ROWS_OUT, ROWS_OUT), :].astype(BF16)
                buf[pl.ds(OFF_GATE, ROWS_GATE), :] = acc_gate[pl.ds(s * ROWS_GATE, ROWS_GATE), :].astype(BF16)
                for j in range(D_MODEL // PLE_DIM):
                    buf[pl.ds(OFF_PROJ, ROWS_PROJ), pl.ds(j * PLE_DIM, PLE_DIM)] = acc_proj[
                        pl.ds(s * ROWS_OUT + j * ROWS_PROJ, ROWS_PROJ), :
                    ].astype(BF16)
                out_copy(s).start()
            out_copy(N_DEV - 2).wait()
            out_copy(N_DEV - 1).wait()
            if scatters:
                scatter.finish()

    def tok(width):
        return pl.BlockSpec((tk, width), lambda i: (i, 0))

    hbm = pl.BlockSpec(memory_space=pl.ANY)
    pack_shape = jax.ShapeDtypeStruct((N_DEV, ROWS_GRAD, D_MODEL), BF16)
    operands = [dproj, hn, cat, dx1, r, dgpre, dpp, p_all]
    in_specs = [tok(PROJ_WIDTH), tok(D_MODEL), tok(D_MODEL), tok(D_MODEL), tok(D_MODEL), tok(D_MODEL), tok(D_MODEL),
                pl.BlockSpec((None, None, tk, PLE_DIM), lambda i: (layer, 0, i, 0))]
    out_specs, out_shape = [hbm], [pack_shape]
    scratch_shapes = [
        pltpu.VMEM((PROJ_WIDTH, D_MODEL), F32),
        pltpu.VMEM((D_MODEL, D_MODEL), F32),
        pltpu.VMEM((D_MODEL, D_MODEL), F32),
        pltpu.VMEM((D_MODEL, PLE_DIM), F32),
        pltpu.VMEM((2, ROWS_GRAD, D_MODEL), BF16),
        pltpu.SemaphoreType.DMA((2,)),
    ]
    if scatters:
        operands.append(scatter_pack)
        in_specs.append(hbm)
        out_specs.append(hbm)
        out_shape.append(pack_shape)
        scratch_shapes += list(SCATTER_SEMS)

    return pl.pallas_call(
        body,
        name=f"layer{layer}_weight_grads",
        grid=(nt,),
        in_specs=in_specs,
        out_specs=out_specs,
        out_shape=out_shape,
        scratch_shapes=scratch_shapes,
        compiler_params=pltpu.CompilerParams(dimension_semantics=("arbitrary",), vmem_limit_bytes=58 * MIB),
    )(*operands)


def _reduce_scatter_all_reduce(layer, pack, smalls, head, dws):
    rows, n = pack.shape[1], pack.shape[2]
    assert DEPTH * WIDTH_A == D_MODEL and n == D_MODEL

    def body(g_ref, *refs):
        small_refs, refs = refs[:DEPTH], refs[DEPTH:]
        head_ref, refs = refs[0], refs[1:]
        dws_refs, refs = refs[:DEPTH], refs[DEPTH:]
        (out_ref, total_ref, r1, a_s, r2, sp, sr1, sq, send1, recv1, send2, recv2, ssend, srecv) = refs
        x, y, c = lax.axis_index("x"), lax.axis_index("y"), lax.axis_index("c")
        sibling = (x, y, 1 - c)
        chip = 2 * x + y
        flips = [(1, 0), (0, 1), (1, 1)]

        for l in range(DEPTH):
            sp[l * SMALL_ROWS:(l + 1) * SMALL_ROWS, :] = small_refs[l][...]
            sp[TOTAL_WS:TOTAL_ROWS, l * WIDTH_A:(l + 1) * WIDTH_A] = dws_refs[l][...]
        sp[TOTAL_HEAD:TOTAL_WS, :] = head_ref[...]

        small_pair = pltpu.make_async_remote_copy(
            src_ref=sp, dst_ref=sr1, send_sem=ssend.at[0], recv_sem=srecv.at[0], device_id=sibling, device_id_type=MESH
        )

        def to_sibling(j):
            return pltpu.make_async_remote_copy(
                src_ref=g_ref.at[2 * j + 1 - c], dst_ref=r1.at[j], send_sem=send1.at[j], recv_sem=recv1.at[j],
                device_id=sibling, device_id_type=MESH,
            )

        first = [to_sibling(j) for j in range(4)]
        small_pair.start()
        for cp in first:
            cp.start()

        small_pair.wait_recv()
        sq[chip] = sp[...] + sr1[...]
        small_chips = [
            pltpu.make_async_remote_copy(
                src_ref=sq.at[chip], dst_ref=sq.at[chip], send_sem=ssend.at[1 + k], recv_sem=srecv.at[1 + k],
                device_id=(x ^ fx, y ^ fy, c), device_id_type=MESH,
            )
            for k, (fx, fy) in enumerate(flips)
        ]
        for cp in small_chips:
            cp.start()

        def to_chip(j):
            return pltpu.make_async_remote_copy(
                src_ref=a_s.at[j], dst_ref=r2.at[chip], send_sem=send2.at[j], recv_sem=recv2.at[chip],
                device_id=(j // 2, j % 2, c), device_id_type=MESH,
            )

        def from_chip(k):
            return pltpu.make_async_remote_copy(
                src_ref=a_s.at[k], dst_ref=r2.at[k], send_sem=send2.at[k], recv_sem=recv2.at[k],
                device_id=(k // 2, k % 2, c), device_id_type=MESH,
            )

        for j in range(4):
            first[j].wait_recv()

            @pl.when(chip != j)
            def _():
                a_s[j] = (g_ref[2 * j + c].astype(F32) + r1[j].astype(F32)).astype(BF16)
                to_chip(j).start()

        out_ref[...] = g_ref[2 * chip + c].astype(F32) + r1[chip].astype(F32)
        for cp in small_chips:
            cp.wait_recv()
        total_ref[...] = ((sq[0] + sq[1]) + sq[2]) + sq[3]
        for k in range(4):
            @pl.when(chip != k)
            def _():
                from_chip(k).wait_recv()
                out_ref[...] += r2[k].astype(F32)
        small_pair.wait_send()
        for cp in first + small_chips:
            cp.wait_send()
        for j in range(4):
            @pl.when(chip != j)
            def _():
                to_chip(j).wait_send()

    vmem = pl.BlockSpec(memory_space=pltpu.VMEM)
    return pl.pallas_call(
        body,
        name=f"layer{layer}_grad_reduce_scatter",
        out_shape=[jax.ShapeDtypeStruct((rows, n), F32), jax.ShapeDtypeStruct((TOTAL_ROWS, D_MODEL), F32)],
        in_specs=[vmem] * (2 + 2 * DEPTH),
        out_specs=[vmem, vmem],
        scratch_shapes=[
            pltpu.VMEM((4, rows, n), BF16),
            pltpu.VMEM((4, rows, n), BF16),
            pltpu.VMEM((4, rows, n), BF16),
            pltpu.VMEM((TOTAL_ROWS, D_MODEL), F32),
            pltpu.VMEM((TOTAL_ROWS, D_MODEL), F32),
            pltpu.VMEM((4, TOTAL_ROWS, D_MODEL), F32),
            pltpu.SemaphoreType.DMA((4,)),
            pltpu.SemaphoreType.DMA((4,)),
            pltpu.SemaphoreType.DMA((4,)),
            pltpu.SemaphoreType.DMA((4,)),
            pltpu.SemaphoreType.DMA((4,)),
            pltpu.SemaphoreType.DMA((4,)),
        ],
        compiler_params=pltpu.CompilerParams(vmem_limit_bytes=48 * MIB),
    )(pack, *smalls, head, *dws)


def _adam_step(w, g, m, v):
    m = ADAM_B1 * m + (1.0 - ADAM_B1) * g
    v = ADAM_B2 * v + (1.0 - ADAM_B2) * (g * g)
    m_hat = m / (1.0 - ADAM_B1 ** ADAM_STEP)
    v_hat = v / (1.0 - ADAM_B2 ** ADAM_STEP)
    return -ADAM_LR * (m_hat / (jnp.sqrt(v_hat) + ADAM_EPS) + ADAM_WD * w), m, v


def _adamw_rows(name, reduced, row_off, states):
    n = len(states)

    def body(*refs):
        red = refs[:DEPTH]
        ins = refs[DEPTH:DEPTH + 3 * n]
        outs = refs[DEPTH + 3 * n:]
        layer = pl.program_id(0)
        for l in range(DEPTH):
            @pl.when(layer == l)
            def _():
                for k in range(n):
                    w_ref, m_ref, v_ref = ins[3 * k:3 * k + 3]
                    g_ref, d_ref, nm_ref, nv_ref = outs[4 * k:4 * k + 4]
                    g = red[l][row_off[k]:row_off[k] + w_ref.shape[0], :]
                    d, m, v = _adam_step(w_ref[...], g, m_ref[...], v_ref[...])
                    g_ref[...] = g
                    d_ref[...] = d
                    nm_ref[...] = m
                    nv_ref[...] = v

    flat = [a for st in states for a in st]
    state_specs, out_specs, out_shape = [], [], []
    for w, _, _ in states:
        spec = pl.BlockSpec((None,) + w.shape[1:], lambda l: (l, 0, 0))
        state_specs += [spec] * 3
        out_specs += [spec] * 4
        out_shape += [jax.ShapeDtypeStruct(w.shape, F32)] * 4
    red_specs = [pl.BlockSpec(a.shape, lambda l: (0, 0)) for a in reduced]
    operands = [pltpu.with_memory_space_constraint(a, pltpu.HBM) for a in (*reduced, *flat)]
    outs = pl.pallas_call(
        body,
        name=name,
        grid=(DEPTH,),
        out_shape=[pltpu.HBM(a.shape, a.dtype) for a in out_shape],
        in_specs=red_specs + state_specs,
        out_specs=out_specs,
        compiler_params=pltpu.CompilerParams(dimension_semantics=("arbitrary",), vmem_limit_bytes=48 * MIB),
    )(*operands)
    return [tuple(outs[4 * k:4 * k + 4]) for k in range(n)]


def _adamw_small(total, g_conv, g_proj, st):
    names = ["norm_g", "ple_norm_g", "ln_v_g", "ln_v_b", "b_s", "w_s", "final_g", "conv_w", "w_ple_proj"]
    cut = names[:7]

    def body(total_ref, gconv_ref, gproj_ref, *refs):
        ins = {nm: refs[3 * k:3 * k + 3] for k, nm in enumerate(names)}
        outs, pos = {}, 3 * len(names)
        for nm in names:
            cnt = 4 if nm in cut else 3
            outs[nm] = refs[pos:pos + cnt]
            pos += cnt

        def update(nm, idx, g):
            w_ref, m_ref, v_ref = ins[nm]
            d, m, v = _adam_step(w_ref[idx], g, m_ref[idx], v_ref[idx])
            o = outs[nm]
            if nm in cut:
                o[0][idx] = g
                o = o[1:]
            o[0][idx] = d
            o[1][idx] = m
            o[2][idx] = v

        tril = (lax.broadcasted_iota(jnp.int32, (CHUNK, CHUNK), 0) >= lax.broadcasted_iota(jnp.int32, (CHUNK, CHUNK), 1))
        for l in range(DEPTH):
            base = l * SMALL_ROWS
            row = (slice(l, l + 1), slice(None))
            update("norm_g", row, total_ref[base + SMALL_NORM:base + SMALL_NORM + 1, :])
            update("ple_norm_g", row, total_ref[base + SMALL_PLE:base + SMALL_PLE + 1, :])
            update("ln_v_g", row, total_ref[base + SMALL_LN:base + SMALL_LN + 1, 0:WIDTH_A])
            update("ln_v_b", row, total_ref[base + SMALL_LN:base + SMALL_LN + 1, WIDTH_A:2 * WIDTH_A])
            for h in range(HEADS_A):
                update("b_s", (l, slice(h, h + 1), slice(None)),
                       total_ref[base + SMALL_BS:base + SMALL_BS + 1, h * HEAD_DIM:(h + 1) * HEAD_DIM])
                lanes = slice(l * WIDTH_A + h * CHUNK, l * WIDTH_A + (h + 1) * CHUNK)
                update("w_s", (l, h), jnp.where(tril, total_ref[TOTAL_WS:TOTAL_ROWS, lanes], 0.0))
        update("final_g", (slice(None), slice(None)), total_ref[TOTAL_HEAD + HEAD_FINAL:TOTAL_HEAD + HEAD_FINAL + 1, :])
        update("conv_w", (slice(None),) * 3, gconv_ref[...])
        update("w_ple_proj", (slice(None),) * 3, gproj_ref[...])

    flat = [a for nm in names for a in st[nm]]
    out_shape = []
    for nm in names:
        out_shape += [jax.ShapeDtypeStruct(st[nm][0].shape, F32)] * (4 if nm in cut else 3)
    def whole(a):
        return pl.BlockSpec(a.shape, lambda i: (0,) * len(a.shape))

    operands = [pltpu.with_memory_space_constraint(a, pltpu.HBM) for a in (total, g_conv, g_proj, *flat)]
    outs = pl.pallas_call(
        body,
        name="adamw_small",
        grid=(1,),
        out_shape=[pltpu.HBM(a.shape, a.dtype) for a in out_shape],
        in_specs=[whole(a) for a in operands],
        out_specs=[whole(a) for a in out_shape],
        compiler_params=pltpu.CompilerParams(dimension_semantics=("arbitrary",), vmem_limit_bytes=32 * MIB),
    )(*operands)
    res, pos = {}, 0
    for nm in names:
        cnt = 4 if nm in cut else 3
        got = tuple(outs[pos:pos + cnt])
        res[nm] = got if nm in cut else ((g_conv if nm == "conv_w" else g_proj),) + got
        pos += cnt
    return res


def _split3_bf16(a):
    b1 = a.astype(BF16)
    r1 = a - b1.astype(F32)
    b2 = r1.astype(BF16)
    b3 = (r1 - b2.astype(F32)).astype(BF16)
    return b1, b2, b3


def _pack_weight_shard(w_in_l, w_out_l, w_gate_l, w_proj_l, conv_w_l):
    w_in_t = jnp.transpose(w_in_l).astype(BF16)
    proj_t = jnp.transpose(w_proj_l).astype(BF16)
    proj_rows = proj_t.reshape(D_MODEL // PLE_DIM, ROWS_PROJ, PLE_DIM).transpose(1, 0, 2).reshape(ROWS_PROJ, D_MODEL)
    conv_parts = jnp.concatenate([b.reshape(-1) for b in _split3_bf16(conv_w_l)])
    conv_rows = jnp.concatenate([conv_parts, jnp.zeros((ROWS_CONV * D_MODEL - conv_parts.shape[0],), BF16)])
    return jnp.concatenate(
        [w_in_t, w_out_l.astype(BF16), w_gate_l.astype(BF16), proj_rows, conv_rows.reshape(ROWS_CONV, D_MODEL)], axis=0
    )


def _unpack_conv(wg):
    per_dev = wg.reshape(N_DEV, ROWS_LAYER, D_MODEL)
    n_conv = (WIDTH_B // N_DEV) * 3
    conv_parts = per_dev[:, OFF_CONV].astype(F32)[:, :3 * n_conv].reshape(N_DEV, 3, n_conv)
    conv = (conv_parts[:, 0] + conv_parts[:, 1]) + conv_parts[:, 2]
    conv_k = jnp.transpose(conv.reshape(WIDTH_B, 3))
    conv_k = jnp.concatenate([conv_k, jnp.zeros((5, WIDTH_B), F32)], axis=0)
    return conv_k


def _unpack_grad_proj(red):
    proj_rows = red[OFF_PROJ:OFF_PROJ + ROWS_PROJ]
    proj_t = proj_rows.reshape(ROWS_PROJ, D_MODEL // PLE_DIM, PLE_DIM).transpose(1, 0, 2).reshape(ROWS_OUT, PLE_DIM)
    return jnp.transpose(proj_t)


def kernel(x, p, norm_g, w_in, ln_v_g, ln_v_b, w_s, b_s, conv_w, w_out, ple_norm_g, w_ple_gate, w_ple_proj, final_g, loss_target, m_norm_g, m_w_in, m_ln_v_g, m_ln_v_b, m_w_s, m_b_s, m_conv_w, m_w_out, m_ple_norm_g, m_w_ple_gate, m_w_ple_proj, m_final_g, v_norm_g, v_w_in, v_ln_v_g, v_ln_v_b, v_w_s, v_b_s, v_conv_w, v_w_out, v_ple_norm_g, v_w_ple_gate, v_w_ple_proj, v_final_g):
    me = 4 * lax.axis_index("x") + 2 * lax.axis_index("y") + lax.axis_index("c")
    xs = x[0]
    target = loss_target[0]

    shards = [_pack_weight_shard(w_in[l], w_out[l], w_ple_gate[l], w_ple_proj[l], conv_w[l]) for l in range(DEPTH)]
    tril = jnp.tril(jnp.ones((CHUNK, CHUNK), F32))

    def consts(l, wg_l):
        conv_k = _unpack_conv(wg_l)
        w_mix = w_s[l] * tril[None]
        small = dict(
            conv_k=conv_k,
            norm_g=norm_g[l].reshape(1, D_MODEL), ln_g=ln_v_g[l].reshape(1, WIDTH_A), ln_b=ln_v_b[l].reshape(1, WIDTH_A),
            w_mix=w_mix.astype(BF16), w_mix_t=jnp.swapaxes(w_mix, 1, 2).astype(BF16),
            b_mix=jnp.broadcast_to(b_s[l][:, :, None], (HEADS_A, CHUNK, HEAD_DIM)),
            ple_g=ple_norm_g[l].reshape(1, D_MODEL),
        )
        return dict({k: pltpu.with_memory_space_constraint(a, pltpu.HBM) for k, a in small.items()}, wg=wg_l)

    layer_consts = [consts(0, _all_gather_rows(shards[0]))]
    saved = []
    h = xs
    for l in range(DEPTH):
        k = layer_consts[l]
        outs = _forward_layer(
            l, h, p, k["wg"], k["conv_k"], k["norm_g"], k["ln_g"], k["ln_b"], k["w_mix"], k["b_mix"],
            k["ple_g"], next_shard=shards[l + 1] if l + 1 < DEPTH else None)
        proj, hn, cat, r, gpre, x1, x2 = outs[:7]
        if l + 1 < DEPTH:
            layer_consts.append(consts(l + 1, outs[7]))
        saved.append(dict(x_in=h, proj=proj, hn=hn, cat=cat, r=r, gpre=gpre, x1=x1))
        h = x2

    smalls, dws = [None] * DEPTH, [None] * DEPTH
    reduced = [None] * DEPTH
    pending = None
    dx = h
    for l in reversed(range(DEPTH)):
        k, s = layer_consts[l], saved[l]
        outs = _backward_layer(
            l, dx, s["x_in"], s["x1"], s["proj"], s["gpre"], p, k["wg"], k["conv_k"],
            k["norm_g"], k["ln_g"], k["ln_b"], k["w_mix"], k["w_mix_t"], k["b_mix"], k["ple_g"],
            loss_head=(target, final_g.reshape(1, D_MODEL)) if l == DEPTH - 1 else None)
        dx, dproj, dx1, dgpre, dpp, smalls[l], dws[l] = outs[:7]
        if l == DEPTH - 1:
            head = outs[7]
        outs = _weight_grads(l, dproj, s["hn"], s["cat"], dx1, s["r"], dgpre, dpp, p, scatter_pack=pending)
        if pending is not None:
            reduced[l + 1] = _sum_pieces(l + 1, outs[1])
        pending = outs[0]
    reduced[0], total = _reduce_scatter_all_reduce(0, pending, smalls, head, dws)
    grad_x = dx[None]
    loss = total[TOTAL_HEAD + HEAD_LOSS, 0]

    n_ch = WIDTH_B // N_DEV
    g_conv = jnp.stack([total[l * SMALL_ROWS + SMALL_CONV:l * SMALL_ROWS + SMALL_CONV + 3, 0:WIDTH_B] for l in range(DEPTH)], axis=1)
    g_conv = lax.dynamic_slice_in_dim(g_conv, me * n_ch, n_ch, axis=2)
    g_proj = jnp.stack([_unpack_grad_proj(reduced[l]) for l in range(DEPTH)])

    def t_in(a):
        return jnp.swapaxes(a, 1, 2)

    def t_conv(a):
        return jnp.transpose(a, (2, 0, 1))

    (r_in,) = _adamw_rows("adamw_w_in", reduced, [OFF_IN], [(t_in(w_in), t_in(m_w_in), t_in(v_w_in))])
    r_out, r_gate = _adamw_rows(
        "adamw_w_out_gate", reduced, [OFF_OUT, OFF_GATE],
        [(w_out, m_w_out, v_w_out), (w_ple_gate, m_w_ple_gate, v_w_ple_gate)])
    small = _adamw_small(total, g_conv, g_proj, dict(
        norm_g=(norm_g, m_norm_g, v_norm_g), ple_norm_g=(ple_norm_g, m_ple_norm_g, v_ple_norm_g),
        ln_v_g=(ln_v_g, m_ln_v_g, v_ln_v_g), ln_v_b=(ln_v_b, m_ln_v_b, v_ln_v_b),
        b_s=(b_s, m_b_s, v_b_s), w_s=(w_s, m_w_s, v_w_s),
        final_g=tuple(a.reshape(1, D_MODEL) for a in (final_g, m_final_g, v_final_g)),
        conv_w=(t_conv(conv_w), t_conv(m_conv_w), t_conv(v_conv_w)),
        w_ple_proj=(w_ple_proj, m_w_ple_proj, v_w_ple_proj),
    ))
    res = dict(small, w_in=tuple(t_in(a) for a in r_in), w_out=r_out, w_ple_gate=r_gate)
    res["final_g"] = tuple(a.reshape(D_MODEL) for a in res["final_g"])
    res["conv_w"] = tuple(jnp.transpose(a, (1, 2, 0)) for a in res["conv_w"])
    order = ["norm_g", "w_in", "ln_v_g", "ln_v_b", "w_s", "b_s", "conv_w", "w_out", "ple_norm_g", "w_ple_gate", "w_ple_proj", "final_g"]
    return (loss, grad_x, *[res[n][0] for n in order], *[res[n][1] for n in order],
            *[res[n][2] for n in order], *[res[n][3] for n in order])
```

```python
import jax
import jax.numpy as jnp
from jax import lax
from jax.experimental import pallas as pl
from jax.experimental.pallas import tpu as pltpu

F32 = jnp.float32
BF16 = jnp.bfloat16

D_MODEL = 1024
WIDTH_A = 512
WIDTH_B = 512
HEADS_A = 4
HEAD_DIM = 128
CHUNK = 128
PLE_DIM = 256
PROJ_WIDTH = 3584
DEPTH = 2
EPS = 1e-6
N_DEV = 8

ADAM_LR = 0.001
ADAM_B1 = 0.9
ADAM_B2 = 0.999
ADAM_EPS = 1e-08
ADAM_WD = 0.01
ADAM_STEP = 10

ROWS_IN = PROJ_WIDTH // N_DEV
ROWS_OUT = D_MODEL // N_DEV
ROWS_GATE = D_MODEL // N_DEV
ROWS_PROJ = (D_MODEL // N_DEV) * PLE_DIM // D_MODEL
ROWS_CONV = 16
OFF_IN = 0
OFF_OUT = OFF_IN + ROWS_IN
OFF_GATE = OFF_OUT + ROWS_OUT
OFF_PROJ = OFF_GATE + ROWS_GATE
OFF_CONV = OFF_PROJ + ROWS_PROJ
ROWS_GRAD = OFF_CONV
ROWS_LAYER = OFF_CONV + ROWS_CONV

SMALL_ROWS = 8
SMALL_NORM = 0
SMALL_PLE = 1
SMALL_LN = 2
SMALL_BS = 3
SMALL_CONV = 4
HEAD_FINAL = 0
HEAD_LOSS = 1
TOTAL_HEAD = DEPTH * SMALL_ROWS
TOTAL_WS = TOTAL_HEAD + SMALL_ROWS
TOTAL_ROWS = TOTAL_WS + CHUNK

MIB = 1024 * 1024
MESH = pl.DeviceIdType.MESH

NT_DIMS = (((1,), (1,)), ((), ()))
TN_DIMS = (((0,), (0,)), ((), ()))


def _dot(a, b):
    return jnp.dot(a, b, preferred_element_type=F32)


def _dot_nt(a, b):
    return lax.dot_general(a, b, NT_DIMS, preferred_element_type=F32)


def _dot_tn(a, b):
    return lax.dot_general(a, b, TN_DIMS, preferred_element_type=F32)


def _colsum8(a):
    rows, n = a.shape
    return jnp.sum(a.reshape(rows // 8, 8, n), axis=0)


def _sigmoid(z):
    return 1.0 / (1.0 + jnp.exp(-z))


def _tile(t, want):
    return want if t % want == 0 else t


class _TwoLevelGather:
    def __init__(self, x_ref, out_ref, m_per, send_sems, recv_sems, local_sem):
        x, y, c = lax.axis_index("x"), lax.axis_index("y"), lax.axis_index("c")
        self.c = c
        self.me, self.sibling = (x, y, c), (x, y, 1 - c)
        self.chips = [(1 - x, y), (x, 1 - y), (1 - x, 1 - y)]
        self.x_ref, self.out_ref, self.m_per = x_ref, out_ref, m_per
        self.send_sems, self.recv_sems = send_sems, recv_sems
        self.mine = pltpu.make_async_copy(x_ref, self.rows(*self.me), local_sem)

    def rows(self, px, py, pc):
        return self.out_ref.at[pl.ds((4 * px + 2 * py + pc) * self.m_per, self.m_per), :]

    def copy(self, k, block, to, src=None):
        return pltpu.make_async_remote_copy(
            src_ref=self.rows(*block) if src is None else src,
            dst_ref=self.rows(*block),
            send_sem=self.send_sems.at[k],
            recv_sem=self.recv_sems.at[k],
            device_id=to,
            device_id_type=MESH,
        )

    def first(self):
        out = [self.copy(0, self.me, self.sibling, src=self.x_ref)]
        return out + [self.copy(1 + j, self.me, (*chip, self.c), src=self.x_ref) for j, chip in enumerate(self.chips)]

    def passed(self):
        return [self.copy(4 + j, (*chip, self.c), self.sibling) for j, chip in enumerate(self.chips)]

    def start(self):
        self.mine.start()
        for cp in self.first():
            cp.start()

    def pass_on(self):
        passed = self.passed()
        for j, chip in enumerate(self.chips):
            self.copy(1 + j, (*chip, self.c), self.me).wait_recv()
            passed[j].start()

    def finish(self):
        self.copy(0, self.sibling, self.me).wait_recv()
        for j, chip in enumerate(self.chips):
            self.copy(4 + j, (*chip, 1 - self.c), self.me).wait_recv()
        for cp in self.first() + self.passed():
            cp.wait_send()
        self.mine.wait()


GATHER_SEMS = [pltpu.SemaphoreType.DMA((7,)), pltpu.SemaphoreType.DMA((7,)), pltpu.SemaphoreType.DMA]


def _all_gather_rows(shard):
    m_per, n = shard.shape

    def body(x_ref, out_ref, send_sems, recv_sems, local_sem):
        ag = _TwoLevelGather(x_ref, out_ref, m_per, send_sems, recv_sems, local_sem)
        ag.start()
        ag.pass_on()
        ag.finish()

    return pl.pallas_call(
        body,
        name="weights_all_gather",
        out_shape=pltpu.HBM((N_DEV * m_per, n), shard.dtype),
        in_specs=[pl.BlockSpec(memory_space=pltpu.HBM)],
        out_specs=pl.BlockSpec(memory_space=pltpu.HBM),
        scratch_shapes=list(GATHER_SEMS),
    )(pltpu.with_memory_space_constraint(shard, pltpu.HBM))


PROJ_PARTS = D_MODEL // PLE_DIM
N_WEIGHT_COPIES = N_DEV * (3 + PROJ_PARTS)


def _weight_copies(wg_ref, w_in_t, w_out, w_gate, w_proj_t, sems):
    copies = []
    for s in range(N_DEV):
        base = s * ROWS_LAYER
        for dst, off, rows in ((w_in_t, OFF_IN, ROWS_IN), (w_out, OFF_OUT, ROWS_OUT), (w_gate, OFF_GATE, ROWS_GATE)):
            copies.append((wg_ref.at[pl.ds(base + off, rows), :], dst.at[pl.ds(s * rows, rows), :]))
        for j in range(PROJ_PARTS):
            copies.append((
                wg_ref.at[pl.ds(base + OFF_PROJ, ROWS_PROJ), pl.ds(j * PLE_DIM, PLE_DIM)],
                w_proj_t.at[pl.ds(s * ROWS_OUT + j * ROWS_PROJ, ROWS_PROJ), :],
            ))
    return [pltpu.make_async_copy(src, dst, sems.at[k]) for k, (src, dst) in enumerate(copies)]


def _forward_layer(layer, x, p_all, wg, conv_k, norm_g, ln_g, ln_b, w_mix, b_mix, ple_g, next_shard=None):
    t = x.shape[0]
    tm = _tile(t, 512)
    nt = t // tm
    gathers = next_shard is not None

    def body(*refs):
        (x_ref, p_ref, wg_ref, cw_ref, ng_ref, lng_ref, lnb_ref, wm_ref, bm_ref, pg_ref) = refs[:10]
        refs = refs[10:]
        if gathers:
            shard_ref, refs = refs[0], refs[1:]
        (proj_ref, hn_ref, cat_ref, r_ref, gpre_ref, x1_ref, x2_ref) = refs[:7]
        refs = refs[7:]
        if gathers:
            gathered_ref, refs = refs[0], refs[1:]
        (w_in_t, w_out, w_gate, wpt_ref, vln_s, mixed_s, halo_s, sems) = refs[:8]
        i = pl.program_id(0)
        if gathers:
            ag = _TwoLevelGather(shard_ref, gathered_ref, ROWS_LAYER, *refs[8:11])

            @pl.when(i == 0)
            def _():
                ag.start()

            @pl.when(i == (5 * nt) // 8)
            def _():
                ag.pass_on()

        @pl.when(i == 0)
        def _():
            copies = _weight_copies(wg_ref, w_in_t, w_out, w_gate, wpt_ref, sems)
            for cp in copies:
                cp.start()
            halo_s[...] = jnp.zeros_like(halo_s)
            for cp in copies:
                cp.wait()

        xv = x_ref[...]
        rstd0 = lax.rsqrt(jnp.mean(xv * xv, axis=-1, keepdims=True) + EPS)
        hn_ref[...] = (xv * rstd0 * ng_ref[...]).astype(BF16)

        def proj_section(k):
            sec = _dot_nt(hn_ref[...], w_in_t[pl.ds(k * 512, 512), :])
            proj_ref[:, k * 512:(k + 1) * 512] = sec.astype(BF16)
            return sec

        def gate_section(k):
            proj_section(k)
            return proj_ref[:, k * 512:(k + 1) * 512]

        v = proj_section(1)
        mu = jnp.mean(v, axis=-1, keepdims=True)
        vc = v - mu
        var = jnp.mean(vc * vc, axis=-1, keepdims=True)
        vln = vc * lax.rsqrt(var + EPS) * lng_ref[...] + lnb_ref[...]
        vln_s[...] = vln.astype(BF16)
        for ci in range(tm // CHUNK):
            rows = pl.ds(ci * CHUNK, CHUNK)
            for h in range(HEADS_A):
                cols = pl.ds(h * HEAD_DIM, HEAD_DIM)
                mixed_s[rows, cols] = (_dot(wm_ref[h], vln_s[rows, cols]) + bm_ref[h]).astype(BF16)
        u = gate_section(0)
        za = gate_section(2)
        cat_ref[:, 0:512] = (u * mixed_s[...]) * (za * _sigmoid(za))

        xc = proj_section(5) * proj_section(3)
        prev = halo_s[...]
        row = lax.broadcasted_iota(jnp.int32, (tm, WIDTH_B), 0)
        xc_m1 = jnp.where(row == 0, prev[7:8, :], pltpu.roll(xc, 1, 0))
        xc_m2 = jnp.where(row == 0, prev[6:7, :], jnp.where(row == 1, prev[7:8, :], pltpu.roll(xc, 2, 0)))
        halo_s[...] = xc[tm - 8:tm, :]
        cw = cw_ref[...]
        yc = cw[0:1, :] * xc_m2 + cw[1:2, :] * xc_m1 + cw[2:3, :] * xc
        zb = gate_section(6)
        cat_ref[:, 512:1024] = (gate_section(4) * yc.astype(BF16)) * (zb * _sigmoid(zb))

        x1 = xv + _dot(cat_ref[...], w_out[...])
        x1_ref[...] = x1
        rstd1 = lax.rsqrt(jnp.mean(x1 * x1, axis=-1, keepdims=True) + EPS)
        r_ref[...] = (x1 * rstd1 * pg_ref[...]).astype(BF16)
        gpre = _dot(r_ref[...], w_gate[...])
        gpre_ref[...] = gpre.astype(BF16)
        pp = _dot_nt(p_ref[...].astype(BF16), wpt_ref[...])
        x2_ref[...] = x1 + _sigmoid(gpre) * pp

        if gathers:
            @pl.when(i == nt - 1)
            def _():
                ag.finish()

    def tok(width):
        return pl.BlockSpec((tm, width), lambda i: (i, 0))

    def whole(shape):
        return pl.BlockSpec(shape, lambda i: (0,) * len(shape))

    hbm = pl.BlockSpec(memory_space=pl.ANY)
    operands = [x, p_all, wg, conv_k, norm_g, ln_g, ln_b, w_mix, b_mix, ple_g]
    in_specs = [
        tok(D_MODEL), pl.BlockSpec((None, None, tm, PLE_DIM), lambda i: (layer, 0, i, 0)), hbm,
        whole((8, WIDTH_B)), whole((1, D_MODEL)), whole((1, WIDTH_A)), whole((1, WIDTH_A)),
        whole((HEADS_A, CHUNK, CHUNK)), whole((HEADS_A, CHUNK, HEAD_DIM)), whole((1, D_MODEL)),
    ]
    out_specs = [tok(PROJ_WIDTH), tok(D_MODEL), tok(D_MODEL), tok(D_MODEL), tok(D_MODEL), tok(D_MODEL), tok(D_MODEL)]
    out_shape = [
        jax.ShapeDtypeStruct((t, PROJ_WIDTH), BF16),
        jax.ShapeDtypeStruct((t, D_MODEL), BF16),
        jax.ShapeDtypeStruct((t, D_MODEL), BF16),
        jax.ShapeDtypeStruct((t, D_MODEL), BF16),
        jax.ShapeDtypeStruct((t, D_MODEL), BF16),
        jax.ShapeDtypeStruct((t, D_MODEL), F32),
        jax.ShapeDtypeStruct((t, D_MODEL), F32),
    ]
    scratch_shapes = [
        pltpu.VMEM((PROJ_WIDTH, D_MODEL), BF16),
        pltpu.VMEM((D_MODEL, D_MODEL), BF16),
        pltpu.VMEM((D_MODEL, D_MODEL), BF16),
        pltpu.VMEM((D_MODEL, PLE_DIM), BF16),
        pltpu.VMEM((tm, WIDTH_A), BF16),
        pltpu.VMEM((tm, WIDTH_A), BF16),
        pltpu.VMEM((8, WIDTH_B), F32),
        pltpu.SemaphoreType.DMA((N_WEIGHT_COPIES,)),
    ]
    if gathers:
        operands.append(pltpu.with_memory_space_constraint(next_shard, pltpu.HBM))
        in_specs.append(pl.BlockSpec(memory_space=pltpu.HBM))
        out_specs.append(pl.BlockSpec(memory_space=pltpu.HBM))
        out_shape.append(pltpu.HBM((N_DEV * ROWS_LAYER, D_MODEL), BF16))
        scratch_shapes += list(GATHER_SEMS)

    return pl.pallas_call(
        body,
        name=f"layer{layer}_forward",
        grid=(nt,),
        in_specs=in_specs,
        out_specs=out_specs,
        out_shape=out_shape,
        scratch_shapes=scratch_shapes,
        compiler_params=pltpu.CompilerParams(dimension_semantics=("arbitrary",), vmem_limit_bytes=56 * MIB),
    )(*operands)


class _DirectScatter:
    def __init__(self, pack_ref, pieces_ref, send_sems, recv_sems, local_sem):
        x, y, c = lax.axis_index("x"), lax.axis_index("y"), lax.axis_index("c")
        me = 4 * x + 2 * y + c
        self.copies = []
        for k in range(N_DEV - 1):
            fx, fy, fc = ((k + 1) >> 2) & 1, ((k + 1) >> 1) & 1, (k + 1) & 1
            tx, ty, tc = x ^ fx, y ^ fy, c ^ fc
            self.copies.append(
                pltpu.make_async_remote_copy(
                    src_ref=pack_ref.at[4 * tx + 2 * ty + tc], dst_ref=pieces_ref.at[me],
                    send_sem=send_sems.at[k], recv_sem=recv_sems.at[k],
                    device_id=(tx, ty, tc), device_id_type=MESH,
                )
            )
        self.mine = pltpu.make_async_copy(pack_ref.at[me], pieces_ref.at[me], local_sem)

    def start(self):
        self.mine.start()
        for cp in self.copies:
            cp.start()

    def finish(self):
        for cp in self.copies:
            cp.wait_recv()
        for cp in self.copies:
            cp.wait_send()
        self.mine.wait()


SCATTER_SEMS = [pltpu.SemaphoreType.DMA((N_DEV - 1,)), pltpu.SemaphoreType.DMA((N_DEV - 1,)), pltpu.SemaphoreType.DMA]


def _backward_layer(layer, dx2, x_in, x1, proj, gpre, p_all, wg, conv_k, norm_g, ln_g, ln_b,
                    w_mix, w_mix_t, b_mix, ple_g, loss_head=None):
    t = x_in.shape[0]
    tm = _tile(t, 256)
    nt = t // tm
    n_chunks = tm // CHUNK
    halo_rows = 16
    heads = loss_head is not None

    def body(*refs):
        (dx2_ref, xin_ref, x1_ref, proj_ref, halo_ref, gpre_ref, p_ref, wg_ref, cw_ref,
         ng_ref, lng_ref, lnb_ref, wm_ref, wmt_ref, bm_ref, pg_ref) = refs[:16]
        refs = refs[16:]
        if heads:
            tgt_ref, fg_ref = refs[:2]
            refs = refs[2:]
        (dxin_ref, dproj_ref, dx1_ref, dgpre_ref, dpp_ref, small_ref, dws_ref) = refs[:7]
        refs = refs[7:]
        if heads:
            head_ref, refs = refs[0], refs[1:]
        (w_in_t, w_out, w_gate, wpt_ref, vln_s, mixed_s, dmix_s, dvln_s, carry_s,
         ng_acc, pg_acc, lng_acc, lnb_acc, cw_acc, dbm_ref, sems) = refs[:16]
        if heads:
            loss_acc, fg_acc = refs[16:18]
        i = pl.program_id(0)
        tile = nt - 1 - i

        @pl.when(i == 0)
        def _():
            copies = _weight_copies(wg_ref, w_in_t, w_out, w_gate, wpt_ref, sems)
            for cp in copies:
                cp.start()
            if heads:
                loss_acc[...] = jnp.zeros_like(loss_acc)
                fg_acc[...] = jnp.zeros_like(fg_acc)
            carry_s[...] = jnp.zeros_like(carry_s)
            ng_acc[...] = jnp.zeros_like(ng_acc)
            pg_acc[...] = jnp.zeros_like(pg_acc)
            lng_acc[...] = jnp.zeros_like(lng_acc)
            lnb_acc[...] = jnp.zeros_like(lnb_acc)
            cw_acc[...] = jnp.zeros_like(cw_acc)
            dws_ref[...] = jnp.zeros_like(dws_ref)
            dbm_ref[...] = jnp.zeros_like(dbm_ref)
            for cp in copies:
                cp.wait()

        if heads:
            x2v = dx2_ref[...]
            fg = fg_ref[...]
            rstdf = lax.rsqrt(jnp.mean(x2v * x2v, axis=-1, keepdims=True) + EPS)
            xhatf = x2v * rstdf
            err = xhatf * fg - tgt_ref[...]
            loss_acc[...] += _colsum8(err * err)
            dy = err * (1.0 / D_MODEL)
            fg_acc[...] += _colsum8(dy * xhatf)
            dxhf = dy * fg
            dx2v = rstdf * (dxhf - xhatf * jnp.mean(dxhf * xhatf, axis=-1, keepdims=True))
        else:
            dx2v = dx2_ref[...]

        gate = _sigmoid(gpre_ref[...].astype(F32))
        pp = _dot_nt(p_ref[...].astype(BF16), wpt_ref[...])
        dpp = dx2v * gate
        dpp_ref[...] = dpp.astype(BF16)
        dgpre = (dpp * pp * (1.0 - gate)).astype(BF16)
        dgpre_ref[...] = dgpre
        dr = _dot_nt(dgpre, w_gate[...])
        x1v = x1_ref[...]
        rstd1 = lax.rsqrt(jnp.mean(x1v * x1v, axis=-1, keepdims=True) + EPS)
        xhat1 = x1v * rstd1
        pg_acc[...] += _colsum8(dr * xhat1)
        dxh = dr * pg_ref[...]
        dx1 = dx2v + rstd1 * (dxh - xhat1 * jnp.mean(dxh * xhat1, axis=-1, keepdims=True))
        dx1b = dx1.astype(BF16)
        dx1_ref[...] = dx1b

        dcat = _dot_nt(dx1b, w_out[...])
        dca = dcat[:, 0:512]
        dcb = dcat[:, 512:1024]

        u = proj_ref[:, 0:512]
        v = proj_ref[:, 512:1024].astype(F32)
        za = proj_ref[:, 1024:1536]
        mu = jnp.mean(v, axis=-1, keepdims=True)
        vc = v - mu
        var = jnp.mean(vc * vc, axis=-1, keepdims=True)
        rs = lax.rsqrt(var + EPS)
        vhat = vc * rs
        lng = lng_ref[...]
        vln_s[...] = (vhat * lng + lnb_ref[...]).astype(BF16)
        for ci in range(n_chunks):
            rows = pl.ds(ci * CHUNK, CHUNK)
            for h in range(HEADS_A):
                cols = pl.ds(h * HEAD_DIM, HEAD_DIM)
                mixed_s[rows, cols] = (_dot(wm_ref[h], vln_s[rows, cols]) + bm_ref[h]).astype(BF16)
        mixed = mixed_s[...]
        sga = _sigmoid(za)
        sa = za * sga
        dsa = sga + sa * (1.0 - sga)

        def put_section(k, val):
            dproj_ref[:, k * 512:(k + 1) * 512] = val.astype(BF16)

        dcab = dca.astype(BF16)
        dca_sa = dcab * sa
        put_section(0, dca_sa * mixed)
        dmix_s[...] = dca_sa * u
        put_section(2, (dcab * dsa) * (u * mixed))
        dbm_acc = jnp.zeros((CHUNK, WIDTH_A), F32)
        for ci in range(n_chunks):
            rows = pl.ds(ci * CHUNK, CHUNK)
            dbm_acc = dbm_acc + dmix_s[rows, :].astype(F32)
            for h in range(HEADS_A):
                cols = pl.ds(h * HEAD_DIM, HEAD_DIM)
                dvln_s[rows, cols] = _dot(wmt_ref[h], dmix_s[rows, cols])
                dws_ref[:, cols] += _dot_nt(dmix_s[rows, cols], vln_s[rows, cols])
        dbm_ref[...] += dbm_acc
        dvln = dvln_s[...]
        lng_acc[...] += _colsum8(dvln * vhat)
        lnb_acc[...] += _colsum8(dvln)
        dvh = dvln * lng
        dv = rs * (dvh - jnp.mean(dvh, axis=-1, keepdims=True) - vhat * jnp.mean(dvh * vhat, axis=-1, keepdims=True))
        put_section(1, dv)

        hb = proj_ref[:, 1536:2048].astype(F32)
        gb = proj_ref[:, 2048:2560]
        gc = proj_ref[:, 2560:3072].astype(F32)
        zb = proj_ref[:, 3072:3584]
        xc = gc * hb
        prev = halo_ref[:, 2560:3072].astype(F32) * halo_ref[:, 1536:2048].astype(F32)
        prev = jnp.where(tile > 0, prev, 0.0)
        row = lax.broadcasted_iota(jnp.int32, (tm, WIDTH_B), 0)
        p1 = prev[halo_rows - 1:halo_rows, :]
        p2 = prev[halo_rows - 2:halo_rows - 1, :]
        xc_m1 = jnp.where(row == 0, p1, pltpu.roll(xc, 1, 0))
        xc_m2 = jnp.where(row == 0, p2, jnp.where(row == 1, p1, pltpu.roll(xc, 2, 0)))
        cw = cw_ref[...]
        yc = cw[0:1, :] * xc_m2 + cw[1:2, :] * xc_m1 + cw[2:3, :] * xc
        sgb = _sigmoid(zb)
        sb = zb * sgb
        dsb = sgb + sb * (1.0 - sgb)
        dcbb = dcb.astype(BF16)
        ycb = yc.astype(BF16)
        dcb_sb = dcbb * sb
        put_section(4, dcb_sb * ycb)
        dyc = (dcb_sb * gb).astype(F32)
        put_section(6, (dcbb * dsb) * (gb * ycb))
        nxt = carry_s[...]
        dyc_p1 = jnp.where(row == tm - 1, nxt[0:1, :], pltpu.roll(dyc, tm - 1, 0))
        dyc_p2 = jnp.where(row == tm - 1, nxt[1:2, :], jnp.where(row == tm - 2, nxt[0:1, :], pltpu.roll(dyc, tm - 2, 0)))
        carry_s[...] = dyc[0:8, :]
        dxc = cw[2:3, :] * dyc + cw[1:2, :] * dyc_p1 + cw[0:1, :] * dyc_p2
        cw_acc[0] += _colsum8(dyc * xc_m2)
        cw_acc[1] += _colsum8(dyc * xc_m1)
        cw_acc[2] += _colsum8(dyc * xc)
        put_section(3, dxc * gc)
        put_section(5, dxc * hb)

        dhn = _dot(dproj_ref[...], w_in_t[...])
        xv = xin_ref[...]
        rstd0 = lax.rsqrt(jnp.mean(xv * xv, axis=-1, keepdims=True) + EPS)
        xhat0 = xv * rstd0
        ng_acc[...] += _colsum8(dhn * xhat0)
        dxh0 = dhn * ng_ref[...]
        dxin_ref[...] = dx1 + rstd0 * (dxh0 - xhat0 * jnp.mean(dxh0 * xhat0, axis=-1, keepdims=True))

        @pl.when(i == nt - 1)
        def _():
            small_ref[...] = jnp.zeros_like(small_ref)
            small_ref[SMALL_NORM:SMALL_NORM + 1, :] = jnp.sum(ng_acc[...], axis=0, keepdims=True)
            small_ref[SMALL_PLE:SMALL_PLE + 1, :] = jnp.sum(pg_acc[...], axis=0, keepdims=True)
            small_ref[SMALL_LN:SMALL_LN + 1, 0:WIDTH_A] = jnp.sum(lng_acc[...], axis=0, keepdims=True)
            small_ref[SMALL_LN:SMALL_LN + 1, WIDTH_A:2 * WIDTH_A] = jnp.sum(lnb_acc[...], axis=0, keepdims=True)
            for h in range(HEADS_A):
                cols = pl.ds(h * HEAD_DIM, HEAD_DIM)
                small_ref[SMALL_BS:SMALL_BS + 1, cols] = jnp.sum(jnp.transpose(dbm_ref[:, cols]), axis=0, keepdims=True)
            for k in range(3):
                small_ref[SMALL_CONV + k:SMALL_CONV + k + 1, 0:WIDTH_B] = jnp.sum(cw_acc[k], axis=0, keepdims=True)
            if heads:
                total = jnp.sum(loss_acc[...]) * (0.5 / D_MODEL)
                rows8 = lax.broadcasted_iota(jnp.int32, (SMALL_ROWS, D_MODEL), 0)
                lanes8 = lax.broadcasted_iota(jnp.int32, (SMALL_ROWS, D_MODEL), 1)
                head_ref[...] = jnp.where((rows8 == HEAD_LOSS) & (lanes8 == 0), total, 0.0)
                head_ref[HEAD_FINAL:HEAD_FINAL + 1, :] = jnp.sum(fg_acc[...], axis=0, keepdims=True)

    def tok(width):
        return pl.BlockSpec((tm, width), lambda i: (nt - 1 - i, 0))

    def whole(shape):
        return pl.BlockSpec(shape, lambda i: (0,) * len(shape))

    halo_spec = pl.BlockSpec(
        (halo_rows, PROJ_WIDTH), lambda i: (jnp.maximum((nt - 1 - i) * (tm // halo_rows) - 1, 0), 0)
    )
    hbm = pl.BlockSpec(memory_space=pl.ANY)
    operands = [dx2, x_in, x1, proj, proj, gpre, p_all, wg, conv_k, norm_g, ln_g, ln_b, w_mix, w_mix_t, b_mix, ple_g]
    in_specs = [
        tok(D_MODEL), tok(D_MODEL), tok(D_MODEL), tok(PROJ_WIDTH), halo_spec, tok(D_MODEL),
        pl.BlockSpec((None, None, tm, PLE_DIM), lambda i: (layer, 0, nt - 1 - i, 0)), hbm,
        whole((8, WIDTH_B)), whole((1, D_MODEL)), whole((1, WIDTH_A)), whole((1, WIDTH_A)),
        whole((HEADS_A, CHUNK, CHUNK)), whole((HEADS_A, CHUNK, CHUNK)), whole((HEADS_A, CHUNK, HEAD_DIM)),
        whole((1, D_MODEL)),
    ]
    out_specs = [
        tok(D_MODEL), tok(PROJ_WIDTH), tok(D_MODEL), tok(D_MODEL), tok(D_MODEL),
        whole((SMALL_ROWS, D_MODEL)), whole((CHUNK, WIDTH_A)),
    ]
    out_shape = [
        jax.ShapeDtypeStruct((t, D_MODEL), F32),
        jax.ShapeDtypeStruct((t, PROJ_WIDTH), BF16),
        jax.ShapeDtypeStruct((t, D_MODEL), BF16),
        jax.ShapeDtypeStruct((t, D_MODEL), BF16),
        jax.ShapeDtypeStruct((t, D_MODEL), BF16),
        jax.ShapeDtypeStruct((SMALL_ROWS, D_MODEL), F32),
        jax.ShapeDtypeStruct((CHUNK, WIDTH_A), F32),
    ]
    scratch_shapes = [
        pltpu.VMEM((PROJ_WIDTH, D_MODEL), BF16),
        pltpu.VMEM((D_MODEL, D_MODEL), BF16),
        pltpu.VMEM((D_MODEL, D_MODEL), BF16),
        pltpu.VMEM((D_MODEL, PLE_DIM), BF16),
        pltpu.VMEM((tm, WIDTH_A), BF16),
        pltpu.VMEM((tm, WIDTH_A), BF16),
        pltpu.VMEM((tm, WIDTH_A), BF16),
        pltpu.VMEM((tm, WIDTH_A), F32),
        pltpu.VMEM((8, WIDTH_B), F32),
        pltpu.VMEM((8, D_MODEL), F32),
        pltpu.VMEM((8, D_MODEL), F32),
        pltpu.VMEM((8, WIDTH_A), F32),
        pltpu.VMEM((8, WIDTH_A), F32),
        pltpu.VMEM((3, 8, WIDTH_B), F32),
        pltpu.VMEM((CHUNK, WIDTH_A), F32),
        pltpu.SemaphoreType.DMA((N_WEIGHT_COPIES,)),
    ]
    if heads:
        operands += list(loss_head)
        in_specs += [tok(D_MODEL), whole((1, D_MODEL))]
        out_specs.append(whole((SMALL_ROWS, D_MODEL)))
        out_shape.append(jax.ShapeDtypeStruct((SMALL_ROWS, D_MODEL), F32))
        scratch_shapes += [pltpu.VMEM((8, D_MODEL), F32), pltpu.VMEM((8, D_MODEL), F32)]

    return pl.pallas_call(
        body,
        name=f"layer{layer}_backward",
        grid=(nt,),
        in_specs=in_specs,
        out_specs=out_specs,
        out_shape=out_shape,
        scratch_shapes=scratch_shapes,
        compiler_params=pltpu.CompilerParams(dimension_semantics=("arbitrary",), vmem_limit_bytes=56 * MIB),
    )(*operands)


def _sum_pieces(layer, pieces):
    rows, n = pieces.shape[1], pieces.shape[2]
    blocks = 2
    rb = rows // blocks

    def body(p_ref, out_ref):
        total = p_ref[0].astype(F32)
        for j in range(1, N_DEV):
            total = total + p_ref[j].astype(F32)
        out_ref[...] = total

    return pl.pallas_call(
        body,
        name=f"layer{layer}_grad_sum",
        grid=(blocks,),
        out_shape=pltpu.HBM((rows, n), F32),
        in_specs=[pl.BlockSpec((N_DEV, rb, n), lambda i: (0, i, 0))],
        out_specs=pl.BlockSpec((rb, n), lambda i: (i, 0)),
        compiler_params=pltpu.CompilerParams(dimension_semantics=("arbitrary",), vmem_limit_bytes=32 * MIB),
    )(pieces)


def _weight_grads(layer, dproj, hn, cat, dx1, r, dgpre, dpp, p_all, scatter_pack=None):
    t = hn.shape[0]
    tk = _tile(t, 512)
    nt = t // tk
    in_blocks = PROJ_WIDTH // 512
    scatters = scatter_pack is not None

    def body(*refs):
        (dproj_ref, hn_ref, cat_ref, dx1_ref, r_ref, dgpre_ref, dpp_ref, p_ref) = refs[:8]
        refs = refs[8:]
        if scatters:
            prior_ref, refs = refs[0], refs[1:]
        pack_ref, refs = refs[0], refs[1:]
        if scatters:
            pieces_ref, refs = refs[0], refs[1:]
        (acc_in, acc_out, acc_gate, acc_proj, stage, sems) = refs[:6]
        i = pl.program_id(0)
        if scatters:
            scatter = _DirectScatter(prior_ref, pieces_ref, *refs[6:9])

            @pl.when(i == 0)
            def _():
                scatter.start()

        @pl.when(i == 0)
        def _():
            acc_in[...] = jnp.zeros_like(acc_in)
            acc_out[...] = jnp.zeros_like(acc_out)
            acc_gate[...] = jnp.zeros_like(acc_gate)
            acc_proj[...] = jnp.zeros_like(acc_proj)

        hnv = hn_ref[...]
        for b in range(in_blocks):
            acc_in[pl.ds(b * 512, 512), :] += _dot_tn(dproj_ref[:, b * 512:(b + 1) * 512], hnv)
        dx1v = dx1_ref[...]
        dgv = dgpre_ref[...]
        for b in range(D_MODEL // 512):
            acc_out[pl.ds(b * 512, 512), :] += _dot_tn(cat_ref[:, b * 512:(b + 1) * 512], dx1v)
            acc_gate[pl.ds(b * 512, 512), :] += _dot_tn(r_ref[:, b * 512:(b + 1) * 512], dgv)
        pv = p_ref[...].astype(BF16)
        for b in range(D_MODEL // 512):
            acc_proj[pl.ds(b * 512, 512), :] += _dot_tn(dpp_ref[:, b * 512:(b + 1) * 512], pv)

        @pl.when(i == nt - 1)
        def _():
            def out_copy(s):
                return pltpu.make_async_copy(stage.at[s % 2], pack_ref.at[s], sems.at[s % 2])

            for s in range(N_DEV):
                if s >= 2:
                    out_copy(s - 2).wait()
                buf = stage.at[s % 2]
                buf[pl.ds(OFF_IN, ROWS_IN), :] = acc_in[pl.ds(s * ROWS_IN, ROWS_IN), :].astype(BF16)
                buf[pl.ds(OFF_OUT, ROWS_OUT), :] = acc_out[pl.ds(s * ROWS_OUT, ROWS_OUT), :].astype(BF16)
                buf[pl.ds(OFF_GATE, ROWS_GATE), :] = acc_gate[pl.ds(s * ROWS_GATE, ROWS_GATE), :].astype(BF16)
                for j in range(D_MODEL // PLE_DIM):
                    buf[pl.ds(OFF_PROJ, ROWS_PROJ), pl.ds(j * PLE_DIM, PLE_DIM)] = acc_proj[
                        pl.ds(s * ROWS_OUT + j * ROWS_PROJ, ROWS_PROJ), :
                    ].astype(BF16)
                out_copy(s).start()
            out_copy(N_DEV - 2).wait()
            out_copy(N_DEV - 1).wait()
            if scatters:
                scatter.finish()

    def tok(width):
        return pl.BlockSpec((tk, width), lambda i: (i, 0))

    hbm = pl.BlockSpec(memory_space=pl.ANY)
    pack_shape = jax.ShapeDtypeStruct((N_DEV, ROWS_GRAD, D_MODEL), BF16)
    operands = [dproj, hn, cat, dx1, r, dgpre, dpp, p_all]
    in_specs = [tok(PROJ_WIDTH), tok(D_MODEL), tok(D_MODEL), tok(D_MODEL), tok(D_MODEL), tok(D_MODEL), tok(D_MODEL),
                pl.BlockSpec((None, None, tk, PLE_DIM), lambda i: (layer, 0, i, 0))]
    out_specs, out_shape = [hbm], [pack_shape]
    scratch_shapes = [
        pltpu.VMEM((PROJ_WIDTH, D_MODEL), F32),
        pltpu.VMEM((D_MODEL, D_MODEL), F32),
        pltpu.VMEM((D_MODEL, D_MODEL), F32),
        pltpu.VMEM((D_MODEL, PLE_DIM), F32),
        pltpu.VMEM((2, ROWS_GRAD, D_MODEL), BF16),
        pltpu.SemaphoreType.DMA((2,)),
    ]
    if scatters:
        operands.append(scatter_pack)
        in_specs.append(hbm)
        out_specs.append(hbm)
        out_shape.append(pack_shape)
        scratch_shapes += list(SCATTER_SEMS)

    return pl.pallas_call(
        body,
        name=f"layer{layer}_weight_grads",
        grid=(nt,),
        in_specs=in_specs,
        out_specs=out_specs,
        out_shape=out_shape,
        scratch_shapes=scratch_shapes,
        compiler_params=pltpu.CompilerParams(dimension_semantics=("arbitrary",), vmem_limit_bytes=58 * MIB),
    )(*operands)


def _reduce_scatter_all_reduce(layer, pack, smalls, head, dws):
    rows, n = pack.shape[1], pack.shape[2]
    assert DEPTH * WIDTH_A == D_MODEL and n == D_MODEL

    def body(g_ref, *refs):
        small_refs, refs = refs[:DEPTH], refs[DEPTH:]
        head_ref, refs = refs[0], refs[1:]
        dws_refs, refs = refs[:DEPTH], refs[DEPTH:]
        (out_ref, total_ref, r1, a_s, r2, sp, sr1, sq, send1, recv1, send2, recv2, ssend, srecv) = refs
        x, y, c = lax.axis_index("x"), lax.axis_index("y"), lax.axis_index("c")
        sibling = (x, y, 1 - c)
        chip = 2 * x + y
        flips = [(1, 0), (0, 1), (1, 1)]

        for l in range(DEPTH):
            sp[l * SMALL_ROWS:(l + 1) * SMALL_ROWS, :] = small_refs[l][...]
            sp[TOTAL_WS:TOTAL_ROWS, l * WIDTH_A:(l + 1) * WIDTH_A] = dws_refs[l][...]
        sp[TOTAL_HEAD:TOTAL_WS, :] = head_ref[...]

        small_pair = pltpu.make_async_remote_copy(
            src_ref=sp, dst_ref=sr1, send_sem=ssend.at[0], recv_sem=srecv.at[0], device_id=sibling, device_id_type=MESH
        )

        def to_sibling(j):
            return pltpu.make_async_remote_copy(
                src_ref=g_ref.at[2 * j + 1 - c], dst_ref=r1.at[j], send_sem=send1.at[j], recv_sem=recv1.at[j],
                device_id=sibling, device_id_type=MESH,
            )

        first = [to_sibling(j) for j in range(4)]
        small_pair.start()
        for cp in first:
            cp.start()

        small_pair.wait_recv()
        sq[chip] = sp[...] + sr1[...]
        small_chips = [
            pltpu.make_async_remote_copy(
                src_ref=sq.at[chip], dst_ref=sq.at[chip], send_sem=ssend.at[1 + k], recv_sem=srecv.at[1 + k],
                device_id=(x ^ fx, y ^ fy, c), device_id_type=MESH,
            )
            for k, (fx, fy) in enumerate(flips)
        ]
        for cp in small_chips:
            cp.start()

        def to_chip(j):
            return pltpu.make_async_remote_copy(
                src_ref=a_s.at[j], dst_ref=r2.at[chip], send_sem=send2.at[j], recv_sem=recv2.at[chip],
                device_id=(j // 2, j % 2, c), device_id_type=MESH,
            )

        def from_chip(k):
            return pltpu.make_async_remote_copy(
                src_ref=a_s.at[k], dst_ref=r2.at[k], send_sem=send2.at[k], recv_sem=recv2.at[k],
                device_id=(k // 2, k % 2, c), device_id_type=MESH,
            )

        for j in range(4):
            first[j].wait_recv()

            @pl.when(chip != j)
            def _():
                a_s[j] = (g_ref[2 * j + c].astype(F32) + r1[j].astype(F32)).astype(BF16)
                to_chip(j).start()

        out_ref[...] = g_ref[2 * chip + c].astype(F32) + r1[chip].astype(F32)
        for cp in small_chips:
            cp.wait_recv()
        total_ref[...] = ((sq[0] + sq[1]) + sq[2]) + sq[3]
        for k in range(4):
            @pl.when(chip != k)
            def _():
                from_chip(k).wait_recv()
                out_ref[...] += r2[k].astype(F32)
        small_pair.wait_send()
        for cp in first + small_chips:
            cp.wait_send()
        for j in range(4):
            @pl.when(chip != j)
            def _():
                to_chip(j).wait_send()

    vmem = pl.BlockSpec(memory_space=pltpu.VMEM)
    return pl.pallas_call(
        body,
        name=f"layer{layer}_grad_reduce_scatter",
        out_shape=[jax.ShapeDtypeStruct((rows, n), F32), jax.ShapeDtypeStruct((TOTAL_ROWS, D_MODEL), F32)],
        in_specs=[vmem] * (2 + 2 * DEPTH),
        out_specs=[vmem, vmem],
        scratch_shapes=[
            pltpu.VMEM((4, rows, n), BF16),
            pltpu.VMEM((4, rows, n), BF16),
            pltpu.VMEM((4, rows, n), BF16),
            pltpu.VMEM((TOTAL_ROWS, D_MODEL), F32),
            pltpu.VMEM((TOTAL_ROWS, D_MODEL), F32),
            pltpu.VMEM((4, TOTAL_ROWS, D_MODEL), F32),
            pltpu.SemaphoreType.DMA((4,)),
            pltpu.SemaphoreType.DMA((4,)),
            pltpu.SemaphoreType.DMA((4,)),
            pltpu.SemaphoreType.DMA((4,)),
            pltpu.SemaphoreType.DMA((4,)),
            pltpu.SemaphoreType.DMA((4,)),
        ],
        compiler_params=pltpu.CompilerParams(vmem_limit_bytes=48 * MIB),
    )(pack, *smalls, head, *dws)


def _adam_step(w, g, m, v):
    m = ADAM_B1 * m + (1.0 - ADAM_B1) * g
    v = ADAM_B2 * v + (1.0 - ADAM_B2) * (g * g)
    m_hat = m / (1.0 - ADAM_B1 ** ADAM_STEP)
    v_hat = v / (1.0 - ADAM_B2 ** ADAM_STEP)
    return -ADAM_LR * (m_hat / (jnp.sqrt(v_hat) + ADAM_EPS) + ADAM_WD * w), m, v


def _adamw_rows(name, reduced, row_off, states):
    n = len(states)

    def body(*refs):
        red = refs[:DEPTH]
        ins = refs[DEPTH:DEPTH + 3 * n]
        outs = refs[DEPTH + 3 * n:]
        layer = pl.program_id(0)
        for l in range(DEPTH):
            @pl.when(layer == l)
            def _():
                for k in range(n):
                    w_ref, m_ref, v_ref = ins[3 * k:3 * k + 3]
                    g_ref, d_ref, nm_ref, nv_ref = outs[4 * k:4 * k + 4]
                    g = red[l][row_off[k]:row_off[k] + w_ref.shape[0], :]
                    d, m, v = _adam_step(w_ref[...], g, m_ref[...], v_ref[...])
                    g_ref[...] = g
                    d_ref[...] = d
                    nm_ref[...] = m
                    nv_ref[...] = v

    flat = [a for st in states for a in st]
    state_specs, out_specs, out_shape = [], [], []
    for w, _, _ in states:
        spec = pl.BlockSpec((None,) + w.shape[1:], lambda l: (l, 0, 0))
        state_specs += [spec] * 3
        out_specs += [spec] * 4
        out_shape += [jax.ShapeDtypeStruct(w.shape, F32)] * 4
    red_specs = [pl.BlockSpec(a.shape, lambda l: (0, 0)) for a in reduced]
    operands = [pltpu.with_memory_space_constraint(a, pltpu.HBM) for a in (*reduced, *flat)]
    outs = pl.pallas_call(
        body,
        name=name,
        grid=(DEPTH,),
        out_shape=[pltpu.HBM(a.shape, a.dtype) for a in out_shape],
        in_specs=red_specs + state_specs,
        out_specs=out_specs,
        compiler_params=pltpu.CompilerParams(dimension_semantics=("arbitrary",), vmem_limit_bytes=48 * MIB),
    )(*operands)
    return [tuple(outs[4 * k:4 * k + 4]) for k in range(n)]


def _adamw_small(total, g_conv, g_proj, st):
    names = ["norm_g", "ple_norm_g", "ln_v_g", "ln_v_b", "b_s", "w_s", "final_g", "conv_w", "w_ple_proj"]
    cut = names[:7]

    def body(total_ref, gconv_ref, gproj_ref, *refs):
        ins = {nm: refs[3 * k:3 * k + 3] for k, nm in enumerate(names)}
        outs, pos = {}, 3 * len(names)
        for nm in names:
            cnt = 4 if nm in cut else 3
            outs[nm] = refs[pos:pos + cnt]
            pos += cnt

        def update(nm, idx, g):
            w_ref, m_ref, v_ref = ins[nm]
            d, m, v = _adam_step(w_ref[idx], g, m_ref[idx], v_ref[idx])
            o = outs[nm]
            if nm in cut:
                o[0][idx] = g
                o = o[1:]
            o[0][idx] = d
            o[1][idx] = m
            o[2][idx] = v

        tril = (lax.broadcasted_iota(jnp.int32, (CHUNK, CHUNK), 0) >= lax.broadcasted_iota(jnp.int32, (CHUNK, CHUNK), 1))
        for l in range(DEPTH):
            base = l * SMALL_ROWS
            row = (slice(l, l + 1), slice(None))
            update("norm_g", row, total_ref[base + SMALL_NORM:base + SMALL_NORM + 1, :])
            update("ple_norm_g", row, total_ref[base + SMALL_PLE:base + SMALL_PLE + 1, :])
            update("ln_v_g", row, total_ref[base + SMALL_LN:base + SMALL_LN + 1, 0:WIDTH_A])
            update("ln_v_b", row, total_ref[base + SMALL_LN:base + SMALL_LN + 1, WIDTH_A:2 * WIDTH_A])
            for h in range(HEADS_A):
                update("b_s", (l, slice(h, h + 1), slice(None)),
                       total_ref[base + SMALL_BS:base + SMALL_BS + 1, h * HEAD_DIM:(h + 1) * HEAD_DIM])
                lanes = slice(l * WIDTH_A + h * CHUNK, l * WIDTH_A + (h + 1) * CHUNK)
                update("w_s", (l, h), jnp.where(tril, total_ref[TOTAL_WS:TOTAL_ROWS, lanes], 0.0))
        update("final_g", (slice(None), slice(None)), total_ref[TOTAL_HEAD + HEAD_FINAL:TOTAL_HEAD + HEAD_FINAL + 1, :])
        update("conv_w", (slice(None),) * 3, gconv_ref[...])
        update("w_ple_proj", (slice(None),) * 3, gproj_ref[...])

    flat = [a for nm in names for a in st[nm]]
    out_shape = []
    for nm in names:
        out_shape += [jax.ShapeDtypeStruct(st[nm][0].shape, F32)] * (4 if nm in cut else 3)
    def whole(a):
        return pl.BlockSpec(a.shape, lambda i: (0,) * len(a.shape))

    operands = [pltpu.with_memory_space_constraint(a, pltpu.HBM) for a in (total, g_conv, g_proj, *flat)]
    outs = pl.pallas_call(
        body,
        name="adamw_small",
        grid=(1,),
        out_shape=[pltpu.HBM(a.shape, a.dtype) for a in out_shape],
        in_specs=[whole(a) for a in operands],
        out_specs=[whole(a) for a in out_shape],
        compiler_params=pltpu.CompilerParams(dimension_semantics=("arbitrary",), vmem_limit_bytes=32 * MIB),
    )(*operands)
    res, pos = {}, 0
    for nm in names:
        cnt = 4 if nm in cut else 3
        got = tuple(outs[pos:pos + cnt])
        res[nm] = got if nm in cut else ((g_conv if nm == "conv_w" else g_proj),) + got
        pos += cnt
    return res


def _split3_bf16(a):
    b1 = a.astype(BF16)
    r1 = a - b1.astype(F32)
    b2 = r1.astype(BF16)
    b3 = (r1 - b2.astype(F32)).astype(BF16)
    return b1, b2, b3


def _pack_weight_shard(w_in_l, w_out_l, w_gate_l, w_proj_l, conv_w_l):
    w_in_t = jnp.transpose(w_in_l).astype(BF16)
    proj_t = jnp.transpose(w_proj_l).astype(BF16)
    proj_rows = proj_t.reshape(D_MODEL // PLE_DIM, ROWS_PROJ, PLE_DIM).transpose(1, 0, 2).reshape(ROWS_PROJ, D_MODEL)
    conv_parts = jnp.concatenate([b.reshape(-1) for b in _split3_bf16(conv_w_l)])
    conv_rows = jnp.concatenate([conv_parts, jnp.zeros((ROWS_CONV * D_MODEL - conv_parts.shape[0],), BF16)])
    return jnp.concatenate(
        [w_in_t, w_out_l.astype(BF16), w_gate_l.astype(BF16), proj_rows, conv_rows.reshape(ROWS_CONV, D_MODEL)], axis=0
    )


def _unpack_conv(wg):
    per_dev = wg.reshape(N_DEV, ROWS_LAYER, D_MODEL)
    n_conv = (WIDTH_B // N_DEV) * 3
    conv_parts = per_dev[:, OFF_CONV].astype(F32)[:, :3 * n_conv].reshape(N_DEV, 3, n_conv)
    conv = (conv_parts[:, 0] + conv_parts[:, 1]) + conv_parts[:, 2]
    conv_k = jnp.transpose(conv.reshape(WIDTH_B, 3))
    conv_k = jnp.concatenate([conv_k, jnp.zeros((5, WIDTH_B), F32)], axis=0)
    return conv_k


def _unpack_grad_proj(red):
    proj_rows = red[OFF_PROJ:OFF_PROJ + ROWS_PROJ]
    proj_t = proj_rows.reshape(ROWS_PROJ, D_MODEL // PLE_DIM, PLE_DIM).transpose(1, 0, 2).reshape(ROWS_OUT, PLE_DIM)
    return jnp.transpose(proj_t)


def kernel(x, p, norm_g, w_in, ln_v_g, ln_v_b, w_s, b_s, conv_w, w_out, ple_norm_g, w_ple_gate, w_ple_proj, final_g, loss_target, m_norm_g, m_w_in, m_ln_v_g, m_ln_v_b, m_w_s, m_b_s, m_conv_w, m_w_out, m_ple_norm_g, m_w_ple_gate, m_w_ple_proj, m_final_g, v_norm_g, v_w_in, v_ln_v_g, v_ln_v_b, v_w_s, v_b_s, v_conv_w, v_w_out, v_ple_norm_g, v_w_ple_gate, v_w_ple_proj, v_final_g):
    me = 4 * lax.axis_index("x") + 2 * lax.axis_index("y") + lax.axis_index("c")
    xs = x[0]
    target = loss_target[0]

    shards = [_pack_weight_shard(w_in[l], w_out[l], w_ple_gate[l], w_ple_proj[l], conv_w[l]) for l in range(DEPTH)]
    tril = jnp.tril(jnp.ones((CHUNK, CHUNK), F32))

    def consts(l, wg_l):
        conv_k = _unpack_conv(wg_l)
        w_mix = w_s[l] * tril[None]
        small = dict(
            conv_k=conv_k,
            norm_g=norm_g[l].reshape(1, D_MODEL), ln_g=ln_v_g[l].reshape(1, WIDTH_A), ln_b=ln_v_b[l].reshape(1, WIDTH_A),
            w_mix=w_mix.astype(BF16), w_mix_t=jnp.swapaxes(w_mix, 1, 2).astype(BF16),
            b_mix=jnp.broadcast_to(b_s[l][:, :, None], (HEADS_A, CHUNK, HEAD_DIM)),
            ple_g=ple_norm_g[l].reshape(1, D_MODEL),
        )
        return dict({k: pltpu.with_memory_space_constraint(a, pltpu.HBM) for k, a in small.items()}, wg=wg_l)

    layer_consts = [consts(0, _all_gather_rows(shards[0]))]
    saved = []
    h = xs
    for l in range(DEPTH):
        k = layer_consts[l]
        outs = _forward_layer(
            l, h, p, k["wg"], k["conv_k"], k["norm_g"], k["ln_g"], k["ln_b"], k["w_mix"], k["b_mix"],
            k["ple_g"], next_shard=shards[l + 1] if l + 1 < DEPTH else None)
        proj, hn, cat, r, gpre, x1, x2 = outs[:7]
        if l + 1 < DEPTH:
            layer_consts.append(consts(l + 1, outs[7]))
        saved.append(dict(x_in=h, proj=proj, hn=hn, cat=cat, r=r, gpre=gpre, x1=x1))
        h = x2

    smalls, dws = [None] * DEPTH, [None] * DEPTH
    reduced = [None] * DEPTH
    pending = None
    dx = h
    for l in reversed(range(DEPTH)):
        k, s = layer_consts[l], saved[l]
        outs = _backward_layer(
            l, dx, s["x_in"], s["x1"], s["proj"], s["gpre"], p, k["wg"], k["conv_k"],
            k["norm_g"], k["ln_g"], k["ln_b"], k["w_mix"], k["w_mix_t"], k["b_mix"], k["ple_g"],
            loss_head=(target, final_g.reshape(1, D_MODEL)) if l == DEPTH - 1 else None)
        dx, dproj, dx1, dgpre, dpp, smalls[l], dws[l] = outs[:7]
        if l == DEPTH - 1:
            head = outs[7]
        outs = _weight_grads(l, dproj, s["hn"], s["cat"], dx1, s["r"], dgpre, dpp, p, scatter_pack=pending)
        if pending is not None:
            reduced[l + 1] = _sum_pieces(l + 1, outs[1])
        pending = outs[0]
    reduced[0], total = _reduce_scatter_all_reduce(0, pending, smalls, head, dws)
    grad_x = dx[None]
    loss = total[TOTAL_HEAD + HEAD_LOSS, 0]

    n_ch = WIDTH_B // N_DEV
    g_conv = jnp.stack([total[l * SMALL_ROWS + SMALL_CONV:l * SMALL_ROWS + SMALL_CONV + 3, 0:WIDTH_B] for l in range(DEPTH)], axis=1)
    g_conv = lax.dynamic_slice_in_dim(g_conv, me * n_ch, n_ch, axis=2)
    g_proj = jnp.stack([_unpack_grad_proj(reduced[l]) for l in range(DEPTH)])

    def t_in(a):
        return jnp.swapaxes(a, 1, 2)

    def t_conv(a):
        return jnp.transpose(a, (2, 0, 1))

    (r_in,) = _adamw_rows("adamw_w_in", reduced, [OFF_IN], [(t_in(w_in), t_in(m_w_in), t_in(v_w_in))])
    r_out, r_gate = _adamw_rows(
        "adamw_w_out_gate", reduced, [OFF_OUT, OFF_GATE],
        [(w_out, m_w_out, v_w_out), (w_ple_gate, m_w_ple_gate, v_w_ple_gate)])
    small = _adamw_small(total, g_conv, g_proj, dict(
        norm_g=(norm_g, m_norm_g, v_norm_g), ple_norm_g=(ple_norm_g, m_ple_norm_g, v_ple_norm_g),
        ln_v_g=(ln_v_g, m_ln_v_g, v_ln_v_g), ln_v_b=(ln_v_b, m_ln_v_b, v_ln_v_b),
        b_s=(b_s, m_b_s, v_b_s), w_s=(w_s, m_w_s, v_w_s),
        final_g=tuple(a.reshape(1, D_MODEL) for a in (final_g, m_final_g, v_final_g)),
        conv_w=(t_conv(conv_w), t_conv(m_conv_w), t_conv(v_conv_w)),
        w_ple_proj=(w_ple_proj, m_w_ple_proj, v_w_ple_proj),
    ))
    res = dict(small, w_in=tuple(t_in(a) for a in r_in), w_out=r_out, w_ple_gate=r_gate)
    res["final_g"] = tuple(a.reshape(D_MODEL) for a in res["final_g"])
    res["conv_w"] = tuple(jnp.transpose(a, (1, 2, 0)) for a in res["conv_w"])
    order = ["norm_g", "w_in", "ln_v_g", "ln_v_b", "w_s", "b_s", "conv_w", "w_out", "ple_norm_g", "w_ple_gate", "w_ple_proj", "final_g"]
    return (loss, grad_x, *[res[n][0] for n in order], *[res[n][1] for n in order],
            *[res[n][2] for n in order], *[res[n][3] for n in order])
```

```python
import jax
import jax.numpy as jnp
from jax import lax
from jax.experimental import pallas as pl
from jax.experimental.pallas import tpu as pltpu

F32 = jnp.float32
BF16 = jnp.bfloat16

D_MODEL = 1024
WIDTH_A = 512
WIDTH_B = 512
HEADS_A = 4
HEAD_DIM = 128
CHUNK = 128
PLE_DIM = 256
PROJ_WIDTH = 3584
DEPTH = 2
EPS = 1e-6
N_DEV = 8

ADAM_LR = 0.001
ADAM_B1 = 0.9
ADAM_B2 = 0.999
ADAM_EPS = 1e-08
ADAM_WD = 0.01
ADAM_STEP = 10

ROWS_IN = PROJ_WIDTH // N_DEV
ROWS_OUT = D_MODEL // N_DEV
ROWS_GATE = D_MODEL // N_DEV
ROWS_PROJ = (D_MODEL // N_DEV) * PLE_DIM // D_MODEL
ROWS_CONV = 16
OFF_IN = 0
OFF_OUT = OFF_IN + ROWS_IN
OFF_GATE = OFF_OUT + ROWS_OUT
OFF_PROJ = OFF_GATE + ROWS_GATE
OFF_CONV = OFF_PROJ + ROWS_PROJ
ROWS_GRAD = OFF_CONV
ROWS_LAYER = OFF_CONV + ROWS_CONV

SMALL_ROWS = 8
SMALL_NORM = 0
SMALL_PLE = 1
SMALL_LN = 2
SMALL_BS = 3
SMALL_CONV = 4
HEAD_FINAL = 0
HEAD_LOSS = 1
TOTAL_HEAD = DEPTH * SMALL_ROWS
TOTAL_WS = TOTAL_HEAD + SMALL_ROWS
TOTAL_ROWS = TOTAL_WS + CHUNK

MIB = 1024 * 1024
MESH = pl.DeviceIdType.MESH

NT_DIMS = (((1,), (1,)), ((), ()))
TN_DIMS = (((0,), (0,)), ((), ()))


def _dot(a, b):
    return jnp.dot(a, b, preferred_element_type=F32)


def _dot_nt(a, b):
    return lax.dot_general(a, b, NT_DIMS, preferred_element_type=F32)


def _dot_tn(a, b):
    return lax.dot_general(a, b, TN_DIMS, preferred_element_type=F32)


def _colsum8(a):
    rows, n = a.shape
    return jnp.sum(a.reshape(rows // 8, 8, n), axis=0)


def _sigmoid(z):
    return 1.0 / (1.0 + jnp.exp(-z))


def _tile(t, want):
    return want if t % want == 0 else t


class _TwoLevelGather:
    def __init__(self, x_ref, out_ref, m_per, send_sems, recv_sems, local_sem):
        x, y, c = lax.axis_index("x"), lax.axis_index("y"), lax.axis_index("c")
        self.me, self.sibling = (x, y, c), (x, y, 1 - c)
        self.xn, self.yn, self.diag = (1 - x, y, c), (x, 1 - y, c), (1 - x, 1 - y, c)
        self.x_ref, self.out_ref, self.m_per = x_ref, out_ref, m_per
        self.half = (m_per // 32) * 16
        self.send_sems, self.recv_sems = send_sems, recv_sems
        self.mine = pltpu.make_async_copy(x_ref, self.rows(self.me), local_sem)

    def rows(self, block, part=None):
        px, py, pc = block
        base = (4 * px + 2 * py + pc) * self.m_per
        if part is None:
            return self.out_ref.at[pl.ds(base, self.m_per), :]
        if part == 0:
            return self.out_ref.at[pl.ds(base, self.half), :]
        return self.out_ref.at[pl.ds(base + self.half, self.m_per - self.half), :]

    def copy(self, k, block, to, src=None, part=None):
        return pltpu.make_async_remote_copy(
            src_ref=self.rows(block, part) if src is None else src,
            dst_ref=self.rows(block, part),
            send_sem=self.send_sems.at[k],
            recv_sem=self.recv_sems.at[k],
            device_id=to,
            device_id_type=MESH,
        )

    def first(self):
        return [self.copy(0, self.me, self.sibling, src=self.x_ref),
                self.copy(1, self.me, self.xn, src=self.x_ref),
                self.copy(2, self.me, self.yn, src=self.x_ref)]

    def second(self):
        return [self.copy(3, self.xn, self.yn, part=0), self.copy(7, self.yn, self.xn, part=1),
                self.copy(4, self.xn, self.sibling), self.copy(5, self.yn, self.sibling)]

    def third(self):
        return [self.copy(6, self.diag, self.sibling)]

    def start(self):
        self.mine.start()
        for cp in self.first():
            cp.start()

    def pass_on(self):
        fwd_x, fwd_y, sib_x, sib_y = self.second()
        self.copy(1, self.xn, self.me).wait_recv()
        fwd_x.start()
        sib_x.start()
        self.copy(2, self.yn, self.me).wait_recv()
        fwd_y.start()
        sib_y.start()

    def pass_on_diagonal(self):
        self.copy(3, self.diag, self.me, part=0).wait_recv()
        self.copy(7, self.diag, self.me, part=1).wait_recv()
        self.third()[0].start()

    def finish(self):
        sib = (self.sibling[0], self.sibling[1], self.sibling[2])
        self.copy(0, sib, self.me).wait_recv()
        for k, chip in ((4, self.xn), (5, self.yn), (6, self.diag)):
            self.copy(k, (chip[0], chip[1], sib[2]), self.me).wait_recv()
        for cp in self.first() + self.second() + self.third():
            cp.wait_send()
        self.mine.wait()


GATHER_SEMS = [pltpu.SemaphoreType.DMA((8,)), pltpu.SemaphoreType.DMA((8,)), pltpu.SemaphoreType.DMA]


def _all_gather_rows(shard):
    m_per, n = shard.shape

    def body(x_ref, out_ref, send_sems, recv_sems, local_sem):
        ag = _TwoLevelGather(x_ref, out_ref, m_per, send_sems, recv_sems, local_sem)
        ag.start()
        ag.pass_on()
        ag.pass_on_diagonal()
        ag.finish()

    return pl.pallas_call(
        body,
        name="weights_all_gather",
        out_shape=pltpu.HBM((N_DEV * m_per, n), shard.dtype),
        in_specs=[pl.BlockSpec(memory_space=pltpu.HBM)],
        out_specs=pl.BlockSpec(memory_space=pltpu.HBM),
        scratch_shapes=list(GATHER_SEMS),
    )(pltpu.with_memory_space_constraint(shard, pltpu.HBM))


PROJ_PARTS = D_MODEL // PLE_DIM
N_WEIGHT_COPIES = N_DEV * (3 + PROJ_PARTS)


def _weight_copies(wg_ref, w_in_t, w_out, w_gate, w_proj_t, sems):
    copies = []
    for s in range(N_DEV):
        base = s * ROWS_LAYER
        for dst, off, rows in ((w_in_t, OFF_IN, ROWS_IN), (w_out, OFF_OUT, ROWS_OUT), (w_gate, OFF_GATE, ROWS_GATE)):
            copies.append((wg_ref.at[pl.ds(base + off, rows), :], dst.at[pl.ds(s * rows, rows), :]))
        for j in range(PROJ_PARTS):
            copies.append((
                wg_ref.at[pl.ds(base + OFF_PROJ, ROWS_PROJ), pl.ds(j * PLE_DIM, PLE_DIM)],
                w_proj_t.at[pl.ds(s * ROWS_OUT + j * ROWS_PROJ, ROWS_PROJ), :],
            ))
    return [pltpu.make_async_copy(src, dst, sems.at[k]) for k, (src, dst) in enumerate(copies)]


def _forward_layer(layer, x, p_all, wg, conv_k, norm_g, ln_g, ln_b, w_mix, b_mix, ple_g, next_shard=None):
    t = x.shape[0]
    tm = _tile(t, 512)
    nt = t // tm
    gathers = next_shard is not None

    def body(*refs):
        (x_ref, p_ref, wg_ref, cw_ref, ng_ref, lng_ref, lnb_ref, wm_ref, bm_ref, pg_ref) = refs[:10]
        refs = refs[10:]
        if gathers:
            shard_ref, refs = refs[0], refs[1:]
        (proj_ref, hn_ref, cat_ref, r_ref, gpre_ref, x1_ref, x2_ref) = refs[:7]
        refs = refs[7:]
        if gathers:
            gathered_ref, refs = refs[0], refs[1:]
        (w_in_t, w_out, w_gate, wpt_ref, vln_s, mixed_s, halo_s, sems) = refs[:8]
        i = pl.program_id(0)
        if gathers:
            ag = _TwoLevelGather(shard_ref, gathered_ref, ROWS_LAYER, *refs[8:11])

            @pl.when(i == 0)
            def _():
                ag.start()

            @pl.when(i == (5 * nt) // 16)
            def _():
                ag.pass_on()

            @pl.when(i == nt // 2)
            def _():
                ag.pass_on_diagonal()

        @pl.when(i == 0)
        def _():
            copies = _weight_copies(wg_ref, w_in_t, w_out, w_gate, wpt_ref, sems)
            for cp in copies:
                cp.start()
            halo_s[...] = jnp.zeros_like(halo_s)
            for cp in copies:
                cp.wait()

        xv = x_ref[...]
        rstd0 = lax.rsqrt(jnp.mean(xv * xv, axis=-1, keepdims=True) + EPS)
        hn_ref[...] = (xv * rstd0 * ng_ref[...]).astype(BF16)

        def proj_section(k):
            sec = _dot_nt(hn_ref[...], w_in_t[pl.ds(k * 512, 512), :])
            proj_ref[:, k * 512:(k + 1) * 512] = sec.astype(BF16)
            return sec

        v = proj_section(1)
        mu = jnp.mean(v, axis=-1, keepdims=True)
        vc = v - mu
        var = jnp.mean(vc * vc, axis=-1, keepdims=True)
        vln = vc * lax.rsqrt(var + EPS) * lng_ref[...] + lnb_ref[...]
        vln_s[...] = vln.astype(BF16)
        for ci in range(tm // CHUNK):
            rows = pl.ds(ci * CHUNK, CHUNK)
            for h in range(HEADS_A):
                cols = pl.ds(h * HEAD_DIM, HEAD_DIM)
                mixed_s[rows, cols] = _dot(wm_ref[h], vln_s[rows, cols]) + bm_ref[h]
        u = proj_section(0)
        za = proj_section(2)
        out_a = u * mixed_s[...] * (za * _sigmoid(za))
        cat_ref[:, 0:512] = out_a.astype(BF16)

        xc = proj_section(5) * proj_section(3)
        prev = halo_s[...]
        row = lax.broadcasted_iota(jnp.int32, (tm, WIDTH_B), 0)
        xc_m1 = jnp.where(row == 0, prev[7:8, :], pltpu.roll(xc, 1, 0))
        xc_m2 = jnp.where(row == 0, prev[6:7, :], jnp.where(row == 1, prev[7:8, :], pltpu.roll(xc, 2, 0)))
        halo_s[...] = xc[tm - 8:tm, :]
        cw = cw_ref[...]
        yc = cw[0:1, :] * xc_m2 + cw[1:2, :] * xc_m1 + cw[2:3, :] * xc
        zb = proj_section(6)
        out_b = proj_section(4) * yc * (zb * _sigmoid(zb))
        cat_ref[:, 512:1024] = out_b.astype(BF16)

        x1 = xv + _dot(cat_ref[...], w_out[...])
        x1_ref[...] = x1
        rstd1 = lax.rsqrt(jnp.mean(x1 * x1, axis=-1, keepdims=True) + EPS)
        r_ref[...] = (x1 * rstd1 * pg_ref[...]).astype(BF16)
        gpre = _dot(r_ref[...], w_gate[...])
        gpre_ref[...] = gpre.astype(BF16)
        pp = _dot_nt(p_ref[...].astype(BF16), wpt_ref[...])
        x2_ref[...] = x1 + _sigmoid(gpre) * pp

        if gathers:
            @pl.when(i == nt - 1)
            def _():
                ag.finish()

    def tok(width):
        return pl.BlockSpec((tm, width), lambda i: (i, 0))

    def whole(shape):
        return pl.BlockSpec(shape, lambda i: (0,) * len(shape))

    hbm = pl.BlockSpec(memory_space=pl.ANY)
    operands = [x, p_all, wg, conv_k, norm_g, ln_g, ln_b, w_mix, b_mix, ple_g]
    in_specs = [
        tok(D_MODEL), pl.BlockSpec((None, None, tm, PLE_DIM), lambda i: (layer, 0, i, 0)), hbm,
        whole((8, WIDTH_B)), whole((1, D_MODEL)), whole((1, WIDTH_A)), whole((1, WIDTH_A)),
        whole((HEADS_A, CHUNK, CHUNK)), whole((HEADS_A, CHUNK, HEAD_DIM)), whole((1, D_MODEL)),
    ]
    out_specs = [tok(PROJ_WIDTH), tok(D_MODEL), tok(D_MODEL), tok(D_MODEL), tok(D_MODEL), tok(D_MODEL), tok(D_MODEL)]
    out_shape = [
        jax.ShapeDtypeStruct((t, PROJ_WIDTH), BF16),
        jax.ShapeDtypeStruct((t, D_MODEL), BF16),
        jax.ShapeDtypeStruct((t, D_MODEL), BF16),
        jax.ShapeDtypeStruct((t, D_MODEL), BF16),
        jax.ShapeDtypeStruct((t, D_MODEL), BF16),
        jax.ShapeDtypeStruct((t, D_MODEL), F32),
        jax.ShapeDtypeStruct((t, D_MODEL), F32),
    ]
    scratch_shapes = [
        pltpu.VMEM((PROJ_WIDTH, D_MODEL), BF16),
        pltpu.VMEM((D_MODEL, D_MODEL), BF16),
        pltpu.VMEM((D_MODEL, D_MODEL), BF16),
        pltpu.VMEM((D_MODEL, PLE_DIM), BF16),
        pltpu.VMEM((tm, WIDTH_A), BF16),
        pltpu.VMEM((tm, WIDTH_A), F32),
        pltpu.VMEM((8, WIDTH_B), F32),
        pltpu.SemaphoreType.DMA((N_WEIGHT_COPIES,)),
    ]
    if gathers:
        operands.append(pltpu.with_memory_space_constraint(next_shard, pltpu.HBM))
        in_specs.append(pl.BlockSpec(memory_space=pltpu.HBM))
        out_specs.append(pl.BlockSpec(memory_space=pltpu.HBM))
        out_shape.append(pltpu.HBM((N_DEV * ROWS_LAYER, D_MODEL), BF16))
        scratch_shapes += list(GATHER_SEMS)

    return pl.pallas_call(
        body,
        name=f"layer{layer}_forward",
        grid=(nt,),
        in_specs=in_specs,
        out_specs=out_specs,
        out_shape=out_shape,
        scratch_shapes=scratch_shapes,
        compiler_params=pltpu.CompilerParams(dimension_semantics=("arbitrary",), vmem_limit_bytes=56 * MIB),
    )(*operands)


class _DirectScatter:
    def __init__(self, pack_ref, pieces_ref, send_sems, recv_sems, local_sem):
        x, y, c = lax.axis_index("x"), lax.axis_index("y"), lax.axis_index("c")
        me = 4 * x + 2 * y + c
        self.copies = []
        for k in range(N_DEV - 1):
            fx, fy, fc = ((k + 1) >> 2) & 1, ((k + 1) >> 1) & 1, (k + 1) & 1
            tx, ty, tc = x ^ fx, y ^ fy, c ^ fc
            self.copies.append(
                pltpu.make_async_remote_copy(
                    src_ref=pack_ref.at[4 * tx + 2 * ty + tc], dst_ref=pieces_ref.at[me],
                    send_sem=send_sems.at[k], recv_sem=recv_sems.at[k],
                    device_id=(tx, ty, tc), device_id_type=MESH,
                )
            )
        self.mine = pltpu.make_async_copy(pack_ref.at[me], pieces_ref.at[me], local_sem)

    def start(self):
        self.mine.start()
        for cp in self.copies:
            cp.start()

    def finish(self):
        for cp in self.copies:
            cp.wait_recv()
        for cp in self.copies:
            cp.wait_send()
        self.mine.wait()


SCATTER_SEMS = [pltpu.SemaphoreType.DMA((N_DEV - 1,)), pltpu.SemaphoreType.DMA((N_DEV - 1,)), pltpu.SemaphoreType.DMA]


def _backward_layer(layer, dx2, x_in, x1, proj, gpre, p_all, wg, conv_k, norm_g, ln_g, ln_b,
                    w_mix, w_mix_t, b_mix, ple_g, loss_head=None):
    t = x_in.shape[0]
    tm = _tile(t, 256)
    nt = t // tm
    n_chunks = tm // CHUNK
    halo_rows = 16
    heads = loss_head is not None

    def body(*refs):
        (dx2_ref, xin_ref, x1_ref, proj_ref, halo_ref, gpre_ref, p_ref, wg_ref, cw_ref,
         ng_ref, lng_ref, lnb_ref, wm_ref, wmt_ref, bm_ref, pg_ref) = refs[:16]
        refs = refs[16:]
        if heads:
            tgt_ref, fg_ref = refs[:2]
            refs = refs[2:]
        (dxin_ref, dproj_ref, dx1_ref, dgpre_ref, dpp_ref, small_ref, dws_ref) = refs[:7]
        refs = refs[7:]
        if heads:
            head_ref, refs = refs[0], refs[1:]
        (w_in_t, w_out, w_gate, wpt_ref, vln_s, mixed_s, dmix_s, dvln_s, carry_s,
         ng_acc, pg_acc, lng_acc, lnb_acc, cw_acc, dbm_ref, sems) = refs[:16]
        if heads:
            loss_acc, fg_acc = refs[16:18]
        i = pl.program_id(0)
        tile = nt - 1 - i

        @pl.when(i == 0)
        def _():
            copies = _weight_copies(wg_ref, w_in_t, w_out, w_gate, wpt_ref, sems)
            for cp in copies:
                cp.start()
            if heads:
                loss_acc[...] = jnp.zeros_like(loss_acc)
                fg_acc[...] = jnp.zeros_like(fg_acc)
            carry_s[...] = jnp.zeros_like(carry_s)
            ng_acc[...] = jnp.zeros_like(ng_acc)
            pg_acc[...] = jnp.zeros_like(pg_acc)
            lng_acc[...] = jnp.zeros_like(lng_acc)
            lnb_acc[...] = jnp.zeros_like(lnb_acc)
            cw_acc[...] = jnp.zeros_like(cw_acc)
            dws_ref[...] = jnp.zeros_like(dws_ref)
            dbm_ref[...] = jnp.zeros_like(dbm_ref)
            for cp in copies:
                cp.wait()

        if heads:
            x2v = dx2_ref[...]
            fg = fg_ref[...]
            rstdf = lax.rsqrt(jnp.mean(x2v * x2v, axis=-1, keepdims=True) + EPS)
            xhatf = x2v * rstdf
            err = xhatf * fg - tgt_ref[...]
            loss_acc[...] += _colsum8(err * err)
            dy = err * (1.0 / D_MODEL)
            fg_acc[...] += _colsum8(dy * xhatf)
            dxhf = dy * fg
            dx2v = rstdf * (dxhf - xhatf * jnp.mean(dxhf * xhatf, axis=-1, keepdims=True))
        else:
            dx2v = dx2_ref[...]

        gate = _sigmoid(gpre_ref[...].astype(F32))
        pp = _dot_nt(p_ref[...].astype(BF16), wpt_ref[...])
        dpp = dx2v * gate
        dpp_ref[...] = dpp.astype(BF16)
        dgpre = (dpp * pp * (1.0 - gate)).astype(BF16)
        dgpre_ref[...] = dgpre
        dr = _dot_nt(dgpre, w_gate[...])
        x1v = x1_ref[...]
        rstd1 = lax.rsqrt(jnp.mean(x1v * x1v, axis=-1, keepdims=True) + EPS)
        xhat1 = x1v * rstd1
        pg_acc[...] += _colsum8(dr * xhat1)
        dxh = dr * pg_ref[...]
        dx1 = dx2v + rstd1 * (dxh - xhat1 * jnp.mean(dxh * xhat1, axis=-1, keepdims=True))
        dx1b = dx1.astype(BF16)
        dx1_ref[...] = dx1b

        dcat = _dot_nt(dx1b, w_out[...])
        dca = dcat[:, 0:512]
        dcb = dcat[:, 512:1024]

        u = proj_ref[:, 0:512]
        v = proj_ref[:, 512:1024].astype(F32)
        za = proj_ref[:, 1024:1536]
        mu = jnp.mean(v, axis=-1, keepdims=True)
        vc = v - mu
        var = jnp.mean(vc * vc, axis=-1, keepdims=True)
        rs = lax.rsqrt(var + EPS)
        vhat = vc * rs
        lng = lng_ref[...]
        vln_s[...] = (vhat * lng + lnb_ref[...]).astype(BF16)
        for ci in range(n_chunks):
            rows = pl.ds(ci * CHUNK, CHUNK)
            for h in range(HEADS_A):
                cols = pl.ds(h * HEAD_DIM, HEAD_DIM)
                mixed_s[rows, cols] = (_dot(wm_ref[h], vln_s[rows, cols]) + bm_ref[h]).astype(BF16)
        mixed = mixed_s[...]
        sga = _sigmoid(za)
        sa = za * sga
        dsa = sga + sa * (1.0 - sga)

        def put_section(k, val):
            dproj_ref[:, k * 512:(k + 1) * 512] = val.astype(BF16)

        dcab = dca.astype(BF16)
        dca_sa = dcab * sa
        put_section(0, dca_sa * mixed)
        dmix_s[...] = dca_sa * u
        put_section(2, (dcab * dsa) * (u * mixed))
        dbm_acc = jnp.zeros((CHUNK, WIDTH_A), F32)
        for ci in range(n_chunks):
            rows = pl.ds(ci * CHUNK, CHUNK)
            dbm_acc = dbm_acc + dmix_s[rows, :].astype(F32)
            for h in range(HEADS_A):
                cols = pl.ds(h * HEAD_DIM, HEAD_DIM)
                dvln_s[rows, cols] = _dot(wmt_ref[h], dmix_s[rows, cols])
                dws_ref[:, cols] += _dot_nt(dmix_s[rows, cols], vln_s[rows, cols])
        dbm_ref[...] += dbm_acc
        dvln = dvln_s[...]
        lng_acc[...] += _colsum8(dvln * vhat)
        lnb_acc[...] += _colsum8(dvln)
        dvh = dvln * lng
        dv = rs * (dvh - jnp.mean(dvh, axis=-1, keepdims=True) - vhat * jnp.mean(dvh * vhat, axis=-1, keepdims=True))
        put_section(1, dv)

        hb = proj_ref[:, 1536:2048].astype(F32)
        gb = proj_ref[:, 2048:2560]
        gc = proj_ref[:, 2560:3072].astype(F32)
        zb = proj_ref[:, 3072:3584]
        xc = gc * hb
        prev = halo_ref[:, 2560:3072].astype(F32) * halo_ref[:, 1536:2048].astype(F32)
        prev = jnp.where(tile > 0, prev, 0.0)
        row = lax.broadcasted_iota(jnp.int32, (tm, WIDTH_B), 0)
        p1 = prev[halo_rows - 1:halo_rows, :]
        p2 = prev[halo_rows - 2:halo_rows - 1, :]
        xc_m1 = jnp.where(row == 0, p1, pltpu.roll(xc, 1, 0))
        xc_m2 = jnp.where(row == 0, p2, jnp.where(row == 1, p1, pltpu.roll(xc, 2, 0)))
        cw = cw_ref[...]
        yc = cw[0:1, :] * xc_m2 + cw[1:2, :] * xc_m1 + cw[2:3, :] * xc
        sgb = _sigmoid(zb)
        sb = zb * sgb
        dsb = sgb + sb * (1.0 - sgb)
        dcbb = dcb.astype(BF16)
        ycb = yc.astype(BF16)
        dcb_sb = dcbb * sb
        put_section(4, dcb_sb * ycb)
        dyc = (dcb_sb * gb).astype(F32)
        put_section(6, (dcbb * dsb) * (gb * ycb))
        nxt = carry_s[...]
        dyc_p1 = jnp.where(row == tm - 1, nxt[0:1, :], pltpu.roll(dyc, tm - 1, 0))
        dyc_p2 = jnp.where(row == tm - 1, nxt[1:2, :], jnp.where(row == tm - 2, nxt[0:1, :], pltpu.roll(dyc, tm - 2, 0)))
        carry_s[...] = dyc[0:8, :]
        dxc = cw[2:3, :] * dyc + cw[1:2, :] * dyc_p1 + cw[0:1, :] * dyc_p2
        cw_acc[0] += _colsum8(dyc * xc_m2)
        cw_acc[1] += _colsum8(dyc * xc_m1)
        cw_acc[2] += _colsum8(dyc * xc)
        put_section(3, dxc * gc)
        put_section(5, dxc * hb)

        dhn = _dot(dproj_ref[...], w_in_t[...])
        xv = xin_ref[...]
        rstd0 = lax.rsqrt(jnp.mean(xv * xv, axis=-1, keepdims=True) + EPS)
        xhat0 = xv * rstd0
        ng_acc[...] += _colsum8(dhn * xhat0)
        dxh0 = dhn * ng_ref[...]
        dxin_ref[...] = dx1 + rstd0 * (dxh0 - xhat0 * jnp.mean(dxh0 * xhat0, axis=-1, keepdims=True))

        @pl.when(i == nt - 1)
        def _():
            small_ref[...] = jnp.zeros_like(small_ref)
            small_ref[SMALL_NORM:SMALL_NORM + 1, :] = jnp.sum(ng_acc[...], axis=0, keepdims=True)
            small_ref[SMALL_PLE:SMALL_PLE + 1, :] = jnp.sum(pg_acc[...], axis=0, keepdims=True)
            small_ref[SMALL_LN:SMALL_LN + 1, 0:WIDTH_A] = jnp.sum(lng_acc[...], axis=0, keepdims=True)
            small_ref[SMALL_LN:SMALL_LN + 1, WIDTH_A:2 * WIDTH_A] = jnp.sum(lnb_acc[...], axis=0, keepdims=True)
            for h in range(HEADS_A):
                cols = pl.ds(h * HEAD_DIM, HEAD_DIM)
                small_ref[SMALL_BS:SMALL_BS + 1, cols] = jnp.sum(jnp.transpose(dbm_ref[:, cols]), axis=0, keepdims=True)
            for k in range(3):
                small_ref[SMALL_CONV + k:SMALL_CONV + k + 1, 0:WIDTH_B] = jnp.sum(cw_acc[k], axis=0, keepdims=True)
            if heads:
                total = jnp.sum(loss_acc[...]) * (0.5 / D_MODEL)
                rows8 = lax.broadcasted_iota(jnp.int32, (SMALL_ROWS, D_MODEL), 0)
                lanes8 = lax.broadcasted_iota(jnp.int32, (SMALL_ROWS, D_MODEL), 1)
                head_ref[...] = jnp.where((rows8 == HEAD_LOSS) & (lanes8 == 0), total, 0.0)
                head_ref[HEAD_FINAL:HEAD_FINAL + 1, :] = jnp.sum(fg_acc[...], axis=0, keepdims=True)

    def tok(width):
        return pl.BlockSpec((tm, width), lambda i: (nt - 1 - i, 0))

    def whole(shape):
        return pl.BlockSpec(shape, lambda i: (0,) * len(shape))

    halo_spec = pl.BlockSpec(
        (halo_rows, PROJ_WIDTH), lambda i: (jnp.maximum((nt - 1 - i) * (tm // halo_rows) - 1, 0), 0)
    )
    hbm = pl.BlockSpec(memory_space=pl.ANY)
    operands = [dx2, x_in, x1, proj, proj, gpre, p_all, wg, conv_k, norm_g, ln_g, ln_b, w_mix, w_mix_t, b_mix, ple_g]
    in_specs = [
        tok(D_MODEL), tok(D_MODEL), tok(D_MODEL), tok(PROJ_WIDTH), halo_spec, tok(D_MODEL),
        pl.BlockSpec((None, None, tm, PLE_DIM), lambda i: (layer, 0, nt - 1 - i, 0)), hbm,
        whole((8, WIDTH_B)), whole((1, D_MODEL)), whole((1, WIDTH_A)), whole((1, WIDTH_A)),
        whole((HEADS_A, CHUNK, CHUNK)), whole((HEADS_A, CHUNK, CHUNK)), whole((HEADS_A, CHUNK, HEAD_DIM)),
        whole((1, D_MODEL)),
    ]
    out_specs = [
        tok(D_MODEL), tok(PROJ_WIDTH), tok(D_MODEL), tok(D_MODEL), tok(D_MODEL),
        whole((SMALL_ROWS, D_MODEL)), whole((CHUNK, WIDTH_A)),
    ]
    out_shape = [
        jax.ShapeDtypeStruct((t, D_MODEL), F32),
        jax.ShapeDtypeStruct((t, PROJ_WIDTH), BF16),
        jax.ShapeDtypeStruct((t, D_MODEL), BF16),
        jax.ShapeDtypeStruct((t, D_MODEL), BF16),
        jax.ShapeDtypeStruct((t, D_MODEL), BF16),
        jax.ShapeDtypeStruct((SMALL_ROWS, D_MODEL), F32),
        jax.ShapeDtypeStruct((CHUNK, WIDTH_A), F32),
    ]
    scratch_shapes = [
        pltpu.VMEM((PROJ_WIDTH, D_MODEL), BF16),
        pltpu.VMEM((D_MODEL, D_MODEL), BF16),
        pltpu.VMEM((D_MODEL, D_MODEL), BF16),
        pltpu.VMEM((D_MODEL, PLE_DIM), BF16),
        pltpu.VMEM((tm, WIDTH_A), BF16),
        pltpu.VMEM((tm, WIDTH_A), BF16),
        pltpu.VMEM((tm, WIDTH_A), BF16),
        pltpu.VMEM((tm, WIDTH_A), F32),
        pltpu.VMEM((8, WIDTH_B), F32),
        pltpu.VMEM((8, D_MODEL), F32),
        pltpu.VMEM((8, D_MODEL), F32),
        pltpu.VMEM((8, WIDTH_A), F32),
        pltpu.VMEM((8, WIDTH_A), F32),
        pltpu.VMEM((3, 8, WIDTH_B), F32),
        pltpu.VMEM((CHUNK, WIDTH_A), F32),
        pltpu.SemaphoreType.DMA((N_WEIGHT_COPIES,)),
    ]
    if heads:
        operands += list(loss_head)
        in_specs += [tok(D_MODEL), whole((1, D_MODEL))]
        out_specs.append(whole((SMALL_ROWS, D_MODEL)))
        out_shape.append(jax.ShapeDtypeStruct((SMALL_ROWS, D_MODEL), F32))
        scratch_shapes += [pltpu.VMEM((8, D_MODEL), F32), pltpu.VMEM((8, D_MODEL), F32)]

    return pl.pallas_call(
        body,
        name=f"layer{layer}_backward",
        grid=(nt,),
        in_specs=in_specs,
        out_specs=out_specs,
        out_shape=out_shape,
        scratch_shapes=scratch_shapes,
        compiler_params=pltpu.CompilerParams(dimension_semantics=("arbitrary",), vmem_limit_bytes=56 * MIB),
    )(*operands)


def _sum_pieces(layer, pieces):
    rows, n = pieces.shape[1], pieces.shape[2]
    blocks = 2
    rb = rows // blocks

    def body(p_ref, out_ref):
        total = p_ref[0].astype(F32)
        for j in range(1, N_DEV):
            total = total + p_ref[j].astype(F32)
        out_ref[...] = total

    return pl.pallas_call(
        body,
        name=f"layer{layer}_grad_sum",
        grid=(blocks,),
        out_shape=pltpu.HBM((rows, n), F32),
        in_specs=[pl.BlockSpec((N_DEV, rb, n), lambda i: (0, i, 0))],
        out_specs=pl.BlockSpec((rb, n), lambda i: (i, 0)),
        compiler_params=pltpu.CompilerParams(dimension_semantics=("arbitrary",), vmem_limit_bytes=32 * MIB),
    )(pieces)


def _weight_grads(layer, dproj, hn, cat, dx1, r, dgpre, dpp, p_all, scatter_pack=None):
    t = hn.shape[0]
    tk = _tile(t, 512)
    nt = t // tk
    in_blocks = PROJ_WIDTH // 512
    scatters = scatter_pack is not None

    def body(*refs):
        (dproj_ref, hn_ref, cat_ref, dx1_ref, r_ref, dgpre_ref, dpp_ref, p_ref) = refs[:8]
        refs = refs[8:]
        if scatters:
            prior_ref, refs = refs[0], refs[1:]
        pack_ref, refs = refs[0], refs[1:]
        if scatters:
            pieces_ref, refs = refs[0], refs[1:]
        (acc_in, acc_out, acc_gate, acc_proj, stage, sems) = refs[:6]
        i = pl.program_id(0)
        if scatters:
            scatter = _DirectScatter(prior_ref, pieces_ref, *refs[6:9])

            @pl.when(i == 0)
            def _():
                scatter.start()

        @pl.when(i == 0)
        def _():
            acc_in[...] = jnp.zeros_like(acc_in)
            acc_out[...] = jnp.zeros_like(acc_out)
            acc_gate[...] = jnp.zeros_like(acc_gate)
            acc_proj[...] = jnp.zeros_like(acc_proj)

        hnv = hn_ref[...]
        for b in range(in_blocks):
            acc_in[pl.ds(b * 512, 512), :] += _dot_tn(dproj_ref[:, b * 512:(b + 1) * 512], hnv)
        dx1v = dx1_ref[...]
        dgv = dgpre_ref[...]
        for b in range(D_MODEL // 512):
            acc_out[pl.ds(b * 512, 512), :] += _dot_tn(cat_ref[:, b * 512:(b + 1) * 512], dx1v)
            acc_gate[pl.ds(b * 512, 512), :] += _dot_tn(r_ref[:, b * 512:(b + 1) * 512], dgv)
        pv = p_ref[...].astype(BF16)
        for b in range(D_MODEL // 512):
            acc_proj[pl.ds(b * 512, 512), :] += _dot_tn(dpp_ref[:, b * 512:(b + 1) * 512], pv)

        @pl.when(i == nt - 1)
        def _():
            def out_copy(s):
                return pltpu.make_async_copy(stage.at[s % 2], pack_ref.at[s], sems.at[s % 2])

            for s in range(N_DEV):
                if s >= 2:
                    out_copy(s - 2).wait()
                buf = stage.at[s % 2]
                buf[pl.ds(OFF_IN, ROWS_IN), :] = acc_in[pl.ds(s * ROWS_IN, ROWS_IN), :].astype(BF16)
                buf[pl.ds(OFF_OUT, ROWS_OUT), :] = acc_out[pl.ds(s * ROWS_OUT, ROWS_OUT), :].astype(BF16)
                buf[pl.ds(OFF_GATE, ROWS_GATE), :] = acc_gate[pl.ds(s * ROWS_GATE, ROWS_GATE), :].astype(BF16)
                for j in range(D_MODEL // PLE_DIM):
                    buf[pl.ds(OFF_PROJ, ROWS_PROJ), pl.ds(j * PLE_DIM, PLE_DIM)] = acc_proj[
                        pl.ds(s * ROWS_OUT + j * ROWS_PROJ, ROWS_PROJ), :
                    ].astype(BF16)
                out_copy(s).start()
            out_copy(N_DEV - 2).wait()
            out_copy(N_DEV - 1).wait()
            if scatters:
                scatter.finish()

    def tok(width):
        return pl.BlockSpec((tk, width), lambda i: (i, 0))

    hbm = pl.BlockSpec(memory_space=pl.ANY)
    pack_shape = jax.ShapeDtypeStruct((N_DEV, ROWS_GRAD, D_MODEL), BF16)
    operands = [dproj, hn, cat, dx1, r, dgpre, dpp, p_all]
    in_specs = [tok(PROJ_WIDTH), tok(D_MODEL), tok(D_MODEL), tok(D_MODEL), tok(D_MODEL), tok(D_MODEL), tok(D_MODEL),
                pl.BlockSpec((None, None, tk, PLE_DIM), lambda i: (layer, 0, i, 0))]
    out_specs, out_shape = [hbm], [pack_shape]
    scratch_shapes = [
        pltpu.VMEM((PROJ_WIDTH, D_MODEL), F32),
        pltpu.VMEM((D_MODEL, D_MODEL), F32),
        pltpu.VMEM((D_MODEL, D_MODEL), F32),
        pltpu.VMEM((D_MODEL, PLE_DIM), F32),
        pltpu.VMEM((2, ROWS_GRAD, D_MODEL), BF16),
        pltpu.SemaphoreType.DMA((2,)),
    ]
    if scatters:
        operands.append(scatter_pack)
        in_specs.append(hbm)
        out_specs.append(hbm)
        out_shape.append(pack_shape)
        scratch_shapes += list(SCATTER_SEMS)

    return pl.pallas_call(
        body,
        name=f"layer{layer}_weight_grads",
        grid=(nt,),
        in_specs=in_specs,
        out_specs=out_specs,
        out_shape=out_shape,
        scratch_shapes=scratch_shapes,
        compiler_params=pltpu.CompilerParams(dimension_semantics=("arbitrary",), vmem_limit_bytes=58 * MIB),
    )(*operands)


def _reduce_scatter_all_reduce(layer, pack, smalls, head, dws):
    rows, n = pack.shape[1], pack.shape[2]
    assert DEPTH * WIDTH_A == D_MODEL and n == D_MODEL

    def body(g_ref, *refs):
        small_refs, refs = refs[:DEPTH], refs[DEPTH:]
        head_ref, refs = refs[0], refs[1:]
        dws_refs, refs = refs[:DEPTH], refs[DEPTH:]
        (out_ref, total_ref, r1, a_s, r2, sp, sr1, sq, send1, recv1, send2, recv2, ssend, srecv) = refs
        x, y, c = lax.axis_index("x"), lax.axis_index("y"), lax.axis_index("c")
        sibling = (x, y, 1 - c)
        chip = 2 * x + y
        flips = [(1, 0), (0, 1), (1, 1)]

        for l in range(DEPTH):
            sp[l * SMALL_ROWS:(l + 1) * SMALL_ROWS, :] = small_refs[l][...]
            sp[TOTAL_WS:TOTAL_ROWS, l * WIDTH_A:(l + 1) * WIDTH_A] = dws_refs[l][...]
        sp[TOTAL_HEAD:TOTAL_WS, :] = head_ref[...]

        small_pair = pltpu.make_async_remote_copy(
            src_ref=sp, dst_ref=sr1, send_sem=ssend.at[0], recv_sem=srecv.at[0], device_id=sibling, device_id_type=MESH
        )

        def to_sibling(j):
            return pltpu.make_async_remote_copy(
                src_ref=g_ref.at[2 * j + 1 - c], dst_ref=r1.at[j], send_sem=send1.at[j], recv_sem=recv1.at[j],
                device_id=sibling, device_id_type=MESH,
            )

        first = [to_sibling(j) for j in range(4)]
        small_pair.start()
        for cp in first:
            cp.start()

        small_pair.wait_recv()
        sq[chip] = sp[...] + sr1[...]
        small_chips = [
            pltpu.make_async_remote_copy(
                src_ref=sq.at[chip], dst_ref=sq.at[chip], send_sem=ssend.at[1 + k], recv_sem=srecv.at[1 + k],
                device_id=(x ^ fx, y ^ fy, c), device_id_type=MESH,
            )
            for k, (fx, fy) in enumerate(flips)
        ]
        for cp in small_chips:
            cp.start()

        def to_chip(j):
            return pltpu.make_async_remote_copy(
                src_ref=a_s.at[j], dst_ref=r2.at[chip], send_sem=send2.at[j], recv_sem=recv2.at[chip],
                device_id=(j // 2, j % 2, c), device_id_type=MESH,
            )

        def from_chip(k):
            return pltpu.make_async_remote_copy(
                src_ref=a_s.at[k], dst_ref=r2.at[k], send_sem=send2.at[k], recv_sem=recv2.at[k],
                device_id=(k // 2, k % 2, c), device_id_type=MESH,
            )

        for j in range(4):
            first[j].wait_recv()

            @pl.when(chip != j)
            def _():
                a_s[j] = (g_ref[2 * j + c].astype(F32) + r1[j].astype(F32)).astype(BF16)
                to_chip(j).start()

        out_ref[...] = g_ref[2 * chip + c].astype(F32) + r1[chip].astype(F32)
        for cp in small_chips:
            cp.wait_recv()
        total_ref[...] = ((sq[0] + sq[1]) + sq[2]) + sq[3]
        for k in range(4):
            @pl.when(chip != k)
            def _():
                from_chip(k).wait_recv()
                out_ref[...] += r2[k].astype(F32)
        small_pair.wait_send()
        for cp in first + small_chips:
            cp.wait_send()
        for j in range(4):
            @pl.when(chip != j)
            def _():
                to_chip(j).wait_send()

    vmem = pl.BlockSpec(memory_space=pltpu.VMEM)
    return pl.pallas_call(
        body,
        name=f"layer{layer}_grad_reduce_scatter",
        out_shape=[jax.ShapeDtypeStruct((rows, n), F32), jax.ShapeDtypeStruct((TOTAL_ROWS, D_MODEL), F32)],
        in_specs=[vmem] * (2 + 2 * DEPTH),
        out_specs=[vmem, vmem],
        scratch_shapes=[
            pltpu.VMEM((4, rows, n), BF16),
            pltpu.VMEM((4, rows, n), BF16),
            pltpu.VMEM((4, rows, n), BF16),
            pltpu.VMEM((TOTAL_ROWS, D_MODEL), F32),
            pltpu.VMEM((TOTAL_ROWS, D_MODEL), F32),
            pltpu.VMEM((4, TOTAL_ROWS, D_MODEL), F32),
            pltpu.SemaphoreType.DMA((4,)),
            pltpu.SemaphoreType.DMA((4,)),
            pltpu.SemaphoreType.DMA((4,)),
            pltpu.SemaphoreType.DMA((4,)),
            pltpu.SemaphoreType.DMA((4,)),
            pltpu.SemaphoreType.DMA((4,)),
        ],
        compiler_params=pltpu.CompilerParams(vmem_limit_bytes=48 * MIB),
    )(pack, *smalls, head, *dws)


def _adam_step(w, g, m, v):
    m = ADAM_B1 * m + (1.0 - ADAM_B1) * g
    v = ADAM_B2 * v + (1.0 - ADAM_B2) * (g * g)
    m_hat = m / (1.0 - ADAM_B1 ** ADAM_STEP)
    v_hat = v / (1.0 - ADAM_B2 ** ADAM_STEP)
    return -ADAM_LR * (m_hat / (jnp.sqrt(v_hat) + ADAM_EPS) + ADAM_WD * w), m, v


def _adamw_rows(name, reduced, row_off, states):
    n = len(states)

    def body(*refs):
        red = refs[:DEPTH]
        ins = refs[DEPTH:DEPTH + 3 * n]
        outs = refs[DEPTH + 3 * n:]
        layer = pl.program_id(0)
        for l in range(DEPTH):
            @pl.when(layer == l)
            def _():
                for k in range(n):
                    w_ref, m_ref, v_ref = ins[3 * k:3 * k + 3]
                    g_ref, d_ref, nm_ref, nv_ref = outs[4 * k:4 * k + 4]
                    g = red[l][row_off[k]:row_off[k] + w_ref.shape[0], :]
                    d, m, v = _adam_step(w_ref[...], g, m_ref[...], v_ref[...])
                    g_ref[...] = g
                    d_ref[...] = d
                    nm_ref[...] = m
                    nv_ref[...] = v

    flat = [a for st in states for a in st]
    state_specs, out_specs, out_shape = [], [], []
    for w, _, _ in states:
        spec = pl.BlockSpec((None,) + w.shape[1:], lambda l: (l, 0, 0))
        state_specs += [spec] * 3
        out_specs += [spec] * 4
        out_shape += [jax.ShapeDtypeStruct(w.shape, F32)] * 4
    red_specs = [pl.BlockSpec(a.shape, lambda l: (0, 0)) for a in reduced]
    operands = [pltpu.with_memory_space_constraint(a, pltpu.HBM) for a in (*reduced, *flat)]
    outs = pl.pallas_call(
        body,
        name=name,
        grid=(DEPTH,),
        out_shape=[pltpu.HBM(a.shape, a.dtype) for a in out_shape],
        in_specs=red_specs + state_specs,
        out_specs=out_specs,
        compiler_params=pltpu.CompilerParams(dimension_semantics=("arbitrary",), vmem_limit_bytes=48 * MIB),
    )(*operands)
    return [tuple(outs[4 * k:4 * k + 4]) for k in range(n)]


def _adamw_small(total, g_conv, g_proj, st):
    names = ["norm_g", "ple_norm_g", "ln_v_g", "ln_v_b", "b_s", "w_s", "final_g", "conv_w", "w_ple_proj"]
    cut = names[:7]

    def body(total_ref, gconv_ref, gproj_ref, *refs):
        ins = {nm: refs[3 * k:3 * k + 3] for k, nm in enumerate(names)}
        outs, pos = {}, 3 * len(names)
        for nm in names:
            cnt = 4 if nm in cut else 3
            outs[nm] = refs[pos:pos + cnt]
            pos += cnt

        def update(nm, idx, g):
            w_ref, m_ref, v_ref = ins[nm]
            d, m, v = _adam_step(w_ref[idx], g, m_ref[idx], v_ref[idx])
            o = outs[nm]
            if nm in cut:
                o[0][idx] = g
                o = o[1:]
            o[0][idx] = d
            o[1][idx] = m
            o[2][idx] = v

        tril = (lax.broadcasted_iota(jnp.int32, (CHUNK, CHUNK), 0) >= lax.broadcasted_iota(jnp.int32, (CHUNK, CHUNK), 1))
        for l in range(DEPTH):
            base = l * SMALL_ROWS
            row = (slice(l, l + 1), slice(None))
            update("norm_g", row, total_ref[base + SMALL_NORM:base + SMALL_NORM + 1, :])
            update("ple_norm_g", row, total_ref[base + SMALL_PLE:base + SMALL_PLE + 1, :])
            update("ln_v_g", row, total_ref[base + SMALL_LN:base + SMALL_LN + 1, 0:WIDTH_A])
            update("ln_v_b", row, total_ref[base + SMALL_LN:base + SMALL_LN + 1, WIDTH_A:2 * WIDTH_A])
            for h in range(HEADS_A):
                update("b_s", (l, slice(h, h + 1), slice(None)),
                       total_ref[base + SMALL_BS:base + SMALL_BS + 1, h * HEAD_DIM:(h + 1) * HEAD_DIM])
                lanes = slice(l * WIDTH_A + h * CHUNK, l * WIDTH_A + (h + 1) * CHUNK)
                update("w_s", (l, h), jnp.where(tril, total_ref[TOTAL_WS:TOTAL_ROWS, lanes], 0.0))
        update("final_g", (slice(None), slice(None)), total_ref[TOTAL_HEAD + HEAD_FINAL:TOTAL_HEAD + HEAD_FINAL + 1, :])
        update("conv_w", (slice(None),) * 3, gconv_ref[...])
        update("w_ple_proj", (slice(None),) * 3, gproj_ref[...])

    flat = [a for nm in names for a in st[nm]]
    out_shape = []
    for nm in names:
        out_shape += [jax.ShapeDtypeStruct(st[nm][0].shape, F32)] * (4 if nm in cut else 3)
    def whole(a):
        return pl.BlockSpec(a.shape, lambda i: (0,) * len(a.shape))

    operands = [pltpu.with_memory_space_constraint(a, pltpu.HBM) for a in (total, g_conv, g_proj, *flat)]
    outs = pl.pallas_call(
        body,
        name="adamw_small",
        grid=(1,),
        out_shape=[pltpu.HBM(a.shape, a.dtype) for a in out_shape],
        in_specs=[whole(a) for a in operands],
        out_specs=[whole(a) for a in out_shape],
        compiler_params=pltpu.CompilerParams(dimension_semantics=("arbitrary",), vmem_limit_bytes=32 * MIB),
    )(*operands)
    res, pos = {}, 0
    for nm in names:
        cnt = 4 if nm in cut else 3
        got = tuple(outs[pos:pos + cnt])
        res[nm] = got if nm in cut else ((g_conv if nm == "conv_w" else g_proj),) + got
        pos += cnt
    return res


def _split3_bf16(a):
    b1 = a.astype(BF16)
    r1 = a - b1.astype(F32)
    b2 = r1.astype(BF16)
    b3 = (r1 - b2.astype(F32)).astype(BF16)
    return b1, b2, b3


def _pack_weight_shard(w_in_l, w_out_l, w_gate_l, w_proj_l, conv_w_l):
    w_in_t = jnp.transpose(w_in_l).astype(BF16)
    proj_t = jnp.transpose(w_proj_l).astype(BF16)
    proj_rows = proj_t.reshape(D_MODEL // PLE_DIM, ROWS_PROJ, PLE_DIM).transpose(1, 0, 2).reshape(ROWS_PROJ, D_MODEL)
    conv_parts = jnp.concatenate([b.reshape(-1) for b in _split3_bf16(conv_w_l)])
    conv_rows = jnp.concatenate([conv_parts, jnp.zeros((ROWS_CONV * D_MODEL - conv_parts.shape[0],), BF16)])
    return jnp.concatenate(
        [w_in_t, w_out_l.astype(BF16), w_gate_l.astype(BF16), proj_rows, conv_rows.reshape(ROWS_CONV, D_MODEL)], axis=0
    )


def _unpack_conv(wg):
    per_dev = wg.reshape(N_DEV, ROWS_LAYER, D_MODEL)
    n_conv = (WIDTH_B // N_DEV) * 3
    conv_parts = per_dev[:, OFF_CONV].astype(F32)[:, :3 * n_conv].reshape(N_DEV, 3, n_conv)
    conv = (conv_parts[:, 0] + conv_parts[:, 1]) + conv_parts[:, 2]
    conv_k = jnp.transpose(conv.reshape(WIDTH_B, 3))
    conv_k = jnp.concatenate([conv_k, jnp.zeros((5, WIDTH_B), F32)], axis=0)
    return conv_k


def _unpack_grad_proj(red):
    proj_rows = red[OFF_PROJ:OFF_PROJ + ROWS_PROJ]
    proj_t = proj_rows.reshape(ROWS_PROJ, D_MODEL // PLE_DIM, PLE_DIM).transpose(1, 0, 2).reshape(ROWS_OUT, PLE_DIM)
    return jnp.transpose(proj_t)


def kernel(x, p, norm_g, w_in, ln_v_g, ln_v_b, w_s, b_s, conv_w, w_out, ple_norm_g, w_ple_gate, w_ple_proj, final_g, loss_target, m_norm_g, m_w_in, m_ln_v_g, m_ln_v_b, m_w_s, m_b_s, m_conv_w, m_w_out, m_ple_norm_g, m_w_ple_gate, m_w_ple_proj, m_final_g, v_norm_g, v_w_in, v_ln_v_g, v_ln_v_b, v_w_s, v_b_s, v_conv_w, v_w_out, v_ple_norm_g, v_w_ple_gate, v_w_ple_proj, v_final_g):
    me = 4 * lax.axis_index("x") + 2 * lax.axis_index("y") + lax.axis_index("c")
    xs = x[0]
    target = loss_target[0]

    shards = [_pack_weight_shard(w_in[l], w_out[l], w_ple_gate[l], w_ple_proj[l], conv_w[l]) for l in range(DEPTH)]
    tril = jnp.tril(jnp.ones((CHUNK, CHUNK), F32))

    def consts(l, wg_l):
        conv_k = _unpack_conv(wg_l)
        w_mix = w_s[l] * tril[None]
        small = dict(
            conv_k=conv_k,
            norm_g=norm_g[l].reshape(1, D_MODEL), ln_g=ln_v_g[l].reshape(1, WIDTH_A), ln_b=ln_v_b[l].reshape(1, WIDTH_A),
            w_mix=w_mix.astype(BF16), w_mix_t=jnp.swapaxes(w_mix, 1, 2).astype(BF16),
            b_mix=jnp.broadcast_to(b_s[l][:, :, None], (HEADS_A, CHUNK, HEAD_DIM)),
            ple_g=ple_norm_g[l].reshape(1, D_MODEL),
        )
        return dict({k: pltpu.with_memory_space_constraint(a, pltpu.HBM) for k, a in small.items()}, wg=wg_l)

    layer_consts = [consts(0, _all_gather_rows(shards[0]))]
    saved = []
    h = xs
    for l in range(DEPTH):
        k = layer_consts[l]
        outs = _forward_layer(
            l, h, p, k["wg"], k["conv_k"], k["norm_g"], k["ln_g"], k["ln_b"], k["w_mix"], k["b_mix"],
            k["ple_g"], next_shard=shards[l + 1] if l + 1 < DEPTH else None)
        proj, hn, cat, r, gpre, x1, x2 = outs[:7]
        if l + 1 < DEPTH:
            layer_consts.append(consts(l + 1, outs[7]))
        saved.append(dict(x_in=h, proj=proj, hn=hn, cat=cat, r=r, gpre=gpre, x1=x1))
        h = x2

    smalls, dws = [None] * DEPTH, [None] * DEPTH
    reduced = [None] * DEPTH
    pending = None
    dx = h
    for l in reversed(range(DEPTH)):
        k, s = layer_consts[l], saved[l]
        outs = _backward_layer(
            l, dx, s["x_in"], s["x1"], s["proj"], s["gpre"], p, k["wg"], k["conv_k"],
            k["norm_g"], k["ln_g"], k["ln_b"], k["w_mix"], k["w_mix_t"], k["b_mix"], k["ple_g"],
            loss_head=(target, final_g.reshape(1, D_MODEL)) if l == DEPTH - 1 else None)
        dx, dproj, dx1, dgpre, dpp, smalls[l], dws[l] = outs[:7]
        if l == DEPTH - 1:
            head = outs[7]
        outs = _weight_grads(l, dproj, s["hn"], s["cat"], dx1, s["r"], dgpre, dpp, p, scatter_pack=pending)
        if pending is not None:
            reduced[l + 1] = _sum_pieces(l + 1, outs[1])
        pending = outs[0]
    reduced[0], total = _reduce_scatter_all_reduce(0, pending, smalls, head, dws)
    grad_x = dx[None]
    loss = total[TOTAL_HEAD + HEAD_LOSS, 0]

    n_ch = WIDTH_B // N_DEV
    g_conv = jnp.stack([total[l * SMALL_ROWS + SMALL_CONV:l * SMALL_ROWS + SMALL_CONV + 3, 0:WIDTH_B] for l in range(DEPTH)], axis=1)
    g_conv = lax.dynamic_slice_in_dim(g_conv, me * n_ch, n_ch, axis=2)
    g_proj = jnp.stack([_unpack_grad_proj(reduced[l]) for l in range(DEPTH)])

    def t_in(a):
        return jnp.swapaxes(a, 1, 2)

    def t_conv(a):
        return jnp.transpose(a, (2, 0, 1))

    (r_in,) = _adamw_rows("adamw_w_in", reduced, [OFF_IN], [(t_in(w_in), t_in(m_w_in), t_in(v_w_in))])
    r_out, r_gate = _adamw_rows(
        "adamw_w_out_gate", reduced, [OFF_OUT, OFF_GATE],
        [(w_out, m_w_out, v_w_out), (w_ple_gate, m_w_ple_gate, v_w_ple_gate)])
    small = _adamw_small(total, g_conv, g_proj, dict(
        norm_g=(norm_g, m_norm_g, v_norm_g), ple_norm_g=(ple_norm_g, m_ple_norm_g, v_ple_norm_g),
        ln_v_g=(ln_v_g, m_ln_v_g, v_ln_v_g), ln_v_b=(ln_v_b, m_ln_v_b, v_ln_v_b),
        b_s=(b_s, m_b_s, v_b_s), w_s=(w_s, m_w_s, v_w_s),
        final_g=tuple(a.reshape(1, D_MODEL) for a in (final_g, m_final_g, v_final_g)),
        conv_w=(t_conv(conv_w), t_conv(m_conv_w), t_conv(v_conv_w)),
        w_ple_proj=(w_ple_proj, m_w_ple_proj, v_w_ple_proj),
    ))
    res = dict(small, w_in=tuple(t_in(a) for a in r_in), w_out=r_out, w_ple_gate=r_gate)
    res["final_g"] = tuple(a.reshape(D_MODEL) for a in res["final_g"])
    res["conv_w"] = tuple(jnp.transpose(a, (1, 2, 0)) for a in res["conv_w"])
    order = ["norm_g", "w_in", "ln_v_g", "ln_v_b", "w_s", "b_s", "conv_w", "w_out", "ple_norm_g", "w_ple_gate", "w_ple_proj", "final_g"]
    return (loss, grad_x, *[res[n][0] for n in order], *[res[n][1] for n in order],
            *[res[n][2] for n in order], *[res[n][3] for n in order])
```

```python
import jax
import jax.numpy as jnp
from jax import lax
from jax.experimental import pallas as pl
from jax.experimental.pallas import tpu as pltpu

F32 = jnp.float32
BF16 = jnp.bfloat16

D_MODEL = 1024
WIDTH_A = 512
WIDTH_B = 512
HEADS_A = 4
HEAD_DIM = 128
CHUNK = 128
PLE_DIM = 256
PROJ_WIDTH = 3584
DEPTH = 2
EPS = 1e-6
N_DEV = 8

ADAM_LR = 0.001
ADAM_B1 = 0.9
ADAM_B2 = 0.999
ADAM_EPS = 1e-08
ADAM_WD = 0.01
ADAM_STEP = 10

ROWS_IN = PROJ_WIDTH // N_DEV
ROWS_OUT = D_MODEL // N_DEV
ROWS_GATE = D_MODEL // N_DEV
ROWS_PROJ = (D_MODEL // N_DEV) * PLE_DIM // D_MODEL
ROWS_CONV = 16
OFF_IN = 0
OFF_OUT = OFF_IN + ROWS_IN
OFF_GATE = OFF_OUT + ROWS_OUT
OFF_PROJ = OFF_GATE + ROWS_GATE
OFF_CONV = OFF_PROJ + ROWS_PROJ
ROWS_GRAD = OFF_CONV
ROWS_LAYER = OFF_CONV + ROWS_CONV

SMALL_ROWS = 8
SMALL_NORM = 0
SMALL_PLE = 1
SMALL_LN = 2
SMALL_BS = 3
SMALL_CONV = 4
HEAD_FINAL = 0
HEAD_LOSS = 1
TOTAL_HEAD = DEPTH * SMALL_ROWS
TOTAL_WS = TOTAL_HEAD + SMALL_ROWS
TOTAL_ROWS = TOTAL_WS + CHUNK

MIB = 1024 * 1024
MESH = pl.DeviceIdType.MESH

NT_DIMS = (((1,), (1,)), ((), ()))
TN_DIMS = (((0,), (0,)), ((), ()))


def _dot(a, b):
    return jnp.dot(a, b, preferred_element_type=F32)


def _dot_nt(a, b):
    return lax.dot_general(a, b, NT_DIMS, preferred_element_type=F32)


def _dot_tn(a, b):
    return lax.dot_general(a, b, TN_DIMS, preferred_element_type=F32)


def _colsum8(a):
    rows, n = a.shape
    return jnp.sum(a.reshape(rows // 8, 8, n), axis=0)


def _sigmoid(z):
    return 1.0 / (1.0 + jnp.exp(-z))


def _tile(t, want):
    return want if t % want == 0 else t


class _TwoLevelGather:
    def __init__(self, x_ref, out_ref, m_per, send_sems, recv_sems, local_sem):
        x, y, c = lax.axis_index("x"), lax.axis_index("y"), lax.axis_index("c")
        self.me, self.sibling = (x, y, c), (x, y, 1 - c)
        self.xn, self.yn, self.diag = (1 - x, y, c), (x, 1 - y, c), (1 - x, 1 - y, c)
        self.x_ref, self.out_ref, self.m_per = x_ref, out_ref, m_per
        self.half = (m_per // 32) * 16
        self.send_sems, self.recv_sems = send_sems, recv_sems
        self.mine = pltpu.make_async_copy(x_ref, self.rows(self.me), local_sem)

    def rows(self, block, part=None):
        px, py, pc = block
        base = (4 * px + 2 * py + pc) * self.m_per
        if part is None:
            return self.out_ref.at[pl.ds(base, self.m_per), :]
        if part == 0:
            return self.out_ref.at[pl.ds(base, self.half), :]
        return self.out_ref.at[pl.ds(base + self.half, self.m_per - self.half), :]

    def copy(self, k, block, to, src=None, part=None):
        return pltpu.make_async_remote_copy(
            src_ref=self.rows(block, part) if src is None else src,
            dst_ref=self.rows(block, part),
            send_sem=self.send_sems.at[k],
            recv_sem=self.recv_sems.at[k],
            device_id=to,
            device_id_type=MESH,
        )

    def first(self):
        return [self.copy(0, self.me, self.sibling, src=self.x_ref),
                self.copy(1, self.me, self.xn, src=self.x_ref),
                self.copy(2, self.me, self.yn, src=self.x_ref)]

    def second(self):
        return [self.copy(3, self.xn, self.yn, part=0), self.copy(7, self.yn, self.xn, part=1),
                self.copy(4, self.xn, self.sibling), self.copy(5, self.yn, self.sibling)]

    def third(self):
        return [self.copy(6, self.diag, self.sibling)]

    def start(self):
        self.mine.start()
        for cp in self.first():
            cp.start()

    def pass_on(self):
        fwd_x, fwd_y, sib_x, sib_y = self.second()
        self.copy(1, self.xn, self.me).wait_recv()
        fwd_x.start()
        sib_x.start()
        self.copy(2, self.yn, self.me).wait_recv()
        fwd_y.start()
        sib_y.start()

    def pass_on_diagonal(self):
        self.copy(3, self.diag, self.me, part=0).wait_recv()
        self.copy(7, self.diag, self.me, part=1).wait_recv()
        self.third()[0].start()

    def finish(self):
        sib = (self.sibling[0], self.sibling[1], self.sibling[2])
        self.copy(0, sib, self.me).wait_recv()
        for k, chip in ((4, self.xn), (5, self.yn), (6, self.diag)):
            self.copy(k, (chip[0], chip[1], sib[2]), self.me).wait_recv()
        for cp in self.first() + self.second() + self.third():
            cp.wait_send()
        self.mine.wait()


GATHER_SEMS = [pltpu.SemaphoreType.DMA((8,)), pltpu.SemaphoreType.DMA((8,)), pltpu.SemaphoreType.DMA]


def _all_gather_rows(shard):
    m_per, n = shard.shape

    def body(x_ref, out_ref, send_sems, recv_sems, local_sem):
        ag = _TwoLevelGather(x_ref, out_ref, m_per, send_sems, recv_sems, local_sem)
        ag.start()
        ag.pass_on()
        ag.pass_on_diagonal()
        ag.finish()

    return pl.pallas_call(
        body,
        name="weights_all_gather",
        out_shape=pltpu.HBM((N_DEV * m_per, n), shard.dtype),
        in_specs=[pl.BlockSpec(memory_space=pltpu.HBM)],
        out_specs=pl.BlockSpec(memory_space=pltpu.HBM),
        scratch_shapes=list(GATHER_SEMS),
    )(pltpu.with_memory_space_constraint(shard, pltpu.HBM))


PROJ_PARTS = D_MODEL // PLE_DIM
N_WEIGHT_COPIES = N_DEV * (3 + PROJ_PARTS)


def _weight_copies(wg_ref, w_in_t, w_out, w_gate, w_proj_t, sems):
    copies = []
    for s in range(N_DEV):
        base = s * ROWS_LAYER
        for dst, off, rows in ((w_in_t, OFF_IN, ROWS_IN), (w_out, OFF_OUT, ROWS_OUT), (w_gate, OFF_GATE, ROWS_GATE)):
            copies.append((wg_ref.at[pl.ds(base + off, rows), :], dst.at[pl.ds(s * rows, rows), :]))
        for j in range(PROJ_PARTS):
            copies.append((
                wg_ref.at[pl.ds(base + OFF_PROJ, ROWS_PROJ), pl.ds(j * PLE_DIM, PLE_DIM)],
                w_proj_t.at[pl.ds(s * ROWS_OUT + j * ROWS_PROJ, ROWS_PROJ), :],
            ))
    return [pltpu.make_async_copy(src, dst, sems.at[k]) for k, (src, dst) in enumerate(copies)]


def _forward_layer(layer, x, p_all, wg, conv_k, norm_g, ln_g, ln_b, w_mix, b_mix, ple_g, next_shard=None):
    t = x.shape[0]
    tm = _tile(t, 512)
    nt = t // tm
    gathers = next_shard is not None

    def body(*refs):
        (x_ref, p_ref, wg_ref, cw_ref, ng_ref, lng_ref, lnb_ref, wm_ref, bm_ref, pg_ref) = refs[:10]
        refs = refs[10:]
        if gathers:
            shard_ref, refs = refs[0], refs[1:]
        (proj_ref, hn_ref, cat_ref, r_ref, gpre_ref, x1_ref, x2_ref) = refs[:7]
        refs = refs[7:]
        if gathers:
            gathered_ref, refs = refs[0], refs[1:]
        (w_in_t, w_out, w_gate, wpt_ref, vln_s, mixed_s, halo_s, sems) = refs[:8]
        i = pl.program_id(0)
        if gathers:
            ag = _TwoLevelGather(shard_ref, gathered_ref, ROWS_LAYER, *refs[8:11])

            @pl.when(i == 0)
            def _():
                ag.start()

            @pl.when(i == (5 * nt) // 16)
            def _():
                ag.pass_on()

            @pl.when(i == nt // 2)
            def _():
                ag.pass_on_diagonal()

        @pl.when(i == 0)
        def _():
            copies = _weight_copies(wg_ref, w_in_t, w_out, w_gate, wpt_ref, sems)
            for cp in copies:
                cp.start()
            halo_s[...] = jnp.zeros_like(halo_s)
            for cp in copies:
                cp.wait()

        xv = x_ref[...]
        rstd0 = lax.rsqrt(jnp.mean(xv * xv, axis=-1, keepdims=True) + EPS)
        hn_ref[...] = (xv * rstd0 * ng_ref[...]).astype(BF16)

        def proj_section(k):
            sec = _dot_nt(hn_ref[...], w_in_t[pl.ds(k * 512, 512), :])
            proj_ref[:, k * 512:(k + 1) * 512] = sec.astype(BF16)
            return sec

        v = proj_section(1)
        mu = jnp.mean(v, axis=-1, keepdims=True)
        vc = v - mu
        var = jnp.mean(vc * vc, axis=-1, keepdims=True)
        vln = vc * lax.rsqrt(var + EPS) * lng_ref[...] + lnb_ref[...]
        vln_s[...] = vln.astype(BF16)
        for ci in range(tm // CHUNK):
            rows = pl.ds(ci * CHUNK, CHUNK)
            for h in range(HEADS_A):
                cols = pl.ds(h * HEAD_DIM, HEAD_DIM)
                mixed_s[rows, cols] = _dot(wm_ref[h], vln_s[rows, cols]) + bm_ref[h]
        u = proj_section(0)
        za = proj_section(2)
        out_a = u * mixed_s[...] * (za * _sigmoid(za))
        cat_ref[:, 0:512] = out_a.astype(BF16)

        xc = proj_section(5) * proj_section(3)
        prev = halo_s[...]
        row = lax.broadcasted_iota(jnp.int32, (tm, WIDTH_B), 0)
        xc_m1 = jnp.where(row == 0, prev[7:8, :], pltpu.roll(xc, 1, 0))
        xc_m2 = jnp.where(row == 0, prev[6:7, :], jnp.where(row == 1, prev[7:8, :], pltpu.roll(xc, 2, 0)))
        halo_s[...] = xc[tm - 8:tm, :]
        cw = cw_ref[...]
        yc = cw[0:1, :] * xc_m2 + cw[1:2, :] * xc_m1 + cw[2:3, :] * xc
        zb = proj_section(6)
        out_b = proj_section(4) * yc * (zb * _sigmoid(zb))
        cat_ref[:, 512:1024] = out_b.astype(BF16)

        x1 = xv + _dot(cat_ref[...], w_out[...])
        x1_ref[...] = x1
        rstd1 = lax.rsqrt(jnp.mean(x1 * x1, axis=-1, keepdims=True) + EPS)
        r_ref[...] = (x1 * rstd1 * pg_ref[...]).astype(BF16)
        gpre = _dot(r_ref[...], w_gate[...])
        gpre_ref[...] = gpre.astype(BF16)
        pp = _dot_nt(p_ref[...].astype(BF16), wpt_ref[...])
        x2_ref[...] = x1 + _sigmoid(gpre) * pp

        if gathers:
            @pl.when(i == nt - 1)
            def _():
                ag.finish()

    def tok(width):
        return pl.BlockSpec((tm, width), lambda i: (i, 0))

    def whole(shape):
        return pl.BlockSpec(shape, lambda i: (0,) * len(shape))

    hbm = pl.BlockSpec(memory_space=pl.ANY)
    operands = [x, p_all, wg, conv_k, norm_g, ln_g, ln_b, w_mix, b_mix, ple_g]
    in_specs = [
        tok(D_MODEL), pl.BlockSpec((None, None, tm, PLE_DIM), lambda i: (layer, 0, i, 0)), hbm,
        whole((8, WIDTH_B)), whole((1, D_MODEL)), whole((1, WIDTH_A)), whole((1, WIDTH_A)),
        whole((HEADS_A, CHUNK, CHUNK)), whole((HEADS_A, CHUNK, HEAD_DIM)), whole((1, D_MODEL)),
    ]
    out_specs = [tok(PROJ_WIDTH), tok(D_MODEL), tok(D_MODEL), tok(D_MODEL), tok(D_MODEL), tok(D_MODEL), tok(D_MODEL)]
    out_shape = [
        jax.ShapeDtypeStruct((t, PROJ_WIDTH), BF16),
        jax.ShapeDtypeStruct((t, D_MODEL), BF16),
        jax.ShapeDtypeStruct((t, D_MODEL), BF16),
        jax.ShapeDtypeStruct((t, D_MODEL), BF16),
        jax.ShapeDtypeStruct((t, D_MODEL), BF16),
        jax.ShapeDtypeStruct((t, D_MODEL), F32),
        jax.ShapeDtypeStruct((t, D_MODEL), F32),
    ]
    scratch_shapes = [
        pltpu.VMEM((PROJ_WIDTH, D_MODEL), BF16),
        pltpu.VMEM((D_MODEL, D_MODEL), BF16),
        pltpu.VMEM((D_MODEL, D_MODEL), BF16),
        pltpu.VMEM((D_MODEL, PLE_DIM), BF16),
        pltpu.VMEM((tm, WIDTH_A), BF16),
        pltpu.VMEM((tm, WIDTH_A), F32),
        pltpu.VMEM((8, WIDTH_B), F32),
        pltpu.SemaphoreType.DMA((N_WEIGHT_COPIES,)),
    ]
    if gathers:
        operands.append(pltpu.with_memory_space_constraint(next_shard, pltpu.HBM))
        in_specs.append(pl.BlockSpec(memory_space=pltpu.HBM))
        out_specs.append(pl.BlockSpec(memory_space=pltpu.HBM))
        out_shape.append(pltpu.HBM((N_DEV * ROWS_LAYER, D_MODEL), BF16))
        scratch_shapes += list(GATHER_SEMS)

    return pl.pallas_call(
        body,
        name=f"layer{layer}_forward",
        grid=(nt,),
        in_specs=in_specs,
        out_specs=out_specs,
        out_shape=out_shape,
        scratch_shapes=scratch_shapes,
        compiler_params=pltpu.CompilerParams(dimension_semantics=("arbitrary",), vmem_limit_bytes=56 * MIB),
    )(*operands)


class _DirectScatter:
    def __init__(self, pack_ref, pieces_ref, send_sems, recv_sems, local_sem):
        x, y, c = lax.axis_index("x"), lax.axis_index("y"), lax.axis_index("c")
        me = 4 * x + 2 * y + c
        self.copies = []
        for k in range(N_DEV - 1):
            fx, fy, fc = ((k + 1) >> 2) & 1, ((k + 1) >> 1) & 1, (k + 1) & 1
            tx, ty, tc = x ^ fx, y ^ fy, c ^ fc
            self.copies.append(
                pltpu.make_async_remote_copy(
                    src_ref=pack_ref.at[4 * tx + 2 * ty + tc], dst_ref=pieces_ref.at[me],
                    send_sem=send_sems.at[k], recv_sem=recv_sems.at[k],
                    device_id=(tx, ty, tc), device_id_type=MESH,
                )
            )
        self.mine = pltpu.make_async_copy(pack_ref.at[me], pieces_ref.at[me], local_sem)

    def start(self):
        self.mine.start()
        for cp in self.copies:
            cp.start()

    def finish(self):
        for cp in self.copies:
            cp.wait_recv()
        for cp in self.copies:
            cp.wait_send()
        self.mine.wait()


SCATTER_SEMS = [pltpu.SemaphoreType.DMA((N_DEV - 1,)), pltpu.SemaphoreType.DMA((N_DEV - 1,)), pltpu.SemaphoreType.DMA]


def _backward_layer(layer, dx2, x_in, x1, proj, gpre, p_all, wg, conv_k, norm_g, ln_g, ln_b,
                    w_mix, w_mix_t, b_mix, ple_g, loss_head=None):
    t = x_in.shape[0]
    tm = _tile(t, 256)
    nt = t // tm
    n_chunks = tm // CHUNK
    halo_rows = 16
    heads = loss_head is not None

    def body(*refs):
        (dx2_ref, xin_ref, x1_ref, proj_ref, halo_ref, gpre_ref, p_ref, wg_ref, cw_ref,
         ng_ref, lng_ref, lnb_ref, wm_ref, wmt_ref, bm_ref, pg_ref) = refs[:16]
        refs = refs[16:]
        if heads:
            tgt_ref, fg_ref = refs[:2]
            refs = refs[2:]
        (dxin_ref, dproj_ref, dx1_ref, dgpre_ref, dpp_ref, small_ref, dws_ref) = refs[:7]
        refs = refs[7:]
        if heads:
            head_ref, refs = refs[0], refs[1:]
        (w_in_t, w_out, w_gate, wpt_ref, vln_s, mixed_s, dmix_s, dvln_s, carry_s,
         ng_acc, pg_acc, lng_acc, lnb_acc, cw_acc, dbm_ref, sems) = refs[:16]
        if heads:
            loss_acc, fg_acc = refs[16:18]
        i = pl.program_id(0)
        tile = nt - 1 - i

        @pl.when(i == 0)
        def _():
            copies = _weight_copies(wg_ref, w_in_t, w_out, w_gate, wpt_ref, sems)
            for cp in copies:
                cp.start()
            if heads:
                loss_acc[...] = jnp.zeros_like(loss_acc)
                fg_acc[...] = jnp.zeros_like(fg_acc)
            carry_s[...] = jnp.zeros_like(carry_s)
            ng_acc[...] = jnp.zeros_like(ng_acc)
            pg_acc[...] = jnp.zeros_like(pg_acc)
            lng_acc[...] = jnp.zeros_like(lng_acc)
            lnb_acc[...] = jnp.zeros_like(lnb_acc)
            cw_acc[...] = jnp.zeros_like(cw_acc)
            dws_ref[...] = jnp.zeros_like(dws_ref)
            dbm_ref[...] = jnp.zeros_like(dbm_ref)
            for cp in copies:
                cp.wait()

        if heads:
            x2v = dx2_ref[...]
            fg = fg_ref[...]
            rstdf = lax.rsqrt(jnp.mean(x2v * x2v, axis=-1, keepdims=True) + EPS)
            xhatf = x2v * rstdf
            err = xhatf * fg - tgt_ref[...]
            loss_acc[...] += _colsum8(err * err)
            dy = err * (1.0 / D_MODEL)
            fg_acc[...] += _colsum8(dy * xhatf)
            dxhf = dy * fg
            dx2v = rstdf * (dxhf - xhatf * jnp.mean(dxhf * xhatf, axis=-1, keepdims=True))
        else:
            dx2v = dx2_ref[...]

        gate = _sigmoid(gpre_ref[...].astype(F32))
        pp = _dot_nt(p_ref[...].astype(BF16), wpt_ref[...])
        dpp = dx2v * gate
        dpp_ref[...] = dpp.astype(BF16)
        dgpre = (dpp * pp * (1.0 - gate)).astype(BF16)
        dgpre_ref[...] = dgpre
        dr = _dot_nt(dgpre, w_gate[...])
        x1v = x1_ref[...]
        rstd1 = lax.rsqrt(jnp.mean(x1v * x1v, axis=-1, keepdims=True) + EPS)
        xhat1 = x1v * rstd1
        pg_acc[...] += _colsum8(dr * xhat1)
        dxh = dr * pg_ref[...]
        dx1 = dx2v + rstd1 * (dxh - xhat1 * jnp.mean(dxh * xhat1, axis=-1, keepdims=True))
        dx1b = dx1.astype(BF16)
        dx1_ref[...] = dx1b

        dcat = _dot_nt(dx1b, w_out[...])
        dca = dcat[:, 0:512]
        dcb = dcat[:, 512:1024]

        u = proj_ref[:, 0:512]
        v = proj_ref[:, 512:1024].astype(F32)
        za = proj_ref[:, 1024:1536]
        mu = jnp.mean(v, axis=-1, keepdims=True)
        vc = v - mu
        var = jnp.mean(vc * vc, axis=-1, keepdims=True)
        rs = lax.rsqrt(var + EPS)
        vhat = vc * rs
        lng = lng_ref[...]
        vln_s[...] = (vhat * lng + lnb_ref[...]).astype(BF16)
        for ci in range(n_chunks):
            rows = pl.ds(ci * CHUNK, CHUNK)
            for h in range(HEADS_A):
                cols = pl.ds(h * HEAD_DIM, HEAD_DIM)
                mixed_s[rows, cols] = (_dot(wm_ref[h], vln_s[rows, cols]) + bm_ref[h]).astype(BF16)
        mixed = mixed_s[...]
        sga = _sigmoid(za)
        sa = za * sga
        dsa = sga + sa * (1.0 - sga)

        def put_section(k, val):
            dproj_ref[:, k * 512:(k + 1) * 512] = val.astype(BF16)

        dcab = dca.astype(BF16)
        dca_sa = dcab * sa
        put_section(0, dca_sa * mixed)
        dmix_s[...] = dca_sa * u
        put_section(2, (dcab * dsa) * (u * mixed))
        dbm_acc = jnp.zeros((CHUNK, WIDTH_A), F32)
        for ci in range(n_chunks):
            rows = pl.ds(ci * CHUNK, CHUNK)
            dbm_acc = dbm_acc + dmix_s[rows, :].astype(F32)
            for h in range(HEADS_A):
                cols = pl.ds(h * HEAD_DIM, HEAD_DIM)
                dvln_s[rows, cols] = _dot(wmt_ref[h], dmix_s[rows, cols])
                dws_ref[:, cols] += _dot_nt(dmix_s[rows, cols], vln_s[rows, cols])
        dbm_ref[...] += dbm_acc
        dvln = dvln_s[...]
        lng_acc[...] += _colsum8(dvln * vhat)
        lnb_acc[...] += _colsum8(dvln)
        dvh = dvln * lng
        dv = rs * (dvh - jnp.mean(dvh, axis=-1, keepdims=True) - vhat * jnp.mean(dvh * vhat, axis=-1, keepdims=True))
        put_section(1, dv)

        hb = proj_ref[:, 1536:2048].astype(F32)
        gb = proj_ref[:, 2048:2560]
        gc = proj_ref[:, 2560:3072].astype(F32)
        zb = proj_ref[:, 3072:3584]
        xc = gc * hb
        prev = halo_ref[:, 2560:3072].astype(F32) * halo_ref[:, 1536:2048].astype(F32)
        prev = jnp.where(tile > 0, prev, 0.0)
        row = lax.broadcasted_iota(jnp.int32, (tm, WIDTH_B), 0)
        p1 = prev[halo_rows - 1:halo_rows, :]
        p2 = prev[halo_rows - 2:halo_rows - 1, :]
        xc_m1 = jnp.where(row == 0, p1, pltpu.roll(xc, 1, 0))
        xc_m2 = jnp.where(row == 0, p2, jnp.where(row == 1, p1, pltpu.roll(xc, 2, 0)))
        cw = cw_ref[...]
        yc = cw[0:1, :] * xc_m2 + cw[1:2, :] * xc_m1 + cw[2:3, :] * xc
        sgb = _sigmoid(zb)
        sb = zb * sgb
        dsb = sgb + sb * (1.0 - sgb)
        dcbb = dcb.astype(BF16)
        ycb = yc.astype(BF16)
        dcb_sb = dcbb * sb
        put_section(4, dcb_sb * ycb)
        dyc = (dcb_sb * gb).astype(F32)
        put_section(6, (dcbb * dsb) * (gb * ycb))
        nxt = carry_s[...]
        dyc_p1 = jnp.where(row == tm - 1, nxt[0:1, :], pltpu.roll(dyc, tm - 1, 0))
        dyc_p2 = jnp.where(row == tm - 1, nxt[1:2, :], jnp.where(row == tm - 2, nxt[0:1, :], pltpu.roll(dyc, tm - 2, 0)))
        carry_s[...] = dyc[0:8, :]
        dxc = cw[2:3, :] * dyc + cw[1:2, :] * dyc_p1 + cw[0:1, :] * dyc_p2
        cw_acc[0] += _colsum8(dyc * xc_m2)
        cw_acc[1] += _colsum8(dyc * xc_m1)
        cw_acc[2] += _colsum8(dyc * xc)
        put_section(3, dxc * gc)
        put_section(5, dxc * hb)

        dhn = _dot(dproj_ref[...], w_in_t[...])
        xv = xin_ref[...]
        rstd0 = lax.rsqrt(jnp.mean(xv * xv, axis=-1, keepdims=True) + EPS)
        xhat0 = xv * rstd0
        ng_acc[...] += _colsum8(dhn * xhat0)
        dxh0 = dhn * ng_ref[...]
        dxin_ref[...] = dx1 + rstd0 * (dxh0 - xhat0 * jnp.mean(dxh0 * xhat0, axis=-1, keepdims=True))

        @pl.when(i == nt - 1)
        def _():
            small_ref[...] = jnp.zeros_like(small_ref)
            small_ref[SMALL_NORM:SMALL_NORM + 1, :] = jnp.sum(ng_acc[...], axis=0, keepdims=True)
            small_ref[SMALL_PLE:SMALL_PLE + 1, :] = jnp.sum(pg_acc[...], axis=0, keepdims=True)
            small_ref[SMALL_LN:SMALL_LN + 1, 0:WIDTH_A] = jnp.sum(lng_acc[...], axis=0, keepdims=True)
            small_ref[SMALL_LN:SMALL_LN + 1, WIDTH_A:2 * WIDTH_A] = jnp.sum(lnb_acc[...], axis=0, keepdims=True)
            for h in range(HEADS_A):
                cols = pl.ds(h * HEAD_DIM, HEAD_DIM)
                small_ref[SMALL_BS:SMALL_BS + 1, cols] = jnp.sum(jnp.transpose(dbm_ref[:, cols]), axis=0, keepdims=True)
            for k in range(3):
                small_ref[SMALL_CONV + k:SMALL_CONV + k + 1, 0:WIDTH_B] = jnp.sum(cw_acc[k], axis=0, keepdims=True)
            if heads:
                total = jnp.sum(loss_acc[...]) * (0.5 / D_MODEL)
                rows8 = lax.broadcasted_iota(jnp.int32, (SMALL_ROWS, D_MODEL), 0)
                lanes8 = lax.broadcasted_iota(jnp.int32, (SMALL_ROWS, D_MODEL), 1)
                head_ref[...] = jnp.where((rows8 == HEAD_LOSS) & (lanes8 == 0), total, 0.0)
                head_ref[HEAD_FINAL:HEAD_FINAL + 1, :] = jnp.sum(fg_acc[...], axis=0, keepdims=True)

    def tok(width):
        return pl.BlockSpec((tm, width), lambda i: (nt - 1 - i, 0))

    def whole(shape):
        return pl.BlockSpec(shape, lambda i: (0,) * len(shape))

    halo_spec = pl.BlockSpec(
        (halo_rows, PROJ_WIDTH), lambda i: (jnp.maximum((nt - 1 - i) * (tm // halo_rows) - 1, 0), 0)
    )
    hbm = pl.BlockSpec(memory_space=pl.ANY)
    operands = [dx2, x_in, x1, proj, proj, gpre, p_all, wg, conv_k, norm_g, ln_g, ln_b, w_mix, w_mix_t, b_mix, ple_g]
    in_specs = [
        tok(D_MODEL), tok(D_MODEL), tok(D_MODEL), tok(PROJ_WIDTH), halo_spec, tok(D_MODEL),
        pl.BlockSpec((None, None, tm, PLE_DIM), lambda i: (layer, 0, nt - 1 - i, 0)), hbm,
        whole((8, WIDTH_B)), whole((1, D_MODEL)), whole((1, WIDTH_A)), whole((1, WIDTH_A)),
        whole((HEADS_A, CHUNK, CHUNK)), whole((HEADS_A, CHUNK, CHUNK)), whole((HEADS_A, CHUNK, HEAD_DIM)),
        whole((1, D_MODEL)),
    ]
    out_specs = [
        tok(D_MODEL), tok(PROJ_WIDTH), tok(D_MODEL), tok(D_MODEL), tok(D_MODEL),
        whole((SMALL_ROWS, D_MODEL)), whole((CHUNK, WIDTH_A)),
    ]
    out_shape = [
        jax.ShapeDtypeStruct((t, D_MODEL), F32),
        jax.ShapeDtypeStruct((t, PROJ_WIDTH), BF16),
        jax.ShapeDtypeStruct((t, D_MODEL), BF16),
        jax.ShapeDtypeStruct((t, D_MODEL), BF16),
        jax.ShapeDtypeStruct((t, D_MODEL), BF16),
        jax.ShapeDtypeStruct((SMALL_ROWS, D_MODEL), F32),
        jax.ShapeDtypeStruct((CHUNK, WIDTH_A), F32),
    ]
    scratch_shapes = [
        pltpu.VMEM((PROJ_WIDTH, D_MODEL), BF16),
        pltpu.VMEM((D_MODEL, D_MODEL), BF16),
        pltpu.VMEM((D_MODEL, D_MODEL), BF16),
        pltpu.VMEM((D_MODEL, PLE_DIM), BF16),
        pltpu.VMEM((tm, WIDTH_A), BF16),
        pltpu.VMEM((tm, WIDTH_A), BF16),
        pltpu.VMEM((tm, WIDTH_A), BF16),
        pltpu.VMEM((tm, WIDTH_A), F32),
        pltpu.VMEM((8, WIDTH_B), F32),
        pltpu.VMEM((8, D_MODEL), F32),
        pltpu.VMEM((8, D_MODEL), F32),
        pltpu.VMEM((8, WIDTH_A), F32),
        pltpu.VMEM((8, WIDTH_A), F32),
        pltpu.VMEM((3, 8, WIDTH_B), F32),
        pltpu.VMEM((CHUNK, WIDTH_A), F32),
        pltpu.SemaphoreType.DMA((N_WEIGHT_COPIES,)),
    ]
    if heads:
        operands += list(loss_head)
        in_specs += [tok(D_MODEL), whole((1, D_MODEL))]
        out_specs.append(whole((SMALL_ROWS, D_MODEL)))
        out_shape.append(jax.ShapeDtypeStruct((SMALL_ROWS, D_MODEL), F32))
        scratch_shapes += [pltpu.VMEM((8, D_MODEL), F32), pltpu.VMEM((8, D_MODEL), F32)]

    return pl.pallas_call(
        body,
        name=f"layer{layer}_backward",
        grid=(nt,),
        in_specs=in_specs,
        out_specs=out_specs,
        out_shape=out_shape,
        scratch_shapes=scratch_shapes,
        compiler_params=pltpu.CompilerParams(dimension_semantics=("arbitrary",), vmem_limit_bytes=56 * MIB),
    )(*operands)


def _sum_pieces(layer, pieces):
    rows, n = pieces.shape[1], pieces.shape[2]
    blocks = 2
    rb = rows // blocks

    def body(p_ref, out_ref):
        total = p_ref[0].astype(F32)
        for j in range(1, N_DEV):
            total = total + p_ref[j].astype(F32)
        out_ref[...] = total

    return pl.pallas_call(
        body,
        name=f"layer{layer}_grad_sum",
        grid=(blocks,),
        out_shape=pltpu.HBM((rows, n), F32),
        in_specs=[pl.BlockSpec((N_DEV, rb, n), lambda i: (0, i, 0))],
        out_specs=pl.BlockSpec((rb, n), lambda i: (i, 0)),
        compiler_params=pltpu.CompilerParams(dimension_semantics=("arbitrary",), vmem_limit_bytes=32 * MIB),
    )(pieces)


def _weight_grads(layer, dproj, hn, cat, dx1, r, dgpre, dpp, p_all, scatter_pack=None):
    t = hn.shape[0]
    tk = _tile(t, 512)
    nt = t // tk
    in_blocks = PROJ_WIDTH // 512
    scatters = scatter_pack is not None

    def body(*refs):
        (dproj_ref, hn_ref, cat_ref, dx1_ref, r_ref, dgpre_ref, dpp_ref, p_ref) = refs[:8]
        refs = refs[8:]
        if scatters:
            prior_ref, refs = refs[0], refs[1:]
        pack_ref, refs = refs[0], refs[1:]
        if scatters:
            pieces_ref, refs = refs[0], refs[1:]
        (acc_in, acc_out, acc_gate, acc_proj, stage, sems) = refs[:6]
        i = pl.program_id(0)
        if scatters:
            scatter = _DirectScatter(prior_ref, pieces_ref, *refs[6:9])

            @pl.when(i == 0)
            def _():
                scatter.start()

        @pl.when(i == 0)
        def _():
            acc_in[...] = jnp.zeros_like(acc_in)
            acc_out[...] = jnp.zeros_like(acc_out)
            acc_gate[...] = jnp.zeros_like(acc_gate)
            acc_proj[...] = jnp.zeros_like(acc_proj)

        hnv = hn_ref[...]
        for b in range(in_blocks):
            acc_in[pl.ds(b * 512, 512), :] += _dot_tn(dproj_ref[:, b * 512:(b + 1) * 512], hnv)
        dx1v = dx1_ref[...]
        dgv = dgpre_ref[...]
        for b in range(D_MODEL // 512):
            acc_out[pl.ds(b * 512, 512), :] += _dot_tn(cat_ref[:, b * 512:(b + 1) * 512], dx1v)
            acc_gate[pl.ds(b * 512, 512), :] += _dot_tn(r_ref[:, b * 512:(b + 1) * 512], dgv)
        pv = p_ref[...].astype(BF16)
        for b in range(D_MODEL // 512):
            acc_proj[pl.ds(b * 512, 512), :] += _dot_tn(dpp_ref[:, b * 512:(b + 1) * 512], pv)

        @pl.when(i == nt - 1)
        def _():
            def out_copy(s):
                return pltpu.make_async_copy(stage.at[s % 2], pack_ref.at[s], sems.at[s % 2])

            for s in range(N_DEV):
                if s >= 2:
                    out_copy(s - 2).wait()
                buf = stage.at[s % 2]
                buf[pl.ds(OFF_IN, ROWS_IN), :] = acc_in[pl.ds(s * ROWS_IN, ROWS_IN), :].astype(BF16)
                buf[pl.ds(OFF_OUT, ROWS_OUT), :] = acc_out[pl.ds(s * ROWS_OUT, ROWS_OUT), :].astype(BF16)
                buf[pl.ds(OFF_GATE, ROWS_GATE), :] = acc_gate[pl.ds(s * ROWS_GATE, ROWS_GATE), :].astype(BF16)
                for j in range(D_MODEL // PLE_DIM):
                    buf[pl.ds(OFF_PROJ, ROWS_PROJ), pl.ds(j * PLE_DIM, PLE_DIM)] = acc_proj[
                        pl.ds(s * ROWS_OUT + j * ROWS_PROJ, ROWS_PROJ), :
                    ].astype(BF16)
                out_copy(s).start()
            out_copy(N_DEV - 2).wait()
            out_copy(N_DEV - 1).wait()
            if scatters:
                scatter.finish()

    def tok(width):
        return pl.BlockSpec((tk, width), lambda i: (i, 0))

    hbm = pl.BlockSpec(memory_space=pl.ANY)
    pack_shape = jax.ShapeDtypeStruct((N_DEV, ROWS_GRAD, D_MODEL), BF16)
    operands = [dproj, hn, cat, dx1, r, dgpre, dpp, p_all]
    in_specs = [tok(PROJ_WIDTH), tok(D_MODEL), tok(D_MODEL), tok(D_MODEL), tok(D_MODEL), tok(D_MODEL), tok(D_MODEL),
                pl.BlockSpec((None, None, tk, PLE_DIM), lambda i: (layer, 0, i, 0))]
    out_specs, out_shape = [hbm], [pack_shape]
    scratch_shapes = [
        pltpu.VMEM((PROJ_WIDTH, D_MODEL), F32),
        pltpu.VMEM((D_MODEL, D_MODEL), F32),
        pltpu.VMEM((D_MODEL, D_MODEL), F32),
        pltpu.VMEM((D_MODEL, PLE_DIM), F32),
        pltpu.VMEM((2, ROWS_GRAD, D_MODEL), BF16),
        pltpu.SemaphoreType.DMA((2,)),
    ]
    if scatters:
        operands.append(scatter_pack)
        in_specs.append(hbm)
        out_specs.append(hbm)
        out_shape.append(pack_shape)
        scratch_shapes += list(SCATTER_SEMS)

    return pl.pallas_call(
        body,
        name=f"layer{layer}_weight_grads",
        grid=(nt,),
        in_specs=in_specs,
        out_specs=out_specs,
        out_shape=out_shape,
        scratch_shapes=scratch_shapes,
        compiler_params=pltpu.CompilerParams(dimension_semantics=("arbitrary",), vmem_limit_bytes=58 * MIB),
    )(*operands)


def _reduce_scatter_all_reduce(layer, pack, smalls, head, dws):
    rows, n = pack.shape[1], pack.shape[2]
    assert DEPTH * WIDTH_A == D_MODEL and n == D_MODEL

    def body(g_ref, *refs):
        small_refs, refs = refs[:DEPTH], refs[DEPTH:]
        head_ref, refs = refs[0], refs[1:]
        dws_refs, refs = refs[:DEPTH], refs[DEPTH:]
        (out_ref, total_ref, r1, a_s, r2, via, sp, sr1, sq, send1, recv1, send2, recv2, ssend, srecv) = refs
        x, y, c = lax.axis_index("x"), lax.axis_index("y"), lax.axis_index("c")
        sibling = (x, y, 1 - c)
        chip = 2 * x + y
        flips = [(1, 0), (0, 1), (1, 1)]

        for l in range(DEPTH):
            sp[l * SMALL_ROWS:(l + 1) * SMALL_ROWS, :] = small_refs[l][...]
            sp[TOTAL_WS:TOTAL_ROWS, l * WIDTH_A:(l + 1) * WIDTH_A] = dws_refs[l][...]
        sp[TOTAL_HEAD:TOTAL_WS, :] = head_ref[...]

        small_pair = pltpu.make_async_remote_copy(
            src_ref=sp, dst_ref=sr1, send_sem=ssend.at[0], recv_sem=srecv.at[0], device_id=sibling, device_id_type=MESH
        )

        def to_sibling(j):
            return pltpu.make_async_remote_copy(
                src_ref=g_ref.at[2 * j + 1 - c], dst_ref=r1.at[j], send_sem=send1.at[j], recv_sem=recv1.at[j],
                device_id=sibling, device_id_type=MESH,
            )

        first = [to_sibling(j) for j in range(4)]
        small_pair.start()
        for cp in first:
            cp.start()

        small_pair.wait_recv()
        sq[chip] = sp[...] + sr1[...]
        small_chips = [
            pltpu.make_async_remote_copy(
                src_ref=sq.at[chip], dst_ref=sq.at[chip], send_sem=ssend.at[1 + k], recv_sem=srecv.at[1 + k],
                device_id=(x ^ fx, y ^ fy, c), device_id_type=MESH,
            )
            for k, (fx, fy) in enumerate(flips)
        ]
        for cp in small_chips:
            cp.start()

        for j in range(4):
            first[j].wait_recv()

            @pl.when(chip != j)
            def _():
                a_s[j] = (g_ref[2 * j + c].astype(F32) + r1[j].astype(F32)).astype(BF16)

        half = rows // 2
        lo, hi = pl.ds(0, half), pl.ds(half, rows - half)
        x_nbr, y_nbr = (1 - x, y, c), (x, 1 - y, c)
        chip_x, chip_y, chip_d = 2 * (1 - x) + y, 2 * x + (1 - y), 2 * (1 - x) + (1 - y)

        def ici(k, src, dst, to):
            return pltpu.make_async_remote_copy(
                src_ref=src, dst_ref=dst, send_sem=send2.at[k], recv_sem=recv2.at[k], device_id=to, device_id_type=MESH)

        second = [
            ici(0, a_s.at[chip_d, lo, :], via.at[0], x_nbr),
            ici(1, a_s.at[chip_d, hi, :], via.at[1], y_nbr),
            ici(2, a_s.at[chip_x, lo, :], r2.at[0, lo, :], x_nbr),
            ici(3, a_s.at[chip_y, hi, :], r2.at[1, hi, :], y_nbr),
            ici(4, a_s.at[chip_x, hi, :], r2.at[0, hi, :], x_nbr),
            ici(5, a_s.at[chip_y, lo, :], r2.at[1, lo, :], y_nbr),
        ]
        for cp in second[:4]:
            cp.start()

        out_ref[...] = g_ref[2 * chip + c].astype(F32) + r1[chip].astype(F32)
        for cp in small_chips:
            cp.wait_recv()
        total_ref[...] = ((sq[0] + sq[1]) + sq[2]) + sq[3]

        second[0].wait_recv()
        a_s[chip_y, lo, :] = (a_s[chip_y, lo, :].astype(F32) + via[0].astype(F32)).astype(BF16)
        second[5].start()
        second[1].wait_recv()
        a_s[chip_x, hi, :] = (a_s[chip_x, hi, :].astype(F32) + via[1].astype(F32)).astype(BF16)
        second[4].start()

        second[2].wait_recv()
        second[4].wait_recv()
        out_ref[...] += r2[0].astype(F32)
        second[3].wait_recv()
        second[5].wait_recv()
        out_ref[...] += r2[1].astype(F32)
        small_pair.wait_send()
        for cp in first + small_chips + second:
            cp.wait_send()

    vmem = pl.BlockSpec(memory_space=pltpu.VMEM)
    return pl.pallas_call(
        body,
        name=f"layer{layer}_grad_reduce_scatter",
        out_shape=[jax.ShapeDtypeStruct((rows, n), F32), jax.ShapeDtypeStruct((TOTAL_ROWS, D_MODEL), F32)],
        in_specs=[vmem] * (2 + 2 * DEPTH),
        out_specs=[vmem, vmem],
        scratch_shapes=[
            pltpu.VMEM((4, rows, n), BF16),
            pltpu.VMEM((4, rows, n), BF16),
            pltpu.VMEM((2, rows, n), BF16),
            pltpu.VMEM((2, rows // 2, n), BF16),
            pltpu.VMEM((TOTAL_ROWS, D_MODEL), F32),
            pltpu.VMEM((TOTAL_ROWS, D_MODEL), F32),
            pltpu.VMEM((4, TOTAL_ROWS, D_MODEL), F32),
            pltpu.SemaphoreType.DMA((4,)),
            pltpu.SemaphoreType.DMA((4,)),
            pltpu.SemaphoreType.DMA((6,)),
            pltpu.SemaphoreType.DMA((6,)),
            pltpu.SemaphoreType.DMA((4,)),
            pltpu.SemaphoreType.DMA((4,)),
        ],
        compiler_params=pltpu.CompilerParams(vmem_limit_bytes=48 * MIB),
    )(pack, *smalls, head, *dws)


def _adam_step(w, g, m, v):
    m = ADAM_B1 * m + (1.0 - ADAM_B1) * g
    v = ADAM_B2 * v + (1.0 - ADAM_B2) * (g * g)
    m_hat = m / (1.0 - ADAM_B1 ** ADAM_STEP)
    v_hat = v / (1.0 - ADAM_B2 ** ADAM_STEP)
    return -ADAM_LR * (m_hat / (jnp.sqrt(v_hat) + ADAM_EPS) + ADAM_WD * w), m, v


def _adamw_rows(name, reduced, row_off, states):
    n = len(states)

    def body(*refs):
        red = refs[:DEPTH]
        ins = refs[DEPTH:DEPTH + 3 * n]
        outs = refs[DEPTH + 3 * n:]
        layer = pl.program_id(0)
        for l in range(DEPTH):
            @pl.when(layer == l)
            def _():
                for k in range(n):
                    w_ref, m_ref, v_ref = ins[3 * k:3 * k + 3]
                    g_ref, d_ref, nm_ref, nv_ref = outs[4 * k:4 * k + 4]
                    g = red[l][row_off[k]:row_off[k] + w_ref.shape[0], :]
                    d, m, v = _adam_step(w_ref[...], g, m_ref[...], v_ref[...])
                    g_ref[...] = g
                    d_ref[...] = d
                    nm_ref[...] = m
                    nv_ref[...] = v

    flat = [a for st in states for a in st]
    state_specs, out_specs, out_shape = [], [], []
    for w, _, _ in states:
        spec = pl.BlockSpec((None,) + w.shape[1:], lambda l: (l, 0, 0))
        state_specs += [spec] * 3
        out_specs += [spec] * 4
        out_shape += [jax.ShapeDtypeStruct(w.shape, F32)] * 4
    red_specs = [pl.BlockSpec(a.shape, lambda l: (0, 0)) for a in reduced]
    operands = [pltpu.with_memory_space_constraint(a, pltpu.HBM) for a in (*reduced, *flat)]
    outs = pl.pallas_call(
        body,
        name=name,
        grid=(DEPTH,),
        out_shape=[pltpu.HBM(a.shape, a.dtype) for a in out_shape],
        in_specs=red_specs + state_specs,
        out_specs=out_specs,
        compiler_params=pltpu.CompilerParams(dimension_semantics=("arbitrary",), vmem_limit_bytes=48 * MIB),
    )(*operands)
    return [tuple(outs[4 * k:4 * k + 4]) for k in range(n)]


def _adamw_small(total, g_conv, g_proj, st):
    names = ["norm_g", "ple_norm_g", "ln_v_g", "ln_v_b", "b_s", "w_s", "final_g", "conv_w", "w_ple_proj"]
    cut = names[:7]

    def body(total_ref, gconv_ref, gproj_ref, *refs):
        ins = {nm: refs[3 * k:3 * k + 3] for k, nm in enumerate(names)}
        outs, pos = {}, 3 * len(names)
        for nm in names:
            cnt = 4 if nm in cut else 3
            outs[nm] = refs[pos:pos + cnt]
            pos += cnt

        def update(nm, idx, g):
            w_ref, m_ref, v_ref = ins[nm]
            d, m, v = _adam_step(w_ref[idx], g, m_ref[idx], v_ref[idx])
            o = outs[nm]
            if nm in cut:
                o[0][idx] = g
                o = o[1:]
            o[0][idx] = d
            o[1][idx] = m
            o[2][idx] = v

        tril = (lax.broadcasted_iota(jnp.int32, (CHUNK, CHUNK), 0) >= lax.broadcasted_iota(jnp.int32, (CHUNK, CHUNK), 1))
        for l in range(DEPTH):
            base = l * SMALL_ROWS
            row = (slice(l, l + 1), slice(None))
            update("norm_g", row, total_ref[base + SMALL_NORM:base + SMALL_NORM + 1, :])
            update("ple_norm_g", row, total_ref[base + SMALL_PLE:base + SMALL_PLE + 1, :])
            update("ln_v_g", row, total_ref[base + SMALL_LN:base + SMALL_LN + 1, 0:WIDTH_A])
            update("ln_v_b", row, total_ref[base + SMALL_LN:base + SMALL_LN + 1, WIDTH_A:2 * WIDTH_A])
            for h in range(HEADS_A):
                update("b_s", (l, slice(h, h + 1), slice(None)),
                       total_ref[base + SMALL_BS:base + SMALL_BS + 1, h * HEAD_DIM:(h + 1) * HEAD_DIM])
                lanes = slice(l * WIDTH_A + h * CHUNK, l * WIDTH_A + (h + 1) * CHUNK)
                update("w_s", (l, h), jnp.where(tril, total_ref[TOTAL_WS:TOTAL_ROWS, lanes], 0.0))
        update("final_g", (slice(None), slice(None)), total_ref[TOTAL_HEAD + HEAD_FINAL:TOTAL_HEAD + HEAD_FINAL + 1, :])
        update("conv_w", (slice(None),) * 3, gconv_ref[...])
        update("w_ple_proj", (slice(None),) * 3, gproj_ref[...])

    flat = [a for nm in names for a in st[nm]]
    out_shape = []
    for nm in names:
        out_shape += [jax.ShapeDtypeStruct(st[nm][0].shape, F32)] * (4 if nm in cut else 3)
    def whole(a):
        return pl.BlockSpec(a.shape, lambda i: (0,) * len(a.shape))

    operands = [pltpu.with_memory_space_constraint(a, pltpu.HBM) for a in (total, g_conv, g_proj, *flat)]
    outs = pl.pallas_call(
        body,
        name="adamw_small",
        grid=(1,),
        out_shape=[pltpu.HBM(a.shape, a.dtype) for a in out_shape],
        in_specs=[whole(a) for a in operands],
        out_specs=[whole(a) for a in out_shape],
        compiler_params=pltpu.CompilerParams(dimension_semantics=("arbitrary",), vmem_limit_bytes=32 * MIB),
    )(*operands)
    res, pos = {}, 0
    for nm in names:
        cnt = 4 if nm in cut else 3
        got = tuple(outs[pos:pos + cnt])
        res[nm] = got if nm in cut else ((g_conv if nm == "conv_w" else g_proj),) + got
        pos += cnt
    return res


def _split3_bf16(a):
    b1 = a.astype(BF16)
    r1 = a - b1.astype(F32)
    b2 = r1.astype(BF16)
    b3 = (r1 - b2.astype(F32)).astype(BF16)
    return b1, b2, b3


def _pack_weight_shard(w_in_l, w_out_l, w_gate_l, w_proj_l, conv_w_l):
    w_in_t = jnp.transpose(w_in_l).astype(BF16)
    proj_t = jnp.transpose(w_proj_l).astype(BF16)
    proj_rows = proj_t.reshape(D_MODEL // PLE_DIM, ROWS_PROJ, PLE_DIM).transpose(1, 0, 2).reshape(ROWS_PROJ, D_MODEL)
    conv_parts = jnp.concatenate([b.reshape(-1) for b in _split3_bf16(conv_w_l)])
    conv_rows = jnp.concatenate([conv_parts, jnp.zeros((ROWS_CONV * D_MODEL - conv_parts.shape[0],), BF16)])
    return jnp.concatenate(
        [w_in_t, w_out_l.astype(BF16), w_gate_l.astype(BF16), proj_rows, conv_rows.reshape(ROWS_CONV, D_MODEL)], axis=0
    )


def _unpack_conv(wg):
    per_dev = wg.reshape(N_DEV, ROWS_LAYER, D_MODEL)
    n_conv = (WIDTH_B // N_DEV) * 3
    conv_parts = per_dev[:, OFF_CONV].astype(F32)[:, :3 * n_conv].reshape(N_DEV, 3, n_conv)
    conv = (conv_parts[:, 0] + conv_parts[:, 1]) + conv_parts[:, 2]
    conv_k = jnp.transpose(conv.reshape(WIDTH_B, 3))
    conv_k = jnp.concatenate([conv_k, jnp.zeros((5, WIDTH_B), F32)], axis=0)
    return conv_k


def _unpack_grad_proj(red):
    proj_rows = red[OFF_PROJ:OFF_PROJ + ROWS_PROJ]
    proj_t = proj_rows.reshape(ROWS_PROJ, D_MODEL // PLE_DIM, PLE_DIM).transpose(1, 0, 2).reshape(ROWS_OUT, PLE_DIM)
    return jnp.transpose(proj_t)


def kernel(x, p, norm_g, w_in, ln_v_g, ln_v_b, w_s, b_s, conv_w, w_out, ple_norm_g, w_ple_gate, w_ple_proj, final_g, loss_target, m_norm_g, m_w_in, m_ln_v_g, m_ln_v_b, m_w_s, m_b_s, m_conv_w, m_w_out, m_ple_norm_g, m_w_ple_gate, m_w_ple_proj, m_final_g, v_norm_g, v_w_in, v_ln_v_g, v_ln_v_b, v_w_s, v_b_s, v_conv_w, v_w_out, v_ple_norm_g, v_w_ple_gate, v_w_ple_proj, v_final_g):
    me = 4 * lax.axis_index("x") + 2 * lax.axis_index("y") + lax.axis_index("c")
    xs = x[0]
    target = loss_target[0]

    shards = [_pack_weight_shard(w_in[l], w_out[l], w_ple_gate[l], w_ple_proj[l], conv_w[l]) for l in range(DEPTH)]
    tril = jnp.tril(jnp.ones((CHUNK, CHUNK), F32))

    def consts(l, wg_l):
        conv_k = _unpack_conv(wg_l)
        w_mix = w_s[l] * tril[None]
        small = dict(
            conv_k=conv_k,
            norm_g=norm_g[l].reshape(1, D_MODEL), ln_g=ln_v_g[l].reshape(1, WIDTH_A), ln_b=ln_v_b[l].reshape(1, WIDTH_A),
            w_mix=w_mix.astype(BF16), w_mix_t=jnp.swapaxes(w_mix, 1, 2).astype(BF16),
            b_mix=jnp.broadcast_to(b_s[l][:, :, None], (HEADS_A, CHUNK, HEAD_DIM)),
            ple_g=ple_norm_g[l].reshape(1, D_MODEL),
        )
        return dict({k: pltpu.with_memory_space_constraint(a, pltpu.HBM) for k, a in small.items()}, wg=wg_l)

    layer_consts = [consts(0, _all_gather_rows(shards[0]))]
    saved = []
    h = xs
    for l in range(DEPTH):
        k = layer_consts[l]
        outs = _forward_layer(
            l, h, p, k["wg"], k["conv_k"], k["norm_g"], k["ln_g"], k["ln_b"], k["w_mix"], k["b_mix"],
            k["ple_g"], next_shard=shards[l + 1] if l + 1 < DEPTH else None)
        proj, hn, cat, r, gpre, x1, x2 = outs[:7]
        if l + 1 < DEPTH:
            layer_consts.append(consts(l + 1, outs[7]))
        saved.append(dict(x_in=h, proj=proj, hn=hn, cat=cat, r=r, gpre=gpre, x1=x1))
        h = x2

    smalls, dws = [None] * DEPTH, [None] * DEPTH
    reduced = [None] * DEPTH
    pending = None
    dx = h
    for l in reversed(range(DEPTH)):
        k, s = layer_consts[l], saved[l]
        outs = _backward_layer(
            l, dx, s["x_in"], s["x1"], s["proj"], s["gpre"], p, k["wg"], k["conv_k"],
            k["norm_g"], k["ln_g"], k["ln_b"], k["w_mix"], k["w_mix_t"], k["b_mix"], k["ple_g"],
            loss_head=(target, final_g.reshape(1, D_MODEL)) if l == DEPTH - 1 else None)
        dx, dproj, dx1, dgpre, dpp, smalls[l], dws[l] = outs[:7]
        if l == DEPTH - 1:
            head = outs[7]
        outs = _weight_grads(l, dproj, s["hn"], s["cat"], dx1, s["r"], dgpre, dpp, p, scatter_pack=pending)
        if pending is not None:
            reduced[l + 1] = _sum_pieces(l + 1, outs[1])
        pending = outs[0]
    reduced[0], total = _reduce_scatter_all_reduce(0, pending, smalls, head, dws)
    grad_x = dx[None]
    loss = total[TOTAL_HEAD + HEAD_LOSS, 0]

    n_ch = WIDTH_B // N_DEV
    g_conv = jnp.stack([total[l * SMALL_ROWS + SMALL_CONV:l * SMALL_ROWS + SMALL_CONV + 3, 0:WIDTH_B] for l in range(DEPTH)], axis=1)
    g_conv = lax.dynamic_slice_in_dim(g_conv, me * n_ch, n_ch, axis=2)
    g_proj = jnp.stack([_unpack_grad_proj(reduced[l]) for l in range(DEPTH)])

    def t_in(a):
        return jnp.swapaxes(a, 1, 2)

    def t_conv(a):
        return jnp.transpose(a, (2, 0, 1))

    (r_in,) = _adamw_rows("adamw_w_in", reduced, [OFF_IN], [(t_in(w_in), t_in(m_w_in), t_in(v_w_in))])
    r_out, r_gate = _adamw_rows(
        "adamw_w_out_gate", reduced, [OFF_OUT, OFF_GATE],
        [(w_out, m_w_out, v_w_out), (w_ple_gate, m_w_ple_gate, v_w_ple_gate)])
    small = _adamw_small(total, g_conv, g_proj, dict(
        norm_g=(norm_g, m_norm_g, v_norm_g), ple_norm_g=(ple_norm_g, m_ple_norm_g, v_ple_norm_g),
        ln_v_g=(ln_v_g, m_ln_v_g, v_ln_v_g), ln_v_b=(ln_v_b, m_ln_v_b, v_ln_v_b),
        b_s=(b_s, m_b_s, v_b_s), w_s=(w_s, m_w_s, v_w_s),
        final_g=tuple(a.reshape(1, D_MODEL) for a in (final_g, m_final_g, v_final_g)),
        conv_w=(t_conv(conv_w), t_conv(m_conv_w), t_conv(v_conv_w)),
        w_ple_proj=(w_ple_proj, m_w_ple_proj, v_w_ple_proj),
    ))
    res = dict(small, w_in=tuple(t_in(a) for a in r_in), w_out=r_out, w_ple_gate=r_gate)
    res["final_g"] = tuple(a.reshape(D_MODEL) for a in res["final_g"])
    res["conv_w"] = tuple(jnp.transpose(a, (1, 2, 0)) for a in res["conv_w"])
    order = ["norm_g", "w_in", "ln_v_g", "ln_v_b", "w_s", "b_s", "conv_w", "w_out", "ple_norm_g", "w_ple_gate", "w_ple_proj", "final_g"]
    return (loss, grad_x, *[res[n][0] for n in order], *[res[n][1] for n in order],
            *[res[n][2] for n in order], *[res[n][3] for n in order])
```

```python
import jax
import jax.numpy as jnp
from jax import lax
from jax.experimental import pallas as pl
from jax.experimental.pallas import tpu as pltpu

F32 = jnp.float32
BF16 = jnp.bfloat16

D_MODEL = 1024
WIDTH_A = 512
WIDTH_B = 512
HEADS_A = 4
HEAD_DIM = 128
CHUNK = 128
PLE_DIM = 256
PROJ_WIDTH = 3584
DEPTH = 2
EPS = 1e-6
N_DEV = 8

ADAM_LR = 0.001
ADAM_B1 = 0.9
ADAM_B2 = 0.999
ADAM_EPS = 1e-08
ADAM_WD = 0.01
ADAM_STEP = 10

ROWS_IN = PROJ_WIDTH // N_DEV
ROWS_OUT = D_MODEL // N_DEV
ROWS_GATE = D_MODEL // N_DEV
ROWS_PROJ = (D_MODEL // N_DEV) * PLE_DIM // D_MODEL
ROWS_CONV = 16
OFF_IN = 0
OFF_OUT = OFF_IN + ROWS_IN
OFF_GATE = OFF_OUT + ROWS_OUT
OFF_PROJ = OFF_GATE + ROWS_GATE
OFF_CONV = OFF_PROJ + ROWS_PROJ
ROWS_GRAD = OFF_CONV
ROWS_LAYER = OFF_CONV + ROWS_CONV

SMALL_ROWS = 8
SMALL_NORM = 0
SMALL_PLE = 1
SMALL_LN = 2
SMALL_BS = 3
SMALL_CONV = 4
HEAD_FINAL = 0
HEAD_LOSS = 1
TOTAL_HEAD = DEPTH * SMALL_ROWS
TOTAL_WS = TOTAL_HEAD + SMALL_ROWS
TOTAL_ROWS = TOTAL_WS + CHUNK

MIB = 1024 * 1024
MESH = pl.DeviceIdType.MESH

NT_DIMS = (((1,), (1,)), ((), ()))
TN_DIMS = (((0,), (0,)), ((), ()))


def _dot(a, b):
    return jnp.dot(a, b, preferred_element_type=F32)


def _dot_nt(a, b):
    return lax.dot_general(a, b, NT_DIMS, preferred_element_type=F32)


def _dot_tn(a, b):
    return lax.dot_general(a, b, TN_DIMS, preferred_element_type=F32)


def _colsum8(a):
    rows, n = a.shape
    return jnp.sum(a.reshape(rows // 8, 8, n), axis=0)


def _sigmoid(z):
    return 1.0 / (1.0 + jnp.exp(-z))


def _tile(t, want):
    return want if t % want == 0 else t


class _TwoLevelGather:
    def __init__(self, x_ref, out_ref, m_per, send_sems, recv_sems, local_sem):
        x, y, c = lax.axis_index("x"), lax.axis_index("y"), lax.axis_index("c")
        self.me, self.sibling = (x, y, c), (x, y, 1 - c)
        self.xn, self.yn, self.diag = (1 - x, y, c), (x, 1 - y, c), (1 - x, 1 - y, c)
        self.x_ref, self.out_ref, self.m_per = x_ref, out_ref, m_per
        self.half = (m_per // 32) * 16
        self.send_sems, self.recv_sems = send_sems, recv_sems
        self.mine = pltpu.make_async_copy(x_ref, self.rows(self.me), local_sem)

    def rows(self, block, part=None):
        px, py, pc = block
        base = (4 * px + 2 * py + pc) * self.m_per
        if part is None:
            return self.out_ref.at[pl.ds(base, self.m_per), :]
        if part == 0:
            return self.out_ref.at[pl.ds(base, self.half), :]
        return self.out_ref.at[pl.ds(base + self.half, self.m_per - self.half), :]

    def copy(self, k, block, to, src=None, part=None):
        return pltpu.make_async_remote_copy(
            src_ref=self.rows(block, part) if src is None else src,
            dst_ref=self.rows(block, part),
            send_sem=self.send_sems.at[k],
            recv_sem=self.recv_sems.at[k],
            device_id=to,
            device_id_type=MESH,
        )

    def first(self):
        return [self.copy(0, self.me, self.sibling, src=self.x_ref),
                self.copy(1, self.me, self.xn, src=self.x_ref),
                self.copy(2, self.me, self.yn, src=self.x_ref)]

    def second(self):
        return [self.copy(3, self.xn, self.yn, part=0), self.copy(7, self.yn, self.xn, part=1),
                self.copy(4, self.xn, self.sibling), self.copy(5, self.yn, self.sibling)]

    def third(self):
        return [self.copy(6, self.diag, self.sibling)]

    def start(self):
        self.mine.start()
        for cp in self.first():
            cp.start()

    def pass_on(self):
        fwd_x, fwd_y, sib_x, sib_y = self.second()
        self.copy(1, self.xn, self.me).wait_recv()
        fwd_x.start()
        sib_x.start()
        self.copy(2, self.yn, self.me).wait_recv()
        fwd_y.start()
        sib_y.start()

    def pass_on_diagonal(self):
        self.copy(3, self.diag, self.me, part=0).wait_recv()
        self.copy(7, self.diag, self.me, part=1).wait_recv()
        self.third()[0].start()

    def finish(self):
        sib = (self.sibling[0], self.sibling[1], self.sibling[2])
        self.copy(0, sib, self.me).wait_recv()
        for k, chip in ((4, self.xn), (5, self.yn), (6, self.diag)):
            self.copy(k, (chip[0], chip[1], sib[2]), self.me).wait_recv()
        for cp in self.first() + self.second() + self.third():
            cp.wait_send()
        self.mine.wait()


GATHER_SEMS = [pltpu.SemaphoreType.DMA((8,)), pltpu.SemaphoreType.DMA((8,)), pltpu.SemaphoreType.DMA]


def _all_gather_rows(shard):
    m_per, n = shard.shape

    def body(x_ref, out_ref, send_sems, recv_sems, local_sem):
        ag = _TwoLevelGather(x_ref, out_ref, m_per, send_sems, recv_sems, local_sem)
        ag.start()
        ag.pass_on()
        ag.pass_on_diagonal()
        ag.finish()

    return pl.pallas_call(
        body,
        name="weights_all_gather",
        out_shape=pltpu.HBM((N_DEV * m_per, n), shard.dtype),
        in_specs=[pl.BlockSpec(memory_space=pltpu.HBM)],
        out_specs=pl.BlockSpec(memory_space=pltpu.HBM),
        scratch_shapes=list(GATHER_SEMS),
    )(pltpu.with_memory_space_constraint(shard, pltpu.HBM))


PROJ_PARTS = D_MODEL // PLE_DIM
N_WEIGHT_COPIES = N_DEV * (3 + PROJ_PARTS)


def _weight_copies(wg_ref, w_in_t, w_out, w_gate, w_proj_t, sems):
    copies = []
    for s in range(N_DEV):
        base = s * ROWS_LAYER
        for dst, off, rows in ((w_in_t, OFF_IN, ROWS_IN), (w_out, OFF_OUT, ROWS_OUT), (w_gate, OFF_GATE, ROWS_GATE)):
            copies.append((wg_ref.at[pl.ds(base + off, rows), :], dst.at[pl.ds(s * rows, rows), :]))
        for j in range(PROJ_PARTS):
            copies.append((
                wg_ref.at[pl.ds(base + OFF_PROJ, ROWS_PROJ), pl.ds(j * PLE_DIM, PLE_DIM)],
                w_proj_t.at[pl.ds(s * ROWS_OUT + j * ROWS_PROJ, ROWS_PROJ), :],
            ))
    return [pltpu.make_async_copy(src, dst, sems.at[k]) for k, (src, dst) in enumerate(copies)]


def _forward_layer(layer, x, p_all, wg, conv_k, norm_g, ln_g, ln_b, w_mix, b_mix, ple_g, next_shard=None):
    t = x.shape[0]
    tm = _tile(t, 512)
    nt = t // tm
    gathers = next_shard is not None

    def body(*refs):
        (x_ref, p_ref, wg_ref, cw_ref, ng_ref, lng_ref, lnb_ref, wm_ref, bm_ref, pg_ref) = refs[:10]
        refs = refs[10:]
        if gathers:
            shard_ref, refs = refs[0], refs[1:]
        (proj_ref, hn_ref, cat_ref, r_ref, gpre_ref, x1_ref, x2_ref) = refs[:7]
        refs = refs[7:]
        if gathers:
            gathered_ref, refs = refs[0], refs[1:]
        (w_in_t, w_out, w_gate, wpt_ref, vln_s, mixed_s, halo_s, sems) = refs[:8]
        i = pl.program_id(0)
        if gathers:
            ag = _TwoLevelGather(shard_ref, gathered_ref, ROWS_LAYER, *refs[8:11])

            @pl.when(i == 0)
            def _():
                ag.start()

            @pl.when(i == (5 * nt) // 16)
            def _():
                ag.pass_on()

            @pl.when(i == nt // 2)
            def _():
                ag.pass_on_diagonal()

        @pl.when(i == 0)
        def _():
            copies = _weight_copies(wg_ref, w_in_t, w_out, w_gate, wpt_ref, sems)
            for cp in copies:
                cp.start()
            halo_s[...] = jnp.zeros_like(halo_s)
            for cp in copies:
                cp.wait()

        xv = x_ref[...]
        rstd0 = lax.rsqrt(jnp.mean(xv * xv, axis=-1, keepdims=True) + EPS)
        hn_ref[...] = (xv * rstd0 * ng_ref[...]).astype(BF16)

        def proj_section(k):
            sec = _dot_nt(hn_ref[...], w_in_t[pl.ds(k * 512, 512), :])
            proj_ref[:, k * 512:(k + 1) * 512] = sec.astype(BF16)
            return sec

        v = proj_section(1)
        mu = jnp.mean(v, axis=-1, keepdims=True)
        vc = v - mu
        var = jnp.mean(vc * vc, axis=-1, keepdims=True)
        vln = vc * lax.rsqrt(var + EPS) * lng_ref[...] + lnb_ref[...]
        vln_s[...] = vln.astype(BF16)
        for ci in range(tm // CHUNK):
            rows = pl.ds(ci * CHUNK, CHUNK)
            for h in range(HEADS_A):
                cols = pl.ds(h * HEAD_DIM, HEAD_DIM)
                mixed_s[rows, cols] = _dot(wm_ref[h], vln_s[rows, cols]) + bm_ref[h]
        u = proj_section(0)
        za = proj_section(2)
        out_a = u * mixed_s[...] * (za * _sigmoid(za))
        cat_ref[:, 0:512] = out_a.astype(BF16)

        xc = proj_section(5) * proj_section(3)
        prev = halo_s[...]
        row = lax.broadcasted_iota(jnp.int32, (tm, WIDTH_B), 0)
        xc_m1 = jnp.where(row == 0, prev[7:8, :], pltpu.roll(xc, 1, 0))
        xc_m2 = jnp.where(row == 0, prev[6:7, :], jnp.where(row == 1, prev[7:8, :], pltpu.roll(xc, 2, 0)))
        halo_s[...] = xc[tm - 8:tm, :]
        cw = cw_ref[...]
        yc = cw[0:1, :] * xc_m2 + cw[1:2, :] * xc_m1 + cw[2:3, :] * xc
        zb = proj_section(6)
        out_b = proj_section(4) * yc * (zb * _sigmoid(zb))
        cat_ref[:, 512:1024] = out_b.astype(BF16)

        x1 = xv + _dot(cat_ref[...], w_out[...])
        x1_ref[...] = x1
        rstd1 = lax.rsqrt(jnp.mean(x1 * x1, axis=-1, keepdims=True) + EPS)
        r_ref[...] = (x1 * rstd1 * pg_ref[...]).astype(BF16)
        gpre = _dot(r_ref[...], w_gate[...])
        gpre_ref[...] = gpre.astype(BF16)
        pp = _dot_nt(p_ref[...].astype(BF16), wpt_ref[...])
        x2_ref[...] = x1 + _sigmoid(gpre) * pp

        if gathers:
            @pl.when(i == nt - 1)
            def _():
                ag.finish()

    def tok(width):
        return pl.BlockSpec((tm, width), lambda i: (i, 0))

    def whole(shape):
        return pl.BlockSpec(shape, lambda i: (0,) * len(shape))

    hbm = pl.BlockSpec(memory_space=pl.ANY)
    operands = [x, p_all, wg, conv_k, norm_g, ln_g, ln_b, w_mix, b_mix, ple_g]
    in_specs = [
        tok(D_MODEL), pl.BlockSpec((None, None, tm, PLE_DIM), lambda i: (layer, 0, i, 0)), hbm,
        whole((8, WIDTH_B)), whole((1, D_MODEL)), whole((1, WIDTH_A)), whole((1, WIDTH_A)),
        whole((HEADS_A, CHUNK, CHUNK)), whole((HEADS_A, CHUNK, HEAD_DIM)), whole((1, D_MODEL)),
    ]
    out_specs = [tok(PROJ_WIDTH), tok(D_MODEL), tok(D_MODEL), tok(D_MODEL), tok(D_MODEL), tok(D_MODEL), tok(D_MODEL)]
    out_shape = [
        jax.ShapeDtypeStruct((t, PROJ_WIDTH), BF16),
        jax.ShapeDtypeStruct((t, D_MODEL), BF16),
        jax.ShapeDtypeStruct((t, D_MODEL), BF16),
        jax.ShapeDtypeStruct((t, D_MODEL), BF16),
        jax.ShapeDtypeStruct((t, D_MODEL), BF16),
        jax.ShapeDtypeStruct((t, D_MODEL), F32),
        jax.ShapeDtypeStruct((t, D_MODEL), F32),
    ]
    scratch_shapes = [
        pltpu.VMEM((PROJ_WIDTH, D_MODEL), BF16),
        pltpu.VMEM((D_MODEL, D_MODEL), BF16),
        pltpu.VMEM((D_MODEL, D_MODEL), BF16),
        pltpu.VMEM((D_MODEL, PLE_DIM), BF16),
        pltpu.VMEM((tm, WIDTH_A), BF16),
        pltpu.VMEM((tm, WIDTH_A), F32),
        pltpu.VMEM((8, WIDTH_B), F32),
        pltpu.SemaphoreType.DMA((N_WEIGHT_COPIES,)),
    ]
    if gathers:
        operands.append(pltpu.with_memory_space_constraint(next_shard, pltpu.HBM))
        in_specs.append(pl.BlockSpec(memory_space=pltpu.HBM))
        out_specs.append(pl.BlockSpec(memory_space=pltpu.HBM))
        out_shape.append(pltpu.HBM((N_DEV * ROWS_LAYER, D_MODEL), BF16))
        scratch_shapes += list(GATHER_SEMS)

    return pl.pallas_call(
        body,
        name=f"layer{layer}_forward",
        grid=(nt,),
        in_specs=in_specs,
        out_specs=out_specs,
        out_shape=out_shape,
        scratch_shapes=scratch_shapes,
        compiler_params=pltpu.CompilerParams(dimension_semantics=("arbitrary",), vmem_limit_bytes=56 * MIB),
    )(*operands)


class _DirectScatter:
    def __init__(self, pack_ref, pieces_ref, send_sems, recv_sems, local_sem):
        x, y, c = lax.axis_index("x"), lax.axis_index("y"), lax.axis_index("c")
        me = 4 * x + 2 * y + c
        self.copies = []
        for k in range(N_DEV - 1):
            fx, fy, fc = ((k + 1) >> 2) & 1, ((k + 1) >> 1) & 1, (k + 1) & 1
            tx, ty, tc = x ^ fx, y ^ fy, c ^ fc
            self.copies.append(
                pltpu.make_async_remote_copy(
                    src_ref=pack_ref.at[4 * tx + 2 * ty + tc], dst_ref=pieces_ref.at[me],
                    send_sem=send_sems.at[k], recv_sem=recv_sems.at[k],
                    device_id=(tx, ty, tc), device_id_type=MESH,
                )
            )
        self.mine = pltpu.make_async_copy(pack_ref.at[me], pieces_ref.at[me], local_sem)

    def start(self):
        self.mine.start()
        for cp in self.copies:
            cp.start()

    def finish(self):
        for cp in self.copies:
            cp.wait_recv()
        for cp in self.copies:
            cp.wait_send()
        self.mine.wait()


SCATTER_SEMS = [pltpu.SemaphoreType.DMA((N_DEV - 1,)), pltpu.SemaphoreType.DMA((N_DEV - 1,)), pltpu.SemaphoreType.DMA]


def _backward_layer(layer, dx2, x_in, x1, proj, gpre, p_all, wg, conv_k, norm_g, ln_g, ln_b,
                    w_mix, w_mix_t, b_mix, ple_g, loss_head=None):
    t = x_in.shape[0]
    tm = _tile(t, 256)
    nt = t // tm
    n_chunks = tm // CHUNK
    halo_rows = 16
    heads = loss_head is not None

    def body(*refs):
        (dx2_ref, xin_ref, x1_ref, proj_ref, halo_ref, gpre_ref, p_ref, wg_ref, cw_ref,
         ng_ref, lng_ref, lnb_ref, wm_ref, wmt_ref, bm_ref, pg_ref) = refs[:16]
        refs = refs[16:]
        if heads:
            tgt_ref, fg_ref = refs[:2]
            refs = refs[2:]
        (dxin_ref, dproj_ref, dx1_ref, dgpre_ref, dpp_ref, small_ref, dws_ref) = refs[:7]
        refs = refs[7:]
        if heads:
            head_ref, refs = refs[0], refs[1:]
        (w_in_t, w_out, w_gate, wpt_ref, vln_s, mixed_s, dmix_s, dvln_s, carry_s,
         ng_acc, pg_acc, lng_acc, lnb_acc, cw_acc, dbm_ref, sems) = refs[:16]
        if heads:
            loss_acc, fg_acc = refs[16:18]
        i = pl.program_id(0)
        tile = nt - 1 - i

        @pl.when(i == 0)
        def _():
            copies = _weight_copies(wg_ref, w_in_t, w_out, w_gate, wpt_ref, sems)
            for cp in copies:
                cp.start()
            if heads:
                loss_acc[...] = jnp.zeros_like(loss_acc)
                fg_acc[...] = jnp.zeros_like(fg_acc)
            carry_s[...] = jnp.zeros_like(carry_s)
            ng_acc[...] = jnp.zeros_like(ng_acc)
            pg_acc[...] = jnp.zeros_like(pg_acc)
            lng_acc[...] = jnp.zeros_like(lng_acc)
            lnb_acc[...] = jnp.zeros_like(lnb_acc)
            cw_acc[...] = jnp.zeros_like(cw_acc)
            dws_ref[...] = jnp.zeros_like(dws_ref)
            dbm_ref[...] = jnp.zeros_like(dbm_ref)
            for cp in copies:
                cp.wait()

        if heads:
            x2v = dx2_ref[...]
            fg = fg_ref[...]
            rstdf = lax.rsqrt(jnp.mean(x2v * x2v, axis=-1, keepdims=True) + EPS)
            xhatf = x2v * rstdf
            err = xhatf * fg - tgt_ref[...]
            loss_acc[...] += _colsum8(err * err)
            dy = err * (1.0 / D_MODEL)
            fg_acc[...] += _colsum8(dy * xhatf)
            dxhf = dy * fg
            dx2v = rstdf * (dxhf - xhatf * jnp.mean(dxhf * xhatf, axis=-1, keepdims=True))
        else:
            dx2v = dx2_ref[...]

        gate = _sigmoid(gpre_ref[...].astype(F32))
        pp = _dot_nt(p_ref[...].astype(BF16), wpt_ref[...])
        dpp = dx2v * gate
        dpp_ref[...] = dpp.astype(BF16)
        dgpre = (dpp * pp * (1.0 - gate)).astype(BF16)
        dgpre_ref[...] = dgpre
        dr = _dot_nt(dgpre, w_gate[...])
        x1v = x1_ref[...]
        rstd1 = lax.rsqrt(jnp.mean(x1v * x1v, axis=-1, keepdims=True) + EPS)
        xhat1 = x1v * rstd1
        pg_acc[...] += _colsum8(dr * xhat1)
        dxh = dr * pg_ref[...]
        dx1 = dx2v + rstd1 * (dxh - xhat1 * jnp.mean(dxh * xhat1, axis=-1, keepdims=True))
        dx1b = dx1.astype(BF16)
        dx1_ref[...] = dx1b

        dcat = _dot_nt(dx1b, w_out[...])
        dca = dcat[:, 0:512]
        dcb = dcat[:, 512:1024]

        u = proj_ref[:, 0:512]
        v = proj_ref[:, 512:1024].astype(F32)
        za = proj_ref[:, 1024:1536]
        mu = jnp.mean(v, axis=-1, keepdims=True)
        vc = v - mu
        var = jnp.mean(vc * vc, axis=-1, keepdims=True)
        rs = lax.rsqrt(var + EPS)
        vhat = vc * rs
        lng = lng_ref[...]
        vln_s[...] = (vhat * lng + lnb_ref[...]).astype(BF16)
        for ci in range(n_chunks):
            rows = pl.ds(ci * CHUNK, CHUNK)
            for h in range(HEADS_A):
                cols = pl.ds(h * HEAD_DIM, HEAD_DIM)
                mixed_s[rows, cols] = (_dot(wm_ref[h], vln_s[rows, cols]) + bm_ref[h]).astype(BF16)
        mixed = mixed_s[...]
        sga = _sigmoid(za)
        sa = za * sga
        dsa = sga + sa * (1.0 - sga)

        def put_section(k, val):
            dproj_ref[:, k * 512:(k + 1) * 512] = val.astype(BF16)

        dcab = dca.astype(BF16)
        dca_sa = dcab * sa
        put_section(0, dca_sa * mixed)
        dmix_s[...] = dca_sa * u
        put_section(2, (dcab * dsa) * (u * mixed))
        dbm_acc = jnp.zeros((CHUNK, WIDTH_A), F32)
        for ci in range(n_chunks):
            rows = pl.ds(ci * CHUNK, CHUNK)
            dbm_acc = dbm_acc + dmix_s[rows, :].astype(F32)
            for h in range(HEADS_A):
                cols = pl.ds(h * HEAD_DIM, HEAD_DIM)
                dvln_s[rows, cols] = _dot(wmt_ref[h], dmix_s[rows, cols])
                dws_ref[:, cols] += _dot_nt(dmix_s[rows, cols], vln_s[rows, cols])
        dbm_ref[...] += dbm_acc
        dvln = dvln_s[...]
        lng_acc[...] += _colsum8(dvln * vhat)
        lnb_acc[...] += _colsum8(dvln)
        dvh = dvln * lng
        dv = rs * (dvh - jnp.mean(dvh, axis=-1, keepdims=True) - vhat * jnp.mean(dvh * vhat, axis=-1, keepdims=True))
        put_section(1, dv)

        hb = proj_ref[:, 1536:2048].astype(F32)
        gb = proj_ref[:, 2048:2560]
        gc = proj_ref[:, 2560:3072].astype(F32)
        zb = proj_ref[:, 3072:3584]
        xc = gc * hb
        prev = halo_ref[:, 2560:3072].astype(F32) * halo_ref[:, 1536:2048].astype(F32)
        prev = jnp.where(tile > 0, prev, 0.0)
        row = lax.broadcasted_iota(jnp.int32, (tm, WIDTH_B), 0)
        p1 = prev[halo_rows - 1:halo_rows, :]
        p2 = prev[halo_rows - 2:halo_rows - 1, :]
        xc_m1 = jnp.where(row == 0, p1, pltpu.roll(xc, 1, 0))
        xc_m2 = jnp.where(row == 0, p2, jnp.where(row == 1, p1, pltpu.roll(xc, 2, 0)))
        cw = cw_ref[...]
        yc = cw[0:1, :] * xc_m2 + cw[1:2, :] * xc_m1 + cw[2:3, :] * xc
        sgb = _sigmoid(zb)
        sb = zb * sgb
        dsb = sgb + sb * (1.0 - sgb)
        dcbb = dcb.astype(BF16)
        ycb = yc.astype(BF16)
        dcb_sb = dcbb * sb
        put_section(4, dcb_sb * ycb)
        dyc = (dcb_sb * gb).astype(F32)
        put_section(6, (dcbb * dsb) * (gb * ycb))
        nxt = carry_s[...]
        dyc_p1 = jnp.where(row == tm - 1, nxt[0:1, :], pltpu.roll(dyc, tm - 1, 0))
        dyc_p2 = jnp.where(row == tm - 1, nxt[1:2, :], jnp.where(row == tm - 2, nxt[0:1, :], pltpu.roll(dyc, tm - 2, 0)))
        carry_s[...] = dyc[0:8, :]
        dxc = cw[2:3, :] * dyc + cw[1:2, :] * dyc_p1 + cw[0:1, :] * dyc_p2
        cw_acc[0] += _colsum8(dyc * xc_m2)
        cw_acc[1] += _colsum8(dyc * xc_m1)
        cw_acc[2] += _colsum8(dyc * xc)
        put_section(3, dxc * gc)
        put_section(5, dxc * hb)

        dhn = _dot(dproj_ref[...], w_in_t[...])
        xv = xin_ref[...]
        rstd0 = lax.rsqrt(jnp.mean(xv * xv, axis=-1, keepdims=True) + EPS)
        xhat0 = xv * rstd0
        ng_acc[...] += _colsum8(dhn * xhat0)
        dxh0 = dhn * ng_ref[...]
        dxin_ref[...] = dx1 + rstd0 * (dxh0 - xhat0 * jnp.mean(dxh0 * xhat0, axis=-1, keepdims=True))

        @pl.when(i == nt - 1)
        def _():
            small_ref[...] = jnp.zeros_like(small_ref)
            small_ref[SMALL_NORM:SMALL_NORM + 1, :] = jnp.sum(ng_acc[...], axis=0, keepdims=True)
            small_ref[SMALL_PLE:SMALL_PLE + 1, :] = jnp.sum(pg_acc[...], axis=0, keepdims=True)
            small_ref[SMALL_LN:SMALL_LN + 1, 0:WIDTH_A] = jnp.sum(lng_acc[...], axis=0, keepdims=True)
            small_ref[SMALL_LN:SMALL_LN + 1, WIDTH_A:2 * WIDTH_A] = jnp.sum(lnb_acc[...], axis=0, keepdims=True)
            for h in range(HEADS_A):
                cols = pl.ds(h * HEAD_DIM, HEAD_DIM)
                small_ref[SMALL_BS:SMALL_BS + 1, cols] = jnp.sum(jnp.transpose(dbm_ref[:, cols]), axis=0, keepdims=True)
            for k in range(3):
                small_ref[SMALL_CONV + k:SMALL_CONV + k + 1, 0:WIDTH_B] = jnp.sum(cw_acc[k], axis=0, keepdims=True)
            if heads:
                total = jnp.sum(loss_acc[...]) * (0.5 / D_MODEL)
                rows8 = lax.broadcasted_iota(jnp.int32, (SMALL_ROWS, D_MODEL), 0)
                lanes8 = lax.broadcasted_iota(jnp.int32, (SMALL_ROWS, D_MODEL), 1)
                head_ref[...] = jnp.where((rows8 == HEAD_LOSS) & (lanes8 == 0), total, 0.0)
                head_ref[HEAD_FINAL:HEAD_FINAL + 1, :] = jnp.sum(fg_acc[...], axis=0, keepdims=True)

    def tok(width):
        return pl.BlockSpec((tm, width), lambda i: (nt - 1 - i, 0))

    def whole(shape):
        return pl.BlockSpec(shape, lambda i: (0,) * len(shape))

    halo_spec = pl.BlockSpec(
        (halo_rows, PROJ_WIDTH), lambda i: (jnp.maximum((nt - 1 - i) * (tm // halo_rows) - 1, 0), 0)
    )
    hbm = pl.BlockSpec(memory_space=pl.ANY)
    operands = [dx2, x_in, x1, proj, proj, gpre, p_all, wg, conv_k, norm_g, ln_g, ln_b, w_mix, w_mix_t, b_mix, ple_g]
    in_specs = [
        tok(D_MODEL), tok(D_MODEL), tok(D_MODEL), tok(PROJ_WIDTH), halo_spec, tok(D_MODEL),
        pl.BlockSpec((None, None, tm, PLE_DIM), lambda i: (layer, 0, nt - 1 - i, 0)), hbm,
        whole((8, WIDTH_B)), whole((1, D_MODEL)), whole((1, WIDTH_A)), whole((1, WIDTH_A)),
        whole((HEADS_A, CHUNK, CHUNK)), whole((HEADS_A, CHUNK, CHUNK)), whole((HEADS_A, CHUNK, HEAD_DIM)),
        whole((1, D_MODEL)),
    ]
    out_specs = [
        tok(D_MODEL), tok(PROJ_WIDTH), tok(D_MODEL), tok(D_MODEL), tok(D_MODEL),
        whole((SMALL_ROWS, D_MODEL)), whole((CHUNK, WIDTH_A)),
    ]
    out_shape = [
        jax.ShapeDtypeStruct((t, D_MODEL), F32),
        jax.ShapeDtypeStruct((t, PROJ_WIDTH), BF16),
        jax.ShapeDtypeStruct((t, D_MODEL), BF16),
        jax.ShapeDtypeStruct((t, D_MODEL), BF16),
        jax.ShapeDtypeStruct((t, D_MODEL), BF16),
        jax.ShapeDtypeStruct((SMALL_ROWS, D_MODEL), F32),
        jax.ShapeDtypeStruct((CHUNK, WIDTH_A), F32),
    ]
    scratch_shapes = [
        pltpu.VMEM((PROJ_WIDTH, D_MODEL), BF16),
        pltpu.VMEM((D_MODEL, D_MODEL), BF16),
        pltpu.VMEM((D_MODEL, D_MODEL), BF16),
        pltpu.VMEM((D_MODEL, PLE_DIM), BF16),
        pltpu.VMEM((tm, WIDTH_A), BF16),
        pltpu.VMEM((tm, WIDTH_A), BF16),
        pltpu.VMEM((tm, WIDTH_A), BF16),
        pltpu.VMEM((tm, WIDTH_A), F32),
        pltpu.VMEM((8, WIDTH_B), F32),
        pltpu.VMEM((8, D_MODEL), F32),
        pltpu.VMEM((8, D_MODEL), F32),
        pltpu.VMEM((8, WIDTH_A), F32),
        pltpu.VMEM((8, WIDTH_A), F32),
        pltpu.VMEM((3, 8, WIDTH_B), F32),
        pltpu.VMEM((CHUNK, WIDTH_A), F32),
        pltpu.SemaphoreType.DMA((N_WEIGHT_COPIES,)),
    ]
    if heads:
        operands += list(loss_head)
        in_specs += [tok(D_MODEL), whole((1, D_MODEL))]
        out_specs.append(whole((SMALL_ROWS, D_MODEL)))
        out_shape.append(jax.ShapeDtypeStruct((SMALL_ROWS, D_MODEL), F32))
        scratch_shapes += [pltpu.VMEM((8, D_MODEL), F32), pltpu.VMEM((8, D_MODEL), F32)]

    return pl.pallas_call(
        body,
        name=f"layer{layer}_backward",
        grid=(nt,),
        in_specs=in_specs,
        out_specs=out_specs,
        out_shape=out_shape,
        scratch_shapes=scratch_shapes,
        compiler_params=pltpu.CompilerParams(dimension_semantics=("arbitrary",), vmem_limit_bytes=56 * MIB),
    )(*operands)


def _sum_pieces(layer, pieces):
    rows, n = pieces.shape[1], pieces.shape[2]
    blocks = 2
    rb = rows // blocks

    def body(p_ref, out_ref):
        total = p_ref[0].astype(F32)
        for j in range(1, N_DEV):
            total = total + p_ref[j].astype(F32)
        out_ref[...] = total

    return pl.pallas_call(
        body,
        name=f"layer{layer}_grad_sum",
        grid=(blocks,),
        out_shape=pltpu.HBM((rows, n), F32),
        in_specs=[pl.BlockSpec((N_DEV, rb, n), lambda i: (0, i, 0))],
        out_specs=pl.BlockSpec((rb, n), lambda i: (i, 0)),
        compiler_params=pltpu.CompilerParams(dimension_semantics=("arbitrary",), vmem_limit_bytes=32 * MIB),
    )(pieces)


def _weight_grads(layer, dproj, hn, cat, dx1, r, dgpre, dpp, p_all, scatter_pack=None):
    t = hn.shape[0]
    tk = _tile(t, 512)
    nt = t // tk
    in_blocks = PROJ_WIDTH // 512
    scatters = scatter_pack is not None

    def body(*refs):
        (dproj_ref, hn_ref, cat_ref, dx1_ref, r_ref, dgpre_ref, dpp_ref, p_ref) = refs[:8]
        refs = refs[8:]
        if scatters:
            prior_ref, refs = refs[0], refs[1:]
        pack_ref, refs = refs[0], refs[1:]
        if scatters:
            pieces_ref, refs = refs[0], refs[1:]
        (acc_in, acc_out, acc_gate, acc_proj, stage, sems) = refs[:6]
        i = pl.program_id(0)
        if scatters:
            scatter = _DirectScatter(prior_ref, pieces_ref, *refs[6:9])

            @pl.when(i == 0)
            def _():
                scatter.start()

        @pl.when(i == 0)
        def _():
            acc_in[...] = jnp.zeros_like(acc_in)
            acc_out[...] = jnp.zeros_like(acc_out)
            acc_gate[...] = jnp.zeros_like(acc_gate)
            acc_proj[...] = jnp.zeros_like(acc_proj)

        hnv = hn_ref[...]
        for b in range(in_blocks):
            acc_in[pl.ds(b * 512, 512), :] += _dot_tn(dproj_ref[:, b * 512:(b + 1) * 512], hnv)
        dx1v = dx1_ref[...]
        dgv = dgpre_ref[...]
        for b in range(D_MODEL // 512):
            acc_out[pl.ds(b * 512, 512), :] += _dot_tn(cat_ref[:, b * 512:(b + 1) * 512], dx1v)
            acc_gate[pl.ds(b * 512, 512), :] += _dot_tn(r_ref[:, b * 512:(b + 1) * 512], dgv)
        pv = p_ref[...].astype(BF16)
        for b in range(D_MODEL // 512):
            acc_proj[pl.ds(b * 512, 512), :] += _dot_tn(dpp_ref[:, b * 512:(b + 1) * 512], pv)

        @pl.when(i == nt - 1)
        def _():
            def out_copy(s):
                return pltpu.make_async_copy(stage.at[s % 2], pack_ref.at[s], sems.at[s % 2])

            for s in range(N_DEV):
                if s >= 2:
                    out_copy(s - 2).wait()
                buf = stage.at[s % 2]
                buf[pl.ds(OFF_IN, ROWS_IN), :] = acc_in[pl.ds(s * ROWS_IN, ROWS_IN), :].astype(BF16)
                buf[pl.ds(OFF_OUT, ROWS_OUT), :] = acc_out[pl.ds(s * ROWS_OUT, ROWS_OUT), :].astype(BF16)
                buf[pl.ds(OFF_GATE, ROWS_GATE), :] = acc_gate[pl.ds(s * ROWS_GATE, ROWS_GATE), :].astype(BF16)
                for j in range(D_MODEL // PLE_DIM):
                    buf[pl.ds(OFF_PROJ, ROWS_PROJ), pl.ds(j * PLE_DIM, PLE_DIM)] = acc_proj[
                        pl.ds(s * ROWS_OUT + j * ROWS_PROJ, ROWS_PROJ), :
                    ].astype(BF16)
                out_copy(s).start()
            out_copy(N_DEV - 2).wait()
            out_copy(N_DEV - 1).wait()
            if scatters:
                scatter.finish()

    def tok(width):
        return pl.BlockSpec((tk, width), lambda i: (i, 0))

    hbm = pl.BlockSpec(memory_space=pl.ANY)
    pack_shape = jax.ShapeDtypeStruct((N_DEV, ROWS_GRAD, D_MODEL), BF16)
    operands = [dproj, hn, cat, dx1, r, dgpre, dpp, p_all]
    in_specs = [tok(PROJ_WIDTH), tok(D_MODEL), tok(D_MODEL), tok(D_MODEL), tok(D_MODEL), tok(D_MODEL), tok(D_MODEL),
                pl.BlockSpec((None, None, tk, PLE_DIM), lambda i: (layer, 0, i, 0))]
    out_specs, out_shape = [hbm], [pack_shape]
    scratch_shapes = [
        pltpu.VMEM((PROJ_WIDTH, D_MODEL), F32),
        pltpu.VMEM((D_MODEL, D_MODEL), F32),
        pltpu.VMEM((D_MODEL, D_MODEL), F32),
        pltpu.VMEM((D_MODEL, PLE_DIM), F32),
        pltpu.VMEM((2, ROWS_GRAD, D_MODEL), BF16),
        pltpu.SemaphoreType.DMA((2,)),
    ]
    if scatters:
        operands.append(scatter_pack)
        in_specs.append(hbm)
        out_specs.append(hbm)
        out_shape.append(pack_shape)
        scratch_shapes += list(SCATTER_SEMS)

    return pl.pallas_call(
        body,
        name=f"layer{layer}_weight_grads",
        grid=(nt,),
        in_specs=in_specs,
        out_specs=out_specs,
        out_shape=out_shape,
        scratch_shapes=scratch_shapes,
        compiler_params=pltpu.CompilerParams(dimension_semantics=("arbitrary",), vmem_limit_bytes=58 * MIB),
    )(*operands)


def _reduce_scatter_all_reduce(layer, pack, smalls, head, dws):
    rows, n = pack.shape[1], pack.shape[2]
    assert DEPTH * WIDTH_A == D_MODEL and n == D_MODEL

    def body(g_ref, *refs):
        small_refs, refs = refs[:DEPTH], refs[DEPTH:]
        head_ref, refs = refs[0], refs[1:]
        dws_refs, refs = refs[:DEPTH], refs[DEPTH:]
        (out_ref, total_ref, r1, a_s, r2, via, sp, sr1, sq, send1, recv1, send2, recv2, ssend, srecv) = refs
        x, y, c = lax.axis_index("x"), lax.axis_index("y"), lax.axis_index("c")
        sibling = (x, y, 1 - c)
        chip = 2 * x + y
        flips = [(1, 0), (0, 1), (1, 1)]

        for l in range(DEPTH):
            sp[l * SMALL_ROWS:(l + 1) * SMALL_ROWS, :] = small_refs[l][...]
            sp[TOTAL_WS:TOTAL_ROWS, l * WIDTH_A:(l + 1) * WIDTH_A] = dws_refs[l][...]
        sp[TOTAL_HEAD:TOTAL_WS, :] = head_ref[...]

        small_pair = pltpu.make_async_remote_copy(
            src_ref=sp, dst_ref=sr1, send_sem=ssend.at[0], recv_sem=srecv.at[0], device_id=sibling, device_id_type=MESH
        )

        def to_sibling(j):
            return pltpu.make_async_remote_copy(
                src_ref=g_ref.at[2 * j + 1 - c], dst_ref=r1.at[j], send_sem=send1.at[j], recv_sem=recv1.at[j],
                device_id=sibling, device_id_type=MESH,
            )

        first = [to_sibling(j) for j in range(4)]
        small_pair.start()
        for cp in first:
            cp.start()

        small_pair.wait_recv()
        sq[chip] = sp[...] + sr1[...]
        small_chips = [
            pltpu.make_async_remote_copy(
                src_ref=sq.at[chip], dst_ref=sq.at[chip], send_sem=ssend.at[1 + k], recv_sem=srecv.at[1 + k],
                device_id=(x ^ fx, y ^ fy, c), device_id_type=MESH,
            )
            for k, (fx, fy) in enumerate(flips)
        ]
        for cp in small_chips:
            cp.start()

        for j in range(4):
            first[j].wait_recv()

            @pl.when(chip != j)
            def _():
                a_s[j] = (g_ref[2 * j + c].astype(F32) + r1[j].astype(F32)).astype(BF16)

        half = rows // 2
        lo, hi = pl.ds(0, half), pl.ds(half, rows - half)
        x_nbr, y_nbr = (1 - x, y, c), (x, 1 - y, c)
        chip_x, chip_y, chip_d = 2 * (1 - x) + y, 2 * x + (1 - y), 2 * (1 - x) + (1 - y)

        def ici(k, src, dst, to):
            return pltpu.make_async_remote_copy(
                src_ref=src, dst_ref=dst, send_sem=send2.at[k], recv_sem=recv2.at[k], device_id=to, device_id_type=MESH)

        second = [
            ici(0, a_s.at[chip_d, lo, :], via.at[0], x_nbr),
            ici(1, a_s.at[chip_d, hi, :], via.at[1], y_nbr),
            ici(2, a_s.at[chip_x, lo, :], r2.at[0, lo, :], x_nbr),
            ici(3, a_s.at[chip_y, hi, :], r2.at[1, hi, :], y_nbr),
            ici(4, a_s.at[chip_x, hi, :], r2.at[0, hi, :], x_nbr),
            ici(5, a_s.at[chip_y, lo, :], r2.at[1, lo, :], y_nbr),
        ]
        for cp in second[:4]:
            cp.start()

        out_ref[...] = g_ref[2 * chip + c].astype(F32) + r1[chip].astype(F32)
        for cp in small_chips:
            cp.wait_recv()
        total_ref[...] = ((sq[0] + sq[1]) + sq[2]) + sq[3]

        second[0].wait_recv()
        a_s[chip_y, lo, :] = (a_s[chip_y, lo, :].astype(F32) + via[0].astype(F32)).astype(BF16)
        second[5].start()
        second[1].wait_recv()
        a_s[chip_x, hi, :] = (a_s[chip_x, hi, :].astype(F32) + via[1].astype(F32)).astype(BF16)
        second[4].start()

        second[2].wait_recv()
        second[4].wait_recv()
        out_ref[...] += r2[0].astype(F32)
        second[3].wait_recv()
        second[5].wait_recv()
        out_ref[...] += r2[1].astype(F32)
        small_pair.wait_send()
        for cp in first + small_chips + second:
            cp.wait_send()

    vmem = pl.BlockSpec(memory_space=pltpu.VMEM)
    return pl.pallas_call(
        body,
        name=f"layer{layer}_grad_reduce_scatter",
        out_shape=[jax.ShapeDtypeStruct((rows, n), F32), jax.ShapeDtypeStruct((TOTAL_ROWS, D_MODEL), F32)],
        in_specs=[vmem] * (2 + 2 * DEPTH),
        out_specs=[vmem, vmem],
        scratch_shapes=[
            pltpu.VMEM((4, rows, n), BF16),
            pltpu.VMEM((4, rows, n), BF16),
            pltpu.VMEM((2, rows, n), BF16),
            pltpu.VMEM((2, rows // 2, n), BF16),
            pltpu.VMEM((TOTAL_ROWS, D_MODEL), F32),
            pltpu.VMEM((TOTAL_ROWS, D_MODEL), F32),
            pltpu.VMEM((4, TOTAL_ROWS, D_MODEL), F32),
            pltpu.SemaphoreType.DMA((4,)),
            pltpu.SemaphoreType.DMA((4,)),
            pltpu.SemaphoreType.DMA((6,)),
            pltpu.SemaphoreType.DMA((6,)),
            pltpu.SemaphoreType.DMA((4,)),
            pltpu.SemaphoreType.DMA((4,)),
        ],
        compiler_params=pltpu.CompilerParams(vmem_limit_bytes=48 * MIB),
    )(pack, *smalls, head, *dws)


def _adam_step(w, g, m, v):
    m = ADAM_B1 * m + (1.0 - ADAM_B1) * g
    v = ADAM_B2 * v + (1.0 - ADAM_B2) * (g * g)
    m_hat = m / (1.0 - ADAM_B1 ** ADAM_STEP)
    v_hat = v / (1.0 - ADAM_B2 ** ADAM_STEP)
    return -ADAM_LR * (m_hat / (jnp.sqrt(v_hat) + ADAM_EPS) + ADAM_WD * w), m, v


def _adamw_rows(name, reduced, row_off, states):
    n = len(states)
    parts = 4

    def body(*refs):
        red = refs[:DEPTH * n]
        ins = refs[DEPTH * n:DEPTH * n + 3 * n]
        outs = refs[DEPTH * n + 3 * n:]
        layer = pl.program_id(0)
        for l in range(DEPTH):
            @pl.when(layer == l)
            def _():
                for k in range(n):
                    w_ref, m_ref, v_ref = ins[3 * k:3 * k + 3]
                    g_ref, d_ref, nm_ref, nv_ref = outs[4 * k:4 * k + 4]
                    g = red[DEPTH * k + l][...]
                    d, m, v = _adam_step(w_ref[...], g, m_ref[...], v_ref[...])
                    g_ref[...] = g
                    d_ref[...] = d
                    nm_ref[...] = m
                    nv_ref[...] = v

    flat = [a for st in states for a in st]
    red_ops, red_specs, state_specs, out_specs, out_shape = [], [], [], [], []
    for k, (w, _, _) in enumerate(states):
        rb = w.shape[1] // parts
        assert w.shape[1] % parts == 0 and rb % 8 == 0 and row_off[k] % rb == 0
        spec = pl.BlockSpec((None, rb, w.shape[2]), lambda l, j: (l, j, 0))
        state_specs += [spec] * 3
        out_specs += [spec] * 4
        out_shape += [jax.ShapeDtypeStruct(w.shape, F32)] * 4
        first = row_off[k] // rb
        red_ops += list(reduced)
        red_specs += [pl.BlockSpec((rb, w.shape[2]), lambda l, j, first=first: (first + j, 0))] * DEPTH
    operands = [pltpu.with_memory_space_constraint(a, pltpu.HBM) for a in (*red_ops, *flat)]
    outs = pl.pallas_call(
        body,
        name=name,
        grid=(DEPTH, parts),
        out_shape=[pltpu.HBM(a.shape, a.dtype) for a in out_shape],
        in_specs=red_specs + state_specs,
        out_specs=out_specs,
        compiler_params=pltpu.CompilerParams(dimension_semantics=("arbitrary", "arbitrary"), vmem_limit_bytes=32 * MIB),
    )(*operands)
    return [tuple(outs[4 * k:4 * k + 4]) for k in range(n)]


def _adamw_small(total, g_conv, g_proj, st):
    names = ["norm_g", "ple_norm_g", "ln_v_g", "ln_v_b", "b_s", "w_s", "final_g", "conv_w", "w_ple_proj"]
    cut = names[:7]

    def body(total_ref, gconv_ref, gproj_ref, *refs):
        ins = {nm: refs[3 * k:3 * k + 3] for k, nm in enumerate(names)}
        outs, pos = {}, 3 * len(names)
        for nm in names:
            cnt = 4 if nm in cut else 3
            outs[nm] = refs[pos:pos + cnt]
            pos += cnt

        def update(nm, idx, g):
            w_ref, m_ref, v_ref = ins[nm]
            d, m, v = _adam_step(w_ref[idx], g, m_ref[idx], v_ref[idx])
            o = outs[nm]
            if nm in cut:
                o[0][idx] = g
                o = o[1:]
            o[0][idx] = d
            o[1][idx] = m
            o[2][idx] = v

        tril = (lax.broadcasted_iota(jnp.int32, (CHUNK, CHUNK), 0) >= lax.broadcasted_iota(jnp.int32, (CHUNK, CHUNK), 1))
        for l in range(DEPTH):
            base = l * SMALL_ROWS
            row = (slice(l, l + 1), slice(None))
            update("norm_g", row, total_ref[base + SMALL_NORM:base + SMALL_NORM + 1, :])
            update("ple_norm_g", row, total_ref[base + SMALL_PLE:base + SMALL_PLE + 1, :])
            update("ln_v_g", row, total_ref[base + SMALL_LN:base + SMALL_LN + 1, 0:WIDTH_A])
            update("ln_v_b", row, total_ref[base + SMALL_LN:base + SMALL_LN + 1, WIDTH_A:2 * WIDTH_A])
            for h in range(HEADS_A):
                update("b_s", (l, slice(h, h + 1), slice(None)),
                       total_ref[base + SMALL_BS:base + SMALL_BS + 1, h * HEAD_DIM:(h + 1) * HEAD_DIM])
                lanes = slice(l * WIDTH_A + h * CHUNK, l * WIDTH_A + (h + 1) * CHUNK)
                update("w_s", (l, h), jnp.where(tril, total_ref[TOTAL_WS:TOTAL_ROWS, lanes], 0.0))
        update("final_g", (slice(None), slice(None)), total_ref[TOTAL_HEAD + HEAD_FINAL:TOTAL_HEAD + HEAD_FINAL + 1, :])
        update("conv_w", (slice(None),) * 3, gconv_ref[...])
        update("w_ple_proj", (slice(None),) * 3, gproj_ref[...])

    flat = [a for nm in names for a in st[nm]]
    out_shape = []
    for nm in names:
        out_shape += [jax.ShapeDtypeStruct(st[nm][0].shape, F32)] * (4 if nm in cut else 3)
    def whole(a):
        return pl.BlockSpec(a.shape, lambda i: (0,) * len(a.shape))

    operands = [pltpu.with_memory_space_constraint(a, pltpu.HBM) for a in (total, g_conv, g_proj, *flat)]
    outs = pl.pallas_call(
        body,
        name="adamw_small",
        grid=(1,),
        out_shape=[pltpu.HBM(a.shape, a.dtype) for a in out_shape],
        in_specs=[whole(a) for a in operands],
        out_specs=[whole(a) for a in out_shape],
        compiler_params=pltpu.CompilerParams(dimension_semantics=("arbitrary",), vmem_limit_bytes=32 * MIB),
    )(*operands)
    res, pos = {}, 0
    for nm in names:
        cnt = 4 if nm in cut else 3
        got = tuple(outs[pos:pos + cnt])
        res[nm] = got if nm in cut else ((g_conv if nm == "conv_w" else g_proj),) + got
        pos += cnt
    return res


def _split3_bf16(a):
    b1 = a.astype(BF16)
    r1 = a - b1.astype(F32)
    b2 = r1.astype(BF16)
    b3 = (r1 - b2.astype(F32)).astype(BF16)
    return b1, b2, b3


def _pack_weight_shard(w_in_l, w_out_l, w_gate_l, w_proj_l, conv_w_l):
    w_in_t = jnp.transpose(w_in_l).astype(BF16)
    proj_t = jnp.transpose(w_proj_l).astype(BF16)
    proj_rows = proj_t.reshape(D_MODEL // PLE_DIM, ROWS_PROJ, PLE_DIM).transpose(1, 0, 2).reshape(ROWS_PROJ, D_MODEL)
    conv_parts = jnp.concatenate([b.reshape(-1) for b in _split3_bf16(conv_w_l)])
    conv_rows = jnp.concatenate([conv_parts, jnp.zeros((ROWS_CONV * D_MODEL - conv_parts.shape[0],), BF16)])
    return jnp.concatenate(
        [w_in_t, w_out_l.astype(BF16), w_gate_l.astype(BF16), proj_rows, conv_rows.reshape(ROWS_CONV, D_MODEL)], axis=0
    )


def _unpack_conv(wg):
    per_dev = wg.reshape(N_DEV, ROWS_LAYER, D_MODEL)
    n_conv = (WIDTH_B // N_DEV) * 3
    conv_parts = per_dev[:, OFF_CONV].astype(F32)[:, :3 * n_conv].reshape(N_DEV, 3, n_conv)
    conv = (conv_parts[:, 0] + conv_parts[:, 1]) + conv_parts[:, 2]
    conv_k = jnp.transpose(conv.reshape(WIDTH_B, 3))
    conv_k = jnp.concatenate([conv_k, jnp.zeros((5, WIDTH_B), F32)], axis=0)
    return conv_k


def _unpack_grad_proj(red):
    proj_rows = red[OFF_PROJ:OFF_PROJ + ROWS_PROJ]
    proj_t = proj_rows.reshape(ROWS_PROJ, D_MODEL // PLE_DIM, PLE_DIM).transpose(1, 0, 2).reshape(ROWS_OUT, PLE_DIM)
    return jnp.transpose(proj_t)


def kernel(x, p, norm_g, w_in, ln_v_g, ln_v_b, w_s, b_s, conv_w, w_out, ple_norm_g, w_ple_gate, w_ple_proj, final_g, loss_target, m_norm_g, m_w_in, m_ln_v_g, m_ln_v_b, m_w_s, m_b_s, m_conv_w, m_w_out, m_ple_norm_g, m_w_ple_gate, m_w_ple_proj, m_final_g, v_norm_g, v_w_in, v_ln_v_g, v_ln_v_b, v_w_s, v_b_s, v_conv_w, v_w_out, v_ple_norm_g, v_w_ple_gate, v_w_ple_proj, v_final_g):
    me = 4 * lax.axis_index("x") + 2 * lax.axis_index("y") + lax.axis_index("c")
    xs = x[0]
    target = loss_target[0]

    shards = [_pack_weight_shard(w_in[l], w_out[l], w_ple_gate[l], w_ple_proj[l], conv_w[l]) for l in range(DEPTH)]
    tril = jnp.tril(jnp.ones((CHUNK, CHUNK), F32))

    def consts(l, wg_l):
        conv_k = _unpack_conv(wg_l)
        w_mix = w_s[l] * tril[None]
        small = dict(
            conv_k=conv_k,
            norm_g=norm_g[l].reshape(1, D_MODEL), ln_g=ln_v_g[l].reshape(1, WIDTH_A), ln_b=ln_v_b[l].reshape(1, WIDTH_A),
            w_mix=w_mix.astype(BF16), w_mix_t=jnp.swapaxes(w_mix, 1, 2).astype(BF16),
            b_mix=jnp.broadcast_to(b_s[l][:, :, None], (HEADS_A, CHUNK, HEAD_DIM)),
            ple_g=ple_norm_g[l].reshape(1, D_MODEL),
        )
        return dict({k: pltpu.with_memory_space_constraint(a, pltpu.HBM) for k, a in small.items()}, wg=wg_l)

    layer_consts = [consts(0, _all_gather_rows(shards[0]))]
    saved = []
    h = xs
    for l in range(DEPTH):
        k = layer_consts[l]
        outs = _forward_layer(
            l, h, p, k["wg"], k["conv_k"], k["norm_g"], k["ln_g"], k["ln_b"], k["w_mix"], k["b_mix"],
            k["ple_g"], next_shard=shards[l + 1] if l + 1 < DEPTH else None)
        proj, hn, cat, r, gpre, x1, x2 = outs[:7]
        if l + 1 < DEPTH:
            layer_consts.append(consts(l + 1, outs[7]))
        saved.append(dict(x_in=h, proj=proj, hn=hn, cat=cat, r=r, gpre=gpre, x1=x1))
        h = x2

    smalls, dws = [None] * DEPTH, [None] * DEPTH
    reduced = [None] * DEPTH
    pending = None
    dx = h
    for l in reversed(range(DEPTH)):
        k, s = layer_consts[l], saved[l]
        outs = _backward_layer(
            l, dx, s["x_in"], s["x1"], s["proj"], s["gpre"], p, k["wg"], k["conv_k"],
            k["norm_g"], k["ln_g"], k["ln_b"], k["w_mix"], k["w_mix_t"], k["b_mix"], k["ple_g"],
            loss_head=(target, final_g.reshape(1, D_MODEL)) if l == DEPTH - 1 else None)
        dx, dproj, dx1, dgpre, dpp, smalls[l], dws[l] = outs[:7]
        if l == DEPTH - 1:
            head = outs[7]
        outs = _weight_grads(l, dproj, s["hn"], s["cat"], dx1, s["r"], dgpre, dpp, p, scatter_pack=pending)
        if pending is not None:
            reduced[l + 1] = _sum_pieces(l + 1, outs[1])
        pending = outs[0]
    reduced[0], total = _reduce_scatter_all_reduce(0, pending, smalls, head, dws)
    grad_x = dx[None]
    loss = total[TOTAL_HEAD + HEAD_LOSS, 0]

    n_ch = WIDTH_B // N_DEV
    g_conv = jnp.stack([total[l * SMALL_ROWS + SMALL_CONV:l * SMALL_ROWS + SMALL_CONV + 3, 0:WIDTH_B] for l in range(DEPTH)], axis=1)
    g_conv = lax.dynamic_slice_in_dim(g_conv, me * n_ch, n_ch, axis=2)
    g_proj = jnp.stack([_unpack_grad_proj(reduced[l]) for l in range(DEPTH)])

    def t_in(a):
        return jnp.swapaxes(a, 1, 2)

    def t_conv(a):
        return jnp.transpose(a, (2, 0, 1))

    r_in, r_out, r_gate = _adamw_rows(
        "adamw_sharded", reduced, [OFF_IN, OFF_OUT, OFF_GATE],
        [(t_in(w_in), t_in(m_w_in), t_in(v_w_in)), (w_out, m_w_out, v_w_out), (w_ple_gate, m_w_ple_gate, v_w_ple_gate)])
    small = _adamw_small(total, g_conv, g_proj, dict(
        norm_g=(norm_g, m_norm_g, v_norm_g), ple_norm_g=(ple_norm_g, m_ple_norm_g, v_ple_norm_g),
        ln_v_g=(ln_v_g, m_ln_v_g, v_ln_v_g), ln_v_b=(ln_v_b, m_ln_v_b, v_ln_v_b),
        b_s=(b_s, m_b_s, v_b_s), w_s=(w_s, m_w_s, v_w_s),
        final_g=tuple(a.reshape(1, D_MODEL) for a in (final_g, m_final_g, v_final_g)),
        conv_w=(t_conv(conv_w), t_conv(m_conv_w), t_conv(v_conv_w)),
        w_ple_proj=(w_ple_proj, m_w_ple_proj, v_w_ple_proj),
    ))
    res = dict(small, w_in=tuple(t_in(a) for a in r_in), w_out=r_out, w_ple_gate=r_gate)
    res["final_g"] = tuple(a.reshape(D_MODEL) for a in res["final_g"])
    res["conv_w"] = tuple(jnp.transpose(a, (1, 2, 0)) for a in res["conv_w"])
    order = ["norm_g", "w_in", "ln_v_g", "ln_v_b", "w_s", "b_s", "conv_w", "w_out", "ple_norm_g", "w_ple_gate", "w_ple_proj", "final_g"]
    return (loss, grad_x, *[res[n][0] for n in order], *[res[n][1] for n in order],
            *[res[n][2] for n in order], *[res[n][3] for n in order])
```

```python
import jax
import jax.numpy as jnp
from jax import lax
from jax.experimental import pallas as pl
from jax.experimental.pallas import tpu as pltpu

F32 = jnp.float32
BF16 = jnp.bfloat16

D_MODEL = 1024
WIDTH_A = 512
WIDTH_B = 512
HEADS_A = 4
HEAD_DIM = 128
CHUNK = 128
PLE_DIM = 256
PROJ_WIDTH = 3584
DEPTH = 2
EPS = 1e-6
N_DEV = 8

ADAM_LR = 0.001
ADAM_B1 = 0.9
ADAM_B2 = 0.999
ADAM_EPS = 1e-08
ADAM_WD = 0.01
ADAM_STEP = 10

ROWS_IN = PROJ_WIDTH // N_DEV
ROWS_OUT = D_MODEL // N_DEV
ROWS_GATE = D_MODEL // N_DEV
ROWS_PROJ = (D_MODEL // N_DEV) * PLE_DIM // D_MODEL
ROWS_CONV = 16
OFF_IN = 0
OFF_OUT = OFF_IN + ROWS_IN
OFF_GATE = OFF_OUT + ROWS_OUT
OFF_PROJ = OFF_GATE + ROWS_GATE
OFF_CONV = OFF_PROJ + ROWS_PROJ
ROWS_GRAD = OFF_CONV
ROWS_LAYER = OFF_CONV + ROWS_CONV

SMALL_ROWS = 8
SMALL_NORM = 0
SMALL_PLE = 1
SMALL_LN = 2
SMALL_BS = 3
SMALL_CONV = 4
HEAD_FINAL = 0
HEAD_LOSS = 1
TOTAL_HEAD = DEPTH * SMALL_ROWS
TOTAL_WS = TOTAL_HEAD + SMALL_ROWS
TOTAL_ROWS = TOTAL_WS + CHUNK

MIB = 1024 * 1024
MESH = pl.DeviceIdType.MESH

NT_DIMS = (((1,), (1,)), ((), ()))
TN_DIMS = (((0,), (0,)), ((), ()))


def _dot(a, b):
    return jnp.dot(a, b, preferred_element_type=F32)


def _dot_nt(a, b):
    return lax.dot_general(a, b, NT_DIMS, preferred_element_type=F32)


def _dot_tn(a, b):
    return lax.dot_general(a, b, TN_DIMS, preferred_element_type=F32)


def _colsum8(a):
    rows, n = a.shape
    return jnp.sum(a.reshape(rows // 8, 8, n), axis=0)


def _sigmoid(z):
    return 1.0 / (1.0 + jnp.exp(-z))


def _tile(t, want):
    return want if t % want == 0 else t


class _TwoLevelGather:
    def __init__(self, x_ref, out_ref, m_per, send_sems, recv_sems, local_sem):
        x, y, c = lax.axis_index("x"), lax.axis_index("y"), lax.axis_index("c")
        self.me, self.sibling = (x, y, c), (x, y, 1 - c)
        self.xn, self.yn, self.diag = (1 - x, y, c), (x, 1 - y, c), (1 - x, 1 - y, c)
        self.x_ref, self.out_ref, self.m_per = x_ref, out_ref, m_per
        self.half = (m_per // 32) * 16
        self.send_sems, self.recv_sems = send_sems, recv_sems
        self.mine = pltpu.make_async_copy(x_ref, self.rows(self.me), local_sem)

    def rows(self, block, part=None):
        px, py, pc = block
        base = (4 * px + 2 * py + pc) * self.m_per
        if part is None:
            return self.out_ref.at[pl.ds(base, self.m_per), :]
        if part == 0:
            return self.out_ref.at[pl.ds(base, self.half), :]
        return self.out_ref.at[pl.ds(base + self.half, self.m_per - self.half), :]

    def copy(self, k, block, to, src=None, part=None):
        return pltpu.make_async_remote_copy(
            src_ref=self.rows(block, part) if src is None else src,
            dst_ref=self.rows(block, part),
            send_sem=self.send_sems.at[k],
            recv_sem=self.recv_sems.at[k],
            device_id=to,
            device_id_type=MESH,
        )

    def first(self):
        return [self.copy(0, self.me, self.sibling, src=self.x_ref),
                self.copy(1, self.me, self.xn, src=self.x_ref),
                self.copy(2, self.me, self.yn, src=self.x_ref)]

    def second(self):
        return [self.copy(3, self.xn, self.yn, part=0), self.copy(7, self.yn, self.xn, part=1),
                self.copy(4, self.xn, self.sibling), self.copy(5, self.yn, self.sibling)]

    def third(self):
        return [self.copy(6, self.diag, self.sibling)]

    def start(self):
        self.mine.start()
        for cp in self.first():
            cp.start()

    def pass_on(self):
        fwd_x, fwd_y, sib_x, sib_y = self.second()
        self.copy(1, self.xn, self.me).wait_recv()
        fwd_x.start()
        sib_x.start()
        self.copy(2, self.yn, self.me).wait_recv()
        fwd_y.start()
        sib_y.start()

    def pass_on_diagonal(self):
        self.copy(3, self.diag, self.me, part=0).wait_recv()
        self.copy(7, self.diag, self.me, part=1).wait_recv()
        self.third()[0].start()

    def finish(self):
        sib = (self.sibling[0], self.sibling[1], self.sibling[2])
        self.copy(0, sib, self.me).wait_recv()
        for k, chip in ((4, self.xn), (5, self.yn), (6, self.diag)):
            self.copy(k, (chip[0], chip[1], sib[2]), self.me).wait_recv()
        for cp in self.first() + self.second() + self.third():
            cp.wait_send()
        self.mine.wait()


GATHER_SEMS = [pltpu.SemaphoreType.DMA((8,)), pltpu.SemaphoreType.DMA((8,)), pltpu.SemaphoreType.DMA]


def _all_gather_rows(shard):
    m_per, n = shard.shape

    def body(x_ref, out_ref, send_sems, recv_sems, local_sem):
        ag = _TwoLevelGather(x_ref, out_ref, m_per, send_sems, recv_sems, local_sem)
        ag.start()
        ag.pass_on()
        ag.pass_on_diagonal()
        ag.finish()

    return pl.pallas_call(
        body,
        name="weights_all_gather",
        out_shape=pltpu.HBM((N_DEV * m_per, n), shard.dtype),
        in_specs=[pl.BlockSpec(memory_space=pltpu.HBM)],
        out_specs=pl.BlockSpec(memory_space=pltpu.HBM),
        scratch_shapes=list(GATHER_SEMS),
    )(pltpu.with_memory_space_constraint(shard, pltpu.HBM))


PROJ_PARTS = D_MODEL // PLE_DIM
N_WEIGHT_COPIES = N_DEV * (3 + PROJ_PARTS)


def _weight_copies(wg_ref, w_in_t, w_out, w_gate, w_proj_t, sems):
    copies = []
    for s in range(N_DEV):
        base = s * ROWS_LAYER
        for dst, off, rows in ((w_in_t, OFF_IN, ROWS_IN), (w_out, OFF_OUT, ROWS_OUT), (w_gate, OFF_GATE, ROWS_GATE)):
            copies.append((wg_ref.at[pl.ds(base + off, rows), :], dst.at[pl.ds(s * rows, rows), :]))
        for j in range(PROJ_PARTS):
            copies.append((
                wg_ref.at[pl.ds(base + OFF_PROJ, ROWS_PROJ), pl.ds(j * PLE_DIM, PLE_DIM)],
                w_proj_t.at[pl.ds(s * ROWS_OUT + j * ROWS_PROJ, ROWS_PROJ), :],
            ))
    return [pltpu.make_async_copy(src, dst, sems.at[k]) for k, (src, dst) in enumerate(copies)]


def _forward_layer(layer, x, p_all, wg, conv_k, norm_g, ln_g, ln_b, w_mix, b_mix, ple_g, next_shard=None):
    t = x.shape[0]
    tm = _tile(t, 512)
    nt = t // tm
    gathers = next_shard is not None

    def body(*refs):
        (x_ref, p_ref, wg_ref, cw_ref, ng_ref, lng_ref, lnb_ref, wm_ref, bm_ref, pg_ref) = refs[:10]
        refs = refs[10:]
        if gathers:
            shard_ref, refs = refs[0], refs[1:]
        (proj_ref, hn_ref, cat_ref, r_ref, gpre_ref, x1_ref, x2_ref) = refs[:7]
        refs = refs[7:]
        if gathers:
            gathered_ref, refs = refs[0], refs[1:]
        (w_in_t, w_out, w_gate, wpt_ref, vln_s, mixed_s, halo_s, sems) = refs[:8]
        i = pl.program_id(0)
        if gathers:
            ag = _TwoLevelGather(shard_ref, gathered_ref, ROWS_LAYER, *refs[8:11])

            @pl.when(i == 0)
            def _():
                ag.start()

            @pl.when(i == (5 * nt) // 16)
            def _():
                ag.pass_on()

            @pl.when(i == nt // 2)
            def _():
                ag.pass_on_diagonal()

        @pl.when(i == 0)
        def _():
            copies = _weight_copies(wg_ref, w_in_t, w_out, w_gate, wpt_ref, sems)
            for cp in copies:
                cp.start()
            halo_s[...] = jnp.zeros_like(halo_s)
            for cp in copies:
                cp.wait()

        xv = x_ref[...]
        rstd0 = lax.rsqrt(jnp.mean(xv * xv, axis=-1, keepdims=True) + EPS)
        hn_ref[...] = (xv * rstd0 * ng_ref[...]).astype(BF16)

        def proj_section(k):
            sec = _dot_nt(hn_ref[...], w_in_t[pl.ds(k * 512, 512), :])
            proj_ref[:, k * 512:(k + 1) * 512] = sec.astype(BF16)
            return sec

        v = proj_section(1)
        mu = jnp.mean(v, axis=-1, keepdims=True)
        vc = v - mu
        var = jnp.mean(vc * vc, axis=-1, keepdims=True)
        vln = vc * lax.rsqrt(var + EPS) * lng_ref[...] + lnb_ref[...]
        vln_s[...] = vln.astype(BF16)
        for ci in range(tm // CHUNK):
            rows = pl.ds(ci * CHUNK, CHUNK)
            for h in range(HEADS_A):
                cols = pl.ds(h * HEAD_DIM, HEAD_DIM)
                mixed_s[rows, cols] = _dot(wm_ref[h], vln_s[rows, cols]) + bm_ref[h]
        u = proj_section(0)
        za = proj_section(2)
        out_a = u * mixed_s[...] * (za * _sigmoid(za))
        cat_ref[:, 0:512] = out_a.astype(BF16)

        xc = proj_section(5) * proj_section(3)
        prev = halo_s[...]
        row = lax.broadcasted_iota(jnp.int32, (tm, WIDTH_B), 0)
        xc_m1 = jnp.where(row == 0, prev[7:8, :], pltpu.roll(xc, 1, 0))
        xc_m2 = jnp.where(row == 0, prev[6:7, :], jnp.where(row == 1, prev[7:8, :], pltpu.roll(xc, 2, 0)))
        halo_s[...] = xc[tm - 8:tm, :]
        cw = cw_ref[...]
        yc = cw[0:1, :] * xc_m2 + cw[1:2, :] * xc_m1 + cw[2:3, :] * xc
        zb = proj_section(6)
        out_b = proj_section(4) * yc * (zb * _sigmoid(zb))
        cat_ref[:, 512:1024] = out_b.astype(BF16)

        x1 = xv + _dot(cat_ref[...], w_out[...])
        x1_ref[...] = x1
        rstd1 = lax.rsqrt(jnp.mean(x1 * x1, axis=-1, keepdims=True) + EPS)
        r_ref[...] = (x1 * rstd1 * pg_ref[...]).astype(BF16)
        gpre = _dot(r_ref[...], w_gate[...])
        gpre_ref[...] = gpre.astype(BF16)
        pp = _dot_nt(p_ref[...].astype(BF16), wpt_ref[...])
        x2_ref[...] = x1 + _sigmoid(gpre) * pp

        if gathers:
            @pl.when(i == nt - 1)
            def _():
                ag.finish()

    def tok(width):
        return pl.BlockSpec((tm, width), lambda i: (i, 0))

    def whole(shape):
        return pl.BlockSpec(shape, lambda i: (0,) * len(shape))

    hbm = pl.BlockSpec(memory_space=pl.ANY)
    operands = [x, p_all, wg, conv_k, norm_g, ln_g, ln_b, w_mix, b_mix, ple_g]
    in_specs = [
        tok(D_MODEL), pl.BlockSpec((None, None, tm, PLE_DIM), lambda i: (layer, 0, i, 0)), hbm,
        whole((8, WIDTH_B)), whole((1, D_MODEL)), whole((1, WIDTH_A)), whole((1, WIDTH_A)),
        whole((HEADS_A, CHUNK, CHUNK)), whole((HEADS_A, CHUNK, HEAD_DIM)), whole((1, D_MODEL)),
    ]
    out_specs = [tok(PROJ_WIDTH), tok(D_MODEL), tok(D_MODEL), tok(D_MODEL), tok(D_MODEL), tok(D_MODEL), tok(D_MODEL)]
    out_shape = [
        jax.ShapeDtypeStruct((t, PROJ_WIDTH), BF16),
        jax.ShapeDtypeStruct((t, D_MODEL), BF16),
        jax.ShapeDtypeStruct((t, D_MODEL), BF16),
        jax.ShapeDtypeStruct((t, D_MODEL), BF16),
        jax.ShapeDtypeStruct((t, D_MODEL), BF16),
        jax.ShapeDtypeStruct((t, D_MODEL), F32),
        jax.ShapeDtypeStruct((t, D_MODEL), F32),
    ]
    scratch_shapes = [
        pltpu.VMEM((PROJ_WIDTH, D_MODEL), BF16),
        pltpu.VMEM((D_MODEL, D_MODEL), BF16),
        pltpu.VMEM((D_MODEL, D_MODEL), BF16),
        pltpu.VMEM((D_MODEL, PLE_DIM), BF16),
        pltpu.VMEM((tm, WIDTH_A), BF16),
        pltpu.VMEM((tm, WIDTH_A), F32),
        pltpu.VMEM((8, WIDTH_B), F32),
        pltpu.SemaphoreType.DMA((N_WEIGHT_COPIES,)),
    ]
    if gathers:
        operands.append(pltpu.with_memory_space_constraint(next_shard, pltpu.HBM))
        in_specs.append(pl.BlockSpec(memory_space=pltpu.HBM))
        out_specs.append(pl.BlockSpec(memory_space=pltpu.HBM))
        out_shape.append(pltpu.HBM((N_DEV * ROWS_LAYER, D_MODEL), BF16))
        scratch_shapes += list(GATHER_SEMS)

    return pl.pallas_call(
        body,
        name=f"layer{layer}_forward",
        grid=(nt,),
        in_specs=in_specs,
        out_specs=out_specs,
        out_shape=out_shape,
        scratch_shapes=scratch_shapes,
        compiler_params=pltpu.CompilerParams(dimension_semantics=("arbitrary",), vmem_limit_bytes=56 * MIB),
    )(*operands)


class _DirectScatter:
    def __init__(self, pack_ref, pieces_ref, send_sems, recv_sems, local_sem):
        x, y, c = lax.axis_index("x"), lax.axis_index("y"), lax.axis_index("c")
        me = 4 * x + 2 * y + c
        self.copies = []
        for k in range(N_DEV - 1):
            fx, fy, fc = ((k + 1) >> 2) & 1, ((k + 1) >> 1) & 1, (k + 1) & 1
            tx, ty, tc = x ^ fx, y ^ fy, c ^ fc
            self.copies.append(
                pltpu.make_async_remote_copy(
                    src_ref=pack_ref.at[4 * tx + 2 * ty + tc], dst_ref=pieces_ref.at[me],
                    send_sem=send_sems.at[k], recv_sem=recv_sems.at[k],
                    device_id=(tx, ty, tc), device_id_type=MESH,
                )
            )
        self.mine = pltpu.make_async_copy(pack_ref.at[me], pieces_ref.at[me], local_sem)

    def start(self):
        self.mine.start()
        for cp in self.copies:
            cp.start()

    def finish(self):
        for cp in self.copies:
            cp.wait_recv()
        for cp in self.copies:
            cp.wait_send()
        self.mine.wait()


SCATTER_SEMS = [pltpu.SemaphoreType.DMA((N_DEV - 1,)), pltpu.SemaphoreType.DMA((N_DEV - 1,)), pltpu.SemaphoreType.DMA]


def _backward_layer(layer, dx2, x_in, x1, proj, gpre, p_all, wg, conv_k, norm_g, ln_g, ln_b,
                    w_mix, w_mix_t, b_mix, ple_g, loss_head=None):
    t = x_in.shape[0]
    tm = _tile(t, 256)
    nt = t // tm
    n_chunks = tm // CHUNK
    halo_rows = 16
    heads = loss_head is not None

    def body(*refs):
        (dx2_ref, xin_ref, x1_ref, proj_ref, halo_ref, gpre_ref, p_ref, wg_ref, cw_ref,
         ng_ref, lng_ref, lnb_ref, wm_ref, wmt_ref, bm_ref, pg_ref) = refs[:16]
        refs = refs[16:]
        if heads:
            tgt_ref, fg_ref = refs[:2]
            refs = refs[2:]
        (dxin_ref, dproj_ref, dx1_ref, dgpre_ref, dpp_ref, small_ref, dws_ref) = refs[:7]
        refs = refs[7:]
        if heads:
            head_ref, refs = refs[0], refs[1:]
        (w_in_t, w_out, w_gate, wpt_ref, vln_s, mixed_s, dmix_s, dvln_s, carry_s,
         ng_acc, pg_acc, lng_acc, lnb_acc, cw_acc, dbm_ref, sems) = refs[:16]
        if heads:
            loss_acc, fg_acc = refs[16:18]
        i = pl.program_id(0)
        tile = nt - 1 - i

        @pl.when(i == 0)
        def _():
            copies = _weight_copies(wg_ref, w_in_t, w_out, w_gate, wpt_ref, sems)
            for cp in copies:
                cp.start()
            if heads:
                loss_acc[...] = jnp.zeros_like(loss_acc)
                fg_acc[...] = jnp.zeros_like(fg_acc)
            carry_s[...] = jnp.zeros_like(carry_s)
            ng_acc[...] = jnp.zeros_like(ng_acc)
            pg_acc[...] = jnp.zeros_like(pg_acc)
            lng_acc[...] = jnp.zeros_like(lng_acc)
            lnb_acc[...] = jnp.zeros_like(lnb_acc)
            cw_acc[...] = jnp.zeros_like(cw_acc)
            dws_ref[...] = jnp.zeros_like(dws_ref)
            dbm_ref[...] = jnp.zeros_like(dbm_ref)
            for cp in copies:
                cp.wait()

        if heads:
            x2v = dx2_ref[...]
            fg = fg_ref[...]
            rstdf = lax.rsqrt(jnp.mean(x2v * x2v, axis=-1, keepdims=True) + EPS)
            xhatf = x2v * rstdf
            err = xhatf * fg - tgt_ref[...]
            loss_acc[...] += _colsum8(err * err)
            dy = err * (1.0 / D_MODEL)
            fg_acc[...] += _colsum8(dy * xhatf)
            dxhf = dy * fg
            dx2v = rstdf * (dxhf - xhatf * jnp.mean(dxhf * xhatf, axis=-1, keepdims=True))
        else:
            dx2v = dx2_ref[...]

        gate = _sigmoid(gpre_ref[...].astype(F32))
        pp = _dot_nt(p_ref[...].astype(BF16), wpt_ref[...])
        dpp = dx2v * gate
        dpp_ref[...] = dpp.astype(BF16)
        dgpre = (dpp * pp * (1.0 - gate)).astype(BF16)
        dgpre_ref[...] = dgpre
        dr = _dot_nt(dgpre, w_gate[...])
        x1v = x1_ref[...]
        rstd1 = lax.rsqrt(jnp.mean(x1v * x1v, axis=-1, keepdims=True) + EPS)
        xhat1 = x1v * rstd1
        pg_acc[...] += _colsum8(dr * xhat1)
        dxh = dr * pg_ref[...]
        dx1 = dx2v + rstd1 * (dxh - xhat1 * jnp.mean(dxh * xhat1, axis=-1, keepdims=True))
        dx1b = dx1.astype(BF16)
        dx1_ref[...] = dx1b

        dcat = _dot_nt(dx1b, w_out[...])
        dca = dcat[:, 0:512]
        dcb = dcat[:, 512:1024]

        u = proj_ref[:, 0:512]
        v = proj_ref[:, 512:1024].astype(F32)
        za = proj_ref[:, 1024:1536]
        mu = jnp.mean(v, axis=-1, keepdims=True)
        vc = v - mu
        var = jnp.mean(vc * vc, axis=-1, keepdims=True)
        rs = lax.rsqrt(var + EPS)
        vhat = vc * rs
        lng = lng_ref[...]
        vln_s[...] = (vhat * lng + lnb_ref[...]).astype(BF16)
        for ci in range(n_chunks):
            rows = pl.ds(ci * CHUNK, CHUNK)
            for h in range(HEADS_A):
                cols = pl.ds(h * HEAD_DIM, HEAD_DIM)
                mixed_s[rows, cols] = (_dot(wm_ref[h], vln_s[rows, cols]) + bm_ref[h]).astype(BF16)
        mixed = mixed_s[...]
        sga = _sigmoid(za)
        sa = za * sga
        dsa = sga + sa * (1.0 - sga)

        def put_section(k, val):
            dproj_ref[:, k * 512:(k + 1) * 512] = val.astype(BF16)

        dcab = dca.astype(BF16)
        dca_sa = dcab * sa
        put_section(0, dca_sa * mixed)
        dmix_s[...] = dca_sa * u
        put_section(2, (dcab * dsa) * (u * mixed))
        dbm_acc = jnp.zeros((CHUNK, WIDTH_A), F32)
        for ci in range(n_chunks):
            rows = pl.ds(ci * CHUNK, CHUNK)
            dbm_acc = dbm_acc + dmix_s[rows, :].astype(F32)
            for h in range(HEADS_A):
                cols = pl.ds(h * HEAD_DIM, HEAD_DIM)
                dvln_s[rows, cols] = _dot(wmt_ref[h], dmix_s[rows, cols])
                dws_ref[:, cols] += _dot_nt(dmix_s[rows, cols], vln_s[rows, cols])
        dbm_ref[...] += dbm_acc
        dvln = dvln_s[...]
        lng_acc[...] += _colsum8(dvln * vhat)
        lnb_acc[...] += _colsum8(dvln)
        dvh = dvln * lng
        dv = rs * (dvh - jnp.mean(dvh, axis=-1, keepdims=True) - vhat * jnp.mean(dvh * vhat, axis=-1, keepdims=True))
        put_section(1, dv)

        hb = proj_ref[:, 1536:2048].astype(F32)
        gb = proj_ref[:, 2048:2560]
        gc = proj_ref[:, 2560:3072].astype(F32)
        zb = proj_ref[:, 3072:3584]
        xc = gc * hb
        prev = halo_ref[:, 2560:3072].astype(F32) * halo_ref[:, 1536:2048].astype(F32)
        prev = jnp.where(tile > 0, prev, 0.0)
        row = lax.broadcasted_iota(jnp.int32, (tm, WIDTH_B), 0)
        p1 = prev[halo_rows - 1:halo_rows, :]
        p2 = prev[halo_rows - 2:halo_rows - 1, :]
        xc_m1 = jnp.where(row == 0, p1, pltpu.roll(xc, 1, 0))
        xc_m2 = jnp.where(row == 0, p2, jnp.where(row == 1, p1, pltpu.roll(xc, 2, 0)))
        cw = cw_ref[...]
        yc = cw[0:1, :] * xc_m2 + cw[1:2, :] * xc_m1 + cw[2:3, :] * xc
        sgb = _sigmoid(zb)
        sb = zb * sgb
        dsb = sgb + sb * (1.0 - sgb)
        dcbb = dcb.astype(BF16)
        ycb = yc.astype(BF16)
        dcb_sb = dcbb * sb
        put_section(4, dcb_sb * ycb)
        dyc = (dcb_sb * gb).astype(F32)
        put_section(6, (dcbb * dsb) * (gb * ycb))
        nxt = carry_s[...]
        dyc_p1 = jnp.where(row == tm - 1, nxt[0:1, :], pltpu.roll(dyc, tm - 1, 0))
        dyc_p2 = jnp.where(row == tm - 1, nxt[1:2, :], jnp.where(row == tm - 2, nxt[0:1, :], pltpu.roll(dyc, tm - 2, 0)))
        carry_s[...] = dyc[0:8, :]
        dxc = cw[2:3, :] * dyc + cw[1:2, :] * dyc_p1 + cw[0:1, :] * dyc_p2
        cw_acc[0] += _colsum8(dyc * xc_m2)
        cw_acc[1] += _colsum8(dyc * xc_m1)
        cw_acc[2] += _colsum8(dyc * xc)
        put_section(3, dxc * gc)
        put_section(5, dxc * hb)

        dhn = _dot(dproj_ref[...], w_in_t[...])
        xv = xin_ref[...]
        rstd0 = lax.rsqrt(jnp.mean(xv * xv, axis=-1, keepdims=True) + EPS)
        xhat0 = xv * rstd0
        ng_acc[...] += _colsum8(dhn * xhat0)
        dxh0 = dhn * ng_ref[...]
        dxin_ref[...] = dx1 + rstd0 * (dxh0 - xhat0 * jnp.mean(dxh0 * xhat0, axis=-1, keepdims=True))

        @pl.when(i == nt - 1)
        def _():
            small_ref[...] = jnp.zeros_like(small_ref)
            small_ref[SMALL_NORM:SMALL_NORM + 1, :] = jnp.sum(ng_acc[...], axis=0, keepdims=True)
            small_ref[SMALL_PLE:SMALL_PLE + 1, :] = jnp.sum(pg_acc[...], axis=0, keepdims=True)
            small_ref[SMALL_LN:SMALL_LN + 1, 0:WIDTH_A] = jnp.sum(lng_acc[...], axis=0, keepdims=True)
            small_ref[SMALL_LN:SMALL_LN + 1, WIDTH_A:2 * WIDTH_A] = jnp.sum(lnb_acc[...], axis=0, keepdims=True)
            for h in range(HEADS_A):
                cols = pl.ds(h * HEAD_DIM, HEAD_DIM)
                small_ref[SMALL_BS:SMALL_BS + 1, cols] = jnp.sum(jnp.transpose(dbm_ref[:, cols]), axis=0, keepdims=True)
            for k in range(3):
                small_ref[SMALL_CONV + k:SMALL_CONV + k + 1, 0:WIDTH_B] = jnp.sum(cw_acc[k], axis=0, keepdims=True)
            if heads:
                total = jnp.sum(loss_acc[...]) * (0.5 / D_MODEL)
                rows8 = lax.broadcasted_iota(jnp.int32, (SMALL_ROWS, D_MODEL), 0)
                lanes8 = lax.broadcasted_iota(jnp.int32, (SMALL_ROWS, D_MODEL), 1)
                head_ref[...] = jnp.where((rows8 == HEAD_LOSS) & (lanes8 == 0), total, 0.0)
                head_ref[HEAD_FINAL:HEAD_FINAL + 1, :] = jnp.sum(fg_acc[...], axis=0, keepdims=True)

    def tok(width):
        return pl.BlockSpec((tm, width), lambda i: (nt - 1 - i, 0))

    def whole(shape):
        return pl.BlockSpec(shape, lambda i: (0,) * len(shape))

    halo_spec = pl.BlockSpec(
        (halo_rows, PROJ_WIDTH), lambda i: (jnp.maximum((nt - 1 - i) * (tm // halo_rows) - 1, 0), 0)
    )
    hbm = pl.BlockSpec(memory_space=pl.ANY)
    operands = [dx2, x_in, x1, proj, proj, gpre, p_all, wg, conv_k, norm_g, ln_g, ln_b, w_mix, w_mix_t, b_mix, ple_g]
    in_specs = [
        tok(D_MODEL), tok(D_MODEL), tok(D_MODEL), tok(PROJ_WIDTH), halo_spec, tok(D_MODEL),
        pl.BlockSpec((None, None, tm, PLE_DIM), lambda i: (layer, 0, nt - 1 - i, 0)), hbm,
        whole((8, WIDTH_B)), whole((1, D_MODEL)), whole((1, WIDTH_A)), whole((1, WIDTH_A)),
        whole((HEADS_A, CHUNK, CHUNK)), whole((HEADS_A, CHUNK, CHUNK)), whole((HEADS_A, CHUNK, HEAD_DIM)),
        whole((1, D_MODEL)),
    ]
    out_specs = [
        tok(D_MODEL), tok(PROJ_WIDTH), tok(D_MODEL), tok(D_MODEL), tok(D_MODEL),
        whole((SMALL_ROWS, D_MODEL)), whole((CHUNK, WIDTH_A)),
    ]
    out_shape = [
        jax.ShapeDtypeStruct((t, D_MODEL), F32),
        jax.ShapeDtypeStruct((t, PROJ_WIDTH), BF16),
        jax.ShapeDtypeStruct((t, D_MODEL), BF16),
        jax.ShapeDtypeStruct((t, D_MODEL), BF16),
        jax.ShapeDtypeStruct((t, D_MODEL), BF16),
        jax.ShapeDtypeStruct((SMALL_ROWS, D_MODEL), F32),
        jax.ShapeDtypeStruct((CHUNK, WIDTH_A), F32),
    ]
    scratch_shapes = [
        pltpu.VMEM((PROJ_WIDTH, D_MODEL), BF16),
        pltpu.VMEM((D_MODEL, D_MODEL), BF16),
        pltpu.VMEM((D_MODEL, D_MODEL), BF16),
        pltpu.VMEM((D_MODEL, PLE_DIM), BF16),
        pltpu.VMEM((tm, WIDTH_A), BF16),
        pltpu.VMEM((tm, WIDTH_A), BF16),
        pltpu.VMEM((tm, WIDTH_A), BF16),
        pltpu.VMEM((tm, WIDTH_A), F32),
        pltpu.VMEM((8, WIDTH_B), F32),
        pltpu.VMEM((8, D_MODEL), F32),
        pltpu.VMEM((8, D_MODEL), F32),
        pltpu.VMEM((8, WIDTH_A), F32),
        pltpu.VMEM((8, WIDTH_A), F32),
        pltpu.VMEM((3, 8, WIDTH_B), F32),
        pltpu.VMEM((CHUNK, WIDTH_A), F32),
        pltpu.SemaphoreType.DMA((N_WEIGHT_COPIES,)),
    ]
    if heads:
        operands += list(loss_head)
        in_specs += [tok(D_MODEL), whole((1, D_MODEL))]
        out_specs.append(whole((SMALL_ROWS, D_MODEL)))
        out_shape.append(jax.ShapeDtypeStruct((SMALL_ROWS, D_MODEL), F32))
        scratch_shapes += [pltpu.VMEM((8, D_MODEL), F32), pltpu.VMEM((8, D_MODEL), F32)]

    return pl.pallas_call(
        body,
        name=f"layer{layer}_backward",
        grid=(nt,),
        in_specs=in_specs,
        out_specs=out_specs,
        out_shape=out_shape,
        scratch_shapes=scratch_shapes,
        compiler_params=pltpu.CompilerParams(dimension_semantics=("arbitrary",), vmem_limit_bytes=56 * MIB),
    )(*operands)


def _sum_pieces(layer, pieces):
    rows, n = pieces.shape[1], pieces.shape[2]
    blocks = 2
    rb = rows // blocks

    def body(p_ref, out_ref):
        total = p_ref[0].astype(F32)
        for j in range(1, N_DEV):
            total = total + p_ref[j].astype(F32)
        out_ref[...] = total

    return pl.pallas_call(
        body,
        name=f"layer{layer}_grad_sum",
        grid=(blocks,),
        out_shape=pltpu.HBM((rows, n), F32),
        in_specs=[pl.BlockSpec((N_DEV, rb, n), lambda i: (0, i, 0))],
        out_specs=pl.BlockSpec((rb, n), lambda i: (i, 0)),
        compiler_params=pltpu.CompilerParams(dimension_semantics=("arbitrary",), vmem_limit_bytes=32 * MIB),
    )(pieces)


def _weight_grads(layer, dproj, hn, cat, dx1, r, dgpre, dpp, p_all, scatter_pack=None):
    t = hn.shape[0]
    tk = _tile(t, 512)
    nt = t // tk
    in_blocks = PROJ_WIDTH // 512
    scatters = scatter_pack is not None

    def body(*refs):
        (dproj_ref, hn_ref, cat_ref, dx1_ref, r_ref, dgpre_ref, dpp_ref, p_ref) = refs[:8]
        refs = refs[8:]
        if scatters:
            prior_ref, refs = refs[0], refs[1:]
        pack_ref, refs = refs[0], refs[1:]
        if scatters:
            pieces_ref, refs = refs[0], refs[1:]
        (acc_in, acc_out, acc_gate, acc_proj, stage, sems) = refs[:6]
        i = pl.program_id(0)
        if scatters:
            scatter = _DirectScatter(prior_ref, pieces_ref, *refs[6:9])

            @pl.when(i == 0)
            def _():
                scatter.start()

        @pl.when(i == 0)
        def _():
            acc_in[...] = jnp.zeros_like(acc_in)
            acc_out[...] = jnp.zeros_like(acc_out)
            acc_gate[...] = jnp.zeros_like(acc_gate)
            acc_proj[...] = jnp.zeros_like(acc_proj)

        hnv = hn_ref[...]
        for b in range(in_blocks):
            acc_in[pl.ds(b * 512, 512), :] += _dot_tn(dproj_ref[:, b * 512:(b + 1) * 512], hnv)
        dx1v = dx1_ref[...]
        dgv = dgpre_ref[...]
        for b in range(D_MODEL // 512):
            acc_out[pl.ds(b * 512, 512), :] += _dot_tn(cat_ref[:, b * 512:(b + 1) * 512], dx1v)
            acc_gate[pl.ds(b * 512, 512), :] += _dot_tn(r_ref[:, b * 512:(b + 1) * 512], dgv)
        pv = p_ref[...].astype(BF16)
        for b in range(D_MODEL // 512):
            acc_proj[pl.ds(b * 512, 512), :] += _dot_tn(dpp_ref[:, b * 512:(b + 1) * 512], pv)

        @pl.when(i == nt - 1)
        def _():
            def out_copy(s):
                return pltpu.make_async_copy(stage.at[s % 2], pack_ref.at[s], sems.at[s % 2])

            for s in range(N_DEV):
                if s >= 2:
                    out_copy(s - 2).wait()
                buf = stage.at[s % 2]
                buf[pl.ds(OFF_IN, ROWS_IN), :] = acc_in[pl.ds(s * ROWS_IN, ROWS_IN), :].astype(BF16)
                buf[pl.ds(OFF_OUT, ROWS_OUT), :] = acc_out[pl.ds(s * ROWS_OUT, ROWS_OUT), :].astype(BF16)
                buf[pl.ds(OFF_GATE, ROWS_GATE), :] = acc_gate[pl.ds(s * ROWS_GATE, ROWS_GATE), :].astype(BF16)
                for j in range(D_MODEL // PLE_DIM):
                    buf[pl.ds(OFF_PROJ, ROWS_PROJ), pl.ds(j * PLE_DIM, PLE_DIM)] = acc_proj[
                        pl.ds(s * ROWS_OUT + j * ROWS_PROJ, ROWS_PROJ), :
                    ].astype(BF16)
                out_copy(s).start()
            out_copy(N_DEV - 2).wait()
            out_copy(N_DEV - 1).wait()
            if scatters:
                scatter.finish()

    def tok(width):
        return pl.BlockSpec((tk, width), lambda i: (i, 0))

    hbm = pl.BlockSpec(memory_space=pl.ANY)
    pack_shape = jax.ShapeDtypeStruct((N_DEV, ROWS_GRAD, D_MODEL), BF16)
    operands = [dproj, hn, cat, dx1, r, dgpre, dpp, p_all]
    in_specs = [tok(PROJ_WIDTH), tok(D_MODEL), tok(D_MODEL), tok(D_MODEL), tok(D_MODEL), tok(D_MODEL), tok(D_MODEL),
                pl.BlockSpec((None, None, tk, PLE_DIM), lambda i: (layer, 0, i, 0))]
    out_specs, out_shape = [hbm], [pack_shape]
    scratch_shapes = [
        pltpu.VMEM((PROJ_WIDTH, D_MODEL), F32),
        pltpu.VMEM((D_MODEL, D_MODEL), F32),
        pltpu.VMEM((D_MODEL, D_MODEL), F32),
        pltpu.VMEM((D_MODEL, PLE_DIM), F32),
        pltpu.VMEM((2, ROWS_GRAD, D_MODEL), BF16),
        pltpu.SemaphoreType.DMA((2,)),
    ]
    if scatters:
        operands.append(scatter_pack)
        in_specs.append(hbm)
        out_specs.append(hbm)
        out_shape.append(pack_shape)
        scratch_shapes += list(SCATTER_SEMS)

    return pl.pallas_call(
        body,
        name=f"layer{layer}_weight_grads",
        grid=(nt,),
        in_specs=in_specs,
        out_specs=out_specs,
        out_shape=out_shape,
        scratch_shapes=scratch_shapes,
        compiler_params=pltpu.CompilerParams(dimension_semantics=("arbitrary",), vmem_limit_bytes=58 * MIB),
    )(*operands)


def _reduce_scatter_all_reduce(layer, pack, smalls, head, dws):
    rows, n = pack.shape[1], pack.shape[2]
    assert DEPTH * WIDTH_A == D_MODEL and n == D_MODEL

    def body(g_ref, *refs):
        small_refs, refs = refs[:DEPTH], refs[DEPTH:]
        head_ref, refs = refs[0], refs[1:]
        dws_refs, refs = refs[:DEPTH], refs[DEPTH:]
        (out_ref, total_ref, r1, a_s, r2, via, sp, sr1, sq, send1, recv1, send2, recv2, ssend, srecv) = refs
        x, y, c = lax.axis_index("x"), lax.axis_index("y"), lax.axis_index("c")
        sibling = (x, y, 1 - c)
        chip = 2 * x + y
        flips = [(1, 0), (0, 1), (1, 1)]

        for l in range(DEPTH):
            sp[l * SMALL_ROWS:(l + 1) * SMALL_ROWS, :] = small_refs[l][...]
            sp[TOTAL_WS:TOTAL_ROWS, l * WIDTH_A:(l + 1) * WIDTH_A] = dws_refs[l][...]
        sp[TOTAL_HEAD:TOTAL_WS, :] = head_ref[...]

        small_pair = pltpu.make_async_remote_copy(
            src_ref=sp, dst_ref=sr1, send_sem=ssend.at[0], recv_sem=srecv.at[0], device_id=sibling, device_id_type=MESH
        )

        def to_sibling(j):
            return pltpu.make_async_remote_copy(
                src_ref=g_ref.at[2 * j + 1 - c], dst_ref=r1.at[j], send_sem=send1.at[j], recv_sem=recv1.at[j],
                device_id=sibling, device_id_type=MESH,
            )

        first = [to_sibling(j) for j in range(4)]
        small_pair.start()
        for cp in first:
            cp.start()

        small_pair.wait_recv()
        sq[chip] = sp[...] + sr1[...]
        small_chips = [
            pltpu.make_async_remote_copy(
                src_ref=sq.at[chip], dst_ref=sq.at[chip], send_sem=ssend.at[1 + k], recv_sem=srecv.at[1 + k],
                device_id=(x ^ fx, y ^ fy, c), device_id_type=MESH,
            )
            for k, (fx, fy) in enumerate(flips)
        ]
        for cp in small_chips:
            cp.start()

        half = rows // 2
        lo, hi = pl.ds(0, half), pl.ds(half, rows - half)
        x_nbr, y_nbr = (1 - x, y, c), (x, 1 - y, c)
        chip_x, chip_y, chip_d = 2 * (1 - x) + y, 2 * x + (1 - y), 2 * (1 - x) + (1 - y)

        def ici(k, src, dst, to):
            return pltpu.make_async_remote_copy(
                src_ref=src, dst_ref=dst, send_sem=send2.at[k], recv_sem=recv2.at[k], device_id=to, device_id_type=MESH)

        second = [
            ici(0, a_s.at[chip_d, lo, :], via.at[0], x_nbr),
            ici(1, a_s.at[chip_d, hi, :], via.at[1], y_nbr),
            ici(2, a_s.at[chip_x, lo, :], r2.at[0, lo, :], x_nbr),
            ici(3, a_s.at[chip_y, hi, :], r2.at[1, hi, :], y_nbr),
            ici(4, a_s.at[chip_x, hi, :], r2.at[0, hi, :], x_nbr),
            ici(5, a_s.at[chip_y, lo, :], r2.at[1, lo, :], y_nbr),
        ]
        for j in range(4):
            first[j].wait_recv()

            @pl.when(chip != j)
            def _():
                a_s[j] = (g_ref[2 * j + c].astype(F32) + r1[j].astype(F32)).astype(BF16)

            @pl.when(chip_d == j)
            def _():
                second[0].start()
                second[1].start()

            @pl.when(chip_x == j)
            def _():
                second[2].start()

            @pl.when(chip_y == j)
            def _():
                second[3].start()

        out_ref[...] = g_ref[2 * chip + c].astype(F32) + r1[chip].astype(F32)
        for cp in small_chips:
            cp.wait_recv()
        total_ref[...] = ((sq[0] + sq[1]) + sq[2]) + sq[3]

        second[0].wait_recv()
        a_s[chip_y, lo, :] = (a_s[chip_y, lo, :].astype(F32) + via[0].astype(F32)).astype(BF16)
        second[5].start()
        second[1].wait_recv()
        a_s[chip_x, hi, :] = (a_s[chip_x, hi, :].astype(F32) + via[1].astype(F32)).astype(BF16)
        second[4].start()

        second[2].wait_recv()
        second[4].wait_recv()
        out_ref[...] += r2[0].astype(F32)
        second[3].wait_recv()
        second[5].wait_recv()
        out_ref[...] += r2[1].astype(F32)
        small_pair.wait_send()
        for cp in first + small_chips + second:
            cp.wait_send()

    vmem = pl.BlockSpec(memory_space=pltpu.VMEM)
    return pl.pallas_call(
        body,
        name=f"layer{layer}_grad_reduce_scatter",
        out_shape=[jax.ShapeDtypeStruct((rows, n), F32), jax.ShapeDtypeStruct((TOTAL_ROWS, D_MODEL), F32)],
        in_specs=[vmem] * (2 + 2 * DEPTH),
        out_specs=[vmem, vmem],
        scratch_shapes=[
            pltpu.VMEM((4, rows, n), BF16),
            pltpu.VMEM((4, rows, n), BF16),
            pltpu.VMEM((2, rows, n), BF16),
            pltpu.VMEM((2, rows // 2, n), BF16),
            pltpu.VMEM((TOTAL_ROWS, D_MODEL), F32),
            pltpu.VMEM((TOTAL_ROWS, D_MODEL), F32),
            pltpu.VMEM((4, TOTAL_ROWS, D_MODEL), F32),
            pltpu.SemaphoreType.DMA((4,)),
            pltpu.SemaphoreType.DMA((4,)),
            pltpu.SemaphoreType.DMA((6,)),
            pltpu.SemaphoreType.DMA((6,)),
            pltpu.SemaphoreType.DMA((4,)),
            pltpu.SemaphoreType.DMA((4,)),
        ],
        compiler_params=pltpu.CompilerParams(vmem_limit_bytes=48 * MIB),
    )(pack, *smalls, head, *dws)


def _adam_step(w, g, m, v):
    m = ADAM_B1 * m + (1.0 - ADAM_B1) * g
    v = ADAM_B2 * v + (1.0 - ADAM_B2) * (g * g)
    m_hat = m / (1.0 - ADAM_B1 ** ADAM_STEP)
    v_hat = v / (1.0 - ADAM_B2 ** ADAM_STEP)
    return -ADAM_LR * (m_hat / (jnp.sqrt(v_hat) + ADAM_EPS) + ADAM_WD * w), m, v


def _adamw_rows(name, reduced, row_off, states):
    n = len(states)

    def body(*refs):
        red = refs[:DEPTH]
        ins = refs[DEPTH:DEPTH + 3 * n]
        outs = refs[DEPTH + 3 * n:]
        layer = pl.program_id(0)
        for l in range(DEPTH):
            @pl.when(layer == l)
            def _():
                for k in range(n):
                    w_ref, m_ref, v_ref = ins[3 * k:3 * k + 3]
                    g_ref, d_ref, nm_ref, nv_ref = outs[4 * k:4 * k + 4]
                    g = red[l][row_off[k]:row_off[k] + w_ref.shape[0], :]
                    d, m, v = _adam_step(w_ref[...], g, m_ref[...], v_ref[...])
                    g_ref[...] = g
                    d_ref[...] = d
                    nm_ref[...] = m
                    nv_ref[...] = v

    flat = [a for st in states for a in st]
    state_specs, out_specs, out_shape = [], [], []
    for w, _, _ in states:
        spec = pl.BlockSpec((None,) + w.shape[1:], lambda l: (l, 0, 0))
        state_specs += [spec] * 3
        out_specs += [spec] * 4
        out_shape += [jax.ShapeDtypeStruct(w.shape, F32)] * 4
    red_specs = [pl.BlockSpec(a.shape, lambda l: (0, 0)) for a in reduced]
    operands = [pltpu.with_memory_space_constraint(a, pltpu.HBM) for a in (*reduced, *flat)]
    outs = pl.pallas_call(
        body,
        name=name,
        grid=(DEPTH,),
        out_shape=[pltpu.HBM(a.shape, a.dtype) for a in out_shape],
        in_specs=red_specs + state_specs,
        out_specs=out_specs,
        compiler_params=pltpu.CompilerParams(dimension_semantics=("arbitrary",), vmem_limit_bytes=48 * MIB),
    )(*operands)
    return [tuple(outs[4 * k:4 * k + 4]) for k in range(n)]


def _adamw_small(total, g_conv, g_proj, st):
    names = ["norm_g", "ple_norm_g", "ln_v_g", "ln_v_b", "b_s", "w_s", "final_g", "conv_w", "w_ple_proj"]
    cut = names[:7]

    def body(total_ref, gconv_ref, gproj_ref, *refs):
        ins = {nm: refs[3 * k:3 * k + 3] for k, nm in enumerate(names)}
        outs, pos = {}, 3 * len(names)
        for nm in names:
            cnt = 4 if nm in cut else 3
            outs[nm] = refs[pos:pos + cnt]
            pos += cnt

        def update(nm, idx, g):
            w_ref, m_ref, v_ref = ins[nm]
            d, m, v = _adam_step(w_ref[idx], g, m_ref[idx], v_ref[idx])
            o = outs[nm]
            if nm in cut:
                o[0][idx] = g
                o = o[1:]
            o[0][idx] = d
            o[1][idx] = m
            o[2][idx] = v

        tril = (lax.broadcasted_iota(jnp.int32, (CHUNK, CHUNK), 0) >= lax.broadcasted_iota(jnp.int32, (CHUNK, CHUNK), 1))
        for l in range(DEPTH):
            base = l * SMALL_ROWS
            row = (slice(l, l + 1), slice(None))
            update("norm_g", row, total_ref[base + SMALL_NORM:base + SMALL_NORM + 1, :])
            update("ple_norm_g", row, total_ref[base + SMALL_PLE:base + SMALL_PLE + 1, :])
            update("ln_v_g", row, total_ref[base + SMALL_LN:base + SMALL_LN + 1, 0:WIDTH_A])
            update("ln_v_b", row, total_ref[base + SMALL_LN:base + SMALL_LN + 1, WIDTH_A:2 * WIDTH_A])
            for h in range(HEADS_A):
                update("b_s", (l, slice(h, h + 1), slice(None)),
                       total_ref[base + SMALL_BS:base + SMALL_BS + 1, h * HEAD_DIM:(h + 1) * HEAD_DIM])
                lanes = slice(l * WIDTH_A + h * CHUNK, l * WIDTH_A + (h + 1) * CHUNK)
                update("w_s", (l, h), jnp.where(tril, total_ref[TOTAL_WS:TOTAL_ROWS, lanes], 0.0))
        update("final_g", (slice(None), slice(None)), total_ref[TOTAL_HEAD + HEAD_FINAL:TOTAL_HEAD + HEAD_FINAL + 1, :])
        update("conv_w", (slice(None),) * 3, gconv_ref[...])
        update("w_ple_proj", (slice(None),) * 3, gproj_ref[...])

    flat = [a for nm in names for a in st[nm]]
    out_shape = []
    for nm in names:
        out_shape += [jax.ShapeDtypeStruct(st[nm][0].shape, F32)] * (4 if nm in cut else 3)
    def whole(a):
        return pl.BlockSpec(a.shape, lambda i: (0,) * len(a.shape))

    operands = [pltpu.with_memory_space_constraint(a, pltpu.HBM) for a in (total, g_conv, g_proj, *flat)]
    outs = pl.pallas_call(
        body,
        name="adamw_small",
        grid=(1,),
        out_shape=[pltpu.HBM(a.shape, a.dtype) for a in out_shape],
        in_specs=[whole(a) for a in operands],
        out_specs=[whole(a) for a in out_shape],
        compiler_params=pltpu.CompilerParams(dimension_semantics=("arbitrary",), vmem_limit_bytes=32 * MIB),
    )(*operands)
    res, pos = {}, 0
    for nm in names:
        cnt = 4 if nm in cut else 3
        got = tuple(outs[pos:pos + cnt])
        res[nm] = got if nm in cut else ((g_conv if nm == "conv_w" else g_proj),) + got
        pos += cnt
    return res


def _split3_bf16(a):
    b1 = a.astype(BF16)
    r1 = a - b1.astype(F32)
    b2 = r1.astype(BF16)
    b3 = (r1 - b2.astype(F32)).astype(BF16)
    return b1, b2, b3


def _pack_weight_shard(w_in_l, w_out_l, w_gate_l, w_proj_l, conv_w_l):
    w_in_t = jnp.transpose(w_in_l).astype(BF16)
    proj_t = jnp.transpose(w_proj_l).astype(BF16)
    proj_rows = proj_t.reshape(D_MODEL // PLE_DIM, ROWS_PROJ, PLE_DIM).transpose(1, 0, 2).reshape(ROWS_PROJ, D_MODEL)
    conv_parts = jnp.concatenate([b.reshape(-1) for b in _split3_bf16(conv_w_l)])
    conv_rows = jnp.concatenate([conv_parts, jnp.zeros((ROWS_CONV * D_MODEL - conv_parts.shape[0],), BF16)])
    return jnp.concatenate(
        [w_in_t, w_out_l.astype(BF16), w_gate_l.astype(BF16), proj_rows, conv_rows.reshape(ROWS_CONV, D_MODEL)], axis=0
    )


def _unpack_conv(wg):
    per_dev = wg.reshape(N_DEV, ROWS_LAYER, D_MODEL)
    n_conv = (WIDTH_B // N_DEV) * 3
    conv_parts = per_dev[:, OFF_CONV].astype(F32)[:, :3 * n_conv].reshape(N_DEV, 3, n_conv)
    conv = (conv_parts[:, 0] + conv_parts[:, 1]) + conv_parts[:, 2]
    conv_k = jnp.transpose(conv.reshape(WIDTH_B, 3))
    conv_k = jnp.concatenate([conv_k, jnp.zeros((5, WIDTH_B), F32)], axis=0)
    return conv_k


def _unpack_grad_proj(red):
    proj_rows = red[OFF_PROJ:OFF_PROJ + ROWS_PROJ]
    proj_t = proj_rows.reshape(ROWS_PROJ, D_MODEL // PLE_DIM, PLE_DIM).transpose(1, 0, 2).reshape(ROWS_OUT, PLE_DIM)
    return jnp.transpose(proj_t)


def kernel(x, p, norm_g, w_in, ln_v_g, ln_v_b, w_s, b_s, conv_w, w_out, ple_norm_g, w_ple_gate, w_ple_proj, final_g, loss_target, m_norm_g, m_w_in, m_ln_v_g, m_ln_v_b, m_w_s, m_b_s, m_conv_w, m_w_out, m_ple_norm_g, m_w_ple_gate, m_w_ple_proj, m_final_g, v_norm_g, v_w_in, v_ln_v_g, v_ln_v_b, v_w_s, v_b_s, v_conv_w, v_w_out, v_ple_norm_g, v_w_ple_gate, v_w_ple_proj, v_final_g):
    me = 4 * lax.axis_index("x") + 2 * lax.axis_index("y") + lax.axis_index("c")
    xs = x[0]
    target = loss_target[0]

    shards = [_pack_weight_shard(w_in[l], w_out[l], w_ple_gate[l], w_ple_proj[l], conv_w[l]) for l in range(DEPTH)]
    tril = jnp.tril(jnp.ones((CHUNK, CHUNK), F32))

    def consts(l, wg_l):
        conv_k = _unpack_conv(wg_l)
        w_mix = w_s[l] * tril[None]
        small = dict(
            conv_k=conv_k,
            norm_g=norm_g[l].reshape(1, D_MODEL), ln_g=ln_v_g[l].reshape(1, WIDTH_A), ln_b=ln_v_b[l].reshape(1, WIDTH_A),
            w_mix=w_mix.astype(BF16), w_mix_t=jnp.swapaxes(w_mix, 1, 2).astype(BF16),
            b_mix=jnp.broadcast_to(b_s[l][:, :, None], (HEADS_A, CHUNK, HEAD_DIM)),
            ple_g=ple_norm_g[l].reshape(1, D_MODEL),
        )
        return dict({k: pltpu.with_memory_space_constraint(a, pltpu.HBM) for k, a in small.items()}, wg=wg_l)

    layer_consts = [consts(0, _all_gather_rows(shards[0]))]
    saved = []
    h = xs
    for l in range(DEPTH):
        k = layer_consts[l]
        outs = _forward_layer(
            l, h, p, k["wg"], k["conv_k"], k["norm_g"], k["ln_g"], k["ln_b"], k["w_mix"], k["b_mix"],
            k["ple_g"], next_shard=shards[l + 1] if l + 1 < DEPTH else None)
        proj, hn, cat, r, gpre, x1, x2 = outs[:7]
        if l + 1 < DEPTH:
            layer_consts.append(consts(l + 1, outs[7]))
        saved.append(dict(x_in=h, proj=proj, hn=hn, cat=cat, r=r, gpre=gpre, x1=x1))
        h = x2

    smalls, dws = [None] * DEPTH, [None] * DEPTH
    reduced = [None] * DEPTH
    pending = None
    dx = h
    for l in reversed(range(DEPTH)):
        k, s = layer_consts[l], saved[l]
        outs = _backward_layer(
            l, dx, s["x_in"], s["x1"], s["proj"], s["gpre"], p, k["wg"], k["conv_k"],
            k["norm_g"], k["ln_g"], k["ln_b"], k["w_mix"], k["w_mix_t"], k["b_mix"], k["ple_g"],
            loss_head=(target, final_g.reshape(1, D_MODEL)) if l == DEPTH - 1 else None)
        dx, dproj, dx1, dgpre, dpp, smalls[l], dws[l] = outs[:7]
        if l == DEPTH - 1:
            head = outs[7]
        outs = _weight_grads(l, dproj, s["hn"], s["cat"], dx1, s["r"], dgpre, dpp, p, scatter_pack=pending)
        if pending is not None:
            reduced[l + 1] = _sum_pieces(l + 1, outs[1])
        pending = outs[0]
    reduced[0], total = _reduce_scatter_all_reduce(0, pending, smalls, head, dws)
    grad_x = dx[None]
    loss = total[TOTAL_HEAD + HEAD_LOSS, 0]

    n_ch = WIDTH_B // N_DEV
    g_conv = jnp.stack([total[l * SMALL_ROWS + SMALL_CONV:l * SMALL_ROWS + SMALL_CONV + 3, 0:WIDTH_B] for l in range(DEPTH)], axis=1)
    g_conv = lax.dynamic_slice_in_dim(g_conv, me * n_ch, n_ch, axis=2)
    g_proj = jnp.stack([_unpack_grad_proj(reduced[l]) for l in range(DEPTH)])

    def t_in(a):
        return jnp.swapaxes(a, 1, 2)

    def t_conv(a):
        return jnp.transpose(a, (2, 0, 1))

    (r_in,) = _adamw_rows("adamw_w_in", reduced, [OFF_IN], [(t_in(w_in), t_in(m_w_in), t_in(v_w_in))])
    r_out, r_gate = _adamw_rows(
        "adamw_w_out_gate", reduced, [OFF_OUT, OFF_GATE],
        [(w_out, m_w_out, v_w_out), (w_ple_gate, m_w_ple_gate, v_w_ple_gate)])
    small = _adamw_small(total, g_conv, g_proj, dict(
        norm_g=(norm_g, m_norm_g, v_norm_g), ple_norm_g=(ple_norm_g, m_ple_norm_g, v_ple_norm_g),
        ln_v_g=(ln_v_g, m_ln_v_g, v_ln_v_g), ln_v_b=(ln_v_b, m_ln_v_b, v_ln_v_b),
        b_s=(b_s, m_b_s, v_b_s), w_s=(w_s, m_w_s, v_w_s),
        final_g=tuple(a.reshape(1, D_MODEL) for a in (final_g, m_final_g, v_final_g)),
        conv_w=(t_conv(conv_w), t_conv(m_conv_w), t_conv(v_conv_w)),
        w_ple_proj=(w_ple_proj, m_w_ple_proj, v_w_ple_proj),
    ))
    res = dict(small, w_in=tuple(t_in(a) for a in r_in), w_out=r_out, w_ple_gate=r_gate)
    res["final_g"] = tuple(a.reshape(D_MODEL) for a in res["final_g"])
    res["conv_w"] = tuple(jnp.transpose(a, (1, 2, 0)) for a in res["conv_w"])
    order = ["norm_g", "w_in", "ln_v_g", "ln_v_b", "w_s", "b_s", "conv_w", "w_out", "ple_norm_g", "w_ple_gate", "w_ple_proj", "final_g"]
    return (loss, grad_x, *[res[n][0] for n in order], *[res[n][1] for n in order],
            *[res[n][2] for n in order], *[res[n][3] for n in order])
```

```python
import jax
import jax.numpy as jnp
from jax import lax
from jax.experimental import pallas as pl
from jax.experimental.pallas import tpu as pltpu

F32 = jnp.float32
BF16 = jnp.bfloat16

D_MODEL = 1024
WIDTH_A = 512
WIDTH_B = 512
HEADS_A = 4
HEAD_DIM = 128
CHUNK = 128
PLE_DIM = 256
PROJ_WIDTH = 3584
DEPTH = 2
EPS = 1e-6
N_DEV = 8

ADAM_LR = 0.001
ADAM_B1 = 0.9
ADAM_B2 = 0.999
ADAM_EPS = 1e-08
ADAM_WD = 0.01
ADAM_STEP = 10

ROWS_IN = PROJ_WIDTH // N_DEV
ROWS_OUT = D_MODEL // N_DEV
ROWS_GATE = D_MODEL // N_DEV
ROWS_PROJ = (D_MODEL // N_DEV) * PLE_DIM // D_MODEL
ROWS_CONV = 16
OFF_IN = 0
OFF_OUT = OFF_IN + ROWS_IN
OFF_GATE = OFF_OUT + ROWS_OUT
OFF_PROJ = OFF_GATE + ROWS_GATE
OFF_CONV = OFF_PROJ + ROWS_PROJ
ROWS_GRAD = OFF_CONV
ROWS_LAYER = OFF_CONV + ROWS_CONV

SMALL_ROWS = 8
SMALL_NORM = 0
SMALL_PLE = 1
SMALL_LN = 2
SMALL_BS = 3
SMALL_CONV = 4
HEAD_FINAL = 0
HEAD_LOSS = 1
TOTAL_HEAD = DEPTH * SMALL_ROWS
TOTAL_WS = TOTAL_HEAD + SMALL_ROWS
TOTAL_ROWS = TOTAL_WS + CHUNK

MIB = 1024 * 1024
MESH = pl.DeviceIdType.MESH

NT_DIMS = (((1,), (1,)), ((), ()))
TN_DIMS = (((0,), (0,)), ((), ()))


def _dot(a, b):
    return jnp.dot(a, b, preferred_element_type=F32)


def _dot_nt(a, b):
    return lax.dot_general(a, b, NT_DIMS, preferred_element_type=F32)


def _dot_tn(a, b):
    return lax.dot_general(a, b, TN_DIMS, preferred_element_type=F32)


def _colsum8(a):
    rows, n = a.shape
    return jnp.sum(a.reshape(rows // 8, 8, n), axis=0)


def _sigmoid(z):
    return 1.0 / (1.0 + jnp.exp(-z))


def _tile(t, want):
    return want if t % want == 0 else t


def _shift_down(a, k, above):
    rolled = pltpu.roll(a, k, 0)
    row = lax.broadcasted_iota(jnp.int32, (8, a.shape[1]), 0)
    head = rolled[0:8]
    for i in range(k):
        head = jnp.where(row == i, above[8 - k + i:8 - k + i + 1, :], head)
    return jnp.concatenate([head, rolled[8:]], axis=0)


def _shift_up(a, k, below):
    rows = a.shape[0]
    rolled = pltpu.roll(a, rows - k, 0)
    row = lax.broadcasted_iota(jnp.int32, (8, a.shape[1]), 0)
    tail = rolled[rows - 8:rows]
    for i in range(k):
        tail = jnp.where(row == 8 - k + i, below[i:i + 1, :], tail)
    return jnp.concatenate([rolled[:rows - 8], tail], axis=0)


class _TwoLevelGather:
    def __init__(self, x_ref, out_ref, m_per, send_sems, recv_sems, local_sem):
        x, y, c = lax.axis_index("x"), lax.axis_index("y"), lax.axis_index("c")
        self.me, self.sibling = (x, y, c), (x, y, 1 - c)
        self.xn, self.yn, self.diag = (1 - x, y, c), (x, 1 - y, c), (1 - x, 1 - y, c)
        self.x_ref, self.out_ref, self.m_per = x_ref, out_ref, m_per
        self.half = (m_per // 32) * 16
        self.send_sems, self.recv_sems = send_sems, recv_sems
        self.mine = pltpu.make_async_copy(x_ref, self.rows(self.me), local_sem)

    def rows(self, block, part=None):
        px, py, pc = block
        base = (4 * px + 2 * py + pc) * self.m_per
        if part is None:
            return self.out_ref.at[pl.ds(base, self.m_per), :]
        if part == 0:
            return self.out_ref.at[pl.ds(base, self.half), :]
        return self.out_ref.at[pl.ds(base + self.half, self.m_per - self.half), :]

    def copy(self, k, block, to, src=None, part=None):
        return pltpu.make_async_remote_copy(
            src_ref=self.rows(block, part) if src is None else src,
            dst_ref=self.rows(block, part),
            send_sem=self.send_sems.at[k],
            recv_sem=self.recv_sems.at[k],
            device_id=to,
            device_id_type=MESH,
        )

    def first(self):
        return [self.copy(0, self.me, self.sibling, src=self.x_ref),
                self.copy(1, self.me, self.xn, src=self.x_ref),
                self.copy(2, self.me, self.yn, src=self.x_ref)]

    def second(self):
        return [self.copy(3, self.xn, self.yn, part=0), self.copy(7, self.yn, self.xn, part=1),
                self.copy(4, self.xn, self.sibling), self.copy(5, self.yn, self.sibling)]

    def third(self):
        return [self.copy(6, self.diag, self.sibling)]

    def start(self):
        self.mine.start()
        for cp in self.first():
            cp.start()

    def pass_on(self):
        fwd_x, fwd_y, sib_x, sib_y = self.second()
        self.copy(1, self.xn, self.me).wait_recv()
        fwd_x.start()
        sib_x.start()
        self.copy(2, self.yn, self.me).wait_recv()
        fwd_y.start()
        sib_y.start()

    def pass_on_diagonal(self):
        self.copy(3, self.diag, self.me, part=0).wait_recv()
        self.copy(7, self.diag, self.me, part=1).wait_recv()
        self.third()[0].start()

    def finish(self):
        sib = (self.sibling[0], self.sibling[1], self.sibling[2])
        self.copy(0, sib, self.me).wait_recv()
        for k, chip in ((4, self.xn), (5, self.yn), (6, self.diag)):
            self.copy(k, (chip[0], chip[1], sib[2]), self.me).wait_recv()
        for cp in self.first() + self.second() + self.third():
            cp.wait_send()
        self.mine.wait()


GATHER_SEMS = [pltpu.SemaphoreType.DMA((8,)), pltpu.SemaphoreType.DMA((8,)), pltpu.SemaphoreType.DMA]


def _all_gather_rows(shard):
    m_per, n = shard.shape

    def body(x_ref, out_ref, send_sems, recv_sems, local_sem):
        ag = _TwoLevelGather(x_ref, out_ref, m_per, send_sems, recv_sems, local_sem)
        ag.start()
        ag.pass_on()
        ag.pass_on_diagonal()
        ag.finish()

    return pl.pallas_call(
        body,
        name="weights_all_gather",
        out_shape=pltpu.HBM((N_DEV * m_per, n), shard.dtype),
        in_specs=[pl.BlockSpec(memory_space=pltpu.HBM)],
        out_specs=pl.BlockSpec(memory_space=pltpu.HBM),
        scratch_shapes=list(GATHER_SEMS),
    )(pltpu.with_memory_space_constraint(shard, pltpu.HBM))


PROJ_PARTS = D_MODEL // PLE_DIM
N_WEIGHT_COPIES = N_DEV * (3 + PROJ_PARTS)


def _weight_copies(wg_ref, w_in_t, w_out, w_gate, w_proj_t, sems):
    copies = []
    for s in range(N_DEV):
        base = s * ROWS_LAYER
        for dst, off, rows in ((w_in_t, OFF_IN, ROWS_IN), (w_out, OFF_OUT, ROWS_OUT), (w_gate, OFF_GATE, ROWS_GATE)):
            copies.append((wg_ref.at[pl.ds(base + off, rows), :], dst.at[pl.ds(s * rows, rows), :]))
        for j in range(PROJ_PARTS):
            copies.append((
                wg_ref.at[pl.ds(base + OFF_PROJ, ROWS_PROJ), pl.ds(j * PLE_DIM, PLE_DIM)],
                w_proj_t.at[pl.ds(s * ROWS_OUT + j * ROWS_PROJ, ROWS_PROJ), :],
            ))
    return [pltpu.make_async_copy(src, dst, sems.at[k]) for k, (src, dst) in enumerate(copies)]


def _forward_layer(layer, x, p_all, wg, conv_k, norm_g, ln_g, ln_b, w_mix, b_mix, ple_g, next_shard=None):
    t = x.shape[0]
    tm = _tile(t, 512)
    nt = t // tm
    gathers = next_shard is not None

    def body(*refs):
        (x_ref, p_ref, wg_ref, cw_ref, ng_ref, lng_ref, lnb_ref, wm_ref, bm_ref, pg_ref) = refs[:10]
        refs = refs[10:]
        if gathers:
            shard_ref, refs = refs[0], refs[1:]
        (proj_ref, hn_ref, cat_ref, r_ref, gpre_ref, x1_ref, x2_ref) = refs[:7]
        refs = refs[7:]
        if gathers:
            gathered_ref, refs = refs[0], refs[1:]
        (w_in_t, w_out, w_gate, wpt_ref, vln_s, mixed_s, halo_s, sems) = refs[:8]
        i = pl.program_id(0)
        if gathers:
            ag = _TwoLevelGather(shard_ref, gathered_ref, ROWS_LAYER, *refs[8:11])

            @pl.when(i == 0)
            def _():
                ag.start()

            @pl.when(i == (5 * nt) // 16)
            def _():
                ag.pass_on()

            @pl.when(i == nt // 2)
            def _():
                ag.pass_on_diagonal()

        @pl.when(i == 0)
        def _():
            copies = _weight_copies(wg_ref, w_in_t, w_out, w_gate, wpt_ref, sems)
            for cp in copies:
                cp.start()
            halo_s[...] = jnp.zeros_like(halo_s)
            for cp in copies:
                cp.wait()

        xv = x_ref[...]
        rstd0 = lax.rsqrt(jnp.mean(xv * xv, axis=-1, keepdims=True) + EPS)
        hn_ref[...] = (xv * rstd0 * ng_ref[...]).astype(BF16)

        def proj_section(k):
            sec = _dot_nt(hn_ref[...], w_in_t[pl.ds(k * 512, 512), :])
            proj_ref[:, k * 512:(k + 1) * 512] = sec.astype(BF16)
            return sec

        v = proj_section(1)
        mu = jnp.mean(v, axis=-1, keepdims=True)
        vc = v - mu
        var = jnp.mean(vc * vc, axis=-1, keepdims=True)
        vln = vc * lax.rsqrt(var + EPS) * lng_ref[...] + lnb_ref[...]
        vln_s[...] = vln.astype(BF16)
        for ci in range(tm // CHUNK):
            rows = pl.ds(ci * CHUNK, CHUNK)
            for h in range(HEADS_A):
                cols = pl.ds(h * HEAD_DIM, HEAD_DIM)
                mixed_s[rows, cols] = _dot(wm_ref[h], vln_s[rows, cols]) + bm_ref[h]
        u = proj_section(0)
        za = proj_section(2)
        out_a = u * mixed_s[...] * (za * _sigmoid(za))
        cat_ref[:, 0:512] = out_a.astype(BF16)

        xc = proj_section(5) * proj_section(3)
        prev = halo_s[...]
        xc_m1 = _shift_down(xc, 1, prev)
        xc_m2 = _shift_down(xc, 2, prev)
        halo_s[...] = xc[tm - 8:tm, :]
        cw = cw_ref[...]
        yc = cw[0:1, :] * xc_m2 + cw[1:2, :] * xc_m1 + cw[2:3, :] * xc
        zb = proj_section(6)
        out_b = proj_section(4) * yc * (zb * _sigmoid(zb))
        cat_ref[:, 512:1024] = out_b.astype(BF16)

        x1 = xv + _dot(cat_ref[...], w_out[...])
        x1_ref[...] = x1
        rstd1 = lax.rsqrt(jnp.mean(x1 * x1, axis=-1, keepdims=True) + EPS)
        r_ref[...] = (x1 * rstd1 * pg_ref[...]).astype(BF16)
        gpre = _dot(r_ref[...], w_gate[...])
        gpre_ref[...] = gpre.astype(BF16)
        pp = _dot_nt(p_ref[...].astype(BF16), wpt_ref[...])
        x2_ref[...] = x1 + _sigmoid(gpre) * pp

        if gathers:
            @pl.when(i == nt - 1)
            def _():
                ag.finish()

    def tok(width):
        return pl.BlockSpec((tm, width), lambda i: (i, 0))

    def whole(shape):
        return pl.BlockSpec(shape, lambda i: (0,) * len(shape))

    hbm = pl.BlockSpec(memory_space=pl.ANY)
    operands = [x, p_all, wg, conv_k, norm_g, ln_g, ln_b, w_mix, b_mix, ple_g]
    in_specs = [
        tok(D_MODEL), pl.BlockSpec((None, None, tm, PLE_DIM), lambda i: (layer, 0, i, 0)), hbm,
        whole((8, WIDTH_B)), whole((1, D_MODEL)), whole((1, WIDTH_A)), whole((1, WIDTH_A)),
        whole((HEADS_A, CHUNK, CHUNK)), whole((HEADS_A, CHUNK, HEAD_DIM)), whole((1, D_MODEL)),
    ]
    out_specs = [tok(PROJ_WIDTH), tok(D_MODEL), tok(D_MODEL), tok(D_MODEL), tok(D_MODEL), tok(D_MODEL), tok(D_MODEL)]
    out_shape = [
        jax.ShapeDtypeStruct((t, PROJ_WIDTH), BF16),
        jax.ShapeDtypeStruct((t, D_MODEL), BF16),
        jax.ShapeDtypeStruct((t, D_MODEL), BF16),
        jax.ShapeDtypeStruct((t, D_MODEL), BF16),
        jax.ShapeDtypeStruct((t, D_MODEL), BF16),
        jax.ShapeDtypeStruct((t, D_MODEL), F32),
        jax.ShapeDtypeStruct((t, D_MODEL), F32),
    ]
    scratch_shapes = [
        pltpu.VMEM((PROJ_WIDTH, D_MODEL), BF16),
        pltpu.VMEM((D_MODEL, D_MODEL), BF16),
        pltpu.VMEM((D_MODEL, D_MODEL), BF16),
        pltpu.VMEM((D_MODEL, PLE_DIM), BF16),
        pltpu.VMEM((tm, WIDTH_A), BF16),
        pltpu.VMEM((tm, WIDTH_A), F32),
        pltpu.VMEM((8, WIDTH_B), F32),
        pltpu.SemaphoreType.DMA((N_WEIGHT_COPIES,)),
    ]
    if gathers:
        operands.append(pltpu.with_memory_space_constraint(next_shard, pltpu.HBM))
        in_specs.append(pl.BlockSpec(memory_space=pltpu.HBM))
        out_specs.append(pl.BlockSpec(memory_space=pltpu.HBM))
        out_shape.append(pltpu.HBM((N_DEV * ROWS_LAYER, D_MODEL), BF16))
        scratch_shapes += list(GATHER_SEMS)

    return pl.pallas_call(
        body,
        name=f"layer{layer}_forward",
        grid=(nt,),
        in_specs=in_specs,
        out_specs=out_specs,
        out_shape=out_shape,
        scratch_shapes=scratch_shapes,
        compiler_params=pltpu.CompilerParams(dimension_semantics=("arbitrary",), vmem_limit_bytes=56 * MIB),
    )(*operands)


class _DirectScatter:
    def __init__(self, pack_ref, pieces_ref, send_sems, recv_sems, local_sem):
        x, y, c = lax.axis_index("x"), lax.axis_index("y"), lax.axis_index("c")
        me = 4 * x + 2 * y + c
        self.copies = []
        for k in range(N_DEV - 1):
            fx, fy, fc = ((k + 1) >> 2) & 1, ((k + 1) >> 1) & 1, (k + 1) & 1
            tx, ty, tc = x ^ fx, y ^ fy, c ^ fc
            self.copies.append(
                pltpu.make_async_remote_copy(
                    src_ref=pack_ref.at[4 * tx + 2 * ty + tc], dst_ref=pieces_ref.at[me],
                    send_sem=send_sems.at[k], recv_sem=recv_sems.at[k],
                    device_id=(tx, ty, tc), device_id_type=MESH,
                )
            )
        self.mine = pltpu.make_async_copy(pack_ref.at[me], pieces_ref.at[me], local_sem)

    def start(self):
        self.mine.start()
        for cp in self.copies:
            cp.start()

    def finish(self):
        for cp in self.copies:
            cp.wait_recv()
        for cp in self.copies:
            cp.wait_send()
        self.mine.wait()


SCATTER_SEMS = [pltpu.SemaphoreType.DMA((N_DEV - 1,)), pltpu.SemaphoreType.DMA((N_DEV - 1,)), pltpu.SemaphoreType.DMA]


def _backward_layer(layer, dx2, x_in, x1, proj, gpre, p_all, wg, conv_k, norm_g, ln_g, ln_b,
                    w_mix, w_mix_t, b_mix, ple_g, loss_head=None):
    t = x_in.shape[0]
    tm = _tile(t, 256)
    nt = t // tm
    n_chunks = tm // CHUNK
    halo_rows = 16
    heads = loss_head is not None

    def body(*refs):
        (dx2_ref, xin_ref, x1_ref, proj_ref, halo_ref, gpre_ref, p_ref, wg_ref, cw_ref,
         ng_ref, lng_ref, lnb_ref, wm_ref, wmt_ref, bm_ref, pg_ref) = refs[:16]
        refs = refs[16:]
        if heads:
            tgt_ref, fg_ref = refs[:2]
            refs = refs[2:]
        (dxin_ref, dproj_ref, dx1_ref, dgpre_ref, dpp_ref, small_ref, dws_ref) = refs[:7]
        refs = refs[7:]
        if heads:
            head_ref, refs = refs[0], refs[1:]
        (w_in_t, w_out, w_gate, wpt_ref, vln_s, mixed_s, dmix_s, dvln_s, carry_s,
         ng_acc, pg_acc, lng_acc, lnb_acc, cw_acc, dbm_ref, sems) = refs[:16]
        if heads:
            loss_acc, fg_acc = refs[16:18]
        i = pl.program_id(0)
        tile = nt - 1 - i

        @pl.when(i == 0)
        def _():
            copies = _weight_copies(wg_ref, w_in_t, w_out, w_gate, wpt_ref, sems)
            for cp in copies:
                cp.start()
            if heads:
                loss_acc[...] = jnp.zeros_like(loss_acc)
                fg_acc[...] = jnp.zeros_like(fg_acc)
            carry_s[...] = jnp.zeros_like(carry_s)
            ng_acc[...] = jnp.zeros_like(ng_acc)
            pg_acc[...] = jnp.zeros_like(pg_acc)
            lng_acc[...] = jnp.zeros_like(lng_acc)
            lnb_acc[...] = jnp.zeros_like(lnb_acc)
            cw_acc[...] = jnp.zeros_like(cw_acc)
            dws_ref[...] = jnp.zeros_like(dws_ref)
            dbm_ref[...] = jnp.zeros_like(dbm_ref)
            for cp in copies:
                cp.wait()

        if heads:
            x2v = dx2_ref[...]
            fg = fg_ref[...]
            rstdf = lax.rsqrt(jnp.mean(x2v * x2v, axis=-1, keepdims=True) + EPS)
            xhatf = x2v * rstdf
            err = xhatf * fg - tgt_ref[...]
            loss_acc[...] += _colsum8(err * err)
            dy = err * (1.0 / D_MODEL)
            fg_acc[...] += _colsum8(dy * xhatf)
            dxhf = dy * fg
            dx2v = rstdf * (dxhf - xhatf * jnp.mean(dxhf * xhatf, axis=-1, keepdims=True))
        else:
            dx2v = dx2_ref[...]

        gate = _sigmoid(gpre_ref[...].astype(F32))
        pp = _dot_nt(p_ref[...].astype(BF16), wpt_ref[...])
        dpp = dx2v * gate
        dpp_ref[...] = dpp.astype(BF16)
        dgpre = (dpp * pp * (1.0 - gate)).astype(BF16)
        dgpre_ref[...] = dgpre
        dr = _dot_nt(dgpre, w_gate[...])
        x1v = x1_ref[...]
        rstd1 = lax.rsqrt(jnp.mean(x1v * x1v, axis=-1, keepdims=True) + EPS)
        xhat1 = x1v * rstd1
        pg_acc[...] += _colsum8(dr * xhat1)
        dxh = dr * pg_ref[...]
        dx1 = dx2v + rstd1 * (dxh - xhat1 * jnp.mean(dxh * xhat1, axis=-1, keepdims=True))
        dx1b = dx1.astype(BF16)
        dx1_ref[...] = dx1b

        dcat = _dot_nt(dx1b, w_out[...])
        dca = dcat[:, 0:512]
        dcb = dcat[:, 512:1024]

        u = proj_ref[:, 0:512]
        v = proj_ref[:, 512:1024].astype(F32)
        za = proj_ref[:, 1024:1536]
        mu = jnp.mean(v, axis=-1, keepdims=True)
        vc = v - mu
        var = jnp.mean(vc * vc, axis=-1, keepdims=True)
        rs = lax.rsqrt(var + EPS)
        vhat = vc * rs
        lng = lng_ref[...]
        vln_s[...] = (vhat * lng + lnb_ref[...]).astype(BF16)
        for ci in range(n_chunks):
            rows = pl.ds(ci * CHUNK, CHUNK)
            for h in range(HEADS_A):
                cols = pl.ds(h * HEAD_DIM, HEAD_DIM)
                mixed_s[rows, cols] = (_dot(wm_ref[h], vln_s[rows, cols]) + bm_ref[h]).astype(BF16)
        mixed = mixed_s[...]
        sga = _sigmoid(za)
        sa = za * sga
        dsa = sga + sa * (1.0 - sga)

        def put_section(k, val):
            dproj_ref[:, k * 512:(k + 1) * 512] = val.astype(BF16)

        dcab = dca.astype(BF16)
        dca_sa = dcab * sa
        put_section(0, dca_sa * mixed)
        dmix_s[...] = dca_sa * u
        put_section(2, (dcab * dsa) * (u * mixed))
        dbm_acc = jnp.zeros((CHUNK, WIDTH_A), F32)
        for ci in range(n_chunks):
            rows = pl.ds(ci * CHUNK, CHUNK)
            dbm_acc = dbm_acc + dmix_s[rows, :].astype(F32)
            for h in range(HEADS_A):
                cols = pl.ds(h * HEAD_DIM, HEAD_DIM)
                dvln_s[rows, cols] = _dot(wmt_ref[h], dmix_s[rows, cols])
                dws_ref[:, cols] += _dot_nt(dmix_s[rows, cols], vln_s[rows, cols])
        dbm_ref[...] += dbm_acc
        dvln = dvln_s[...]
        lng_acc[...] += _colsum8(dvln * vhat)
        lnb_acc[...] += _colsum8(dvln)
        dvh = dvln * lng
        dv = rs * (dvh - jnp.mean(dvh, axis=-1, keepdims=True) - vhat * jnp.mean(dvh * vhat, axis=-1, keepdims=True))
        put_section(1, dv)

        hb = proj_ref[:, 1536:2048].astype(F32)
        gb = proj_ref[:, 2048:2560]
        gc = proj_ref[:, 2560:3072].astype(F32)
        zb = proj_ref[:, 3072:3584]
        xc = gc * hb
        prev = halo_ref[:, 2560:3072].astype(F32) * halo_ref[:, 1536:2048].astype(F32)
        prev = jnp.where(tile > 0, prev[halo_rows - 8:halo_rows, :], 0.0)
        xc_m1 = _shift_down(xc, 1, prev)
        xc_m2 = _shift_down(xc, 2, prev)
        cw = cw_ref[...]
        yc = cw[0:1, :] * xc_m2 + cw[1:2, :] * xc_m1 + cw[2:3, :] * xc
        sgb = _sigmoid(zb)
        sb = zb * sgb
        dsb = sgb + sb * (1.0 - sgb)
        dcbb = dcb.astype(BF16)
        ycb = yc.astype(BF16)
        dcb_sb = dcbb * sb
        put_section(4, dcb_sb * ycb)
        dyc = (dcb_sb * gb).astype(F32)
        put_section(6, (dcbb * dsb) * (gb * ycb))
        nxt = carry_s[...]
        dyc_p1 = _shift_up(dyc, 1, nxt)
        dyc_p2 = _shift_up(dyc, 2, nxt)
        carry_s[...] = dyc[0:8, :]
        dxc = cw[2:3, :] * dyc + cw[1:2, :] * dyc_p1 + cw[0:1, :] * dyc_p2
        cw_acc[0] += _colsum8(dyc * xc_m2)
        cw_acc[1] += _colsum8(dyc * xc_m1)
        cw_acc[2] += _colsum8(dyc * xc)
        put_section(3, dxc * gc)
        put_section(5, dxc * hb)

        dhn = _dot(dproj_ref[...], w_in_t[...])
        xv = xin_ref[...]
        rstd0 = lax.rsqrt(jnp.mean(xv * xv, axis=-1, keepdims=True) + EPS)
        xhat0 = xv * rstd0
        ng_acc[...] += _colsum8(dhn * xhat0)
        dxh0 = dhn * ng_ref[...]
        dxin_ref[...] = dx1 + rstd0 * (dxh0 - xhat0 * jnp.mean(dxh0 * xhat0, axis=-1, keepdims=True))

        @pl.when(i == nt - 1)
        def _():
            small_ref[...] = jnp.zeros_like(small_ref)
            small_ref[SMALL_NORM:SMALL_NORM + 1, :] = jnp.sum(ng_acc[...], axis=0, keepdims=True)
            small_ref[SMALL_PLE:SMALL_PLE + 1, :] = jnp.sum(pg_acc[...], axis=0, keepdims=True)
            small_ref[SMALL_LN:SMALL_LN + 1, 0:WIDTH_A] = jnp.sum(lng_acc[...], axis=0, keepdims=True)
            small_ref[SMALL_LN:SMALL_LN + 1, WIDTH_A:2 * WIDTH_A] = jnp.sum(lnb_acc[...], axis=0, keepdims=True)
            for h in range(HEADS_A):
                cols = pl.ds(h * HEAD_DIM, HEAD_DIM)
                small_ref[SMALL_BS:SMALL_BS + 1, cols] = jnp.sum(jnp.transpose(dbm_ref[:, cols]), axis=0, keepdims=True)
            for k in range(3):
                small_ref[SMALL_CONV + k:SMALL_CONV + k + 1, 0:WIDTH_B] = jnp.sum(cw_acc[k], axis=0, keepdims=True)
            if heads:
                total = jnp.sum(loss_acc[...]) * (0.5 / D_MODEL)
                rows8 = lax.broadcasted_iota(jnp.int32, (SMALL_ROWS, D_MODEL), 0)
                lanes8 = lax.broadcasted_iota(jnp.int32, (SMALL_ROWS, D_MODEL), 1)
                head_ref[...] = jnp.where((rows8 == HEAD_LOSS) & (lanes8 == 0), total, 0.0)
                head_ref[HEAD_FINAL:HEAD_FINAL + 1, :] = jnp.sum(fg_acc[...], axis=0, keepdims=True)

    def tok(width):
        return pl.BlockSpec((tm, width), lambda i: (nt - 1 - i, 0))

    def whole(shape):
        return pl.BlockSpec(shape, lambda i: (0,) * len(shape))

    halo_spec = pl.BlockSpec(
        (halo_rows, PROJ_WIDTH), lambda i: (jnp.maximum((nt - 1 - i) * (tm // halo_rows) - 1, 0), 0)
    )
    hbm = pl.BlockSpec(memory_space=pl.ANY)
    operands = [dx2, x_in, x1, proj, proj, gpre, p_all, wg, conv_k, norm_g, ln_g, ln_b, w_mix, w_mix_t, b_mix, ple_g]
    in_specs = [
        tok(D_MODEL), tok(D_MODEL), tok(D_MODEL), tok(PROJ_WIDTH), halo_spec, tok(D_MODEL),
        pl.BlockSpec((None, None, tm, PLE_DIM), lambda i: (layer, 0, nt - 1 - i, 0)), hbm,
        whole((8, WIDTH_B)), whole((1, D_MODEL)), whole((1, WIDTH_A)), whole((1, WIDTH_A)),
        whole((HEADS_A, CHUNK, CHUNK)), whole((HEADS_A, CHUNK, CHUNK)), whole((HEADS_A, CHUNK, HEAD_DIM)),
        whole((1, D_MODEL)),
    ]
    out_specs = [
        tok(D_MODEL), tok(PROJ_WIDTH), tok(D_MODEL), tok(D_MODEL), tok(D_MODEL),
        whole((SMALL_ROWS, D_MODEL)), whole((CHUNK, WIDTH_A)),
    ]
    out_shape = [
        jax.ShapeDtypeStruct((t, D_MODEL), F32),
        jax.ShapeDtypeStruct((t, PROJ_WIDTH), BF16),
        jax.ShapeDtypeStruct((t, D_MODEL), BF16),
        jax.ShapeDtypeStruct((t, D_MODEL), BF16),
        jax.ShapeDtypeStruct((t, D_MODEL), BF16),
        jax.ShapeDtypeStruct((SMALL_ROWS, D_MODEL), F32),
        jax.ShapeDtypeStruct((CHUNK, WIDTH_A), F32),
    ]
    scratch_shapes = [
        pltpu.VMEM((PROJ_WIDTH, D_MODEL), BF16),
        pltpu.VMEM((D_MODEL, D_MODEL), BF16),
        pltpu.VMEM((D_MODEL, D_MODEL), BF16),
        pltpu.VMEM((D_MODEL, PLE_DIM), BF16),
        pltpu.VMEM((tm, WIDTH_A), BF16),
        pltpu.VMEM((tm, WIDTH_A), BF16),
        pltpu.VMEM((tm, WIDTH_A), BF16),
        pltpu.VMEM((tm, WIDTH_A), F32),
        pltpu.VMEM((8, WIDTH_B), F32),
        pltpu.VMEM((8, D_MODEL), F32),
        pltpu.VMEM((8, D_MODEL), F32),
        pltpu.VMEM((8, WIDTH_A), F32),
        pltpu.VMEM((8, WIDTH_A), F32),
        pltpu.VMEM((3, 8, WIDTH_B), F32),
        pltpu.VMEM((CHUNK, WIDTH_A), F32),
        pltpu.SemaphoreType.DMA((N_WEIGHT_COPIES,)),
    ]
    if heads:
        operands += list(loss_head)
        in_specs += [tok(D_MODEL), whole((1, D_MODEL))]
        out_specs.append(whole((SMALL_ROWS, D_MODEL)))
        out_shape.append(jax.ShapeDtypeStruct((SMALL_ROWS, D_MODEL), F32))
        scratch_shapes += [pltpu.VMEM((8, D_MODEL), F32), pltpu.VMEM((8, D_MODEL), F32)]

    return pl.pallas_call(
        body,
        name=f"layer{layer}_backward",
        grid=(nt,),
        in_specs=in_specs,
        out_specs=out_specs,
        out_shape=out_shape,
        scratch_shapes=scratch_shapes,
        compiler_params=pltpu.CompilerParams(dimension_semantics=("arbitrary",), vmem_limit_bytes=56 * MIB),
    )(*operands)


def _sum_pieces(layer, pieces):
    rows, n = pieces.shape[1], pieces.shape[2]
    blocks = 2
    rb = rows // blocks

    def body(p_ref, out_ref):
        total = p_ref[0].astype(F32)
        for j in range(1, N_DEV):
            total = total + p_ref[j].astype(F32)
        out_ref[...] = total

    return pl.pallas_call(
        body,
        name=f"layer{layer}_grad_sum",
        grid=(blocks,),
        out_shape=pltpu.HBM((rows, n), F32),
        in_specs=[pl.BlockSpec((N_DEV, rb, n), lambda i: (0, i, 0))],
        out_specs=pl.BlockSpec((rb, n), lambda i: (i, 0)),
        compiler_params=pltpu.CompilerParams(dimension_semantics=("arbitrary",), vmem_limit_bytes=32 * MIB),
    )(pieces)


def _weight_grads(layer, dproj, hn, cat, dx1, r, dgpre, dpp, p_all, scatter_pack=None):
    t = hn.shape[0]
    tk = _tile(t, 512)
    nt = t // tk
    in_blocks = PROJ_WIDTH // 512
    scatters = scatter_pack is not None

    def body(*refs):
        (dproj_ref, hn_ref, cat_ref, dx1_ref, r_ref, dgpre_ref, dpp_ref, p_ref) = refs[:8]
        refs = refs[8:]
        if scatters:
            prior_ref, refs = refs[0], refs[1:]
        pack_ref, refs = refs[0], refs[1:]
        if scatters:
            pieces_ref, refs = refs[0], refs[1:]
        (acc_in, acc_out, acc_gate, acc_proj, stage, sems) = refs[:6]
        i = pl.program_id(0)
        if scatters:
            scatter = _DirectScatter(prior_ref, pieces_ref, *refs[6:9])

            @pl.when(i == 0)
            def _():
                scatter.start()

        @pl.when(i == 0)
        def _():
            acc_in[...] = jnp.zeros_like(acc_in)
            acc_out[...] = jnp.zeros_like(acc_out)
            acc_gate[...] = jnp.zeros_like(acc_gate)
            acc_proj[...] = jnp.zeros_like(acc_proj)

        hnv = hn_ref[...]
        for b in range(in_blocks):
            acc_in[pl.ds(b * 512, 512), :] += _dot_tn(dproj_ref[:, b * 512:(b + 1) * 512], hnv)
        dx1v = dx1_ref[...]
        dgv = dgpre_ref[...]
        for b in range(D_MODEL // 512):
            acc_out[pl.ds(b * 512, 512), :] += _dot_tn(cat_ref[:, b * 512:(b + 1) * 512], dx1v)
            acc_gate[pl.ds(b * 512, 512), :] += _dot_tn(r_ref[:, b * 512:(b + 1) * 512], dgv)
        pv = p_ref[...].astype(BF16)
        for b in range(D_MODEL // 512):
            acc_proj[pl.ds(b * 512, 512), :] += _dot_tn(dpp_ref[:, b * 512:(b + 1) * 512], pv)

        @pl.when(i == nt - 1)
        def _():
            def out_copy(s):
                return pltpu.make_async_copy(stage.at[s % 2], pack_ref.at[s], sems.at[s % 2])

            for s in range(N_DEV):
                if s >= 2:
                    out_copy(s - 2).wait()
                buf = stage.at[s % 2]
                buf[pl.ds(OFF_IN, ROWS_IN), :] = acc_in[pl.ds(s * ROWS_IN, ROWS_IN), :].astype(BF16)
                buf[pl.ds(OFF_OUT, ROWS_OUT), :] = acc_out[pl.ds(s * ROWS_OUT, ROWS_OUT), :].astype(BF16)
                buf[pl.ds(OFF_GATE, ROWS_GATE), :] = acc_gate[pl.ds(s * ROWS_GATE, ROWS_GATE), :].astype(BF16)
                for j in range(D_MODEL // PLE_DIM):
                    buf[pl.ds(OFF_PROJ, ROWS_PROJ), pl.ds(j * PLE_DIM, PLE_DIM)] = acc_proj[
                        pl.ds(s * ROWS_OUT + j * ROWS_PROJ, ROWS_PROJ), :
                    ].astype(BF16)
                out_copy(s).start()
            out_copy(N_DEV - 2).wait()
            out_copy(N_DEV - 1).wait()
            if scatters:
                scatter.finish()

    def tok(width):
        return pl.BlockSpec((tk, width), lambda i: (i, 0))

    hbm = pl.BlockSpec(memory_space=pl.ANY)
    pack_shape = jax.ShapeDtypeStruct((N_DEV, ROWS_GRAD, D_MODEL), BF16)
    operands = [dproj, hn, cat, dx1, r, dgpre, dpp, p_all]
    in_specs = [tok(PROJ_WIDTH), tok(D_MODEL), tok(D_MODEL), tok(D_MODEL), tok(D_MODEL), tok(D_MODEL), tok(D_MODEL),
                pl.BlockSpec((None, None, tk, PLE_DIM), lambda i: (layer, 0, i, 0))]
    out_specs, out_shape = [hbm], [pack_shape]
    scratch_shapes = [
        pltpu.VMEM((PROJ_WIDTH, D_MODEL), F32),
        pltpu.VMEM((D_MODEL, D_MODEL), F32),
        pltpu.VMEM((D_MODEL, D_MODEL), F32),
        pltpu.VMEM((D_MODEL, PLE_DIM), F32),
        pltpu.VMEM((2, ROWS_GRAD, D_MODEL), BF16),
        pltpu.SemaphoreType.DMA((2,)),
    ]
    if scatters:
        operands.append(scatter_pack)
        in_specs.append(hbm)
        out_specs.append(hbm)
        out_shape.append(pack_shape)
        scratch_shapes += list(SCATTER_SEMS)

    return pl.pallas_call(
        body,
        name=f"layer{layer}_weight_grads",
        grid=(nt,),
        in_specs=in_specs,
        out_specs=out_specs,
        out_shape=out_shape,
        scratch_shapes=scratch_shapes,
        compiler_params=pltpu.CompilerParams(dimension_semantics=("arbitrary",), vmem_limit_bytes=58 * MIB),
    )(*operands)


def _reduce_scatter_all_reduce(layer, pack, smalls, head, dws):
    rows, n = pack.shape[1], pack.shape[2]
    assert DEPTH * WIDTH_A == D_MODEL and n == D_MODEL

    def body(g_ref, *refs):
        small_refs, refs = refs[:DEPTH], refs[DEPTH:]
        head_ref, refs = refs[0], refs[1:]
        dws_refs, refs = refs[:DEPTH], refs[DEPTH:]
        (out_ref, total_ref, r1, a_s, r2, via, sp, sr1, sq, send1, recv1, send2, recv2, ssend, srecv) = refs
        x, y, c = lax.axis_index("x"), lax.axis_index("y"), lax.axis_index("c")
        sibling = (x, y, 1 - c)
        chip = 2 * x + y
        flips = [(1, 0), (0, 1), (1, 1)]

        for l in range(DEPTH):
            sp[l * SMALL_ROWS:(l + 1) * SMALL_ROWS, :] = small_refs[l][...]
            sp[TOTAL_WS:TOTAL_ROWS, l * WIDTH_A:(l + 1) * WIDTH_A] = dws_refs[l][...]
        sp[TOTAL_HEAD:TOTAL_WS, :] = head_ref[...]

        small_pair = pltpu.make_async_remote_copy(
            src_ref=sp, dst_ref=sr1, send_sem=ssend.at[0], recv_sem=srecv.at[0], device_id=sibling, device_id_type=MESH
        )

        def to_sibling(j):
            return pltpu.make_async_remote_copy(
                src_ref=g_ref.at[2 * j + 1 - c], dst_ref=r1.at[j], send_sem=send1.at[j], recv_sem=recv1.at[j],
                device_id=sibling, device_id_type=MESH,
            )

        first = [to_sibling(j) for j in range(4)]
        small_pair.start()
        for cp in first:
            cp.start()

        small_pair.wait_recv()
        sq[chip] = sp[...] + sr1[...]
        small_chips = [
            pltpu.make_async_remote_copy(
                src_ref=sq.at[chip], dst_ref=sq.at[chip], send_sem=ssend.at[1 + k], recv_sem=srecv.at[1 + k],
                device_id=(x ^ fx, y ^ fy, c), device_id_type=MESH,
            )
            for k, (fx, fy) in enumerate(flips)
        ]
        for cp in small_chips:
            cp.start()

        for j in range(4):
            first[j].wait_recv()

            @pl.when(chip != j)
            def _():
                a_s[j] = (g_ref[2 * j + c].astype(F32) + r1[j].astype(F32)).astype(BF16)

        half = rows // 2
        lo, hi = pl.ds(0, half), pl.ds(half, rows - half)
        x_nbr, y_nbr = (1 - x, y, c), (x, 1 - y, c)
        chip_x, chip_y, chip_d = 2 * (1 - x) + y, 2 * x + (1 - y), 2 * (1 - x) + (1 - y)

        def ici(k, src, dst, to):
            return pltpu.make_async_remote_copy(
                src_ref=src, dst_ref=dst, send_sem=send2.at[k], recv_sem=recv2.at[k], device_id=to, device_id_type=MESH)

        second = [
            ici(0, a_s.at[chip_d, lo, :], via.at[0], x_nbr),
            ici(1, a_s.at[chip_d, hi, :], via.at[1], y_nbr),
            ici(2, a_s.at[chip_x, lo, :], r2.at[0, lo, :], x_nbr),
            ici(3, a_s.at[chip_y, hi, :], r2.at[1, hi, :], y_nbr),
            ici(4, a_s.at[chip_x, hi, :], r2.at[0, hi, :], x_nbr),
            ici(5, a_s.at[chip_y, lo, :], r2.at[1, lo, :], y_nbr),
        ]
        for cp in second[:4]:
            cp.start()

        out_ref[...] = g_ref[2 * chip + c].astype(F32) + r1[chip].astype(F32)
        for cp in small_chips:
            cp.wait_recv()
        total_ref[...] = ((sq[0] + sq[1]) + sq[2]) + sq[3]

        second[0].wait_recv()
        a_s[chip_y, lo, :] = (a_s[chip_y, lo, :].astype(F32) + via[0].astype(F32)).astype(BF16)
        second[5].start()
        second[1].wait_recv()
        a_s[chip_x, hi, :] = (a_s[chip_x, hi, :].astype(F32) + via[1].astype(F32)).astype(BF16)
        second[4].start()

        second[2].wait_recv()
        second[4].wait_recv()
        out_ref[...] += r2[0].astype(F32)
        second[3].wait_recv()
        second[5].wait_recv()
        out_ref[...] += r2[1].astype(F32)
        small_pair.wait_send()
        for cp in first + small_chips + second:
            cp.wait_send()

    vmem = pl.BlockSpec(memory_space=pltpu.VMEM)
    return pl.pallas_call(
        body,
        name=f"layer{layer}_grad_reduce_scatter",
        out_shape=[jax.ShapeDtypeStruct((rows, n), F32), jax.ShapeDtypeStruct((TOTAL_ROWS, D_MODEL), F32)],
        in_specs=[vmem] * (2 + 2 * DEPTH),
        out_specs=[vmem, vmem],
        scratch_shapes=[
            pltpu.VMEM((4, rows, n), BF16),
            pltpu.VMEM((4, rows, n), BF16),
            pltpu.VMEM((2, rows, n), BF16),
            pltpu.VMEM((2, rows // 2, n), BF16),
            pltpu.VMEM((TOTAL_ROWS, D_MODEL), F32),
            pltpu.VMEM((TOTAL_ROWS, D_MODEL), F32),
            pltpu.VMEM((4, TOTAL_ROWS, D_MODEL), F32),
            pltpu.SemaphoreType.DMA((4,)),
            pltpu.SemaphoreType.DMA((4,)),
            pltpu.SemaphoreType.DMA((6,)),
            pltpu.SemaphoreType.DMA((6,)),
            pltpu.SemaphoreType.DMA((4,)),
            pltpu.SemaphoreType.DMA((4,)),
        ],
        compiler_params=pltpu.CompilerParams(vmem_limit_bytes=48 * MIB),
    )(pack, *smalls, head, *dws)


def _adam_step(w, g, m, v):
    m = ADAM_B1 * m + (1.0 - ADAM_B1) * g
    v = ADAM_B2 * v + (1.0 - ADAM_B2) * (g * g)
    m_hat = m / (1.0 - ADAM_B1 ** ADAM_STEP)
    v_hat = v / (1.0 - ADAM_B2 ** ADAM_STEP)
    return -ADAM_LR * (m_hat / (jnp.sqrt(v_hat) + ADAM_EPS) + ADAM_WD * w), m, v


def _adamw_rows(name, reduced, row_off, states):
    n = len(states)

    def body(*refs):
        red = refs[:DEPTH]
        ins = refs[DEPTH:DEPTH + 3 * n]
        outs = refs[DEPTH + 3 * n:]
        layer = pl.program_id(0)
        for l in range(DEPTH):
            @pl.when(layer == l)
            def _():
                for k in range(n):
                    w_ref, m_ref, v_ref = ins[3 * k:3 * k + 3]
                    g_ref, d_ref, nm_ref, nv_ref = outs[4 * k:4 * k + 4]
                    g = red[l][row_off[k]:row_off[k] + w_ref.shape[0], :]
                    d, m, v = _adam_step(w_ref[...], g, m_ref[...], v_ref[...])
                    g_ref[...] = g
                    d_ref[...] = d
                    nm_ref[...] = m
                    nv_ref[...] = v

    flat = [a for st in states for a in st]
    state_specs, out_specs, out_shape = [], [], []
    for w, _, _ in states:
        spec = pl.BlockSpec((None,) + w.shape[1:], lambda l: (l, 0, 0))
        state_specs += [spec] * 3
        out_specs += [spec] * 4
        out_shape += [jax.ShapeDtypeStruct(w.shape, F32)] * 4
    red_specs = [pl.BlockSpec(a.shape, lambda l: (0, 0)) for a in reduced]
    operands = [pltpu.with_memory_space_constraint(a, pltpu.HBM) for a in (*reduced, *flat)]
    outs = pl.pallas_call(
        body,
        name=name,
        grid=(DEPTH,),
        out_shape=[pltpu.HBM(a.shape, a.dtype) for a in out_shape],
        in_specs=red_specs + state_specs,
        out_specs=out_specs,
        compiler_params=pltpu.CompilerParams(dimension_semantics=("arbitrary",), vmem_limit_bytes=48 * MIB),
    )(*operands)
    return [tuple(outs[4 * k:4 * k + 4]) for k in range(n)]


def _adamw_small(total, g_conv, g_proj, st):
    names = ["norm_g", "ple_norm_g", "ln_v_g", "ln_v_b", "b_s", "w_s", "final_g", "conv_w", "w_ple_proj"]
    cut = names[:7]

    def body(total_ref, gconv_ref, gproj_ref, *refs):
        ins = {nm: refs[3 * k:3 * k + 3] for k, nm in enumerate(names)}
        outs, pos = {}, 3 * len(names)
        for nm in names:
            cnt = 4 if nm in cut else 3
            outs[nm] = refs[pos:pos + cnt]
            pos += cnt

        def update(nm, idx, g):
            w_ref, m_ref, v_ref = ins[nm]
            d, m, v = _adam_step(w_ref[idx], g, m_ref[idx], v_ref[idx])
            o = outs[nm]
            if nm in cut:
                o[0][idx] = g
                o = o[1:]
            o[0][idx] = d
            o[1][idx] = m
            o[2][idx] = v

        tril = (lax.broadcasted_iota(jnp.int32, (CHUNK, CHUNK), 0) >= lax.broadcasted_iota(jnp.int32, (CHUNK, CHUNK), 1))
        for l in range(DEPTH):
            base = l * SMALL_ROWS
            row = (slice(l, l + 1), slice(None))
            update("norm_g", row, total_ref[base + SMALL_NORM:base + SMALL_NORM + 1, :])
            update("ple_norm_g", row, total_ref[base + SMALL_PLE:base + SMALL_PLE + 1, :])
            update("ln_v_g", row, total_ref[base + SMALL_LN:base + SMALL_LN + 1, 0:WIDTH_A])
            update("ln_v_b", row, total_ref[base + SMALL_LN:base + SMALL_LN + 1, WIDTH_A:2 * WIDTH_A])
            for h in range(HEADS_A):
                update("b_s", (l, slice(h, h + 1), slice(None)),
                       total_ref[base + SMALL_BS:base + SMALL_BS + 1, h * HEAD_DIM:(h + 1) * HEAD_DIM])
                lanes = slice(l * WIDTH_A + h * CHUNK, l * WIDTH_A + (h + 1) * CHUNK)
                update("w_s", (l, h), jnp.where(tril, total_ref[TOTAL_WS:TOTAL_ROWS, lanes], 0.0))
        update("final_g", (slice(None), slice(None)), total_ref[TOTAL_HEAD + HEAD_FINAL:TOTAL_HEAD + HEAD_FINAL + 1, :])
        update("conv_w", (slice(None),) * 3, gconv_ref[...])
        update("w_ple_proj", (slice(None),) * 3, gproj_ref[...])

    flat = [a for nm in names for a in st[nm]]
    out_shape = []
    for nm in names:
        out_shape += [jax.ShapeDtypeStruct(st[nm][0].shape, F32)] * (4 if nm in cut else 3)
    def whole(a):
        return pl.BlockSpec(a.shape, lambda i: (0,) * len(a.shape))

    operands = [pltpu.with_memory_space_constraint(a, pltpu.HBM) for a in (total, g_conv, g_proj, *flat)]
    outs = pl.pallas_call(
        body,
        name="adamw_small",
        grid=(1,),
        out_shape=[pltpu.HBM(a.shape, a.dtype) for a in out_shape],
        in_specs=[whole(a) for a in operands],
        out_specs=[whole(a) for a in out_shape],
        compiler_params=pltpu.CompilerParams(dimension_semantics=("arbitrary",), vmem_limit_bytes=32 * MIB),
    )(*operands)
    res, pos = {}, 0
    for nm in names:
        cnt = 4 if nm in cut else 3
        got = tuple(outs[pos:pos + cnt])
        res[nm] = got if nm in cut else ((g_conv if nm == "conv_w" else g_proj),) + got
        pos += cnt
    return res


def _split3_bf16(a):
    b1 = a.astype(BF16)
    r1 = a - b1.astype(F32)
    b2 = r1.astype(BF16)
    b3 = (r1 - b2.astype(F32)).astype(BF16)
    return b1, b2, b3


def _pack_weight_shard(w_in_l, w_out_l, w_gate_l, w_proj_l, conv_w_l):
    w_in_t = jnp.transpose(w_in_l).astype(BF16)
    proj_t = jnp.transpose(w_proj_l).astype(BF16)
    proj_rows = proj_t.reshape(D_MODEL // PLE_DIM, ROWS_PROJ, PLE_DIM).transpose(1, 0, 2).reshape(ROWS_PROJ, D_MODEL)
    conv_parts = jnp.concatenate([b.reshape(-1) for b in _split3_bf16(conv_w_l)])
    conv_rows = jnp.concatenate([conv_parts, jnp.zeros((ROWS_CONV * D_MODEL - conv_parts.shape[0],), BF16)])
    return jnp.concatenate(
        [w_in_t, w_out_l.astype(BF16), w_gate_l.astype(BF16), proj_rows, conv_rows.reshape(ROWS_CONV, D_MODEL)], axis=0
    )


def _unpack_conv(wg):
    per_dev = wg.reshape(N_DEV, ROWS_LAYER, D_MODEL)
    n_conv = (WIDTH_B // N_DEV) * 3
    conv_parts = per_dev[:, OFF_CONV].astype(F32)[:, :3 * n_conv].reshape(N_DEV, 3, n_conv)
    conv = (conv_parts[:, 0] + conv_parts[:, 1]) + conv_parts[:, 2]
    conv_k = jnp.transpose(conv.reshape(WIDTH_B, 3))
    conv_k = jnp.concatenate([conv_k, jnp.zeros((5, WIDTH_B), F32)], axis=0)
    return conv_k


def _unpack_grad_proj(red):
    proj_rows = red[OFF_PROJ:OFF_PROJ + ROWS_PROJ]
    proj_t = proj_rows.reshape(ROWS_PROJ, D_MODEL // PLE_DIM, PLE_DIM).transpose(1, 0, 2).reshape(ROWS_OUT, PLE_DIM)
    return jnp.transpose(proj_t)


def kernel(x, p, norm_g, w_in, ln_v_g, ln_v_b, w_s, b_s, conv_w, w_out, ple_norm_g, w_ple_gate, w_ple_proj, final_g, loss_target, m_norm_g, m_w_in, m_ln_v_g, m_ln_v_b, m_w_s, m_b_s, m_conv_w, m_w_out, m_ple_norm_g, m_w_ple_gate, m_w_ple_proj, m_final_g, v_norm_g, v_w_in, v_ln_v_g, v_ln_v_b, v_w_s, v_b_s, v_conv_w, v_w_out, v_ple_norm_g, v_w_ple_gate, v_w_ple_proj, v_final_g):
    me = 4 * lax.axis_index("x") + 2 * lax.axis_index("y") + lax.axis_index("c")
    xs = x[0]
    target = loss_target[0]

    shards = [_pack_weight_shard(w_in[l], w_out[l], w_ple_gate[l], w_ple_proj[l], conv_w[l]) for l in range(DEPTH)]
    tril = jnp.tril(jnp.ones((CHUNK, CHUNK), F32))

    def consts(l, wg_l):
        conv_k = _unpack_conv(wg_l)
        w_mix = w_s[l] * tril[None]
        small = dict(
            conv_k=conv_k,
            norm_g=norm_g[l].reshape(1, D_MODEL), ln_g=ln_v_g[l].reshape(1, WIDTH_A), ln_b=ln_v_b[l].reshape(1, WIDTH_A),
            w_mix=w_mix.astype(BF16), w_mix_t=jnp.swapaxes(w_mix, 1, 2).astype(BF16),
            b_mix=jnp.broadcast_to(b_s[l][:, :, None], (HEADS_A, CHUNK, HEAD_DIM)),
            ple_g=ple_norm_g[l].reshape(1, D_MODEL),
        )
        return dict({k: pltpu.with_memory_space_constraint(a, pltpu.HBM) for k, a in small.items()}, wg=wg_l)

    layer_consts = [consts(0, _all_gather_rows(shards[0]))]
    saved = []
    h = xs
    for l in range(DEPTH):
        k = layer_consts[l]
        outs = _forward_layer(
            l, h, p, k["wg"], k["conv_k"], k["norm_g"], k["ln_g"], k["ln_b"], k["w_mix"], k["b_mix"],
            k["ple_g"], next_shard=shards[l + 1] if l + 1 < DEPTH else None)
        proj, hn, cat, r, gpre, x1, x2 = outs[:7]
        if l + 1 < DEPTH:
            layer_consts.append(consts(l + 1, outs[7]))
        saved.append(dict(x_in=h, proj=proj, hn=hn, cat=cat, r=r, gpre=gpre, x1=x1))
        h = x2

    smalls, dws = [None] * DEPTH, [None] * DEPTH
    reduced = [None] * DEPTH
    pending = None
    dx = h
    for l in reversed(range(DEPTH)):
        k, s = layer_consts[l], saved[l]
        outs = _backward_layer(
            l, dx, s["x_in"], s["x1"], s["proj"], s["gpre"], p, k["wg"], k["conv_k"],
            k["norm_g"], k["ln_g"], k["ln_b"], k["w_mix"], k["w_mix_t"], k["b_mix"], k["ple_g"],
            loss_head=(target, final_g.reshape(1, D_MODEL)) if l == DEPTH - 1 else None)
        dx, dproj, dx1, dgpre, dpp, smalls[l], dws[l] = outs[:7]
        if l == DEPTH - 1:
            head = outs[7]
        outs = _weight_grads(l, dproj, s["hn"], s["cat"], dx1, s["r"], dgpre, dpp, p, scatter_pack=pending)
        if pending is not None:
            reduced[l + 1] = _sum_pieces(l + 1, outs[1])
        pending = outs[0]
    reduced[0], total = _reduce_scatter_all_reduce(0, pending, smalls, head, dws)
    grad_x = dx[None]
    loss = total[TOTAL_HEAD + HEAD_LOSS, 0]

    n_ch = WIDTH_B // N_DEV
    g_conv = jnp.stack([total[l * SMALL_ROWS + SMALL_CONV:l * SMALL_ROWS + SMALL_CONV + 3, 0:WIDTH_B] for l in range(DEPTH)], axis=1)
    g_conv = lax.dynamic_slice_in_dim(g_conv, me * n_ch, n_ch, axis=2)
    g_proj = jnp.stack([_unpack_grad_proj(reduced[l]) for l in range(DEPTH)])

    def t_in(a):
        return jnp.swapaxes(a, 1, 2)

    def t_conv(a):
        return jnp.transpose(a, (2, 0, 1))

    (r_in,) = _adamw_rows("adamw_w_in", reduced, [OFF_IN], [(t_in(w_in), t_in(m_w_in), t_in(v_w_in))])
    r_out, r_gate = _adamw_rows(
        "adamw_w_out_gate", reduced, [OFF_OUT, OFF_GATE],
        [(w_out, m_w_out, v_w_out), (w_ple_gate, m_w_ple_gate, v_w_ple_gate)])
    small = _adamw_small(total, g_conv, g_proj, dict(
        norm_g=(norm_g, m_norm_g, v_norm_g), ple_norm_g=(ple_norm_g, m_ple_norm_g, v_ple_norm_g),
        ln_v_g=(ln_v_g, m_ln_v_g, v_ln_v_g), ln_v_b=(ln_v_b, m_ln_v_b, v_ln_v_b),
        b_s=(b_s, m_b_s, v_b_s), w_s=(w_s, m_w_s, v_w_s),
        final_g=tuple(a.reshape(1, D_MODEL) for a in (final_g, m_final_g, v_final_g)),
        conv_w=(t_conv(conv_w), t_conv(m_conv_w), t_conv(v_conv_w)),
        w_ple_proj=(w_ple_proj, m_w_ple_proj, v_w_ple_proj),
    ))
    res = dict(small, w_in=tuple(t_in(a) for a in r_in), w_out=r_out, w_ple_gate=r_gate)
    res["final_g"] = tuple(a.reshape(D_MODEL) for a in res["final_g"])
    res["conv_w"] = tuple(jnp.transpose(a, (1, 2, 0)) for a in res["conv_w"])
    order = ["norm_g", "w_in", "ln_v_g", "ln_v_b", "w_s", "b_s", "conv_w", "w_out", "ple_norm_g", "w_ple_gate", "w_ple_proj", "final_g"]
    return (loss, grad_x, *[res[n][0] for n in order], *[res[n][1] for n in order],
            *[res[n][2] for n in order], *[res[n][3] for n in order])
```

```python
import jax
import jax.numpy as jnp
from jax import lax
from jax.experimental import pallas as pl
from jax.experimental.pallas import tpu as pltpu

F32 = jnp.float32
BF16 = jnp.bfloat16

D_MODEL = 1024
WIDTH_A = 512
WIDTH_B = 512
HEADS_A = 4
HEAD_DIM = 128
CHUNK = 128
PLE_DIM = 256
PROJ_WIDTH = 3584
DEPTH = 2
EPS = 1e-6
N_DEV = 8

ADAM_LR = 0.001
ADAM_B1 = 0.9
ADAM_B2 = 0.999
ADAM_EPS = 1e-08
ADAM_WD = 0.01
ADAM_STEP = 10

ROWS_IN = PROJ_WIDTH // N_DEV
ROWS_OUT = D_MODEL // N_DEV
ROWS_GATE = D_MODEL // N_DEV
ROWS_PROJ = (D_MODEL // N_DEV) * PLE_DIM // D_MODEL
ROWS_CONV = 16
OFF_IN = 0
OFF_OUT = OFF_IN + ROWS_IN
OFF_GATE = OFF_OUT + ROWS_OUT
OFF_PROJ = OFF_GATE + ROWS_GATE
OFF_CONV = OFF_PROJ + ROWS_PROJ
ROWS_GRAD = OFF_CONV
ROWS_LAYER = OFF_CONV + ROWS_CONV

SMALL_ROWS = 8
SMALL_NORM = 0
SMALL_PLE = 1
SMALL_LN = 2
SMALL_BS = 3
SMALL_CONV = 4
HEAD_FINAL = 0
HEAD_LOSS = 1
TOTAL_HEAD = DEPTH * SMALL_ROWS
TOTAL_WS = TOTAL_HEAD + SMALL_ROWS
TOTAL_ROWS = TOTAL_WS + CHUNK

MIB = 1024 * 1024
MESH = pl.DeviceIdType.MESH

NT_DIMS = (((1,), (1,)), ((), ()))
TN_DIMS = (((0,), (0,)), ((), ()))


def _dot(a, b):
    return jnp.dot(a, b, preferred_element_type=F32)


def _dot_nt(a, b):
    return lax.dot_general(a, b, NT_DIMS, preferred_element_type=F32)


def _dot_tn(a, b):
    return lax.dot_general(a, b, TN_DIMS, preferred_element_type=F32)


def _colsum8(a):
    rows, n = a.shape
    return jnp.sum(a.reshape(rows // 8, 8, n), axis=0)


def _sigmoid(z):
    return 1.0 / (1.0 + jnp.exp(-z))


def _tile(t, want):
    return want if t % want == 0 else t


class _TwoLevelGather:
    def __init__(self, x_ref, out_ref, m_per, send_sems, recv_sems, local_sem):
        x, y, c = lax.axis_index("x"), lax.axis_index("y"), lax.axis_index("c")
        self.me, self.sibling = (x, y, c), (x, y, 1 - c)
        self.xn, self.yn, self.diag = (1 - x, y, c), (x, 1 - y, c), (1 - x, 1 - y, c)
        self.x_ref, self.out_ref, self.m_per = x_ref, out_ref, m_per
        self.half = (m_per // 32) * 16
        self.send_sems, self.recv_sems = send_sems, recv_sems
        self.mine = pltpu.make_async_copy(x_ref, self.rows(self.me), local_sem)

    def rows(self, block, part=None):
        px, py, pc = block
        base = (4 * px + 2 * py + pc) * self.m_per
        if part is None:
            return self.out_ref.at[pl.ds(base, self.m_per), :]
        if part == 0:
            return self.out_ref.at[pl.ds(base, self.half), :]
        return self.out_ref.at[pl.ds(base + self.half, self.m_per - self.half), :]

    def copy(self, k, block, to, src=None, part=None):
        return pltpu.make_async_remote_copy(
            src_ref=self.rows(block, part) if src is None else src,
            dst_ref=self.rows(block, part),
            send_sem=self.send_sems.at[k],
            recv_sem=self.recv_sems.at[k],
            device_id=to,
            device_id_type=MESH,
        )

    def first(self):
        return [self.copy(0, self.me, self.sibling, src=self.x_ref),
                self.copy(1, self.me, self.xn, src=self.x_ref),
                self.copy(2, self.me, self.yn, src=self.x_ref)]

    def second(self):
        return [self.copy(3, self.xn, self.yn, part=0), self.copy(7, self.yn, self.xn, part=1),
                self.copy(4, self.xn, self.sibling), self.copy(5, self.yn, self.sibling)]

    def third(self):
        return [self.copy(6, self.diag, self.sibling)]

    def start(self):
        self.mine.start()
        for cp in self.first():
            cp.start()

    def pass_on(self):
        fwd_x, fwd_y, sib_x, sib_y = self.second()
        self.copy(1, self.xn, self.me).wait_recv()
        fwd_x.start()
        sib_x.start()
        self.copy(2, self.yn, self.me).wait_recv()
        fwd_y.start()
        sib_y.start()

    def pass_on_diagonal(self):
        self.copy(3, self.diag, self.me, part=0).wait_recv()
        self.copy(7, self.diag, self.me, part=1).wait_recv()
        self.third()[0].start()

    def finish(self):
        sib = (self.sibling[0], self.sibling[1], self.sibling[2])
        self.copy(0, sib, self.me).wait_recv()
        for k, chip in ((4, self.xn), (5, self.yn), (6, self.diag)):
            self.copy(k, (chip[0], chip[1], sib[2]), self.me).wait_recv()
        for cp in self.first() + self.second() + self.third():
            cp.wait_send()
        self.mine.wait()


GATHER_SEMS = [pltpu.SemaphoreType.DMA((8,)), pltpu.SemaphoreType.DMA((8,)), pltpu.SemaphoreType.DMA]


def _all_gather_rows(shard):
    m_per, n = shard.shape

    def body(x_ref, out_ref, send_sems, recv_sems, local_sem):
        ag = _TwoLevelGather(x_ref, out_ref, m_per, send_sems, recv_sems, local_sem)
        ag.start()
        ag.pass_on()
        ag.pass_on_diagonal()
        ag.finish()

    return pl.pallas_call(
        body,
        name="weights_all_gather",
        out_shape=pltpu.HBM((N_DEV * m_per, n), shard.dtype),
        in_specs=[pl.BlockSpec(memory_space=pltpu.HBM)],
        out_specs=pl.BlockSpec(memory_space=pltpu.HBM),
        scratch_shapes=list(GATHER_SEMS),
    )(pltpu.with_memory_space_constraint(shard, pltpu.HBM))


PROJ_PARTS = D_MODEL // PLE_DIM
N_WEIGHT_COPIES = N_DEV * (3 + PROJ_PARTS)


def _weight_copies(wg_ref, w_in_t, w_out, w_gate, w_proj_t, sems):
    copies = []
    for s in range(N_DEV):
        base = s * ROWS_LAYER
        for dst, off, rows in ((w_in_t, OFF_IN, ROWS_IN), (w_out, OFF_OUT, ROWS_OUT), (w_gate, OFF_GATE, ROWS_GATE)):
            copies.append((wg_ref.at[pl.ds(base + off, rows), :], dst.at[pl.ds(s * rows, rows), :]))
        for j in range(PROJ_PARTS):
            copies.append((
                wg_ref.at[pl.ds(base + OFF_PROJ, ROWS_PROJ), pl.ds(j * PLE_DIM, PLE_DIM)],
                w_proj_t.at[pl.ds(s * ROWS_OUT + j * ROWS_PROJ, ROWS_PROJ), :],
            ))
    return [pltpu.make_async_copy(src, dst, sems.at[k]) for k, (src, dst) in enumerate(copies)]


def _forward_layer(layer, x, p_all, wg, conv_k, norm_g, ln_g, ln_b, w_mix, b_mix, ple_g, next_shard=None):
    t = x.shape[0]
    tm = _tile(t, 512)
    nt = t // tm
    gathers = next_shard is not None

    def body(*refs):
        (x_ref, p_ref, wg_ref, cw_ref, ng_ref, lng_ref, lnb_ref, wm_ref, bm_ref, pg_ref) = refs[:10]
        refs = refs[10:]
        if gathers:
            shard_ref, refs = refs[0], refs[1:]
        (proj_ref, hn_ref, cat_ref, r_ref, gpre_ref, x1_ref, x2_ref) = refs[:7]
        refs = refs[7:]
        if gathers:
            gathered_ref, refs = refs[0], refs[1:]
        (w_in_t, w_out, w_gate, wpt_ref, vln_s, mixed_s, halo_s, sems) = refs[:8]
        i = pl.program_id(0)
        if gathers:
            ag = _TwoLevelGather(shard_ref, gathered_ref, ROWS_LAYER, *refs[8:11])

            @pl.when(i == 0)
            def _():
                ag.start()

            @pl.when(i == (5 * nt) // 16)
            def _():
                ag.pass_on()

            @pl.when(i == nt // 2)
            def _():
                ag.pass_on_diagonal()

        @pl.when(i == 0)
        def _():
            copies = _weight_copies(wg_ref, w_in_t, w_out, w_gate, wpt_ref, sems)
            for cp in copies:
                cp.start()
            halo_s[...] = jnp.zeros_like(halo_s)
            for cp in copies:
                cp.wait()

        xv = x_ref[...]
        rstd0 = lax.rsqrt(jnp.mean(xv * xv, axis=-1, keepdims=True) + EPS)
        hn_ref[...] = (xv * rstd0 * ng_ref[...]).astype(BF16)

        def proj_section(k):
            sec = _dot_nt(hn_ref[...], w_in_t[pl.ds(k * 512, 512), :])
            proj_ref[:, k * 512:(k + 1) * 512] = sec.astype(BF16)
            return sec

        v = proj_section(1)
        mu = jnp.mean(v, axis=-1, keepdims=True)
        vc = v - mu
        var = jnp.mean(vc * vc, axis=-1, keepdims=True)
        vln = vc * lax.rsqrt(var + EPS) * lng_ref[...] + lnb_ref[...]
        vln_s[...] = vln.astype(BF16)
        for ci in range(tm // CHUNK):
            rows = pl.ds(ci * CHUNK, CHUNK)
            for h in range(HEADS_A):
                cols = pl.ds(h * HEAD_DIM, HEAD_DIM)
                mixed_s[rows, cols] = _dot(wm_ref[h], vln_s[rows, cols]) + bm_ref[h]
        u = proj_section(0)
        za = proj_section(2)
        out_a = u * mixed_s[...] * (za * _sigmoid(za))
        cat_ref[:, 0:512] = out_a.astype(BF16)

        xc = proj_section(5) * proj_section(3)
        prev = halo_s[...]
        row = lax.broadcasted_iota(jnp.int32, (tm, WIDTH_B), 0)
        xc_m1 = jnp.where(row == 0, prev[7:8, :], pltpu.roll(xc, 1, 0))
        xc_m2 = jnp.where(row == 0, prev[6:7, :], jnp.where(row == 1, prev[7:8, :], pltpu.roll(xc, 2, 0)))
        halo_s[...] = xc[tm - 8:tm, :]
        cw = cw_ref[...]
        yc = cw[0:1, :] * xc_m2 + cw[1:2, :] * xc_m1 + cw[2:3, :] * xc
        zb = proj_section(6)
        out_b = proj_section(4) * yc * (zb * _sigmoid(zb))
        cat_ref[:, 512:1024] = out_b.astype(BF16)

        x1 = xv + _dot(cat_ref[...], w_out[...])
        x1_ref[...] = x1
        rstd1 = lax.rsqrt(jnp.mean(x1 * x1, axis=-1, keepdims=True) + EPS)
        r_ref[...] = (x1 * rstd1 * pg_ref[...]).astype(BF16)
        gpre = _dot(r_ref[...], w_gate[...])
        gpre_ref[...] = gpre.astype(BF16)
        pp = _dot_nt(p_ref[...].astype(BF16), wpt_ref[...])
        x2_ref[...] = x1 + _sigmoid(gpre) * pp

        if gathers:
            @pl.when(i == nt - 1)
            def _():
                ag.finish()

    def tok(width):
        return pl.BlockSpec((tm, width), lambda i: (i, 0))

    def whole(shape):
        return pl.BlockSpec(shape, lambda i: (0,) * len(shape))

    hbm = pl.BlockSpec(memory_space=pl.ANY)
    operands = [x, p_all, wg, conv_k, norm_g, ln_g, ln_b, w_mix, b_mix, ple_g]
    in_specs = [
        tok(D_MODEL), pl.BlockSpec((None, None, tm, PLE_DIM), lambda i: (layer, 0, i, 0)), hbm,
        whole((8, WIDTH_B)), whole((1, D_MODEL)), whole((1, WIDTH_A)), whole((1, WIDTH_A)),
        whole((HEADS_A, CHUNK, CHUNK)), whole((HEADS_A, CHUNK, HEAD_DIM)), whole((1, D_MODEL)),
    ]
    out_specs = [tok(PROJ_WIDTH), tok(D_MODEL), tok(D_MODEL), tok(D_MODEL), tok(D_MODEL), tok(D_MODEL), tok(D_MODEL)]
    out_shape = [
        jax.ShapeDtypeStruct((t, PROJ_WIDTH), BF16),
        jax.ShapeDtypeStruct((t, D_MODEL), BF16),
        jax.ShapeDtypeStruct((t, D_MODEL), BF16),
        jax.ShapeDtypeStruct((t, D_MODEL), BF16),
        jax.ShapeDtypeStruct((t, D_MODEL), BF16),
        jax.ShapeDtypeStruct((t, D_MODEL), F32),
        jax.ShapeDtypeStruct((t, D_MODEL), F32),
    ]
    scratch_shapes = [
        pltpu.VMEM((PROJ_WIDTH, D_MODEL), BF16),
        pltpu.VMEM((D_MODEL, D_MODEL), BF16),
        pltpu.VMEM((D_MODEL, D_MODEL), BF16),
        pltpu.VMEM((D_MODEL, PLE_DIM), BF16),
        pltpu.VMEM((tm, WIDTH_A), BF16),
        pltpu.VMEM((tm, WIDTH_A), F32),
        pltpu.VMEM((8, WIDTH_B), F32),
        pltpu.SemaphoreType.DMA((N_WEIGHT_COPIES,)),
    ]
    if gathers:
        operands.append(pltpu.with_memory_space_constraint(next_shard, pltpu.HBM))
        in_specs.append(pl.BlockSpec(memory_space=pltpu.HBM))
        out_specs.append(pl.BlockSpec(memory_space=pltpu.HBM))
        out_shape.append(pltpu.HBM((N_DEV * ROWS_LAYER, D_MODEL), BF16))
        scratch_shapes += list(GATHER_SEMS)

    return pl.pallas_call(
        body,
        name=f"layer{layer}_forward",
        grid=(nt,),
        in_specs=in_specs,
        out_specs=out_specs,
        out_shape=out_shape,
        scratch_shapes=scratch_shapes,
        compiler_params=pltpu.CompilerParams(dimension_semantics=("arbitrary",), vmem_limit_bytes=56 * MIB),
    )(*operands)


class _DirectScatter:
    def __init__(self, pack_ref, pieces_ref, send_sems, recv_sems, local_sem):
        x, y, c = lax.axis_index("x"), lax.axis_index("y"), lax.axis_index("c")
        me = 4 * x + 2 * y + c
        self.copies = []
        for k in range(N_DEV - 1):
            fx, fy, fc = ((k + 1) >> 2) & 1, ((k + 1) >> 1) & 1, (k + 1) & 1
            tx, ty, tc = x ^ fx, y ^ fy, c ^ fc
            self.copies.append(
                pltpu.make_async_remote_copy(
                    src_ref=pack_ref.at[4 * tx + 2 * ty + tc], dst_ref=pieces_ref.at[me],
                    send_sem=send_sems.at[k], recv_sem=recv_sems.at[k],
                    device_id=(tx, ty, tc), device_id_type=MESH,
                )
            )
        self.mine = pltpu.make_async_copy(pack_ref.at[me], pieces_ref.at[me], local_sem)

    def start(self):
        self.mine.start()
        for cp in self.copies:
            cp.start()

    def finish(self):
        for cp in self.copies:
            cp.wait_recv()
        for cp in self.copies:
            cp.wait_send()
        self.mine.wait()


SCATTER_SEMS = [pltpu.SemaphoreType.DMA((N_DEV - 1,)), pltpu.SemaphoreType.DMA((N_DEV - 1,)), pltpu.SemaphoreType.DMA]


def _backward_layer(layer, dx2, x_in, x1, proj, gpre, p_all, wg, conv_k, norm_g, ln_g, ln_b,
                    w_mix, w_mix_t, b_mix, ple_g, loss_head=None):
    t = x_in.shape[0]
    tm = _tile(t, 256)
    nt = t // tm
    n_chunks = tm // CHUNK
    halo_rows = 16
    heads = loss_head is not None

    def body(*refs):
        (dx2_ref, xin_ref, x1_ref, proj_ref, halo_ref, gpre_ref, p_ref, wg_ref, cw_ref,
         ng_ref, lng_ref, lnb_ref, wm_ref, wmt_ref, bm_ref, pg_ref) = refs[:16]
        refs = refs[16:]
        if heads:
            tgt_ref, fg_ref = refs[:2]
            refs = refs[2:]
        (dxin_ref, dproj_ref, dx1_ref, dgpre_ref, dpp_ref, small_ref, dws_ref) = refs[:7]
        refs = refs[7:]
        if heads:
            head_ref, refs = refs[0], refs[1:]
        (w_in_t, w_out, w_gate, wpt_ref, vln_s, mixed_s, dmix_s, dvln_s, carry_s,
         ng_acc, pg_acc, lng_acc, lnb_acc, cw_acc, dbm_ref, sems) = refs[:16]
        if heads:
            loss_acc, fg_acc = refs[16:18]
        i = pl.program_id(0)
        tile = nt - 1 - i

        @pl.when(i == 0)
        def _():
            copies = _weight_copies(wg_ref, w_in_t, w_out, w_gate, wpt_ref, sems)
            for cp in copies:
                cp.start()
            if heads:
                loss_acc[...] = jnp.zeros_like(loss_acc)
                fg_acc[...] = jnp.zeros_like(fg_acc)
            carry_s[...] = jnp.zeros_like(carry_s)
            ng_acc[...] = jnp.zeros_like(ng_acc)
            pg_acc[...] = jnp.zeros_like(pg_acc)
            lng_acc[...] = jnp.zeros_like(lng_acc)
            lnb_acc[...] = jnp.zeros_like(lnb_acc)
            cw_acc[...] = jnp.zeros_like(cw_acc)
            dws_ref[...] = jnp.zeros_like(dws_ref)
            dbm_ref[...] = jnp.zeros_like(dbm_ref)
            for cp in copies:
                cp.wait()

        if heads:
            x2v = dx2_ref[...]
            fg = fg_ref[...]
            rstdf = lax.rsqrt(jnp.mean(x2v * x2v, axis=-1, keepdims=True) + EPS)
            xhatf = x2v * rstdf
            err = xhatf * fg - tgt_ref[...]
            loss_acc[...] += _colsum8(err * err)
            dy = err * (1.0 / D_MODEL)
            fg_acc[...] += _colsum8(dy * xhatf)
            dxhf = dy * fg
            dx2v = rstdf * (dxhf - xhatf * jnp.mean(dxhf * xhatf, axis=-1, keepdims=True))
        else:
            dx2v = dx2_ref[...]

        gate = _sigmoid(gpre_ref[...].astype(F32))
        pp = _dot_nt(p_ref[...].astype(BF16), wpt_ref[...])
        dpp = dx2v * gate
        dpp_ref[...] = dpp.astype(BF16)
        dgpre = (dpp * pp * (1.0 - gate)).astype(BF16)
        dgpre_ref[...] = dgpre
        dr = _dot_nt(dgpre, w_gate[...])
        x1v = x1_ref[...]
        rstd1 = lax.rsqrt(jnp.mean(x1v * x1v, axis=-1, keepdims=True) + EPS)
        xhat1 = x1v * rstd1
        pg_acc[...] += _colsum8(dr * xhat1)
        dxh = dr * pg_ref[...]
        dx1 = dx2v + rstd1 * (dxh - xhat1 * jnp.mean(dxh * xhat1, axis=-1, keepdims=True))
        dx1b = dx1.astype(BF16)
        dx1_ref[...] = dx1b

        dcat = _dot_nt(dx1b, w_out[...])
        dca = dcat[:, 0:512]
        dcb = dcat[:, 512:1024]

        u = proj_ref[:, 0:512]
        v = proj_ref[:, 512:1024].astype(F32)
        za = proj_ref[:, 1024:1536]
        mu = jnp.mean(v, axis=-1, keepdims=True)
        vc = v - mu
        var = jnp.mean(vc * vc, axis=-1, keepdims=True)
        rs = lax.rsqrt(var + EPS)
        vhat = vc * rs
        lng = lng_ref[...]
        vln_s[...] = (vhat * lng + lnb_ref[...]).astype(BF16)
        for ci in range(n_chunks):
            rows = pl.ds(ci * CHUNK, CHUNK)
            for h in range(HEADS_A):
                cols = pl.ds(h * HEAD_DIM, HEAD_DIM)
                mixed_s[rows, cols] = (_dot(wm_ref[h], vln_s[rows, cols]) + bm_ref[h]).astype(BF16)
        mixed = mixed_s[...]
        sga = _sigmoid(za)
        sa = za * sga
        dsa = sga + sa * (1.0 - sga)

        def put_section(k, val):
            dproj_ref[:, k * 512:(k + 1) * 512] = val.astype(BF16)

        dcab = dca.astype(BF16)
        dca_sa = dcab * sa
        put_section(0, dca_sa * mixed)
        dmix_s[...] = dca_sa * u
        put_section(2, (dcab * dsa) * (u * mixed))
        dbm_acc = jnp.zeros((CHUNK, WIDTH_A), F32)
        for ci in range(n_chunks):
            rows = pl.ds(ci * CHUNK, CHUNK)
            dbm_acc = dbm_acc + dmix_s[rows, :].astype(F32)
            for h in range(HEADS_A):
                cols = pl.ds(h * HEAD_DIM, HEAD_DIM)
                dvln_s[rows, cols] = _dot(wmt_ref[h], dmix_s[rows, cols])
                dws_ref[:, cols] += _dot_nt(dmix_s[rows, cols], vln_s[rows, cols])
        dbm_ref[...] += dbm_acc
        dvln = dvln_s[...]
        lng_acc[...] += _colsum8(dvln * vhat)
        lnb_acc[...] += _colsum8(dvln)
        dvh = dvln * lng
        dv = rs * (dvh - jnp.mean(dvh, axis=-1, keepdims=True) - vhat * jnp.mean(dvh * vhat, axis=-1, keepdims=True))
        put_section(1, dv)

        hb = proj_ref[:, 1536:2048].astype(F32)
        gb = proj_ref[:, 2048:2560]
        gc = proj_ref[:, 2560:3072].astype(F32)
        zb = proj_ref[:, 3072:3584]
        xc = gc * hb
        prev = halo_ref[:, 2560:3072].astype(F32) * halo_ref[:, 1536:2048].astype(F32)
        prev = jnp.where(tile > 0, prev, 0.0)
        row = lax.broadcasted_iota(jnp.int32, (tm, WIDTH_B), 0)
        p1 = prev[halo_rows - 1:halo_rows, :]
        p2 = prev[halo_rows - 2:halo_rows - 1, :]
        xc_m1 = jnp.where(row == 0, p1, pltpu.roll(xc, 1, 0))
        xc_m2 = jnp.where(row == 0, p2, jnp.where(row == 1, p1, pltpu.roll(xc, 2, 0)))
        cw = cw_ref[...]
        yc = cw[0:1, :] * xc_m2 + cw[1:2, :] * xc_m1 + cw[2:3, :] * xc
        sgb = _sigmoid(zb)
        sb = zb * sgb
        dsb = sgb + sb * (1.0 - sgb)
        dcbb = dcb.astype(BF16)
        ycb = yc.astype(BF16)
        dcb_sb = dcbb * sb
        put_section(4, dcb_sb * ycb)
        dyc = (dcb_sb * gb).astype(F32)
        put_section(6, (dcbb * dsb) * (gb * ycb))
        nxt = carry_s[...]
        dyc_p1 = jnp.where(row == tm - 1, nxt[0:1, :], pltpu.roll(dyc, tm - 1, 0))
        dyc_p2 = jnp.where(row == tm - 1, nxt[1:2, :], jnp.where(row == tm - 2, nxt[0:1, :], pltpu.roll(dyc, tm - 2, 0)))
        carry_s[...] = dyc[0:8, :]
        dxc = cw[2:3, :] * dyc + cw[1:2, :] * dyc_p1 + cw[0:1, :] * dyc_p2
        cw_acc[0] += _colsum8(dyc * xc_m2)
        cw_acc[1] += _colsum8(dyc * xc_m1)
        cw_acc[2] += _colsum8(dyc * xc)
        put_section(3, dxc * gc)
        put_section(5, dxc * hb)

        dhn = _dot(dproj_ref[...], w_in_t[...])
        xv = xin_ref[...]
        rstd0 = lax.rsqrt(jnp.mean(xv * xv, axis=-1, keepdims=True) + EPS)
        xhat0 = xv * rstd0
        ng_acc[...] += _colsum8(dhn * xhat0)
        dxh0 = dhn * ng_ref[...]
        dxin_ref[...] = dx1 + rstd0 * (dxh0 - xhat0 * jnp.mean(dxh0 * xhat0, axis=-1, keepdims=True))

        @pl.when(i == nt - 1)
        def _():
            small_ref[...] = jnp.zeros_like(small_ref)
            small_ref[SMALL_NORM:SMALL_NORM + 1, :] = jnp.sum(ng_acc[...], axis=0, keepdims=True)
            small_ref[SMALL_PLE:SMALL_PLE + 1, :] = jnp.sum(pg_acc[...], axis=0, keepdims=True)
            small_ref[SMALL_LN:SMALL_LN + 1, 0:WIDTH_A] = jnp.sum(lng_acc[...], axis=0, keepdims=True)
            small_ref[SMALL_LN:SMALL_LN + 1, WIDTH_A:2 * WIDTH_A] = jnp.sum(lnb_acc[...], axis=0, keepdims=True)
            for h in range(HEADS_A):
                cols = pl.ds(h * HEAD_DIM, HEAD_DIM)
                small_ref[SMALL_BS:SMALL_BS + 1, cols] = jnp.sum(jnp.transpose(dbm_ref[:, cols]), axis=0, keepdims=True)
            for k in range(3):
                small_ref[SMALL_CONV + k:SMALL_CONV + k + 1, 0:WIDTH_B] = jnp.sum(cw_acc[k], axis=0, keepdims=True)
            if heads:
                total = jnp.sum(loss_acc[...]) * (0.5 / D_MODEL)
                rows8 = lax.broadcasted_iota(jnp.int32, (SMALL_ROWS, D_MODEL), 0)
                lanes8 = lax.broadcasted_iota(jnp.int32, (SMALL_ROWS, D_MODEL), 1)
                head_ref[...] = jnp.where((rows8 == HEAD_LOSS) & (lanes8 == 0), total, 0.0)
                head_ref[HEAD_FINAL:HEAD_FINAL + 1, :] = jnp.sum(fg_acc[...], axis=0, keepdims=True)

    def tok(width):
        return pl.BlockSpec((tm, width), lambda i: (nt - 1 - i, 0))

    def whole(shape):
        return pl.BlockSpec(shape, lambda i: (0,) * len(shape))

    halo_spec = pl.BlockSpec(
        (halo_rows, PROJ_WIDTH), lambda i: (jnp.maximum((nt - 1 - i) * (tm // halo_rows) - 1, 0), 0)
    )
    hbm = pl.BlockSpec(memory_space=pl.ANY)
    operands = [dx2, x_in, x1, proj, proj, gpre, p_all, wg, conv_k, norm_g, ln_g, ln_b, w_mix, w_mix_t, b_mix, ple_g]
    in_specs = [
        tok(D_MODEL), tok(D_MODEL), tok(D_MODEL), tok(PROJ_WIDTH), halo_spec, tok(D_MODEL),
        pl.BlockSpec((None, None, tm, PLE_DIM), lambda i: (layer, 0, nt - 1 - i, 0)), hbm,
        whole((8, WIDTH_B)), whole((1, D_MODEL)), whole((1, WIDTH_A)), whole((1, WIDTH_A)),
        whole((HEADS_A, CHUNK, CHUNK)), whole((HEADS_A, CHUNK, CHUNK)), whole((HEADS_A, CHUNK, HEAD_DIM)),
        whole((1, D_MODEL)),
    ]
    out_specs = [
        tok(D_MODEL), tok(PROJ_WIDTH), tok(D_MODEL), tok(D_MODEL), tok(D_MODEL),
        whole((SMALL_ROWS, D_MODEL)), whole((CHUNK, WIDTH_A)),
    ]
    out_shape = [
        jax.ShapeDtypeStruct((t, D_MODEL), F32),
        jax.ShapeDtypeStruct((t, PROJ_WIDTH), BF16),
        jax.ShapeDtypeStruct((t, D_MODEL), BF16),
        jax.ShapeDtypeStruct((t, D_MODEL), BF16),
        jax.ShapeDtypeStruct((t, D_MODEL), BF16),
        jax.ShapeDtypeStruct((SMALL_ROWS, D_MODEL), F32),
        jax.ShapeDtypeStruct((CHUNK, WIDTH_A), F32),
    ]
    scratch_shapes = [
        pltpu.VMEM((PROJ_WIDTH, D_MODEL), BF16),
        pltpu.VMEM((D_MODEL, D_MODEL), BF16),
        pltpu.VMEM((D_MODEL, D_MODEL), BF16),
        pltpu.VMEM((D_MODEL, PLE_DIM), BF16),
        pltpu.VMEM((tm, WIDTH_A), BF16),
        pltpu.VMEM((tm, WIDTH_A), BF16),
        pltpu.VMEM((tm, WIDTH_A), BF16),
        pltpu.VMEM((tm, WIDTH_A), F32),
        pltpu.VMEM((8, WIDTH_B), F32),
        pltpu.VMEM((8, D_MODEL), F32),
        pltpu.VMEM((8, D_MODEL), F32),
        pltpu.VMEM((8, WIDTH_A), F32),
        pltpu.VMEM((8, WIDTH_A), F32),
        pltpu.VMEM((3, 8, WIDTH_B), F32),
        pltpu.VMEM((CHUNK, WIDTH_A), F32),
        pltpu.SemaphoreType.DMA((N_WEIGHT_COPIES,)),
    ]
    if heads:
        operands += list(loss_head)
        in_specs += [tok(D_MODEL), whole((1, D_MODEL))]
        out_specs.append(whole((SMALL_ROWS, D_MODEL)))
        out_shape.append(jax.ShapeDtypeStruct((SMALL_ROWS, D_MODEL), F32))
        scratch_shapes += [pltpu.VMEM((8, D_MODEL), F32), pltpu.VMEM((8, D_MODEL), F32)]

    return pl.pallas_call(
        body,
        name=f"layer{layer}_backward",
        grid=(nt,),
        in_specs=in_specs,
        out_specs=out_specs,
        out_shape=out_shape,
        scratch_shapes=scratch_shapes,
        compiler_params=pltpu.CompilerParams(dimension_semantics=("arbitrary",), vmem_limit_bytes=56 * MIB),
    )(*operands)


def _scatter_targets():
    x, y, c = lax.axis_index("x"), lax.axis_index("y"), lax.axis_index("c")
    out = []
    for k in range(N_DEV - 1):
        fx, fy, fc = ((k + 1) >> 2) & 1, ((k + 1) >> 1) & 1, (k + 1) & 1
        tx, ty, tc = x ^ fx, y ^ fy, c ^ fc
        out.append((4 * tx + 2 * ty + tc, (tx, ty, tc)))
    return 4 * x + 2 * y + c, out


def _scatter_start(layer, pack):
    n = N_DEV - 1

    def body(pack_ref, land_ref, *rest):
        sems = rest[:2 * n]
        me, targets = _scatter_targets()
        for k, (block, device) in enumerate(targets):
            pltpu.make_async_remote_copy(
                src_ref=pack_ref.at[block], dst_ref=land_ref.at[me], send_sem=sems[k], recv_sem=sems[n + k],
                device_id=device, device_id_type=MESH,
            ).start()

    hbm = pl.BlockSpec(memory_space=pltpu.HBM)
    sem = pl.BlockSpec(memory_space=pltpu.SEMAPHORE)
    outs = pl.pallas_call(
        body,
        name=f"layer{layer}_scatter_start",
        out_shape=(*[pltpu.SemaphoreType.DMA(())] * (2 * n), pltpu.HBM(pack.shape, pack.dtype), pltpu.HBM(pack.shape, pack.dtype)),
        in_specs=(hbm, hbm),
        out_specs=(*[sem] * (2 * n), hbm, hbm),
        input_output_aliases={0: 2 * n, 1: 2 * n + 1},
        compiler_params=pltpu.CompilerParams(has_side_effects=pltpu.SideEffectType.DATAFLOW_SIDE_EFFECTING),
    )(pltpu.with_memory_space_constraint(pack, pltpu.HBM),
      pltpu.with_memory_space_constraint(lax.empty(pack.shape, pack.dtype), pltpu.HBM))
    return outs[:2 * n], outs[2 * n], outs[2 * n + 1]


def _scatter_wait(layer, sems, pack_thru, land_thru, after):
    n = N_DEV - 1

    def body(pack_ref, land_ref, *rest):
        sem_refs = rest[:2 * n]
        me, targets = _scatter_targets()
        for k, (block, device) in enumerate(targets):
            copy = pltpu.make_async_remote_copy(
                src_ref=pack_ref.at[block], dst_ref=land_ref.at[me], send_sem=sem_refs[k], recv_sem=sem_refs[n + k],
                device_id=device, device_id_type=MESH,
            )
            copy.wait_send()
            copy.wait_recv()

    hbm = pl.BlockSpec(memory_space=pltpu.HBM)
    sem = pl.BlockSpec(memory_space=pltpu.SEMAPHORE)
    return pl.pallas_call(
        body,
        name=f"layer{layer}_scatter_wait",
        out_shape=(pltpu.HBM(pack_thru.shape, pack_thru.dtype), pltpu.HBM(pack_thru.shape, pack_thru.dtype)),
        in_specs=(hbm, hbm, *[sem] * (2 * n), pl.BlockSpec(memory_space=pl.ANY)),
        out_specs=(hbm, hbm),
        input_output_aliases={0: 0, 1: 1},
        compiler_params=pltpu.CompilerParams(has_side_effects=pltpu.SideEffectType.DATAFLOW_SIDE_EFFECTING),
    )(pack_thru, land_thru, *sems, after)[1]


def _sum_pieces(layer, pieces):
    rows, n = pieces.shape[1], pieces.shape[2]
    blocks = 2
    rb = rows // blocks

    def body(p_ref, out_ref):
        total = p_ref[0].astype(F32)
        for j in range(1, N_DEV):
            total = total + p_ref[j].astype(F32)
        out_ref[...] = total

    return pl.pallas_call(
        body,
        name=f"layer{layer}_grad_sum",
        grid=(blocks,),
        out_shape=pltpu.HBM((rows, n), F32),
        in_specs=[pl.BlockSpec((N_DEV, rb, n), lambda i: (0, i, 0))],
        out_specs=pl.BlockSpec((rb, n), lambda i: (i, 0)),
        compiler_params=pltpu.CompilerParams(dimension_semantics=("arbitrary",), vmem_limit_bytes=32 * MIB),
    )(pieces)


def _weight_grads(layer, dproj, hn, cat, dx1, r, dgpre, dpp, p_all, scatter_pack=None):
    t = hn.shape[0]
    tk = _tile(t, 512)
    nt = t // tk
    in_blocks = PROJ_WIDTH // 512
    scatters = scatter_pack is not None

    def body(*refs):
        (dproj_ref, hn_ref, cat_ref, dx1_ref, r_ref, dgpre_ref, dpp_ref, p_ref) = refs[:8]
        refs = refs[8:]
        if scatters:
            prior_ref, refs = refs[0], refs[1:]
        pack_ref, refs = refs[0], refs[1:]
        if scatters:
            pieces_ref, refs = refs[0], refs[1:]
        (acc_in, acc_out, acc_gate, acc_proj, stage, sems) = refs[:6]
        i = pl.program_id(0)
        if scatters:
            scatter = _DirectScatter(prior_ref, pieces_ref, *refs[6:9])

            @pl.when(i == 0)
            def _():
                scatter.start()

        @pl.when(i == 0)
        def _():
            acc_in[...] = jnp.zeros_like(acc_in)
            acc_out[...] = jnp.zeros_like(acc_out)
            acc_gate[...] = jnp.zeros_like(acc_gate)
            acc_proj[...] = jnp.zeros_like(acc_proj)

        hnv = hn_ref[...]
        for b in range(in_blocks):
            acc_in[pl.ds(b * 512, 512), :] += _dot_tn(dproj_ref[:, b * 512:(b + 1) * 512], hnv)
        dx1v = dx1_ref[...]
        dgv = dgpre_ref[...]
        for b in range(D_MODEL // 512):
            acc_out[pl.ds(b * 512, 512), :] += _dot_tn(cat_ref[:, b * 512:(b + 1) * 512], dx1v)
            acc_gate[pl.ds(b * 512, 512), :] += _dot_tn(r_ref[:, b * 512:(b + 1) * 512], dgv)
        pv = p_ref[...].astype(BF16)
        for b in range(D_MODEL // 512):
            acc_proj[pl.ds(b * 512, 512), :] += _dot_tn(dpp_ref[:, b * 512:(b + 1) * 512], pv)

        @pl.when(i == nt - 1)
        def _():
            def out_copy(s):
                return pltpu.make_async_copy(stage.at[s % 2], pack_ref.at[s], sems.at[s % 2])

            for s in range(N_DEV):
                if s >= 2:
                    out_copy(s - 2).wait()
                buf = stage.at[s % 2]
                buf[pl.ds(OFF_IN, ROWS_IN), :] = acc_in[pl.ds(s * ROWS_IN, ROWS_IN), :].astype(BF16)
                buf[pl.ds(OFF_OUT, ROWS_OUT), :] = acc_out[pl.ds(s * ROWS_OUT, ROWS_OUT), :].astype(BF16)
                buf[pl.ds(OFF_GATE, ROWS_GATE), :] = acc_gate[pl.ds(s * ROWS_GATE, ROWS_GATE), :].astype(BF16)
                for j in range(D_MODEL // PLE_DIM):
                    buf[pl.ds(OFF_PROJ, ROWS_PROJ), pl.ds(j * PLE_DIM, PLE_DIM)] = acc_proj[
                        pl.ds(s * ROWS_OUT + j * ROWS_PROJ, ROWS_PROJ), :
                    ].astype(BF16)
                out_copy(s).start()
            out_copy(N_DEV - 2).wait()
            out_copy(N_DEV - 1).wait()
            if scatters:
                scatter.finish()

    def tok(width):
        return pl.BlockSpec((tk, width), lambda i: (i, 0))

    hbm = pl.BlockSpec(memory_space=pl.ANY)
    pack_shape = jax.ShapeDtypeStruct((N_DEV, ROWS_GRAD, D_MODEL), BF16)
    operands = [dproj, hn, cat, dx1, r, dgpre, dpp, p_all]
    in_specs = [tok(PROJ_WIDTH), tok(D_MODEL), tok(D_MODEL), tok(D_MODEL), tok(D_MODEL), tok(D_MODEL), tok(D_MODEL),
                pl.BlockSpec((None, None, tk, PLE_DIM), lambda i: (layer, 0, i, 0))]
    out_specs, out_shape = [hbm], [pack_shape]
    scratch_shapes = [
        pltpu.VMEM((PROJ_WIDTH, D_MODEL), F32),
        pltpu.VMEM((D_MODEL, D_MODEL), F32),
        pltpu.VMEM((D_MODEL, D_MODEL), F32),
        pltpu.VMEM((D_MODEL, PLE_DIM), F32),
        pltpu.VMEM((2, ROWS_GRAD, D_MODEL), BF16),
        pltpu.SemaphoreType.DMA((2,)),
    ]
    if scatters:
        operands.append(scatter_pack)
        in_specs.append(hbm)
        out_specs.append(hbm)
        out_shape.append(pack_shape)
        scratch_shapes += list(SCATTER_SEMS)

    return pl.pallas_call(
        body,
        name=f"layer{layer}_weight_grads",
        grid=(nt,),
        in_specs=in_specs,
        out_specs=out_specs,
        out_shape=out_shape,
        scratch_shapes=scratch_shapes,
        compiler_params=pltpu.CompilerParams(dimension_semantics=("arbitrary",), vmem_limit_bytes=58 * MIB),
    )(*operands)


def _reduce_scatter_all_reduce(layer, pack, smalls, head, dws):
    rows, n = pack.shape[1], pack.shape[2]
    assert DEPTH * WIDTH_A == D_MODEL and n == D_MODEL

    def body(g_ref, *refs):
        small_refs, refs = refs[:DEPTH], refs[DEPTH:]
        head_ref, refs = refs[0], refs[1:]
        dws_refs, refs = refs[:DEPTH], refs[DEPTH:]
        (out_ref, total_ref, r1, a_s, r2, via, sp, sr1, sq, send1, recv1, send2, recv2, ssend, srecv) = refs
        x, y, c = lax.axis_index("x"), lax.axis_index("y"), lax.axis_index("c")
        sibling = (x, y, 1 - c)
        chip = 2 * x + y
        flips = [(1, 0), (0, 1), (1, 1)]

        for l in range(DEPTH):
            sp[l * SMALL_ROWS:(l + 1) * SMALL_ROWS, :] = small_refs[l][...]
            sp[TOTAL_WS:TOTAL_ROWS, l * WIDTH_A:(l + 1) * WIDTH_A] = dws_refs[l][...]
        sp[TOTAL_HEAD:TOTAL_WS, :] = head_ref[...]

        small_pair = pltpu.make_async_remote_copy(
            src_ref=sp, dst_ref=sr1, send_sem=ssend.at[0], recv_sem=srecv.at[0], device_id=sibling, device_id_type=MESH
        )

        def to_sibling(j):
            return pltpu.make_async_remote_copy(
                src_ref=g_ref.at[2 * j + 1 - c], dst_ref=r1.at[j], send_sem=send1.at[j], recv_sem=recv1.at[j],
                device_id=sibling, device_id_type=MESH,
            )

        first = [to_sibling(j) for j in range(4)]
        small_pair.start()
        for cp in first:
            cp.start()

        small_pair.wait_recv()
        sq[chip] = sp[...] + sr1[...]
        small_chips = [
            pltpu.make_async_remote_copy(
                src_ref=sq.at[chip], dst_ref=sq.at[chip], send_sem=ssend.at[1 + k], recv_sem=srecv.at[1 + k],
                device_id=(x ^ fx, y ^ fy, c), device_id_type=MESH,
            )
            for k, (fx, fy) in enumerate(flips)
        ]
        for cp in small_chips:
            cp.start()

        for j in range(4):
            first[j].wait_recv()

            @pl.when(chip != j)
            def _():
                a_s[j] = (g_ref[2 * j + c].astype(F32) + r1[j].astype(F32)).astype(BF16)

        half = rows // 2
        lo, hi = pl.ds(0, half), pl.ds(half, rows - half)
        x_nbr, y_nbr = (1 - x, y, c), (x, 1 - y, c)
        chip_x, chip_y, chip_d = 2 * (1 - x) + y, 2 * x + (1 - y), 2 * (1 - x) + (1 - y)

        def ici(k, src, dst, to):
            return pltpu.make_async_remote_copy(
                src_ref=src, dst_ref=dst, send_sem=send2.at[k], recv_sem=recv2.at[k], device_id=to, device_id_type=MESH)

        second = [
            ici(0, a_s.at[chip_d, lo, :], via.at[0], x_nbr),
            ici(1, a_s.at[chip_d, hi, :], via.at[1], y_nbr),
            ici(2, a_s.at[chip_x, lo, :], r2.at[0, lo, :], x_nbr),
            ici(3, a_s.at[chip_y, hi, :], r2.at[1, hi, :], y_nbr),
            ici(4, a_s.at[chip_x, hi, :], r2.at[0, hi, :], x_nbr),
            ici(5, a_s.at[chip_y, lo, :], r2.at[1, lo, :], y_nbr),
        ]
        for cp in second[:4]:
            cp.start()

        out_ref[...] = g_ref[2 * chip + c].astype(F32) + r1[chip].astype(F32)
        for cp in small_chips:
            cp.wait_recv()
        total_ref[...] = ((sq[0] + sq[1]) + sq[2]) + sq[3]

        second[0].wait_recv()
        a_s[chip_y, lo, :] = (a_s[chip_y, lo, :].astype(F32) + via[0].astype(F32)).astype(BF16)
        second[5].start()
        second[1].wait_recv()
        a_s[chip_x, hi, :] = (a_s[chip_x, hi, :].astype(F32) + via[1].astype(F32)).astype(BF16)
        second[4].start()

        second[2].wait_recv()
        second[4].wait_recv()
        out_ref[...] += r2[0].astype(F32)
        second[3].wait_recv()
        second[5].wait_recv()
        out_ref[...] += r2[1].astype(F32)
        small_pair.wait_send()
        for cp in first + small_chips + second:
            cp.wait_send()

    vmem = pl.BlockSpec(memory_space=pltpu.VMEM)
    return pl.pallas_call(
        body,
        name=f"layer{layer}_grad_reduce_scatter",
        out_shape=[jax.ShapeDtypeStruct((rows, n), F32), jax.ShapeDtypeStruct((TOTAL_ROWS, D_MODEL), F32)],
        in_specs=[vmem] * (2 + 2 * DEPTH),
        out_specs=[vmem, vmem],
        scratch_shapes=[
            pltpu.VMEM((4, rows, n), BF16),
            pltpu.VMEM((4, rows, n), BF16),
            pltpu.VMEM((2, rows, n), BF16),
            pltpu.VMEM((2, rows // 2, n), BF16),
            pltpu.VMEM((TOTAL_ROWS, D_MODEL), F32),
            pltpu.VMEM((TOTAL_ROWS, D_MODEL), F32),
            pltpu.VMEM((4, TOTAL_ROWS, D_MODEL), F32),
            pltpu.SemaphoreType.DMA((4,)),
            pltpu.SemaphoreType.DMA((4,)),
            pltpu.SemaphoreType.DMA((6,)),
            pltpu.SemaphoreType.DMA((6,)),
            pltpu.SemaphoreType.DMA((4,)),
            pltpu.SemaphoreType.DMA((4,)),
        ],
        compiler_params=pltpu.CompilerParams(vmem_limit_bytes=48 * MIB),
    )(pack, *smalls, head, *dws)


def _adam_step(w, g, m, v):
    m = ADAM_B1 * m + (1.0 - ADAM_B1) * g
    v = ADAM_B2 * v + (1.0 - ADAM_B2) * (g * g)
    m_hat = m / (1.0 - ADAM_B1 ** ADAM_STEP)
    v_hat = v / (1.0 - ADAM_B2 ** ADAM_STEP)
    return -ADAM_LR * (m_hat / (jnp.sqrt(v_hat) + ADAM_EPS) + ADAM_WD * w), m, v


def _adamw_rows(name, reduced, row_off, states):
    n = len(states)

    def body(*refs):
        red = refs[:DEPTH]
        ins = refs[DEPTH:DEPTH + 3 * n]
        outs = refs[DEPTH + 3 * n:]
        layer = pl.program_id(0)
        for l in range(DEPTH):
            @pl.when(layer == l)
            def _():
                for k in range(n):
                    w_ref, m_ref, v_ref = ins[3 * k:3 * k + 3]
                    g_ref, d_ref, nm_ref, nv_ref = outs[4 * k:4 * k + 4]
                    g = red[l][row_off[k]:row_off[k] + w_ref.shape[0], :]
                    d, m, v = _adam_step(w_ref[...], g, m_ref[...], v_ref[...])
                    g_ref[...] = g
                    d_ref[...] = d
                    nm_ref[...] = m
                    nv_ref[...] = v

    flat = [a for st in states for a in st]
    state_specs, out_specs, out_shape = [], [], []
    for w, _, _ in states:
        spec = pl.BlockSpec((None,) + w.shape[1:], lambda l: (l, 0, 0))
        state_specs += [spec] * 3
        out_specs += [spec] * 4
        out_shape += [jax.ShapeDtypeStruct(w.shape, F32)] * 4
    red_specs = [pl.BlockSpec(a.shape, lambda l: (0, 0)) for a in reduced]
    operands = [pltpu.with_memory_space_constraint(a, pltpu.HBM) for a in (*reduced, *flat)]
    outs = pl.pallas_call(
        body,
        name=name,
        grid=(DEPTH,),
        out_shape=[pltpu.HBM(a.shape, a.dtype) for a in out_shape],
        in_specs=red_specs + state_specs,
        out_specs=out_specs,
        compiler_params=pltpu.CompilerParams(dimension_semantics=("arbitrary",), vmem_limit_bytes=48 * MIB),
    )(*operands)
    return [tuple(outs[4 * k:4 * k + 4]) for k in range(n)]


def _adamw_small(total, g_conv, g_proj, st):
    names = ["norm_g", "ple_norm_g", "ln_v_g", "ln_v_b", "b_s", "w_s", "final_g", "conv_w", "w_ple_proj"]
    cut = names[:7]

    def body(total_ref, gconv_ref, gproj_ref, *refs):
        ins = {nm: refs[3 * k:3 * k + 3] for k, nm in enumerate(names)}
        outs, pos = {}, 3 * len(names)
        for nm in names:
            cnt = 4 if nm in cut else 3
            outs[nm] = refs[pos:pos + cnt]
            pos += cnt

        def update(nm, idx, g):
            w_ref, m_ref, v_ref = ins[nm]
            d, m, v = _adam_step(w_ref[idx], g, m_ref[idx], v_ref[idx])
            o = outs[nm]
            if nm in cut:
                o[0][idx] = g
                o = o[1:]
            o[0][idx] = d
            o[1][idx] = m
            o[2][idx] = v

        tril = (lax.broadcasted_iota(jnp.int32, (CHUNK, CHUNK), 0) >= lax.broadcasted_iota(jnp.int32, (CHUNK, CHUNK), 1))
        for l in range(DEPTH):
            base = l * SMALL_ROWS
            row = (slice(l, l + 1), slice(None))
            update("norm_g", row, total_ref[base + SMALL_NORM:base + SMALL_NORM + 1, :])
            update("ple_norm_g", row, total_ref[base + SMALL_PLE:base + SMALL_PLE + 1, :])
            update("ln_v_g", row, total_ref[base + SMALL_LN:base + SMALL_LN + 1, 0:WIDTH_A])
            update("ln_v_b", row, total_ref[base + SMALL_LN:base + SMALL_LN + 1, WIDTH_A:2 * WIDTH_A])
            for h in range(HEADS_A):
                update("b_s", (l, slice(h, h + 1), slice(None)),
                       total_ref[base + SMALL_BS:base + SMALL_BS + 1, h * HEAD_DIM:(h + 1) * HEAD_DIM])
                lanes = slice(l * WIDTH_A + h * CHUNK, l * WIDTH_A + (h + 1) * CHUNK)
                update("w_s", (l, h), jnp.where(tril, total_ref[TOTAL_WS:TOTAL_ROWS, lanes], 0.0))
        update("final_g", (slice(None), slice(None)), total_ref[TOTAL_HEAD + HEAD_FINAL:TOTAL_HEAD + HEAD_FINAL + 1, :])
        update("conv_w", (slice(None),) * 3, gconv_ref[...])
        update("w_ple_proj", (slice(None),) * 3, gproj_ref[...])

    flat = [a for nm in names for a in st[nm]]
    out_shape = []
    for nm in names:
        out_shape += [jax.ShapeDtypeStruct(st[nm][0].shape, F32)] * (4 if nm in cut else 3)
    def whole(a):
        return pl.BlockSpec(a.shape, lambda i: (0,) * len(a.shape))

    operands = [pltpu.with_memory_space_constraint(a, pltpu.HBM) for a in (total, g_conv, g_proj, *flat)]
    outs = pl.pallas_call(
        body,
        name="adamw_small",
        grid=(1,),
        out_shape=[pltpu.HBM(a.shape, a.dtype) for a in out_shape],
        in_specs=[whole(a) for a in operands],
        out_specs=[whole(a) for a in out_shape],
        compiler_params=pltpu.CompilerParams(dimension_semantics=("arbitrary",), vmem_limit_bytes=32 * MIB),
    )(*operands)
    res, pos = {}, 0
    for nm in names:
        cnt = 4 if nm in cut else 3
        got = tuple(outs[pos:pos + cnt])
        res[nm] = got if nm in cut else ((g_conv if nm == "conv_w" else g_proj),) + got
        pos += cnt
    return res


def _split3_bf16(a):
    b1 = a.astype(BF16)
    r1 = a - b1.astype(F32)
    b2 = r1.astype(BF16)
    b3 = (r1 - b2.astype(F32)).astype(BF16)
    return b1, b2, b3


def _pack_weight_shard(w_in_l, w_out_l, w_gate_l, w_proj_l, conv_w_l):
    w_in_t = jnp.transpose(w_in_l).astype(BF16)
    proj_t = jnp.transpose(w_proj_l).astype(BF16)
    proj_rows = proj_t.reshape(D_MODEL // PLE_DIM, ROWS_PROJ, PLE_DIM).transpose(1, 0, 2).reshape(ROWS_PROJ, D_MODEL)
    conv_parts = jnp.concatenate([b.reshape(-1) for b in _split3_bf16(conv_w_l)])
    conv_rows = jnp.concatenate([conv_parts, jnp.zeros((ROWS_CONV * D_MODEL - conv_parts.shape[0],), BF16)])
    return jnp.concatenate(
        [w_in_t, w_out_l.astype(BF16), w_gate_l.astype(BF16), proj_rows, conv_rows.reshape(ROWS_CONV, D_MODEL)], axis=0
    )


def _unpack_conv(wg):
    per_dev = wg.reshape(N_DEV, ROWS_LAYER, D_MODEL)
    n_conv = (WIDTH_B // N_DEV) * 3
    conv_parts = per_dev[:, OFF_CONV].astype(F32)[:, :3 * n_conv].reshape(N_DEV, 3, n_conv)
    conv = (conv_parts[:, 0] + conv_parts[:, 1]) + conv_parts[:, 2]
    conv_k = jnp.transpose(conv.reshape(WIDTH_B, 3))
    conv_k = jnp.concatenate([conv_k, jnp.zeros((5, WIDTH_B), F32)], axis=0)
    return conv_k


def _unpack_grad_proj(red):
    proj_rows = red[OFF_PROJ:OFF_PROJ + ROWS_PROJ]
    proj_t = proj_rows.reshape(ROWS_PROJ, D_MODEL // PLE_DIM, PLE_DIM).transpose(1, 0, 2).reshape(ROWS_OUT, PLE_DIM)
    return jnp.transpose(proj_t)


def kernel(x, p, norm_g, w_in, ln_v_g, ln_v_b, w_s, b_s, conv_w, w_out, ple_norm_g, w_ple_gate, w_ple_proj, final_g, loss_target, m_norm_g, m_w_in, m_ln_v_g, m_ln_v_b, m_w_s, m_b_s, m_conv_w, m_w_out, m_ple_norm_g, m_w_ple_gate, m_w_ple_proj, m_final_g, v_norm_g, v_w_in, v_ln_v_g, v_ln_v_b, v_w_s, v_b_s, v_conv_w, v_w_out, v_ple_norm_g, v_w_ple_gate, v_w_ple_proj, v_final_g):
    me = 4 * lax.axis_index("x") + 2 * lax.axis_index("y") + lax.axis_index("c")
    xs = x[0]
    target = loss_target[0]

    shards = [_pack_weight_shard(w_in[l], w_out[l], w_ple_gate[l], w_ple_proj[l], conv_w[l]) for l in range(DEPTH)]
    tril = jnp.tril(jnp.ones((CHUNK, CHUNK), F32))

    def consts(l, wg_l):
        conv_k = _unpack_conv(wg_l)
        w_mix = w_s[l] * tril[None]
        small = dict(
            conv_k=conv_k,
            norm_g=norm_g[l].reshape(1, D_MODEL), ln_g=ln_v_g[l].reshape(1, WIDTH_A), ln_b=ln_v_b[l].reshape(1, WIDTH_A),
            w_mix=w_mix.astype(BF16), w_mix_t=jnp.swapaxes(w_mix, 1, 2).astype(BF16),
            b_mix=jnp.broadcast_to(b_s[l][:, :, None], (HEADS_A, CHUNK, HEAD_DIM)),
            ple_g=ple_norm_g[l].reshape(1, D_MODEL),
        )
        return dict({k: pltpu.with_memory_space_constraint(a, pltpu.HBM) for k, a in small.items()}, wg=wg_l)

    layer_consts = [consts(0, _all_gather_rows(shards[0]))]
    saved = []
    h = xs
    for l in range(DEPTH):
        k = layer_consts[l]
        outs = _forward_layer(
            l, h, p, k["wg"], k["conv_k"], k["norm_g"], k["ln_g"], k["ln_b"], k["w_mix"], k["b_mix"],
            k["ple_g"], next_shard=shards[l + 1] if l + 1 < DEPTH else None)
        proj, hn, cat, r, gpre, x1, x2 = outs[:7]
        if l + 1 < DEPTH:
            layer_consts.append(consts(l + 1, outs[7]))
        saved.append(dict(x_in=h, proj=proj, hn=hn, cat=cat, r=r, gpre=gpre, x1=x1))
        h = x2

    smalls, dws = [None] * DEPTH, [None] * DEPTH
    reduced = [None] * DEPTH
    pending = None
    dx = h
    for l in reversed(range(DEPTH)):
        k, s = layer_consts[l], saved[l]
        outs = _backward_layer(
            l, dx, s["x_in"], s["x1"], s["proj"], s["gpre"], p, k["wg"], k["conv_k"],
            k["norm_g"], k["ln_g"], k["ln_b"], k["w_mix"], k["w_mix_t"], k["b_mix"], k["ple_g"],
            loss_head=(target, final_g.reshape(1, D_MODEL)) if l == DEPTH - 1 else None)
        dx, dproj, dx1, dgpre, dpp, smalls[l], dws[l] = outs[:7]
        if l == DEPTH - 1:
            head = outs[7]
        (pack,) = _weight_grads(l, dproj, s["hn"], s["cat"], dx1, s["r"], dgpre, dpp, p)
        if pending is not None:
            sems, pack_thru, land_thru = pending
            pieces = _scatter_wait(l + 1, sems, pack_thru, land_thru, pack)
            own = lax.dynamic_index_in_dim(pack_thru, me, 0, keepdims=False)
            reduced[l + 1] = _sum_pieces(l + 1, lax.dynamic_update_index_in_dim(pieces, own, me, 0))
        pending = _scatter_start(l, pack) if l > 0 else pack
    reduced[0], total = _reduce_scatter_all_reduce(0, pending, smalls, head, dws)
    grad_x = dx[None]
    loss = total[TOTAL_HEAD + HEAD_LOSS, 0]

    n_ch = WIDTH_B // N_DEV
    g_conv = jnp.stack([total[l * SMALL_ROWS + SMALL_CONV:l * SMALL_ROWS + SMALL_CONV + 3, 0:WIDTH_B] for l in range(DEPTH)], axis=1)
    g_conv = lax.dynamic_slice_in_dim(g_conv, me * n_ch, n_ch, axis=2)
    g_proj = jnp.stack([_unpack_grad_proj(reduced[l]) for l in range(DEPTH)])

    def t_in(a):
        return jnp.swapaxes(a, 1, 2)

    def t_conv(a):
        return jnp.transpose(a, (2, 0, 1))

    (r_in,) = _adamw_rows("adamw_w_in", reduced, [OFF_IN], [(t_in(w_in), t_in(m_w_in), t_in(v_w_in))])
    r_out, r_gate = _adamw_rows(
        "adamw_w_out_gate", reduced, [OFF_OUT, OFF_GATE],
        [(w_out, m_w_out, v_w_out), (w_ple_gate, m_w_ple_gate, v_w_ple_gate)])
    small = _adamw_small(total, g_conv, g_proj, dict(
        norm_g=(norm_g, m_norm_g, v_norm_g), ple_norm_g=(ple_norm_g, m_ple_norm_g, v_ple_norm_g),
        ln_v_g=(ln_v_g, m_ln_v_g, v_ln_v_g), ln_v_b=(ln_v_b, m_ln_v_b, v_ln_v_b),
        b_s=(b_s, m_b_s, v_b_s), w_s=(w_s, m_w_s, v_w_s),
        final_g=tuple(a.reshape(1, D_MODEL) for a in (final_g, m_final_g, v_final_g)),
        conv_w=(t_conv(conv_w), t_conv(m_conv_w), t_conv(v_conv_w)),
        w_ple_proj=(w_ple_proj, m_w_ple_proj, v_w_ple_proj),
    ))
    res = dict(small, w_in=tuple(t_in(a) for a in r_in), w_out=r_out, w_ple_gate=r_gate)
    res["final_g"] = tuple(a.reshape(D_MODEL) for a in res["final_g"])
    res["conv_w"] = tuple(jnp.transpose(a, (1, 2, 0)) for a in res["conv_w"])
    order = ["norm_g", "w_in", "ln_v_g", "ln_v_b", "w_s", "b_s", "conv_w", "w_out", "ple_norm_g", "w_ple_gate", "w_ple_proj", "final_g"]
    return (loss, grad_x, *[res[n][0] for n in order], *[res[n][1] for n in order],
            *[res[n][2] for n in order], *[res[n][3] for n in order])
```

```python
import jax
import jax.numpy as jnp
from jax import lax
from jax.experimental import pallas as pl
from jax.experimental.pallas import tpu as pltpu

F32 = jnp.float32
BF16 = jnp.bfloat16

D_MODEL = 1024
WIDTH_A = 512
WIDTH_B = 512
HEADS_A = 4
HEAD_DIM = 128
CHUNK = 128
PLE_DIM = 256
PROJ_WIDTH = 3584
DEPTH = 2
EPS = 1e-6
N_DEV = 8

ADAM_LR = 0.001
ADAM_B1 = 0.9
ADAM_B2 = 0.999
ADAM_EPS = 1e-08
ADAM_WD = 0.01
ADAM_STEP = 10

ROWS_IN = PROJ_WIDTH // N_DEV
ROWS_OUT = D_MODEL // N_DEV
ROWS_GATE = D_MODEL // N_DEV
ROWS_PROJ = (D_MODEL // N_DEV) * PLE_DIM // D_MODEL
ROWS_CONV = 16
OFF_IN = 0
OFF_OUT = OFF_IN + ROWS_IN
OFF_GATE = OFF_OUT + ROWS_OUT
OFF_PROJ = OFF_GATE + ROWS_GATE
OFF_CONV = OFF_PROJ + ROWS_PROJ
ROWS_GRAD = OFF_CONV
ROWS_LAYER = OFF_CONV + ROWS_CONV

SMALL_ROWS = 8
SMALL_NORM = 0
SMALL_PLE = 1
SMALL_LN = 2
SMALL_BS = 3
SMALL_CONV = 4
HEAD_FINAL = 0
HEAD_LOSS = 1
TOTAL_HEAD = DEPTH * SMALL_ROWS
TOTAL_WS = TOTAL_HEAD + SMALL_ROWS
TOTAL_ROWS = TOTAL_WS + CHUNK

MIB = 1024 * 1024
MESH = pl.DeviceIdType.MESH

NT_DIMS = (((1,), (1,)), ((), ()))
TN_DIMS = (((0,), (0,)), ((), ()))


def _dot(a, b):
    return jnp.dot(a, b, preferred_element_type=F32)


def _dot_nt(a, b):
    return lax.dot_general(a, b, NT_DIMS, preferred_element_type=F32)


def _dot_tn(a, b):
    return lax.dot_general(a, b, TN_DIMS, preferred_element_type=F32)


def _colsum8(a):
    rows, n = a.shape
    return jnp.sum(a.reshape(rows // 8, 8, n), axis=0)


def _sigmoid(z):
    return 1.0 / (1.0 + jnp.exp(-z))


def _tile(t, want):
    return want if t % want == 0 else t


class _TwoLevelGather:
    def __init__(self, x_ref, out_ref, m_per, send_sems, recv_sems, local_sem):
        x, y, c = lax.axis_index("x"), lax.axis_index("y"), lax.axis_index("c")
        self.me, self.sibling = (x, y, c), (x, y, 1 - c)
        self.xn, self.yn, self.diag = (1 - x, y, c), (x, 1 - y, c), (1 - x, 1 - y, c)
        self.x_ref, self.out_ref, self.m_per = x_ref, out_ref, m_per
        self.half = (m_per // 32) * 16
        self.send_sems, self.recv_sems = send_sems, recv_sems
        self.mine = pltpu.make_async_copy(x_ref, self.rows(self.me), local_sem)

    def rows(self, block, part=None):
        px, py, pc = block
        base = (4 * px + 2 * py + pc) * self.m_per
        if part is None:
            return self.out_ref.at[pl.ds(base, self.m_per), :]
        if part == 0:
            return self.out_ref.at[pl.ds(base, self.half), :]
        return self.out_ref.at[pl.ds(base + self.half, self.m_per - self.half), :]

    def copy(self, k, block, to, src=None, part=None):
        return pltpu.make_async_remote_copy(
            src_ref=self.rows(block, part) if src is None else src,
            dst_ref=self.rows(block, part),
            send_sem=self.send_sems.at[k],
            recv_sem=self.recv_sems.at[k],
            device_id=to,
            device_id_type=MESH,
        )

    def first(self):
        return [self.copy(0, self.me, self.sibling, src=self.x_ref),
                self.copy(1, self.me, self.xn, src=self.x_ref),
                self.copy(2, self.me, self.yn, src=self.x_ref)]

    def second(self):
        return [self.copy(3, self.xn, self.yn, part=0), self.copy(7, self.yn, self.xn, part=1),
                self.copy(4, self.xn, self.sibling), self.copy(5, self.yn, self.sibling)]

    def third(self):
        return [self.copy(6, self.diag, self.sibling)]

    def start(self):
        self.mine.start()
        for cp in self.first():
            cp.start()

    def pass_on(self):
        fwd_x, fwd_y, sib_x, sib_y = self.second()
        self.copy(1, self.xn, self.me).wait_recv()
        fwd_x.start()
        sib_x.start()
        self.copy(2, self.yn, self.me).wait_recv()
        fwd_y.start()
        sib_y.start()

    def pass_on_diagonal(self):
        self.copy(3, self.diag, self.me, part=0).wait_recv()
        self.copy(7, self.diag, self.me, part=1).wait_recv()
        self.third()[0].start()

    def finish(self):
        sib = (self.sibling[0], self.sibling[1], self.sibling[2])
        self.copy(0, sib, self.me).wait_recv()
        for k, chip in ((4, self.xn), (5, self.yn), (6, self.diag)):
            self.copy(k, (chip[0], chip[1], sib[2]), self.me).wait_recv()
        for cp in self.first() + self.second() + self.third():
            cp.wait_send()
        self.mine.wait()


GATHER_SEMS = [pltpu.SemaphoreType.DMA((8,)), pltpu.SemaphoreType.DMA((8,)), pltpu.SemaphoreType.DMA]


def _all_gather_rows(shard):
    m_per, n = shard.shape

    def body(x_ref, out_ref, send_sems, recv_sems, local_sem):
        ag = _TwoLevelGather(x_ref, out_ref, m_per, send_sems, recv_sems, local_sem)
        ag.start()
        ag.pass_on()
        ag.pass_on_diagonal()
        ag.finish()

    return pl.pallas_call(
        body,
        name="weights_all_gather",
        out_shape=pltpu.HBM((N_DEV * m_per, n), shard.dtype),
        in_specs=[pl.BlockSpec(memory_space=pltpu.HBM)],
        out_specs=pl.BlockSpec(memory_space=pltpu.HBM),
        scratch_shapes=list(GATHER_SEMS),
    )(pltpu.with_memory_space_constraint(shard, pltpu.HBM))


PROJ_PARTS = D_MODEL // PLE_DIM
N_WEIGHT_COPIES = N_DEV * (3 + PROJ_PARTS)


def _weight_copies(wg_ref, w_in_t, w_out, w_gate, w_proj_t, sems):
    copies = []
    for s in range(N_DEV):
        base = s * ROWS_LAYER
        for dst, off, rows in ((w_in_t, OFF_IN, ROWS_IN), (w_out, OFF_OUT, ROWS_OUT), (w_gate, OFF_GATE, ROWS_GATE)):
            copies.append((wg_ref.at[pl.ds(base + off, rows), :], dst.at[pl.ds(s * rows, rows), :]))
        for j in range(PROJ_PARTS):
            copies.append((
                wg_ref.at[pl.ds(base + OFF_PROJ, ROWS_PROJ), pl.ds(j * PLE_DIM, PLE_DIM)],
                w_proj_t.at[pl.ds(s * ROWS_OUT + j * ROWS_PROJ, ROWS_PROJ), :],
            ))
    return [pltpu.make_async_copy(src, dst, sems.at[k]) for k, (src, dst) in enumerate(copies)]


def _forward_layer(layer, x, p_all, wg, conv_k, norm_g, ln_g, ln_b, w_mix, b_mix, ple_g, next_shard=None):
    t = x.shape[0]
    tm = _tile(t, 512)
    nt = t // tm
    gathers = next_shard is not None

    def body(*refs):
        (x_ref, p_ref, wg_ref, cw_ref, ng_ref, lng_ref, lnb_ref, wm_ref, bm_ref, pg_ref) = refs[:10]
        refs = refs[10:]
        if gathers:
            shard_ref, refs = refs[0], refs[1:]
        (proj_ref, hn_ref, cat_ref, r_ref, gpre_ref, x1_ref, x2_ref) = refs[:7]
        refs = refs[7:]
        if gathers:
            gathered_ref, refs = refs[0], refs[1:]
        (w_in_t, w_out, w_gate, wpt_ref, vln_s, mixed_s, halo_s, sems) = refs[:8]
        i = pl.program_id(0)
        if gathers:
            ag = _TwoLevelGather(shard_ref, gathered_ref, ROWS_LAYER, *refs[8:11])

            @pl.when(i == 0)
            def _():
                ag.start()

            @pl.when(i == (5 * nt) // 16)
            def _():
                ag.pass_on()

            @pl.when(i == nt // 2)
            def _():
                ag.pass_on_diagonal()

        @pl.when(i == 0)
        def _():
            copies = _weight_copies(wg_ref, w_in_t, w_out, w_gate, wpt_ref, sems)
            for cp in copies:
                cp.start()
            halo_s[...] = jnp.zeros_like(halo_s)
            for cp in copies:
                cp.wait()

        xv = x_ref[...]
        rstd0 = lax.rsqrt(jnp.mean(xv * xv, axis=-1, keepdims=True) + EPS)
        hn_ref[...] = (xv * rstd0 * ng_ref[...]).astype(BF16)

        def proj_section(k):
            sec = _dot_nt(hn_ref[...], w_in_t[pl.ds(k * 512, 512), :])
            proj_ref[:, k * 512:(k + 1) * 512] = sec.astype(BF16)
            return sec

        v = proj_section(1)
        mu = jnp.mean(v, axis=-1, keepdims=True)
        vc = v - mu
        var = jnp.mean(vc * vc, axis=-1, keepdims=True)
        vln = vc * lax.rsqrt(var + EPS) * lng_ref[...] + lnb_ref[...]
        vln_s[...] = vln.astype(BF16)
        for ci in range(tm // CHUNK):
            rows = pl.ds(ci * CHUNK, CHUNK)
            for h in range(HEADS_A):
                cols = pl.ds(h * HEAD_DIM, HEAD_DIM)
                mixed_s[rows, cols] = _dot(wm_ref[h], vln_s[rows, cols]) + bm_ref[h]
        u = proj_section(0)
        za = proj_section(2)
        out_a = u * mixed_s[...] * (za * _sigmoid(za))
        cat_ref[:, 0:512] = out_a.astype(BF16)

        xc = proj_section(5) * proj_section(3)
        prev = halo_s[...]
        row = lax.broadcasted_iota(jnp.int32, (tm, WIDTH_B), 0)
        xc_m1 = jnp.where(row == 0, prev[7:8, :], pltpu.roll(xc, 1, 0))
        xc_m2 = jnp.where(row == 0, prev[6:7, :], jnp.where(row == 1, prev[7:8, :], pltpu.roll(xc, 2, 0)))
        halo_s[...] = xc[tm - 8:tm, :]
        cw = cw_ref[...]
        yc = cw[0:1, :] * xc_m2 + cw[1:2, :] * xc_m1 + cw[2:3, :] * xc
        zb = proj_section(6)
        out_b = proj_section(4) * yc * (zb * _sigmoid(zb))
        cat_ref[:, 512:1024] = out_b.astype(BF16)

        x1 = xv + _dot(cat_ref[...], w_out[...])
        x1_ref[...] = x1
        rstd1 = lax.rsqrt(jnp.mean(x1 * x1, axis=-1, keepdims=True) + EPS)
        r_ref[...] = (x1 * rstd1 * pg_ref[...]).astype(BF16)
        gpre = _dot(r_ref[...], w_gate[...])
        gpre_ref[...] = gpre.astype(BF16)
        pp = _dot_nt(p_ref[...].astype(BF16), wpt_ref[...])
        x2_ref[...] = x1 + _sigmoid(gpre) * pp

        if gathers:
            @pl.when(i == nt - 1)
            def _():
                ag.finish()

    def tok(width):
        return pl.BlockSpec((tm, width), lambda i: (i, 0))

    def whole(shape):
        return pl.BlockSpec(shape, lambda i: (0,) * len(shape))

    hbm = pl.BlockSpec(memory_space=pl.ANY)
    operands = [x, p_all, wg, conv_k, norm_g, ln_g, ln_b, w_mix, b_mix, ple_g]
    in_specs = [
        tok(D_MODEL), pl.BlockSpec((None, None, tm, PLE_DIM), lambda i: (layer, 0, i, 0)), hbm,
        whole((8, WIDTH_B)), whole((1, D_MODEL)), whole((1, WIDTH_A)), whole((1, WIDTH_A)),
        whole((HEADS_A, CHUNK, CHUNK)), whole((HEADS_A, CHUNK, HEAD_DIM)), whole((1, D_MODEL)),
    ]
    out_specs = [tok(PROJ_WIDTH), tok(D_MODEL), tok(D_MODEL), tok(D_MODEL), tok(D_MODEL), tok(D_MODEL), tok(D_MODEL)]
    out_shape = [
        jax.ShapeDtypeStruct((t, PROJ_WIDTH), BF16),
        jax.ShapeDtypeStruct((t, D_MODEL), BF16),
        jax.ShapeDtypeStruct((t, D_MODEL), BF16),
        jax.ShapeDtypeStruct((t, D_MODEL), BF16),
        jax.ShapeDtypeStruct((t, D_MODEL), BF16),
        jax.ShapeDtypeStruct((t, D_MODEL), F32),
        jax.ShapeDtypeStruct((t, D_MODEL), F32),
    ]
    scratch_shapes = [
        pltpu.VMEM((PROJ_WIDTH, D_MODEL), BF16),
        pltpu.VMEM((D_MODEL, D_MODEL), BF16),
        pltpu.VMEM((D_MODEL, D_MODEL), BF16),
        pltpu.VMEM((D_MODEL, PLE_DIM), BF16),
        pltpu.VMEM((tm, WIDTH_A), BF16),
        pltpu.VMEM((tm, WIDTH_A), F32),
        pltpu.VMEM((8, WIDTH_B), F32),
        pltpu.SemaphoreType.DMA((N_WEIGHT_COPIES,)),
    ]
    if gathers:
        operands.append(pltpu.with_memory_space_constraint(next_shard, pltpu.HBM))
        in_specs.append(pl.BlockSpec(memory_space=pltpu.HBM))
        out_specs.append(pl.BlockSpec(memory_space=pltpu.HBM))
        out_shape.append(pltpu.HBM((N_DEV * ROWS_LAYER, D_MODEL), BF16))
        scratch_shapes += list(GATHER_SEMS)

    return pl.pallas_call(
        body,
        name=f"layer{layer}_forward",
        grid=(nt,),
        in_specs=in_specs,
        out_specs=out_specs,
        out_shape=out_shape,
        scratch_shapes=scratch_shapes,
        compiler_params=pltpu.CompilerParams(dimension_semantics=("arbitrary",), vmem_limit_bytes=56 * MIB),
    )(*operands)


class _DirectScatter:
    def __init__(self, pack_ref, pieces_ref, send_sems, recv_sems, local_sem):
        x, y, c = lax.axis_index("x"), lax.axis_index("y"), lax.axis_index("c")
        me = 4 * x + 2 * y + c
        self.copies = []
        for k in range(N_DEV - 1):
            fx, fy, fc = ((k + 1) >> 2) & 1, ((k + 1) >> 1) & 1, (k + 1) & 1
            tx, ty, tc = x ^ fx, y ^ fy, c ^ fc
            self.copies.append(
                pltpu.make_async_remote_copy(
                    src_ref=pack_ref.at[4 * tx + 2 * ty + tc], dst_ref=pieces_ref.at[me],
                    send_sem=send_sems.at[k], recv_sem=recv_sems.at[k],
                    device_id=(tx, ty, tc), device_id_type=MESH,
                )
            )
        self.mine = pltpu.make_async_copy(pack_ref.at[me], pieces_ref.at[me], local_sem)

    def start(self):
        self.mine.start()
        for cp in self.copies:
            cp.start()

    def finish(self):
        for cp in self.copies:
            cp.wait_recv()
        for cp in self.copies:
            cp.wait_send()
        self.mine.wait()


SCATTER_SEMS = [pltpu.SemaphoreType.DMA((N_DEV - 1,)), pltpu.SemaphoreType.DMA((N_DEV - 1,)), pltpu.SemaphoreType.DMA]


def _backward_layer(layer, dx2, x_in, x1, proj, gpre, p_all, wg, conv_k, norm_g, ln_g, ln_b,
                    w_mix, w_mix_t, b_mix, ple_g, loss_head=None):
    t = x_in.shape[0]
    tm = _tile(t, 256)
    nt = t // tm
    n_chunks = tm // CHUNK
    halo_rows = 16
    heads = loss_head is not None

    def body(*refs):
        (dx2_ref, xin_ref, x1_ref, proj_ref, halo_ref, gpre_ref, p_ref, wg_ref, cw_ref,
         ng_ref, lng_ref, lnb_ref, wm_ref, wmt_ref, bm_ref, pg_ref) = refs[:16]
        refs = refs[16:]
        if heads:
            tgt_ref, fg_ref = refs[:2]
            refs = refs[2:]
        (dxin_ref, dproj_ref, dx1_ref, dgpre_ref, dpp_ref, small_ref, dws_ref) = refs[:7]
        refs = refs[7:]
        if heads:
            head_ref, refs = refs[0], refs[1:]
        (w_in_t, w_out, w_gate, wpt_ref, vln_s, mixed_s, dmix_s, dvln_s, carry_s,
         ng_acc, pg_acc, lng_acc, lnb_acc, cw_acc, dbm_ref, sems) = refs[:16]
        if heads:
            loss_acc, fg_acc = refs[16:18]
        i = pl.program_id(0)
        tile = nt - 1 - i

        @pl.when(i == 0)
        def _():
            copies = _weight_copies(wg_ref, w_in_t, w_out, w_gate, wpt_ref, sems)
            for cp in copies:
                cp.start()
            if heads:
                loss_acc[...] = jnp.zeros_like(loss_acc)
                fg_acc[...] = jnp.zeros_like(fg_acc)
            carry_s[...] = jnp.zeros_like(carry_s)
            ng_acc[...] = jnp.zeros_like(ng_acc)
            pg_acc[...] = jnp.zeros_like(pg_acc)
            lng_acc[...] = jnp.zeros_like(lng_acc)
            lnb_acc[...] = jnp.zeros_like(lnb_acc)
            cw_acc[...] = jnp.zeros_like(cw_acc)
            dws_ref[...] = jnp.zeros_like(dws_ref)
            dbm_ref[...] = jnp.zeros_like(dbm_ref)
            for cp in copies:
                cp.wait()

        if heads:
            x2v = dx2_ref[...]
            fg = fg_ref[...]
            rstdf = lax.rsqrt(jnp.mean(x2v * x2v, axis=-1, keepdims=True) + EPS)
            xhatf = x2v * rstdf
            err = xhatf * fg - tgt_ref[...]
            loss_acc[...] += _colsum8(err * err)
            dy = err * (1.0 / D_MODEL)
            fg_acc[...] += _colsum8(dy * xhatf)
            dxhf = dy * fg
            dx2v = rstdf * (dxhf - xhatf * jnp.mean(dxhf * xhatf, axis=-1, keepdims=True))
        else:
            dx2v = dx2_ref[...]

        gate = _sigmoid(gpre_ref[...].astype(F32))
        pp = _dot_nt(p_ref[...].astype(BF16), wpt_ref[...])
        dpp = dx2v * gate
        dpp_ref[...] = dpp.astype(BF16)
        dgpre = (dpp * pp * (1.0 - gate)).astype(BF16)
        dgpre_ref[...] = dgpre
        dr = _dot_nt(dgpre, w_gate[...])
        x1v = x1_ref[...]
        rstd1 = lax.rsqrt(jnp.mean(x1v * x1v, axis=-1, keepdims=True) + EPS)
        xhat1 = x1v * rstd1
        pg_acc[...] += _colsum8(dr * xhat1)
        dxh = dr * pg_ref[...]
        dx1 = dx2v + rstd1 * (dxh - xhat1 * jnp.mean(dxh * xhat1, axis=-1, keepdims=True))
        dx1b = dx1.astype(BF16)
        dx1_ref[...] = dx1b

        dcat = _dot_nt(dx1b, w_out[...])
        dca = dcat[:, 0:512]
        dcb = dcat[:, 512:1024]

        u = proj_ref[:, 0:512]
        v = proj_ref[:, 512:1024].astype(F32)
        za = proj_ref[:, 1024:1536]
        mu = jnp.mean(v, axis=-1, keepdims=True)
        vc = v - mu
        var = jnp.mean(vc * vc, axis=-1, keepdims=True)
        rs = lax.rsqrt(var + EPS)
        vhat = vc * rs
        lng = lng_ref[...]
        vln_s[...] = (vhat * lng + lnb_ref[...]).astype(BF16)
        for ci in range(n_chunks):
            rows = pl.ds(ci * CHUNK, CHUNK)
            for h in range(HEADS_A):
                cols = pl.ds(h * HEAD_DIM, HEAD_DIM)
                mixed_s[rows, cols] = (_dot(wm_ref[h], vln_s[rows, cols]) + bm_ref[h]).astype(BF16)
        mixed = mixed_s[...]
        sga = _sigmoid(za)
        sa = za * sga
        dsa = sga + sa * (1.0 - sga)

        def put_section(k, val):
            dproj_ref[:, k * 512:(k + 1) * 512] = val.astype(BF16)

        dcab = dca.astype(BF16)
        dca_sa = dcab * sa
        put_section(0, dca_sa * mixed)
        dmix_s[...] = dca_sa * u
        put_section(2, (dcab * dsa) * (u * mixed))
        dbm_acc = jnp.zeros((CHUNK, WIDTH_A), F32)
        for ci in range(n_chunks):
            rows = pl.ds(ci * CHUNK, CHUNK)
            dbm_acc = dbm_acc + dmix_s[rows, :].astype(F32)
            for h in range(HEADS_A):
                cols = pl.ds(h * HEAD_DIM, HEAD_DIM)
                dvln_s[rows, cols] = _dot(wmt_ref[h], dmix_s[rows, cols])
                dws_ref[:, cols] += _dot_nt(dmix_s[rows, cols], vln_s[rows, cols])
        dbm_ref[...] += dbm_acc
        dvln = dvln_s[...]
        lng_acc[...] += _colsum8(dvln * vhat)
        lnb_acc[...] += _colsum8(dvln)
        dvh = dvln * lng
        dv = rs * (dvh - jnp.mean(dvh, axis=-1, keepdims=True) - vhat * jnp.mean(dvh * vhat, axis=-1, keepdims=True))
        put_section(1, dv)

        hb = proj_ref[:, 1536:2048].astype(F32)
        gb = proj_ref[:, 2048:2560]
        gc = proj_ref[:, 2560:3072].astype(F32)
        zb = proj_ref[:, 3072:3584]
        xc = gc * hb
        prev = halo_ref[:, 2560:3072].astype(F32) * halo_ref[:, 1536:2048].astype(F32)
        prev = jnp.where(tile > 0, prev, 0.0)
        row = lax.broadcasted_iota(jnp.int32, (tm, WIDTH_B), 0)
        p1 = prev[halo_rows - 1:halo_rows, :]
        p2 = prev[halo_rows - 2:halo_rows - 1, :]
        xc_m1 = jnp.where(row == 0, p1, pltpu.roll(xc, 1, 0))
        xc_m2 = jnp.where(row == 0, p2, jnp.where(row == 1, p1, pltpu.roll(xc, 2, 0)))
        cw = cw_ref[...]
        yc = cw[0:1, :] * xc_m2 + cw[1:2, :] * xc_m1 + cw[2:3, :] * xc
        sgb = _sigmoid(zb)
        sb = zb * sgb
        dsb = sgb + sb * (1.0 - sgb)
        dcbb = dcb.astype(BF16)
        ycb = yc.astype(BF16)
        dcb_sb = dcbb * sb
        put_section(4, dcb_sb * ycb)
        dyc = (dcb_sb * gb).astype(F32)
        put_section(6, (dcbb * dsb) * (gb * ycb))
        nxt = carry_s[...]
        dyc_p1 = jnp.where(row == tm - 1, nxt[0:1, :], pltpu.roll(dyc, tm - 1, 0))
        dyc_p2 = jnp.where(row == tm - 1, nxt[1:2, :], jnp.where(row == tm - 2, nxt[0:1, :], pltpu.roll(dyc, tm - 2, 0)))
        carry_s[...] = dyc[0:8, :]
        dxc = cw[2:3, :] * dyc + cw[1:2, :] * dyc_p1 + cw[0:1, :] * dyc_p2
        cw_acc[0] += _colsum8(dyc * xc_m2)
        cw_acc[1] += _colsum8(dyc * xc_m1)
        cw_acc[2] += _colsum8(dyc * xc)
        put_section(3, dxc * gc)
        put_section(5, dxc * hb)

        dhn = _dot(dproj_ref[...], w_in_t[...])
        xv = xin_ref[...]
        rstd0 = lax.rsqrt(jnp.mean(xv * xv, axis=-1, keepdims=True) + EPS)
        xhat0 = xv * rstd0
        ng_acc[...] += _colsum8(dhn * xhat0)
        dxh0 = dhn * ng_ref[...]
        dxin_ref[...] = dx1 + rstd0 * (dxh0 - xhat0 * jnp.mean(dxh0 * xhat0, axis=-1, keepdims=True))

        @pl.when(i == nt - 1)
        def _():
            small_ref[...] = jnp.zeros_like(small_ref)
            small_ref[SMALL_NORM:SMALL_NORM + 1, :] = jnp.sum(ng_acc[...], axis=0, keepdims=True)
            small_ref[SMALL_PLE:SMALL_PLE + 1, :] = jnp.sum(pg_acc[...], axis=0, keepdims=True)
            small_ref[SMALL_LN:SMALL_LN + 1, 0:WIDTH_A] = jnp.sum(lng_acc[...], axis=0, keepdims=True)
            small_ref[SMALL_LN:SMALL_LN + 1, WIDTH_A:2 * WIDTH_A] = jnp.sum(lnb_acc[...], axis=0, keepdims=True)
            for h in range(HEADS_A):
                cols = pl.ds(h * HEAD_DIM, HEAD_DIM)
                small_ref[SMALL_BS:SMALL_BS + 1, cols] = jnp.sum(jnp.transpose(dbm_ref[:, cols]), axis=0, keepdims=True)
            for k in range(3):
                small_ref[SMALL_CONV + k:SMALL_CONV + k + 1, 0:WIDTH_B] = jnp.sum(cw_acc[k], axis=0, keepdims=True)
            if heads:
                total = jnp.sum(loss_acc[...]) * (0.5 / D_MODEL)
                rows8 = lax.broadcasted_iota(jnp.int32, (SMALL_ROWS, D_MODEL), 0)
                lanes8 = lax.broadcasted_iota(jnp.int32, (SMALL_ROWS, D_MODEL), 1)
                head_ref[...] = jnp.where((rows8 == HEAD_LOSS) & (lanes8 == 0), total, 0.0)
                head_ref[HEAD_FINAL:HEAD_FINAL + 1, :] = jnp.sum(fg_acc[...], axis=0, keepdims=True)

    def tok(width):
        return pl.BlockSpec((tm, width), lambda i: (nt - 1 - i, 0))

    def whole(shape):
        return pl.BlockSpec(shape, lambda i: (0,) * len(shape))

    halo_spec = pl.BlockSpec(
        (halo_rows, PROJ_WIDTH), lambda i: (jnp.maximum((nt - 1 - i) * (tm // halo_rows) - 1, 0), 0)
    )
    hbm = pl.BlockSpec(memory_space=pl.ANY)
    operands = [dx2, x_in, x1, proj, proj, gpre, p_all, wg, conv_k, norm_g, ln_g, ln_b, w_mix, w_mix_t, b_mix, ple_g]
    in_specs = [
        tok(D_MODEL), tok(D_MODEL), tok(D_MODEL), tok(PROJ_WIDTH), halo_spec, tok(D_MODEL),
        pl.BlockSpec((None, None, tm, PLE_DIM), lambda i: (layer, 0, nt - 1 - i, 0)), hbm,
        whole((8, WIDTH_B)), whole((1, D_MODEL)), whole((1, WIDTH_A)), whole((1, WIDTH_A)),
        whole((HEADS_A, CHUNK, CHUNK)), whole((HEADS_A, CHUNK, CHUNK)), whole((HEADS_A, CHUNK, HEAD_DIM)),
        whole((1, D_MODEL)),
    ]
    out_specs = [
        tok(D_MODEL), tok(PROJ_WIDTH), tok(D_MODEL), tok(D_MODEL), tok(D_MODEL),
        whole((SMALL_ROWS, D_MODEL)), whole((CHUNK, WIDTH_A)),
    ]
    out_shape = [
        jax.ShapeDtypeStruct((t, D_MODEL), F32),
        jax.ShapeDtypeStruct((t, PROJ_WIDTH), BF16),
        jax.ShapeDtypeStruct((t, D_MODEL), BF16),
        jax.ShapeDtypeStruct((t, D_MODEL), BF16),
        jax.ShapeDtypeStruct((t, D_MODEL), BF16),
        jax.ShapeDtypeStruct((SMALL_ROWS, D_MODEL), F32),
        jax.ShapeDtypeStruct((CHUNK, WIDTH_A), F32),
    ]
    scratch_shapes = [
        pltpu.VMEM((PROJ_WIDTH, D_MODEL), BF16),
        pltpu.VMEM((D_MODEL, D_MODEL), BF16),
        pltpu.VMEM((D_MODEL, D_MODEL), BF16),
        pltpu.VMEM((D_MODEL, PLE_DIM), BF16),
        pltpu.VMEM((tm, WIDTH_A), BF16),
        pltpu.VMEM((tm, WIDTH_A), BF16),
        pltpu.VMEM((tm, WIDTH_A), BF16),
        pltpu.VMEM((tm, WIDTH_A), F32),
        pltpu.VMEM((8, WIDTH_B), F32),
        pltpu.VMEM((8, D_MODEL), F32),
        pltpu.VMEM((8, D_MODEL), F32),
        pltpu.VMEM((8, WIDTH_A), F32),
        pltpu.VMEM((8, WIDTH_A), F32),
        pltpu.VMEM((3, 8, WIDTH_B), F32),
        pltpu.VMEM((CHUNK, WIDTH_A), F32),
        pltpu.SemaphoreType.DMA((N_WEIGHT_COPIES,)),
    ]
    if heads:
        operands += list(loss_head)
        in_specs += [tok(D_MODEL), whole((1, D_MODEL))]
        out_specs.append(whole((SMALL_ROWS, D_MODEL)))
        out_shape.append(jax.ShapeDtypeStruct((SMALL_ROWS, D_MODEL), F32))
        scratch_shapes += [pltpu.VMEM((8, D_MODEL), F32), pltpu.VMEM((8, D_MODEL), F32)]

    return pl.pallas_call(
        body,
        name=f"layer{layer}_backward",
        grid=(nt,),
        in_specs=in_specs,
        out_specs=out_specs,
        out_shape=out_shape,
        scratch_shapes=scratch_shapes,
        compiler_params=pltpu.CompilerParams(dimension_semantics=("arbitrary",), vmem_limit_bytes=56 * MIB),
    )(*operands)


def _scatter_targets():
    x, y, c = lax.axis_index("x"), lax.axis_index("y"), lax.axis_index("c")
    out = []
    for k in range(N_DEV - 1):
        fx, fy, fc = ((k + 1) >> 2) & 1, ((k + 1) >> 1) & 1, (k + 1) & 1
        tx, ty, tc = x ^ fx, y ^ fy, c ^ fc
        out.append((4 * tx + 2 * ty + tc, (tx, ty, tc)))
    return 4 * x + 2 * y + c, out


def _scatter_start(layer, pack):
    n = N_DEV - 1

    def body(pack_ref, land_ref, *rest):
        sems = rest[:2 * n]
        me, targets = _scatter_targets()
        for k, (block, device) in enumerate(targets):
            pltpu.make_async_remote_copy(
                src_ref=pack_ref.at[block], dst_ref=land_ref.at[me], send_sem=sems[k], recv_sem=sems[n + k],
                device_id=device, device_id_type=MESH,
            ).start()

    hbm = pl.BlockSpec(memory_space=pltpu.HBM)
    sem = pl.BlockSpec(memory_space=pltpu.SEMAPHORE)
    outs = pl.pallas_call(
        body,
        name=f"layer{layer}_scatter_start",
        out_shape=(*[pltpu.SemaphoreType.DMA(())] * (2 * n), pltpu.HBM(pack.shape, pack.dtype)),
        in_specs=(hbm, hbm),
        out_specs=(*[sem] * (2 * n), hbm),
        input_output_aliases={1: 2 * n},
        compiler_params=pltpu.CompilerParams(has_side_effects=pltpu.SideEffectType.DATAFLOW_SIDE_EFFECTING),
    )(pack, pltpu.with_memory_space_constraint(lax.empty(pack.shape, pack.dtype), pltpu.HBM))
    return outs[:2 * n], pack, outs[2 * n]


def _scatter_wait(layer, sems, pack_thru, land_thru, after):
    n = N_DEV - 1

    def body(pack_ref, land_ref, *rest):
        sem_refs = rest[:2 * n]
        me, targets = _scatter_targets()
        for k, (block, device) in enumerate(targets):
            copy = pltpu.make_async_remote_copy(
                src_ref=pack_ref.at[block], dst_ref=land_ref.at[me], send_sem=sem_refs[k], recv_sem=sem_refs[n + k],
                device_id=device, device_id_type=MESH,
            )
            copy.wait_send()
            copy.wait_recv()

    hbm = pl.BlockSpec(memory_space=pltpu.HBM)
    sem = pl.BlockSpec(memory_space=pltpu.SEMAPHORE)
    return pl.pallas_call(
        body,
        name=f"layer{layer}_scatter_wait",
        out_shape=pltpu.HBM(pack_thru.shape, pack_thru.dtype),
        in_specs=(hbm, hbm, *[sem] * (2 * n), pl.BlockSpec(memory_space=pl.ANY)),
        out_specs=hbm,
        input_output_aliases={1: 0},
        compiler_params=pltpu.CompilerParams(has_side_effects=pltpu.SideEffectType.DATAFLOW_SIDE_EFFECTING),
    )(pack_thru, land_thru, *sems, after)


def _sum_pieces(layer, pieces):
    rows, n = pieces.shape[1], pieces.shape[2]
    blocks = 2
    rb = rows // blocks

    def body(p_ref, out_ref):
        total = p_ref[0].astype(F32)
        for j in range(1, N_DEV):
            total = total + p_ref[j].astype(F32)
        out_ref[...] = total

    return pl.pallas_call(
        body,
        name=f"layer{layer}_grad_sum",
        grid=(blocks,),
        out_shape=pltpu.HBM((rows, n), F32),
        in_specs=[pl.BlockSpec((N_DEV, rb, n), lambda i: (0, i, 0))],
        out_specs=pl.BlockSpec((rb, n), lambda i: (i, 0)),
        compiler_params=pltpu.CompilerParams(dimension_semantics=("arbitrary",), vmem_limit_bytes=32 * MIB),
    )(pieces)


def _weight_grads(layer, dproj, hn, cat, dx1, r, dgpre, dpp, p_all, scatter_pack=None):
    t = hn.shape[0]
    tk = _tile(t, 512)
    nt = t // tk
    in_blocks = PROJ_WIDTH // 512
    scatters = scatter_pack is not None

    def body(*refs):
        (dproj_ref, hn_ref, cat_ref, dx1_ref, r_ref, dgpre_ref, dpp_ref, p_ref) = refs[:8]
        refs = refs[8:]
        if scatters:
            prior_ref, refs = refs[0], refs[1:]
        pack_ref, refs = refs[0], refs[1:]
        if scatters:
            pieces_ref, refs = refs[0], refs[1:]
        (acc_in, acc_out, acc_gate, acc_proj, stage, sems) = refs[:6]
        i = pl.program_id(0)
        if scatters:
            scatter = _DirectScatter(prior_ref, pieces_ref, *refs[6:9])

            @pl.when(i == 0)
            def _():
                scatter.start()

        @pl.when(i == 0)
        def _():
            acc_in[...] = jnp.zeros_like(acc_in)
            acc_out[...] = jnp.zeros_like(acc_out)
            acc_gate[...] = jnp.zeros_like(acc_gate)
            acc_proj[...] = jnp.zeros_like(acc_proj)

        hnv = hn_ref[...]
        for b in range(in_blocks):
            acc_in[pl.ds(b * 512, 512), :] += _dot_tn(dproj_ref[:, b * 512:(b + 1) * 512], hnv)
        dx1v = dx1_ref[...]
        dgv = dgpre_ref[...]
        for b in range(D_MODEL // 512):
            acc_out[pl.ds(b * 512, 512), :] += _dot_tn(cat_ref[:, b * 512:(b + 1) * 512], dx1v)
            acc_gate[pl.ds(b * 512, 512), :] += _dot_tn(r_ref[:, b * 512:(b + 1) * 512], dgv)
        pv = p_ref[...].astype(BF16)
        for b in range(D_MODEL // 512):
            acc_proj[pl.ds(b * 512, 512), :] += _dot_tn(dpp_ref[:, b * 512:(b + 1) * 512], pv)

        @pl.when(i == nt - 1)
        def _():
            def out_copy(s):
                return pltpu.make_async_copy(stage.at[s % 2], pack_ref.at[s], sems.at[s % 2])

            for s in range(N_DEV):
                if s >= 2:
                    out_copy(s - 2).wait()
                buf = stage.at[s % 2]
                buf[pl.ds(OFF_IN, ROWS_IN), :] = acc_in[pl.ds(s * ROWS_IN, ROWS_IN), :].astype(BF16)
                buf[pl.ds(OFF_OUT, ROWS_OUT), :] = acc_out[pl.ds(s * ROWS_OUT, ROWS_OUT), :].astype(BF16)
                buf[pl.ds(OFF_GATE, ROWS_GATE), :] = acc_gate[pl.ds(s * ROWS_GATE, ROWS_GATE), :].astype(BF16)
                for j in range(D_MODEL // PLE_DIM):
                    buf[pl.ds(OFF_PROJ, ROWS_PROJ), pl.ds(j * PLE_DIM, PLE_DIM)] = acc_proj[
                        pl.ds(s * ROWS_OUT + j * ROWS_PROJ, ROWS_PROJ), :
                    ].astype(BF16)
                out_copy(s).start()
            out_copy(N_DEV - 2).wait()
            out_copy(N_DEV - 1).wait()
            if scatters:
                scatter.finish()

    def tok(width):
        return pl.BlockSpec((tk, width), lambda i: (i, 0))

    hbm = pl.BlockSpec(memory_space=pl.ANY)
    pack_shape = jax.ShapeDtypeStruct((N_DEV, ROWS_GRAD, D_MODEL), BF16)
    operands = [dproj, hn, cat, dx1, r, dgpre, dpp, p_all]
    in_specs = [tok(PROJ_WIDTH), tok(D_MODEL), tok(D_MODEL), tok(D_MODEL), tok(D_MODEL), tok(D_MODEL), tok(D_MODEL),
                pl.BlockSpec((None, None, tk, PLE_DIM), lambda i: (layer, 0, i, 0))]
    out_specs, out_shape = [hbm], [pack_shape]
    scratch_shapes = [
        pltpu.VMEM((PROJ_WIDTH, D_MODEL), F32),
        pltpu.VMEM((D_MODEL, D_MODEL), F32),
        pltpu.VMEM((D_MODEL, D_MODEL), F32),
        pltpu.VMEM((D_MODEL, PLE_DIM), F32),
        pltpu.VMEM((2, ROWS_GRAD, D_MODEL), BF16),
        pltpu.SemaphoreType.DMA((2,)),
    ]
    if scatters:
        operands.append(scatter_pack)
        in_specs.append(hbm)
        out_specs.append(hbm)
        out_shape.append(pack_shape)
        scratch_shapes += list(SCATTER_SEMS)

    return pl.pallas_call(
        body,
        name=f"layer{layer}_weight_grads",
        grid=(nt,),
        in_specs=in_specs,
        out_specs=out_specs,
        out_shape=out_shape,
        scratch_shapes=scratch_shapes,
        compiler_params=pltpu.CompilerParams(dimension_semantics=("arbitrary",), vmem_limit_bytes=58 * MIB),
    )(*operands)


def _reduce_scatter_all_reduce(layer, pack, smalls, head, dws):
    rows, n = pack.shape[1], pack.shape[2]
    assert DEPTH * WIDTH_A == D_MODEL and n == D_MODEL

    def body(g_ref, *refs):
        small_refs, refs = refs[:DEPTH], refs[DEPTH:]
        head_ref, refs = refs[0], refs[1:]
        dws_refs, refs = refs[:DEPTH], refs[DEPTH:]
        (out_ref, total_ref, r1, a_s, r2, via, sp, sr1, sq, send1, recv1, send2, recv2, ssend, srecv) = refs
        x, y, c = lax.axis_index("x"), lax.axis_index("y"), lax.axis_index("c")
        sibling = (x, y, 1 - c)
        chip = 2 * x + y
        flips = [(1, 0), (0, 1), (1, 1)]

        for l in range(DEPTH):
            sp[l * SMALL_ROWS:(l + 1) * SMALL_ROWS, :] = small_refs[l][...]
            sp[TOTAL_WS:TOTAL_ROWS, l * WIDTH_A:(l + 1) * WIDTH_A] = dws_refs[l][...]
        sp[TOTAL_HEAD:TOTAL_WS, :] = head_ref[...]

        small_pair = pltpu.make_async_remote_copy(
            src_ref=sp, dst_ref=sr1, send_sem=ssend.at[0], recv_sem=srecv.at[0], device_id=sibling, device_id_type=MESH
        )

        def to_sibling(j):
            return pltpu.make_async_remote_copy(
                src_ref=g_ref.at[2 * j + 1 - c], dst_ref=r1.at[j], send_sem=send1.at[j], recv_sem=recv1.at[j],
                device_id=sibling, device_id_type=MESH,
            )

        first = [to_sibling(j) for j in range(4)]
        small_pair.start()
        for cp in first:
            cp.start()

        small_pair.wait_recv()
        sq[chip] = sp[...] + sr1[...]
        small_chips = [
            pltpu.make_async_remote_copy(
                src_ref=sq.at[chip], dst_ref=sq.at[chip], send_sem=ssend.at[1 + k], recv_sem=srecv.at[1 + k],
                device_id=(x ^ fx, y ^ fy, c), device_id_type=MESH,
            )
            for k, (fx, fy) in enumerate(flips)
        ]
        for cp in small_chips:
            cp.start()

        for j in range(4):
            first[j].wait_recv()

            @pl.when(chip != j)
            def _():
                a_s[j] = (g_ref[2 * j + c].astype(F32) + r1[j].astype(F32)).astype(BF16)

        half = rows // 2
        lo, hi = pl.ds(0, half), pl.ds(half, rows - half)
        x_nbr, y_nbr = (1 - x, y, c), (x, 1 - y, c)
        chip_x, chip_y, chip_d = 2 * (1 - x) + y, 2 * x + (1 - y), 2 * (1 - x) + (1 - y)

        def ici(k, src, dst, to):
            return pltpu.make_async_remote_copy(
                src_ref=src, dst_ref=dst, send_sem=send2.at[k], recv_sem=recv2.at[k], device_id=to, device_id_type=MESH)

        second = [
            ici(0, a_s.at[chip_d, lo, :], via.at[0], x_nbr),
            ici(1, a_s.at[chip_d, hi, :], via.at[1], y_nbr),
            ici(2, a_s.at[chip_x, lo, :], r2.at[0, lo, :], x_nbr),
            ici(3, a_s.at[chip_y, hi, :], r2.at[1, hi, :], y_nbr),
            ici(4, a_s.at[chip_x, hi, :], r2.at[0, hi, :], x_nbr),
            ici(5, a_s.at[chip_y, lo, :], r2.at[1, lo, :], y_nbr),
        ]
        for cp in second[:4]:
            cp.start()

        out_ref[...] = g_ref[2 * chip + c].astype(F32) + r1[chip].astype(F32)
        for cp in small_chips:
            cp.wait_recv()
        total_ref[...] = ((sq[0] + sq[1]) + sq[2]) + sq[3]

        second[0].wait_recv()
        a_s[chip_y, lo, :] = (a_s[chip_y, lo, :].astype(F32) + via[0].astype(F32)).astype(BF16)
        second[5].start()
        second[1].wait_recv()
        a_s[chip_x, hi, :] = (a_s[chip_x, hi, :].astype(F32) + via[1].astype(F32)).astype(BF16)
        second[4].start()

        second[2].wait_recv()
        second[4].wait_recv()
        out_ref[...] += r2[0].astype(F32)
        second[3].wait_recv()
        second[5].wait_recv()
        out_ref[...] += r2[1].astype(F32)
        small_pair.wait_send()
        for cp in first + small_chips + second:
            cp.wait_send()

    vmem = pl.BlockSpec(memory_space=pltpu.VMEM)
    return pl.pallas_call(
        body,
        name=f"layer{layer}_grad_reduce_scatter",
        out_shape=[jax.ShapeDtypeStruct((rows, n), F32), jax.ShapeDtypeStruct((TOTAL_ROWS, D_MODEL), F32)],
        in_specs=[vmem] * (2 + 2 * DEPTH),
        out_specs=[vmem, vmem],
        scratch_shapes=[
            pltpu.VMEM((4, rows, n), BF16),
            pltpu.VMEM((4, rows, n), BF16),
            pltpu.VMEM((2, rows, n), BF16),
            pltpu.VMEM((2, rows // 2, n), BF16),
            pltpu.VMEM((TOTAL_ROWS, D_MODEL), F32),
            pltpu.VMEM((TOTAL_ROWS, D_MODEL), F32),
            pltpu.VMEM((4, TOTAL_ROWS, D_MODEL), F32),
            pltpu.SemaphoreType.DMA((4,)),
            pltpu.SemaphoreType.DMA((4,)),
            pltpu.SemaphoreType.DMA((6,)),
            pltpu.SemaphoreType.DMA((6,)),
            pltpu.SemaphoreType.DMA((4,)),
            pltpu.SemaphoreType.DMA((4,)),
        ],
        compiler_params=pltpu.CompilerParams(vmem_limit_bytes=48 * MIB),
    )(pack, *smalls, head, *dws)


def _adam_step(w, g, m, v):
    m = ADAM_B1 * m + (1.0 - ADAM_B1) * g
    v = ADAM_B2 * v + (1.0 - ADAM_B2) * (g * g)
    m_hat = m / (1.0 - ADAM_B1 ** ADAM_STEP)
    v_hat = v / (1.0 - ADAM_B2 ** ADAM_STEP)
    return -ADAM_LR * (m_hat / (jnp.sqrt(v_hat) + ADAM_EPS) + ADAM_WD * w), m, v


def _adamw_rows(name, reduced, row_off, states):
    n = len(states)

    def body(*refs):
        red = refs[:DEPTH]
        ins = refs[DEPTH:DEPTH + 3 * n]
        outs = refs[DEPTH + 3 * n:]
        layer = pl.program_id(0)
        for l in range(DEPTH):
            @pl.when(layer == l)
            def _():
                for k in range(n):
                    w_ref, m_ref, v_ref = ins[3 * k:3 * k + 3]
                    g_ref, d_ref, nm_ref, nv_ref = outs[4 * k:4 * k + 4]
                    g = red[l][row_off[k]:row_off[k] + w_ref.shape[0], :]
                    d, m, v = _adam_step(w_ref[...], g, m_ref[...], v_ref[...])
                    g_ref[...] = g
                    d_ref[...] = d
                    nm_ref[...] = m
                    nv_ref[...] = v

    flat = [a for st in states for a in st]
    state_specs, out_specs, out_shape = [], [], []
    for w, _, _ in states:
        spec = pl.BlockSpec((None,) + w.shape[1:], lambda l: (l, 0, 0))
        state_specs += [spec] * 3
        out_specs += [spec] * 4
        out_shape += [jax.ShapeDtypeStruct(w.shape, F32)] * 4
    red_specs = [pl.BlockSpec(a.shape, lambda l: (0, 0)) for a in reduced]
    operands = [pltpu.with_memory_space_constraint(a, pltpu.HBM) for a in (*reduced, *flat)]
    outs = pl.pallas_call(
        body,
        name=name,
        grid=(DEPTH,),
        out_shape=[pltpu.HBM(a.shape, a.dtype) for a in out_shape],
        in_specs=red_specs + state_specs,
        out_specs=out_specs,
        compiler_params=pltpu.CompilerParams(dimension_semantics=("arbitrary",), vmem_limit_bytes=48 * MIB),
    )(*operands)
    return [tuple(outs[4 * k:4 * k + 4]) for k in range(n)]


def _adamw_small(total, g_conv, g_proj, st):
    names = ["norm_g", "ple_norm_g", "ln_v_g", "ln_v_b", "b_s", "w_s", "final_g", "conv_w", "w_ple_proj"]
    cut = names[:7]

    def body(total_ref, gconv_ref, gproj_ref, *refs):
        ins = {nm: refs[3 * k:3 * k + 3] for k, nm in enumerate(names)}
        outs, pos = {}, 3 * len(names)
        for nm in names:
            cnt = 4 if nm in cut else 3
            outs[nm] = refs[pos:pos + cnt]
            pos += cnt

        def update(nm, idx, g):
            w_ref, m_ref, v_ref = ins[nm]
            d, m, v = _adam_step(w_ref[idx], g, m_ref[idx], v_ref[idx])
            o = outs[nm]
            if nm in cut:
                o[0][idx] = g
                o = o[1:]
            o[0][idx] = d
            o[1][idx] = m
            o[2][idx] = v

        tril = (lax.broadcasted_iota(jnp.int32, (CHUNK, CHUNK), 0) >= lax.broadcasted_iota(jnp.int32, (CHUNK, CHUNK), 1))
        for l in range(DEPTH):
            base = l * SMALL_ROWS
            row = (slice(l, l + 1), slice(None))
            update("norm_g", row, total_ref[base + SMALL_NORM:base + SMALL_NORM + 1, :])
            update("ple_norm_g", row, total_ref[base + SMALL_PLE:base + SMALL_PLE + 1, :])
            update("ln_v_g", row, total_ref[base + SMALL_LN:base + SMALL_LN + 1, 0:WIDTH_A])
            update("ln_v_b", row, total_ref[base + SMALL_LN:base + SMALL_LN + 1, WIDTH_A:2 * WIDTH_A])
            for h in range(HEADS_A):
                update("b_s", (l, slice(h, h + 1), slice(None)),
                       total_ref[base + SMALL_BS:base + SMALL_BS + 1, h * HEAD_DIM:(h + 1) * HEAD_DIM])
                lanes = slice(l * WIDTH_A + h * CHUNK, l * WIDTH_A + (h + 1) * CHUNK)
                update("w_s", (l, h), jnp.where(tril, total_ref[TOTAL_WS:TOTAL_ROWS, lanes], 0.0))
        update("final_g", (slice(None), slice(None)), total_ref[TOTAL_HEAD + HEAD_FINAL:TOTAL_HEAD + HEAD_FINAL + 1, :])
        update("conv_w", (slice(None),) * 3, gconv_ref[...])
        update("w_ple_proj", (slice(None),) * 3, gproj_ref[...])

    flat = [a for nm in names for a in st[nm]]
    out_shape = []
    for nm in names:
        out_shape += [jax.ShapeDtypeStruct(st[nm][0].shape, F32)] * (4 if nm in cut else 3)
    def whole(a):
        return pl.BlockSpec(a.shape, lambda i: (0,) * len(a.shape))

    operands = [pltpu.with_memory_space_constraint(a, pltpu.HBM) for a in (total, g_conv, g_proj, *flat)]
    outs = pl.pallas_call(
        body,
        name="adamw_small",
        grid=(1,),
        out_shape=[pltpu.HBM(a.shape, a.dtype) for a in out_shape],
        in_specs=[whole(a) for a in operands],
        out_specs=[whole(a) for a in out_shape],
        compiler_params=pltpu.CompilerParams(dimension_semantics=("arbitrary",), vmem_limit_bytes=32 * MIB),
    )(*operands)
    res, pos = {}, 0
    for nm in names:
        cnt = 4 if nm in cut else 3
        got = tuple(outs[pos:pos + cnt])
        res[nm] = got if nm in cut else ((g_conv if nm == "conv_w" else g_proj),) + got
        pos += cnt
    return res


def _split3_bf16(a):
    b1 = a.astype(BF16)
    r1 = a - b1.astype(F32)
    b2 = r1.astype(BF16)
    b3 = (r1 - b2.astype(F32)).astype(BF16)
    return b1, b2, b3


def _pack_weight_shard(w_in_l, w_out_l, w_gate_l, w_proj_l, conv_w_l):
    w_in_t = jnp.transpose(w_in_l).astype(BF16)
    proj_t = jnp.transpose(w_proj_l).astype(BF16)
    proj_rows = proj_t.reshape(D_MODEL // PLE_DIM, ROWS_PROJ, PLE_DIM).transpose(1, 0, 2).reshape(ROWS_PROJ, D_MODEL)
    conv_parts = jnp.concatenate([b.reshape(-1) for b in _split3_bf16(conv_w_l)])
    conv_rows = jnp.concatenate([conv_parts, jnp.zeros((ROWS_CONV * D_MODEL - conv_parts.shape[0],), BF16)])
    return jnp.concatenate(
        [w_in_t, w_out_l.astype(BF16), w_gate_l.astype(BF16), proj_rows, conv_rows.reshape(ROWS_CONV, D_MODEL)], axis=0
    )


def _unpack_conv(wg):
    per_dev = wg.reshape(N_DEV, ROWS_LAYER, D_MODEL)
    n_conv = (WIDTH_B // N_DEV) * 3
    conv_parts = per_dev[:, OFF_CONV].astype(F32)[:, :3 * n_conv].reshape(N_DEV, 3, n_conv)
    conv = (conv_parts[:, 0] + conv_parts[:, 1]) + conv_parts[:, 2]
    conv_k = jnp.transpose(conv.reshape(WIDTH_B, 3))
    conv_k = jnp.concatenate([conv_k, jnp.zeros((5, WIDTH_B), F32)], axis=0)
    return conv_k


def _unpack_grad_proj(red):
    proj_rows = red[OFF_PROJ:OFF_PROJ + ROWS_PROJ]
    proj_t = proj_rows.reshape(ROWS_PROJ, D_MODEL // PLE_DIM, PLE_DIM).transpose(1, 0, 2).reshape(ROWS_OUT, PLE_DIM)
    return jnp.transpose(proj_t)


def kernel(x, p, norm_g, w_in, ln_v_g, ln_v_b, w_s, b_s, conv_w, w_out, ple_norm_g, w_ple_gate, w_ple_proj, final_g, loss_target, m_norm_g, m_w_in, m_ln_v_g, m_ln_v_b, m_w_s, m_b_s, m_conv_w, m_w_out, m_ple_norm_g, m_w_ple_gate, m_w_ple_proj, m_final_g, v_norm_g, v_w_in, v_ln_v_g, v_ln_v_b, v_w_s, v_b_s, v_conv_w, v_w_out, v_ple_norm_g, v_w_ple_gate, v_w_ple_proj, v_final_g):
    me = 4 * lax.axis_index("x") + 2 * lax.axis_index("y") + lax.axis_index("c")
    xs = x[0]
    target = loss_target[0]

    shards = [_pack_weight_shard(w_in[l], w_out[l], w_ple_gate[l], w_ple_proj[l], conv_w[l]) for l in range(DEPTH)]
    tril = jnp.tril(jnp.ones((CHUNK, CHUNK), F32))

    def consts(l, wg_l):
        conv_k = _unpack_conv(wg_l)
        w_mix = w_s[l] * tril[None]
        small = dict(
            conv_k=conv_k,
            norm_g=norm_g[l].reshape(1, D_MODEL), ln_g=ln_v_g[l].reshape(1, WIDTH_A), ln_b=ln_v_b[l].reshape(1, WIDTH_A),
            w_mix=w_mix.astype(BF16), w_mix_t=jnp.swapaxes(w_mix, 1, 2).astype(BF16),
            b_mix=jnp.broadcast_to(b_s[l][:, :, None], (HEADS_A, CHUNK, HEAD_DIM)),
            ple_g=ple_norm_g[l].reshape(1, D_MODEL),
        )
        return dict({k: pltpu.with_memory_space_constraint(a, pltpu.HBM) for k, a in small.items()}, wg=wg_l)

    layer_consts = [consts(0, _all_gather_rows(shards[0]))]
    saved = []
    h = xs
    for l in range(DEPTH):
        k = layer_consts[l]
        outs = _forward_layer(
            l, h, p, k["wg"], k["conv_k"], k["norm_g"], k["ln_g"], k["ln_b"], k["w_mix"], k["b_mix"],
            k["ple_g"], next_shard=shards[l + 1] if l + 1 < DEPTH else None)
        proj, hn, cat, r, gpre, x1, x2 = outs[:7]
        if l + 1 < DEPTH:
            layer_consts.append(consts(l + 1, outs[7]))
        saved.append(dict(x_in=h, proj=proj, hn=hn, cat=cat, r=r, gpre=gpre, x1=x1))
        h = x2

    smalls, dws = [None] * DEPTH, [None] * DEPTH
    reduced = [None] * DEPTH
    pending = None
    dx = h
    for l in reversed(range(DEPTH)):
        k, s = layer_consts[l], saved[l]
        outs = _backward_layer(
            l, dx, s["x_in"], s["x1"], s["proj"], s["gpre"], p, k["wg"], k["conv_k"],
            k["norm_g"], k["ln_g"], k["ln_b"], k["w_mix"], k["w_mix_t"], k["b_mix"], k["ple_g"],
            loss_head=(target, final_g.reshape(1, D_MODEL)) if l == DEPTH - 1 else None)
        dx, dproj, dx1, dgpre, dpp, smalls[l], dws[l] = outs[:7]
        if l == DEPTH - 1:
            head = outs[7]
        (pack,) = _weight_grads(l, dproj, s["hn"], s["cat"], dx1, s["r"], dgpre, dpp, p)
        if pending is not None:
            sems, pack_thru, land_thru = pending
            pieces = _scatter_wait(l + 1, sems, pack_thru, land_thru, pack)
            own = lax.dynamic_index_in_dim(pack_thru, me, 0, keepdims=False)
            reduced[l + 1] = _sum_pieces(l + 1, lax.dynamic_update_index_in_dim(pieces, own, me, 0))
        pending = _scatter_start(l, pack) if l > 0 else pack
    reduced[0], total = _reduce_scatter_all_reduce(0, pending, smalls, head, dws)
    grad_x = dx[None]
    loss = total[TOTAL_HEAD + HEAD_LOSS, 0]

    n_ch = WIDTH_B // N_DEV
    g_conv = jnp.stack([total[l * SMALL_ROWS + SMALL_CONV:l * SMALL_ROWS + SMALL_CONV + 3, 0:WIDTH_B] for l in range(DEPTH)], axis=1)
    g_conv = lax.dynamic_slice_in_dim(g_conv, me * n_ch, n_ch, axis=2)
    g_proj = jnp.stack([_unpack_grad_proj(reduced[l]) for l in range(DEPTH)])

    def t_in(a):
        return jnp.swapaxes(a, 1, 2)

    def t_conv(a):
        return jnp.transpose(a, (2, 0, 1))

    (r_in,) = _adamw_rows("adamw_w_in", reduced, [OFF_IN], [(t_in(w_in), t_in(m_w_in), t_in(v_w_in))])
    r_out, r_gate = _adamw_rows(
        "adamw_w_out_gate", reduced, [OFF_OUT, OFF_GATE],
        [(w_out, m_w_out, v_w_out), (w_ple_gate, m_w_ple_gate, v_w_ple_gate)])
    small = _adamw_small(total, g_conv, g_proj, dict(
        norm_g=(norm_g, m_norm_g, v_norm_g), ple_norm_g=(ple_norm_g, m_ple_norm_g, v_ple_norm_g),
        ln_v_g=(ln_v_g, m_ln_v_g, v_ln_v_g), ln_v_b=(ln_v_b, m_ln_v_b, v_ln_v_b),
        b_s=(b_s, m_b_s, v_b_s), w_s=(w_s, m_w_s, v_w_s),
        final_g=tuple(a.reshape(1, D_MODEL) for a in (final_g, m_final_g, v_final_g)),
        conv_w=(t_conv(conv_w), t_conv(m_conv_w), t_conv(v_conv_w)),
        w_ple_proj=(w_ple_proj, m_w_ple_proj, v_w_ple_proj),
    ))
    res = dict(small, w_in=tuple(t_in(a) for a in r_in), w_out=r_out, w_ple_gate=r_gate)
    res["final_g"] = tuple(a.reshape(D_MODEL) for a in res["final_g"])
    res["conv_w"] = tuple(jnp.transpose(a, (1, 2, 0)) for a in res["conv_w"])
    order = ["norm_g", "w_in", "ln_v_g", "ln_v_b", "w_s", "b_s", "conv_w", "w_out", "ple_norm_g", "w_ple_gate", "w_ple_proj", "final_g"]
    return (loss, grad_x, *[res[n][0] for n in order], *[res[n][1] for n in order],
            *[res[n][2] for n in order], *[res[n][3] for n in order])
```

```python
import jax
import jax.numpy as jnp
from jax import lax
from jax.experimental import pallas as pl
from jax.experimental.pallas import tpu as pltpu

F32 = jnp.float32
BF16 = jnp.bfloat16

D_MODEL = 1024
WIDTH_A = 512
WIDTH_B = 512
HEADS_A = 4
HEAD_DIM = 128
CHUNK = 128
PLE_DIM = 256
PROJ_WIDTH = 3584
DEPTH = 2
EPS = 1e-6
N_DEV = 8

ADAM_LR = 0.001
ADAM_B1 = 0.9
ADAM_B2 = 0.999
ADAM_EPS = 1e-08
ADAM_WD = 0.01
ADAM_STEP = 10

ROWS_IN = PROJ_WIDTH // N_DEV
ROWS_OUT = D_MODEL // N_DEV
ROWS_GATE = D_MODEL // N_DEV
ROWS_PROJ = (D_MODEL // N_DEV) * PLE_DIM // D_MODEL
ROWS_CONV = 16
OFF_IN = 0
OFF_OUT = OFF_IN + ROWS_IN
OFF_GATE = OFF_OUT + ROWS_OUT
OFF_PROJ = OFF_GATE + ROWS_GATE
OFF_CONV = OFF_PROJ + ROWS_PROJ
ROWS_GRAD = OFF_CONV
ROWS_LAYER = OFF_CONV + ROWS_CONV

SMALL_ROWS = 8
SMALL_NORM = 0
SMALL_PLE = 1
SMALL_LN = 2
SMALL_BS = 3
SMALL_CONV = 4
HEAD_FINAL = 0
HEAD_LOSS = 1
TOTAL_HEAD = DEPTH * SMALL_ROWS
TOTAL_WS = TOTAL_HEAD + SMALL_ROWS
TOTAL_ROWS = TOTAL_WS + CHUNK

MIB = 1024 * 1024
MESH = pl.DeviceIdType.MESH

NT_DIMS = (((1,), (1,)), ((), ()))
TN_DIMS = (((0,), (0,)), ((), ()))


def _dot(a, b):
    return jnp.dot(a, b, preferred_element_type=F32)


def _dot_nt(a, b):
    return lax.dot_general(a, b, NT_DIMS, preferred_element_type=F32)


def _dot_tn(a, b):
    return lax.dot_general(a, b, TN_DIMS, preferred_element_type=F32)


def _colsum8(a):
    rows, n = a.shape
    return jnp.sum(a.reshape(rows // 8, 8, n), axis=0)


def _sigmoid(z):
    return 1.0 / (1.0 + jnp.exp(-z))


def _tile(t, want):
    return want if t % want == 0 else t


class _TwoLevelGather:
    def __init__(self, x_ref, out_ref, m_per, send_sems, recv_sems, local_sem):
        x, y, c = lax.axis_index("x"), lax.axis_index("y"), lax.axis_index("c")
        self.me, self.sibling = (x, y, c), (x, y, 1 - c)
        self.xn, self.yn, self.diag = (1 - x, y, c), (x, 1 - y, c), (1 - x, 1 - y, c)
        self.x_ref, self.out_ref, self.m_per = x_ref, out_ref, m_per
        self.half = (m_per // 32) * 16
        self.send_sems, self.recv_sems = send_sems, recv_sems
        self.mine = pltpu.make_async_copy(x_ref, self.rows(self.me), local_sem)

    def rows(self, block, part=None):
        px, py, pc = block
        base = (4 * px + 2 * py + pc) * self.m_per
        if part is None:
            return self.out_ref.at[pl.ds(base, self.m_per), :]
        if part == 0:
            return self.out_ref.at[pl.ds(base, self.half), :]
        return self.out_ref.at[pl.ds(base + self.half, self.m_per - self.half), :]

    def copy(self, k, block, to, src=None, part=None):
        return pltpu.make_async_remote_copy(
            src_ref=self.rows(block, part) if src is None else src,
            dst_ref=self.rows(block, part),
            send_sem=self.send_sems.at[k],
            recv_sem=self.recv_sems.at[k],
            device_id=to,
            device_id_type=MESH,
        )

    def first(self):
        return [self.copy(0, self.me, self.sibling, src=self.x_ref),
                self.copy(1, self.me, self.xn, src=self.x_ref),
                self.copy(2, self.me, self.yn, src=self.x_ref)]

    def second(self):
        return [self.copy(3, self.xn, self.yn, part=0), self.copy(7, self.yn, self.xn, part=1),
                self.copy(4, self.xn, self.sibling), self.copy(5, self.yn, self.sibling)]

    def third(self):
        return [self.copy(6, self.diag, self.sibling)]

    def start(self):
        self.mine.start()
        for cp in self.first():
            cp.start()

    def pass_on(self):
        fwd_x, fwd_y, sib_x, sib_y = self.second()
        self.copy(1, self.xn, self.me).wait_recv()
        fwd_x.start()
        sib_x.start()
        self.copy(2, self.yn, self.me).wait_recv()
        fwd_y.start()
        sib_y.start()

    def pass_on_diagonal(self):
        self.copy(3, self.diag, self.me, part=0).wait_recv()
        self.copy(7, self.diag, self.me, part=1).wait_recv()
        self.third()[0].start()

    def finish(self):
        sib = (self.sibling[0], self.sibling[1], self.sibling[2])
        self.copy(0, sib, self.me).wait_recv()
        for k, chip in ((4, self.xn), (5, self.yn), (6, self.diag)):
            self.copy(k, (chip[0], chip[1], sib[2]), self.me).wait_recv()
        for cp in self.first() + self.second() + self.third():
            cp.wait_send()
        self.mine.wait()


GATHER_SEMS = [pltpu.SemaphoreType.DMA((8,)), pltpu.SemaphoreType.DMA((8,)), pltpu.SemaphoreType.DMA]


def _all_gather_rows(shard):
    m_per, n = shard.shape

    def body(x_ref, out_ref, send_sems, recv_sems, local_sem):
        ag = _TwoLevelGather(x_ref, out_ref, m_per, send_sems, recv_sems, local_sem)
        ag.start()
        ag.pass_on()
        ag.pass_on_diagonal()
        ag.finish()

    return pl.pallas_call(
        body,
        name="weights_all_gather",
        out_shape=pltpu.HBM((N_DEV * m_per, n), shard.dtype),
        in_specs=[pl.BlockSpec(memory_space=pltpu.HBM)],
        out_specs=pl.BlockSpec(memory_space=pltpu.HBM),
        scratch_shapes=list(GATHER_SEMS),
    )(pltpu.with_memory_space_constraint(shard, pltpu.HBM))


PROJ_PARTS = D_MODEL // PLE_DIM
N_WEIGHT_COPIES = N_DEV * (3 + PROJ_PARTS)


def _weight_copies(wg_ref, w_in_t, w_out, w_gate, w_proj_t, sems):
    copies = []
    for s in range(N_DEV):
        base = s * ROWS_LAYER
        for dst, off, rows in ((w_in_t, OFF_IN, ROWS_IN), (w_out, OFF_OUT, ROWS_OUT), (w_gate, OFF_GATE, ROWS_GATE)):
            copies.append((wg_ref.at[pl.ds(base + off, rows), :], dst.at[pl.ds(s * rows, rows), :]))
        for j in range(PROJ_PARTS):
            copies.append((
                wg_ref.at[pl.ds(base + OFF_PROJ, ROWS_PROJ), pl.ds(j * PLE_DIM, PLE_DIM)],
                w_proj_t.at[pl.ds(s * ROWS_OUT + j * ROWS_PROJ, ROWS_PROJ), :],
            ))
    return [pltpu.make_async_copy(src, dst, sems.at[k]) for k, (src, dst) in enumerate(copies)]


def _forward_layer(layer, x, p_all, wg, conv_k, norm_g, ln_g, ln_b, w_mix, b_mix, ple_g, next_shard=None):
    t = x.shape[0]
    tm = _tile(t, 512)
    nt = t // tm
    gathers = next_shard is not None

    def body(*refs):
        (x_ref, p_ref, wg_ref, cw_ref, ng_ref, lng_ref, lnb_ref, wm_ref, bm_ref, pg_ref) = refs[:10]
        refs = refs[10:]
        if gathers:
            shard_ref, refs = refs[0], refs[1:]
        (proj_ref, hn_ref, cat_ref, r_ref, gpre_ref, x1_ref, x2_ref) = refs[:7]
        refs = refs[7:]
        if gathers:
            gathered_ref, refs = refs[0], refs[1:]
        (w_in_t, w_out, w_gate, wpt_ref, vln_s, mixed_s, halo_s, sems) = refs[:8]
        i = pl.program_id(0)
        if gathers:
            ag = _TwoLevelGather(shard_ref, gathered_ref, ROWS_LAYER, *refs[8:11])

            @pl.when(i == 0)
            def _():
                ag.start()

            @pl.when(i == (5 * nt) // 16)
            def _():
                ag.pass_on()

            @pl.when(i == nt // 2)
            def _():
                ag.pass_on_diagonal()

        @pl.when(i == 0)
        def _():
            copies = _weight_copies(wg_ref, w_in_t, w_out, w_gate, wpt_ref, sems)
            for cp in copies:
                cp.start()
            halo_s[...] = jnp.zeros_like(halo_s)
            for cp in copies:
                cp.wait()

        xv = x_ref[...]
        rstd0 = lax.rsqrt(jnp.mean(xv * xv, axis=-1, keepdims=True) + EPS)
        hn_ref[...] = (xv * rstd0 * ng_ref[...]).astype(BF16)

        def proj_section(k):
            sec = _dot_nt(hn_ref[...], w_in_t[pl.ds(k * 512, 512), :])
            proj_ref[:, k * 512:(k + 1) * 512] = sec.astype(BF16)
            return sec

        v = proj_section(1)
        mu = jnp.mean(v, axis=-1, keepdims=True)
        vc = v - mu
        var = jnp.mean(vc * vc, axis=-1, keepdims=True)
        vln = vc * lax.rsqrt(var + EPS) * lng_ref[...] + lnb_ref[...]
        vln_s[...] = vln.astype(BF16)
        for ci in range(tm // CHUNK):
            rows = pl.ds(ci * CHUNK, CHUNK)
            for h in range(HEADS_A):
                cols = pl.ds(h * HEAD_DIM, HEAD_DIM)
                mixed_s[rows, cols] = _dot(wm_ref[h], vln_s[rows, cols]) + bm_ref[h]
        u = proj_section(0)
        za = proj_section(2)
        out_a = u * mixed_s[...] * (za * _sigmoid(za))
        cat_ref[:, 0:512] = out_a.astype(BF16)

        xc = proj_section(5) * proj_section(3)
        prev = halo_s[...]
        row = lax.broadcasted_iota(jnp.int32, (tm, WIDTH_B), 0)
        xc_m1 = jnp.where(row == 0, prev[7:8, :], pltpu.roll(xc, 1, 0))
        xc_m2 = jnp.where(row == 0, prev[6:7, :], jnp.where(row == 1, prev[7:8, :], pltpu.roll(xc, 2, 0)))
        halo_s[...] = xc[tm - 8:tm, :]
        cw = cw_ref[...]
        yc = cw[0:1, :] * xc_m2 + cw[1:2, :] * xc_m1 + cw[2:3, :] * xc
        zb = proj_section(6)
        out_b = proj_section(4) * yc * (zb * _sigmoid(zb))
        cat_ref[:, 512:1024] = out_b.astype(BF16)

        x1 = xv + _dot(cat_ref[...], w_out[...])
        x1_ref[...] = x1
        rstd1 = lax.rsqrt(jnp.mean(x1 * x1, axis=-1, keepdims=True) + EPS)
        r_ref[...] = (x1 * rstd1 * pg_ref[...]).astype(BF16)
        gpre = _dot(r_ref[...], w_gate[...])
        gpre_ref[...] = gpre.astype(BF16)
        pp = _dot_nt(p_ref[...].astype(BF16), wpt_ref[...])
        x2_ref[...] = x1 + _sigmoid(gpre) * pp

        if gathers:
            @pl.when(i == nt - 1)
            def _():
                ag.finish()

    def tok(width):
        return pl.BlockSpec((tm, width), lambda i: (i, 0))

    def whole(shape):
        return pl.BlockSpec(shape, lambda i: (0,) * len(shape))

    hbm = pl.BlockSpec(memory_space=pl.ANY)
    operands = [x, p_all, wg, conv_k, norm_g, ln_g, ln_b, w_mix, b_mix, ple_g]
    in_specs = [
        tok(D_MODEL), pl.BlockSpec((None, None, tm, PLE_DIM), lambda i: (layer, 0, i, 0)), hbm,
        whole((8, WIDTH_B)), whole((1, D_MODEL)), whole((1, WIDTH_A)), whole((1, WIDTH_A)),
        whole((HEADS_A, CHUNK, CHUNK)), whole((HEADS_A, CHUNK, HEAD_DIM)), whole((1, D_MODEL)),
    ]
    out_specs = [tok(PROJ_WIDTH), tok(D_MODEL), tok(D_MODEL), tok(D_MODEL), tok(D_MODEL), tok(D_MODEL), tok(D_MODEL)]
    out_shape = [
        jax.ShapeDtypeStruct((t, PROJ_WIDTH), BF16),
        jax.ShapeDtypeStruct((t, D_MODEL), BF16),
        jax.ShapeDtypeStruct((t, D_MODEL), BF16),
        jax.ShapeDtypeStruct((t, D_MODEL), BF16),
        jax.ShapeDtypeStruct((t, D_MODEL), BF16),
        jax.ShapeDtypeStruct((t, D_MODEL), F32),
        jax.ShapeDtypeStruct((t, D_MODEL), F32),
    ]
    scratch_shapes = [
        pltpu.VMEM((PROJ_WIDTH, D_MODEL), BF16),
        pltpu.VMEM((D_MODEL, D_MODEL), BF16),
        pltpu.VMEM((D_MODEL, D_MODEL), BF16),
        pltpu.VMEM((D_MODEL, PLE_DIM), BF16),
        pltpu.VMEM((tm, WIDTH_A), BF16),
        pltpu.VMEM((tm, WIDTH_A), F32),
        pltpu.VMEM((8, WIDTH_B), F32),
        pltpu.SemaphoreType.DMA((N_WEIGHT_COPIES,)),
    ]
    if gathers:
        operands.append(pltpu.with_memory_space_constraint(next_shard, pltpu.HBM))
        in_specs.append(pl.BlockSpec(memory_space=pltpu.HBM))
        out_specs.append(pl.BlockSpec(memory_space=pltpu.HBM))
        out_shape.append(pltpu.HBM((N_DEV * ROWS_LAYER, D_MODEL), BF16))
        scratch_shapes += list(GATHER_SEMS)

    return pl.pallas_call(
        body,
        name=f"layer{layer}_forward",
        grid=(nt,),
        in_specs=in_specs,
        out_specs=out_specs,
        out_shape=out_shape,
        scratch_shapes=scratch_shapes,
        compiler_params=pltpu.CompilerParams(dimension_semantics=("arbitrary",), vmem_limit_bytes=56 * MIB),
    )(*operands)


class _DirectScatter:
    def __init__(self, pack_ref, pieces_ref, send_sems, recv_sems, local_sem):
        x, y, c = lax.axis_index("x"), lax.axis_index("y"), lax.axis_index("c")
        me = 4 * x + 2 * y + c
        self.copies = []
        for k in range(N_DEV - 1):
            fx, fy, fc = ((k + 1) >> 2) & 1, ((k + 1) >> 1) & 1, (k + 1) & 1
            tx, ty, tc = x ^ fx, y ^ fy, c ^ fc
            self.copies.append(
                pltpu.make_async_remote_copy(
                    src_ref=pack_ref.at[4 * tx + 2 * ty + tc], dst_ref=pieces_ref.at[me],
                    send_sem=send_sems.at[k], recv_sem=recv_sems.at[k],
                    device_id=(tx, ty, tc), device_id_type=MESH,
                )
            )
        self.mine = pltpu.make_async_copy(pack_ref.at[me], pieces_ref.at[me], local_sem)

    def start(self):
        self.mine.start()
        for cp in self.copies:
            cp.start()

    def finish(self):
        for cp in self.copies:
            cp.wait_recv()
        for cp in self.copies:
            cp.wait_send()
        self.mine.wait()


SCATTER_SEMS = [pltpu.SemaphoreType.DMA((N_DEV - 1,)), pltpu.SemaphoreType.DMA((N_DEV - 1,)), pltpu.SemaphoreType.DMA]


def _backward_layer(layer, dx2, x_in, x1, proj, gpre, p_all, wg, conv_k, norm_g, ln_g, ln_b,
                    w_mix, w_mix_t, b_mix, ple_g, loss_head=None):
    t = x_in.shape[0]
    tm = _tile(t, 256)
    nt = t // tm
    n_chunks = tm // CHUNK
    halo_rows = 16
    heads = loss_head is not None

    def body(*refs):
        (dx2_ref, xin_ref, x1_ref, proj_ref, halo_ref, gpre_ref, p_ref, wg_ref, cw_ref,
         ng_ref, lng_ref, lnb_ref, wm_ref, wmt_ref, bm_ref, pg_ref) = refs[:16]
        refs = refs[16:]
        if heads:
            tgt_ref, fg_ref = refs[:2]
            refs = refs[2:]
        (dxin_ref, dproj_ref, dx1_ref, dgpre_ref, dpp_ref, small_ref, dws_ref) = refs[:7]
        refs = refs[7:]
        if heads:
            head_ref, refs = refs[0], refs[1:]
        (w_in_t, w_out, w_gate, wpt_ref, vln_s, mixed_s, dmix_s, dvln_s, carry_s,
         ng_acc, pg_acc, lng_acc, lnb_acc, cw_acc, dbm_ref, sems) = refs[:16]
        if heads:
            loss_acc, fg_acc = refs[16:18]
        i = pl.program_id(0)
        tile = nt - 1 - i

        @pl.when(i == 0)
        def _():
            copies = _weight_copies(wg_ref, w_in_t, w_out, w_gate, wpt_ref, sems)
            for cp in copies:
                cp.start()
            if heads:
                loss_acc[...] = jnp.zeros_like(loss_acc)
                fg_acc[...] = jnp.zeros_like(fg_acc)
            carry_s[...] = jnp.zeros_like(carry_s)
            ng_acc[...] = jnp.zeros_like(ng_acc)
            pg_acc[...] = jnp.zeros_like(pg_acc)
            lng_acc[...] = jnp.zeros_like(lng_acc)
            lnb_acc[...] = jnp.zeros_like(lnb_acc)
            cw_acc[...] = jnp.zeros_like(cw_acc)
            dws_ref[...] = jnp.zeros_like(dws_ref)
            dbm_ref[...] = jnp.zeros_like(dbm_ref)
            for cp in copies:
                cp.wait()

        if heads:
            x2v = dx2_ref[...]
            fg = fg_ref[...]
            rstdf = lax.rsqrt(jnp.mean(x2v * x2v, axis=-1, keepdims=True) + EPS)
            xhatf = x2v * rstdf
            err = xhatf * fg - tgt_ref[...]
            loss_acc[...] += _colsum8(err * err)
            dy = err * (1.0 / D_MODEL)
            fg_acc[...] += _colsum8(dy * xhatf)
            dxhf = dy * fg
            dx2v = rstdf * (dxhf - xhatf * jnp.mean(dxhf * xhatf, axis=-1, keepdims=True))
        else:
            dx2v = dx2_ref[...]

        gate = _sigmoid(gpre_ref[...].astype(F32))
        pp = _dot_nt(p_ref[...].astype(BF16), wpt_ref[...])
        dpp = dx2v * gate
        dpp_ref[...] = dpp.astype(BF16)
        dgpre = (dpp * pp * (1.0 - gate)).astype(BF16)
        dgpre_ref[...] = dgpre
        dr = _dot_nt(dgpre, w_gate[...])
        x1v = x1_ref[...]
        rstd1 = lax.rsqrt(jnp.mean(x1v * x1v, axis=-1, keepdims=True) + EPS)
        xhat1 = x1v * rstd1
        pg_acc[...] += _colsum8(dr * xhat1)
        dxh = dr * pg_ref[...]
        dx1 = dx2v + rstd1 * (dxh - xhat1 * jnp.mean(dxh * xhat1, axis=-1, keepdims=True))
        dx1b = dx1.astype(BF16)
        dx1_ref[...] = dx1b

        dcat = _dot_nt(dx1b, w_out[...])
        dca = dcat[:, 0:512]
        dcb = dcat[:, 512:1024]

        u = proj_ref[:, 0:512]
        v = proj_ref[:, 512:1024].astype(F32)
        za = proj_ref[:, 1024:1536]
        mu = jnp.mean(v, axis=-1, keepdims=True)
        vc = v - mu
        var = jnp.mean(vc * vc, axis=-1, keepdims=True)
        rs = lax.rsqrt(var + EPS)
        vhat = vc * rs
        lng = lng_ref[...]
        vln_s[...] = (vhat * lng + lnb_ref[...]).astype(BF16)
        for ci in range(n_chunks):
            rows = pl.ds(ci * CHUNK, CHUNK)
            for h in range(HEADS_A):
                cols = pl.ds(h * HEAD_DIM, HEAD_DIM)
                mixed_s[rows, cols] = (_dot(wm_ref[h], vln_s[rows, cols]) + bm_ref[h]).astype(BF16)
        mixed = mixed_s[...]
        sga = _sigmoid(za)
        sa = za * sga
        dsa = sga + sa * (1.0 - sga)

        def put_section(k, val):
            dproj_ref[:, k * 512:(k + 1) * 512] = val.astype(BF16)

        dcab = dca.astype(BF16)
        dca_sa = dcab * sa
        put_section(0, dca_sa * mixed)
        dmix_s[...] = dca_sa * u
        put_section(2, (dcab * dsa) * (u * mixed))
        dbm_acc = jnp.zeros((CHUNK, WIDTH_A), F32)
        for ci in range(n_chunks):
            rows = pl.ds(ci * CHUNK, CHUNK)
            dbm_acc = dbm_acc + dmix_s[rows, :].astype(F32)
            for h in range(HEADS_A):
                cols = pl.ds(h * HEAD_DIM, HEAD_DIM)
                dvln_s[rows, cols] = _dot(wmt_ref[h], dmix_s[rows, cols])
                dws_ref[:, cols] += _dot_nt(dmix_s[rows, cols], vln_s[rows, cols])
        dbm_ref[...] += dbm_acc
        dvln = dvln_s[...]
        lng_acc[...] += _colsum8(dvln * vhat)
        lnb_acc[...] += _colsum8(dvln)
        dvh = dvln * lng
        dv = rs * (dvh - jnp.mean(dvh, axis=-1, keepdims=True) - vhat * jnp.mean(dvh * vhat, axis=-1, keepdims=True))
        put_section(1, dv)

        hb = proj_ref[:, 1536:2048].astype(F32)
        gb = proj_ref[:, 2048:2560]
        gc = proj_ref[:, 2560:3072].astype(F32)
        zb = proj_ref[:, 3072:3584]
        xc = gc * hb
        prev = halo_ref[:, 2560:3072].astype(F32) * halo_ref[:, 1536:2048].astype(F32)
        prev = jnp.where(tile > 0, prev, 0.0)
        row = lax.broadcasted_iota(jnp.int32, (tm, WIDTH_B), 0)
        p1 = prev[halo_rows - 1:halo_rows, :]
        p2 = prev[halo_rows - 2:halo_rows - 1, :]
        xc_m1 = jnp.where(row == 0, p1, pltpu.roll(xc, 1, 0))
        xc_m2 = jnp.where(row == 0, p2, jnp.where(row == 1, p1, pltpu.roll(xc, 2, 0)))
        cw = cw_ref[...]
        yc = cw[0:1, :] * xc_m2 + cw[1:2, :] * xc_m1 + cw[2:3, :] * xc
        sgb = _sigmoid(zb)
        sb = zb * sgb
        dsb = sgb + sb * (1.0 - sgb)
        dcbb = dcb.astype(BF16)
        ycb = yc.astype(BF16)
        dcb_sb = dcbb * sb
        put_section(4, dcb_sb * ycb)
        dyc = (dcb_sb * gb).astype(F32)
        put_section(6, (dcbb * dsb) * (gb * ycb))
        nxt = carry_s[...]
        dyc_p1 = jnp.where(row == tm - 1, nxt[0:1, :], pltpu.roll(dyc, tm - 1, 0))
        dyc_p2 = jnp.where(row == tm - 1, nxt[1:2, :], jnp.where(row == tm - 2, nxt[0:1, :], pltpu.roll(dyc, tm - 2, 0)))
        carry_s[...] = dyc[0:8, :]
        dxc = cw[2:3, :] * dyc + cw[1:2, :] * dyc_p1 + cw[0:1, :] * dyc_p2
        cw_acc[0] += _colsum8(dyc * xc_m2)
        cw_acc[1] += _colsum8(dyc * xc_m1)
        cw_acc[2] += _colsum8(dyc * xc)
        put_section(3, dxc * gc)
        put_section(5, dxc * hb)

        dhn = _dot(dproj_ref[...], w_in_t[...])
        xv = xin_ref[...]
        rstd0 = lax.rsqrt(jnp.mean(xv * xv, axis=-1, keepdims=True) + EPS)
        xhat0 = xv * rstd0
        ng_acc[...] += _colsum8(dhn * xhat0)
        dxh0 = dhn * ng_ref[...]
        dxin_ref[...] = dx1 + rstd0 * (dxh0 - xhat0 * jnp.mean(dxh0 * xhat0, axis=-1, keepdims=True))

        @pl.when(i == nt - 1)
        def _():
            small_ref[...] = jnp.zeros_like(small_ref)
            small_ref[SMALL_NORM:SMALL_NORM + 1, :] = jnp.sum(ng_acc[...], axis=0, keepdims=True)
            small_ref[SMALL_PLE:SMALL_PLE + 1, :] = jnp.sum(pg_acc[...], axis=0, keepdims=True)
            small_ref[SMALL_LN:SMALL_LN + 1, 0:WIDTH_A] = jnp.sum(lng_acc[...], axis=0, keepdims=True)
            small_ref[SMALL_LN:SMALL_LN + 1, WIDTH_A:2 * WIDTH_A] = jnp.sum(lnb_acc[...], axis=0, keepdims=True)
            for h in range(HEADS_A):
                cols = pl.ds(h * HEAD_DIM, HEAD_DIM)
                small_ref[SMALL_BS:SMALL_BS + 1, cols] = jnp.sum(jnp.transpose(dbm_ref[:, cols]), axis=0, keepdims=True)
            for k in range(3):
                small_ref[SMALL_CONV + k:SMALL_CONV + k + 1, 0:WIDTH_B] = jnp.sum(cw_acc[k], axis=0, keepdims=True)
            if heads:
                total = jnp.sum(loss_acc[...]) * (0.5 / D_MODEL)
                rows8 = lax.broadcasted_iota(jnp.int32, (SMALL_ROWS, D_MODEL), 0)
                lanes8 = lax.broadcasted_iota(jnp.int32, (SMALL_ROWS, D_MODEL), 1)
                head_ref[...] = jnp.where((rows8 == HEAD_LOSS) & (lanes8 == 0), total, 0.0)
                head_ref[HEAD_FINAL:HEAD_FINAL + 1, :] = jnp.sum(fg_acc[...], axis=0, keepdims=True)

    def tok(width):
        return pl.BlockSpec((tm, width), lambda i: (nt - 1 - i, 0))

    def whole(shape):
        return pl.BlockSpec(shape, lambda i: (0,) * len(shape))

    halo_spec = pl.BlockSpec(
        (halo_rows, PROJ_WIDTH), lambda i: (jnp.maximum((nt - 1 - i) * (tm // halo_rows) - 1, 0), 0)
    )
    hbm = pl.BlockSpec(memory_space=pl.ANY)
    operands = [dx2, x_in, x1, proj, proj, gpre, p_all, wg, conv_k, norm_g, ln_g, ln_b, w_mix, w_mix_t, b_mix, ple_g]
    in_specs = [
        tok(D_MODEL), tok(D_MODEL), tok(D_MODEL), tok(PROJ_WIDTH), halo_spec, tok(D_MODEL),
        pl.BlockSpec((None, None, tm, PLE_DIM), lambda i: (layer, 0, nt - 1 - i, 0)), hbm,
        whole((8, WIDTH_B)), whole((1, D_MODEL)), whole((1, WIDTH_A)), whole((1, WIDTH_A)),
        whole((HEADS_A, CHUNK, CHUNK)), whole((HEADS_A, CHUNK, CHUNK)), whole((HEADS_A, CHUNK, HEAD_DIM)),
        whole((1, D_MODEL)),
    ]
    out_specs = [
        tok(D_MODEL), tok(PROJ_WIDTH), tok(D_MODEL), tok(D_MODEL), tok(D_MODEL),
        whole((SMALL_ROWS, D_MODEL)), whole((CHUNK, WIDTH_A)),
    ]
    out_shape = [
        jax.ShapeDtypeStruct((t, D_MODEL), F32),
        jax.ShapeDtypeStruct((t, PROJ_WIDTH), BF16),
        jax.ShapeDtypeStruct((t, D_MODEL), BF16),
        jax.ShapeDtypeStruct((t, D_MODEL), BF16),
        jax.ShapeDtypeStruct((t, D_MODEL), BF16),
        jax.ShapeDtypeStruct((SMALL_ROWS, D_MODEL), F32),
        jax.ShapeDtypeStruct((CHUNK, WIDTH_A), F32),
    ]
    scratch_shapes = [
        pltpu.VMEM((PROJ_WIDTH, D_MODEL), BF16),
        pltpu.VMEM((D_MODEL, D_MODEL), BF16),
        pltpu.VMEM((D_MODEL, D_MODEL), BF16),
        pltpu.VMEM((D_MODEL, PLE_DIM), BF16),
        pltpu.VMEM((tm, WIDTH_A), BF16),
        pltpu.VMEM((tm, WIDTH_A), BF16),
        pltpu.VMEM((tm, WIDTH_A), BF16),
        pltpu.VMEM((tm, WIDTH_A), F32),
        pltpu.VMEM((8, WIDTH_B), F32),
        pltpu.VMEM((8, D_MODEL), F32),
        pltpu.VMEM((8, D_MODEL), F32),
        pltpu.VMEM((8, WIDTH_A), F32),
        pltpu.VMEM((8, WIDTH_A), F32),
        pltpu.VMEM((3, 8, WIDTH_B), F32),
        pltpu.VMEM((CHUNK, WIDTH_A), F32),
        pltpu.SemaphoreType.DMA((N_WEIGHT_COPIES,)),
    ]
    if heads:
        operands += list(loss_head)
        in_specs += [tok(D_MODEL), whole((1, D_MODEL))]
        out_specs.append(whole((SMALL_ROWS, D_MODEL)))
        out_shape.append(jax.ShapeDtypeStruct((SMALL_ROWS, D_MODEL), F32))
        scratch_shapes += [pltpu.VMEM((8, D_MODEL), F32), pltpu.VMEM((8, D_MODEL), F32)]

    return pl.pallas_call(
        body,
        name=f"layer{layer}_backward",
        grid=(nt,),
        in_specs=in_specs,
        out_specs=out_specs,
        out_shape=out_shape,
        scratch_shapes=scratch_shapes,
        compiler_params=pltpu.CompilerParams(dimension_semantics=("arbitrary",), vmem_limit_bytes=56 * MIB),
    )(*operands)


def _scatter_targets():
    x, y, c = lax.axis_index("x"), lax.axis_index("y"), lax.axis_index("c")
    out = []
    for k in range(N_DEV - 1):
        fx, fy, fc = ((k + 1) >> 2) & 1, ((k + 1) >> 1) & 1, (k + 1) & 1
        tx, ty, tc = x ^ fx, y ^ fy, c ^ fc
        out.append((4 * tx + 2 * ty + tc, (tx, ty, tc)))
    return 4 * x + 2 * y + c, out


def _scatter_start(layer, pack):
    n = N_DEV - 1

    def body(pack_ref, land_ref, *rest):
        sems = rest[:2 * n + 1]
        me, targets = _scatter_targets()
        pltpu.make_async_copy(pack_ref.at[me], land_ref.at[me], sems[2 * n]).start()
        for k, (block, device) in enumerate(targets):
            pltpu.make_async_remote_copy(
                src_ref=pack_ref.at[block], dst_ref=land_ref.at[me], send_sem=sems[k], recv_sem=sems[n + k],
                device_id=device, device_id_type=MESH,
            ).start()

    hbm = pl.BlockSpec(memory_space=pltpu.HBM)
    sem = pl.BlockSpec(memory_space=pltpu.SEMAPHORE)
    outs = pl.pallas_call(
        body,
        name=f"layer{layer}_scatter_start",
        out_shape=(*[pltpu.SemaphoreType.DMA(())] * (2 * n + 1), pltpu.HBM(pack.shape, pack.dtype)),
        in_specs=(hbm, hbm),
        out_specs=(*[sem] * (2 * n + 1), hbm),
        input_output_aliases={1: 2 * n + 1},
        compiler_params=pltpu.CompilerParams(has_side_effects=pltpu.SideEffectType.DATAFLOW_SIDE_EFFECTING),
    )(pack, pltpu.with_memory_space_constraint(lax.empty(pack.shape, pack.dtype), pltpu.HBM))
    return outs[:2 * n + 1], pack, outs[2 * n + 1]


def _scatter_wait(layer, sems, pack_thru, land_thru, after):
    n = N_DEV - 1

    def body(pack_ref, land_ref, *rest):
        sem_refs = rest[:2 * n + 1]
        me, targets = _scatter_targets()
        pltpu.make_async_copy(pack_ref.at[me], land_ref.at[me], sem_refs[2 * n]).wait()
        for k, (block, device) in enumerate(targets):
            copy = pltpu.make_async_remote_copy(
                src_ref=pack_ref.at[block], dst_ref=land_ref.at[me], send_sem=sem_refs[k], recv_sem=sem_refs[n + k],
                device_id=device, device_id_type=MESH,
            )
            copy.wait_send()
            copy.wait_recv()

    hbm = pl.BlockSpec(memory_space=pltpu.HBM)
    sem = pl.BlockSpec(memory_space=pltpu.SEMAPHORE)
    return pl.pallas_call(
        body,
        name=f"layer{layer}_scatter_wait",
        out_shape=pltpu.HBM(pack_thru.shape, pack_thru.dtype),
        in_specs=(hbm, hbm, *[sem] * (2 * n + 1), pl.BlockSpec(memory_space=pl.ANY)),
        out_specs=hbm,
        input_output_aliases={1: 0},
        compiler_params=pltpu.CompilerParams(has_side_effects=pltpu.SideEffectType.DATAFLOW_SIDE_EFFECTING),
    )(pack_thru, land_thru, *sems, after)


def _sum_pieces(layer, pieces):
    rows, n = pieces.shape[1], pieces.shape[2]
    blocks = 2
    rb = rows // blocks

    def body(p_ref, out_ref):
        total = p_ref[0].astype(F32)
        for j in range(1, N_DEV):
            total = total + p_ref[j].astype(F32)
        out_ref[...] = total

    return pl.pallas_call(
        body,
        name=f"layer{layer}_grad_sum",
        grid=(blocks,),
        out_shape=pltpu.HBM((rows, n), F32),
        in_specs=[pl.BlockSpec((N_DEV, rb, n), lambda i: (0, i, 0))],
        out_specs=pl.BlockSpec((rb, n), lambda i: (i, 0)),
        compiler_params=pltpu.CompilerParams(dimension_semantics=("arbitrary",), vmem_limit_bytes=32 * MIB),
    )(pieces)


def _weight_grads(layer, dproj, hn, cat, dx1, r, dgpre, dpp, p_all, scatter_pack=None):
    t = hn.shape[0]
    tk = _tile(t, 512)
    nt = t // tk
    in_blocks = PROJ_WIDTH // 512
    scatters = scatter_pack is not None

    def body(*refs):
        (dproj_ref, hn_ref, cat_ref, dx1_ref, r_ref, dgpre_ref, dpp_ref, p_ref) = refs[:8]
        refs = refs[8:]
        if scatters:
            prior_ref, refs = refs[0], refs[1:]
        pack_ref, refs = refs[0], refs[1:]
        if scatters:
            pieces_ref, refs = refs[0], refs[1:]
        (acc_in, acc_out, acc_gate, acc_proj, stage, sems) = refs[:6]
        i = pl.program_id(0)
        if scatters:
            scatter = _DirectScatter(prior_ref, pieces_ref, *refs[6:9])

            @pl.when(i == 0)
            def _():
                scatter.start()

        @pl.when(i == 0)
        def _():
            acc_in[...] = jnp.zeros_like(acc_in)
            acc_out[...] = jnp.zeros_like(acc_out)
            acc_gate[...] = jnp.zeros_like(acc_gate)
            acc_proj[...] = jnp.zeros_like(acc_proj)

        hnv = hn_ref[...]
        for b in range(in_blocks):
            acc_in[pl.ds(b * 512, 512), :] += _dot_tn(dproj_ref[:, b * 512:(b + 1) * 512], hnv)
        dx1v = dx1_ref[...]
        dgv = dgpre_ref[...]
        for b in range(D_MODEL // 512):
            acc_out[pl.ds(b * 512, 512), :] += _dot_tn(cat_ref[:, b * 512:(b + 1) * 512], dx1v)
            acc_gate[pl.ds(b * 512, 512), :] += _dot_tn(r_ref[:, b * 512:(b + 1) * 512], dgv)
        pv = p_ref[...].astype(BF16)
        for b in range(D_MODEL // 512):
            acc_proj[pl.ds(b * 512, 512), :] += _dot_tn(dpp_ref[:, b * 512:(b + 1) * 512], pv)

        @pl.when(i == nt - 1)
        def _():
            def out_copy(s):
                return pltpu.make_async_copy(stage.at[s % 2], pack_ref.at[s], sems.at[s % 2])

            for s in range(N_DEV):
                if s >= 2:
                    out_copy(s - 2).wait()
                buf = stage.at[s % 2]
                buf[pl.ds(OFF_IN, ROWS_IN), :] = acc_in[pl.ds(s * ROWS_IN, ROWS_IN), :].astype(BF16)
                buf[pl.ds(OFF_OUT, ROWS_OUT), :] = acc_out[pl.ds(s * ROWS_OUT, ROWS_OUT), :].astype(BF16)
                buf[pl.ds(OFF_GATE, ROWS_GATE), :] = acc_gate[pl.ds(s * ROWS_GATE, ROWS_GATE), :].astype(BF16)
                for j in range(D_MODEL // PLE_DIM):
                    buf[pl.ds(OFF_PROJ, ROWS_PROJ), pl.ds(j * PLE_DIM, PLE_DIM)] = acc_proj[
                        pl.ds(s * ROWS_OUT + j * ROWS_PROJ, ROWS_PROJ), :
                    ].astype(BF16)
                out_copy(s).start()
            out_copy(N_DEV - 2).wait()
            out_copy(N_DEV - 1).wait()
            if scatters:
                scatter.finish()

    def tok(width):
        return pl.BlockSpec((tk, width), lambda i: (i, 0))

    hbm = pl.BlockSpec(memory_space=pl.ANY)
    pack_shape = jax.ShapeDtypeStruct((N_DEV, ROWS_GRAD, D_MODEL), BF16)
    operands = [dproj, hn, cat, dx1, r, dgpre, dpp, p_all]
    in_specs = [tok(PROJ_WIDTH), tok(D_MODEL), tok(D_MODEL), tok(D_MODEL), tok(D_MODEL), tok(D_MODEL), tok(D_MODEL),
                pl.BlockSpec((None, None, tk, PLE_DIM), lambda i: (layer, 0, i, 0))]
    out_specs, out_shape = [hbm], [pack_shape]
    scratch_shapes = [
        pltpu.VMEM((PROJ_WIDTH, D_MODEL), F32),
        pltpu.VMEM((D_MODEL, D_MODEL), F32),
        pltpu.VMEM((D_MODEL, D_MODEL), F32),
        pltpu.VMEM((D_MODEL, PLE_DIM), F32),
        pltpu.VMEM((2, ROWS_GRAD, D_MODEL), BF16),
        pltpu.SemaphoreType.DMA((2,)),
    ]
    if scatters:
        operands.append(scatter_pack)
        in_specs.append(hbm)
        out_specs.append(hbm)
        out_shape.append(pack_shape)
        scratch_shapes += list(SCATTER_SEMS)

    return pl.pallas_call(
        body,
        name=f"layer{layer}_weight_grads",
        grid=(nt,),
        in_specs=in_specs,
        out_specs=out_specs,
        out_shape=out_shape,
        scratch_shapes=scratch_shapes,
        compiler_params=pltpu.CompilerParams(dimension_semantics=("arbitrary",), vmem_limit_bytes=58 * MIB),
    )(*operands)


def _reduce_scatter_all_reduce(layer, pack, smalls, head, dws):
    rows, n = pack.shape[1], pack.shape[2]
    assert DEPTH * WIDTH_A == D_MODEL and n == D_MODEL

    def body(g_ref, *refs):
        small_refs, refs = refs[:DEPTH], refs[DEPTH:]
        head_ref, refs = refs[0], refs[1:]
        dws_refs, refs = refs[:DEPTH], refs[DEPTH:]
        (out_ref, total_ref, r1, a_s, r2, via, sp, sr1, sq, send1, recv1, send2, recv2, ssend, srecv) = refs
        x, y, c = lax.axis_index("x"), lax.axis_index("y"), lax.axis_index("c")
        sibling = (x, y, 1 - c)
        chip = 2 * x + y
        flips = [(1, 0), (0, 1), (1, 1)]

        for l in range(DEPTH):
            sp[l * SMALL_ROWS:(l + 1) * SMALL_ROWS, :] = small_refs[l][...]
            sp[TOTAL_WS:TOTAL_ROWS, l * WIDTH_A:(l + 1) * WIDTH_A] = dws_refs[l][...]
        sp[TOTAL_HEAD:TOTAL_WS, :] = head_ref[...]

        small_pair = pltpu.make_async_remote_copy(
            src_ref=sp, dst_ref=sr1, send_sem=ssend.at[0], recv_sem=srecv.at[0], device_id=sibling, device_id_type=MESH
        )

        def to_sibling(j):
            return pltpu.make_async_remote_copy(
                src_ref=g_ref.at[2 * j + 1 - c], dst_ref=r1.at[j], send_sem=send1.at[j], recv_sem=recv1.at[j],
                device_id=sibling, device_id_type=MESH,
            )

        first = [to_sibling(j) for j in range(4)]
        small_pair.start()
        for cp in first:
            cp.start()

        small_pair.wait_recv()
        sq[chip] = sp[...] + sr1[...]
        small_chips = [
            pltpu.make_async_remote_copy(
                src_ref=sq.at[chip], dst_ref=sq.at[chip], send_sem=ssend.at[1 + k], recv_sem=srecv.at[1 + k],
                device_id=(x ^ fx, y ^ fy, c), device_id_type=MESH,
            )
            for k, (fx, fy) in enumerate(flips)
        ]
        for cp in small_chips:
            cp.start()

        for j in range(4):
            first[j].wait_recv()

            @pl.when(chip != j)
            def _():
                a_s[j] = (g_ref[2 * j + c].astype(F32) + r1[j].astype(F32)).astype(BF16)

        half = rows // 2
        lo, hi = pl.ds(0, half), pl.ds(half, rows - half)
        x_nbr, y_nbr = (1 - x, y, c), (x, 1 - y, c)
        chip_x, chip_y, chip_d = 2 * (1 - x) + y, 2 * x + (1 - y), 2 * (1 - x) + (1 - y)

        def ici(k, src, dst, to):
            return pltpu.make_async_remote_copy(
                src_ref=src, dst_ref=dst, send_sem=send2.at[k], recv_sem=recv2.at[k], device_id=to, device_id_type=MESH)

        second = [
            ici(0, a_s.at[chip_d, lo, :], via.at[0], x_nbr),
            ici(1, a_s.at[chip_d, hi, :], via.at[1], y_nbr),
            ici(2, a_s.at[chip_x, lo, :], r2.at[0, lo, :], x_nbr),
            ici(3, a_s.at[chip_y, hi, :], r2.at[1, hi, :], y_nbr),
            ici(4, a_s.at[chip_x, hi, :], r2.at[0, hi, :], x_nbr),
            ici(5, a_s.at[chip_y, lo, :], r2.at[1, lo, :], y_nbr),
        ]
        for cp in second[:4]:
            cp.start()

        out_ref[...] = g_ref[2 * chip + c].astype(F32) + r1[chip].astype(F32)
        for cp in small_chips:
            cp.wait_recv()
        total_ref[...] = ((sq[0] + sq[1]) + sq[2]) + sq[3]

        second[0].wait_recv()
        a_s[chip_y, lo, :] = (a_s[chip_y, lo, :].astype(F32) + via[0].astype(F32)).astype(BF16)
        second[5].start()
        second[1].wait_recv()
        a_s[chip_x, hi, :] = (a_s[chip_x, hi, :].astype(F32) + via[1].astype(F32)).astype(BF16)
        second[4].start()

        second[2].wait_recv()
        second[4].wait_recv()
        out_ref[...] += r2[0].astype(F32)
        second[3].wait_recv()
        second[5].wait_recv()
        out_ref[...] += r2[1].astype(F32)
        small_pair.wait_send()
        for cp in first + small_chips + second:
            cp.wait_send()

    vmem = pl.BlockSpec(memory_space=pltpu.VMEM)
    return pl.pallas_call(
        body,
        name=f"layer{layer}_grad_reduce_scatter",
        out_shape=[jax.ShapeDtypeStruct((rows, n), F32), jax.ShapeDtypeStruct((TOTAL_ROWS, D_MODEL), F32)],
        in_specs=[vmem] * (2 + 2 * DEPTH),
        out_specs=[vmem, vmem],
        scratch_shapes=[
            pltpu.VMEM((4, rows, n), BF16),
            pltpu.VMEM((4, rows, n), BF16),
            pltpu.VMEM((2, rows, n), BF16),
            pltpu.VMEM((2, rows // 2, n), BF16),
            pltpu.VMEM((TOTAL_ROWS, D_MODEL), F32),
            pltpu.VMEM((TOTAL_ROWS, D_MODEL), F32),
            pltpu.VMEM((4, TOTAL_ROWS, D_MODEL), F32),
            pltpu.SemaphoreType.DMA((4,)),
            pltpu.SemaphoreType.DMA((4,)),
            pltpu.SemaphoreType.DMA((6,)),
            pltpu.SemaphoreType.DMA((6,)),
            pltpu.SemaphoreType.DMA((4,)),
            pltpu.SemaphoreType.DMA((4,)),
        ],
        compiler_params=pltpu.CompilerParams(vmem_limit_bytes=48 * MIB),
    )(pack, *smalls, head, *dws)


def _adam_step(w, g, m, v):
    m = ADAM_B1 * m + (1.0 - ADAM_B1) * g
    v = ADAM_B2 * v + (1.0 - ADAM_B2) * (g * g)
    m_hat = m / (1.0 - ADAM_B1 ** ADAM_STEP)
    v_hat = v / (1.0 - ADAM_B2 ** ADAM_STEP)
    return -ADAM_LR * (m_hat / (jnp.sqrt(v_hat) + ADAM_EPS) + ADAM_WD * w), m, v


def _adamw_rows(name, reduced, row_off, states):
    n = len(states)

    def body(*refs):
        red = refs[:DEPTH]
        ins = refs[DEPTH:DEPTH + 3 * n]
        outs = refs[DEPTH + 3 * n:]
        layer = pl.program_id(0)
        for l in range(DEPTH):
            @pl.when(layer == l)
            def _():
                for k in range(n):
                    w_ref, m_ref, v_ref = ins[3 * k:3 * k + 3]
                    g_ref, d_ref, nm_ref, nv_ref = outs[4 * k:4 * k + 4]
                    g = red[l][row_off[k]:row_off[k] + w_ref.shape[0], :]
                    d, m, v = _adam_step(w_ref[...], g, m_ref[...], v_ref[...])
                    g_ref[...] = g
                    d_ref[...] = d
                    nm_ref[...] = m
                    nv_ref[...] = v

    flat = [a for st in states for a in st]
    state_specs, out_specs, out_shape = [], [], []
    for w, _, _ in states:
        spec = pl.BlockSpec((None,) + w.shape[1:], lambda l: (l, 0, 0))
        state_specs += [spec] * 3
        out_specs += [spec] * 4
        out_shape += [jax.ShapeDtypeStruct(w.shape, F32)] * 4
    red_specs = [pl.BlockSpec(a.shape, lambda l: (0, 0)) for a in reduced]
    operands = [pltpu.with_memory_space_constraint(a, pltpu.HBM) for a in (*reduced, *flat)]
    outs = pl.pallas_call(
        body,
        name=name,
        grid=(DEPTH,),
        out_shape=[pltpu.HBM(a.shape, a.dtype) for a in out_shape],
        in_specs=red_specs + state_specs,
        out_specs=out_specs,
        compiler_params=pltpu.CompilerParams(dimension_semantics=("arbitrary",), vmem_limit_bytes=48 * MIB),
    )(*operands)
    return [tuple(outs[4 * k:4 * k + 4]) for k in range(n)]


def _adamw_small(total, g_conv, g_proj, st):
    names = ["norm_g", "ple_norm_g", "ln_v_g", "ln_v_b", "b_s", "w_s", "final_g", "conv_w", "w_ple_proj"]
    cut = names[:7]

    def body(total_ref, gconv_ref, gproj_ref, *refs):
        ins = {nm: refs[3 * k:3 * k + 3] for k, nm in enumerate(names)}
        outs, pos = {}, 3 * len(names)
        for nm in names:
            cnt = 4 if nm in cut else 3
            outs[nm] = refs[pos:pos + cnt]
            pos += cnt

        def update(nm, idx, g):
            w_ref, m_ref, v_ref = ins[nm]
            d, m, v = _adam_step(w_ref[idx], g, m_ref[idx], v_ref[idx])
            o = outs[nm]
            if nm in cut:
                o[0][idx] = g
                o = o[1:]
            o[0][idx] = d
            o[1][idx] = m
            o[2][idx] = v

        tril = (lax.broadcasted_iota(jnp.int32, (CHUNK, CHUNK), 0) >= lax.broadcasted_iota(jnp.int32, (CHUNK, CHUNK), 1))
        for l in range(DEPTH):
            base = l * SMALL_ROWS
            row = (slice(l, l + 1), slice(None))
            update("norm_g", row, total_ref[base + SMALL_NORM:base + SMALL_NORM + 1, :])
            update("ple_norm_g", row, total_ref[base + SMALL_PLE:base + SMALL_PLE + 1, :])
            update("ln_v_g", row, total_ref[base + SMALL_LN:base + SMALL_LN + 1, 0:WIDTH_A])
            update("ln_v_b", row, total_ref[base + SMALL_LN:base + SMALL_LN + 1, WIDTH_A:2 * WIDTH_A])
            for h in range(HEADS_A):
                update("b_s", (l, slice(h, h + 1), slice(None)),
                       total_ref[base + SMALL_BS:base + SMALL_BS + 1, h * HEAD_DIM:(h + 1) * HEAD_DIM])
                lanes = slice(l * WIDTH_A + h * CHUNK, l * WIDTH_A + (h + 1) * CHUNK)
                update("w_s", (l, h), jnp.where(tril, total_ref[TOTAL_WS:TOTAL_ROWS, lanes], 0.0))
        update("final_g", (slice(None), slice(None)), total_ref[TOTAL_HEAD + HEAD_FINAL:TOTAL_HEAD + HEAD_FINAL + 1, :])
        update("conv_w", (slice(None),) * 3, gconv_ref[...])
        update("w_ple_proj", (slice(None),) * 3, gproj_ref[...])

    flat = [a for nm in names for a in st[nm]]
    out_shape = []
    for nm in names:
        out_shape += [jax.ShapeDtypeStruct(st[nm][0].shape, F32)] * (4 if nm in cut else 3)
    def whole(a):
        return pl.BlockSpec(a.shape, lambda i: (0,) * len(a.shape))

    operands = [pltpu.with_memory_space_constraint(a, pltpu.HBM) for a in (total, g_conv, g_proj, *flat)]
    outs = pl.pallas_call(
        body,
        name="adamw_small",
        grid=(1,),
        out_shape=[pltpu.HBM(a.shape, a.dtype) for a in out_shape],
        in_specs=[whole(a) for a in operands],
        out_specs=[whole(a) for a in out_shape],
        compiler_params=pltpu.CompilerParams(dimension_semantics=("arbitrary",), vmem_limit_bytes=32 * MIB),
    )(*operands)
    res, pos = {}, 0
    for nm in names:
        cnt = 4 if nm in cut else 3
        got = tuple(outs[pos:pos + cnt])
        res[nm] = got if nm in cut else ((g_conv if nm == "conv_w" else g_proj),) + got
        pos += cnt
    return res


def _split3_bf16(a):
    b1 = a.astype(BF16)
    r1 = a - b1.astype(F32)
    b2 = r1.astype(BF16)
    b3 = (r1 - b2.astype(F32)).astype(BF16)
    return b1, b2, b3


def _pack_weight_shard(w_in_l, w_out_l, w_gate_l, w_proj_l, conv_w_l):
    w_in_t = jnp.transpose(w_in_l).astype(BF16)
    proj_t = jnp.transpose(w_proj_l).astype(BF16)
    proj_rows = proj_t.reshape(D_MODEL // PLE_DIM, ROWS_PROJ, PLE_DIM).transpose(1, 0, 2).reshape(ROWS_PROJ, D_MODEL)
    conv_parts = jnp.concatenate([b.reshape(-1) for b in _split3_bf16(conv_w_l)])
    conv_rows = jnp.concatenate([conv_parts, jnp.zeros((ROWS_CONV * D_MODEL - conv_parts.shape[0],), BF16)])
    return jnp.concatenate(
        [w_in_t, w_out_l.astype(BF16), w_gate_l.astype(BF16), proj_rows, conv_rows.reshape(ROWS_CONV, D_MODEL)], axis=0
    )


def _unpack_conv(wg):
    per_dev = wg.reshape(N_DEV, ROWS_LAYER, D_MODEL)
    n_conv = (WIDTH_B // N_DEV) * 3
    conv_parts = per_dev[:, OFF_CONV].astype(F32)[:, :3 * n_conv].reshape(N_DEV, 3, n_conv)
    conv = (conv_parts[:, 0] + conv_parts[:, 1]) + conv_parts[:, 2]
    conv_k = jnp.transpose(conv.reshape(WIDTH_B, 3))
    conv_k = jnp.concatenate([conv_k, jnp.zeros((5, WIDTH_B), F32)], axis=0)
    return conv_k


def _unpack_grad_proj(red):
    proj_rows = red[OFF_PROJ:OFF_PROJ + ROWS_PROJ]
    proj_t = proj_rows.reshape(ROWS_PROJ, D_MODEL // PLE_DIM, PLE_DIM).transpose(1, 0, 2).reshape(ROWS_OUT, PLE_DIM)
    return jnp.transpose(proj_t)


def kernel(x, p, norm_g, w_in, ln_v_g, ln_v_b, w_s, b_s, conv_w, w_out, ple_norm_g, w_ple_gate, w_ple_proj, final_g, loss_target, m_norm_g, m_w_in, m_ln_v_g, m_ln_v_b, m_w_s, m_b_s, m_conv_w, m_w_out, m_ple_norm_g, m_w_ple_gate, m_w_ple_proj, m_final_g, v_norm_g, v_w_in, v_ln_v_g, v_ln_v_b, v_w_s, v_b_s, v_conv_w, v_w_out, v_ple_norm_g, v_w_ple_gate, v_w_ple_proj, v_final_g):
    me = 4 * lax.axis_index("x") + 2 * lax.axis_index("y") + lax.axis_index("c")
    xs = x[0]
    target = loss_target[0]

    shards = [_pack_weight_shard(w_in[l], w_out[l], w_ple_gate[l], w_ple_proj[l], conv_w[l]) for l in range(DEPTH)]
    tril = jnp.tril(jnp.ones((CHUNK, CHUNK), F32))

    def consts(l, wg_l):
        conv_k = _unpack_conv(wg_l)
        w_mix = w_s[l] * tril[None]
        small = dict(
            conv_k=conv_k,
            norm_g=norm_g[l].reshape(1, D_MODEL), ln_g=ln_v_g[l].reshape(1, WIDTH_A), ln_b=ln_v_b[l].reshape(1, WIDTH_A),
            w_mix=w_mix.astype(BF16), w_mix_t=jnp.swapaxes(w_mix, 1, 2).astype(BF16),
            b_mix=jnp.broadcast_to(b_s[l][:, :, None], (HEADS_A, CHUNK, HEAD_DIM)),
            ple_g=ple_norm_g[l].reshape(1, D_MODEL),
        )
        return dict({k: pltpu.with_memory_space_constraint(a, pltpu.HBM) for k, a in small.items()}, wg=wg_l)

    layer_consts = [consts(0, _all_gather_rows(shards[0]))]
    saved = []
    h = xs
    for l in range(DEPTH):
        k = layer_consts[l]
        outs = _forward_layer(
            l, h, p, k["wg"], k["conv_k"], k["norm_g"], k["ln_g"], k["ln_b"], k["w_mix"], k["b_mix"],
            k["ple_g"], next_shard=shards[l + 1] if l + 1 < DEPTH else None)
        proj, hn, cat, r, gpre, x1, x2 = outs[:7]
        if l + 1 < DEPTH:
            layer_consts.append(consts(l + 1, outs[7]))
        saved.append(dict(x_in=h, proj=proj, hn=hn, cat=cat, r=r, gpre=gpre, x1=x1))
        h = x2

    smalls, dws = [None] * DEPTH, [None] * DEPTH
    reduced = [None] * DEPTH
    pending = None
    dx = h
    for l in reversed(range(DEPTH)):
        k, s = layer_consts[l], saved[l]
        outs = _backward_layer(
            l, dx, s["x_in"], s["x1"], s["proj"], s["gpre"], p, k["wg"], k["conv_k"],
            k["norm_g"], k["ln_g"], k["ln_b"], k["w_mix"], k["w_mix_t"], k["b_mix"], k["ple_g"],
            loss_head=(target, final_g.reshape(1, D_MODEL)) if l == DEPTH - 1 else None)
        dx, dproj, dx1, dgpre, dpp, smalls[l], dws[l] = outs[:7]
        if l == DEPTH - 1:
            head = outs[7]
        (pack,) = _weight_grads(l, dproj, s["hn"], s["cat"], dx1, s["r"], dgpre, dpp, p)
        if pending is not None:
            sems, pack_thru, land_thru = pending
            pieces = _scatter_wait(l + 1, sems, pack_thru, land_thru, pack)
            reduced[l + 1] = _sum_pieces(l + 1, pltpu.with_memory_space_constraint(pieces, pltpu.HBM))
        pending = _scatter_start(l, pack) if l > 0 else pack
    reduced[0], total = _reduce_scatter_all_reduce(0, pending, smalls, head, dws)
    grad_x = dx[None]
    loss = total[TOTAL_HEAD + HEAD_LOSS, 0]

    n_ch = WIDTH_B // N_DEV
    g_conv = jnp.stack([total[l * SMALL_ROWS + SMALL_CONV:l * SMALL_ROWS + SMALL_CONV + 3, 0:WIDTH_B] for l in range(DEPTH)], axis=1)
    g_conv = lax.dynamic_slice_in_dim(g_conv, me * n_ch, n_ch, axis=2)
    g_proj = jnp.stack([_unpack_grad_proj(reduced[l]) for l in range(DEPTH)])

    def t_in(a):
        return jnp.swapaxes(a, 1, 2)

    def t_conv(a):
        return jnp.transpose(a, (2, 0, 1))

    (r_in,) = _adamw_rows("adamw_w_in", reduced, [OFF_IN], [(t_in(w_in), t_in(m_w_in), t_in(v_w_in))])
    r_out, r_gate = _adamw_rows(
        "adamw_w_out_gate", reduced, [OFF_OUT, OFF_GATE],
        [(w_out, m_w_out, v_w_out), (w_ple_gate, m_w_ple_gate, v_w_ple_gate)])
    small = _adamw_small(total, g_conv, g_proj, dict(
        norm_g=(norm_g, m_norm_g, v_norm_g), ple_norm_g=(ple_norm_g, m_ple_norm_g, v_ple_norm_g),
        ln_v_g=(ln_v_g, m_ln_v_g, v_ln_v_g), ln_v_b=(ln_v_b, m_ln_v_b, v_ln_v_b),
        b_s=(b_s, m_b_s, v_b_s), w_s=(w_s, m_w_s, v_w_s),
        final_g=tuple(a.reshape(1, D_MODEL) for a in (final_g, m_final_g, v_final_g)),
        conv_w=(t_conv(conv_w), t_conv(m_conv_w), t_conv(v_conv_w)),
        w_ple_proj=(w_ple_proj, m_w_ple_proj, v_w_ple_proj),
    ))
    res = dict(small, w_in=tuple(t_in(a) for a in r_in), w_out=r_out, w_ple_gate=r_gate)
    res["final_g"] = tuple(a.reshape(D_MODEL) for a in res["final_g"])
    res["conv_w"] = tuple(jnp.transpose(a, (1, 2, 0)) for a in res["conv_w"])
    order = ["norm_g", "w_in", "ln_v_g", "ln_v_b", "w_s", "b_s", "conv_w", "w_out", "ple_norm_g", "w_ple_gate", "w_ple_proj", "final_g"]
    return (loss, grad_x, *[res[n][0] for n in order], *[res[n][1] for n in order],
            *[res[n][2] for n in order], *[res[n][3] for n in order])
```

```python
import jax
import jax.numpy as jnp
from jax import lax
from jax.experimental import pallas as pl
from jax.experimental.pallas import tpu as pltpu

F32 = jnp.float32
BF16 = jnp.bfloat16

D_MODEL = 1024
WIDTH_A = 512
WIDTH_B = 512
HEADS_A = 4
HEAD_DIM = 128
CHUNK = 128
PLE_DIM = 256
PROJ_WIDTH = 3584
DEPTH = 2
EPS = 1e-6
N_DEV = 8

ADAM_LR = 0.001
ADAM_B1 = 0.9
ADAM_B2 = 0.999
ADAM_EPS = 1e-08
ADAM_WD = 0.01
ADAM_STEP = 10

ROWS_IN = PROJ_WIDTH // N_DEV
ROWS_OUT = D_MODEL // N_DEV
ROWS_GATE = D_MODEL // N_DEV
ROWS_PROJ = (D_MODEL // N_DEV) * PLE_DIM // D_MODEL
ROWS_CONV = 16
OFF_IN = 0
OFF_OUT = OFF_IN + ROWS_IN
OFF_GATE = OFF_OUT + ROWS_OUT
OFF_PROJ = OFF_GATE + ROWS_GATE
OFF_CONV = OFF_PROJ + ROWS_PROJ
ROWS_GRAD = OFF_CONV
ROWS_LAYER = OFF_CONV + ROWS_CONV

SMALL_ROWS = 8
SMALL_NORM = 0
SMALL_PLE = 1
SMALL_LN = 2
SMALL_BS = 3
SMALL_CONV = 4
HEAD_FINAL = 0
HEAD_LOSS = 1
TOTAL_HEAD = DEPTH * SMALL_ROWS
TOTAL_WS = TOTAL_HEAD + SMALL_ROWS
TOTAL_ROWS = TOTAL_WS + CHUNK

MIB = 1024 * 1024
MESH = pl.DeviceIdType.MESH

NT_DIMS = (((1,), (1,)), ((), ()))
TN_DIMS = (((0,), (0,)), ((), ()))


def _dot(a, b):
    return jnp.dot(a, b, preferred_element_type=F32)


def _dot_nt(a, b):
    return lax.dot_general(a, b, NT_DIMS, preferred_element_type=F32)


def _dot_tn(a, b):
    return lax.dot_general(a, b, TN_DIMS, preferred_element_type=F32)


def _colsum8(a):
    rows, n = a.shape
    return jnp.sum(a.reshape(rows // 8, 8, n), axis=0)


def _sigmoid(z):
    return 1.0 / (1.0 + jnp.exp(-z))


def _tile(t, want):
    return want if t % want == 0 else t


class _TwoLevelGather:
    def __init__(self, x_ref, out_ref, m_per, send_sems, recv_sems, local_sem):
        x, y, c = lax.axis_index("x"), lax.axis_index("y"), lax.axis_index("c")
        self.me, self.sibling = (x, y, c), (x, y, 1 - c)
        self.xn, self.yn, self.diag = (1 - x, y, c), (x, 1 - y, c), (1 - x, 1 - y, c)
        self.x_ref, self.out_ref, self.m_per = x_ref, out_ref, m_per
        self.half = (m_per // 32) * 16
        self.send_sems, self.recv_sems = send_sems, recv_sems
        self.mine = pltpu.make_async_copy(x_ref, self.rows(self.me), local_sem)

    def rows(self, block, part=None):
        px, py, pc = block
        base = (4 * px + 2 * py + pc) * self.m_per
        if part is None:
            return self.out_ref.at[pl.ds(base, self.m_per), :]
        if part == 0:
            return self.out_ref.at[pl.ds(base, self.half), :]
        return self.out_ref.at[pl.ds(base + self.half, self.m_per - self.half), :]

    def copy(self, k, block, to, src=None, part=None):
        return pltpu.make_async_remote_copy(
            src_ref=self.rows(block, part) if src is None else src,
            dst_ref=self.rows(block, part),
            send_sem=self.send_sems.at[k],
            recv_sem=self.recv_sems.at[k],
            device_id=to,
            device_id_type=MESH,
        )

    def first(self):
        return [self.copy(0, self.me, self.sibling, src=self.x_ref),
                self.copy(1, self.me, self.xn, src=self.x_ref),
                self.copy(2, self.me, self.yn, src=self.x_ref)]

    def second(self):
        return [self.copy(3, self.xn, self.yn, part=0), self.copy(7, self.yn, self.xn, part=1),
                self.copy(4, self.xn, self.sibling), self.copy(5, self.yn, self.sibling)]

    def third(self):
        return [self.copy(6, self.diag, self.sibling)]

    def start(self):
        self.mine.start()
        for cp in self.first():
            cp.start()

    def pass_on(self):
        fwd_x, fwd_y, sib_x, sib_y = self.second()
        self.copy(1, self.xn, self.me).wait_recv()
        fwd_x.start()
        sib_x.start()
        self.copy(2, self.yn, self.me).wait_recv()
        fwd_y.start()
        sib_y.start()

    def pass_on_diagonal(self):
        self.copy(3, self.diag, self.me, part=0).wait_recv()
        self.copy(7, self.diag, self.me, part=1).wait_recv()
        self.third()[0].start()

    def finish(self):
        sib = (self.sibling[0], self.sibling[1], self.sibling[2])
        self.copy(0, sib, self.me).wait_recv()
        for k, chip in ((4, self.xn), (5, self.yn), (6, self.diag)):
            self.copy(k, (chip[0], chip[1], sib[2]), self.me).wait_recv()
        for cp in self.first() + self.second() + self.third():
            cp.wait_send()
        self.mine.wait()


GATHER_SEMS = [pltpu.SemaphoreType.DMA((8,)), pltpu.SemaphoreType.DMA((8,)), pltpu.SemaphoreType.DMA]


def _all_gather_rows(shard):
    m_per, n = shard.shape

    def body(x_ref, out_ref, send_sems, recv_sems, local_sem):
        ag = _TwoLevelGather(x_ref, out_ref, m_per, send_sems, recv_sems, local_sem)
        ag.start()
        ag.pass_on()
        ag.pass_on_diagonal()
        ag.finish()

    return pl.pallas_call(
        body,
        name="weights_all_gather",
        out_shape=pltpu.HBM((N_DEV * m_per, n), shard.dtype),
        in_specs=[pl.BlockSpec(memory_space=pltpu.HBM)],
        out_specs=pl.BlockSpec(memory_space=pltpu.HBM),
        scratch_shapes=list(GATHER_SEMS),
    )(pltpu.with_memory_space_constraint(shard, pltpu.HBM))


PROJ_PARTS = D_MODEL // PLE_DIM
N_WEIGHT_COPIES = N_DEV * (3 + PROJ_PARTS)


def _weight_copies(wg_ref, w_in_t, w_out, w_gate, w_proj_t, sems):
    copies = []
    for s in range(N_DEV):
        base = s * ROWS_LAYER
        for dst, off, rows in ((w_in_t, OFF_IN, ROWS_IN), (w_out, OFF_OUT, ROWS_OUT), (w_gate, OFF_GATE, ROWS_GATE)):
            copies.append((wg_ref.at[pl.ds(base + off, rows), :], dst.at[pl.ds(s * rows, rows), :]))
        for j in range(PROJ_PARTS):
            copies.append((
                wg_ref.at[pl.ds(base + OFF_PROJ, ROWS_PROJ), pl.ds(j * PLE_DIM, PLE_DIM)],
                w_proj_t.at[pl.ds(s * ROWS_OUT + j * ROWS_PROJ, ROWS_PROJ), :],
            ))
    return [pltpu.make_async_copy(src, dst, sems.at[k]) for k, (src, dst) in enumerate(copies)]


W_IN, W_OUT, W_GATE, W_PROJ = (0,), (1,), (2,), tuple(range(3, 3 + PROJ_PARTS))


def _wait_weights(first_step, copies, which):
    @pl.when(first_step)
    def _():
        for s in range(N_DEV):
            for j in which:
                copies[s * (3 + PROJ_PARTS) + j].wait()


def _forward_layer(layer, x, p_all, wg, conv_k, norm_g, ln_g, ln_b, w_mix, b_mix, ple_g, next_shard=None):
    t = x.shape[0]
    tm = _tile(t, 512)
    nt = t // tm
    gathers = next_shard is not None

    def body(*refs):
        (x_ref, p_ref, wg_ref, cw_ref, ng_ref, lng_ref, lnb_ref, wm_ref, bm_ref, pg_ref) = refs[:10]
        refs = refs[10:]
        if gathers:
            shard_ref, refs = refs[0], refs[1:]
        (proj_ref, hn_ref, cat_ref, r_ref, gpre_ref, x1_ref, x2_ref) = refs[:7]
        refs = refs[7:]
        if gathers:
            gathered_ref, refs = refs[0], refs[1:]
        (w_in_t, w_out, w_gate, wpt_ref, vln_s, mixed_s, halo_s, sems) = refs[:8]
        i = pl.program_id(0)
        if gathers:
            ag = _TwoLevelGather(shard_ref, gathered_ref, ROWS_LAYER, *refs[8:11])

            @pl.when(i == 0)
            def _():
                ag.start()

            @pl.when(i == (5 * nt) // 16)
            def _():
                ag.pass_on()

            @pl.when(i == nt // 2)
            def _():
                ag.pass_on_diagonal()

        copies = _weight_copies(wg_ref, w_in_t, w_out, w_gate, wpt_ref, sems)

        @pl.when(i == 0)
        def _():
            for cp in copies:
                cp.start()
            halo_s[...] = jnp.zeros_like(halo_s)

        xv = x_ref[...]
        rstd0 = lax.rsqrt(jnp.mean(xv * xv, axis=-1, keepdims=True) + EPS)
        hn_ref[...] = (xv * rstd0 * ng_ref[...]).astype(BF16)
        _wait_weights(i == 0, copies, W_IN)

        def proj_section(k):
            sec = _dot_nt(hn_ref[...], w_in_t[pl.ds(k * 512, 512), :])
            proj_ref[:, k * 512:(k + 1) * 512] = sec.astype(BF16)
            return sec

        v = proj_section(1)
        mu = jnp.mean(v, axis=-1, keepdims=True)
        vc = v - mu
        var = jnp.mean(vc * vc, axis=-1, keepdims=True)
        vln = vc * lax.rsqrt(var + EPS) * lng_ref[...] + lnb_ref[...]
        vln_s[...] = vln.astype(BF16)
        for ci in range(tm // CHUNK):
            rows = pl.ds(ci * CHUNK, CHUNK)
            for h in range(HEADS_A):
                cols = pl.ds(h * HEAD_DIM, HEAD_DIM)
                mixed_s[rows, cols] = _dot(wm_ref[h], vln_s[rows, cols]) + bm_ref[h]
        u = proj_section(0)
        za = proj_section(2)
        out_a = u * mixed_s[...] * (za * _sigmoid(za))
        cat_ref[:, 0:512] = out_a.astype(BF16)

        xc = proj_section(5) * proj_section(3)
        prev = halo_s[...]
        row = lax.broadcasted_iota(jnp.int32, (tm, WIDTH_B), 0)
        xc_m1 = jnp.where(row == 0, prev[7:8, :], pltpu.roll(xc, 1, 0))
        xc_m2 = jnp.where(row == 0, prev[6:7, :], jnp.where(row == 1, prev[7:8, :], pltpu.roll(xc, 2, 0)))
        halo_s[...] = xc[tm - 8:tm, :]
        cw = cw_ref[...]
        yc = cw[0:1, :] * xc_m2 + cw[1:2, :] * xc_m1 + cw[2:3, :] * xc
        zb = proj_section(6)
        out_b = proj_section(4) * yc * (zb * _sigmoid(zb))
        cat_ref[:, 512:1024] = out_b.astype(BF16)

        _wait_weights(i == 0, copies, W_OUT)
        x1 = xv + _dot(cat_ref[...], w_out[...])
        x1_ref[...] = x1
        rstd1 = lax.rsqrt(jnp.mean(x1 * x1, axis=-1, keepdims=True) + EPS)
        r_ref[...] = (x1 * rstd1 * pg_ref[...]).astype(BF16)
        _wait_weights(i == 0, copies, W_GATE + W_PROJ)
        gpre = _dot(r_ref[...], w_gate[...])
        gpre_ref[...] = gpre.astype(BF16)
        pp = _dot_nt(p_ref[...].astype(BF16), wpt_ref[...])
        x2_ref[...] = x1 + _sigmoid(gpre) * pp

        if gathers:
            @pl.when(i == nt - 1)
            def _():
                ag.finish()

    def tok(width):
        return pl.BlockSpec((tm, width), lambda i: (i, 0))

    def whole(shape):
        return pl.BlockSpec(shape, lambda i: (0,) * len(shape))

    hbm = pl.BlockSpec(memory_space=pl.ANY)
    operands = [x, p_all, wg, conv_k, norm_g, ln_g, ln_b, w_mix, b_mix, ple_g]
    in_specs = [
        tok(D_MODEL), pl.BlockSpec((None, None, tm, PLE_DIM), lambda i: (layer, 0, i, 0)), hbm,
        whole((8, WIDTH_B)), whole((1, D_MODEL)), whole((1, WIDTH_A)), whole((1, WIDTH_A)),
        whole((HEADS_A, CHUNK, CHUNK)), whole((HEADS_A, CHUNK, HEAD_DIM)), whole((1, D_MODEL)),
    ]
    out_specs = [tok(PROJ_WIDTH), tok(D_MODEL), tok(D_MODEL), tok(D_MODEL), tok(D_MODEL), tok(D_MODEL), tok(D_MODEL)]
    out_shape = [
        jax.ShapeDtypeStruct((t, PROJ_WIDTH), BF16),
        jax.ShapeDtypeStruct((t, D_MODEL), BF16),
        jax.ShapeDtypeStruct((t, D_MODEL), BF16),
        jax.ShapeDtypeStruct((t, D_MODEL), BF16),
        jax.ShapeDtypeStruct((t, D_MODEL), BF16),
        jax.ShapeDtypeStruct((t, D_MODEL), F32),
        jax.ShapeDtypeStruct((t, D_MODEL), F32),
    ]
    scratch_shapes = [
        pltpu.VMEM((PROJ_WIDTH, D_MODEL), BF16),
        pltpu.VMEM((D_MODEL, D_MODEL), BF16),
        pltpu.VMEM((D_MODEL, D_MODEL), BF16),
        pltpu.VMEM((D_MODEL, PLE_DIM), BF16),
        pltpu.VMEM((tm, WIDTH_A), BF16),
        pltpu.VMEM((tm, WIDTH_A), F32),
        pltpu.VMEM((8, WIDTH_B), F32),
        pltpu.SemaphoreType.DMA((N_WEIGHT_COPIES,)),
    ]
    if gathers:
        operands.append(pltpu.with_memory_space_constraint(next_shard, pltpu.HBM))
        in_specs.append(pl.BlockSpec(memory_space=pltpu.HBM))
        out_specs.append(pl.BlockSpec(memory_space=pltpu.HBM))
        out_shape.append(pltpu.HBM((N_DEV * ROWS_LAYER, D_MODEL), BF16))
        scratch_shapes += list(GATHER_SEMS)

    return pl.pallas_call(
        body,
        name=f"layer{layer}_forward",
        grid=(nt,),
        in_specs=in_specs,
        out_specs=out_specs,
        out_shape=out_shape,
        scratch_shapes=scratch_shapes,
        compiler_params=pltpu.CompilerParams(dimension_semantics=("arbitrary",), vmem_limit_bytes=56 * MIB),
    )(*operands)


class _DirectScatter:
    def __init__(self, pack_ref, pieces_ref, send_sems, recv_sems, local_sem):
        x, y, c = lax.axis_index("x"), lax.axis_index("y"), lax.axis_index("c")
        me = 4 * x + 2 * y + c
        self.copies = []
        for k in range(N_DEV - 1):
            fx, fy, fc = ((k + 1) >> 2) & 1, ((k + 1) >> 1) & 1, (k + 1) & 1
            tx, ty, tc = x ^ fx, y ^ fy, c ^ fc
            self.copies.append(
                pltpu.make_async_remote_copy(
                    src_ref=pack_ref.at[4 * tx + 2 * ty + tc], dst_ref=pieces_ref.at[me],
                    send_sem=send_sems.at[k], recv_sem=recv_sems.at[k],
                    device_id=(tx, ty, tc), device_id_type=MESH,
                )
            )
        self.mine = pltpu.make_async_copy(pack_ref.at[me], pieces_ref.at[me], local_sem)

    def start(self):
        self.mine.start()
        for cp in self.copies:
            cp.start()

    def finish(self):
        for cp in self.copies:
            cp.wait_recv()
        for cp in self.copies:
            cp.wait_send()
        self.mine.wait()


SCATTER_SEMS = [pltpu.SemaphoreType.DMA((N_DEV - 1,)), pltpu.SemaphoreType.DMA((N_DEV - 1,)), pltpu.SemaphoreType.DMA]


def _backward_layer(layer, dx2, x_in, x1, proj, gpre, p_all, wg, conv_k, norm_g, ln_g, ln_b,
                    w_mix, w_mix_t, b_mix, ple_g, loss_head=None):
    t = x_in.shape[0]
    tm = _tile(t, 256)
    nt = t // tm
    n_chunks = tm // CHUNK
    halo_rows = 16
    heads = loss_head is not None

    def body(*refs):
        (dx2_ref, xin_ref, x1_ref, proj_ref, halo_ref, gpre_ref, p_ref, wg_ref, cw_ref,
         ng_ref, lng_ref, lnb_ref, wm_ref, wmt_ref, bm_ref, pg_ref) = refs[:16]
        refs = refs[16:]
        if heads:
            tgt_ref, fg_ref = refs[:2]
            refs = refs[2:]
        (dxin_ref, dproj_ref, dx1_ref, dgpre_ref, dpp_ref, small_ref, dws_ref) = refs[:7]
        refs = refs[7:]
        if heads:
            head_ref, refs = refs[0], refs[1:]
        (w_in_t, w_out, w_gate, wpt_ref, vln_s, mixed_s, dmix_s, dvln_s, carry_s,
         ng_acc, pg_acc, lng_acc, lnb_acc, cw_acc, dbm_ref, sems) = refs[:16]
        if heads:
            loss_acc, fg_acc = refs[16:18]
        i = pl.program_id(0)
        tile = nt - 1 - i

        copies = _weight_copies(wg_ref, w_in_t, w_out, w_gate, wpt_ref, sems)

        @pl.when(i == 0)
        def _():
            for cp in copies:
                cp.start()
            if heads:
                loss_acc[...] = jnp.zeros_like(loss_acc)
                fg_acc[...] = jnp.zeros_like(fg_acc)
            carry_s[...] = jnp.zeros_like(carry_s)
            ng_acc[...] = jnp.zeros_like(ng_acc)
            pg_acc[...] = jnp.zeros_like(pg_acc)
            lng_acc[...] = jnp.zeros_like(lng_acc)
            lnb_acc[...] = jnp.zeros_like(lnb_acc)
            cw_acc[...] = jnp.zeros_like(cw_acc)
            dws_ref[...] = jnp.zeros_like(dws_ref)
            dbm_ref[...] = jnp.zeros_like(dbm_ref)

        if heads:
            x2v = dx2_ref[...]
            fg = fg_ref[...]
            rstdf = lax.rsqrt(jnp.mean(x2v * x2v, axis=-1, keepdims=True) + EPS)
            xhatf = x2v * rstdf
            err = xhatf * fg - tgt_ref[...]
            loss_acc[...] += _colsum8(err * err)
            dy = err * (1.0 / D_MODEL)
            fg_acc[...] += _colsum8(dy * xhatf)
            dxhf = dy * fg
            dx2v = rstdf * (dxhf - xhatf * jnp.mean(dxhf * xhatf, axis=-1, keepdims=True))
        else:
            dx2v = dx2_ref[...]

        gate = _sigmoid(gpre_ref[...].astype(F32))
        _wait_weights(i == 0, copies, W_PROJ)
        pp = _dot_nt(p_ref[...].astype(BF16), wpt_ref[...])
        dpp = dx2v * gate
        dpp_ref[...] = dpp.astype(BF16)
        dgpre = (dpp * pp * (1.0 - gate)).astype(BF16)
        dgpre_ref[...] = dgpre
        _wait_weights(i == 0, copies, W_GATE)
        dr = _dot_nt(dgpre, w_gate[...])
        x1v = x1_ref[...]
        rstd1 = lax.rsqrt(jnp.mean(x1v * x1v, axis=-1, keepdims=True) + EPS)
        xhat1 = x1v * rstd1
        pg_acc[...] += _colsum8(dr * xhat1)
        dxh = dr * pg_ref[...]
        dx1 = dx2v + rstd1 * (dxh - xhat1 * jnp.mean(dxh * xhat1, axis=-1, keepdims=True))
        dx1b = dx1.astype(BF16)
        dx1_ref[...] = dx1b

        _wait_weights(i == 0, copies, W_OUT)
        dcat = _dot_nt(dx1b, w_out[...])
        dca = dcat[:, 0:512]
        dcb = dcat[:, 512:1024]

        u = proj_ref[:, 0:512]
        v = proj_ref[:, 512:1024].astype(F32)
        za = proj_ref[:, 1024:1536]
        mu = jnp.mean(v, axis=-1, keepdims=True)
        vc = v - mu
        var = jnp.mean(vc * vc, axis=-1, keepdims=True)
        rs = lax.rsqrt(var + EPS)
        vhat = vc * rs
        lng = lng_ref[...]
        vln_s[...] = (vhat * lng + lnb_ref[...]).astype(BF16)
        for ci in range(n_chunks):
            rows = pl.ds(ci * CHUNK, CHUNK)
            for h in range(HEADS_A):
                cols = pl.ds(h * HEAD_DIM, HEAD_DIM)
                mixed_s[rows, cols] = (_dot(wm_ref[h], vln_s[rows, cols]) + bm_ref[h]).astype(BF16)
        mixed = mixed_s[...]
        sga = _sigmoid(za)
        sa = za * sga
        dsa = sga + sa * (1.0 - sga)

        def put_section(k, val):
            dproj_ref[:, k * 512:(k + 1) * 512] = val.astype(BF16)

        dcab = dca.astype(BF16)
        dca_sa = dcab * sa
        put_section(0, dca_sa * mixed)
        dmix_s[...] = dca_sa * u
        put_section(2, (dcab * dsa) * (u * mixed))
        dbm_acc = jnp.zeros((CHUNK, WIDTH_A), F32)
        for ci in range(n_chunks):
            rows = pl.ds(ci * CHUNK, CHUNK)
            dbm_acc = dbm_acc + dmix_s[rows, :].astype(F32)
            for h in range(HEADS_A):
                cols = pl.ds(h * HEAD_DIM, HEAD_DIM)
                dvln_s[rows, cols] = _dot(wmt_ref[h], dmix_s[rows, cols])
                dws_ref[:, cols] += _dot_nt(dmix_s[rows, cols], vln_s[rows, cols])
        dbm_ref[...] += dbm_acc
        dvln = dvln_s[...]
        lng_acc[...] += _colsum8(dvln * vhat)
        lnb_acc[...] += _colsum8(dvln)
        dvh = dvln * lng
        dv = rs * (dvh - jnp.mean(dvh, axis=-1, keepdims=True) - vhat * jnp.mean(dvh * vhat, axis=-1, keepdims=True))
        put_section(1, dv)

        hb = proj_ref[:, 1536:2048].astype(F32)
        gb = proj_ref[:, 2048:2560]
        gc = proj_ref[:, 2560:3072].astype(F32)
        zb = proj_ref[:, 3072:3584]
        xc = gc * hb
        prev = halo_ref[:, 2560:3072].astype(F32) * halo_ref[:, 1536:2048].astype(F32)
        prev = jnp.where(tile > 0, prev, 0.0)
        row = lax.broadcasted_iota(jnp.int32, (tm, WIDTH_B), 0)
        p1 = prev[halo_rows - 1:halo_rows, :]
        p2 = prev[halo_rows - 2:halo_rows - 1, :]
        xc_m1 = jnp.where(row == 0, p1, pltpu.roll(xc, 1, 0))
        xc_m2 = jnp.where(row == 0, p2, jnp.where(row == 1, p1, pltpu.roll(xc, 2, 0)))
        cw = cw_ref[...]
        yc = cw[0:1, :] * xc_m2 + cw[1:2, :] * xc_m1 + cw[2:3, :] * xc
        sgb = _sigmoid(zb)
        sb = zb * sgb
        dsb = sgb + sb * (1.0 - sgb)
        dcbb = dcb.astype(BF16)
        ycb = yc.astype(BF16)
        dcb_sb = dcbb * sb
        put_section(4, dcb_sb * ycb)
        dyc = (dcb_sb * gb).astype(F32)
        put_section(6, (dcbb * dsb) * (gb * ycb))
        nxt = carry_s[...]
        dyc_p1 = jnp.where(row == tm - 1, nxt[0:1, :], pltpu.roll(dyc, tm - 1, 0))
        dyc_p2 = jnp.where(row == tm - 1, nxt[1:2, :], jnp.where(row == tm - 2, nxt[0:1, :], pltpu.roll(dyc, tm - 2, 0)))
        carry_s[...] = dyc[0:8, :]
        dxc = cw[2:3, :] * dyc + cw[1:2, :] * dyc_p1 + cw[0:1, :] * dyc_p2
        cw_acc[0] += _colsum8(dyc * xc_m2)
        cw_acc[1] += _colsum8(dyc * xc_m1)
        cw_acc[2] += _colsum8(dyc * xc)
        put_section(3, dxc * gc)
        put_section(5, dxc * hb)

        _wait_weights(i == 0, copies, W_IN)
        dhn = _dot(dproj_ref[...], w_in_t[...])
        xv = xin_ref[...]
        rstd0 = lax.rsqrt(jnp.mean(xv * xv, axis=-1, keepdims=True) + EPS)
        xhat0 = xv * rstd0
        ng_acc[...] += _colsum8(dhn * xhat0)
        dxh0 = dhn * ng_ref[...]
        dxin_ref[...] = dx1 + rstd0 * (dxh0 - xhat0 * jnp.mean(dxh0 * xhat0, axis=-1, keepdims=True))

        @pl.when(i == nt - 1)
        def _():
            small_ref[...] = jnp.zeros_like(small_ref)
            small_ref[SMALL_NORM:SMALL_NORM + 1, :] = jnp.sum(ng_acc[...], axis=0, keepdims=True)
            small_ref[SMALL_PLE:SMALL_PLE + 1, :] = jnp.sum(pg_acc[...], axis=0, keepdims=True)
            small_ref[SMALL_LN:SMALL_LN + 1, 0:WIDTH_A] = jnp.sum(lng_acc[...], axis=0, keepdims=True)
            small_ref[SMALL_LN:SMALL_LN + 1, WIDTH_A:2 * WIDTH_A] = jnp.sum(lnb_acc[...], axis=0, keepdims=True)
            for h in range(HEADS_A):
                cols = pl.ds(h * HEAD_DIM, HEAD_DIM)
                small_ref[SMALL_BS:SMALL_BS + 1, cols] = jnp.sum(jnp.transpose(dbm_ref[:, cols]), axis=0, keepdims=True)
            for k in range(3):
                small_ref[SMALL_CONV + k:SMALL_CONV + k + 1, 0:WIDTH_B] = jnp.sum(cw_acc[k], axis=0, keepdims=True)
            if heads:
                total = jnp.sum(loss_acc[...]) * (0.5 / D_MODEL)
                rows8 = lax.broadcasted_iota(jnp.int32, (SMALL_ROWS, D_MODEL), 0)
                lanes8 = lax.broadcasted_iota(jnp.int32, (SMALL_ROWS, D_MODEL), 1)
                head_ref[...] = jnp.where((rows8 == HEAD_LOSS) & (lanes8 == 0), total, 0.0)
                head_ref[HEAD_FINAL:HEAD_FINAL + 1, :] = jnp.sum(fg_acc[...], axis=0, keepdims=True)

    def tok(width):
        return pl.BlockSpec((tm, width), lambda i: (nt - 1 - i, 0))

    def whole(shape):
        return pl.BlockSpec(shape, lambda i: (0,) * len(shape))

    halo_spec = pl.BlockSpec(
        (halo_rows, PROJ_WIDTH), lambda i: (jnp.maximum((nt - 1 - i) * (tm // halo_rows) - 1, 0), 0)
    )
    hbm = pl.BlockSpec(memory_space=pl.ANY)
    operands = [dx2, x_in, x1, proj, proj, gpre, p_all, wg, conv_k, norm_g, ln_g, ln_b, w_mix, w_mix_t, b_mix, ple_g]
    in_specs = [
        tok(D_MODEL), tok(D_MODEL), tok(D_MODEL), tok(PROJ_WIDTH), halo_spec, tok(D_MODEL),
        pl.BlockSpec((None, None, tm, PLE_DIM), lambda i: (layer, 0, nt - 1 - i, 0)), hbm,
        whole((8, WIDTH_B)), whole((1, D_MODEL)), whole((1, WIDTH_A)), whole((1, WIDTH_A)),
        whole((HEADS_A, CHUNK, CHUNK)), whole((HEADS_A, CHUNK, CHUNK)), whole((HEADS_A, CHUNK, HEAD_DIM)),
        whole((1, D_MODEL)),
    ]
    out_specs = [
        tok(D_MODEL), tok(PROJ_WIDTH), tok(D_MODEL), tok(D_MODEL), tok(D_MODEL),
        whole((SMALL_ROWS, D_MODEL)), whole((CHUNK, WIDTH_A)),
    ]
    out_shape = [
        jax.ShapeDtypeStruct((t, D_MODEL), F32),
        jax.ShapeDtypeStruct((t, PROJ_WIDTH), BF16),
        jax.ShapeDtypeStruct((t, D_MODEL), BF16),
        jax.ShapeDtypeStruct((t, D_MODEL), BF16),
        jax.ShapeDtypeStruct((t, D_MODEL), BF16),
        jax.ShapeDtypeStruct((SMALL_ROWS, D_MODEL), F32),
        jax.ShapeDtypeStruct((CHUNK, WIDTH_A), F32),
    ]
    scratch_shapes = [
        pltpu.VMEM((PROJ_WIDTH, D_MODEL), BF16),
        pltpu.VMEM((D_MODEL, D_MODEL), BF16),
        pltpu.VMEM((D_MODEL, D_MODEL), BF16),
        pltpu.VMEM((D_MODEL, PLE_DIM), BF16),
        pltpu.VMEM((tm, WIDTH_A), BF16),
        pltpu.VMEM((tm, WIDTH_A), BF16),
        pltpu.VMEM((tm, WIDTH_A), BF16),
        pltpu.VMEM((tm, WIDTH_A), F32),
        pltpu.VMEM((8, WIDTH_B), F32),
        pltpu.VMEM((8, D_MODEL), F32),
        pltpu.VMEM((8, D_MODEL), F32),
        pltpu.VMEM((8, WIDTH_A), F32),
        pltpu.VMEM((8, WIDTH_A), F32),
        pltpu.VMEM((3, 8, WIDTH_B), F32),
        pltpu.VMEM((CHUNK, WIDTH_A), F32),
        pltpu.SemaphoreType.DMA((N_WEIGHT_COPIES,)),
    ]
    if heads:
        operands += list(loss_head)
        in_specs += [tok(D_MODEL), whole((1, D_MODEL))]
        out_specs.append(whole((SMALL_ROWS, D_MODEL)))
        out_shape.append(jax.ShapeDtypeStruct((SMALL_ROWS, D_MODEL), F32))
        scratch_shapes += [pltpu.VMEM((8, D_MODEL), F32), pltpu.VMEM((8, D_MODEL), F32)]

    return pl.pallas_call(
        body,
        name=f"layer{layer}_backward",
        grid=(nt,),
        in_specs=in_specs,
        out_specs=out_specs,
        out_shape=out_shape,
        scratch_shapes=scratch_shapes,
        compiler_params=pltpu.CompilerParams(dimension_semantics=("arbitrary",), vmem_limit_bytes=56 * MIB),
    )(*operands)


def _scatter_targets():
    x, y, c = lax.axis_index("x"), lax.axis_index("y"), lax.axis_index("c")
    out = []
    for k in range(N_DEV - 1):
        fx, fy, fc = ((k + 1) >> 2) & 1, ((k + 1) >> 1) & 1, (k + 1) & 1
        tx, ty, tc = x ^ fx, y ^ fy, c ^ fc
        out.append((4 * tx + 2 * ty + tc, (tx, ty, tc)))
    return 4 * x + 2 * y + c, out


def _scatter_start(layer, pack):
    n = N_DEV - 1

    def body(pack_ref, land_ref, *rest):
        sems = rest[:2 * n + 1]
        me, targets = _scatter_targets()
        pltpu.make_async_copy(pack_ref.at[me], land_ref.at[me], sems[2 * n]).start()
        for k, (block, device) in enumerate(targets):
            pltpu.make_async_remote_copy(
                src_ref=pack_ref.at[block], dst_ref=land_ref.at[me], send_sem=sems[k], recv_sem=sems[n + k],
                device_id=device, device_id_type=MESH,
            ).start()

    hbm = pl.BlockSpec(memory_space=pltpu.HBM)
    sem = pl.BlockSpec(memory_space=pltpu.SEMAPHORE)
    outs = pl.pallas_call(
        body,
        name=f"layer{layer}_scatter_start",
        out_shape=(*[pltpu.SemaphoreType.DMA(())] * (2 * n + 1), pltpu.HBM(pack.shape, pack.dtype)),
        in_specs=(hbm, hbm),
        out_specs=(*[sem] * (2 * n + 1), hbm),
        input_output_aliases={1: 2 * n + 1},
        compiler_params=pltpu.CompilerParams(has_side_effects=pltpu.SideEffectType.DATAFLOW_SIDE_EFFECTING),
    )(pack, pltpu.with_memory_space_constraint(lax.empty(pack.shape, pack.dtype), pltpu.HBM))
    return outs[:2 * n + 1], pack, outs[2 * n + 1]


def _scatter_wait(layer, sems, pack_thru, land_thru, after):
    n = N_DEV - 1

    def body(pack_ref, land_ref, *rest):
        sem_refs = rest[:2 * n + 1]
        me, targets = _scatter_targets()
        pltpu.make_async_copy(pack_ref.at[me], land_ref.at[me], sem_refs[2 * n]).wait()
        for k, (block, device) in enumerate(targets):
            copy = pltpu.make_async_remote_copy(
                src_ref=pack_ref.at[block], dst_ref=land_ref.at[me], send_sem=sem_refs[k], recv_sem=sem_refs[n + k],
                device_id=device, device_id_type=MESH,
            )
            copy.wait_send()
            copy.wait_recv()

    hbm = pl.BlockSpec(memory_space=pltpu.HBM)
    sem = pl.BlockSpec(memory_space=pltpu.SEMAPHORE)
    return pl.pallas_call(
        body,
        name=f"layer{layer}_scatter_wait",
        out_shape=pltpu.HBM(pack_thru.shape, pack_thru.dtype),
        in_specs=(hbm, hbm, *[sem] * (2 * n + 1), pl.BlockSpec(memory_space=pl.ANY)),
        out_specs=hbm,
        input_output_aliases={1: 0},
        compiler_params=pltpu.CompilerParams(has_side_effects=pltpu.SideEffectType.DATAFLOW_SIDE_EFFECTING),
    )(pack_thru, land_thru, *sems, after)


def _sum_pieces(layer, pieces):
    rows, n = pieces.shape[1], pieces.shape[2]
    blocks = 2
    rb = rows // blocks

    def body(p_ref, out_ref):
        total = p_ref[0].astype(F32)
        for j in range(1, N_DEV):
            total = total + p_ref[j].astype(F32)
        out_ref[...] = total

    return pl.pallas_call(
        body,
        name=f"layer{layer}_grad_sum",
        grid=(blocks,),
        out_shape=pltpu.HBM((rows, n), F32),
        in_specs=[pl.BlockSpec((N_DEV, rb, n), lambda i: (0, i, 0))],
        out_specs=pl.BlockSpec((rb, n), lambda i: (i, 0)),
        compiler_params=pltpu.CompilerParams(dimension_semantics=("arbitrary",), vmem_limit_bytes=32 * MIB),
    )(pieces)


def _weight_grads(layer, dproj, hn, cat, dx1, r, dgpre, dpp, p_all, scatter_pack=None):
    t = hn.shape[0]
    tk = _tile(t, 512)
    nt = t // tk
    in_blocks = PROJ_WIDTH // 512
    scatters = scatter_pack is not None

    def body(*refs):
        (dproj_ref, hn_ref, cat_ref, dx1_ref, r_ref, dgpre_ref, dpp_ref, p_ref) = refs[:8]
        refs = refs[8:]
        if scatters:
            prior_ref, refs = refs[0], refs[1:]
        pack_ref, refs = refs[0], refs[1:]
        if scatters:
            pieces_ref, refs = refs[0], refs[1:]
        (acc_in, acc_out, acc_gate, acc_proj, stage, sems) = refs[:6]
        i = pl.program_id(0)
        if scatters:
            scatter = _DirectScatter(prior_ref, pieces_ref, *refs[6:9])

            @pl.when(i == 0)
            def _():
                scatter.start()

        @pl.when(i == 0)
        def _():
            acc_in[...] = jnp.zeros_like(acc_in)
            acc_out[...] = jnp.zeros_like(acc_out)
            acc_gate[...] = jnp.zeros_like(acc_gate)
            acc_proj[...] = jnp.zeros_like(acc_proj)

        hnv = hn_ref[...]
        for b in range(in_blocks):
            acc_in[pl.ds(b * 512, 512), :] += _dot_tn(dproj_ref[:, b * 512:(b + 1) * 512], hnv)
        dx1v = dx1_ref[...]
        dgv = dgpre_ref[...]
        for b in range(D_MODEL // 512):
            acc_out[pl.ds(b * 512, 512), :] += _dot_tn(cat_ref[:, b * 512:(b + 1) * 512], dx1v)
            acc_gate[pl.ds(b * 512, 512), :] += _dot_tn(r_ref[:, b * 512:(b + 1) * 512], dgv)
        pv = p_ref[...].astype(BF16)
        for b in range(D_MODEL // 512):
            acc_proj[pl.ds(b * 512, 512), :] += _dot_tn(dpp_ref[:, b * 512:(b + 1) * 512], pv)

        @pl.when(i == nt - 1)
        def _():
            def out_copy(s):
                return pltpu.make_async_copy(stage.at[s % 2], pack_ref.at[s], sems.at[s % 2])

            for s in range(N_DEV):
                if s >= 2:
                    out_copy(s - 2).wait()
                buf = stage.at[s % 2]
                buf[pl.ds(OFF_IN, ROWS_IN), :] = acc_in[pl.ds(s * ROWS_IN, ROWS_IN), :].astype(BF16)
                buf[pl.ds(OFF_OUT, ROWS_OUT), :] = acc_out[pl.ds(s * ROWS_OUT, ROWS_OUT), :].astype(BF16)
                buf[pl.ds(OFF_GATE, ROWS_GATE), :] = acc_gate[pl.ds(s * ROWS_GATE, ROWS_GATE), :].astype(BF16)
                for j in range(D_MODEL // PLE_DIM):
                    buf[pl.ds(OFF_PROJ, ROWS_PROJ), pl.ds(j * PLE_DIM, PLE_DIM)] = acc_proj[
                        pl.ds(s * ROWS_OUT + j * ROWS_PROJ, ROWS_PROJ), :
                    ].astype(BF16)
                out_copy(s).start()
            out_copy(N_DEV - 2).wait()
            out_copy(N_DEV - 1).wait()
            if scatters:
                scatter.finish()

    def tok(width):
        return pl.BlockSpec((tk, width), lambda i: (i, 0))

    hbm = pl.BlockSpec(memory_space=pl.ANY)
    pack_shape = jax.ShapeDtypeStruct((N_DEV, ROWS_GRAD, D_MODEL), BF16)
    operands = [dproj, hn, cat, dx1, r, dgpre, dpp, p_all]
    in_specs = [tok(PROJ_WIDTH), tok(D_MODEL), tok(D_MODEL), tok(D_MODEL), tok(D_MODEL), tok(D_MODEL), tok(D_MODEL),
                pl.BlockSpec((None, None, tk, PLE_DIM), lambda i: (layer, 0, i, 0))]
    out_specs, out_shape = [hbm], [pack_shape]
    scratch_shapes = [
        pltpu.VMEM((PROJ_WIDTH, D_MODEL), F32),
        pltpu.VMEM((D_MODEL, D_MODEL), F32),
        pltpu.VMEM((D_MODEL, D_MODEL), F32),
        pltpu.VMEM((D_MODEL, PLE_DIM), F32),
        pltpu.VMEM((2, ROWS_GRAD, D_MODEL), BF16),
        pltpu.SemaphoreType.DMA((2,)),
    ]
    if scatters:
        operands.append(scatter_pack)
        in_specs.append(hbm)
        out_specs.append(hbm)
        out_shape.append(pack_shape)
        scratch_shapes += list(SCATTER_SEMS)

    return pl.pallas_call(
        body,
        name=f"layer{layer}_weight_grads",
        grid=(nt,),
        in_specs=in_specs,
        out_specs=out_specs,
        out_shape=out_shape,
        scratch_shapes=scratch_shapes,
        compiler_params=pltpu.CompilerParams(dimension_semantics=("arbitrary",), vmem_limit_bytes=58 * MIB),
    )(*operands)


def _reduce_scatter_all_reduce(layer, pack, smalls, head, dws):
    rows, n = pack.shape[1], pack.shape[2]
    assert DEPTH * WIDTH_A == D_MODEL and n == D_MODEL

    def body(g_ref, *refs):
        small_refs, refs = refs[:DEPTH], refs[DEPTH:]
        head_ref, refs = refs[0], refs[1:]
        dws_refs, refs = refs[:DEPTH], refs[DEPTH:]
        (out_ref, total_ref, r1, a_s, r2, via, sp, sr1, sq, send1, recv1, send2, recv2, ssend, srecv) = refs
        x, y, c = lax.axis_index("x"), lax.axis_index("y"), lax.axis_index("c")
        sibling = (x, y, 1 - c)
        chip = 2 * x + y
        flips = [(1, 0), (0, 1), (1, 1)]

        for l in range(DEPTH):
            sp[l * SMALL_ROWS:(l + 1) * SMALL_ROWS, :] = small_refs[l][...]
            sp[TOTAL_WS:TOTAL_ROWS, l * WIDTH_A:(l + 1) * WIDTH_A] = dws_refs[l][...]
        sp[TOTAL_HEAD:TOTAL_WS, :] = head_ref[...]

        small_pair = pltpu.make_async_remote_copy(
            src_ref=sp, dst_ref=sr1, send_sem=ssend.at[0], recv_sem=srecv.at[0], device_id=sibling, device_id_type=MESH
        )

        def to_sibling(j):
            return pltpu.make_async_remote_copy(
                src_ref=g_ref.at[2 * j + 1 - c], dst_ref=r1.at[j], send_sem=send1.at[j], recv_sem=recv1.at[j],
                device_id=sibling, device_id_type=MESH,
            )

        first = [to_sibling(j) for j in range(4)]
        small_pair.start()
        for cp in first:
            cp.start()

        small_pair.wait_recv()
        sq[chip] = sp[...] + sr1[...]
        small_chips = [
            pltpu.make_async_remote_copy(
                src_ref=sq.at[chip], dst_ref=sq.at[chip], send_sem=ssend.at[1 + k], recv_sem=srecv.at[1 + k],
                device_id=(x ^ fx, y ^ fy, c), device_id_type=MESH,
            )
            for k, (fx, fy) in enumerate(flips)
        ]
        for cp in small_chips:
            cp.start()

        for j in range(4):
            first[j].wait_recv()

            @pl.when(chip != j)
            def _():
                a_s[j] = (g_ref[2 * j + c].astype(F32) + r1[j].astype(F32)).astype(BF16)

        half = rows // 2
        lo, hi = pl.ds(0, half), pl.ds(half, rows - half)
        x_nbr, y_nbr = (1 - x, y, c), (x, 1 - y, c)
        chip_x, chip_y, chip_d = 2 * (1 - x) + y, 2 * x + (1 - y), 2 * (1 - x) + (1 - y)

        def ici(k, src, dst, to):
            return pltpu.make_async_remote_copy(
                src_ref=src, dst_ref=dst, send_sem=send2.at[k], recv_sem=recv2.at[k], device_id=to, device_id_type=MESH)

        second = [
            ici(0, a_s.at[chip_d, lo, :], via.at[0], x_nbr),
            ici(1, a_s.at[chip_d, hi, :], via.at[1], y_nbr),
            ici(2, a_s.at[chip_x, lo, :], r2.at[0, lo, :], x_nbr),
            ici(3, a_s.at[chip_y, hi, :], r2.at[1, hi, :], y_nbr),
            ici(4, a_s.at[chip_x, hi, :], r2.at[0, hi, :], x_nbr),
            ici(5, a_s.at[chip_y, lo, :], r2.at[1, lo, :], y_nbr),
        ]
        for cp in second[:4]:
            cp.start()

        out_ref[...] = g_ref[2 * chip + c].astype(F32) + r1[chip].astype(F32)
        for cp in small_chips:
            cp.wait_recv()
        total_ref[...] = ((sq[0] + sq[1]) + sq[2]) + sq[3]

        second[0].wait_recv()
        a_s[chip_y, lo, :] = (a_s[chip_y, lo, :].astype(F32) + via[0].astype(F32)).astype(BF16)
        second[5].start()
        second[1].wait_recv()
        a_s[chip_x, hi, :] = (a_s[chip_x, hi, :].astype(F32) + via[1].astype(F32)).astype(BF16)
        second[4].start()

        second[2].wait_recv()
        second[4].wait_recv()
        out_ref[...] += r2[0].astype(F32)
        second[3].wait_recv()
        second[5].wait_recv()
        out_ref[...] += r2[1].astype(F32)
        small_pair.wait_send()
        for cp in first + small_chips + second:
            cp.wait_send()

    vmem = pl.BlockSpec(memory_space=pltpu.VMEM)
    return pl.pallas_call(
        body,
        name=f"layer{layer}_grad_reduce_scatter",
        out_shape=[jax.ShapeDtypeStruct((rows, n), F32), jax.ShapeDtypeStruct((TOTAL_ROWS, D_MODEL), F32)],
        in_specs=[vmem] * (2 + 2 * DEPTH),
        out_specs=[vmem, vmem],
        scratch_shapes=[
            pltpu.VMEM((4, rows, n), BF16),
            pltpu.VMEM((4, rows, n), BF16),
            pltpu.VMEM((2, rows, n), BF16),
            pltpu.VMEM((2, rows // 2, n), BF16),
            pltpu.VMEM((TOTAL_ROWS, D_MODEL), F32),
            pltpu.VMEM((TOTAL_ROWS, D_MODEL), F32),
            pltpu.VMEM((4, TOTAL_ROWS, D_MODEL), F32),
            pltpu.SemaphoreType.DMA((4,)),
            pltpu.SemaphoreType.DMA((4,)),
            pltpu.SemaphoreType.DMA((6,)),
            pltpu.SemaphoreType.DMA((6,)),
            pltpu.SemaphoreType.DMA((4,)),
            pltpu.SemaphoreType.DMA((4,)),
        ],
        compiler_params=pltpu.CompilerParams(vmem_limit_bytes=48 * MIB),
    )(pack, *smalls, head, *dws)


def _adam_step(w, g, m, v):
    m = ADAM_B1 * m + (1.0 - ADAM_B1) * g
    v = ADAM_B2 * v + (1.0 - ADAM_B2) * (g * g)
    m_hat = m / (1.0 - ADAM_B1 ** ADAM_STEP)
    v_hat = v / (1.0 - ADAM_B2 ** ADAM_STEP)
    return -ADAM_LR * (m_hat / (jnp.sqrt(v_hat) + ADAM_EPS) + ADAM_WD * w), m, v


def _adamw_rows(name, reduced, row_off, states):
    n = len(states)

    def body(*refs):
        red = refs[:DEPTH]
        ins = refs[DEPTH:DEPTH + 3 * n]
        outs = refs[DEPTH + 3 * n:]
        layer = pl.program_id(0)
        for l in range(DEPTH):
            @pl.when(layer == l)
            def _():
                for k in range(n):
                    w_ref, m_ref, v_ref = ins[3 * k:3 * k + 3]
                    g_ref, d_ref, nm_ref, nv_ref = outs[4 * k:4 * k + 4]
                    g = red[l][row_off[k]:row_off[k] + w_ref.shape[0], :]
                    d, m, v = _adam_step(w_ref[...], g, m_ref[...], v_ref[...])
                    g_ref[...] = g
                    d_ref[...] = d
                    nm_ref[...] = m
                    nv_ref[...] = v

    flat = [a for st in states for a in st]
    state_specs, out_specs, out_shape = [], [], []
    for w, _, _ in states:
        spec = pl.BlockSpec((None,) + w.shape[1:], lambda l: (l, 0, 0))
        state_specs += [spec] * 3
        out_specs += [spec] * 4
        out_shape += [jax.ShapeDtypeStruct(w.shape, F32)] * 4
    red_specs = [pl.BlockSpec(a.shape, lambda l: (0, 0)) for a in reduced]
    operands = [pltpu.with_memory_space_constraint(a, pltpu.HBM) for a in (*reduced, *flat)]
    outs = pl.pallas_call(
        body,
        name=name,
        grid=(DEPTH,),
        out_shape=[pltpu.HBM(a.shape, a.dtype) for a in out_shape],
        in_specs=red_specs + state_specs,
        out_specs=out_specs,
        compiler_params=pltpu.CompilerParams(dimension_semantics=("arbitrary",), vmem_limit_bytes=48 * MIB),
    )(*operands)
    return [tuple(outs[4 * k:4 * k + 4]) for k in range(n)]


def _adamw_small(total, g_conv, g_proj, st):
    names = ["norm_g", "ple_norm_g", "ln_v_g", "ln_v_b", "b_s", "w_s", "final_g", "conv_w", "w_ple_proj"]
    cut = names[:7]

    def body(total_ref, gconv_ref, gproj_ref, *refs):
        ins = {nm: refs[3 * k:3 * k + 3] for k, nm in enumerate(names)}
        outs, pos = {}, 3 * len(names)
        for nm in names:
            cnt = 4 if nm in cut else 3
            outs[nm] = refs[pos:pos + cnt]
            pos += cnt

        def update(nm, idx, g):
            w_ref, m_ref, v_ref = ins[nm]
            d, m, v = _adam_step(w_ref[idx], g, m_ref[idx], v_ref[idx])
            o = outs[nm]
            if nm in cut:
                o[0][idx] = g
                o = o[1:]
            o[0][idx] = d
            o[1][idx] = m
            o[2][idx] = v

        tril = (lax.broadcasted_iota(jnp.int32, (CHUNK, CHUNK), 0) >= lax.broadcasted_iota(jnp.int32, (CHUNK, CHUNK), 1))
        for l in range(DEPTH):
            base = l * SMALL_ROWS
            row = (slice(l, l + 1), slice(None))
            update("norm_g", row, total_ref[base + SMALL_NORM:base + SMALL_NORM + 1, :])
            update("ple_norm_g", row, total_ref[base + SMALL_PLE:base + SMALL_PLE + 1, :])
            update("ln_v_g", row, total_ref[base + SMALL_LN:base + SMALL_LN + 1, 0:WIDTH_A])
            update("ln_v_b", row, total_ref[base + SMALL_LN:base + SMALL_LN + 1, WIDTH_A:2 * WIDTH_A])
            for h in range(HEADS_A):
                update("b_s", (l, slice(h, h + 1), slice(None)),
                       total_ref[base + SMALL_BS:base + SMALL_BS + 1, h * HEAD_DIM:(h + 1) * HEAD_DIM])
                lanes = slice(l * WIDTH_A + h * CHUNK, l * WIDTH_A + (h + 1) * CHUNK)
                update("w_s", (l, h), jnp.where(tril, total_ref[TOTAL_WS:TOTAL_ROWS, lanes], 0.0))
        update("final_g", (slice(None), slice(None)), total_ref[TOTAL_HEAD + HEAD_FINAL:TOTAL_HEAD + HEAD_FINAL + 1, :])
        update("conv_w", (slice(None),) * 3, gconv_ref[...])
        update("w_ple_proj", (slice(None),) * 3, gproj_ref[...])

    flat = [a for nm in names for a in st[nm]]
    out_shape = []
    for nm in names:
        out_shape += [jax.ShapeDtypeStruct(st[nm][0].shape, F32)] * (4 if nm in cut else 3)
    def whole(a):
        return pl.BlockSpec(a.shape, lambda i: (0,) * len(a.shape))

    operands = [pltpu.with_memory_space_constraint(a, pltpu.HBM) for a in (total, g_conv, g_proj, *flat)]
    outs = pl.pallas_call(
        body,
        name="adamw_small",
        grid=(1,),
        out_shape=[pltpu.HBM(a.shape, a.dtype) for a in out_shape],
        in_specs=[whole(a) for a in operands],
        out_specs=[whole(a) for a in out_shape],
        compiler_params=pltpu.CompilerParams(dimension_semantics=("arbitrary",), vmem_limit_bytes=32 * MIB),
    )(*operands)
    res, pos = {}, 0
    for nm in names:
        cnt = 4 if nm in cut else 3
        got = tuple(outs[pos:pos + cnt])
        res[nm] = got if nm in cut else ((g_conv if nm == "conv_w" else g_proj),) + got
        pos += cnt
    return res


def _split3_bf16(a):
    b1 = a.astype(BF16)
    r1 = a - b1.astype(F32)
    b2 = r1.astype(BF16)
    b3 = (r1 - b2.astype(F32)).astype(BF16)
    return b1, b2, b3


def _pack_weight_shard(w_in_l, w_out_l, w_gate_l, w_proj_l, conv_w_l):
    w_in_t = jnp.transpose(w_in_l).astype(BF16)
    proj_t = jnp.transpose(w_proj_l).astype(BF16)
    proj_rows = proj_t.reshape(D_MODEL // PLE_DIM, ROWS_PROJ, PLE_DIM).transpose(1, 0, 2).reshape(ROWS_PROJ, D_MODEL)
    conv_parts = jnp.concatenate([b.reshape(-1) for b in _split3_bf16(conv_w_l)])
    conv_rows = jnp.concatenate([conv_parts, jnp.zeros((ROWS_CONV * D_MODEL - conv_parts.shape[0],), BF16)])
    return jnp.concatenate(
        [w_in_t, w_out_l.astype(BF16), w_gate_l.astype(BF16), proj_rows, conv_rows.reshape(ROWS_CONV, D_MODEL)], axis=0
    )


def _unpack_conv(wg):
    per_dev = wg.reshape(N_DEV, ROWS_LAYER, D_MODEL)
    n_conv = (WIDTH_B // N_DEV) * 3
    conv_parts = per_dev[:, OFF_CONV].astype(F32)[:, :3 * n_conv].reshape(N_DEV, 3, n_conv)
    conv = (conv_parts[:, 0] + conv_parts[:, 1]) + conv_parts[:, 2]
    conv_k = jnp.transpose(conv.reshape(WIDTH_B, 3))
    conv_k = jnp.concatenate([conv_k, jnp.zeros((5, WIDTH_B), F32)], axis=0)
    return conv_k


def _unpack_grad_proj(red):
    proj_rows = red[OFF_PROJ:OFF_PROJ + ROWS_PROJ]
    proj_t = proj_rows.reshape(ROWS_PROJ, D_MODEL // PLE_DIM, PLE_DIM).transpose(1, 0, 2).reshape(ROWS_OUT, PLE_DIM)
    return jnp.transpose(proj_t)


def kernel(x, p, norm_g, w_in, ln_v_g, ln_v_b, w_s, b_s, conv_w, w_out, ple_norm_g, w_ple_gate, w_ple_proj, final_g, loss_target, m_norm_g, m_w_in, m_ln_v_g, m_ln_v_b, m_w_s, m_b_s, m_conv_w, m_w_out, m_ple_norm_g, m_w_ple_gate, m_w_ple_proj, m_final_g, v_norm_g, v_w_in, v_ln_v_g, v_ln_v_b, v_w_s, v_b_s, v_conv_w, v_w_out, v_ple_norm_g, v_w_ple_gate, v_w_ple_proj, v_final_g):
    me = 4 * lax.axis_index("x") + 2 * lax.axis_index("y") + lax.axis_index("c")
    xs = x[0]
    target = loss_target[0]

    shards = [_pack_weight_shard(w_in[l], w_out[l], w_ple_gate[l], w_ple_proj[l], conv_w[l]) for l in range(DEPTH)]
    tril = jnp.tril(jnp.ones((CHUNK, CHUNK), F32))

    def consts(l, wg_l):
        conv_k = _unpack_conv(wg_l)
        w_mix = w_s[l] * tril[None]
        small = dict(
            conv_k=conv_k,
            norm_g=norm_g[l].reshape(1, D_MODEL), ln_g=ln_v_g[l].reshape(1, WIDTH_A), ln_b=ln_v_b[l].reshape(1, WIDTH_A),
            w_mix=w_mix.astype(BF16), w_mix_t=jnp.swapaxes(w_mix, 1, 2).astype(BF16),
            b_mix=jnp.broadcast_to(b_s[l][:, :, None], (HEADS_A, CHUNK, HEAD_DIM)),
            ple_g=ple_norm_g[l].reshape(1, D_MODEL),
        )
        return dict({k: pltpu.with_memory_space_constraint(a, pltpu.HBM) for k, a in small.items()}, wg=wg_l)

    layer_consts = [consts(0, _all_gather_rows(shards[0]))]
    saved = []
    h = xs
    for l in range(DEPTH):
        k = layer_consts[l]
        outs = _forward_layer(
            l, h, p, k["wg"], k["conv_k"], k["norm_g"], k["ln_g"], k["ln_b"], k["w_mix"], k["b_mix"],
            k["ple_g"], next_shard=shards[l + 1] if l + 1 < DEPTH else None)
        proj, hn, cat, r, gpre, x1, x2 = outs[:7]
        if l + 1 < DEPTH:
            layer_consts.append(consts(l + 1, outs[7]))
        saved.append(dict(x_in=h, proj=proj, hn=hn, cat=cat, r=r, gpre=gpre, x1=x1))
        h = x2

    smalls, dws = [None] * DEPTH, [None] * DEPTH
    reduced = [None] * DEPTH
    pending = None
    dx = h
    for l in reversed(range(DEPTH)):
        k, s = layer_consts[l], saved[l]
        outs = _backward_layer(
            l, dx, s["x_in"], s["x1"], s["proj"], s["gpre"], p, k["wg"], k["conv_k"],
            k["norm_g"], k["ln_g"], k["ln_b"], k["w_mix"], k["w_mix_t"], k["b_mix"], k["ple_g"],
            loss_head=(target, final_g.reshape(1, D_MODEL)) if l == DEPTH - 1 else None)
        dx, dproj, dx1, dgpre, dpp, smalls[l], dws[l] = outs[:7]
        if l == DEPTH - 1:
            head = outs[7]
        (pack,) = _weight_grads(l, dproj, s["hn"], s["cat"], dx1, s["r"], dgpre, dpp, p)
        if pending is not None:
            sems, pack_thru, land_thru = pending
            pieces = _scatter_wait(l + 1, sems, pack_thru, land_thru, pack)
            reduced[l + 1] = _sum_pieces(l + 1, pltpu.with_memory_space_constraint(pieces, pltpu.HBM))
        pending = _scatter_start(l, pack) if l > 0 else pack
    reduced[0], total = _reduce_scatter_all_reduce(0, pending, smalls, head, dws)
    grad_x = dx[None]
    loss = total[TOTAL_HEAD + HEAD_LOSS, 0]

    n_ch = WIDTH_B // N_DEV
    g_conv = jnp.stack([total[l * SMALL_ROWS + SMALL_CONV:l * SMALL_ROWS + SMALL_CONV + 3, 0:WIDTH_B] for l in range(DEPTH)], axis=1)
    g_conv = lax.dynamic_slice_in_dim(g_conv, me * n_ch, n_ch, axis=2)
    g_proj = jnp.stack([_unpack_grad_proj(reduced[l]) for l in range(DEPTH)])

    def t_in(a):
        return jnp.swapaxes(a, 1, 2)

    def t_conv(a):
        return jnp.transpose(a, (2, 0, 1))

    (r_in,) = _adamw_rows("adamw_w_in", reduced, [OFF_IN], [(t_in(w_in), t_in(m_w_in), t_in(v_w_in))])
    r_out, r_gate = _adamw_rows(
        "adamw_w_out_gate", reduced, [OFF_OUT, OFF_GATE],
        [(w_out, m_w_out, v_w_out), (w_ple_gate, m_w_ple_gate, v_w_ple_gate)])
    small = _adamw_small(total, g_conv, g_proj, dict(
        norm_g=(norm_g, m_norm_g, v_norm_g), ple_norm_g=(ple_norm_g, m_ple_norm_g, v_ple_norm_g),
        ln_v_g=(ln_v_g, m_ln_v_g, v_ln_v_g), ln_v_b=(ln_v_b, m_ln_v_b, v_ln_v_b),
        b_s=(b_s, m_b_s, v_b_s), w_s=(w_s, m_w_s, v_w_s),
        final_g=tuple(a.reshape(1, D_MODEL) for a in (final_g, m_final_g, v_final_g)),
        conv_w=(t_conv(conv_w), t_conv(m_conv_w), t_conv(v_conv_w)),
        w_ple_proj=(w_ple_proj, m_w_ple_proj, v_w_ple_proj),
    ))
    res = dict(small, w_in=tuple(t_in(a) for a in r_in), w_out=r_out, w_ple_gate=r_gate)
    res["final_g"] = tuple(a.reshape(D_MODEL) for a in res["final_g"])
    res["conv_w"] = tuple(jnp.transpose(a, (1, 2, 0)) for a in res["conv_w"])
    order = ["norm_g", "w_in", "ln_v_g", "ln_v_b", "w_s", "b_s", "conv_w", "w_out", "ple_norm_g", "w_ple_gate", "w_ple_proj", "final_g"]
    return (loss, grad_x, *[res[n][0] for n in order], *[res[n][1] for n in order],
            *[res[n][2] for n in order], *[res[n][3] for n in order])
```

```python
import jax
import jax.numpy as jnp
from jax import lax
from jax.experimental import pallas as pl
from jax.experimental.pallas import tpu as pltpu

F32 = jnp.float32
BF16 = jnp.bfloat16

D_MODEL = 1024
WIDTH_A = 512
WIDTH_B = 512
HEADS_A = 4
HEAD_DIM = 128
CHUNK = 128
PLE_DIM = 256
PROJ_WIDTH = 3584
DEPTH = 2
EPS = 1e-6
N_DEV = 8

ADAM_LR = 0.001
ADAM_B1 = 0.9
ADAM_B2 = 0.999
ADAM_EPS = 1e-08
ADAM_WD = 0.01
ADAM_STEP = 10

ROWS_IN = PROJ_WIDTH // N_DEV
ROWS_OUT = D_MODEL // N_DEV
ROWS_GATE = D_MODEL // N_DEV
ROWS_PROJ = (D_MODEL // N_DEV) * PLE_DIM // D_MODEL
ROWS_CONV = 16
OFF_IN = 0
OFF_OUT = OFF_IN + ROWS_IN
OFF_GATE = OFF_OUT + ROWS_OUT
OFF_PROJ = OFF_GATE + ROWS_GATE
OFF_CONV = OFF_PROJ + ROWS_PROJ
ROWS_GRAD = OFF_CONV
ROWS_LAYER = OFF_CONV + ROWS_CONV

SMALL_ROWS = 8
SMALL_NORM = 0
SMALL_PLE = 1
SMALL_LN = 2
SMALL_BS = 3
SMALL_CONV = 4
HEAD_FINAL = 0
HEAD_LOSS = 1
TOTAL_HEAD = DEPTH * SMALL_ROWS
TOTAL_WS = TOTAL_HEAD + SMALL_ROWS
TOTAL_ROWS = TOTAL_WS + CHUNK

MIB = 1024 * 1024
MESH = pl.DeviceIdType.MESH

NT_DIMS = (((1,), (1,)), ((), ()))
TN_DIMS = (((0,), (0,)), ((), ()))


def _dot(a, b):
    return jnp.dot(a, b, preferred_element_type=F32)


def _dot_nt(a, b):
    return lax.dot_general(a, b, NT_DIMS, preferred_element_type=F32)


def _dot_tn(a, b):
    return lax.dot_general(a, b, TN_DIMS, preferred_element_type=F32)


def _colsum8(a):
    rows, n = a.shape
    return jnp.sum(a.reshape(rows // 8, 8, n), axis=0)


def _sigmoid(z):
    return 1.0 / (1.0 + jnp.exp(-z))


def _tile(t, want):
    return want if t % want == 0 else t


class _TwoLevelGather:
    def __init__(self, x_ref, out_ref, m_per, send_sems, recv_sems, local_sem):
        x, y, c = lax.axis_index("x"), lax.axis_index("y"), lax.axis_index("c")
        self.me, self.sibling = (x, y, c), (x, y, 1 - c)
        self.xn, self.yn, self.diag = (1 - x, y, c), (x, 1 - y, c), (1 - x, 1 - y, c)
        self.x_ref, self.out_ref, self.m_per = x_ref, out_ref, m_per
        self.half = (m_per // 32) * 16
        self.send_sems, self.recv_sems = send_sems, recv_sems
        self.mine = pltpu.make_async_copy(x_ref, self.rows(self.me), local_sem)

    def rows(self, block, part=None):
        px, py, pc = block
        base = (4 * px + 2 * py + pc) * self.m_per
        if part is None:
            return self.out_ref.at[pl.ds(base, self.m_per), :]
        if part == 0:
            return self.out_ref.at[pl.ds(base, self.half), :]
        return self.out_ref.at[pl.ds(base + self.half, self.m_per - self.half), :]

    def copy(self, k, block, to, src=None, part=None):
        return pltpu.make_async_remote_copy(
            src_ref=self.rows(block, part) if src is None else src,
            dst_ref=self.rows(block, part),
            send_sem=self.send_sems.at[k],
            recv_sem=self.recv_sems.at[k],
            device_id=to,
            device_id_type=MESH,
        )

    def first(self):
        return [self.copy(0, self.me, self.sibling, src=self.x_ref),
                self.copy(1, self.me, self.xn, src=self.x_ref),
                self.copy(2, self.me, self.yn, src=self.x_ref)]

    def second(self):
        return [self.copy(3, self.xn, self.yn, part=0), self.copy(7, self.yn, self.xn, part=1),
                self.copy(4, self.xn, self.sibling), self.copy(5, self.yn, self.sibling)]

    def third(self):
        return [self.copy(6, self.diag, self.sibling)]

    def start(self):
        self.mine.start()
        for cp in self.first():
            cp.start()

    def pass_on(self):
        fwd_x, fwd_y, sib_x, sib_y = self.second()
        self.copy(1, self.xn, self.me).wait_recv()
        fwd_x.start()
        sib_x.start()
        self.copy(2, self.yn, self.me).wait_recv()
        fwd_y.start()
        sib_y.start()

    def pass_on_diagonal(self):
        self.copy(3, self.diag, self.me, part=0).wait_recv()
        self.copy(7, self.diag, self.me, part=1).wait_recv()
        self.third()[0].start()

    def finish(self):
        sib = (self.sibling[0], self.sibling[1], self.sibling[2])
        self.copy(0, sib, self.me).wait_recv()
        for k, chip in ((4, self.xn), (5, self.yn), (6, self.diag)):
            self.copy(k, (chip[0], chip[1], sib[2]), self.me).wait_recv()
        for cp in self.first() + self.second() + self.third():
            cp.wait_send()
        self.mine.wait()


GATHER_SEMS = [pltpu.SemaphoreType.DMA((8,)), pltpu.SemaphoreType.DMA((8,)), pltpu.SemaphoreType.DMA]


def _all_gather_rows(shard):
    m_per, n = shard.shape

    def body(x_ref, out_ref, send_sems, recv_sems, local_sem):
        ag = _TwoLevelGather(x_ref, out_ref, m_per, send_sems, recv_sems, local_sem)
        ag.start()
        ag.pass_on()
        ag.pass_on_diagonal()
        ag.finish()

    return pl.pallas_call(
        body,
        name="weights_all_gather",
        out_shape=pltpu.HBM((N_DEV * m_per, n), shard.dtype),
        in_specs=[pl.BlockSpec(memory_space=pltpu.HBM)],
        out_specs=pl.BlockSpec(memory_space=pltpu.HBM),
        scratch_shapes=list(GATHER_SEMS),
    )(pltpu.with_memory_space_constraint(shard, pltpu.HBM))


PROJ_PARTS = D_MODEL // PLE_DIM
N_WEIGHT_COPIES = N_DEV * (3 + PROJ_PARTS)


def _weight_copies(wg_ref, w_in_t, w_out, w_gate, w_proj_t, sems):
    copies = []
    for s in range(N_DEV):
        base = s * ROWS_LAYER
        for dst, off, rows in ((w_in_t, OFF_IN, ROWS_IN), (w_out, OFF_OUT, ROWS_OUT), (w_gate, OFF_GATE, ROWS_GATE)):
            copies.append((wg_ref.at[pl.ds(base + off, rows), :], dst.at[pl.ds(s * rows, rows), :]))
        for j in range(PROJ_PARTS):
            copies.append((
                wg_ref.at[pl.ds(base + OFF_PROJ, ROWS_PROJ), pl.ds(j * PLE_DIM, PLE_DIM)],
                w_proj_t.at[pl.ds(s * ROWS_OUT + j * ROWS_PROJ, ROWS_PROJ), :],
            ))
    return [pltpu.make_async_copy(src, dst, sems.at[k]) for k, (src, dst) in enumerate(copies)]


def _forward_layer(layer, x, p_all, wg, conv_k, norm_g, ln_g, ln_b, w_mix, b_mix, ple_g, next_shard=None):
    t = x.shape[0]
    tm = _tile(t, 512)
    sub = _tile(tm, 256)
    nt = t // tm
    gathers = next_shard is not None

    def body(*refs):
        (x_ref, p_ref, wg_ref, cw_ref, ng_ref, lng_ref, lnb_ref, wm_ref, bm_ref, pg_ref) = refs[:10]
        refs = refs[10:]
        if gathers:
            shard_ref, refs = refs[0], refs[1:]
        (proj_ref, hn_ref, cat_ref, r_ref, gpre_ref, x1_ref, x2_ref) = refs[:7]
        refs = refs[7:]
        if gathers:
            gathered_ref, refs = refs[0], refs[1:]
        (w_in_t, w_out, w_gate, wpt_ref, vln_s, mixed_s, halo_s, sems) = refs[:8]
        i = pl.program_id(0)
        if gathers:
            ag = _TwoLevelGather(shard_ref, gathered_ref, ROWS_LAYER, *refs[8:11])

            @pl.when(i == 0)
            def _():
                ag.start()

            @pl.when(i == (5 * nt) // 16)
            def _():
                ag.pass_on()

            @pl.when(i == nt // 2)
            def _():
                ag.pass_on_diagonal()

        @pl.when(i == 0)
        def _():
            copies = _weight_copies(wg_ref, w_in_t, w_out, w_gate, wpt_ref, sems)
            for cp in copies:
                cp.start()
            halo_s[...] = jnp.zeros_like(halo_s)
            for cp in copies:
                cp.wait()

        prev = halo_s[...]
        cw = cw_ref[...]
        row = lax.broadcasted_iota(jnp.int32, (sub, WIDTH_B), 0)
        for sb in range(tm // sub):
            rows = pl.ds(sb * sub, sub)
            xv = x_ref[rows, :]
            rstd0 = lax.rsqrt(jnp.mean(xv * xv, axis=-1, keepdims=True) + EPS)
            hn_ref[rows, :] = (xv * rstd0 * ng_ref[...]).astype(BF16)

            def proj_section(k):
                sec = _dot_nt(hn_ref[rows, :], w_in_t[pl.ds(k * 512, 512), :])
                proj_ref[rows, k * 512:(k + 1) * 512] = sec.astype(BF16)
                return sec

            v = proj_section(1)
            mu = jnp.mean(v, axis=-1, keepdims=True)
            vc = v - mu
            var = jnp.mean(vc * vc, axis=-1, keepdims=True)
            vln = vc * lax.rsqrt(var + EPS) * lng_ref[...] + lnb_ref[...]
            vln_s[rows, :] = vln.astype(BF16)
            for ci in range(sub // CHUNK):
                crows = pl.ds(sb * sub + ci * CHUNK, CHUNK)
                for h in range(HEADS_A):
                    cols = pl.ds(h * HEAD_DIM, HEAD_DIM)
                    mixed_s[crows, cols] = _dot(wm_ref[h], vln_s[crows, cols]) + bm_ref[h]
            u = proj_section(0)
            za = proj_section(2)
            out_a = u * mixed_s[rows, :] * (za * _sigmoid(za))
            cat_ref[rows, 0:512] = out_a.astype(BF16)

            xc = proj_section(5) * proj_section(3)
            xc_m1 = jnp.where(row == 0, prev[7:8, :], pltpu.roll(xc, 1, 0))
            xc_m2 = jnp.where(row == 0, prev[6:7, :], jnp.where(row == 1, prev[7:8, :], pltpu.roll(xc, 2, 0)))
            prev = xc[sub - 8:sub, :]
            yc = cw[0:1, :] * xc_m2 + cw[1:2, :] * xc_m1 + cw[2:3, :] * xc
            zb = proj_section(6)
            out_b = proj_section(4) * yc * (zb * _sigmoid(zb))
            cat_ref[rows, 512:1024] = out_b.astype(BF16)

            x1 = xv + _dot(cat_ref[rows, :], w_out[...])
            x1_ref[rows, :] = x1
            rstd1 = lax.rsqrt(jnp.mean(x1 * x1, axis=-1, keepdims=True) + EPS)
            r_ref[rows, :] = (x1 * rstd1 * pg_ref[...]).astype(BF16)
            gpre = _dot(r_ref[rows, :], w_gate[...])
            gpre_ref[rows, :] = gpre.astype(BF16)
            pp = _dot_nt(p_ref[rows, :].astype(BF16), wpt_ref[...])
            x2_ref[rows, :] = x1 + _sigmoid(gpre) * pp
        halo_s[...] = prev

        if gathers:
            @pl.when(i == nt - 1)
            def _():
                ag.finish()

    def tok(width):
        return pl.BlockSpec((tm, width), lambda i: (i, 0))

    def whole(shape):
        return pl.BlockSpec(shape, lambda i: (0,) * len(shape))

    hbm = pl.BlockSpec(memory_space=pl.ANY)
    operands = [x, p_all, wg, conv_k, norm_g, ln_g, ln_b, w_mix, b_mix, ple_g]
    in_specs = [
        tok(D_MODEL), pl.BlockSpec((None, None, tm, PLE_DIM), lambda i: (layer, 0, i, 0)), hbm,
        whole((8, WIDTH_B)), whole((1, D_MODEL)), whole((1, WIDTH_A)), whole((1, WIDTH_A)),
        whole((HEADS_A, CHUNK, CHUNK)), whole((HEADS_A, CHUNK, HEAD_DIM)), whole((1, D_MODEL)),
    ]
    out_specs = [tok(PROJ_WIDTH), tok(D_MODEL), tok(D_MODEL), tok(D_MODEL), tok(D_MODEL), tok(D_MODEL), tok(D_MODEL)]
    out_shape = [
        jax.ShapeDtypeStruct((t, PROJ_WIDTH), BF16),
        jax.ShapeDtypeStruct((t, D_MODEL), BF16),
        jax.ShapeDtypeStruct((t, D_MODEL), BF16),
        jax.ShapeDtypeStruct((t, D_MODEL), BF16),
        jax.ShapeDtypeStruct((t, D_MODEL), BF16),
        jax.ShapeDtypeStruct((t, D_MODEL), F32),
        jax.ShapeDtypeStruct((t, D_MODEL), F32),
    ]
    scratch_shapes = [
        pltpu.VMEM((PROJ_WIDTH, D_MODEL), BF16),
        pltpu.VMEM((D_MODEL, D_MODEL), BF16),
        pltpu.VMEM((D_MODEL, D_MODEL), BF16),
        pltpu.VMEM((D_MODEL, PLE_DIM), BF16),
        pltpu.VMEM((tm, WIDTH_A), BF16),
        pltpu.VMEM((tm, WIDTH_A), F32),
        pltpu.VMEM((8, WIDTH_B), F32),
        pltpu.SemaphoreType.DMA((N_WEIGHT_COPIES,)),
    ]
    if gathers:
        operands.append(pltpu.with_memory_space_constraint(next_shard, pltpu.HBM))
        in_specs.append(pl.BlockSpec(memory_space=pltpu.HBM))
        out_specs.append(pl.BlockSpec(memory_space=pltpu.HBM))
        out_shape.append(pltpu.HBM((N_DEV * ROWS_LAYER, D_MODEL), BF16))
        scratch_shapes += list(GATHER_SEMS)

    return pl.pallas_call(
        body,
        name=f"layer{layer}_forward",
        grid=(nt,),
        in_specs=in_specs,
        out_specs=out_specs,
        out_shape=out_shape,
        scratch_shapes=scratch_shapes,
        compiler_params=pltpu.CompilerParams(dimension_semantics=("arbitrary",), vmem_limit_bytes=56 * MIB),
    )(*operands)


class _DirectScatter:
    def __init__(self, pack_ref, pieces_ref, send_sems, recv_sems, local_sem):
        x, y, c = lax.axis_index("x"), lax.axis_index("y"), lax.axis_index("c")
        me = 4 * x + 2 * y + c
        self.copies = []
        for k in range(N_DEV - 1):
            fx, fy, fc = ((k + 1) >> 2) & 1, ((k + 1) >> 1) & 1, (k + 1) & 1
            tx, ty, tc = x ^ fx, y ^ fy, c ^ fc
            self.copies.append(
                pltpu.make_async_remote_copy(
                    src_ref=pack_ref.at[4 * tx + 2 * ty + tc], dst_ref=pieces_ref.at[me],
                    send_sem=send_sems.at[k], recv_sem=recv_sems.at[k],
                    device_id=(tx, ty, tc), device_id_type=MESH,
                )
            )
        self.mine = pltpu.make_async_copy(pack_ref.at[me], pieces_ref.at[me], local_sem)

    def start(self):
        self.mine.start()
        for cp in self.copies:
            cp.start()

    def finish(self):
        for cp in self.copies:
            cp.wait_recv()
        for cp in self.copies:
            cp.wait_send()
        self.mine.wait()


SCATTER_SEMS = [pltpu.SemaphoreType.DMA((N_DEV - 1,)), pltpu.SemaphoreType.DMA((N_DEV - 1,)), pltpu.SemaphoreType.DMA]


def _backward_layer(layer, dx2, x_in, x1, proj, gpre, p_all, wg, conv_k, norm_g, ln_g, ln_b,
                    w_mix, w_mix_t, b_mix, ple_g, loss_head=None):
    t = x_in.shape[0]
    tm = _tile(t, 256)
    nt = t // tm
    n_chunks = tm // CHUNK
    halo_rows = 16
    heads = loss_head is not None

    def body(*refs):
        (dx2_ref, xin_ref, x1_ref, proj_ref, halo_ref, gpre_ref, p_ref, wg_ref, cw_ref,
         ng_ref, lng_ref, lnb_ref, wm_ref, wmt_ref, bm_ref, pg_ref) = refs[:16]
        refs = refs[16:]
        if heads:
            tgt_ref, fg_ref = refs[:2]
            refs = refs[2:]
        (dxin_ref, dproj_ref, dx1_ref, dgpre_ref, dpp_ref, small_ref, dws_ref) = refs[:7]
        refs = refs[7:]
        if heads:
            head_ref, refs = refs[0], refs[1:]
        (w_in_t, w_out, w_gate, wpt_ref, vln_s, mixed_s, dmix_s, dvln_s, carry_s,
         ng_acc, pg_acc, lng_acc, lnb_acc, cw_acc, dbm_ref, sems) = refs[:16]
        if heads:
            loss_acc, fg_acc = refs[16:18]
        i = pl.program_id(0)
        tile = nt - 1 - i

        @pl.when(i == 0)
        def _():
            copies = _weight_copies(wg_ref, w_in_t, w_out, w_gate, wpt_ref, sems)
            for cp in copies:
                cp.start()
            if heads:
                loss_acc[...] = jnp.zeros_like(loss_acc)
                fg_acc[...] = jnp.zeros_like(fg_acc)
            carry_s[...] = jnp.zeros_like(carry_s)
            ng_acc[...] = jnp.zeros_like(ng_acc)
            pg_acc[...] = jnp.zeros_like(pg_acc)
            lng_acc[...] = jnp.zeros_like(lng_acc)
            lnb_acc[...] = jnp.zeros_like(lnb_acc)
            cw_acc[...] = jnp.zeros_like(cw_acc)
            dws_ref[...] = jnp.zeros_like(dws_ref)
            dbm_ref[...] = jnp.zeros_like(dbm_ref)
            for cp in copies:
                cp.wait()

        if heads:
            x2v = dx2_ref[...]
            fg = fg_ref[...]
            rstdf = lax.rsqrt(jnp.mean(x2v * x2v, axis=-1, keepdims=True) + EPS)
            xhatf = x2v * rstdf
            err = xhatf * fg - tgt_ref[...]
            loss_acc[...] += _colsum8(err * err)
            dy = err * (1.0 / D_MODEL)
            fg_acc[...] += _colsum8(dy * xhatf)
            dxhf = dy * fg
            dx2v = rstdf * (dxhf - xhatf * jnp.mean(dxhf * xhatf, axis=-1, keepdims=True))
        else:
            dx2v = dx2_ref[...]

        gate = _sigmoid(gpre_ref[...].astype(F32))
        pp = _dot_nt(p_ref[...].astype(BF16), wpt_ref[...])
        dpp = dx2v * gate
        dpp_ref[...] = dpp.astype(BF16)
        dgpre = (dpp * pp * (1.0 - gate)).astype(BF16)
        dgpre_ref[...] = dgpre
        dr = _dot_nt(dgpre, w_gate[...])
        x1v = x1_ref[...]
        rstd1 = lax.rsqrt(jnp.mean(x1v * x1v, axis=-1, keepdims=True) + EPS)
        xhat1 = x1v * rstd1
        pg_acc[...] += _colsum8(dr * xhat1)
        dxh = dr * pg_ref[...]
        dx1 = dx2v + rstd1 * (dxh - xhat1 * jnp.mean(dxh * xhat1, axis=-1, keepdims=True))
        dx1b = dx1.astype(BF16)
        dx1_ref[...] = dx1b

        dcat = _dot_nt(dx1b, w_out[...])
        dca = dcat[:, 0:512]
        dcb = dcat[:, 512:1024]

        u = proj_ref[:, 0:512]
        v = proj_ref[:, 512:1024].astype(F32)
        za = proj_ref[:, 1024:1536]
        mu = jnp.mean(v, axis=-1, keepdims=True)
        vc = v - mu
        var = jnp.mean(vc * vc, axis=-1, keepdims=True)
        rs = lax.rsqrt(var + EPS)
        vhat = vc * rs
        lng = lng_ref[...]
        vln_s[...] = (vhat * lng + lnb_ref[...]).astype(BF16)
        for ci in range(n_chunks):
            rows = pl.ds(ci * CHUNK, CHUNK)
            for h in range(HEADS_A):
                cols = pl.ds(h * HEAD_DIM, HEAD_DIM)
                mixed_s[rows, cols] = (_dot(wm_ref[h], vln_s[rows, cols]) + bm_ref[h]).astype(BF16)
        mixed = mixed_s[...]
        sga = _sigmoid(za)
        sa = za * sga
        dsa = sga + sa * (1.0 - sga)

        def put_section(k, val):
            dproj_ref[:, k * 512:(k + 1) * 512] = val.astype(BF16)

        dcab = dca.astype(BF16)
        dca_sa = dcab * sa
        put_section(0, dca_sa * mixed)
        dmix_s[...] = dca_sa * u
        put_section(2, (dcab * dsa) * (u * mixed))
        dbm_acc = jnp.zeros((CHUNK, WIDTH_A), F32)
        for ci in range(n_chunks):
            rows = pl.ds(ci * CHUNK, CHUNK)
            dbm_acc = dbm_acc + dmix_s[rows, :].astype(F32)
            for h in range(HEADS_A):
                cols = pl.ds(h * HEAD_DIM, HEAD_DIM)
                dvln_s[rows, cols] = _dot(wmt_ref[h], dmix_s[rows, cols])
                dws_ref[:, cols] += _dot_nt(dmix_s[rows, cols], vln_s[rows, cols])
        dbm_ref[...] += dbm_acc
        dvln = dvln_s[...]
        lng_acc[...] += _colsum8(dvln * vhat)
        lnb_acc[...] += _colsum8(dvln)
        dvh = dvln * lng
        dv = rs * (dvh - jnp.mean(dvh, axis=-1, keepdims=True) - vhat * jnp.mean(dvh * vhat, axis=-1, keepdims=True))
        put_section(1, dv)

        hb = proj_ref[:, 1536:2048].astype(F32)
        gb = proj_ref[:, 2048:2560]
        gc = proj_ref[:, 2560:3072].astype(F32)
        zb = proj_ref[:, 3072:3584]
        xc = gc * hb
        prev = halo_ref[:, 2560:3072].astype(F32) * halo_ref[:, 1536:2048].astype(F32)
        prev = jnp.where(tile > 0, prev, 0.0)
        row = lax.broadcasted_iota(jnp.int32, (tm, WIDTH_B), 0)
        p1 = prev[halo_rows - 1:halo_rows, :]
        p2 = prev[halo_rows - 2:halo_rows - 1, :]
        xc_m1 = jnp.where(row == 0, p1, pltpu.roll(xc, 1, 0))
        xc_m2 = jnp.where(row == 0, p2, jnp.where(row == 1, p1, pltpu.roll(xc, 2, 0)))
        cw = cw_ref[...]
        yc = cw[0:1, :] * xc_m2 + cw[1:2, :] * xc_m1 + cw[2:3, :] * xc
        sgb = _sigmoid(zb)
        sb = zb * sgb
        dsb = sgb + sb * (1.0 - sgb)
        dcbb = dcb.astype(BF16)
        ycb = yc.astype(BF16)
        dcb_sb = dcbb * sb
        put_section(4, dcb_sb * ycb)
        dyc = (dcb_sb * gb).astype(F32)
        put_section(6, (dcbb * dsb) * (gb * ycb))
        nxt = carry_s[...]
        dyc_p1 = jnp.where(row == tm - 1, nxt[0:1, :], pltpu.roll(dyc, tm - 1, 0))
        dyc_p2 = jnp.where(row == tm - 1, nxt[1:2, :], jnp.where(row == tm - 2, nxt[0:1, :], pltpu.roll(dyc, tm - 2, 0)))
        carry_s[...] = dyc[0:8, :]
        dxc = cw[2:3, :] * dyc + cw[1:2, :] * dyc_p1 + cw[0:1, :] * dyc_p2
        cw_acc[0] += _colsum8(dyc * xc_m2)
        cw_acc[1] += _colsum8(dyc * xc_m1)
        cw_acc[2] += _colsum8(dyc * xc)
        put_section(3, dxc * gc)
        put_section(5, dxc * hb)

        dhn = _dot(dproj_ref[...], w_in_t[...])
        xv = xin_ref[...]
        rstd0 = lax.rsqrt(jnp.mean(xv * xv, axis=-1, keepdims=True) + EPS)
        xhat0 = xv * rstd0
        ng_acc[...] += _colsum8(dhn * xhat0)
        dxh0 = dhn * ng_ref[...]
        dxin_ref[...] = dx1 + rstd0 * (dxh0 - xhat0 * jnp.mean(dxh0 * xhat0, axis=-1, keepdims=True))

        @pl.when(i == nt - 1)
        def _():
            small_ref[...] = jnp.zeros_like(small_ref)
            small_ref[SMALL_NORM:SMALL_NORM + 1, :] = jnp.sum(ng_acc[...], axis=0, keepdims=True)
            small_ref[SMALL_PLE:SMALL_PLE + 1, :] = jnp.sum(pg_acc[...], axis=0, keepdims=True)
            small_ref[SMALL_LN:SMALL_LN + 1, 0:WIDTH_A] = jnp.sum(lng_acc[...], axis=0, keepdims=True)
            small_ref[SMALL_LN:SMALL_LN + 1, WIDTH_A:2 * WIDTH_A] = jnp.sum(lnb_acc[...], axis=0, keepdims=True)
            for h in range(HEADS_A):
                cols = pl.ds(h * HEAD_DIM, HEAD_DIM)
                small_ref[SMALL_BS:SMALL_BS + 1, cols] = jnp.sum(jnp.transpose(dbm_ref[:, cols]), axis=0, keepdims=True)
            for k in range(3):
                small_ref[SMALL_CONV + k:SMALL_CONV + k + 1, 0:WIDTH_B] = jnp.sum(cw_acc[k], axis=0, keepdims=True)
            if heads:
                total = jnp.sum(loss_acc[...]) * (0.5 / D_MODEL)
                rows8 = lax.broadcasted_iota(jnp.int32, (SMALL_ROWS, D_MODEL), 0)
                lanes8 = lax.broadcasted_iota(jnp.int32, (SMALL_ROWS, D_MODEL), 1)
                head_ref[...] = jnp.where((rows8 == HEAD_LOSS) & (lanes8 == 0), total, 0.0)
                head_ref[HEAD_FINAL:HEAD_FINAL + 1, :] = jnp.sum(fg_acc[...], axis=0, keepdims=True)

    def tok(width):
        return pl.BlockSpec((tm, width), lambda i: (nt - 1 - i, 0))

    def whole(shape):
        return pl.BlockSpec(shape, lambda i: (0,) * len(shape))

    halo_spec = pl.BlockSpec(
        (halo_rows, PROJ_WIDTH), lambda i: (jnp.maximum((nt - 1 - i) * (tm // halo_rows) - 1, 0), 0)
    )
    hbm = pl.BlockSpec(memory_space=pl.ANY)
    operands = [dx2, x_in, x1, proj, proj, gpre, p_all, wg, conv_k, norm_g, ln_g, ln_b, w_mix, w_mix_t, b_mix, ple_g]
    in_specs = [
        tok(D_MODEL), tok(D_MODEL), tok(D_MODEL), tok(PROJ_WIDTH), halo_spec, tok(D_MODEL),
        pl.BlockSpec((None, None, tm, PLE_DIM), lambda i: (layer, 0, nt - 1 - i, 0)), hbm,
        whole((8, WIDTH_B)), whole((1, D_MODEL)), whole((1, WIDTH_A)), whole((1, WIDTH_A)),
        whole((HEADS_A, CHUNK, CHUNK)), whole((HEADS_A, CHUNK, CHUNK)), whole((HEADS_A, CHUNK, HEAD_DIM)),
        whole((1, D_MODEL)),
    ]
    out_specs = [
        tok(D_MODEL), tok(PROJ_WIDTH), tok(D_MODEL), tok(D_MODEL), tok(D_MODEL),
        whole((SMALL_ROWS, D_MODEL)), whole((CHUNK, WIDTH_A)),
    ]
    out_shape = [
        jax.ShapeDtypeStruct((t, D_MODEL), F32),
        jax.ShapeDtypeStruct((t, PROJ_WIDTH), BF16),
        jax.ShapeDtypeStruct((t, D_MODEL), BF16),
        jax.ShapeDtypeStruct((t, D_MODEL), BF16),
        jax.ShapeDtypeStruct((t, D_MODEL), BF16),
        jax.ShapeDtypeStruct((SMALL_ROWS, D_MODEL), F32),
        jax.ShapeDtypeStruct((CHUNK, WIDTH_A), F32),
    ]
    scratch_shapes = [
        pltpu.VMEM((PROJ_WIDTH, D_MODEL), BF16),
        pltpu.VMEM((D_MODEL, D_MODEL), BF16),
        pltpu.VMEM((D_MODEL, D_MODEL), BF16),
        pltpu.VMEM((D_MODEL, PLE_DIM), BF16),
        pltpu.VMEM((tm, WIDTH_A), BF16),
        pltpu.VMEM((tm, WIDTH_A), BF16),
        pltpu.VMEM((tm, WIDTH_A), BF16),
        pltpu.VMEM((tm, WIDTH_A), F32),
        pltpu.VMEM((8, WIDTH_B), F32),
        pltpu.VMEM((8, D_MODEL), F32),
        pltpu.VMEM((8, D_MODEL), F32),
        pltpu.VMEM((8, WIDTH_A), F32),
        pltpu.VMEM((8, WIDTH_A), F32),
        pltpu.VMEM((3, 8, WIDTH_B), F32),
        pltpu.VMEM((CHUNK, WIDTH_A), F32),
        pltpu.SemaphoreType.DMA((N_WEIGHT_COPIES,)),
    ]
    if heads:
        operands += list(loss_head)
        in_specs += [tok(D_MODEL), whole((1, D_MODEL))]
        out_specs.append(whole((SMALL_ROWS, D_MODEL)))
        out_shape.append(jax.ShapeDtypeStruct((SMALL_ROWS, D_MODEL), F32))
        scratch_shapes += [pltpu.VMEM((8, D_MODEL), F32), pltpu.VMEM((8, D_MODEL), F32)]

    return pl.pallas_call(
        body,
        name=f"layer{layer}_backward",
        grid=(nt,),
        in_specs=in_specs,
        out_specs=out_specs,
        out_shape=out_shape,
        scratch_shapes=scratch_shapes,
        compiler_params=pltpu.CompilerParams(dimension_semantics=("arbitrary",), vmem_limit_bytes=56 * MIB),
    )(*operands)


def _scatter_targets():
    x, y, c = lax.axis_index("x"), lax.axis_index("y"), lax.axis_index("c")
    out = []
    for k in range(N_DEV - 1):
        fx, fy, fc = ((k + 1) >> 2) & 1, ((k + 1) >> 1) & 1, (k + 1) & 1
        tx, ty, tc = x ^ fx, y ^ fy, c ^ fc
        out.append((4 * tx + 2 * ty + tc, (tx, ty, tc)))
    return 4 * x + 2 * y + c, out


def _scatter_start(layer, pack):
    n = N_DEV - 1

    def body(pack_ref, land_ref, *rest):
        sems = rest[:2 * n + 1]
        me, targets = _scatter_targets()
        pltpu.make_async_copy(pack_ref.at[me], land_ref.at[me], sems[2 * n]).start()
        for k, (block, device) in enumerate(targets):
            pltpu.make_async_remote_copy(
                src_ref=pack_ref.at[block], dst_ref=land_ref.at[me], send_sem=sems[k], recv_sem=sems[n + k],
                device_id=device, device_id_type=MESH,
            ).start()

    hbm = pl.BlockSpec(memory_space=pltpu.HBM)
    sem = pl.BlockSpec(memory_space=pltpu.SEMAPHORE)
    outs = pl.pallas_call(
        body,
        name=f"layer{layer}_scatter_start",
        out_shape=(*[pltpu.SemaphoreType.DMA(())] * (2 * n + 1), pltpu.HBM(pack.shape, pack.dtype)),
        in_specs=(hbm, hbm),
        out_specs=(*[sem] * (2 * n + 1), hbm),
        input_output_aliases={1: 2 * n + 1},
        compiler_params=pltpu.CompilerParams(has_side_effects=pltpu.SideEffectType.DATAFLOW_SIDE_EFFECTING),
    )(pack, pltpu.with_memory_space_constraint(lax.empty(pack.shape, pack.dtype), pltpu.HBM))
    return outs[:2 * n + 1], pack, outs[2 * n + 1]


def _scatter_wait(layer, sems, pack_thru, land_thru, after):
    n = N_DEV - 1

    def body(pack_ref, land_ref, *rest):
        sem_refs = rest[:2 * n + 1]
        me, targets = _scatter_targets()
        pltpu.make_async_copy(pack_ref.at[me], land_ref.at[me], sem_refs[2 * n]).wait()
        for k, (block, device) in enumerate(targets):
            copy = pltpu.make_async_remote_copy(
                src_ref=pack_ref.at[block], dst_ref=land_ref.at[me], send_sem=sem_refs[k], recv_sem=sem_refs[n + k],
                device_id=device, device_id_type=MESH,
            )
            copy.wait_send()
            copy.wait_recv()

    hbm = pl.BlockSpec(memory_space=pltpu.HBM)
    sem = pl.BlockSpec(memory_space=pltpu.SEMAPHORE)
    return pl.pallas_call(
        body,
        name=f"layer{layer}_scatter_wait",
        out_shape=pltpu.HBM(pack_thru.shape, pack_thru.dtype),
        in_specs=(hbm, hbm, *[sem] * (2 * n + 1), pl.BlockSpec(memory_space=pl.ANY)),
        out_specs=hbm,
        input_output_aliases={1: 0},
        compiler_params=pltpu.CompilerParams(has_side_effects=pltpu.SideEffectType.DATAFLOW_SIDE_EFFECTING),
    )(pack_thru, land_thru, *sems, after)


def _sum_pieces(layer, pieces):
    rows, n = pieces.shape[1], pieces.shape[2]
    blocks = 2
    rb = rows // blocks

    def body(p_ref, out_ref):
        total = p_ref[0].astype(F32)
        for j in range(1, N_DEV):
            total = total + p_ref[j].astype(F32)
        out_ref[...] = total

    return pl.pallas_call(
        body,
        name=f"layer{layer}_grad_sum",
        grid=(blocks,),
        out_shape=pltpu.HBM((rows, n), F32),
        in_specs=[pl.BlockSpec((N_DEV, rb, n), lambda i: (0, i, 0))],
        out_specs=pl.BlockSpec((rb, n), lambda i: (i, 0)),
        compiler_params=pltpu.CompilerParams(dimension_semantics=("arbitrary",), vmem_limit_bytes=32 * MIB),
    )(pieces)


def _weight_grads(layer, dproj, hn, cat, dx1, r, dgpre, dpp, p_all, scatter_pack=None):
    t = hn.shape[0]
    tk = _tile(t, 512)
    nt = t // tk
    in_blocks = PROJ_WIDTH // 512
    scatters = scatter_pack is not None

    def body(*refs):
        (dproj_ref, hn_ref, cat_ref, dx1_ref, r_ref, dgpre_ref, dpp_ref, p_ref) = refs[:8]
        refs = refs[8:]
        if scatters:
            prior_ref, refs = refs[0], refs[1:]
        pack_ref, refs = refs[0], refs[1:]
        if scatters:
            pieces_ref, refs = refs[0], refs[1:]
        (acc_in, acc_out, acc_gate, acc_proj, stage, sems) = refs[:6]
        i = pl.program_id(0)
        if scatters:
            scatter = _DirectScatter(prior_ref, pieces_ref, *refs[6:9])

            @pl.when(i == 0)
            def _():
                scatter.start()

        @pl.when(i == 0)
        def _():
            acc_in[...] = jnp.zeros_like(acc_in)
            acc_out[...] = jnp.zeros_like(acc_out)
            acc_gate[...] = jnp.zeros_like(acc_gate)
            acc_proj[...] = jnp.zeros_like(acc_proj)

        hnv = hn_ref[...]
        for b in range(in_blocks):
            acc_in[pl.ds(b * 512, 512), :] += _dot_tn(dproj_ref[:, b * 512:(b + 1) * 512], hnv)
        dx1v = dx1_ref[...]
        dgv = dgpre_ref[...]
        for b in range(D_MODEL // 512):
            acc_out[pl.ds(b * 512, 512), :] += _dot_tn(cat_ref[:, b * 512:(b + 1) * 512], dx1v)
            acc_gate[pl.ds(b * 512, 512), :] += _dot_tn(r_ref[:, b * 512:(b + 1) * 512], dgv)
        pv = p_ref[...].astype(BF16)
        for b in range(D_MODEL // 512):
            acc_proj[pl.ds(b * 512, 512), :] += _dot_tn(dpp_ref[:, b * 512:(b + 1) * 512], pv)

        @pl.when(i == nt - 1)
        def _():
            def out_copy(s):
                return pltpu.make_async_copy(stage.at[s % 2], pack_ref.at[s], sems.at[s % 2])

            for s in range(N_DEV):
                if s >= 2:
                    out_copy(s - 2).wait()
                buf = stage.at[s % 2]
                buf[pl.ds(OFF_IN, ROWS_IN), :] = acc_in[pl.ds(s * ROWS_IN, ROWS_IN), :].astype(BF16)
                buf[pl.ds(OFF_OUT, ROWS_OUT), :] = acc_out[pl.ds(s * ROWS_OUT, ROWS_OUT), :].astype(BF16)
                buf[pl.ds(OFF_GATE, ROWS_GATE), :] = acc_gate[pl.ds(s * ROWS_GATE, ROWS_GATE), :].astype(BF16)
                for j in range(D_MODEL // PLE_DIM):
                    buf[pl.ds(OFF_PROJ, ROWS_PROJ), pl.ds(j * PLE_DIM, PLE_DIM)] = acc_proj[
                        pl.ds(s * ROWS_OUT + j * ROWS_PROJ, ROWS_PROJ), :
                    ].astype(BF16)
                out_copy(s).start()
            out_copy(N_DEV - 2).wait()
            out_copy(N_DEV - 1).wait()
            if scatters:
                scatter.finish()

    def tok(width):
        return pl.BlockSpec((tk, width), lambda i: (i, 0))

    hbm = pl.BlockSpec(memory_space=pl.ANY)
    pack_shape = jax.ShapeDtypeStruct((N_DEV, ROWS_GRAD, D_MODEL), BF16)
    operands = [dproj, hn, cat, dx1, r, dgpre, dpp, p_all]
    in_specs = [tok(PROJ_WIDTH), tok(D_MODEL), tok(D_MODEL), tok(D_MODEL), tok(D_MODEL), tok(D_MODEL), tok(D_MODEL),
                pl.BlockSpec((None, None, tk, PLE_DIM), lambda i: (layer, 0, i, 0))]
    out_specs, out_shape = [hbm], [pack_shape]
    scratch_shapes = [
        pltpu.VMEM((PROJ_WIDTH, D_MODEL), F32),
        pltpu.VMEM((D_MODEL, D_MODEL), F32),
        pltpu.VMEM((D_MODEL, D_MODEL), F32),
        pltpu.VMEM((D_MODEL, PLE_DIM), F32),
        pltpu.VMEM((2, ROWS_GRAD, D_MODEL), BF16),
        pltpu.SemaphoreType.DMA((2,)),
    ]
    if scatters:
        operands.append(scatter_pack)
        in_specs.append(hbm)
        out_specs.append(hbm)
        out_shape.append(pack_shape)
        scratch_shapes += list(SCATTER_SEMS)

    return pl.pallas_call(
        body,
        name=f"layer{layer}_weight_grads",
        grid=(nt,),
        in_specs=in_specs,
        out_specs=out_specs,
        out_shape=out_shape,
        scratch_shapes=scratch_shapes,
        compiler_params=pltpu.CompilerParams(dimension_semantics=("arbitrary",), vmem_limit_bytes=58 * MIB),
    )(*operands)


def _reduce_scatter_all_reduce(layer, pack, smalls, head, dws):
    rows, n = pack.shape[1], pack.shape[2]
    assert DEPTH * WIDTH_A == D_MODEL and n == D_MODEL

    def body(g_ref, *refs):
        small_refs, refs = refs[:DEPTH], refs[DEPTH:]
        head_ref, refs = refs[0], refs[1:]
        dws_refs, refs = refs[:DEPTH], refs[DEPTH:]
        (out_ref, total_ref, r1, a_s, r2, via, sp, sr1, sq, send1, recv1, send2, recv2, ssend, srecv) = refs
        x, y, c = lax.axis_index("x"), lax.axis_index("y"), lax.axis_index("c")
        sibling = (x, y, 1 - c)
        chip = 2 * x + y
        flips = [(1, 0), (0, 1), (1, 1)]

        for l in range(DEPTH):
            sp[l * SMALL_ROWS:(l + 1) * SMALL_ROWS, :] = small_refs[l][...]
            sp[TOTAL_WS:TOTAL_ROWS, l * WIDTH_A:(l + 1) * WIDTH_A] = dws_refs[l][...]
        sp[TOTAL_HEAD:TOTAL_WS, :] = head_ref[...]

        small_pair = pltpu.make_async_remote_copy(
            src_ref=sp, dst_ref=sr1, send_sem=ssend.at[0], recv_sem=srecv.at[0], device_id=sibling, device_id_type=MESH
        )

        def to_sibling(j):
            return pltpu.make_async_remote_copy(
                src_ref=g_ref.at[2 * j + 1 - c], dst_ref=r1.at[j], send_sem=send1.at[j], recv_sem=recv1.at[j],
                device_id=sibling, device_id_type=MESH,
            )

        first = [to_sibling(j) for j in range(4)]
        small_pair.start()
        for cp in first:
            cp.start()

        small_pair.wait_recv()
        sq[chip] = sp[...] + sr1[...]
        small_chips = [
            pltpu.make_async_remote_copy(
                src_ref=sq.at[chip], dst_ref=sq.at[chip], send_sem=ssend.at[1 + k], recv_sem=srecv.at[1 + k],
                device_id=(x ^ fx, y ^ fy, c), device_id_type=MESH,
            )
            for k, (fx, fy) in enumerate(flips)
        ]
        for cp in small_chips:
            cp.start()

        for j in range(4):
            first[j].wait_recv()

            @pl.when(chip != j)
            def _():
                a_s[j] = (g_ref[2 * j + c].astype(F32) + r1[j].astype(F32)).astype(BF16)

        half = rows // 2
        lo, hi = pl.ds(0, half), pl.ds(half, rows - half)
        x_nbr, y_nbr = (1 - x, y, c), (x, 1 - y, c)
        chip_x, chip_y, chip_d = 2 * (1 - x) + y, 2 * x + (1 - y), 2 * (1 - x) + (1 - y)

        def ici(k, src, dst, to):
            return pltpu.make_async_remote_copy(
                src_ref=src, dst_ref=dst, send_sem=send2.at[k], recv_sem=recv2.at[k], device_id=to, device_id_type=MESH)

        second = [
            ici(0, a_s.at[chip_d, lo, :], via.at[0], x_nbr),
            ici(1, a_s.at[chip_d, hi, :], via.at[1], y_nbr),
            ici(2, a_s.at[chip_x, lo, :], r2.at[0, lo, :], x_nbr),
            ici(3, a_s.at[chip_y, hi, :], r2.at[1, hi, :], y_nbr),
            ici(4, a_s.at[chip_x, hi, :], r2.at[0, hi, :], x_nbr),
            ici(5, a_s.at[chip_y, lo, :], r2.at[1, lo, :], y_nbr),
        ]
        for cp in second[:4]:
            cp.start()

        out_ref[...] = g_ref[2 * chip + c].astype(F32) + r1[chip].astype(F32)
        for cp in small_chips:
            cp.wait_recv()
        total_ref[...] = ((sq[0] + sq[1]) + sq[2]) + sq[3]

        second[0].wait_recv()
        a_s[chip_y, lo, :] = (a_s[chip_y, lo, :].astype(F32) + via[0].astype(F32)).astype(BF16)
        second[5].start()
        second[1].wait_recv()
        a_s[chip_x, hi, :] = (a_s[chip_x, hi, :].astype(F32) + via[1].astype(F32)).astype(BF16)
        second[4].start()

        second[2].wait_recv()
        second[4].wait_recv()
        out_ref[...] += r2[0].astype(F32)
        second[3].wait_recv()
        second[5].wait_recv()
        out_ref[...] += r2[1].astype(F32)
        small_pair.wait_send()
        for cp in first + small_chips + second:
            cp.wait_send()

    vmem = pl.BlockSpec(memory_space=pltpu.VMEM)
    return pl.pallas_call(
        body,
        name=f"layer{layer}_grad_reduce_scatter",
        out_shape=[jax.ShapeDtypeStruct((rows, n), F32), jax.ShapeDtypeStruct((TOTAL_ROWS, D_MODEL), F32)],
        in_specs=[vmem] * (2 + 2 * DEPTH),
        out_specs=[vmem, vmem],
        scratch_shapes=[
            pltpu.VMEM((4, rows, n), BF16),
            pltpu.VMEM((4, rows, n), BF16),
            pltpu.VMEM((2, rows, n), BF16),
            pltpu.VMEM((2, rows // 2, n), BF16),
            pltpu.VMEM((TOTAL_ROWS, D_MODEL), F32),
            pltpu.VMEM((TOTAL_ROWS, D_MODEL), F32),
            pltpu.VMEM((4, TOTAL_ROWS, D_MODEL), F32),
            pltpu.SemaphoreType.DMA((4,)),
            pltpu.SemaphoreType.DMA((4,)),
            pltpu.SemaphoreType.DMA((6,)),
            pltpu.SemaphoreType.DMA((6,)),
            pltpu.SemaphoreType.DMA((4,)),
            pltpu.SemaphoreType.DMA((4,)),
        ],
        compiler_params=pltpu.CompilerParams(vmem_limit_bytes=48 * MIB),
    )(pack, *smalls, head, *dws)


def _adam_step(w, g, m, v):
    m = ADAM_B1 * m + (1.0 - ADAM_B1) * g
    v = ADAM_B2 * v + (1.0 - ADAM_B2) * (g * g)
    m_hat = m / (1.0 - ADAM_B1 ** ADAM_STEP)
    v_hat = v / (1.0 - ADAM_B2 ** ADAM_STEP)
    return -ADAM_LR * (m_hat / (jnp.sqrt(v_hat) + ADAM_EPS) + ADAM_WD * w), m, v


def _adamw_rows(name, reduced, row_off, states):
    n = len(states)

    def body(*refs):
        red = refs[:DEPTH]
        ins = refs[DEPTH:DEPTH + 3 * n]
        outs = refs[DEPTH + 3 * n:]
        layer = pl.program_id(0)
        for l in range(DEPTH):
            @pl.when(layer == l)
            def _():
                for k in range(n):
                    w_ref, m_ref, v_ref = ins[3 * k:3 * k + 3]
                    g_ref, d_ref, nm_ref, nv_ref = outs[4 * k:4 * k + 4]
                    g = red[l][row_off[k]:row_off[k] + w_ref.shape[0], :]
                    d, m, v = _adam_step(w_ref[...], g, m_ref[...], v_ref[...])
                    g_ref[...] = g
                    d_ref[...] = d
                    nm_ref[...] = m
                    nv_ref[...] = v

    flat = [a for st in states for a in st]
    state_specs, out_specs, out_shape = [], [], []
    for w, _, _ in states:
        spec = pl.BlockSpec((None,) + w.shape[1:], lambda l: (l, 0, 0))
        state_specs += [spec] * 3
        out_specs += [spec] * 4
        out_shape += [jax.ShapeDtypeStruct(w.shape, F32)] * 4
    red_specs = [pl.BlockSpec(a.shape, lambda l: (0, 0)) for a in reduced]
    operands = [pltpu.with_memory_space_constraint(a, pltpu.HBM) for a in (*reduced, *flat)]
    outs = pl.pallas_call(
        body,
        name=name,
        grid=(DEPTH,),
        out_shape=[pltpu.HBM(a.shape, a.dtype) for a in out_shape],
        in_specs=red_specs + state_specs,
        out_specs=out_specs,
        compiler_params=pltpu.CompilerParams(dimension_semantics=("arbitrary",), vmem_limit_bytes=48 * MIB),
    )(*operands)
    return [tuple(outs[4 * k:4 * k + 4]) for k in range(n)]


def _adamw_small(total, g_conv, g_proj, st):
    names = ["norm_g", "ple_norm_g", "ln_v_g", "ln_v_b", "b_s", "w_s", "final_g", "conv_w", "w_ple_proj"]
    cut = names[:7]

    def body(total_ref, gconv_ref, gproj_ref, *refs):
        ins = {nm: refs[3 * k:3 * k + 3] for k, nm in enumerate(names)}
        outs, pos = {}, 3 * len(names)
        for nm in names:
            cnt = 4 if nm in cut else 3
            outs[nm] = refs[pos:pos + cnt]
            pos += cnt

        def update(nm, idx, g):
            w_ref, m_ref, v_ref = ins[nm]
            d, m, v = _adam_step(w_ref[idx], g, m_ref[idx], v_ref[idx])
            o = outs[nm]
            if nm in cut:
                o[0][idx] = g
                o = o[1:]
            o[0][idx] = d
            o[1][idx] = m
            o[2][idx] = v

        tril = (lax.broadcasted_iota(jnp.int32, (CHUNK, CHUNK), 0) >= lax.broadcasted_iota(jnp.int32, (CHUNK, CHUNK), 1))
        for l in range(DEPTH):
            base = l * SMALL_ROWS
            row = (slice(l, l + 1), slice(None))
            update("norm_g", row, total_ref[base + SMALL_NORM:base + SMALL_NORM + 1, :])
            update("ple_norm_g", row, total_ref[base + SMALL_PLE:base + SMALL_PLE + 1, :])
            update("ln_v_g", row, total_ref[base + SMALL_LN:base + SMALL_LN + 1, 0:WIDTH_A])
            update("ln_v_b", row, total_ref[base + SMALL_LN:base + SMALL_LN + 1, WIDTH_A:2 * WIDTH_A])
            for h in range(HEADS_A):
                update("b_s", (l, slice(h, h + 1), slice(None)),
                       total_ref[base + SMALL_BS:base + SMALL_BS + 1, h * HEAD_DIM:(h + 1) * HEAD_DIM])
                lanes = slice(l * WIDTH_A + h * CHUNK, l * WIDTH_A + (h + 1) * CHUNK)
                update("w_s", (l, h), jnp.where(tril, total_ref[TOTAL_WS:TOTAL_ROWS, lanes], 0.0))
        update("final_g", (slice(None), slice(None)), total_ref[TOTAL_HEAD + HEAD_FINAL:TOTAL_HEAD + HEAD_FINAL + 1, :])
        update("conv_w", (slice(None),) * 3, gconv_ref[...])
        update("w_ple_proj", (slice(None),) * 3, gproj_ref[...])

    flat = [a for nm in names for a in st[nm]]
    out_shape = []
    for nm in names:
        out_shape += [jax.ShapeDtypeStruct(st[nm][0].shape, F32)] * (4 if nm in cut else 3)
    def whole(a):
        return pl.BlockSpec(a.shape, lambda i: (0,) * len(a.shape))

    operands = [pltpu.with_memory_space_constraint(a, pltpu.HBM) for a in (total, g_conv, g_proj, *flat)]
    outs = pl.pallas_call(
        body,
        name="adamw_small",
        grid=(1,),
        out_shape=[pltpu.HBM(a.shape, a.dtype) for a in out_shape],
        in_specs=[whole(a) for a in operands],
        out_specs=[whole(a) for a in out_shape],
        compiler_params=pltpu.CompilerParams(dimension_semantics=("arbitrary",), vmem_limit_bytes=32 * MIB),
    )(*operands)
    res, pos = {}, 0
    for nm in names:
        cnt = 4 if nm in cut else 3
        got = tuple(outs[pos:pos + cnt])
        res[nm] = got if nm in cut else ((g_conv if nm == "conv_w" else g_proj),) + got
        pos += cnt
    return res


def _split3_bf16(a):
    b1 = a.astype(BF16)
    r1 = a - b1.astype(F32)
    b2 = r1.astype(BF16)
    b3 = (r1 - b2.astype(F32)).astype(BF16)
    return b1, b2, b3


def _pack_weight_shard(w_in_l, w_out_l, w_gate_l, w_proj_l, conv_w_l):
    w_in_t = jnp.transpose(w_in_l).astype(BF16)
    proj_t = jnp.transpose(w_proj_l).astype(BF16)
    proj_rows = proj_t.reshape(D_MODEL // PLE_DIM, ROWS_PROJ, PLE_DIM).transpose(1, 0, 2).reshape(ROWS_PROJ, D_MODEL)
    conv_parts = jnp.concatenate([b.reshape(-1) for b in _split3_bf16(conv_w_l)])
    conv_rows = jnp.concatenate([conv_parts, jnp.zeros((ROWS_CONV * D_MODEL - conv_parts.shape[0],), BF16)])
    return jnp.concatenate(
        [w_in_t, w_out_l.astype(BF16), w_gate_l.astype(BF16), proj_rows, conv_rows.reshape(ROWS_CONV, D_MODEL)], axis=0
    )


def _unpack_conv(wg):
    per_dev = wg.reshape(N_DEV, ROWS_LAYER, D_MODEL)
    n_conv = (WIDTH_B // N_DEV) * 3
    conv_parts = per_dev[:, OFF_CONV].astype(F32)[:, :3 * n_conv].reshape(N_DEV, 3, n_conv)
    conv = (conv_parts[:, 0] + conv_parts[:, 1]) + conv_parts[:, 2]
    conv_k = jnp.transpose(conv.reshape(WIDTH_B, 3))
    conv_k = jnp.concatenate([conv_k, jnp.zeros((5, WIDTH_B), F32)], axis=0)
    return conv_k


def _unpack_grad_proj(red):
    proj_rows = red[OFF_PROJ:OFF_PROJ + ROWS_PROJ]
    proj_t = proj_rows.reshape(ROWS_PROJ, D_MODEL // PLE_DIM, PLE_DIM).transpose(1, 0, 2).reshape(ROWS_OUT, PLE_DIM)
    return jnp.transpose(proj_t)


def kernel(x, p, norm_g, w_in, ln_v_g, ln_v_b, w_s, b_s, conv_w, w_out, ple_norm_g, w_ple_gate, w_ple_proj, final_g, loss_target, m_norm_g, m_w_in, m_ln_v_g, m_ln_v_b, m_w_s, m_b_s, m_conv_w, m_w_out, m_ple_norm_g, m_w_ple_gate, m_w_ple_proj, m_final_g, v_norm_g, v_w_in, v_ln_v_g, v_ln_v_b, v_w_s, v_b_s, v_conv_w, v_w_out, v_ple_norm_g, v_w_ple_gate, v_w_ple_proj, v_final_g):
    me = 4 * lax.axis_index("x") + 2 * lax.axis_index("y") + lax.axis_index("c")
    xs = x[0]
    target = loss_target[0]

    shards = [_pack_weight_shard(w_in[l], w_out[l], w_ple_gate[l], w_ple_proj[l], conv_w[l]) for l in range(DEPTH)]
    tril = jnp.tril(jnp.ones((CHUNK, CHUNK), F32))

    def consts(l, wg_l):
        conv_k = _unpack_conv(wg_l)
        w_mix = w_s[l] * tril[None]
        small = dict(
            conv_k=conv_k,
            norm_g=norm_g[l].reshape(1, D_MODEL), ln_g=ln_v_g[l].reshape(1, WIDTH_A), ln_b=ln_v_b[l].reshape(1, WIDTH_A),
            w_mix=w_mix.astype(BF16), w_mix_t=jnp.swapaxes(w_mix, 1, 2).astype(BF16),
            b_mix=jnp.broadcast_to(b_s[l][:, :, None], (HEADS_A, CHUNK, HEAD_DIM)),
            ple_g=ple_norm_g[l].reshape(1, D_MODEL),
        )
        return dict({k: pltpu.with_memory_space_constraint(a, pltpu.HBM) for k, a in small.items()}, wg=wg_l)

    layer_consts = [consts(0, _all_gather_rows(shards[0]))]
    saved = []
    h = xs
    for l in range(DEPTH):
        k = layer_consts[l]
        outs = _forward_layer(
            l, h, p, k["wg"], k["conv_k"], k["norm_g"], k["ln_g"], k["ln_b"], k["w_mix"], k["b_mix"],
            k["ple_g"], next_shard=shards[l + 1] if l + 1 < DEPTH else None)
        proj, hn, cat, r, gpre, x1, x2 = outs[:7]
        if l + 1 < DEPTH:
            layer_consts.append(consts(l + 1, outs[7]))
        saved.append(dict(x_in=h, proj=proj, hn=hn, cat=cat, r=r, gpre=gpre, x1=x1))
        h = x2

    smalls, dws = [None] * DEPTH, [None] * DEPTH
    reduced = [None] * DEPTH
    pending = None
    dx = h
    for l in reversed(range(DEPTH)):
        k, s = layer_consts[l], saved[l]
        outs = _backward_layer(
            l, dx, s["x_in"], s["x1"], s["proj"], s["gpre"], p, k["wg"], k["conv_k"],
            k["norm_g"], k["ln_g"], k["ln_b"], k["w_mix"], k["w_mix_t"], k["b_mix"], k["ple_g"],
            loss_head=(target, final_g.reshape(1, D_MODEL)) if l == DEPTH - 1 else None)
        dx, dproj, dx1, dgpre, dpp, smalls[l], dws[l] = outs[:7]
        if l == DEPTH - 1:
            head = outs[7]
        (pack,) = _weight_grads(l, dproj, s["hn"], s["cat"], dx1, s["r"], dgpre, dpp, p)
        if pending is not None:
            sems, pack_thru, land_thru = pending
            pieces = _scatter_wait(l + 1, sems, pack_thru, land_thru, pack)
            reduced[l + 1] = _sum_pieces(l + 1, pltpu.with_memory_space_constraint(pieces, pltpu.HBM))
        pending = _scatter_start(l, pack) if l > 0 else pack
    reduced[0], total = _reduce_scatter_all_reduce(0, pending, smalls, head, dws)
    grad_x = dx[None]
    loss = total[TOTAL_HEAD + HEAD_LOSS, 0]

    n_ch = WIDTH_B // N_DEV
    g_conv = jnp.stack([total[l * SMALL_ROWS + SMALL_CONV:l * SMALL_ROWS + SMALL_CONV + 3, 0:WIDTH_B] for l in range(DEPTH)], axis=1)
    g_conv = lax.dynamic_slice_in_dim(g_conv, me * n_ch, n_ch, axis=2)
    g_proj = jnp.stack([_unpack_grad_proj(reduced[l]) for l in range(DEPTH)])

    def t_in(a):
        return jnp.swapaxes(a, 1, 2)

    def t_conv(a):
        return jnp.transpose(a, (2, 0, 1))

    (r_in,) = _adamw_rows("adamw_w_in", reduced, [OFF_IN], [(t_in(w_in), t_in(m_w_in), t_in(v_w_in))])
    r_out, r_gate = _adamw_rows(
        "adamw_w_out_gate", reduced, [OFF_OUT, OFF_GATE],
        [(w_out, m_w_out, v_w_out), (w_ple_gate, m_w_ple_gate, v_w_ple_gate)])
    small = _adamw_small(total, g_conv, g_proj, dict(
        norm_g=(norm_g, m_norm_g, v_norm_g), ple_norm_g=(ple_norm_g, m_ple_norm_g, v_ple_norm_g),
        ln_v_g=(ln_v_g, m_ln_v_g, v_ln_v_g), ln_v_b=(ln_v_b, m_ln_v_b, v_ln_v_b),
        b_s=(b_s, m_b_s, v_b_s), w_s=(w_s, m_w_s, v_w_s),
        final_g=tuple(a.reshape(1, D_MODEL) for a in (final_g, m_final_g, v_final_g)),
        conv_w=(t_conv(conv_w), t_conv(m_conv_w), t_conv(v_conv_w)),
        w_ple_proj=(w_ple_proj, m_w_ple_proj, v_w_ple_proj),
    ))
    res = dict(small, w_in=tuple(t_in(a) for a in r_in), w_out=r_out, w_ple_gate=r_gate)
    res["final_g"] = tuple(a.reshape(D_MODEL) for a in res["final_g"])
    res["conv_w"] = tuple(jnp.transpose(a, (1, 2, 0)) for a in res["conv_w"])
    order = ["norm_g", "w_in", "ln_v_g", "ln_v_b", "w_s", "b_s", "conv_w", "w_out", "ple_norm_g", "w_ple_gate", "w_ple_proj", "final_g"]
    return (loss, grad_x, *[res[n][0] for n in order], *[res[n][1] for n in order],
            *[res[n][2] for n in order], *[res[n][3] for n in order])
```

```python
import jax
import jax.numpy as jnp
from jax import lax
from jax.experimental import pallas as pl
from jax.experimental.pallas import tpu as pltpu

F32 = jnp.float32
BF16 = jnp.bfloat16

D_MODEL = 1024
WIDTH_A = 512
WIDTH_B = 512
HEADS_A = 4
HEAD_DIM = 128
CHUNK = 128
PLE_DIM = 256
PROJ_WIDTH = 3584
DEPTH = 2
EPS = 1e-6
N_DEV = 8

ADAM_LR = 0.001
ADAM_B1 = 0.9
ADAM_B2 = 0.999
ADAM_EPS = 1e-08
ADAM_WD = 0.01
ADAM_STEP = 10

ROWS_IN = PROJ_WIDTH // N_DEV
ROWS_OUT = D_MODEL // N_DEV
ROWS_GATE = D_MODEL // N_DEV
ROWS_PROJ = (D_MODEL // N_DEV) * PLE_DIM // D_MODEL
ROWS_CONV = 16
OFF_IN = 0
OFF_OUT = OFF_IN + ROWS_IN
OFF_GATE = OFF_OUT + ROWS_OUT
OFF_PROJ = OFF_GATE + ROWS_GATE
OFF_CONV = OFF_PROJ + ROWS_PROJ
ROWS_GRAD = OFF_CONV
ROWS_LAYER = OFF_CONV + ROWS_CONV

SMALL_ROWS = 8
SMALL_NORM = 0
SMALL_PLE = 1
SMALL_LN = 2
SMALL_BS = 3
SMALL_CONV = 4
HEAD_FINAL = 0
HEAD_LOSS = 1
TOTAL_HEAD = DEPTH * SMALL_ROWS
TOTAL_WS = TOTAL_HEAD + SMALL_ROWS
TOTAL_ROWS = TOTAL_WS + CHUNK

MIB = 1024 * 1024
MESH = pl.DeviceIdType.MESH

NT_DIMS = (((1,), (1,)), ((), ()))
TN_DIMS = (((0,), (0,)), ((), ()))


def _dot(a, b):
    return jnp.dot(a, b, preferred_element_type=F32)


def _dot_nt(a, b):
    return lax.dot_general(a, b, NT_DIMS, preferred_element_type=F32)


def _dot_tn(a, b):
    return lax.dot_general(a, b, TN_DIMS, preferred_element_type=F32)


def _colsum8(a):
    rows, n = a.shape
    return jnp.sum(a.reshape(rows // 8, 8, n), axis=0)


def _sigmoid(z):
    return 1.0 / (1.0 + jnp.exp(-z))


def _tile(t, want):
    return want if t % want == 0 else t


class _TwoLevelGather:
    def __init__(self, x_ref, out_ref, m_per, send_sems, recv_sems, local_sem):
        x, y, c = lax.axis_index("x"), lax.axis_index("y"), lax.axis_index("c")
        self.me, self.sibling = (x, y, c), (x, y, 1 - c)
        self.xn, self.yn, self.diag = (1 - x, y, c), (x, 1 - y, c), (1 - x, 1 - y, c)
        self.x_ref, self.out_ref, self.m_per = x_ref, out_ref, m_per
        self.half = (m_per // 32) * 16
        self.send_sems, self.recv_sems = send_sems, recv_sems
        self.mine = pltpu.make_async_copy(x_ref, self.rows(self.me), local_sem)

    def rows(self, block, part=None):
        px, py, pc = block
        base = (4 * px + 2 * py + pc) * self.m_per
        if part is None:
            return self.out_ref.at[pl.ds(base, self.m_per), :]
        if part == 0:
            return self.out_ref.at[pl.ds(base, self.half), :]
        return self.out_ref.at[pl.ds(base + self.half, self.m_per - self.half), :]

    def copy(self, k, block, to, src=None, part=None):
        return pltpu.make_async_remote_copy(
            src_ref=self.rows(block, part) if src is None else src,
            dst_ref=self.rows(block, part),
            send_sem=self.send_sems.at[k],
            recv_sem=self.recv_sems.at[k],
            device_id=to,
            device_id_type=MESH,
        )

    def first(self):
        return [self.copy(0, self.me, self.sibling, src=self.x_ref),
                self.copy(1, self.me, self.xn, src=self.x_ref),
                self.copy(2, self.me, self.yn, src=self.x_ref)]

    def second(self):
        return [self.copy(3, self.xn, self.yn, part=0), self.copy(7, self.yn, self.xn, part=1),
                self.copy(4, self.xn, self.sibling), self.copy(5, self.yn, self.sibling)]

    def third(self):
        return [self.copy(6, self.diag, self.sibling)]

    def start(self):
        self.mine.start()
        for cp in self.first():
            cp.start()

    def pass_on(self):
        fwd_x, fwd_y, sib_x, sib_y = self.second()
        self.copy(1, self.xn, self.me).wait_recv()
        fwd_x.start()
        sib_x.start()
        self.copy(2, self.yn, self.me).wait_recv()
        fwd_y.start()
        sib_y.start()

    def pass_on_diagonal(self):
        self.copy(3, self.diag, self.me, part=0).wait_recv()
        self.copy(7, self.diag, self.me, part=1).wait_recv()
        self.third()[0].start()

    def finish(self):
        sib = (self.sibling[0], self.sibling[1], self.sibling[2])
        self.copy(0, sib, self.me).wait_recv()
        for k, chip in ((4, self.xn), (5, self.yn), (6, self.diag)):
            self.copy(k, (chip[0], chip[1], sib[2]), self.me).wait_recv()
        for cp in self.first() + self.second() + self.third():
            cp.wait_send()
        self.mine.wait()


GATHER_SEMS = [pltpu.SemaphoreType.DMA((8,)), pltpu.SemaphoreType.DMA((8,)), pltpu.SemaphoreType.DMA]


def _all_gather_rows(shard):
    m_per, n = shard.shape

    def body(x_ref, out_ref, send_sems, recv_sems, local_sem):
        ag = _TwoLevelGather(x_ref, out_ref, m_per, send_sems, recv_sems, local_sem)
        ag.start()
        ag.pass_on()
        ag.pass_on_diagonal()
        ag.finish()

    return pl.pallas_call(
        body,
        name="weights_all_gather",
        out_shape=pltpu.HBM((N_DEV * m_per, n), shard.dtype),
        in_specs=[pl.BlockSpec(memory_space=pltpu.HBM)],
        out_specs=pl.BlockSpec(memory_space=pltpu.HBM),
        scratch_shapes=list(GATHER_SEMS),
    )(pltpu.with_memory_space_constraint(shard, pltpu.HBM))


PROJ_PARTS = D_MODEL // PLE_DIM
N_WEIGHT_COPIES = N_DEV * (3 + PROJ_PARTS)


def _weight_copies(wg_ref, w_in_t, w_out, w_gate, w_proj_t, sems):
    copies = []
    for s in range(N_DEV):
        base = s * ROWS_LAYER
        for dst, off, rows in ((w_in_t, OFF_IN, ROWS_IN), (w_out, OFF_OUT, ROWS_OUT), (w_gate, OFF_GATE, ROWS_GATE)):
            copies.append((wg_ref.at[pl.ds(base + off, rows), :], dst.at[pl.ds(s * rows, rows), :]))
        for j in range(PROJ_PARTS):
            copies.append((
                wg_ref.at[pl.ds(base + OFF_PROJ, ROWS_PROJ), pl.ds(j * PLE_DIM, PLE_DIM)],
                w_proj_t.at[pl.ds(s * ROWS_OUT + j * ROWS_PROJ, ROWS_PROJ), :],
            ))
    return [pltpu.make_async_copy(src, dst, sems.at[k]) for k, (src, dst) in enumerate(copies)]


def _forward_layer(layer, x, p_all, wg, conv_k, norm_g, ln_g, ln_b, w_mix, b_mix, ple_g, next_shard=None):
    t = x.shape[0]
    tm = _tile(t, 512)
    nt = t // tm
    gathers = next_shard is not None

    def body(*refs):
        (x_ref, p_ref, wg_ref, cw_ref, ng_ref, lng_ref, lnb_ref, wm_ref, bm_ref, pg_ref) = refs[:10]
        refs = refs[10:]
        if gathers:
            shard_ref, refs = refs[0], refs[1:]
        (proj_ref, hn_ref, cat_ref, r_ref, gpre_ref, x1_ref, x2_ref) = refs[:7]
        refs = refs[7:]
        if gathers:
            gathered_ref, refs = refs[0], refs[1:]
        (w_in_t, w_out, w_gate, wpt_ref, vln_s, mixed_s, halo_s, sems) = refs[:8]
        i = pl.program_id(0)
        if gathers:
            ag = _TwoLevelGather(shard_ref, gathered_ref, ROWS_LAYER, *refs[8:11])

            @pl.when(i == 0)
            def _():
                ag.start()

            @pl.when(i == (5 * nt) // 16)
            def _():
                ag.pass_on()

            @pl.when(i == nt // 2)
            def _():
                ag.pass_on_diagonal()

        @pl.when(i == 0)
        def _():
            copies = _weight_copies(wg_ref, w_in_t, w_out, w_gate, wpt_ref, sems)
            for cp in copies:
                cp.start()
            halo_s[...] = jnp.zeros_like(halo_s)
            for cp in copies:
                cp.wait()

        xv = x_ref[...]
        rstd0 = lax.rsqrt(jnp.mean(xv * xv, axis=-1, keepdims=True) + EPS)
        hn_ref[...] = (xv * rstd0 * ng_ref[...]).astype(BF16)

        def proj_section(k):
            sec = _dot_nt(hn_ref[...], w_in_t[pl.ds(k * 512, 512), :])
            proj_ref[:, k * 512:(k + 1) * 512] = sec.astype(BF16)
            return sec

        v = proj_section(1)
        mu = jnp.mean(v, axis=-1, keepdims=True)
        vc = v - mu
        var = jnp.mean(vc * vc, axis=-1, keepdims=True)
        vln = vc * lax.rsqrt(var + EPS) * lng_ref[...] + lnb_ref[...]
        vln_s[...] = vln.astype(BF16)
        for ci in range(tm // CHUNK):
            rows = pl.ds(ci * CHUNK, CHUNK)
            for h in range(HEADS_A):
                cols = pl.ds(h * HEAD_DIM, HEAD_DIM)
                mixed_s[rows, cols] = _dot(wm_ref[h], vln_s[rows, cols]) + bm_ref[h]
        u = proj_section(0)
        za = proj_section(2)
        out_a = u * mixed_s[...] * (za * _sigmoid(za))
        cat_ref[:, 0:512] = out_a.astype(BF16)

        xc = proj_section(5) * proj_section(3)
        prev = halo_s[...]
        row = lax.broadcasted_iota(jnp.int32, (tm, WIDTH_B), 0)
        xc_m1 = jnp.where(row == 0, prev[7:8, :], pltpu.roll(xc, 1, 0))
        xc_m2 = jnp.where(row == 0, prev[6:7, :], jnp.where(row == 1, prev[7:8, :], pltpu.roll(xc, 2, 0)))
        halo_s[...] = xc[tm - 8:tm, :]
        cw = cw_ref[...]
        yc = cw[0:1, :] * xc_m2 + cw[1:2, :] * xc_m1 + cw[2:3, :] * xc
        zb = proj_section(6)
        out_b = proj_section(4) * yc * (zb * _sigmoid(zb))
        cat_ref[:, 512:1024] = out_b.astype(BF16)

        x1 = xv + _dot(cat_ref[...], w_out[...])
        x1_ref[...] = x1
        rstd1 = lax.rsqrt(jnp.mean(x1 * x1, axis=-1, keepdims=True) + EPS)
        r_ref[...] = (x1 * rstd1 * pg_ref[...]).astype(BF16)
        gpre = _dot(r_ref[...], w_gate[...])
        gpre_ref[...] = gpre.astype(BF16)
        pp = _dot_nt(p_ref[...].astype(BF16), wpt_ref[...])
        x2_ref[...] = x1 + _sigmoid(gpre) * pp

        if gathers:
            @pl.when(i == nt - 1)
            def _():
                ag.finish()

    def tok(width):
        return pl.BlockSpec((tm, width), lambda i: (i, 0))

    def whole(shape):
        return pl.BlockSpec(shape, lambda i: (0,) * len(shape))

    hbm = pl.BlockSpec(memory_space=pl.ANY)
    operands = [x, p_all, wg, conv_k, norm_g, ln_g, ln_b, w_mix, b_mix, ple_g]
    in_specs = [
        tok(D_MODEL), pl.BlockSpec((None, None, tm, PLE_DIM), lambda i: (layer, 0, i, 0)), hbm,
        whole((8, WIDTH_B)), whole((1, D_MODEL)), whole((1, WIDTH_A)), whole((1, WIDTH_A)),
        whole((HEADS_A, CHUNK, CHUNK)), whole((HEADS_A, CHUNK, HEAD_DIM)), whole((1, D_MODEL)),
    ]
    out_specs = [tok(PROJ_WIDTH), tok(D_MODEL), tok(D_MODEL), tok(D_MODEL), tok(D_MODEL), tok(D_MODEL), tok(D_MODEL)]
    out_shape = [
        jax.ShapeDtypeStruct((t, PROJ_WIDTH), BF16),
        jax.ShapeDtypeStruct((t, D_MODEL), BF16),
        jax.ShapeDtypeStruct((t, D_MODEL), BF16),
        jax.ShapeDtypeStruct((t, D_MODEL), BF16),
        jax.ShapeDtypeStruct((t, D_MODEL), BF16),
        jax.ShapeDtypeStruct((t, D_MODEL), F32),
        jax.ShapeDtypeStruct((t, D_MODEL), F32),
    ]
    scratch_shapes = [
        pltpu.VMEM((PROJ_WIDTH, D_MODEL), BF16),
        pltpu.VMEM((D_MODEL, D_MODEL), BF16),
        pltpu.VMEM((D_MODEL, D_MODEL), BF16),
        pltpu.VMEM((D_MODEL, PLE_DIM), BF16),
        pltpu.VMEM((tm, WIDTH_A), BF16),
        pltpu.VMEM((tm, WIDTH_A), F32),
        pltpu.VMEM((8, WIDTH_B), F32),
        pltpu.SemaphoreType.DMA((N_WEIGHT_COPIES,)),
    ]
    if gathers:
        operands.append(pltpu.with_memory_space_constraint(next_shard, pltpu.HBM))
        in_specs.append(pl.BlockSpec(memory_space=pltpu.HBM))
        out_specs.append(pl.BlockSpec(memory_space=pltpu.HBM))
        out_shape.append(pltpu.HBM((N_DEV * ROWS_LAYER, D_MODEL), BF16))
        scratch_shapes += list(GATHER_SEMS)

    return pl.pallas_call(
        body,
        name=f"layer{layer}_forward",
        grid=(nt,),
        in_specs=in_specs,
        out_specs=out_specs,
        out_shape=out_shape,
        scratch_shapes=scratch_shapes,
        compiler_params=pltpu.CompilerParams(dimension_semantics=("arbitrary",), vmem_limit_bytes=56 * MIB),
    )(*operands)


class _DirectScatter:
    def __init__(self, pack_ref, pieces_ref, send_sems, recv_sems, local_sem):
        x, y, c = lax.axis_index("x"), lax.axis_index("y"), lax.axis_index("c")
        me = 4 * x + 2 * y + c
        self.copies = []
        for k in range(N_DEV - 1):
            fx, fy, fc = ((k + 1) >> 2) & 1, ((k + 1) >> 1) & 1, (k + 1) & 1
            tx, ty, tc = x ^ fx, y ^ fy, c ^ fc
            self.copies.append(
                pltpu.make_async_remote_copy(
                    src_ref=pack_ref.at[4 * tx + 2 * ty + tc], dst_ref=pieces_ref.at[me],
                    send_sem=send_sems.at[k], recv_sem=recv_sems.at[k],
                    device_id=(tx, ty, tc), device_id_type=MESH,
                )
            )
        self.mine = pltpu.make_async_copy(pack_ref.at[me], pieces_ref.at[me], local_sem)

    def start(self):
        self.mine.start()
        for cp in self.copies:
            cp.start()

    def finish(self):
        for cp in self.copies:
            cp.wait_recv()
        for cp in self.copies:
            cp.wait_send()
        self.mine.wait()


SCATTER_SEMS = [pltpu.SemaphoreType.DMA((N_DEV - 1,)), pltpu.SemaphoreType.DMA((N_DEV - 1,)), pltpu.SemaphoreType.DMA]


def _backward_layer(layer, dx2, x_in, x1, proj, gpre, p_all, wg, conv_k, norm_g, ln_g, ln_b,
                    w_mix, w_mix_t, b_mix, ple_g, loss_head=None):
    t = x_in.shape[0]
    tm = _tile(t, 256)
    nt = t // tm
    n_chunks = tm // CHUNK
    halo_rows = 16
    heads = loss_head is not None

    def body(*refs):
        (dx2_ref, xin_ref, x1_ref, proj_ref, halo_ref, gpre_ref, p_ref, wg_ref, cw_ref,
         ng_ref, lng_ref, lnb_ref, wm_ref, wmt_ref, bm_ref, pg_ref) = refs[:16]
        refs = refs[16:]
        if heads:
            tgt_ref, fg_ref = refs[:2]
            refs = refs[2:]
        (dxin_ref, dproj_ref, dx1_ref, dgpre_ref, dpp_ref, small_ref, dws_ref) = refs[:7]
        refs = refs[7:]
        if heads:
            head_ref, refs = refs[0], refs[1:]
        (w_in_t, w_out, w_gate, wpt_ref, vln_s, mixed_s, dmix_s, dvln_s, carry_s,
         ng_acc, pg_acc, lng_acc, lnb_acc, cw_acc, dbm_ref, sems) = refs[:16]
        if heads:
            loss_acc, fg_acc = refs[16:18]
        i = pl.program_id(0)
        tile = nt - 1 - i

        @pl.when(i == 0)
        def _():
            copies = _weight_copies(wg_ref, w_in_t, w_out, w_gate, wpt_ref, sems)
            for cp in copies:
                cp.start()
            if heads:
                loss_acc[...] = jnp.zeros_like(loss_acc)
                fg_acc[...] = jnp.zeros_like(fg_acc)
            carry_s[...] = jnp.zeros_like(carry_s)
            ng_acc[...] = jnp.zeros_like(ng_acc)
            pg_acc[...] = jnp.zeros_like(pg_acc)
            lng_acc[...] = jnp.zeros_like(lng_acc)
            lnb_acc[...] = jnp.zeros_like(lnb_acc)
            cw_acc[...] = jnp.zeros_like(cw_acc)
            dws_ref[...] = jnp.zeros_like(dws_ref)
            dbm_ref[...] = jnp.zeros_like(dbm_ref)
            for cp in copies:
                cp.wait()

        if heads:
            x2v = dx2_ref[...]
            fg = fg_ref[...]
            rstdf = lax.rsqrt(jnp.mean(x2v * x2v, axis=-1, keepdims=True) + EPS)
            xhatf = x2v * rstdf
            err = xhatf * fg - tgt_ref[...]
            loss_acc[...] += _colsum8(err * err)
            dy = err * (1.0 / D_MODEL)
            fg_acc[...] += _colsum8(dy * xhatf)
            dxhf = dy * fg
            dx2v = rstdf * (dxhf - xhatf * jnp.mean(dxhf * xhatf, axis=-1, keepdims=True))
        else:
            dx2v = dx2_ref[...]

        gate = _sigmoid(gpre_ref[...].astype(F32))
        pp = _dot_nt(p_ref[...].astype(BF16), wpt_ref[...])
        dpp = dx2v * gate
        dpp_ref[...] = dpp.astype(BF16)
        dgpre = (dpp * pp * (1.0 - gate)).astype(BF16)
        dgpre_ref[...] = dgpre
        dr = _dot_nt(dgpre, w_gate[...])
        x1v = x1_ref[...]
        rstd1 = lax.rsqrt(jnp.mean(x1v * x1v, axis=-1, keepdims=True) + EPS)
        xhat1 = x1v * rstd1
        pg_acc[...] += _colsum8(dr * xhat1)
        dxh = dr * pg_ref[...]
        dx1 = dx2v + rstd1 * (dxh - xhat1 * jnp.mean(dxh * xhat1, axis=-1, keepdims=True))
        dx1b = dx1.astype(BF16)
        dx1_ref[...] = dx1b

        dcat = _dot_nt(dx1b, w_out[...])
        dca = dcat[:, 0:512]
        dcb = dcat[:, 512:1024]

        u = proj_ref[:, 0:512]
        v = proj_ref[:, 512:1024].astype(F32)
        za = proj_ref[:, 1024:1536]
        mu = jnp.mean(v, axis=-1, keepdims=True)
        vc = v - mu
        var = jnp.mean(vc * vc, axis=-1, keepdims=True)
        rs = lax.rsqrt(var + EPS)
        vhat = vc * rs
        lng = lng_ref[...]
        vln_s[...] = (vhat * lng + lnb_ref[...]).astype(BF16)
        for ci in range(n_chunks):
            rows = pl.ds(ci * CHUNK, CHUNK)
            for h in range(HEADS_A):
                cols = pl.ds(h * HEAD_DIM, HEAD_DIM)
                mixed_s[rows, cols] = (_dot(wm_ref[h], vln_s[rows, cols]) + bm_ref[h]).astype(BF16)
        mixed = mixed_s[...]
        sga = _sigmoid(za)
        sa = za * sga
        dsa = sga + sa * (1.0 - sga)

        def put_section(k, val):
            dproj_ref[:, k * 512:(k + 1) * 512] = val.astype(BF16)

        dcab = dca.astype(BF16)
        dca_sa = dcab * sa
        put_section(0, dca_sa * mixed)
        dmix_s[...] = dca_sa * u
        put_section(2, (dcab * dsa) * (u * mixed))
        dbm_acc = jnp.zeros((CHUNK, WIDTH_A), F32)
        for ci in range(n_chunks):
            rows = pl.ds(ci * CHUNK, CHUNK)
            dbm_acc = dbm_acc + dmix_s[rows, :].astype(F32)
            for h in range(HEADS_A):
                cols = pl.ds(h * HEAD_DIM, HEAD_DIM)
                dvln_s[rows, cols] = _dot(wmt_ref[h], dmix_s[rows, cols])
                dws_ref[:, cols] += _dot_nt(dmix_s[rows, cols], vln_s[rows, cols])
        dbm_ref[...] += dbm_acc
        dvln = dvln_s[...]
        lng_acc[...] += _colsum8(dvln * vhat)
        lnb_acc[...] += _colsum8(dvln)
        dvh = dvln * lng
        dv = rs * (dvh - jnp.mean(dvh, axis=-1, keepdims=True) - vhat * jnp.mean(dvh * vhat, axis=-1, keepdims=True))
        put_section(1, dv)

        hb = proj_ref[:, 1536:2048].astype(F32)
        gb = proj_ref[:, 2048:2560]
        gc = proj_ref[:, 2560:3072].astype(F32)
        zb = proj_ref[:, 3072:3584]
        xc = gc * hb
        prev = halo_ref[:, 2560:3072].astype(F32) * halo_ref[:, 1536:2048].astype(F32)
        prev = jnp.where(tile > 0, prev, 0.0)
        row = lax.broadcasted_iota(jnp.int32, (tm, WIDTH_B), 0)
        p1 = prev[halo_rows - 1:halo_rows, :]
        p2 = prev[halo_rows - 2:halo_rows - 1, :]
        xc_m1 = jnp.where(row == 0, p1, pltpu.roll(xc, 1, 0))
        xc_m2 = jnp.where(row == 0, p2, jnp.where(row == 1, p1, pltpu.roll(xc, 2, 0)))
        cw = cw_ref[...]
        yc = cw[0:1, :] * xc_m2 + cw[1:2, :] * xc_m1 + cw[2:3, :] * xc
        sgb = _sigmoid(zb)
        sb = zb * sgb
        dsb = sgb + sb * (1.0 - sgb)
        dcbb = dcb.astype(BF16)
        ycb = yc.astype(BF16)
        dcb_sb = dcbb * sb
        put_section(4, dcb_sb * ycb)
        dyc = (dcb_sb * gb).astype(F32)
        put_section(6, (dcbb * dsb) * (gb * ycb))
        nxt = carry_s[...]
        dyc_p1 = jnp.where(row == tm - 1, nxt[0:1, :], pltpu.roll(dyc, tm - 1, 0))
        dyc_p2 = jnp.where(row == tm - 1, nxt[1:2, :], jnp.where(row == tm - 2, nxt[0:1, :], pltpu.roll(dyc, tm - 2, 0)))
        carry_s[...] = dyc[0:8, :]
        dxc = cw[2:3, :] * dyc + cw[1:2, :] * dyc_p1 + cw[0:1, :] * dyc_p2
        cw_acc[0] += _colsum8(dyc * xc_m2)
        cw_acc[1] += _colsum8(dyc * xc_m1)
        cw_acc[2] += _colsum8(dyc * xc)
        put_section(3, dxc * gc)
        put_section(5, dxc * hb)

        dhn = _dot(dproj_ref[...], w_in_t[...])
        xv = xin_ref[...]
        rstd0 = lax.rsqrt(jnp.mean(xv * xv, axis=-1, keepdims=True) + EPS)
        xhat0 = xv * rstd0
        ng_acc[...] += _colsum8(dhn * xhat0)
        dxh0 = dhn * ng_ref[...]
        dxin_ref[...] = dx1 + rstd0 * (dxh0 - xhat0 * jnp.mean(dxh0 * xhat0, axis=-1, keepdims=True))

        @pl.when(i == nt - 1)
        def _():
            small_ref[...] = jnp.zeros_like(small_ref)
            small_ref[SMALL_NORM:SMALL_NORM + 1, :] = jnp.sum(ng_acc[...], axis=0, keepdims=True)
            small_ref[SMALL_PLE:SMALL_PLE + 1, :] = jnp.sum(pg_acc[...], axis=0, keepdims=True)
            small_ref[SMALL_LN:SMALL_LN + 1, 0:WIDTH_A] = jnp.sum(lng_acc[...], axis=0, keepdims=True)
            small_ref[SMALL_LN:SMALL_LN + 1, WIDTH_A:2 * WIDTH_A] = jnp.sum(lnb_acc[...], axis=0, keepdims=True)
            for h in range(HEADS_A):
                cols = pl.ds(h * HEAD_DIM, HEAD_DIM)
                small_ref[SMALL_BS:SMALL_BS + 1, cols] = jnp.sum(jnp.transpose(dbm_ref[:, cols]), axis=0, keepdims=True)
            for k in range(3):
                small_ref[SMALL_CONV + k:SMALL_CONV + k + 1, 0:WIDTH_B] = jnp.sum(cw_acc[k], axis=0, keepdims=True)
            if heads:
                total = jnp.sum(loss_acc[...]) * (0.5 / D_MODEL)
                rows8 = lax.broadcasted_iota(jnp.int32, (SMALL_ROWS, D_MODEL), 0)
                lanes8 = lax.broadcasted_iota(jnp.int32, (SMALL_ROWS, D_MODEL), 1)
                head_ref[...] = jnp.where((rows8 == HEAD_LOSS) & (lanes8 == 0), total, 0.0)
                head_ref[HEAD_FINAL:HEAD_FINAL + 1, :] = jnp.sum(fg_acc[...], axis=0, keepdims=True)

    def tok(width):
        return pl.BlockSpec((tm, width), lambda i: (nt - 1 - i, 0))

    def whole(shape):
        return pl.BlockSpec(shape, lambda i: (0,) * len(shape))

    halo_spec = pl.BlockSpec(
        (halo_rows, PROJ_WIDTH), lambda i: (jnp.maximum((nt - 1 - i) * (tm // halo_rows) - 1, 0), 0)
    )
    hbm = pl.BlockSpec(memory_space=pl.ANY)
    operands = [dx2, x_in, x1, proj, proj, gpre, p_all, wg, conv_k, norm_g, ln_g, ln_b, w_mix, w_mix_t, b_mix, ple_g]
    in_specs = [
        tok(D_MODEL), tok(D_MODEL), tok(D_MODEL), tok(PROJ_WIDTH), halo_spec, tok(D_MODEL),
        pl.BlockSpec((None, None, tm, PLE_DIM), lambda i: (layer, 0, nt - 1 - i, 0)), hbm,
        whole((8, WIDTH_B)), whole((1, D_MODEL)), whole((1, WIDTH_A)), whole((1, WIDTH_A)),
        whole((HEADS_A, CHUNK, CHUNK)), whole((HEADS_A, CHUNK, CHUNK)), whole((HEADS_A, CHUNK, HEAD_DIM)),
        whole((1, D_MODEL)),
    ]
    out_specs = [
        tok(D_MODEL), tok(PROJ_WIDTH), tok(D_MODEL), tok(D_MODEL), tok(D_MODEL),
        whole((SMALL_ROWS, D_MODEL)), whole((CHUNK, WIDTH_A)),
    ]
    out_shape = [
        jax.ShapeDtypeStruct((t, D_MODEL), F32),
        jax.ShapeDtypeStruct((t, PROJ_WIDTH), BF16),
        jax.ShapeDtypeStruct((t, D_MODEL), BF16),
        jax.ShapeDtypeStruct((t, D_MODEL), BF16),
        jax.ShapeDtypeStruct((t, D_MODEL), BF16),
        jax.ShapeDtypeStruct((SMALL_ROWS, D_MODEL), F32),
        jax.ShapeDtypeStruct((CHUNK, WIDTH_A), F32),
    ]
    scratch_shapes = [
        pltpu.VMEM((PROJ_WIDTH, D_MODEL), BF16),
        pltpu.VMEM((D_MODEL, D_MODEL), BF16),
        pltpu.VMEM((D_MODEL, D_MODEL), BF16),
        pltpu.VMEM((D_MODEL, PLE_DIM), BF16),
        pltpu.VMEM((tm, WIDTH_A), BF16),
        pltpu.VMEM((tm, WIDTH_A), BF16),
        pltpu.VMEM((tm, WIDTH_A), BF16),
        pltpu.VMEM((tm, WIDTH_A), F32),
        pltpu.VMEM((8, WIDTH_B), F32),
        pltpu.VMEM((8, D_MODEL), F32),
        pltpu.VMEM((8, D_MODEL), F32),
        pltpu.VMEM((8, WIDTH_A), F32),
        pltpu.VMEM((8, WIDTH_A), F32),
        pltpu.VMEM((3, 8, WIDTH_B), F32),
        pltpu.VMEM((CHUNK, WIDTH_A), F32),
        pltpu.SemaphoreType.DMA((N_WEIGHT_COPIES,)),
    ]
    if heads:
        operands += list(loss_head)
        in_specs += [tok(D_MODEL), whole((1, D_MODEL))]
        out_specs.append(whole((SMALL_ROWS, D_MODEL)))
        out_shape.append(jax.ShapeDtypeStruct((SMALL_ROWS, D_MODEL), F32))
        scratch_shapes += [pltpu.VMEM((8, D_MODEL), F32), pltpu.VMEM((8, D_MODEL), F32)]

    return pl.pallas_call(
        body,
        name=f"layer{layer}_backward",
        grid=(nt,),
        in_specs=in_specs,
        out_specs=out_specs,
        out_shape=out_shape,
        scratch_shapes=scratch_shapes,
        compiler_params=pltpu.CompilerParams(dimension_semantics=("arbitrary",), vmem_limit_bytes=56 * MIB),
    )(*operands)


def _scatter_targets():
    x, y, c = lax.axis_index("x"), lax.axis_index("y"), lax.axis_index("c")
    out = []
    for k in range(N_DEV - 1):
        fx, fy, fc = ((k + 1) >> 2) & 1, ((k + 1) >> 1) & 1, (k + 1) & 1
        tx, ty, tc = x ^ fx, y ^ fy, c ^ fc
        out.append((4 * tx + 2 * ty + tc, (tx, ty, tc)))
    return 4 * x + 2 * y + c, out


def _scatter_start(layer, pack):
    n = N_DEV - 1

    def body(pack_ref, land_ref, *rest):
        sems = rest[:2 * n + 1]
        me, targets = _scatter_targets()
        pltpu.make_async_copy(pack_ref.at[me], land_ref.at[me], sems[2 * n]).start()
        for k, (block, device) in enumerate(targets):
            pltpu.make_async_remote_copy(
                src_ref=pack_ref.at[block], dst_ref=land_ref.at[me], send_sem=sems[k], recv_sem=sems[n + k],
                device_id=device, device_id_type=MESH,
            ).start()

    hbm = pl.BlockSpec(memory_space=pltpu.HBM)
    sem = pl.BlockSpec(memory_space=pltpu.SEMAPHORE)
    outs = pl.pallas_call(
        body,
        name=f"layer{layer}_scatter_start",
        out_shape=(*[pltpu.SemaphoreType.DMA(())] * (2 * n + 1), pltpu.HBM(pack.shape, pack.dtype)),
        in_specs=(hbm, hbm),
        out_specs=(*[sem] * (2 * n + 1), hbm),
        input_output_aliases={1: 2 * n + 1},
        compiler_params=pltpu.CompilerParams(has_side_effects=pltpu.SideEffectType.DATAFLOW_SIDE_EFFECTING),
    )(pack, pltpu.with_memory_space_constraint(lax.empty(pack.shape, pack.dtype), pltpu.HBM))
    return outs[:2 * n + 1], pack, outs[2 * n + 1]


def _scatter_wait(layer, sems, pack_thru, land_thru, after):
    n = N_DEV - 1

    def body(pack_ref, land_ref, *rest):
        sem_refs = rest[:2 * n + 1]
        me, targets = _scatter_targets()
        pltpu.make_async_copy(pack_ref.at[me], land_ref.at[me], sem_refs[2 * n]).wait()
        for k, (block, device) in enumerate(targets):
            copy = pltpu.make_async_remote_copy(
                src_ref=pack_ref.at[block], dst_ref=land_ref.at[me], send_sem=sem_refs[k], recv_sem=sem_refs[n + k],
                device_id=device, device_id_type=MESH,
            )
            copy.wait_send()
            copy.wait_recv()

    hbm = pl.BlockSpec(memory_space=pltpu.HBM)
    sem = pl.BlockSpec(memory_space=pltpu.SEMAPHORE)
    return pl.pallas_call(
        body,
        name=f"layer{layer}_scatter_wait",
        out_shape=pltpu.HBM(pack_thru.shape, pack_thru.dtype),
        in_specs=(hbm, hbm, *[sem] * (2 * n + 1), pl.BlockSpec(memory_space=pl.ANY)),
        out_specs=hbm,
        input_output_aliases={1: 0},
        compiler_params=pltpu.CompilerParams(has_side_effects=pltpu.SideEffectType.DATAFLOW_SIDE_EFFECTING),
    )(pack_thru, land_thru, *sems, after)


def _sum_pieces(layer, pieces):
    rows, n = pieces.shape[1], pieces.shape[2]
    blocks = 2
    rb = rows // blocks

    def body(p_ref, out_ref):
        total = p_ref[0].astype(F32)
        for j in range(1, N_DEV):
            total = total + p_ref[j].astype(F32)
        out_ref[...] = total

    return pl.pallas_call(
        body,
        name=f"layer{layer}_grad_sum",
        grid=(blocks,),
        out_shape=pltpu.HBM((rows, n), F32),
        in_specs=[pl.BlockSpec((N_DEV, rb, n), lambda i: (0, i, 0))],
        out_specs=pl.BlockSpec((rb, n), lambda i: (i, 0)),
        compiler_params=pltpu.CompilerParams(dimension_semantics=("arbitrary",), vmem_limit_bytes=32 * MIB),
    )(pieces)


def _weight_grads(layer, dproj, hn, cat, dx1, r, dgpre, dpp, p_all, scatter_pack=None):
    t = hn.shape[0]
    tk = _tile(t, 512)
    nt = t // tk
    in_blocks = PROJ_WIDTH // 512
    scatters = scatter_pack is not None

    def body(*refs):
        (dproj_ref, hn_ref, cat_ref, dx1_ref, r_ref, dgpre_ref, dpp_ref, p_ref) = refs[:8]
        refs = refs[8:]
        if scatters:
            prior_ref, refs = refs[0], refs[1:]
        pack_ref, refs = refs[0], refs[1:]
        if scatters:
            pieces_ref, refs = refs[0], refs[1:]
        (acc_in, acc_out, acc_gate, acc_proj, stage, sems) = refs[:6]
        i = pl.program_id(0)
        if scatters:
            scatter = _DirectScatter(prior_ref, pieces_ref, *refs[6:9])

            @pl.when(i == 0)
            def _():
                scatter.start()

        @pl.when(i == 0)
        def _():
            acc_in[...] = jnp.zeros_like(acc_in)
            acc_out[...] = jnp.zeros_like(acc_out)
            acc_gate[...] = jnp.zeros_like(acc_gate)
            acc_proj[...] = jnp.zeros_like(acc_proj)

        hnv = hn_ref[...]
        for b in range(in_blocks):
            acc_in[pl.ds(b * 512, 512), :] += _dot_tn(dproj_ref[:, b * 512:(b + 1) * 512], hnv)
        dx1v = dx1_ref[...]
        dgv = dgpre_ref[...]
        for b in range(D_MODEL // 512):
            acc_out[pl.ds(b * 512, 512), :] += _dot_tn(cat_ref[:, b * 512:(b + 1) * 512], dx1v)
            acc_gate[pl.ds(b * 512, 512), :] += _dot_tn(r_ref[:, b * 512:(b + 1) * 512], dgv)
        pv = p_ref[...].astype(BF16)
        for b in range(D_MODEL // 512):
            acc_proj[pl.ds(b * 512, 512), :] += _dot_tn(dpp_ref[:, b * 512:(b + 1) * 512], pv)

        @pl.when(i == nt - 1)
        def _():
            def out_copy(s):
                return pltpu.make_async_copy(stage.at[s % 2], pack_ref.at[s], sems.at[s % 2])

            for s in range(N_DEV):
                if s >= 2:
                    out_copy(s - 2).wait()
                buf = stage.at[s % 2]
                buf[pl.ds(OFF_IN, ROWS_IN), :] = acc_in[pl.ds(s * ROWS_IN, ROWS_IN), :].astype(BF16)
                buf[pl.ds(OFF_OUT, ROWS_OUT), :] = acc_out[pl.ds(s * ROWS_OUT, ROWS_OUT), :].astype(BF16)
                buf[pl.ds(OFF_GATE, ROWS_GATE), :] = acc_gate[pl.ds(s * ROWS_GATE, ROWS_GATE), :].astype(BF16)
                for j in range(D_MODEL // PLE_DIM):
                    buf[pl.ds(OFF_PROJ, ROWS_PROJ), pl.ds(j * PLE_DIM, PLE_DIM)] = acc_proj[
                        pl.ds(s * ROWS_OUT + j * ROWS_PROJ, ROWS_PROJ), :
                    ].astype(BF16)
                out_copy(s).start()
            out_copy(N_DEV - 2).wait()
            out_copy(N_DEV - 1).wait()
            if scatters:
                scatter.finish()

    def tok(width):
        return pl.BlockSpec((tk, width), lambda i: (i, 0))

    hbm = pl.BlockSpec(memory_space=pl.ANY)
    pack_shape = jax.ShapeDtypeStruct((N_DEV, ROWS_GRAD, D_MODEL), BF16)
    operands = [dproj, hn, cat, dx1, r, dgpre, dpp, p_all]
    in_specs = [tok(PROJ_WIDTH), tok(D_MODEL), tok(D_MODEL), tok(D_MODEL), tok(D_MODEL), tok(D_MODEL), tok(D_MODEL),
                pl.BlockSpec((None, None, tk, PLE_DIM), lambda i: (layer, 0, i, 0))]
    out_specs, out_shape = [hbm], [pack_shape]
    scratch_shapes = [
        pltpu.VMEM((PROJ_WIDTH, D_MODEL), F32),
        pltpu.VMEM((D_MODEL, D_MODEL), F32),
        pltpu.VMEM((D_MODEL, D_MODEL), F32),
        pltpu.VMEM((D_MODEL, PLE_DIM), F32),
        pltpu.VMEM((2, ROWS_GRAD, D_MODEL), BF16),
        pltpu.SemaphoreType.DMA((2,)),
    ]
    if scatters:
        operands.append(scatter_pack)
        in_specs.append(hbm)
        out_specs.append(hbm)
        out_shape.append(pack_shape)
        scratch_shapes += list(SCATTER_SEMS)

    return pl.pallas_call(
        body,
        name=f"layer{layer}_weight_grads",
        grid=(nt,),
        in_specs=in_specs,
        out_specs=out_specs,
        out_shape=out_shape,
        scratch_shapes=scratch_shapes,
        compiler_params=pltpu.CompilerParams(dimension_semantics=("arbitrary",), vmem_limit_bytes=58 * MIB),
    )(*operands)


def _reduce_scatter_all_reduce(layer, pack, smalls, head, dws):
    rows, n = pack.shape[1], pack.shape[2]
    assert DEPTH * WIDTH_A == D_MODEL and n == D_MODEL

    def body(g_ref, *refs):
        small_refs, refs = refs[:DEPTH], refs[DEPTH:]
        head_ref, refs = refs[0], refs[1:]
        dws_refs, refs = refs[:DEPTH], refs[DEPTH:]
        (out_ref, total_ref, r1, a_s, r2, via, sp, sr1, sq, own, send1, recv1, send2, recv2, ssend, srecv, own_sems) = refs
        x, y, c = lax.axis_index("x"), lax.axis_index("y"), lax.axis_index("c")
        mine = [pltpu.make_async_copy(g_ref.at[2 * j + c], own.at[j], own_sems.at[j]) for j in range(4)]
        for cp in mine:
            cp.start()
        sibling = (x, y, 1 - c)
        chip = 2 * x + y
        flips = [(1, 0), (0, 1), (1, 1)]

        for l in range(DEPTH):
            sp[l * SMALL_ROWS:(l + 1) * SMALL_ROWS, :] = small_refs[l][...]
            sp[TOTAL_WS:TOTAL_ROWS, l * WIDTH_A:(l + 1) * WIDTH_A] = dws_refs[l][...]
        sp[TOTAL_HEAD:TOTAL_WS, :] = head_ref[...]

        small_pair = pltpu.make_async_remote_copy(
            src_ref=sp, dst_ref=sr1, send_sem=ssend.at[0], recv_sem=srecv.at[0], device_id=sibling, device_id_type=MESH
        )

        def to_sibling(j):
            return pltpu.make_async_remote_copy(
                src_ref=g_ref.at[2 * j + 1 - c], dst_ref=r1.at[j], send_sem=send1.at[j], recv_sem=recv1.at[j],
                device_id=sibling, device_id_type=MESH,
            )

        first = [to_sibling(j) for j in range(4)]
        small_pair.start()
        for cp in first:
            cp.start()

        small_pair.wait_recv()
        sq[chip] = sp[...] + sr1[...]
        small_chips = [
            pltpu.make_async_remote_copy(
                src_ref=sq.at[chip], dst_ref=sq.at[chip], send_sem=ssend.at[1 + k], recv_sem=srecv.at[1 + k],
                device_id=(x ^ fx, y ^ fy, c), device_id_type=MESH,
            )
            for k, (fx, fy) in enumerate(flips)
        ]
        for cp in small_chips:
            cp.start()

        for j in range(4):
            first[j].wait_recv()
            mine[j].wait()

            @pl.when(chip != j)
            def _():
                a_s[j] = (own[j].astype(F32) + r1[j].astype(F32)).astype(BF16)

        half = rows // 2
        lo, hi = pl.ds(0, half), pl.ds(half, rows - half)
        x_nbr, y_nbr = (1 - x, y, c), (x, 1 - y, c)
        chip_x, chip_y, chip_d = 2 * (1 - x) + y, 2 * x + (1 - y), 2 * (1 - x) + (1 - y)

        def ici(k, src, dst, to):
            return pltpu.make_async_remote_copy(
                src_ref=src, dst_ref=dst, send_sem=send2.at[k], recv_sem=recv2.at[k], device_id=to, device_id_type=MESH)

        second = [
            ici(0, a_s.at[chip_d, lo, :], via.at[0], x_nbr),
            ici(1, a_s.at[chip_d, hi, :], via.at[1], y_nbr),
            ici(2, a_s.at[chip_x, lo, :], r2.at[0, lo, :], x_nbr),
            ici(3, a_s.at[chip_y, hi, :], r2.at[1, hi, :], y_nbr),
            ici(4, a_s.at[chip_x, hi, :], r2.at[0, hi, :], x_nbr),
            ici(5, a_s.at[chip_y, lo, :], r2.at[1, lo, :], y_nbr),
        ]
        for cp in second[:4]:
            cp.start()

        out_ref[...] = own[chip].astype(F32) + r1[chip].astype(F32)
        for cp in small_chips:
            cp.wait_recv()
        total_ref[...] = ((sq[0] + sq[1]) + sq[2]) + sq[3]

        second[0].wait_recv()
        a_s[chip_y, lo, :] = (a_s[chip_y, lo, :].astype(F32) + via[0].astype(F32)).astype(BF16)
        second[5].start()
        second[1].wait_recv()
        a_s[chip_x, hi, :] = (a_s[chip_x, hi, :].astype(F32) + via[1].astype(F32)).astype(BF16)
        second[4].start()

        second[2].wait_recv()
        second[4].wait_recv()
        out_ref[...] += r2[0].astype(F32)
        second[3].wait_recv()
        second[5].wait_recv()
        out_ref[...] += r2[1].astype(F32)
        small_pair.wait_send()
        for cp in first + small_chips + second:
            cp.wait_send()

    vmem = pl.BlockSpec(memory_space=pltpu.VMEM)
    return pl.pallas_call(
        body,
        name=f"layer{layer}_grad_reduce_scatter",
        out_shape=[jax.ShapeDtypeStruct((rows, n), F32), jax.ShapeDtypeStruct((TOTAL_ROWS, D_MODEL), F32)],
        in_specs=[pl.BlockSpec(memory_space=pltpu.HBM)] + [vmem] * (1 + 2 * DEPTH),
        out_specs=[vmem, vmem],
        scratch_shapes=[
            pltpu.VMEM((4, rows, n), BF16),
            pltpu.VMEM((4, rows, n), BF16),
            pltpu.VMEM((2, rows, n), BF16),
            pltpu.VMEM((2, rows // 2, n), BF16),
            pltpu.VMEM((TOTAL_ROWS, D_MODEL), F32),
            pltpu.VMEM((TOTAL_ROWS, D_MODEL), F32),
            pltpu.VMEM((4, TOTAL_ROWS, D_MODEL), F32),
            pltpu.VMEM((4, rows, n), BF16),
            pltpu.SemaphoreType.DMA((4,)),
            pltpu.SemaphoreType.DMA((4,)),
            pltpu.SemaphoreType.DMA((6,)),
            pltpu.SemaphoreType.DMA((6,)),
            pltpu.SemaphoreType.DMA((4,)),
            pltpu.SemaphoreType.DMA((4,)),
            pltpu.SemaphoreType.DMA((4,)),
        ],
        compiler_params=pltpu.CompilerParams(vmem_limit_bytes=48 * MIB),
    )(pltpu.with_memory_space_constraint(pack, pltpu.HBM), *smalls, head, *dws)


def _adam_step(w, g, m, v):
    m = ADAM_B1 * m + (1.0 - ADAM_B1) * g
    v = ADAM_B2 * v + (1.0 - ADAM_B2) * (g * g)
    m_hat = m / (1.0 - ADAM_B1 ** ADAM_STEP)
    v_hat = v / (1.0 - ADAM_B2 ** ADAM_STEP)
    return -ADAM_LR * (m_hat / (jnp.sqrt(v_hat) + ADAM_EPS) + ADAM_WD * w), m, v


def _adamw_rows(name, reduced, row_off, states):
    n = len(states)

    def body(*refs):
        red = refs[:DEPTH]
        ins = refs[DEPTH:DEPTH + 3 * n]
        outs = refs[DEPTH + 3 * n:]
        layer = pl.program_id(0)
        for l in range(DEPTH):
            @pl.when(layer == l)
            def _():
                for k in range(n):
                    w_ref, m_ref, v_ref = ins[3 * k:3 * k + 3]
                    g_ref, d_ref, nm_ref, nv_ref = outs[4 * k:4 * k + 4]
                    g = red[l][row_off[k]:row_off[k] + w_ref.shape[0], :]
                    d, m, v = _adam_step(w_ref[...], g, m_ref[...], v_ref[...])
                    g_ref[...] = g
                    d_ref[...] = d
                    nm_ref[...] = m
                    nv_ref[...] = v

    flat = [a for st in states for a in st]
    state_specs, out_specs, out_shape = [], [], []
    for w, _, _ in states:
        spec = pl.BlockSpec((None,) + w.shape[1:], lambda l: (l, 0, 0))
        state_specs += [spec] * 3
        out_specs += [spec] * 4
        out_shape += [jax.ShapeDtypeStruct(w.shape, F32)] * 4
    red_specs = [pl.BlockSpec(a.shape, lambda l: (0, 0)) for a in reduced]
    operands = [pltpu.with_memory_space_constraint(a, pltpu.HBM) for a in (*reduced, *flat)]
    outs = pl.pallas_call(
        body,
        name=name,
        grid=(DEPTH,),
        out_shape=[pltpu.HBM(a.shape, a.dtype) for a in out_shape],
        in_specs=red_specs + state_specs,
        out_specs=out_specs,
        compiler_params=pltpu.CompilerParams(dimension_semantics=("arbitrary",), vmem_limit_bytes=48 * MIB),
    )(*operands)
    return [tuple(outs[4 * k:4 * k + 4]) for k in range(n)]


def _adamw_small(total, g_conv, g_proj, st):
    names = ["norm_g", "ple_norm_g", "ln_v_g", "ln_v_b", "b_s", "w_s", "final_g", "conv_w", "w_ple_proj"]
    cut = names[:7]

    def body(total_ref, gconv_ref, gproj_ref, *refs):
        ins = {nm: refs[3 * k:3 * k + 3] for k, nm in enumerate(names)}
        outs, pos = {}, 3 * len(names)
        for nm in names:
            cnt = 4 if nm in cut else 3
            outs[nm] = refs[pos:pos + cnt]
            pos += cnt

        def update(nm, idx, g):
            w_ref, m_ref, v_ref = ins[nm]
            d, m, v = _adam_step(w_ref[idx], g, m_ref[idx], v_ref[idx])
            o = outs[nm]
            if nm in cut:
                o[0][idx] = g
                o = o[1:]
            o[0][idx] = d
            o[1][idx] = m
            o[2][idx] = v

        tril = (lax.broadcasted_iota(jnp.int32, (CHUNK, CHUNK), 0) >= lax.broadcasted_iota(jnp.int32, (CHUNK, CHUNK), 1))
        for l in range(DEPTH):
            base = l * SMALL_ROWS
            row = (slice(l, l + 1), slice(None))
            update("norm_g", row, total_ref[base + SMALL_NORM:base + SMALL_NORM + 1, :])
            update("ple_norm_g", row, total_ref[base + SMALL_PLE:base + SMALL_PLE + 1, :])
            update("ln_v_g", row, total_ref[base + SMALL_LN:base + SMALL_LN + 1, 0:WIDTH_A])
            update("ln_v_b", row, total_ref[base + SMALL_LN:base + SMALL_LN + 1, WIDTH_A:2 * WIDTH_A])
            for h in range(HEADS_A):
                update("b_s", (l, slice(h, h + 1), slice(None)),
                       total_ref[base + SMALL_BS:base + SMALL_BS + 1, h * HEAD_DIM:(h + 1) * HEAD_DIM])
                lanes = slice(l * WIDTH_A + h * CHUNK, l * WIDTH_A + (h + 1) * CHUNK)
                update("w_s", (l, h), jnp.where(tril, total_ref[TOTAL_WS:TOTAL_ROWS, lanes], 0.0))
        update("final_g", (slice(None), slice(None)), total_ref[TOTAL_HEAD + HEAD_FINAL:TOTAL_HEAD + HEAD_FINAL + 1, :])
        update("conv_w", (slice(None),) * 3, gconv_ref[...])
        update("w_ple_proj", (slice(None),) * 3, gproj_ref[...])

    flat = [a for nm in names for a in st[nm]]
    out_shape = []
    for nm in names:
        out_shape += [jax.ShapeDtypeStruct(st[nm][0].shape, F32)] * (4 if nm in cut else 3)
    def whole(a):
        return pl.BlockSpec(a.shape, lambda i: (0,) * len(a.shape))

    operands = [pltpu.with_memory_space_constraint(a, pltpu.HBM) for a in (total, g_conv, g_proj, *flat)]
    outs = pl.pallas_call(
        body,
        name="adamw_small",
        grid=(1,),
        out_shape=[pltpu.HBM(a.shape, a.dtype) for a in out_shape],
        in_specs=[whole(a) for a in operands],
        out_specs=[whole(a) for a in out_shape],
        compiler_params=pltpu.CompilerParams(dimension_semantics=("arbitrary",), vmem_limit_bytes=32 * MIB),
    )(*operands)
    res, pos = {}, 0
    for nm in names:
        cnt = 4 if nm in cut else 3
        got = tuple(outs[pos:pos + cnt])
        res[nm] = got if nm in cut else ((g_conv if nm == "conv_w" else g_proj),) + got
        pos += cnt
    return res


def _split3_bf16(a):
    b1 = a.astype(BF16)
    r1 = a - b1.astype(F32)
    b2 = r1.astype(BF16)
    b3 = (r1 - b2.astype(F32)).astype(BF16)
    return b1, b2, b3


def _pack_weight_shard(w_in_l, w_out_l, w_gate_l, w_proj_l, conv_w_l):
    w_in_t = jnp.transpose(w_in_l).astype(BF16)
    proj_t = jnp.transpose(w_proj_l).astype(BF16)
    proj_rows = proj_t.reshape(D_MODEL // PLE_DIM, ROWS_PROJ, PLE_DIM).transpose(1, 0, 2).reshape(ROWS_PROJ, D_MODEL)
    conv_parts = jnp.concatenate([b.reshape(-1) for b in _split3_bf16(conv_w_l)])
    conv_rows = jnp.concatenate([conv_parts, jnp.zeros((ROWS_CONV * D_MODEL - conv_parts.shape[0],), BF16)])
    return jnp.concatenate(
        [w_in_t, w_out_l.astype(BF16), w_gate_l.astype(BF16), proj_rows, conv_rows.reshape(ROWS_CONV, D_MODEL)], axis=0
    )


def _unpack_conv(wg):
    per_dev = wg.reshape(N_DEV, ROWS_LAYER, D_MODEL)
    n_conv = (WIDTH_B // N_DEV) * 3
    conv_parts = per_dev[:, OFF_CONV].astype(F32)[:, :3 * n_conv].reshape(N_DEV, 3, n_conv)
    conv = (conv_parts[:, 0] + conv_parts[:, 1]) + conv_parts[:, 2]
    conv_k = jnp.transpose(conv.reshape(WIDTH_B, 3))
    conv_k = jnp.concatenate([conv_k, jnp.zeros((5, WIDTH_B), F32)], axis=0)
    return conv_k


def _unpack_grad_proj(red):
    proj_rows = red[OFF_PROJ:OFF_PROJ + ROWS_PROJ]
    proj_t = proj_rows.reshape(ROWS_PROJ, D_MODEL // PLE_DIM, PLE_DIM).transpose(1, 0, 2).reshape(ROWS_OUT, PLE_DIM)
    return jnp.transpose(proj_t)


def kernel(x, p, norm_g, w_in, ln_v_g, ln_v_b, w_s, b_s, conv_w, w_out, ple_norm_g, w_ple_gate, w_ple_proj, final_g, loss_target, m_norm_g, m_w_in, m_ln_v_g, m_ln_v_b, m_w_s, m_b_s, m_conv_w, m_w_out, m_ple_norm_g, m_w_ple_gate, m_w_ple_proj, m_final_g, v_norm_g, v_w_in, v_ln_v_g, v_ln_v_b, v_w_s, v_b_s, v_conv_w, v_w_out, v_ple_norm_g, v_w_ple_gate, v_w_ple_proj, v_final_g):
    me = 4 * lax.axis_index("x") + 2 * lax.axis_index("y") + lax.axis_index("c")
    xs = x[0]
    target = loss_target[0]

    shards = [_pack_weight_shard(w_in[l], w_out[l], w_ple_gate[l], w_ple_proj[l], conv_w[l]) for l in range(DEPTH)]
    tril = jnp.tril(jnp.ones((CHUNK, CHUNK), F32))

    def consts(l, wg_l):
        conv_k = _unpack_conv(wg_l)
        w_mix = w_s[l] * tril[None]
        small = dict(
            conv_k=conv_k,
            norm_g=norm_g[l].reshape(1, D_MODEL), ln_g=ln_v_g[l].reshape(1, WIDTH_A), ln_b=ln_v_b[l].reshape(1, WIDTH_A),
            w_mix=w_mix.astype(BF16), w_mix_t=jnp.swapaxes(w_mix, 1, 2).astype(BF16),
            b_mix=jnp.broadcast_to(b_s[l][:, :, None], (HEADS_A, CHUNK, HEAD_DIM)),
            ple_g=ple_norm_g[l].reshape(1, D_MODEL),
        )
        return dict({k: pltpu.with_memory_space_constraint(a, pltpu.HBM) for k, a in small.items()}, wg=wg_l)

    layer_consts = [consts(0, _all_gather_rows(shards[0]))]
    saved = []
    h = xs
    for l in range(DEPTH):
        k = layer_consts[l]
        outs = _forward_layer(
            l, h, p, k["wg"], k["conv_k"], k["norm_g"], k["ln_g"], k["ln_b"], k["w_mix"], k["b_mix"],
            k["ple_g"], next_shard=shards[l + 1] if l + 1 < DEPTH else None)
        proj, hn, cat, r, gpre, x1, x2 = outs[:7]
        if l + 1 < DEPTH:
            layer_consts.append(consts(l + 1, outs[7]))
        saved.append(dict(x_in=h, proj=proj, hn=hn, cat=cat, r=r, gpre=gpre, x1=x1))
        h = x2

    smalls, dws = [None] * DEPTH, [None] * DEPTH
    reduced = [None] * DEPTH
    pending = None
    dx = h
    for l in reversed(range(DEPTH)):
        k, s = layer_consts[l], saved[l]
        outs = _backward_layer(
            l, dx, s["x_in"], s["x1"], s["proj"], s["gpre"], p, k["wg"], k["conv_k"],
            k["norm_g"], k["ln_g"], k["ln_b"], k["w_mix"], k["w_mix_t"], k["b_mix"], k["ple_g"],
            loss_head=(target, final_g.reshape(1, D_MODEL)) if l == DEPTH - 1 else None)
        dx, dproj, dx1, dgpre, dpp, smalls[l], dws[l] = outs[:7]
        if l == DEPTH - 1:
            head = outs[7]
        (pack,) = _weight_grads(l, dproj, s["hn"], s["cat"], dx1, s["r"], dgpre, dpp, p)
        if pending is not None:
            sems, pack_thru, land_thru = pending
            pieces = _scatter_wait(l + 1, sems, pack_thru, land_thru, pack)
            reduced[l + 1] = _sum_pieces(l + 1, pltpu.with_memory_space_constraint(pieces, pltpu.HBM))
        pending = _scatter_start(l, pack) if l > 0 else pack
    reduced[0], total = _reduce_scatter_all_reduce(0, pending, smalls, head, dws)
    grad_x = dx[None]
    loss = total[TOTAL_HEAD + HEAD_LOSS, 0]

    n_ch = WIDTH_B // N_DEV
    g_conv = jnp.stack([total[l * SMALL_ROWS + SMALL_CONV:l * SMALL_ROWS + SMALL_CONV + 3, 0:WIDTH_B] for l in range(DEPTH)], axis=1)
    g_conv = lax.dynamic_slice_in_dim(g_conv, me * n_ch, n_ch, axis=2)
    g_proj = jnp.stack([_unpack_grad_proj(reduced[l]) for l in range(DEPTH)])

    def t_in(a):
        return jnp.swapaxes(a, 1, 2)

    def t_conv(a):
        return jnp.transpose(a, (2, 0, 1))

    (r_in,) = _adamw_rows("adamw_w_in", reduced, [OFF_IN], [(t_in(w_in), t_in(m_w_in), t_in(v_w_in))])
    r_out, r_gate = _adamw_rows(
        "adamw_w_out_gate", reduced, [OFF_OUT, OFF_GATE],
        [(w_out, m_w_out, v_w_out), (w_ple_gate, m_w_ple_gate, v_w_ple_gate)])
    small = _adamw_small(total, g_conv, g_proj, dict(
        norm_g=(norm_g, m_norm_g, v_norm_g), ple_norm_g=(ple_norm_g, m_ple_norm_g, v_ple_norm_g),
        ln_v_g=(ln_v_g, m_ln_v_g, v_ln_v_g), ln_v_b=(ln_v_b, m_ln_v_b, v_ln_v_b),
        b_s=(b_s, m_b_s, v_b_s), w_s=(w_s, m_w_s, v_w_s),
        final_g=tuple(a.reshape(1, D_MODEL) for a in (final_g, m_final_g, v_final_g)),
        conv_w=(t_conv(conv_w), t_conv(m_conv_w), t_conv(v_conv_w)),
        w_ple_proj=(w_ple_proj, m_w_ple_proj, v_w_ple_proj),
    ))
    res = dict(small, w_in=tuple(t_in(a) for a in r_in), w_out=r_out, w_ple_gate=r_gate)
    res["final_g"] = tuple(a.reshape(D_MODEL) for a in res["final_g"])
    res["conv_w"] = tuple(jnp.transpose(a, (1, 2, 0)) for a in res["conv_w"])
    order = ["norm_g", "w_in", "ln_v_g", "ln_v_b", "w_s", "b_s", "conv_w", "w_out", "ple_norm_g", "w_ple_gate", "w_ple_proj", "final_g"]
    return (loss, grad_x, *[res[n][0] for n in order], *[res[n][1] for n in order],
            *[res[n][2] for n in order], *[res[n][3] for n in order])
```
